```python
import jax, jax.numpy as jnp
from jax import lax
import numpy as np

D_MODEL = 1024
BATCH = 8
SEQ = 8192
DEPTH = 2

W_A = D_MODEL
K_A = 3
W_B = D_MODEL
K_B = 31
W_C = D_MODEL
POOL_WINDOWS = (2, 4, 8, 16)
N_POOL_GROUPS = len(POOL_WINDOWS)
GC = W_C // N_POOL_GROUPS
N_BRANCH = 3
D_FF = 4 * D_MODEL
EPS = 1e-6
COLS_A = 3 * W_A
COLS_B = 2 * W_B
COLS_C = W_C
COLS_G = N_BRANCH * D_MODEL
P_IN = COLS_A + COLS_B + COLS_C + COLS_G
SPLITS = (W_A, 2 * W_A, 3 * W_A, 3 * W_A + W_B, COLS_A + COLS_B, COLS_A + COLS_B + COLS_C)

kernel_name = "hybrid_conv_pool_gated_block"


def rmsnorm(x, g):
    x32 = x.astype(jnp.float32)
    y = x32 * lax.rsqrt(jnp.mean(x32 * x32, axis=-1, keepdims=True) + EPS)
    return (y * g.astype(jnp.float32)).astype(x.dtype)


def layernorm(x, g, b):
    x32 = x.astype(jnp.float32)
    mu = jnp.mean(x32, axis=-1, keepdims=True)
    xc = x32 - mu
    var = jnp.mean(xc * xc, axis=-1, keepdims=True)
    y = xc * lax.rsqrt(var + EPS) * g.astype(jnp.float32) + b.astype(jnp.float32)
    return y.astype(x.dtype)


def causal_depthwise_conv(u, w):
    k, c = w.shape
    return lax.conv_general_dilated(
        u, w[:, None, :].astype(u.dtype), window_strides=(1,), padding=[(k - 1, 0)],
        dimension_numbers=("NWC", "WIO", "NWC"), feature_group_count=c)


def short_conv_mixer(b_gate, c_gate, xh, conv_w, w_out):
    z = causal_depthwise_conv(c_gate * xh, conv_w)
    return (b_gate * z) @ w_out


def conformer_conv_mixer(val, gate, conv_w, conv_b, ln_g, ln_b, w_out, b_out):
    u = val * jax.nn.sigmoid(gate)
    u = causal_depthwise_conv(u, conv_w) + conv_b
    u = jax.nn.silu(layernorm(u, ln_g, ln_b))
    return u @ w_out + b_out


def pool_mixer(u, w_pool, scale):
    seq = u.shape[1]
    u32 = u.astype(jnp.float32)
    cs = jnp.cumsum(u32, axis=1)
    t = jnp.arange(seq)
    parts = []
    for g, w in enumerate(POOL_WINDOWS):
        sl = slice(g * GC, (g + 1) * GC)
        c = cs[..., sl]
        prev = jnp.pad(c, ((0, 0), (w, 0), (0, 0)))[:, :seq]
        cnt = jnp.minimum(t + 1, w).astype(jnp.float32)[None, :, None]
        parts.append((c - prev) / cnt - u32[..., sl])
    p = jnp.stack(parts, axis=2).astype(u.dtype)
    y = jnp.einsum('bsgc,gcd->bsgd', p, w_pool)
    return y.reshape(u.shape[0], seq, W_C) * scale


def _fwd_setup_inputs(seed: int = 0) -> dict:
    key = jax.random.key(seed)
    ks = jax.random.split(key, 20)
    L, D = DEPTH, D_MODEL
    f32 = jnp.float32

    def nrm(k, shape, fan_in):
        return jax.random.normal(k, shape, f32) * (fan_in ** -0.5)

    def gain(k, shape):
        return 1.0 + 0.02 * jax.random.normal(k, shape, f32)

    return {
        "x": jax.random.normal(ks[0], (BATCH, SEQ, D), f32),
        "g_mix": gain(ks[1], (L, D)),
        "w_in": nrm(ks[2], (L, D, P_IN), D),
        "b_in": 0.02 * jax.random.normal(ks[3], (L, P_IN), f32),
        "conv_a": nrm(ks[4], (L, K_A, W_A), K_A),
        "w_out_a": nrm(ks[5], (L, W_A, D), W_A),
        "conv_b": nrm(ks[6], (L, K_B, W_B), K_B),
        "conv_b_bias": 0.02 * jax.random.normal(ks[7], (L, W_B), f32),
        "ln_b_g": gain(ks[8], (L, W_B)),
        "ln_b_b": 0.02 * jax.random.normal(ks[9], (L, W_B), f32),
        "w_out_b": nrm(ks[10], (L, W_B, D), W_B),
        "b_out_b": 0.02 * jax.random.normal(ks[11], (L, D), f32),
        "w_pool": nrm(ks[12], (L, N_POOL_GROUPS, GC, GC), GC),
        "pool_scale": gain(ks[13], (L, W_C)),
        "w_o": nrm(ks[14], (L, D, D), D),
        "g_mlp": gain(ks[15], (L, D)),
        "w_mlp1": nrm(ks[16], (L, D, D_FF), D),
        "w_mlp2": nrm(ks[17], (L, D_FF, D), D_FF),
        "g_final": gain(ks[18], (D,)),
    }


def _fwd_reference(x, g_mix, w_in, b_in, conv_a, w_out_a, conv_b, conv_b_bias, ln_b_g, ln_b_b,
              w_out_b, b_out_b, w_pool, pool_scale, w_o, g_mlp, w_mlp1, w_mlp2, g_final):
    bsz, seq, d = x.shape
    for l in range(DEPTH):
        h = rmsnorm(x, g_mix[l])
        proj = h @ w_in[l] + b_in[l]
        a_b, a_c, a_x, b_val, b_gate, c_in, gates = jnp.split(proj, SPLITS, axis=-1)
        y_a = short_conv_mixer(a_b, a_c, a_x, conv_a[l], w_out_a[l])
        y_b = conformer_conv_mixer(b_val, b_gate, conv_b[l], conv_b_bias[l], ln_b_g[l],
                                   ln_b_b[l], w_out_b[l], b_out_b[l])
        y_c = pool_mixer(c_in, w_pool[l], pool_scale[l])
        g = jax.nn.sigmoid(gates).reshape(bsz, seq, N_BRANCH, d)
        merged = g[:, :, 0] * y_a + g[:, :, 1] * y_b + g[:, :, 2] * y_c
        x = x + merged @ w_o[l]
        h = rmsnorm(x, g_mlp[l])
        x = x + jnp.square(jax.nn.relu(h @ w_mlp1[l])) @ w_mlp2[l]
    return rmsnorm(x, g_final)


import jax as _jax
import jax.numpy as _jnp

TWIN_FORMAT = 'train_step'
FWD_PARAMS = ['x', 'g_mix', 'w_in', 'b_in', 'conv_a', 'w_out_a', 'conv_b', 'conv_b_bias', 'ln_b_g', 'ln_b_b', 'w_out_b', 'b_out_b', 'w_pool', 'pool_scale', 'w_o', 'g_mlp', 'w_mlp1', 'w_mlp2', 'g_final']
TWIN_WEIGHTS = ['g_mix', 'w_in', 'b_in', 'conv_a', 'w_out_a', 'conv_b', 'conv_b_bias', 'ln_b_g', 'ln_b_b', 'w_out_b', 'b_out_b', 'w_pool', 'pool_scale', 'w_o', 'g_mlp', 'w_mlp1', 'w_mlp2', 'g_final']
TWIN_DIFF_INPUT = 'x'
TWIN_INPUTS = ['x', 'g_mix', 'w_in', 'b_in', 'conv_a', 'w_out_a', 'conv_b', 'conv_b_bias', 'ln_b_g', 'ln_b_b', 'w_out_b', 'b_out_b', 'w_pool', 'pool_scale', 'w_o', 'g_mlp', 'w_mlp1', 'w_mlp2', 'g_final', 'loss_target', 'm_g_mix', 'm_w_in', 'm_b_in', 'm_conv_a', 'm_w_out_a', 'm_conv_b', 'm_conv_b_bias', 'm_ln_b_g', 'm_ln_b_b', 'm_w_out_b', 'm_b_out_b', 'm_w_pool', 'm_pool_scale', 'm_w_o', 'm_g_mlp', 'm_w_mlp1', 'm_w_mlp2', 'm_g_final', 'v_g_mix', 'v_w_in', 'v_b_in', 'v_conv_a', 'v_w_out_a', 'v_conv_b', 'v_conv_b_bias', 'v_ln_b_g', 'v_ln_b_b', 'v_w_out_b', 'v_b_out_b', 'v_w_pool', 'v_pool_scale', 'v_w_o', 'v_g_mlp', 'v_w_mlp1', 'v_w_mlp2', 'v_g_final']
TWIN_OUTPUTS = ['loss', 'grad_x', 'grad_g_mix', 'grad_w_in', 'grad_b_in', 'grad_conv_a', 'grad_w_out_a', 'grad_conv_b', 'grad_conv_b_bias', 'grad_ln_b_g', 'grad_ln_b_b', 'grad_w_out_b', 'grad_b_out_b', 'grad_w_pool', 'grad_pool_scale', 'grad_w_o', 'grad_g_mlp', 'grad_w_mlp1', 'grad_w_mlp2', 'grad_g_final', 'delta_g_mix', 'delta_w_in', 'delta_b_in', 'delta_conv_a', 'delta_w_out_a', 'delta_conv_b', 'delta_conv_b_bias', 'delta_ln_b_g', 'delta_ln_b_b', 'delta_w_out_b', 'delta_b_out_b', 'delta_w_pool', 'delta_pool_scale', 'delta_w_o', 'delta_g_mlp', 'delta_w_mlp1', 'delta_w_mlp2', 'delta_g_final', 'new_m_g_mix', 'new_m_w_in', 'new_m_b_in', 'new_m_conv_a', 'new_m_w_out_a', 'new_m_conv_b', 'new_m_conv_b_bias', 'new_m_ln_b_g', 'new_m_ln_b_b', 'new_m_w_out_b', 'new_m_b_out_b', 'new_m_w_pool', 'new_m_pool_scale', 'new_m_w_o', 'new_m_g_mlp', 'new_m_w_mlp1', 'new_m_w_mlp2', 'new_m_g_final', 'new_v_g_mix', 'new_v_w_in', 'new_v_b_in', 'new_v_conv_a', 'new_v_w_out_a', 'new_v_conv_b', 'new_v_conv_b_bias', 'new_v_ln_b_g', 'new_v_ln_b_b', 'new_v_w_out_b', 'new_v_b_out_b', 'new_v_w_pool', 'new_v_pool_scale', 'new_v_w_o', 'new_v_g_mlp', 'new_v_w_mlp1', 'new_v_w_mlp2', 'new_v_g_final']
TWIN_LEAF_KINDS = {'loss': 'loss', 'grad_x': 'grad_x', 'grad_g_mix': 'grad_w', 'grad_w_in': 'grad_w', 'grad_b_in': 'grad_w', 'grad_conv_a': 'grad_w', 'grad_w_out_a': 'grad_w', 'grad_conv_b': 'grad_w', 'grad_conv_b_bias': 'grad_w', 'grad_ln_b_g': 'grad_w', 'grad_ln_b_b': 'grad_w', 'grad_w_out_b': 'grad_w', 'grad_b_out_b': 'grad_w', 'grad_w_pool': 'grad_w', 'grad_pool_scale': 'grad_w', 'grad_w_o': 'grad_w', 'grad_g_mlp': 'grad_w', 'grad_w_mlp1': 'grad_w', 'grad_w_mlp2': 'grad_w', 'grad_g_final': 'grad_w', 'delta_g_mix': 'delta_w', 'delta_w_in': 'delta_w', 'delta_b_in': 'delta_w', 'delta_conv_a': 'delta_w', 'delta_w_out_a': 'delta_w', 'delta_conv_b': 'delta_w', 'delta_conv_b_bias': 'delta_w', 'delta_ln_b_g': 'delta_w', 'delta_ln_b_b': 'delta_w', 'delta_w_out_b': 'delta_w', 'delta_b_out_b': 'delta_w', 'delta_w_pool': 'delta_w', 'delta_pool_scale': 'delta_w', 'delta_w_o': 'delta_w', 'delta_g_mlp': 'delta_w', 'delta_w_mlp1': 'delta_w', 'delta_w_mlp2': 'delta_w', 'delta_g_final': 'delta_w', 'new_m_g_mix': 'new_m', 'new_m_w_in': 'new_m', 'new_m_b_in': 'new_m', 'new_m_conv_a': 'new_m', 'new_m_w_out_a': 'new_m', 'new_m_conv_b': 'new_m', 'new_m_conv_b_bias': 'new_m', 'new_m_ln_b_g': 'new_m', 'new_m_ln_b_b': 'new_m', 'new_m_w_out_b': 'new_m', 'new_m_b_out_b': 'new_m', 'new_m_w_pool': 'new_m', 'new_m_pool_scale': 'new_m', 'new_m_w_o': 'new_m', 'new_m_g_mlp': 'new_m', 'new_m_w_mlp1': 'new_m', 'new_m_w_mlp2': 'new_m', 'new_m_g_final': 'new_m', 'new_v_g_mix': 'new_v', 'new_v_w_in': 'new_v', 'new_v_b_in': 'new_v', 'new_v_conv_a': 'new_v', 'new_v_w_out_a': 'new_v', 'new_v_conv_b': 'new_v', 'new_v_conv_b_bias': 'new_v', 'new_v_ln_b_g': 'new_v', 'new_v_ln_b_b': 'new_v', 'new_v_w_out_b': 'new_v', 'new_v_b_out_b': 'new_v', 'new_v_w_pool': 'new_v', 'new_v_pool_scale': 'new_v', 'new_v_w_o': 'new_v', 'new_v_g_mlp': 'new_v', 'new_v_w_mlp1': 'new_v', 'new_v_w_mlp2': 'new_v', 'new_v_g_final': 'new_v'}


def _forward(args):
    return _fwd_reference(*[args[k] for k in FWD_PARAMS])


def _output_shape():
    def fwd():
        inp = _fwd_setup_inputs(0)
        return _fwd_reference(*[inp[k] for k in FWD_PARAMS])
    out = _jax.eval_shape(fwd)
    return out.shape, out.dtype

N_MICROBATCH = 1
ADAM_LR = 0.001
ADAM_B1 = 0.9
ADAM_B2 = 0.999
ADAM_EPS = 1e-08
ADAM_WD = 0.01
ADAM_STEP = 10
PER_EXAMPLE_BATCH_AXIS = {'x': 0, 'loss_target': 0}
SHARED_INPUTS = []
_WEIGHT_DTYPES = {'g_mix': _jnp.float32, 'w_in': _jnp.float32, 'b_in': _jnp.float32, 'conv_a': _jnp.float32, 'w_out_a': _jnp.float32, 'conv_b': _jnp.float32, 'conv_b_bias': _jnp.float32, 'ln_b_g': _jnp.float32, 'ln_b_b': _jnp.float32, 'w_out_b': _jnp.float32, 'b_out_b': _jnp.float32, 'w_pool': _jnp.float32, 'pool_scale': _jnp.float32, 'w_o': _jnp.float32, 'g_mlp': _jnp.float32, 'w_mlp1': _jnp.float32, 'w_mlp2': _jnp.float32, 'g_final': _jnp.float32}
MOMENT_SCALE = {'g_mix': 2.263858e-01, 'w_in': 7.580915e-02, 'b_in': 6.922609e-02, 'conv_a': 1.076586e-01, 'w_out_a': 1.071850e-01, 'conv_b': 6.826318e-02, 'conv_b_bias': 1.451510e-01, 'ln_b_g': 8.456095e-02, 'ln_b_b': 8.843429e-02, 'w_out_b': 6.907654e-02, 'b_out_b': 1.624489e-01, 'w_pool': 9.325443e-02, 'pool_scale': 9.511302e-02, 'w_o': 1.581226e-01, 'g_mlp': 2.026550e-01, 'w_mlp1': 9.982021e-02, 'w_mlp2': 2.014732e-01, 'g_final': 6.482325e+01}


def _to_microbatches(a, axis):
    t = _jnp.moveaxis(a, axis, 0)
    t = t.reshape((N_MICROBATCH, t.shape[0] // N_MICROBATCH) + t.shape[1:])
    return _jnp.moveaxis(t, 1, axis + 1)


def setup_inputs(seed: int = 0) -> dict:
    inp = _fwd_setup_inputs(seed)
    key = _jax.random.fold_in(_jax.random.key(seed), 7919)
    shape, _ = _output_shape()
    out = dict(inp)
    out["loss_target"] = _jax.random.normal(_jax.random.fold_in(key, 0), shape, _jnp.float32)
    for i, name in enumerate(TWIN_WEIGHTS):
        w = inp[name].astype(_jnp.float32)
        if MOMENT_SCALE is None:
            s = _jnp.sqrt(_jnp.mean(_jnp.square(w)) + 1e-30)
        else:
            s = MOMENT_SCALE[name]
        km, kv = _jax.random.split(_jax.random.fold_in(key, i + 1))
        out[name] = w
        out["m_" + name] = s * _jax.random.normal(km, w.shape, _jnp.float32)
        out["v_" + name] = (s * s) * _jax.random.uniform(kv, w.shape, _jnp.float32, 0.5, 1.5)
    if N_MICROBATCH > 1:
        for name, axis in PER_EXAMPLE_BATCH_AXIS.items():
            out[name] = _to_microbatches(out[name], axis)
    return {'x': out['x'], 'g_mix': out['g_mix'], 'w_in': out['w_in'], 'b_in': out['b_in'], 'conv_a': out['conv_a'], 'w_out_a': out['w_out_a'], 'conv_b': out['conv_b'], 'conv_b_bias': out['conv_b_bias'], 'ln_b_g': out['ln_b_g'], 'ln_b_b': out['ln_b_b'], 'w_out_b': out['w_out_b'], 'b_out_b': out['b_out_b'], 'w_pool': out['w_pool'], 'pool_scale': out['pool_scale'], 'w_o': out['w_o'], 'g_mlp': out['g_mlp'], 'w_mlp1': out['w_mlp1'], 'w_mlp2': out['w_mlp2'], 'g_final': out['g_final'], 'loss_target': out['loss_target'], 'm_g_mix': out['m_g_mix'], 'm_w_in': out['m_w_in'], 'm_b_in': out['m_b_in'], 'm_conv_a': out['m_conv_a'], 'm_w_out_a': out['m_w_out_a'], 'm_conv_b': out['m_conv_b'], 'm_conv_b_bias': out['m_conv_b_bias'], 'm_ln_b_g': out['m_ln_b_g'], 'm_ln_b_b': out['m_ln_b_b'], 'm_w_out_b': out['m_w_out_b'], 'm_b_out_b': out['m_b_out_b'], 'm_w_pool': out['m_w_pool'], 'm_pool_scale': out['m_pool_scale'], 'm_w_o': out['m_w_o'], 'm_g_mlp': out['m_g_mlp'], 'm_w_mlp1': out['m_w_mlp1'], 'm_w_mlp2': out['m_w_mlp2'], 'm_g_final': out['m_g_final'], 'v_g_mix': out['v_g_mix'], 'v_w_in': out['v_w_in'], 'v_b_in': out['v_b_in'], 'v_conv_a': out['v_conv_a'], 'v_w_out_a': out['v_w_out_a'], 'v_conv_b': out['v_conv_b'], 'v_conv_b_bias': out['v_conv_b_bias'], 'v_ln_b_g': out['v_ln_b_g'], 'v_ln_b_b': out['v_ln_b_b'], 'v_w_out_b': out['v_w_out_b'], 'v_b_out_b': out['v_b_out_b'], 'v_w_pool': out['v_w_pool'], 'v_pool_scale': out['v_pool_scale'], 'v_w_o': out['v_w_o'], 'v_g_mlp': out['v_g_mlp'], 'v_w_mlp1': out['v_w_mlp1'], 'v_w_mlp2': out['v_w_mlp2'], 'v_g_final': out['v_g_final']}


def _loss(weights, diff, rest, loss_target):
    with _jax.named_scope("forward"):
        args = {**rest, TWIN_DIFF_INPUT: diff, **{k: w.astype(_WEIGHT_DTYPES[k]) for k, w in weights.items()}}
        y = _forward(args)
    with _jax.named_scope("loss_head"):
        err = _jnp.square(y.astype(_jnp.float32) - loss_target)
        return 0.5 * _jnp.sum(_jnp.mean(err, axis=-1)) if err.ndim else 0.5 * err


def _adamw(w, g, m, v):
    m = ADAM_B1 * m + (1.0 - ADAM_B1) * g
    v = ADAM_B2 * v + (1.0 - ADAM_B2) * _jnp.square(g)
    m_hat = m / (1.0 - ADAM_B1 ** ADAM_STEP)
    v_hat = v / (1.0 - ADAM_B2 ** ADAM_STEP)
    delta = -ADAM_LR * (m_hat / (_jnp.sqrt(v_hat) + ADAM_EPS) + ADAM_WD * w)
    return delta, m, v


def reference(x, g_mix, w_in, b_in, conv_a, w_out_a, conv_b, conv_b_bias, ln_b_g, ln_b_b, w_out_b, b_out_b, w_pool, pool_scale, w_o, g_mlp, w_mlp1, w_mlp2, g_final, loss_target, m_g_mix, m_w_in, m_b_in, m_conv_a, m_w_out_a, m_conv_b, m_conv_b_bias, m_ln_b_g, m_ln_b_b, m_w_out_b, m_b_out_b, m_w_pool, m_pool_scale, m_w_o, m_g_mlp, m_w_mlp1, m_w_mlp2, m_g_final, v_g_mix, v_w_in, v_b_in, v_conv_a, v_w_out_a, v_conv_b, v_conv_b_bias, v_ln_b_g, v_ln_b_b, v_w_out_b, v_b_out_b, v_w_pool, v_pool_scale, v_w_o, v_g_mlp, v_w_mlp1, v_w_mlp2, v_g_final):
    given = dict(x=x, g_mix=g_mix, w_in=w_in, b_in=b_in, conv_a=conv_a, w_out_a=w_out_a, conv_b=conv_b, conv_b_bias=conv_b_bias, ln_b_g=ln_b_g, ln_b_b=ln_b_b, w_out_b=w_out_b, b_out_b=b_out_b, w_pool=w_pool, pool_scale=pool_scale, w_o=w_o, g_mlp=g_mlp, w_mlp1=w_mlp1, w_mlp2=w_mlp2, g_final=g_final, loss_target=loss_target, m_g_mix=m_g_mix, m_w_in=m_w_in, m_b_in=m_b_in, m_conv_a=m_conv_a, m_w_out_a=m_w_out_a, m_conv_b=m_conv_b, m_conv_b_bias=m_conv_b_bias, m_ln_b_g=m_ln_b_g, m_ln_b_b=m_ln_b_b, m_w_out_b=m_w_out_b, m_b_out_b=m_b_out_b, m_w_pool=m_w_pool, m_pool_scale=m_pool_scale, m_w_o=m_w_o, m_g_mlp=m_g_mlp, m_w_mlp1=m_w_mlp1, m_w_mlp2=m_w_mlp2, m_g_final=m_g_final, v_g_mix=v_g_mix, v_w_in=v_w_in, v_b_in=v_b_in, v_conv_a=v_conv_a, v_w_out_a=v_w_out_a, v_conv_b=v_conv_b, v_conv_b_bias=v_conv_b_bias, v_ln_b_g=v_ln_b_g, v_ln_b_b=v_ln_b_b, v_w_out_b=v_w_out_b, v_b_out_b=v_b_out_b, v_w_pool=v_w_pool, v_pool_scale=v_pool_scale, v_w_o=v_w_o, v_g_mlp=v_g_mlp, v_w_mlp1=v_w_mlp1, v_w_mlp2=v_w_mlp2, v_g_final=v_g_final)
    weights = {n: given[n] for n in TWIN_WEIGHTS}
    shared = {n: given[n] for n in SHARED_INPUTS}
    per_example = {n: given[n] for n in ['x']}
    grad_fn = _jax.value_and_grad(_loss, argnums=(0, 1))

    def one_microbatch(ex, loss_target):
        ex = dict(ex)
        diff = ex.pop(TWIN_DIFF_INPUT)
        return grad_fn(weights, diff, {**shared, **ex}, loss_target)

    if N_MICROBATCH == 1:
        loss, (grad_w, grad_x) = one_microbatch(per_example, given["loss_target"])
    else:
        def body(carry, xs):
            loss_sum, grad_sum = carry
            l_k, (gw_k, gx_k) = one_microbatch(xs[0], xs[1])
            with _jax.named_scope("update"):
                return (loss_sum + l_k, _jax.tree.map(_jnp.add, grad_sum, gw_k)), gx_k

        init = (_jnp.zeros((), _jnp.float32), _jax.tree.map(_jnp.zeros_like, weights))
        (loss, grad_w), grad_x = _jax.lax.scan(body, init, (per_example, given["loss_target"]))
    with _jax.named_scope("update"):
        delta_w, new_m, new_v = {}, {}, {}
        for n in TWIN_WEIGHTS:
            delta_w[n], new_m[n], new_v[n] = _adamw(weights[n], grad_w[n], given["m_" + n], given["v_" + n])
    return (loss, grad_x, *[grad_w[n] for n in TWIN_WEIGHTS], *[delta_w[n] for n in TWIN_WEIGHTS],
            *[new_m[n] for n in TWIN_WEIGHTS], *[new_v[n] for n in TWIN_WEIGHTS])
```

```python
import functools

import jax
import jax.numpy as jnp
from jax import lax
from jax.experimental import pallas as pl
from jax.experimental.pallas import tpu as pltpu

F32 = jnp.float32
BF16 = jnp.bfloat16
MESH = pl.DeviceIdType.MESH

N_DEV = 8
EPS = 1e-6
K_A = 3
K_B = 31
POOL_WINDOWS = (2, 4, 8, 16)
N_GROUPS = len(POOL_WINDOWS)
HALO = 32
CHUNK = 16
ADAM_LR, ADAM_B1, ADAM_B2, ADAM_EPS, ADAM_WD, ADAM_STEP = 0.001, 0.9, 0.999, 1e-08, 0.01, 10
VMEM_LIMIT_BYTES = 56 * 1024 * 1024

NN = (((1,), (0,)), ((), ()))
NT = (((1,), (1,)), ((), ()))
TN = (((0,), (0,)), ((), ()))


def _params(*sem):
    return pltpu.CompilerParams(dimension_semantics=sem, vmem_limit_bytes=VMEM_LIMIT_BYTES)


def _sigmoid(v):
    return 1.0 / (1.0 + jnp.exp(-v))


def _mm(name, a, b, *, grid, a_spec, b_spec, out_shape, o_spec, dims, nk=1, acc_shape=None,
        extras=(), extra_specs=(), prologue=None, epilogue=None, alias_in=None):
    n_extra = len(extras)
    has_alias = alias_in is not None

    def body(*refs):
        a_ref, b_ref = refs[0], refs[1]
        ex = refs[2:2 + n_extra]
        o_ref = refs[2 + n_extra + (1 if has_alias else 0)]
        av = a_ref[...]
        if prologue is not None:
            av = prologue(av)
        av = av.astype(BF16)
        bv = b_ref[...]
        bv = bv.reshape((-1, bv.shape[-1])).astype(BF16)
        p = lax.dot_general(av, bv, dims, preferred_element_type=F32)

        def finish(val):
            if epilogue is not None:
                val = epilogue(val, *[e[...] for e in ex])
            o_ref[...] = val.astype(o_ref.dtype).reshape(o_ref.shape)

        if nk == 1:
            finish(p)
        else:
            acc = refs[-1]
            k = pl.program_id(len(grid) - 1)

            @pl.when(k == 0)
            def _():
                acc[...] = p

            @pl.when(k > 0)
            def _():
                acc[...] += p

            @pl.when(k == nk - 1)
            def _():
                finish(acc[...])

    in_specs = [a_spec, b_spec, *extra_specs]
    operands = [a, b, *extras]
    aliases = {}
    if has_alias:
        in_specs.append(pl.BlockSpec(memory_space=pl.ANY))
        operands.append(alias_in)
        aliases = {len(operands) - 1: 0}
    sem = ("parallel",) * (len(grid) - 1) + (("arbitrary",) if nk > 1 else ("parallel",))
    return pl.pallas_call(
        body, name=name, grid=grid, in_specs=in_specs, out_specs=o_spec, out_shape=out_shape,
        scratch_shapes=[pltpu.VMEM(acc_shape, F32)] if nk > 1 else [],
        input_output_aliases=aliases, compiler_params=_params(*sem),
    )(*operands)


def _relu_sq(v):
    r = jnp.maximum(v.astype(F32), 0.0)
    return r * r


def _rms_fwd(name, x, g, tm):
    s, d = x.shape

    def body(x_ref, g_ref, h_ref):
        xv = x_ref[...]
        r = lax.rsqrt(jnp.mean(xv * xv, axis=-1, keepdims=True) + EPS)
        h_ref[...] = (xv * r * g_ref[...]).astype(h_ref.dtype)

    return pl.pallas_call(
        body, name=name, grid=(s // tm,),
        in_specs=[pl.BlockSpec((tm, d), lambda i: (i, 0)), pl.BlockSpec((1, d), lambda i: (0, 0))],
        out_specs=pl.BlockSpec((tm, d), lambda i: (i, 0)),
        out_shape=jax.ShapeDtypeStruct((s, d), BF16), compiler_params=_params("parallel"),
    )(x, g)


def _colsum8(v):
    return jnp.sum(v.reshape(v.shape[0] // 8, 8, v.shape[1]), axis=0)


def _rms_bwd(name, dh, x, g, dres, tm):
    s, d = x.shape
    n = s // tm

    def body(dh_ref, x_ref, g_ref, dr_ref, dx_ref, dg_ref, acc):
        i = pl.program_id(0)
        xv = x_ref[...]
        r = lax.rsqrt(jnp.mean(xv * xv, axis=-1, keepdims=True) + EPS)
        xh = xv * r
        dhv = dh_ref[...].astype(F32)
        part = _colsum8(dhv * xh)

        @pl.when(i == 0)
        def _():
            acc[...] = part

        @pl.when(i > 0)
        def _():
            acc[...] += part

        dxh = dhv * g_ref[...]
        dx = r * (dxh - xh * jnp.mean(dxh * xh, axis=-1, keepdims=True))
        dx_ref[...] = dx + dr_ref[...]

        @pl.when(i == n - 1)
        def _():
            dg_ref[...] = jnp.sum(acc[...], axis=0, keepdims=True)

    return pl.pallas_call(
        body, name=name, grid=(n,),
        in_specs=[pl.BlockSpec((tm, d), lambda i: (i, 0)), pl.BlockSpec((tm, d), lambda i: (i, 0)),
                  pl.BlockSpec((1, d), lambda i: (0, 0)), pl.BlockSpec((tm, d), lambda i: (i, 0))],
        out_specs=[pl.BlockSpec((tm, d), lambda i: (i, 0)), pl.BlockSpec((1, d), lambda i: (0, 0))],
        out_shape=[jax.ShapeDtypeStruct((s, d), F32), jax.ShapeDtypeStruct((1, d), F32)],
        scratch_shapes=[pltpu.VMEM((8, d), F32)], compiler_params=_params("arbitrary"),
    )(dh, x, g, dres)


def _loss_head(x, g, target, tm):
    s, d = x.shape
    n = s // tm

    def body(x_ref, g_ref, t_ref, loss_ref, dx_ref, dg_ref, acc_l, acc_g):
        i = pl.program_id(0)
        xv = x_ref[...]
        r = lax.rsqrt(jnp.mean(xv * xv, axis=-1, keepdims=True) + EPS)
        xh = xv * r
        err = xh * g_ref[...] - t_ref[...]
        dy = err * (1.0 / d)
        lpart = _colsum8(err * err)
        gpart = _colsum8(dy * xh)

        @pl.when(i == 0)
        def _():
            acc_l[...] = lpart
            acc_g[...] = gpart

        @pl.when(i > 0)
        def _():
            acc_l[...] += lpart
            acc_g[...] += gpart

        dxh = dy * g_ref[...]
        dx_ref[...] = r * (dxh - xh * jnp.mean(dxh * xh, axis=-1, keepdims=True))

        @pl.when(i == n - 1)
        def _():
            loss_ref[...] = (0.5 / d) * jnp.sum(jnp.sum(acc_l[...], axis=0, keepdims=True), axis=1, keepdims=True)
            dg_ref[...] = jnp.sum(acc_g[...], axis=0, keepdims=True)

    return pl.pallas_call(
        body, name="loss_head", grid=(n,),
        in_specs=[pl.BlockSpec((tm, d), lambda i: (i, 0)), pl.BlockSpec((1, d), lambda i: (0, 0)),
                  pl.BlockSpec((tm, d), lambda i: (i, 0))],
        out_specs=[pl.BlockSpec((1, 1), lambda i: (0, 0)), pl.BlockSpec((tm, d), lambda i: (i, 0)),
                   pl.BlockSpec((1, d), lambda i: (0, 0))],
        out_shape=[jax.ShapeDtypeStruct((1, 1), F32), jax.ShapeDtypeStruct((s, d), F32),
                   jax.ShapeDtypeStruct((1, d), F32)],
        scratch_shapes=[pltpu.VMEM((8, d), F32), pltpu.VMEM((8, d), F32)], compiler_params=_params("arbitrary"),
    )(x, g, target)


def _sec(ref, n, d):
    return ref[:, n * d:(n + 1) * d].astype(F32)


def _pool_count(row0, rows, window):
    t = row0 + lax.broadcasted_iota(jnp.int32, (rows, 1), 0)
    return jnp.minimum(t + 1, window).astype(F32)


def _mix_pre_fwd(name, proj, conv_a, conv_b, conv_b_bias, ln_g, ln_b, d, tm):
    s = proj.shape[0]
    n = s // tm
    gc = d // N_GROUPS
    hb = tm // HALO

    def body(pj_ref, hp_ref, ca_ref, cb_ref, cbb_ref, lng_ref, lnb_ref, pa_ref, sw_ref, pc_ref, cv_ref,
             eua, eub, euc):
        i = pl.program_id(0)
        keep = (i > 0).astype(F32)
        eua[0:HALO, :] = _sec(hp_ref, 1, d) * _sec(hp_ref, 2, d) * keep
        eub[0:HALO, :] = _sec(hp_ref, 3, d) * _sigmoid(_sec(hp_ref, 4, d)) * keep
        euc[0:HALO, :] = _sec(hp_ref, 5, d) * keep
        eua[HALO:HALO + tm, :] = _sec(pj_ref, 1, d) * _sec(pj_ref, 2, d)
        eub[HALO:HALO + tm, :] = _sec(pj_ref, 3, d) * _sigmoid(_sec(pj_ref, 4, d))
        euc[HALO:HALO + tm, :] = _sec(pj_ref, 5, d)
        for c in range(tm // CHUNK):
            r0 = c * CHUNK
            z = jnp.zeros((CHUNK, d), F32)
            for k in range(K_A):
                z = z + ca_ref[k:k + 1, :] * eua[HALO + r0 - (K_A - 1) + k:HALO + r0 - (K_A - 1) + k + CHUNK, :]
            pa_ref[r0:r0 + CHUNK, :] = (pj_ref[r0:r0 + CHUNK, 0:d].astype(F32) * z).astype(pa_ref.dtype)
            cv = jnp.zeros((CHUNK, d), F32) + cbb_ref[...]
            for k in range(K_B):
                cv = cv + cb_ref[k:k + 1, :] * eub[HALO + r0 - (K_B - 1) + k:HALO + r0 - (K_B - 1) + k + CHUNK, :]
            cv_ref[r0:r0 + CHUNK, :] = cv.astype(cv_ref.dtype)
        cvv = cv_ref[...].astype(F32)
        mu = jnp.mean(cvv, axis=-1, keepdims=True)
        xc = cvv - mu
        xh = xc * lax.rsqrt(jnp.mean(xc * xc, axis=-1, keepdims=True) + EPS)
        ln = xh * lng_ref[...] + lnb_ref[...]
        sw_ref[...] = (ln * _sigmoid(ln)).astype(sw_ref.dtype)
        for gi, w in enumerate(POOL_WINDOWS):
            cols = slice(gi * gc, (gi + 1) * gc)
            tot = euc[HALO:HALO + tm, cols]
            for k in range(1, w):
                tot = tot + euc[HALO - k:HALO - k + tm, cols]
            cnt = _pool_count(i * tm, tm, w)
            pc_ref[:, cols] = (tot / cnt - euc[HALO:HALO + tm, cols]).astype(pc_ref.dtype)

    row = lambda i: (i, 0)
    fixed = lambda i: (0, 0)
    act = jax.ShapeDtypeStruct((s, d), BF16)
    return pl.pallas_call(
        body, name=name, grid=(n,),
        in_specs=[pl.BlockSpec((tm, 6 * d), row),
                  pl.BlockSpec((HALO, 6 * d), lambda i: (jnp.maximum(i * hb - 1, 0), 0)),
                  pl.BlockSpec((K_A, d), fixed), pl.BlockSpec((K_B, d), fixed), pl.BlockSpec((1, d), fixed),
                  pl.BlockSpec((1, d), fixed), pl.BlockSpec((1, d), fixed)],
        out_specs=[pl.BlockSpec((tm, d), row)] * 4,
        out_shape=[act, act, act, act],
        scratch_shapes=[pltpu.VMEM((tm + HALO, d), F32)] * 3, compiler_params=_params("parallel"),
    )(proj, proj, conv_a, conv_b, conv_b_bias, ln_g, ln_b)


def _merge_fwd(name, proj, ya, yb, pw, scale, d, tm):
    s = proj.shape[0]

    def body(g_ref, ya_ref, yb_ref, pw_ref, sc_ref, o_ref):
        m = _sigmoid(_sec(g_ref, 0, d)) * ya_ref[...].astype(F32)
        m = m + _sigmoid(_sec(g_ref, 1, d)) * yb_ref[...].astype(F32)
        m = m + _sigmoid(_sec(g_ref, 2, d)) * (pw_ref[...].astype(F32) * sc_ref[...])
        o_ref[...] = m.astype(o_ref.dtype)

    row = lambda i: (i, 0)
    return pl.pallas_call(
        body, name=name, grid=(s // tm,),
        in_specs=[pl.BlockSpec((tm, 3 * d), lambda i: (i, 2)), pl.BlockSpec((tm, d), row), pl.BlockSpec((tm, d), row),
                  pl.BlockSpec((tm, d), row), pl.BlockSpec((1, d), lambda i: (0, 0))],
        out_specs=pl.BlockSpec((tm, d), row), out_shape=jax.ShapeDtypeStruct((s, d), BF16),
        compiler_params=_params("parallel"),
    )(proj, ya, yb, pw, scale)


def _merge_bwd(name, dm, proj, ya, yb, pw, scale, d, tm):
    s = proj.shape[0]
    n = s // tm

    def body(dm_ref, g_ref, ya_ref, yb_ref, pw_ref, sc_ref, dya_ref, dyb_ref, dpw_ref, dg_ref, dbo_ref, dsc_ref,
             acc_b, acc_s):
        i = pl.program_id(0)
        dmv = dm_ref[...].astype(F32)
        g0 = _sigmoid(_sec(g_ref, 0, d))
        dya_ref[...] = (dmv * g0).astype(dya_ref.dtype)
        dg_ref[:, 0:d] = (dmv * ya_ref[...].astype(F32) * g0 * (1.0 - g0)).astype(dg_ref.dtype)
        g1 = _sigmoid(_sec(g_ref, 1, d))
        dyb = dmv * g1
        dyb_ref[...] = dyb.astype(dyb_ref.dtype)
        dg_ref[:, d:2 * d] = (dmv * yb_ref[...].astype(F32) * g1 * (1.0 - g1)).astype(dg_ref.dtype)
        g2 = _sigmoid(_sec(g_ref, 2, d))
        pwv = pw_ref[...].astype(F32)
        dyc = dmv * g2
        dpw_ref[...] = (dyc * sc_ref[...]).astype(dpw_ref.dtype)
        dg_ref[:, 2 * d:3 * d] = (dmv * (pwv * sc_ref[...]) * g2 * (1.0 - g2)).astype(dg_ref.dtype)
        pb = _colsum8(dyb)
        ps = _colsum8(dyc * pwv)

        @pl.when(i == 0)
        def _():
            acc_b[...] = pb
            acc_s[...] = ps

        @pl.when(i > 0)
        def _():
            acc_b[...] += pb
            acc_s[...] += ps

        @pl.when(i == n - 1)
        def _():
            dbo_ref[...] = jnp.sum(acc_b[...], axis=0, keepdims=True)
            dsc_ref[...] = jnp.sum(acc_s[...], axis=0, keepdims=True)

    row = lambda i: (i, 0)
    fixed = lambda i: (0, 0)
    act = jax.ShapeDtypeStruct((s, d), BF16)
    vec = jax.ShapeDtypeStruct((1, d), F32)
    return pl.pallas_call(
        body, name=name, grid=(n,),
        in_specs=[pl.BlockSpec((tm, d), row), pl.BlockSpec((tm, 3 * d), lambda i: (i, 2)), pl.BlockSpec((tm, d), row),
                  pl.BlockSpec((tm, d), row), pl.BlockSpec((tm, d), row), pl.BlockSpec((1, d), fixed)],
        out_specs=[pl.BlockSpec((tm, d), row)] * 3 + [pl.BlockSpec((tm, 3 * d), row), pl.BlockSpec((1, d), fixed),
                                                      pl.BlockSpec((1, d), fixed)],
        out_shape=[act, act, act, jax.ShapeDtypeStruct((s, 3 * d), BF16), vec, vec],
        scratch_shapes=[pltpu.VMEM((8, d), F32)] * 2, compiler_params=_params("arbitrary"),
    )(dm, proj, ya, yb, pw, scale)


def _mix_pre_bwd(name, proj, cv, dpa, dsw, dpc, dgates, conv_a, conv_b, ln_g, ln_b, d, tm):
    s = proj.shape[0]
    n = s // tm
    gc = d // N_GROUPS
    hb = tm // HALO
    last_halo = s // HALO - 1
    te = tm + HALO

    def ln_bwd(cvv, dswv, lng, lnb):
        mu = jnp.mean(cvv, axis=-1, keepdims=True)
        xc = cvv - mu
        rstd = lax.rsqrt(jnp.mean(xc * xc, axis=-1, keepdims=True) + EPS)
        xh = xc * rstd
        ln = xh * lng + lnb
        sg = _sigmoid(ln)
        dln = dswv * (sg * (1.0 + ln * (1.0 - sg)))
        dxh = dln * lng
        dcv = rstd * (dxh - jnp.mean(dxh, axis=-1, keepdims=True) - xh * jnp.mean(dxh * xh, axis=-1, keepdims=True))
        return dcv, dln, xh

    def body(pj_ref, hp_ref, hf_ref, cv_ref, cvf_ref, dpa_ref, dpaf_ref, dsw_ref, dswf_ref, dpc_ref, dpcf_ref, dgt_ref,
             ca_ref, cb_ref, lng_ref, lnb_ref,
             dpj_ref, dbin_ref, dca_ref, dcb_ref, dcbb_ref, dlng_ref, dlnb_ref,
             eua, eub, edz, edcv, eq, acc_bin, acc_ca, acc_cb, acc_v):
        i = pl.program_id(0)
        keep_p = (i > 0).astype(F32)
        keep_f = (i < n - 1).astype(F32)

        @pl.when(i == 0)
        def _():
            acc_bin[...] = jnp.zeros_like(acc_bin)
            acc_ca[...] = jnp.zeros_like(acc_ca)
            acc_cb[...] = jnp.zeros_like(acc_cb)
            acc_v[...] = jnp.zeros_like(acc_v)

        eua[0:HALO, :] = _sec(hp_ref, 1, d) * _sec(hp_ref, 2, d) * keep_p
        eub[0:HALO, :] = _sec(hp_ref, 3, d) * _sigmoid(_sec(hp_ref, 4, d)) * keep_p
        eua[HALO:te, :] = _sec(pj_ref, 1, d) * _sec(pj_ref, 2, d)
        eub[HALO:te, :] = _sec(pj_ref, 3, d) * _sigmoid(_sec(pj_ref, 4, d))
        edz[0:tm, :] = dpa_ref[...].astype(F32) * _sec(pj_ref, 0, d)
        edz[tm:te, :] = dpaf_ref[...].astype(F32) * _sec(hf_ref, 0, d) * keep_f
        dcv, dln, xh = ln_bwd(cv_ref[...].astype(F32), dsw_ref[...].astype(F32), lng_ref[...], lnb_ref[...])
        edcv[0:tm, :] = dcv
        acc_v[0:8, :] += _colsum8(dcv)
        acc_v[8:16, :] += _colsum8(dln * xh)
        acc_v[16:24, :] += _colsum8(dln)
        dcvf, _, _ = ln_bwd(cvf_ref[...].astype(F32), dswf_ref[...].astype(F32), lng_ref[...], lnb_ref[...])
        edcv[tm:te, :] = dcvf * keep_f
        for gi, w in enumerate(POOL_WINDOWS):
            cols = slice(gi * gc, (gi + 1) * gc)
            eq[0:tm, cols] = dpc_ref[:, cols].astype(F32) / _pool_count(i * tm, tm, w)
            eq[tm:te, cols] = dpcf_ref[:, cols].astype(F32) / _pool_count((i + 1) * tm, HALO, w) * keep_f

        def put(sec_idx, r0, val):
            dpj_ref[r0:r0 + CHUNK, sec_idx * d:(sec_idx + 1) * d] = val.astype(dpj_ref.dtype)
            acc_bin[:, sec_idx * d:(sec_idx + 1) * d] += _colsum8(val)

        for c in range(tm // CHUNK):
            r0 = c * CHUNK
            rows = slice(r0, r0 + CHUNK)
            z = jnp.zeros((CHUNK, d), F32)
            dua = jnp.zeros((CHUNK, d), F32)
            for k in range(K_A):
                z = z + ca_ref[k:k + 1, :] * eua[HALO + r0 - (K_A - 1) + k:HALO + r0 - (K_A - 1) + k + CHUNK, :]
                dua = dua + ca_ref[k:k + 1, :] * edz[r0 + (K_A - 1) - k:r0 + (K_A - 1) - k + CHUNK, :]
            put(0, r0, dpa_ref[rows, :].astype(F32) * z)
            put(1, r0, dua * pj_ref[rows, 2 * d:3 * d].astype(F32))
            put(2, r0, dua * pj_ref[rows, d:2 * d].astype(F32))
            dub = jnp.zeros((CHUNK, d), F32)
            for k in range(K_B):
                dub = dub + cb_ref[k:k + 1, :] * edcv[r0 + (K_B - 1) - k:r0 + (K_B - 1) - k + CHUNK, :]
            bval = pj_ref[rows, 3 * d:4 * d].astype(F32)
            sg = _sigmoid(pj_ref[rows, 4 * d:5 * d].astype(F32))
            put(3, r0, dub * sg)
            put(4, r0, dub * bval * sg * (1.0 - sg))
            for gi, w in enumerate(POOL_WINDOWS):
                cols = slice(gi * gc, (gi + 1) * gc)
                tot = eq[rows, cols]
                for k in range(1, w):
                    tot = tot + eq[r0 + k:r0 + k + CHUNK, cols]
                dci = tot - dpc_ref[rows, cols].astype(F32)
                dpj_ref[rows, 5 * d + gi * gc:5 * d + (gi + 1) * gc] = dci.astype(dpj_ref.dtype)
                acc_bin[:, 5 * d + gi * gc:5 * d + (gi + 1) * gc] += _colsum8(dci)
        for q in range(3):
            gv = dgt_ref[:, q * d:(q + 1) * d]
            dpj_ref[:, (6 + q) * d:(7 + q) * d] = gv
            acc_bin[:, (6 + q) * d:(7 + q) * d] += _colsum8(gv.astype(F32))
        for k in range(K_A):
            a = jnp.zeros((8, d), F32)
            for c in range(tm // CHUNK):
                r0 = c * CHUNK
                a = a + _colsum8(edz[r0:r0 + CHUNK, :] * eua[HALO + r0 - (K_A - 1) + k:HALO + r0 - (K_A - 1) + k + CHUNK, :])
            acc_ca[k] += a
        for k in range(K_B):
            a = jnp.zeros((8, d), F32)
            for c in range(tm // CHUNK):
                r0 = c * CHUNK
                a = a + _colsum8(edcv[r0:r0 + CHUNK, :] * eub[HALO + r0 - (K_B - 1) + k:HALO + r0 - (K_B - 1) + k + CHUNK, :])
            acc_cb[k] += a

        @pl.when(i == n - 1)
        def _():
            dbin_ref[...] = jnp.sum(acc_bin[...], axis=0, keepdims=True)
            for k in range(K_A):
                dca_ref[k:k + 1, :] = jnp.sum(acc_ca[k], axis=0, keepdims=True)
            for k in range(K_B):
                dcb_ref[k:k + 1, :] = jnp.sum(acc_cb[k], axis=0, keepdims=True)
            dcbb_ref[...] = jnp.sum(acc_v[0:8, :], axis=0, keepdims=True)
            dlng_ref[...] = jnp.sum(acc_v[8:16, :], axis=0, keepdims=True)
            dlnb_ref[...] = jnp.sum(acc_v[16:24, :], axis=0, keepdims=True)

    row = lambda i: (i, 0)
    fixed = lambda i: (0, 0)
    past = lambda i: (jnp.maximum(i * hb - 1, 0), 0)
    fut = lambda i: (jnp.minimum((i + 1) * hb, last_halo), 0)
    vec = jax.ShapeDtypeStruct((1, d), F32)
    return pl.pallas_call(
        body, name=name, grid=(n,),
        in_specs=[pl.BlockSpec((tm, 6 * d), row), pl.BlockSpec((HALO, 6 * d), past), pl.BlockSpec((HALO, 6 * d), fut),
                  pl.BlockSpec((tm, d), row), pl.BlockSpec((HALO, d), fut),
                  pl.BlockSpec((tm, d), row), pl.BlockSpec((HALO, d), fut),
                  pl.BlockSpec((tm, d), row), pl.BlockSpec((HALO, d), fut),
                  pl.BlockSpec((tm, d), row), pl.BlockSpec((HALO, d), fut),
                  pl.BlockSpec((tm, 3 * d), row),
                  pl.BlockSpec((K_A, d), fixed), pl.BlockSpec((K_B, d), fixed), pl.BlockSpec((1, d), fixed),
                  pl.BlockSpec((1, d), fixed)],
        out_specs=[pl.BlockSpec((tm, 9 * d), row), pl.BlockSpec((1, 9 * d), fixed), pl.BlockSpec((K_A, d), fixed),
                   pl.BlockSpec((K_B, d), fixed), pl.BlockSpec((1, d), fixed), pl.BlockSpec((1, d), fixed),
                   pl.BlockSpec((1, d), fixed)],
        out_shape=[jax.ShapeDtypeStruct((s, 9 * d), BF16), jax.ShapeDtypeStruct((1, 9 * d), F32),
                   jax.ShapeDtypeStruct((K_A, d), F32), jax.ShapeDtypeStruct((K_B, d), F32), vec, vec, vec],
        scratch_shapes=[pltpu.VMEM((te, d), F32)] * 5 + [pltpu.VMEM((8, 9 * d), F32), pltpu.VMEM((K_A, 8, d), F32),
                                                         pltpu.VMEM((K_B, 8, d), F32), pltpu.VMEM((24, d), F32)],
        compiler_params=_params("arbitrary"),
    )(proj, proj, proj, cv, cv, dpa, dpa, dsw, dsw, dpc, dpc, dgates, conv_a, conv_b, ln_g, ln_b)


def _pool_mm(name, a, gpool, l, d, tm, dims):
    s = a.shape[0]
    gc = d // N_GROUPS

    def body(a_ref, w_ref, o_ref):
        for gi in range(N_GROUPS):
            w = w_ref[:, gi].reshape(gc, gc)
            o_ref[:, gi * gc:(gi + 1) * gc] = lax.dot_general(
                a_ref[:, gi * gc:(gi + 1) * gc], w, dims, preferred_element_type=F32).astype(o_ref.dtype)

    return pl.pallas_call(
        body, name=name, grid=(s // tm,),
        in_specs=[pl.BlockSpec((tm, d), lambda i: (i, 0)),
                  pl.BlockSpec((N_DEV, None, N_GROUPS, gc // N_DEV, gc), lambda i: (0, l, 0, 0, 0))],
        out_specs=pl.BlockSpec((tm, d), lambda i: (i, 0)), out_shape=jax.ShapeDtypeStruct((s, d), BF16),
        compiler_params=_params("parallel"),
    )(a, gpool)


def _pool_wgrad(name, p, dpw, l, n_layers, d, tk, alias_in):
    s = p.shape[0]
    gc = d // N_GROUPS
    n = s // tk

    def body(p_ref, g_ref, *rest):
        o_ref, acc = rest[-2], rest[-1]
        k = pl.program_id(0)
        for gi in range(N_GROUPS):
            cols = slice(gi * gc, (gi + 1) * gc)
            part = lax.dot_general(p_ref[:, cols], g_ref[:, cols], TN, preferred_element_type=F32)

            @pl.when(k == 0)
            def _():
                acc[gi] = part

            @pl.when(k > 0)
            def _():
                acc[gi] += part

        @pl.when(k == n - 1)
        def _():
            for gi in range(N_GROUPS):
                o_ref[:, gi] = acc[gi].astype(o_ref.dtype).reshape(N_DEV, gc // N_DEV, gc)

    in_specs = [pl.BlockSpec((tk, d), lambda k: (k, 0)), pl.BlockSpec((tk, d), lambda k: (k, 0))]
    operands = [p, dpw]
    aliases = {}
    if alias_in is not None:
        in_specs.append(pl.BlockSpec(memory_space=pl.ANY))
        operands.append(alias_in)
        aliases = {2: 0}
    return pl.pallas_call(
        body, name=name, grid=(n,), in_specs=in_specs,
        out_specs=pl.BlockSpec((N_DEV, None, N_GROUPS, gc // N_DEV, gc), lambda k: (0, l, 0, 0, 0)),
        out_shape=jax.ShapeDtypeStruct((N_DEV, n_layers, N_GROUPS, gc // N_DEV, gc), BF16),
        scratch_shapes=[pltpu.VMEM((N_GROUPS, gc, gc), F32)], input_output_aliases=aliases,
        compiler_params=_params("arbitrary"),
    )(*operands)


def _my_place():
    x, y, c = lax.axis_index("x"), lax.axis_index("y"), lax.axis_index("c")
    return x, y, c


def _block_of(x, y, c):
    return 4 * x + 2 * y + c


def _gather_shards(shards):
    n_arr = len(shards)

    def body(*refs):
        srcs = refs[:n_arr]
        outs = refs[n_arr:2 * n_arr]
        send_sems, recv_sems, local_sems = refs[2 * n_arr:]
        x, y, c = _my_place()
        me, sibling = (x, y, c), (x, y, 1 - c)
        chips = [(1 - x, y), (x, 1 - y), (1 - x, 1 - y)]

        def copy(n, k, block, to, src=None):
            rows = outs[n].at[_block_of(*block)]
            return pltpu.make_async_remote_copy(
                src_ref=rows if src is None else src, dst_ref=rows, send_sem=send_sems.at[n, k],
                recv_sem=recv_sems.at[n, k], device_id=to, device_id_type=MESH)

        mine = [pltpu.make_async_copy(srcs[n], outs[n].at[_block_of(*me)], local_sems.at[n]) for n in range(n_arr)]
        for cp in mine:
            cp.start()
        first = []
        for n in range(n_arr):
            first.append(copy(n, 0, me, sibling, src=srcs[n]))
            first += [copy(n, 1 + j, me, (*chip, c), src=srcs[n]) for j, chip in enumerate(chips)]
        for cp in first:
            cp.start()
        passed = []
        for n in range(n_arr):
            for j, chip in enumerate(chips):
                copy(n, 1 + j, (*chip, c), me).wait_recv()
                fwd = copy(n, 4 + j, (*chip, c), sibling)
                fwd.start()
                passed.append(fwd)
        for n in range(n_arr):
            copy(n, 0, sibling, me).wait_recv()
            for j, chip in enumerate(chips):
                copy(n, 4 + j, (*chip, 1 - c), me).wait_recv()
        for cp in first + passed:
            cp.wait_send()
        for cp in mine:
            cp.wait()

    any_spec = pl.BlockSpec(memory_space=pl.ANY)
    return pl.pallas_call(
        body, name="gather_weights",
        in_specs=[any_spec] * n_arr, out_specs=[any_spec] * n_arr,
        out_shape=[jax.ShapeDtypeStruct((N_DEV, *sh.shape), sh.dtype) for sh in shards],
        scratch_shapes=[pltpu.SemaphoreType.DMA((n_arr, 7)), pltpu.SemaphoreType.DMA((n_arr, 7)),
                        pltpu.SemaphoreType.DMA((n_arr,))],
    )(*shards)


def _peers(x, y, c):
    out = []
    for r in range(1, N_DEV):
        fx, fy, fc = (r >> 2) & 1, (r >> 1) & 1, r & 1
        out.append(((1 - x) if fx else x, (1 - y) if fy else y, (1 - c) if fc else c))
    return out


def _exchange_blocks(grads):
    n_arr = len(grads)

    def body(*refs):
        srcs = refs[:n_arr]
        outs = refs[n_arr:2 * n_arr]
        send_sems, recv_sems, local_sems = refs[2 * n_arr:]
        x, y, c = _my_place()
        me = _block_of(x, y, c)
        peers = _peers(x, y, c)
        mine = [pltpu.make_async_copy(srcs[n].at[me], outs[n].at[me], local_sems.at[n]) for n in range(n_arr)]
        for cp in mine:
            cp.start()
        sends = []
        for n in range(n_arr):
            for r, peer in enumerate(peers):
                sends.append(pltpu.make_async_remote_copy(
                    src_ref=srcs[n].at[_block_of(*peer)], dst_ref=outs[n].at[me], send_sem=send_sems.at[n, r],
                    recv_sem=recv_sems.at[n, r], device_id=peer, device_id_type=MESH))
        for cp in sends:
            cp.start()
        for n in range(n_arr):
            for r, peer in enumerate(peers):
                pltpu.make_async_remote_copy(
                    src_ref=srcs[n].at[me], dst_ref=outs[n].at[_block_of(*peer)], send_sem=send_sems.at[n, r],
                    recv_sem=recv_sems.at[n, r], device_id=peer, device_id_type=MESH).wait_recv()
        for cp in sends:
            cp.wait_send()
        for cp in mine:
            cp.wait()

    any_spec = pl.BlockSpec(memory_space=pl.ANY)
    return pl.pallas_call(
        body, name="exchange_grads",
        in_specs=[any_spec] * n_arr, out_specs=[any_spec] * n_arr,
        out_shape=[jax.ShapeDtypeStruct(g.shape, g.dtype) for g in grads],
        scratch_shapes=[pltpu.SemaphoreType.DMA((n_arr, 7)), pltpu.SemaphoreType.DMA((n_arr, 7)),
                        pltpu.SemaphoreType.DMA((n_arr,))],
    )(*grads)


def _adamw_math(w, g, m, v):
    m = ADAM_B1 * m + (1.0 - ADAM_B1) * g
    v = ADAM_B2 * v + (1.0 - ADAM_B2) * (g * g)
    m_hat = m / (1.0 - ADAM_B1 ** ADAM_STEP)
    v_hat = v / (1.0 - ADAM_B2 ** ADAM_STEP)
    delta = -ADAM_LR * (m_hat / (jnp.sqrt(v_hat) + ADAM_EPS) + ADAM_WD * w)
    return delta, m, v


def _adamw(name, parts, w, m, v, *, grid, part_spec, w_spec):
    n_parts = parts.shape[0]

    def body(p_ref, w_ref, m_ref, v_ref, g_ref, d_ref, nm_ref, nv_ref):
        g = p_ref[0].astype(F32)
        for k in range(1, n_parts):
            g = g + p_ref[k].astype(F32)
        delta, nm, nv = _adamw_math(w_ref[...], g, m_ref[...], v_ref[...])
        g_ref[...] = g
        d_ref[...] = delta
        nm_ref[...] = nm
        nv_ref[...] = nv

    out = jax.ShapeDtypeStruct(w.shape, F32)
    return pl.pallas_call(
        body, name=name, grid=grid, in_specs=[part_spec, w_spec, w_spec, w_spec], out_specs=[w_spec] * 4,
        out_shape=[out] * 4, compiler_params=_params(*(("parallel",) * len(grid))),
    )(parts, w, m, v)


def _small_update(partials, triples, conv_rows):
    d = partials[-1].shape[-1]
    n_rep = len(triples)
    n_part = len(partials)
    rows = []
    for p in partials:
        rows.append(p.shape[0] * (p.shape[1] // d))
    offs = [sum(rows[:i]) for i in range(n_part)]
    total = -(-sum(rows) // 8) * 8

    def body(*refs):
        p_refs = refs[:n_part]
        wmv = refs[n_part:n_part + 3 * n_rep]
        outs = refs[n_part + 3 * n_rep:n_part + 3 * n_rep + 4 * n_rep + (n_part - n_rep)]
        buf, send_sems, recv_sems = refs[-3:]
        x, y, c = _my_place()
        me = _block_of(x, y, c)
        peers = _peers(x, y, c)
        mine = buf.at[me]
        if total > sum(rows):
            mine[sum(rows):total, :] = jnp.zeros((total - sum(rows), d), F32)
        for p_ref, off in zip(p_refs, offs):
            nr, nc = p_ref.shape[0], p_ref.shape[1] // d
            if nc == 1:
                mine[off:off + nr, :] = p_ref[...]
            else:
                for r in range(nr):
                    for q in range(nc):
                        mine[off + r * nc + q:off + r * nc + q + 1, :] = p_ref[r:r + 1, q * d:(q + 1) * d]
        sends =[pltpu.make_async_remote_copy(
            src_ref=buf.at[me], dst_ref=buf.at[me], send_sem=send_sems.at[r], recv_sem=recv_sems.at[r],
            device_id=peer, device_id_type=MESH) for r, peer in enumerate(peers)]
        for cp in sends:
            cp.start()
        for r, peer in enumerate(peers):
            pltpu.make_async_remote_copy(
                src_ref=buf.at[me], dst_ref=buf.at[_block_of(*peer)], send_sem=send_sems.at[r],
                recv_sem=recv_sems.at[r], device_id=peer, device_id_type=MESH).wait_recv()
        for cp in sends:
            cp.wait_send()
        tot = buf[0]
        for k in range(1, N_DEV):
            tot = tot + buf[k]
        buf[0] = tot
        for idx in range(n_part):
            nr, nc = p_refs[idx].shape[0], p_refs[idx].shape[1] // d
            if idx < n_rep:
                w_ref, m_ref, v_ref = wmv[3 * idx:3 * idx + 3]
                g_ref, d_ref, nm_ref, nv_ref = outs[4 * idx:4 * idx + 4]
            else:
                g_ref = outs[4 * n_rep + idx - n_rep]
            pieces = [(slice(0, nr), slice(0, d), offs[idx], nr)] if nc == 1 else [
                (slice(r, r + 1), slice(q * d, (q + 1) * d), offs[idx] + r * nc + q, 1)
                for r in range(nr) for q in range(nc)]
            for rws, cols, row, cnt in pieces:
                g = buf[0, row:row + cnt, :]
                g_ref[rws, cols] = g
                if idx < n_rep:
                    delta, nm, nv = _adamw_math(w_ref[rws, cols], g, m_ref[rws, cols], v_ref[rws, cols])
                    d_ref[rws, cols] = delta
                    nm_ref[rws, cols] = nm
                    nv_ref[rws, cols] = nv

    vm = pl.BlockSpec(memory_space=pltpu.VMEM)
    operands = list(partials)
    for t in triples:
        operands += list(t)
    out_shape = []
    for idx in range(n_rep):
        out_shape += [jax.ShapeDtypeStruct(partials[idx].shape, F32)] * 4
    for idx in range(n_rep, n_part):
        out_shape.append(jax.ShapeDtypeStruct(partials[idx].shape, F32))
    return pl.pallas_call(
        body, name="small_allreduce_adamw", in_specs=[vm] * len(operands), out_specs=[vm] * len(out_shape),
        out_shape=out_shape,
        scratch_shapes=[pltpu.VMEM((N_DEV, total, d), F32), pltpu.SemaphoreType.DMA((7,)), pltpu.SemaphoreType.DMA((7,))],
        compiler_params=pltpu.CompilerParams(vmem_limit_bytes=VMEM_LIMIT_BYTES),
    )(*operands)


def kernel(x, g_mix, w_in, b_in, conv_a, w_out_a, conv_b, conv_b_bias, ln_b_g, ln_b_b, w_out_b, b_out_b, w_pool, pool_scale, w_o, g_mlp, w_mlp1, w_mlp2, g_final, loss_target, m_g_mix, m_w_in, m_b_in, m_conv_a, m_w_out_a, m_conv_b, m_conv_b_bias, m_ln_b_g, m_ln_b_b, m_w_out_b, m_b_out_b, m_w_pool, m_pool_scale, m_w_o, m_g_mlp, m_w_mlp1, m_w_mlp2, m_g_final, v_g_mix, v_w_in, v_b_in, v_conv_a, v_w_out_a, v_conv_b, v_conv_b_bias, v_ln_b_g, v_ln_b_b, v_w_out_b, v_b_out_b, v_w_pool, v_pool_scale, v_w_o, v_g_mlp, v_w_mlp1, v_w_mlp2, v_g_final):
    _, s, d = x.shape
    n_layers = g_mix.shape[0]
    p_in = b_in.shape[1]
    ci = w_in.shape[2]
    c1 = w_mlp1.shape[2]
    rf = w_mlp2.shape[1]
    rd = w_out_a.shape[1]
    f = rf * N_DEV
    rp = rf + 3 * rd
    o_a, o_b, o_o = rf // rd, rf // rd + 1, rf // rd + 2
    gc = d // N_GROUPS
    ca_rows = 8
    tm = min(1024, s)
    tr = min(512, s)
    tx = min(256, s)
    tk = min(1024, s)

    row_pack = jnp.concatenate([w_mlp2, w_out_a, w_out_b, w_o], axis=1).astype(BF16)
    conv_pack = jnp.concatenate(
        [conv_a, jnp.zeros((n_layers, ca_rows - K_A, rd), F32), conv_b], axis=1)
    pool_sh = w_pool.astype(BF16)
    g_in, g_1, g_row, g_pool, g_conv = _gather_shards(
        [w_in.astype(BF16), w_mlp1.astype(BF16), row_pack, pool_sh, conv_pack])
    conv_full = jnp.transpose(g_conv, (1, 2, 0, 3)).reshape(n_layers, ca_rows + K_B, d)
    conv_a_f = conv_full[:, :K_A]
    conv_b_f = conv_full[:, ca_rows:]

    xs = [x[0]]
    saved = []
    row2 = lambda j, i: (i, 0)
    for l in range(n_layers):
        x0 = xs[-1]
        vec = lambda a: a[l:l + 1]
        h = _rms_fwd(f"rms_mix_{l}", x0, vec(g_mix), tr)
        proj = _mm(
            f"proj_{l}", h, g_in, grid=(N_DEV, s // tm), a_spec=pl.BlockSpec((tm, d), row2),
            b_spec=pl.BlockSpec((None, None, d, ci), lambda j, i: (j, l, 0, 0)),
            extras=(vec(b_in),), extra_specs=(pl.BlockSpec((1, ci), lambda j, i: (0, j)),),
            epilogue=lambda v, b: v + b, out_shape=jax.ShapeDtypeStruct((s, p_in), BF16),
            o_spec=pl.BlockSpec((tm, ci), lambda j, i: (i, j)), dims=NN)
        p_a, sw, p_c, cv = _mix_pre_fwd(f"mix_fwd_{l}", proj, conv_a_f[l], conv_b_f[l], vec(conv_b_bias), vec(ln_b_g),
                                        vec(ln_b_b), d, tx)

        def dd_weight(off):
            return pl.BlockSpec((N_DEV, None, rd, d), lambda j, i: (0, l, off, 0))

        y_a = _mm(f"out_a_{l}", p_a, g_row, grid=(1, s // tm), a_spec=pl.BlockSpec((tm, d), row2), b_spec=dd_weight(o_a),
                  out_shape=jax.ShapeDtypeStruct((s, d), BF16), o_spec=pl.BlockSpec((tm, d), row2), dims=NN)
        y_b = _mm(f"out_b_{l}", sw, g_row, grid=(1, s // tm), a_spec=pl.BlockSpec((tm, d), row2), b_spec=dd_weight(o_b),
                  extras=(vec(b_out_b),), extra_specs=(pl.BlockSpec((1, d), lambda j, i: (0, 0)),),
                  epilogue=lambda v, b: v + b,
                  out_shape=jax.ShapeDtypeStruct((s, d), BF16), o_spec=pl.BlockSpec((tm, d), row2), dims=NN)
        pw = _pool_mm(f"pool_{l}", p_c, g_pool, l, d, tm, NN)
        merged = _merge_fwd(f"merge_fwd_{l}", proj, y_a, y_b, pw, vec(pool_scale), d, tr)
        x1 = _mm(f"w_o_{l}", merged, g_row, grid=(1, s // tm), a_spec=pl.BlockSpec((tm, d), row2), b_spec=dd_weight(o_o),
                 extras=(x0,), extra_specs=(pl.BlockSpec((tm, d), row2),), epilogue=lambda v, r: v + r,
                 out_shape=jax.ShapeDtypeStruct((s, d), F32), o_spec=pl.BlockSpec((tm, d), row2), dims=NN)
        h2 = _rms_fwd(f"rms_mlp_{l}", x1, vec(g_mlp), tr)
        a_pre = _mm(f"mlp1_{l}", h2, g_1, grid=(N_DEV, s // tm), a_spec=pl.BlockSpec((tm, d), row2),
                    b_spec=pl.BlockSpec((None, None, d, c1), lambda j, i: (j, l, 0, 0)),
                    out_shape=jax.ShapeDtypeStruct((s, f), BF16), o_spec=pl.BlockSpec((tm, c1), lambda j, i: (i, j)),
                    dims=NN)
        x2 = _mm(f"mlp2_{l}", a_pre, g_row, grid=(s // tr, N_DEV), a_spec=pl.BlockSpec((tr, rf), lambda i, k: (i, k)),
                 b_spec=pl.BlockSpec((None, None, rf, d), lambda i, k: (k, l, 0, 0)), prologue=_relu_sq,
                 extras=(x1,), extra_specs=(pl.BlockSpec((tr, d), lambda i, k: (i, 0)),), epilogue=lambda v, r: v + r,
                 out_shape=jax.ShapeDtypeStruct((s, d), F32), o_spec=pl.BlockSpec((tr, d), lambda i, k: (i, 0)),
                 dims=NN, nk=N_DEV, acc_shape=(tr, d))
        saved.append((x0, h, proj, p_a, sw, p_c, cv, y_a, y_b, pw, merged, x1, h2, a_pre))
        xs.append(x2)

    loss_part, dx, dg_final = _loss_head(xs[-1], g_final.reshape(1, d), loss_target[0], tr)
    loss = lax.psum(loss_part[0, 0], ("x", "y", "c"))

    dg_in = dg_1 = dg_row = dg_pool = None
    small = [None] * n_layers
    for l in reversed(range(n_layers)):
        x0, h, proj, p_a, sw, p_c, cv, y_a, y_b, pw, merged, x1, h2, a_pre = saved[l]
        vec = lambda a: a[l:l + 1]
        row_shape = jax.ShapeDtypeStruct((N_DEV, n_layers, rp, d), BF16)

        def dd_weight(off):
            return pl.BlockSpec((N_DEV, None, rd, d), lambda j, i: (0, l, off, 0))

        def dd_grad(name, a, g, off, alias):
            return _mm(name, a, g, grid=(1, s // tk), a_spec=pl.BlockSpec((tk, d), lambda j, k: (k, 0)),
                       b_spec=pl.BlockSpec((tk, d), lambda j, k: (k, 0)), out_shape=row_shape,
                       o_spec=pl.BlockSpec((N_DEV, None, rd, d), lambda j, k: (0, l, off, 0)), dims=TN, nk=s // tk,
                       acc_shape=(d, d), alias_in=alias)

        d_a = _mm(f"d_act_{l}", dx, g_row, grid=(N_DEV, s // tm), a_spec=pl.BlockSpec((tm, d), row2),
                  b_spec=pl.BlockSpec((None, None, rf, d), lambda j, i: (j, l, 0, 0)),
                  extras=(a_pre,), extra_specs=(pl.BlockSpec((tm, rf), lambda j, i: (i, j)),),
                  epilogue=lambda v, a: v * (2.0 * jnp.maximum(a.astype(F32), 0.0)),
                  out_shape=jax.ShapeDtypeStruct((s, f), BF16), o_spec=pl.BlockSpec((tm, rf), lambda j, i: (i, j)),
                  dims=NT)
        dg_row = _mm(f"dw_mlp2_{l}", a_pre, dx, grid=(N_DEV, s // tk), a_spec=pl.BlockSpec((tk, rf), lambda j, k: (k, j)),
                     b_spec=pl.BlockSpec((tk, d), lambda j, k: (k, 0)), prologue=_relu_sq, out_shape=row_shape,
                     o_spec=pl.BlockSpec((None, None, rf, d), lambda j, k: (j, l, 0, 0)), dims=TN, nk=s // tk,
                     acc_shape=(rf, d), alias_in=dg_row)
        d_h2 = _mm(f"d_h2_{l}", d_a, g_1, grid=(s // tm, N_DEV), a_spec=pl.BlockSpec((tm, c1), lambda i, k: (i, k)),
                   b_spec=pl.BlockSpec((None, None, d, c1), lambda i, k: (k, l, 0, 0)),
                   out_shape=jax.ShapeDtypeStruct((s, d), BF16), o_spec=pl.BlockSpec((tm, d), lambda i, k: (i, 0)),
                   dims=NT, nk=N_DEV, acc_shape=(tm, d))
        dg_1 = _mm(f"dw_mlp1_{l}", h2, d_a, grid=(N_DEV, s // tk), a_spec=pl.BlockSpec((tk, d), lambda j, k: (k, 0)),
                   b_spec=pl.BlockSpec((tk, c1), lambda j, k: (k, j)),
                   out_shape=jax.ShapeDtypeStruct((N_DEV, n_layers, d, c1), BF16),
                   o_spec=pl.BlockSpec((None, None, d, c1), lambda j, k: (j, l, 0, 0)), dims=TN, nk=s // tk,
                   acc_shape=(d, c1), alias_in=dg_1)
        dx, dg_mlp = _rms_bwd(f"rms_mlp_bwd_{l}", d_h2, x1, vec(g_mlp), dx, tr)
        d_merged = _mm(f"d_merged_{l}", dx, g_row, grid=(1, s // tm), a_spec=pl.BlockSpec((tm, d), row2),
                       b_spec=dd_weight(o_o), out_shape=jax.ShapeDtypeStruct((s, d), BF16),
                       o_spec=pl.BlockSpec((tm, d), row2), dims=NT)
        dg_row = dd_grad(f"dw_o_{l}", merged, dx, o_o, dg_row)
        d_ya, d_yb, d_pw, d_gates, d_bout, d_pscale = _merge_bwd(f"merge_bwd_{l}", d_merged, proj, y_a, y_b, pw,
                                                                 vec(pool_scale), d, tr)
        d_pa = _mm(f"d_pa_{l}", d_ya, g_row, grid=(1, s // tm), a_spec=pl.BlockSpec((tm, d), row2), b_spec=dd_weight(o_a),
                   out_shape=jax.ShapeDtypeStruct((s, d), BF16), o_spec=pl.BlockSpec((tm, d), row2), dims=NT)
        d_sw = _mm(f"d_sw_{l}", d_yb, g_row, grid=(1, s // tm), a_spec=pl.BlockSpec((tm, d), row2), b_spec=dd_weight(o_b),
                   out_shape=jax.ShapeDtypeStruct((s, d), BF16), o_spec=pl.BlockSpec((tm, d), row2), dims=NT)
        d_pc = _pool_mm(f"d_pool_{l}", d_pw, g_pool, l, d, tm, NT)
        dg_row = dd_grad(f"dw_out_a_{l}", p_a, d_ya, o_a, dg_row)
        dg_row = dd_grad(f"dw_out_b_{l}", sw, d_yb, o_b, dg_row)
        dg_pool = _pool_wgrad(f"dw_pool_{l}", p_c, d_pw, l, n_layers, d, tk, dg_pool)
        d_proj, d_bin, d_ca, d_cb, d_cbb, d_lng, d_lnb = _mix_pre_bwd(
            f"mix_bwd_{l}", proj, cv, d_pa, d_sw, d_pc, d_gates, conv_a_f[l], conv_b_f[l], vec(ln_b_g), vec(ln_b_b), d, tx)
        d_h = _mm(f"d_h_{l}", d_proj, g_in, grid=(s // tm, N_DEV), a_spec=pl.BlockSpec((tm, ci), lambda i, k: (i, k)),
                  b_spec=pl.BlockSpec((None, None, d, ci), lambda i, k: (k, l, 0, 0)),
                  out_shape=jax.ShapeDtypeStruct((s, d), BF16), o_spec=pl.BlockSpec((tm, d), lambda i, k: (i, 0)),
                  dims=NT, nk=N_DEV, acc_shape=(tm, d))
        dg_in = _mm(f"dw_in_{l}", h, d_proj, grid=(N_DEV, s // tk), a_spec=pl.BlockSpec((tk, d), lambda j, k: (k, 0)),
                    b_spec=pl.BlockSpec((tk, ci), lambda j, k: (k, j)),
                    out_shape=jax.ShapeDtypeStruct((N_DEV, n_layers, d, ci), BF16),
                    o_spec=pl.BlockSpec((None, None, d, ci), lambda j, k: (j, l, 0, 0)), dims=TN, nk=s // tk,
                    acc_shape=(d, ci), alias_in=dg_in)
        dx, dg_mix = _rms_bwd(f"rms_mix_bwd_{l}", d_h, x0, vec(g_mix), dx, tr)
        small[l] = (dg_mix, d_bin, d_cbb, d_lng, d_lnb, d_bout, d_pscale, dg_mlp, d_ca, d_cb)

    grad_x = dx[None]

    names = ("g_mix", "b_in", "conv_b_bias", "ln_b_g", "ln_b_b", "b_out_b", "pool_scale", "g_mlp")
    given = dict(g_mix=(g_mix, m_g_mix, v_g_mix), b_in=(b_in, m_b_in, v_b_in),
                 conv_b_bias=(conv_b_bias, m_conv_b_bias, v_conv_b_bias), ln_b_g=(ln_b_g, m_ln_b_g, v_ln_b_g),
                 ln_b_b=(ln_b_b, m_ln_b_b, v_ln_b_b), b_out_b=(b_out_b, m_b_out_b, v_b_out_b),
                 pool_scale=(pool_scale, m_pool_scale, v_pool_scale), g_mlp=(g_mlp, m_g_mlp, v_g_mlp))
    partials, triples = [], []
    for i, nm in enumerate(names):
        partials.append(jnp.concatenate([small[l][i] for l in range(n_layers)], axis=0))
        triples.append(given[nm])
    partials.append(dg_final)
    triples.append(tuple(a.reshape(1, d) for a in (g_final, m_g_final, v_g_final)))
    partials.append(jnp.concatenate([small[l][8] for l in range(n_layers)], axis=0))
    partials.append(jnp.concatenate([small[l][9] for l in range(n_layers)], axis=0))
    outs = _small_update(partials, triples, 2)
    rep = {nm: outs[4 * i:4 * i + 4] for i, nm in enumerate(names)}
    rep["g_final"] = [a.reshape(d) for a in outs[4 * len(names):4 * len(names) + 4]]
    me = _block_of(*_my_place())
    gca = lax.dynamic_slice_in_dim(outs[-2].reshape(n_layers, K_A, d), me * rd, rd, axis=2)
    gcb = lax.dynamic_slice_in_dim(outs[-1].reshape(n_layers, K_B, d), me * rd, rd, axis=2)

    r_in, r_1, r_row, r_pool = _exchange_blocks([dg_in, dg_1, dg_row, dg_pool])
    tb = min(256, d)
    res = {}
    res["w_in"] = _adamw("adamw_w_in", r_in, w_in, m_w_in, v_w_in, grid=(n_layers, d // tb),
                         part_spec=pl.BlockSpec((N_DEV, None, tb, ci), lambda l, i: (0, l, i, 0)),
                         w_spec=pl.BlockSpec((None, tb, ci), lambda l, i: (l, i, 0)))
    res["w_mlp1"] = _adamw("adamw_w_mlp1", r_1, w_mlp1, m_w_mlp1, v_w_mlp1, grid=(n_layers, d // tb),
                           part_spec=pl.BlockSpec((N_DEV, None, tb, c1), lambda l, i: (0, l, i, 0)),
                           w_spec=pl.BlockSpec((None, tb, c1), lambda l, i: (l, i, 0)))
    tf = min(256, rf)
    res["w_mlp2"] = _adamw("adamw_w_mlp2", r_row, w_mlp2, m_w_mlp2, v_w_mlp2, grid=(n_layers, rf // tf),
                           part_spec=pl.BlockSpec((N_DEV, None, tf, d), lambda l, i: (0, l, i, 0)),
                           w_spec=pl.BlockSpec((None, tf, d), lambda l, i: (l, i, 0)))
    for nm, off, trip in (("w_out_a", o_a, (w_out_a, m_w_out_a, v_w_out_a)),
                          ("w_out_b", o_b, (w_out_b, m_w_out_b, v_w_out_b)), ("w_o", o_o, (w_o, m_w_o, v_w_o))):
        res[nm] = _adamw(f"adamw_{nm}", r_row, *trip, grid=(n_layers,),
                         part_spec=pl.BlockSpec((N_DEV, None, rd, d), functools.partial(lambda l, off: (0, l, off, 0), off=off)),
                         w_spec=pl.BlockSpec((None, rd, d), lambda l: (l, 0, 0)))
    res["w_pool"] = _adamw("adamw_w_pool", r_pool, w_pool, m_w_pool, v_w_pool, grid=(n_layers,),
                           part_spec=pl.BlockSpec((N_DEV, None, N_GROUPS, gc // N_DEV, gc), lambda l: (0, l, 0, 0, 0)),
                           w_spec=pl.BlockSpec((None, N_GROUPS, gc // N_DEV, gc), lambda l: (l, 0, 0, 0)))
    whole3 = lambda: (0, 0, 0)
    res["conv_a"] = _adamw("adamw_conv_a", gca[None], conv_a, m_conv_a, v_conv_a, grid=(),
                           part_spec=pl.BlockSpec((1, n_layers, K_A, rd), lambda: (0, 0, 0, 0)),
                           w_spec=pl.BlockSpec((n_layers, K_A, rd), whole3))
    res["conv_b"] = _adamw("adamw_conv_b", gcb[None], conv_b, m_conv_b, v_conv_b, grid=(),
                           part_spec=pl.BlockSpec((1, n_layers, K_B, rd), lambda: (0, 0, 0, 0)),
                           w_spec=pl.BlockSpec((n_layers, K_B, rd), whole3))
    res.update(rep)

    order = ("g_mix", "w_in", "b_in", "conv_a", "w_out_a", "conv_b", "conv_b_bias", "ln_b_g", "ln_b_b", "w_out_b",
             "b_out_b", "w_pool", "pool_scale", "w_o", "g_mlp", "w_mlp1", "w_mlp2", "g_final")
    out = [loss, grad_x]
    for kind in range(4):
        out += [res[nm][kind] for nm in order]
    return tuple(out)
```

```python
import functools

import jax
import jax.numpy as jnp
from jax import lax
from jax.experimental import pallas as pl
from jax.experimental.pallas import tpu as pltpu

F32 = jnp.float32
BF16 = jnp.bfloat16
MESH = pl.DeviceIdType.MESH

N_DEV = 8
EPS = 1e-6
K_A = 3
K_B = 31
POOL_WINDOWS = (2, 4, 8, 16)
N_GROUPS = len(POOL_WINDOWS)
HALO = 32
CHUNK = 16
SUBLANES = 8
TAP_GROUP = 4
ADAM_LR, ADAM_B1, ADAM_B2, ADAM_EPS, ADAM_WD, ADAM_STEP = 0.001, 0.9, 0.999, 1e-08, 0.01, 10
VMEM_LIMIT_BYTES = 56 * 1024 * 1024

NN = (((1,), (0,)), ((), ()))
NT = (((1,), (1,)), ((), ()))
TN = (((0,), (0,)), ((), ()))


def _params(*sem):
    return pltpu.CompilerParams(dimension_semantics=sem, vmem_limit_bytes=VMEM_LIMIT_BYTES)


def _sigmoid(v):
    return 1.0 / (1.0 + jnp.exp(-v))


def _mm(name, a, b, *, grid, a_spec, b_spec, out_shape, o_spec, dims, nk=1, acc_shape=None,
        extras=(), extra_specs=(), prologue=None, epilogue=None, alias_in=None, slabs=None):
    n_extra = len(extras)
    has_alias = alias_in is not None

    def body(*refs):
        a_ref, b_ref = refs[0], refs[1]
        ex = refs[2:2 + n_extra]
        o_ref = refs[2 + n_extra + (1 if has_alias else 0)]
        av = a_ref[...]
        if prologue is not None:
            av = prologue(av)
        av = av.astype(BF16)

        def finish(val, cols=None):
            if epilogue is not None:
                val = epilogue(val, *[e[...] if cols is None else e[:, cols] for e in ex])
            if cols is None:
                o_ref[...] = val.astype(o_ref.dtype).reshape(o_ref.shape)
            else:
                o_ref[:, cols] = val.astype(o_ref.dtype)

        if slabs == "n":
            for q in range(b_ref.shape[0]):
                pq = lax.dot_general(av, b_ref[q].astype(BF16), dims, preferred_element_type=F32)
                finish(pq, slice(q * pq.shape[1], (q + 1) * pq.shape[1]))
            return
        if slabs == "k":
            kc = av.shape[1] // b_ref.shape[0]
            p = None
            for q in range(b_ref.shape[0]):
                pq = lax.dot_general(av[:, q * kc:(q + 1) * kc], b_ref[q].astype(BF16), dims,
                                     preferred_element_type=F32)
                p = pq if p is None else p + pq
        else:
            bv = b_ref[...]
            bv = bv.reshape((-1, bv.shape[-1])).astype(BF16)
            p = lax.dot_general(av, bv, dims, preferred_element_type=F32)

        if nk == 1:
            finish(p)
        else:
            acc = refs[-1]
            k = pl.program_id(len(grid) - 1)

            @pl.when(k == 0)
            def _():
                acc[...] = p

            @pl.when(k > 0)
            def _():
                acc[...] += p

            @pl.when(k == nk - 1)
            def _():
                finish(acc[...])

    in_specs = [a_spec, b_spec, *extra_specs]
    operands = [a, b, *extras]
    aliases = {}
    if has_alias:
        in_specs.append(pl.BlockSpec(memory_space=pl.ANY))
        operands.append(alias_in)
        aliases = {len(operands) - 1: 0}
    sem = ("parallel",) * (len(grid) - 1) + (("arbitrary",) if nk > 1 else ("parallel",))
    return pl.pallas_call(
        body, name=name, grid=grid, in_specs=in_specs, out_specs=o_spec, out_shape=out_shape,
        scratch_shapes=[pltpu.VMEM(acc_shape, F32)] if nk > 1 else [],
        input_output_aliases=aliases, compiler_params=_params(*sem),
    )(*operands)


def _relu_sq(v):
    r = jnp.maximum(v, 0)
    return r * r


def _rms_fwd(name, x, g, tm):
    s, d = x.shape

    def body(x_ref, g_ref, h_ref):
        xv = x_ref[...]
        r = lax.rsqrt(jnp.mean(xv * xv, axis=-1, keepdims=True) + EPS)
        h_ref[...] = (xv * r * g_ref[...]).astype(h_ref.dtype)

    return pl.pallas_call(
        body, name=name, grid=(s // tm,),
        in_specs=[pl.BlockSpec((tm, d), lambda i: (i, 0)), pl.BlockSpec((1, d), lambda i: (0, 0))],
        out_specs=pl.BlockSpec((tm, d), lambda i: (i, 0)),
        out_shape=jax.ShapeDtypeStruct((s, d), BF16), compiler_params=_params("parallel"),
    )(x, g)


def _colsum8(v):
    return jnp.sum(v.reshape(v.shape[0] // 8, 8, v.shape[1]), axis=0)


def _rms_bwd(name, dh, x, g, dres, tm):
    s, d = x.shape
    n = s // tm

    def body(dh_ref, x_ref, g_ref, dr_ref, dx_ref, dx16_ref, dg_ref, acc):
        i = pl.program_id(0)
        xv = x_ref[...]
        r = lax.rsqrt(jnp.mean(xv * xv, axis=-1, keepdims=True) + EPS)
        xh = xv * r
        dhv = dh_ref[...].astype(F32)
        part = _colsum8(dhv * xh)

        @pl.when(i == 0)
        def _():
            acc[...] = part

        @pl.when(i > 0)
        def _():
            acc[...] += part

        dxh = dhv * g_ref[...]
        dx = r * (dxh - xh * jnp.mean(dxh * xh, axis=-1, keepdims=True))
        dx = dx + dr_ref[...]
        dx_ref[...] = dx
        dx16_ref[...] = dx.astype(BF16)

        @pl.when(i == n - 1)
        def _():
            dg_ref[...] = jnp.sum(acc[...], axis=0, keepdims=True)

    return pl.pallas_call(
        body, name=name, grid=(n,),
        in_specs=[pl.BlockSpec((tm, d), lambda i: (i, 0)), pl.BlockSpec((tm, d), lambda i: (i, 0)),
                  pl.BlockSpec((1, d), lambda i: (0, 0)), pl.BlockSpec((tm, d), lambda i: (i, 0))],
        out_specs=[pl.BlockSpec((tm, d), lambda i: (i, 0)), pl.BlockSpec((tm, d), lambda i: (i, 0)),
                   pl.BlockSpec((1, d), lambda i: (0, 0))],
        out_shape=[jax.ShapeDtypeStruct((s, d), F32), jax.ShapeDtypeStruct((s, d), BF16),
                   jax.ShapeDtypeStruct((1, d), F32)],
        scratch_shapes=[pltpu.VMEM((8, d), F32)], compiler_params=_params("arbitrary"),
    )(dh, x, g, dres)


def _loss_head(x, g, target, tm):
    s, d = x.shape
    n = s // tm

    def body(x_ref, g_ref, t_ref, loss_ref, dx_ref, dx16_ref, dg_ref, acc_l, acc_g):
        i = pl.program_id(0)
        xv = x_ref[...]
        r = lax.rsqrt(jnp.mean(xv * xv, axis=-1, keepdims=True) + EPS)
        xh = xv * r
        err = xh * g_ref[...] - t_ref[...]
        dy = err * (1.0 / d)
        lpart = _colsum8(err * err)
        gpart = _colsum8(dy * xh)

        @pl.when(i == 0)
        def _():
            acc_l[...] = lpart
            acc_g[...] = gpart

        @pl.when(i > 0)
        def _():
            acc_l[...] += lpart
            acc_g[...] += gpart

        dxh = dy * g_ref[...]
        dx = r * (dxh - xh * jnp.mean(dxh * xh, axis=-1, keepdims=True))
        dx_ref[...] = dx
        dx16_ref[...] = dx.astype(BF16)

        @pl.when(i == n - 1)
        def _():
            loss_ref[...] = (0.5 / d) * jnp.sum(jnp.sum(acc_l[...], axis=0, keepdims=True), axis=1, keepdims=True)
            dg_ref[...] = jnp.sum(acc_g[...], axis=0, keepdims=True)

    return pl.pallas_call(
        body, name="loss_head", grid=(n,),
        in_specs=[pl.BlockSpec((tm, d), lambda i: (i, 0)), pl.BlockSpec((1, d), lambda i: (0, 0)),
                  pl.BlockSpec((tm, d), lambda i: (i, 0))],
        out_specs=[pl.BlockSpec((1, 1), lambda i: (0, 0)), pl.BlockSpec((tm, d), lambda i: (i, 0)),
                   pl.BlockSpec((tm, d), lambda i: (i, 0)), pl.BlockSpec((1, d), lambda i: (0, 0))],
        out_shape=[jax.ShapeDtypeStruct((1, 1), F32), jax.ShapeDtypeStruct((s, d), F32),
                   jax.ShapeDtypeStruct((s, d), BF16), jax.ShapeDtypeStruct((1, d), F32)],
        scratch_shapes=[pltpu.VMEM((8, d), F32), pltpu.VMEM((8, d), F32)], compiler_params=_params("arbitrary"),
    )(x, g, target)


def _sec(ref, n, d):
    return ref[:, n * d:(n + 1) * d].astype(F32)


def _fill_shifts(sh, ext):
    rows = ext.shape[0] - SUBLANES
    for b in range(1, SUBLANES):
        sh[b - 1, 0:rows, :] = ext[b:b + rows, :]


def _shifted(sh, ext, off, n):
    b = off % SUBLANES
    if b == 0:
        return ext[off:off + n, :]
    return sh[b - 1, off - b:off - b + n, :]


def _pool_count(row0, rows, window):
    t = row0 + lax.broadcasted_iota(jnp.int32, (rows, 1), 0)
    return jnp.minimum(t + 1, window).astype(F32)


def _mix_pre_fwd(name, proj, conv_a, conv_b, conv_b_bias, ln_g, ln_b, d, tm):
    s = proj.shape[0]
    n = s // tm
    gc = d // N_GROUPS
    hb = tm // HALO

    def body(pj_ref, hp_ref, ca_ref, cb_ref, cbb_ref, lng_ref, lnb_ref, pa_ref, sw_ref, pc_ref, cv_ref,
             eua, eub, euc, sh):
        i = pl.program_id(0)
        keep = (i > 0).astype(F32)
        eua[0:HALO, :] = _sec(hp_ref, 1, d) * _sec(hp_ref, 2, d) * keep
        eub[0:HALO, :] = _sec(hp_ref, 3, d) * _sigmoid(_sec(hp_ref, 4, d)) * keep
        euc[0:HALO, :] = _sec(hp_ref, 5, d) * keep
        eua[HALO:HALO + tm, :] = _sec(pj_ref, 1, d) * _sec(pj_ref, 2, d)
        eub[HALO:HALO + tm, :] = _sec(pj_ref, 3, d) * _sigmoid(_sec(pj_ref, 4, d))
        euc[HALO:HALO + tm, :] = _sec(pj_ref, 5, d)
        _fill_shifts(sh, eub)
        for c in range(tm // CHUNK):
            r0 = c * CHUNK
            z = jnp.zeros((CHUNK, d), F32)
            for k in range(K_A):
                z = z + ca_ref[k:k + 1, :] * eua[HALO + r0 - (K_A - 1) + k:HALO + r0 - (K_A - 1) + k + CHUNK, :]
            pa_ref[r0:r0 + CHUNK, :] = (pj_ref[r0:r0 + CHUNK, 0:d].astype(F32) * z).astype(pa_ref.dtype)
            cv = jnp.zeros((CHUNK, d), F32) + cbb_ref[...]
            for k in range(K_B):
                cv = cv + cb_ref[k:k + 1, :] * _shifted(sh, eub, HALO + r0 - (K_B - 1) + k, CHUNK)
            cv_ref[r0:r0 + CHUNK, :] = cv.astype(cv_ref.dtype)
        cvv = cv_ref[...].astype(F32)
        mu = jnp.mean(cvv, axis=-1, keepdims=True)
        xc = cvv - mu
        xh = xc * lax.rsqrt(jnp.mean(xc * xc, axis=-1, keepdims=True) + EPS)
        ln = xh * lng_ref[...] + lnb_ref[...]
        sw_ref[...] = (ln * _sigmoid(ln)).astype(sw_ref.dtype)
        for gi, w in enumerate(POOL_WINDOWS):
            cols = slice(gi * gc, (gi + 1) * gc)
            tot = euc[HALO:HALO + tm, cols]
            for k in range(1, w):
                tot = tot + euc[HALO - k:HALO - k + tm, cols]
            cnt = _pool_count(i * tm, tm, w)
            pc_ref[:, cols] = (tot / cnt - euc[HALO:HALO + tm, cols]).astype(pc_ref.dtype)

    row = lambda i: (i, 0)
    fixed = lambda i: (0, 0)
    act = jax.ShapeDtypeStruct((s, d), BF16)
    return pl.pallas_call(
        body, name=name, grid=(n,),
        in_specs=[pl.BlockSpec((tm, 6 * d), row),
                  pl.BlockSpec((HALO, 6 * d), lambda i: (jnp.maximum(i * hb - 1, 0), 0)),
                  pl.BlockSpec((K_A, d), fixed), pl.BlockSpec((K_B, d), fixed), pl.BlockSpec((1, d), fixed),
                  pl.BlockSpec((1, d), fixed), pl.BlockSpec((1, d), fixed)],
        out_specs=[pl.BlockSpec((tm, d), row)] * 4,
        out_shape=[act, act, act, act],
        scratch_shapes=[pltpu.VMEM((tm + HALO, d), F32)] * 3 + [pltpu.VMEM((SUBLANES - 1, tm + HALO, d), F32)],
        compiler_params=_params("parallel"),
    )(proj, proj, conv_a, conv_b, conv_b_bias, ln_g, ln_b)


def _merge_fwd(name, proj, ya, yb, pw, scale, d, tm):
    s = proj.shape[0]

    def body(g_ref, ya_ref, yb_ref, pw_ref, sc_ref, o_ref):
        m = _sigmoid(_sec(g_ref, 0, d)) * ya_ref[...].astype(F32)
        m = m + _sigmoid(_sec(g_ref, 1, d)) * yb_ref[...].astype(F32)
        m = m + _sigmoid(_sec(g_ref, 2, d)) * (pw_ref[...].astype(F32) * sc_ref[...])
        o_ref[...] = m.astype(o_ref.dtype)

    row = lambda i: (i, 0)
    return pl.pallas_call(
        body, name=name, grid=(s // tm,),
        in_specs=[pl.BlockSpec((tm, 3 * d), lambda i: (i, 2)), pl.BlockSpec((tm, d), row), pl.BlockSpec((tm, d), row),
                  pl.BlockSpec((tm, d), row), pl.BlockSpec((1, d), lambda i: (0, 0))],
        out_specs=pl.BlockSpec((tm, d), row), out_shape=jax.ShapeDtypeStruct((s, d), BF16),
        compiler_params=_params("parallel"),
    )(proj, ya, yb, pw, scale)


def _merge_bwd(name, dm, proj, ya, yb, pw, scale, d, tm):
    s = proj.shape[0]
    n = s // tm

    def body(dm_ref, g_ref, ya_ref, yb_ref, pw_ref, sc_ref, dya_ref, dyb_ref, dpw_ref, dg_ref, dbo_ref, dsc_ref,
             acc_b, acc_s):
        i = pl.program_id(0)
        dmv = dm_ref[...].astype(F32)
        g0 = _sigmoid(_sec(g_ref, 0, d))
        dya_ref[...] = (dmv * g0).astype(dya_ref.dtype)
        dg_ref[:, 0:d] = (dmv * ya_ref[...].astype(F32) * g0 * (1.0 - g0)).astype(dg_ref.dtype)
        g1 = _sigmoid(_sec(g_ref, 1, d))
        dyb = dmv * g1
        dyb_ref[...] = dyb.astype(dyb_ref.dtype)
        dg_ref[:, d:2 * d] = (dmv * yb_ref[...].astype(F32) * g1 * (1.0 - g1)).astype(dg_ref.dtype)
        g2 = _sigmoid(_sec(g_ref, 2, d))
        pwv = pw_ref[...].astype(F32)
        dyc = dmv * g2
        dpw_ref[...] = (dyc * sc_ref[...]).astype(dpw_ref.dtype)
        dg_ref[:, 2 * d:3 * d] = (dmv * (pwv * sc_ref[...]) * g2 * (1.0 - g2)).astype(dg_ref.dtype)
        pb = _colsum8(dyb)
        ps = _colsum8(dyc * pwv)

        @pl.when(i == 0)
        def _():
            acc_b[...] = pb
            acc_s[...] = ps

        @pl.when(i > 0)
        def _():
            acc_b[...] += pb
            acc_s[...] += ps

        @pl.when(i == n - 1)
        def _():
            dbo_ref[...] = jnp.sum(acc_b[...], axis=0, keepdims=True)
            dsc_ref[...] = jnp.sum(acc_s[...], axis=0, keepdims=True)

    row = lambda i: (i, 0)
    fixed = lambda i: (0, 0)
    act = jax.ShapeDtypeStruct((s, d), BF16)
    vec = jax.ShapeDtypeStruct((1, d), F32)
    return pl.pallas_call(
        body, name=name, grid=(n,),
        in_specs=[pl.BlockSpec((tm, d), row), pl.BlockSpec((tm, 3 * d), lambda i: (i, 2)), pl.BlockSpec((tm, d), row),
                  pl.BlockSpec((tm, d), row), pl.BlockSpec((tm, d), row), pl.BlockSpec((1, d), fixed)],
        out_specs=[pl.BlockSpec((tm, d), row)] * 3 + [pl.BlockSpec((tm, 3 * d), row), pl.BlockSpec((1, d), fixed),
                                                      pl.BlockSpec((1, d), fixed)],
        out_shape=[act, act, act, jax.ShapeDtypeStruct((s, 3 * d), BF16), vec, vec],
        scratch_shapes=[pltpu.VMEM((8, d), F32)] * 2, compiler_params=_params("arbitrary"),
    )(dm, proj, ya, yb, pw, scale)


def _mix_pre_bwd(name, proj, cv, dpa, dsw, dpc, dgates, conv_a, conv_b, ln_g, ln_b, d, tm):
    s = proj.shape[0]
    n = s // tm
    gc = d // N_GROUPS
    hb = tm // HALO
    last_halo = s // HALO - 1
    te = tm + HALO

    def ln_bwd(cvv, dswv, lng, lnb):
        mu = jnp.mean(cvv, axis=-1, keepdims=True)
        xc = cvv - mu
        rstd = lax.rsqrt(jnp.mean(xc * xc, axis=-1, keepdims=True) + EPS)
        xh = xc * rstd
        ln = xh * lng + lnb
        sg = _sigmoid(ln)
        dln = dswv * (sg * (1.0 + ln * (1.0 - sg)))
        dxh = dln * lng
        dcv = rstd * (dxh - jnp.mean(dxh, axis=-1, keepdims=True) - xh * jnp.mean(dxh * xh, axis=-1, keepdims=True))
        return dcv, dln, xh

    def body(pj_ref, hp_ref, hf_ref, cv_ref, cvf_ref, dpa_ref, dpaf_ref, dsw_ref, dswf_ref, dpc_ref, dpcf_ref, dgt_ref,
             ca_ref, cb_ref, lng_ref, lnb_ref,
             dpj_ref, dbin_ref, dca_ref, dcb_ref, dcbb_ref, dlng_ref, dlnb_ref,
             eua, eub, edz, edcv, eq, sh, acc_bin, acc_ca, acc_cb, acc_v):
        i = pl.program_id(0)
        keep_p = (i > 0).astype(F32)
        keep_f = (i < n - 1).astype(F32)

        @pl.when(i == 0)
        def _():
            acc_bin[...] = jnp.zeros_like(acc_bin)
            acc_ca[...] = jnp.zeros_like(acc_ca)
            acc_cb[...] = jnp.zeros_like(acc_cb)
            acc_v[...] = jnp.zeros_like(acc_v)

        eua[0:HALO, :] = _sec(hp_ref, 1, d) * _sec(hp_ref, 2, d) * keep_p
        eub[0:HALO, :] = _sec(hp_ref, 3, d) * _sigmoid(_sec(hp_ref, 4, d)) * keep_p
        eua[HALO:te, :] = _sec(pj_ref, 1, d) * _sec(pj_ref, 2, d)
        eub[HALO:te, :] = _sec(pj_ref, 3, d) * _sigmoid(_sec(pj_ref, 4, d))
        edz[0:tm, :] = dpa_ref[...].astype(F32) * _sec(pj_ref, 0, d)
        edz[tm:te, :] = dpaf_ref[...].astype(F32) * _sec(hf_ref, 0, d) * keep_f
        dcv, dln, xh = ln_bwd(cv_ref[...].astype(F32), dsw_ref[...].astype(F32), lng_ref[...], lnb_ref[...])
        edcv[0:tm, :] = dcv
        acc_v[0:8, :] += _colsum8(dcv)
        acc_v[8:16, :] += _colsum8(dln * xh)
        acc_v[16:24, :] += _colsum8(dln)
        dcvf, _, _ = ln_bwd(cvf_ref[...].astype(F32), dswf_ref[...].astype(F32), lng_ref[...], lnb_ref[...])
        edcv[tm:te, :] = dcvf * keep_f
        for gi, w in enumerate(POOL_WINDOWS):
            cols = slice(gi * gc, (gi + 1) * gc)
            eq[0:tm, cols] = dpc_ref[:, cols].astype(F32) / _pool_count(i * tm, tm, w)
            eq[tm:te, cols] = dpcf_ref[:, cols].astype(F32) / _pool_count((i + 1) * tm, HALO, w) * keep_f

        def put(sec_idx, r0, val):
            dpj_ref[r0:r0 + CHUNK, sec_idx * d:(sec_idx + 1) * d] = val.astype(dpj_ref.dtype)
            acc_bin[:, sec_idx * d:(sec_idx + 1) * d] += _colsum8(val)

        _fill_shifts(sh, edcv)
        for c in range(tm // CHUNK):
            r0 = c * CHUNK
            rows = slice(r0, r0 + CHUNK)
            z = jnp.zeros((CHUNK, d), F32)
            dua = jnp.zeros((CHUNK, d), F32)
            for k in range(K_A):
                z = z + ca_ref[k:k + 1, :] * eua[HALO + r0 - (K_A - 1) + k:HALO + r0 - (K_A - 1) + k + CHUNK, :]
                dua = dua + ca_ref[k:k + 1, :] * edz[r0 + (K_A - 1) - k:r0 + (K_A - 1) - k + CHUNK, :]
            put(0, r0, dpa_ref[rows, :].astype(F32) * z)
            put(1, r0, dua * pj_ref[rows, 2 * d:3 * d].astype(F32))
            put(2, r0, dua * pj_ref[rows, d:2 * d].astype(F32))
            dub = jnp.zeros((CHUNK, d), F32)
            for k in range(K_B):
                dub = dub + cb_ref[k:k + 1, :] * _shifted(sh, edcv, r0 + (K_B - 1) - k, CHUNK)
            bval = pj_ref[rows, 3 * d:4 * d].astype(F32)
            sg = _sigmoid(pj_ref[rows, 4 * d:5 * d].astype(F32))
            put(3, r0, dub * sg)
            put(4, r0, dub * bval * sg * (1.0 - sg))
            for gi, w in enumerate(POOL_WINDOWS):
                cols = slice(gi * gc, (gi + 1) * gc)
                tot = eq[rows, cols]
                for k in range(1, w):
                    tot = tot + eq[r0 + k:r0 + k + CHUNK, cols]
                dci = tot - dpc_ref[rows, cols].astype(F32)
                dpj_ref[rows, 5 * d + gi * gc:5 * d + (gi + 1) * gc] = dci.astype(dpj_ref.dtype)
                acc_bin[:, 5 * d + gi * gc:5 * d + (gi + 1) * gc] += _colsum8(dci)
        for q in range(3):
            gv = dgt_ref[:, q * d:(q + 1) * d]
            dpj_ref[:, (6 + q) * d:(7 + q) * d] = gv
            acc_bin[:, (6 + q) * d:(7 + q) * d] += _colsum8(gv.astype(F32))
        for k in range(K_A):
            a = jnp.zeros((8, d), F32)
            for c in range(tm // CHUNK):
                r0 = c * CHUNK
                a = a + _colsum8(edz[r0:r0 + CHUNK, :] * eua[HALO + r0 - (K_A - 1) + k:HALO + r0 - (K_A - 1) + k + CHUNK, :])
            acc_ca[k] += a
        _fill_shifts(sh, eub)
        for k0 in range(0, K_B, TAP_GROUP):
            taps = range(k0, min(k0 + TAP_GROUP, K_B))
            a = {k: jnp.zeros((8, d), F32) for k in taps}
            for c in range(tm // CHUNK):
                r0 = c * CHUNK
                dc = edcv[r0:r0 + CHUNK, :]
                for k in taps:
                    a[k] = a[k] + _colsum8(dc * _shifted(sh, eub, HALO + r0 - (K_B - 1) + k, CHUNK))
            for k in taps:
                acc_cb[k] += a[k]

        @pl.when(i == n - 1)
        def _():
            dbin_ref[...] = jnp.sum(acc_bin[...], axis=0, keepdims=True)
            for k in range(K_A):
                dca_ref[k:k + 1, :] = jnp.sum(acc_ca[k], axis=0, keepdims=True)
            for k in range(K_B):
                dcb_ref[k:k + 1, :] = jnp.sum(acc_cb[k], axis=0, keepdims=True)
            dcbb_ref[...] = jnp.sum(acc_v[0:8, :], axis=0, keepdims=True)
            dlng_ref[...] = jnp.sum(acc_v[8:16, :], axis=0, keepdims=True)
            dlnb_ref[...] = jnp.sum(acc_v[16:24, :], axis=0, keepdims=True)

    row = lambda i: (i, 0)
    fixed = lambda i: (0, 0)
    past = lambda i: (jnp.maximum(i * hb - 1, 0), 0)
    fut = lambda i: (jnp.minimum((i + 1) * hb, last_halo), 0)
    vec = jax.ShapeDtypeStruct((1, d), F32)
    return pl.pallas_call(
        body, name=name, grid=(n,),
        in_specs=[pl.BlockSpec((tm, 6 * d), row), pl.BlockSpec((HALO, 6 * d), past), pl.BlockSpec((HALO, 6 * d), fut),
                  pl.BlockSpec((tm, d), row), pl.BlockSpec((HALO, d), fut),
                  pl.BlockSpec((tm, d), row), pl.BlockSpec((HALO, d), fut),
                  pl.BlockSpec((tm, d), row), pl.BlockSpec((HALO, d), fut),
                  pl.BlockSpec((tm, d), row), pl.BlockSpec((HALO, d), fut),
                  pl.BlockSpec((tm, 3 * d), row),
                  pl.BlockSpec((K_A, d), fixed), pl.BlockSpec((K_B, d), fixed), pl.BlockSpec((1, d), fixed),
                  pl.BlockSpec((1, d), fixed)],
        out_specs=[pl.BlockSpec((tm, 9 * d), row), pl.BlockSpec((1, 9 * d), fixed), pl.BlockSpec((K_A, d), fixed),
                   pl.BlockSpec((K_B, d), fixed), pl.BlockSpec((1, d), fixed), pl.BlockSpec((1, d), fixed),
                   pl.BlockSpec((1, d), fixed)],
        out_shape=[jax.ShapeDtypeStruct((s, 9 * d), BF16), jax.ShapeDtypeStruct((1, 9 * d), F32),
                   jax.ShapeDtypeStruct((K_A, d), F32), jax.ShapeDtypeStruct((K_B, d), F32), vec, vec, vec],
        scratch_shapes=[pltpu.VMEM((te, d), F32)] * 5 + [pltpu.VMEM((SUBLANES - 1, te, d), F32),
                                                         pltpu.VMEM((8, 9 * d), F32), pltpu.VMEM((K_A, 8, d), F32),
                                                         pltpu.VMEM((K_B, 8, d), F32), pltpu.VMEM((24, d), F32)],
        compiler_params=_params("arbitrary"),
    )(proj, proj, proj, cv, cv, dpa, dpa, dsw, dsw, dpc, dpc, dgates, conv_a, conv_b, ln_g, ln_b)


def _pool_mm(name, a, gpool, l, d, tm, dims):
    s = a.shape[0]
    gc = d // N_GROUPS

    def body(a_ref, w_ref, o_ref):
        for gi in range(N_GROUPS):
            w = w_ref[:, gi].reshape(gc, gc)
            o_ref[:, gi * gc:(gi + 1) * gc] = lax.dot_general(
                a_ref[:, gi * gc:(gi + 1) * gc], w, dims, preferred_element_type=F32).astype(o_ref.dtype)

    return pl.pallas_call(
        body, name=name, grid=(s // tm,),
        in_specs=[pl.BlockSpec((tm, d), lambda i: (i, 0)),
                  pl.BlockSpec((N_DEV, None, N_GROUPS, gc // N_DEV, gc), lambda i: (0, l, 0, 0, 0))],
        out_specs=pl.BlockSpec((tm, d), lambda i: (i, 0)), out_shape=jax.ShapeDtypeStruct((s, d), BF16),
        compiler_params=_params("parallel"),
    )(a, gpool)


def _pool_wgrad(name, p, dpw, l, n_layers, d, tk, alias_in):
    s = p.shape[0]
    gc = d // N_GROUPS
    n = s // tk

    def body(p_ref, g_ref, *rest):
        o_ref, acc = rest[-2], rest[-1]
        k = pl.program_id(0)
        for gi in range(N_GROUPS):
            cols = slice(gi * gc, (gi + 1) * gc)
            part = lax.dot_general(p_ref[:, cols], g_ref[:, cols], TN, preferred_element_type=F32)

            @pl.when(k == 0)
            def _():
                acc[gi] = part

            @pl.when(k > 0)
            def _():
                acc[gi] += part

        @pl.when(k == n - 1)
        def _():
            for gi in range(N_GROUPS):
                o_ref[:, gi] = acc[gi].astype(o_ref.dtype).reshape(N_DEV, gc // N_DEV, gc)

    in_specs = [pl.BlockSpec((tk, d), lambda k: (k, 0)), pl.BlockSpec((tk, d), lambda k: (k, 0))]
    operands = [p, dpw]
    aliases = {}
    if alias_in is not None:
        in_specs.append(pl.BlockSpec(memory_space=pl.ANY))
        operands.append(alias_in)
        aliases = {2: 0}
    return pl.pallas_call(
        body, name=name, grid=(n,), in_specs=in_specs,
        out_specs=pl.BlockSpec((N_DEV, None, N_GROUPS, gc // N_DEV, gc), lambda k: (0, l, 0, 0, 0)),
        out_shape=jax.ShapeDtypeStruct((N_DEV, n_layers, N_GROUPS, gc // N_DEV, gc), BF16),
        scratch_shapes=[pltpu.VMEM((N_GROUPS, gc, gc), F32)], input_output_aliases=aliases,
        compiler_params=_params("arbitrary"),
    )(*operands)


def _my_place():
    x, y, c = lax.axis_index("x"), lax.axis_index("y"), lax.axis_index("c")
    return x, y, c


def _block_of(x, y, c):
    return 4 * x + 2 * y + c


def _gather_shards(shards):
    n_arr = len(shards)

    def body(*refs):
        srcs = refs[:n_arr]
        outs = refs[n_arr:2 * n_arr]
        send_sems, recv_sems, local_sems = refs[2 * n_arr:]
        x, y, c = _my_place()
        me, sibling = (x, y, c), (x, y, 1 - c)
        chips = [(1 - x, y), (x, 1 - y), (1 - x, 1 - y)]

        def copy(n, k, block, to, src=None):
            rows = outs[n].at[_block_of(*block)]
            return pltpu.make_async_remote_copy(
                src_ref=rows if src is None else src, dst_ref=rows, send_sem=send_sems.at[n, k],
                recv_sem=recv_sems.at[n, k], device_id=to, device_id_type=MESH)

        mine = [pltpu.make_async_copy(srcs[n], outs[n].at[_block_of(*me)], local_sems.at[n]) for n in range(n_arr)]
        for cp in mine:
            cp.start()
        first = []
        for n in range(n_arr):
            first.append(copy(n, 0, me, sibling, src=srcs[n]))
            first += [copy(n, 1 + j, me, (*chip, c), src=srcs[n]) for j, chip in enumerate(chips)]
        for cp in first:
            cp.start()
        passed = []
        for n in range(n_arr):
            for j, chip in enumerate(chips):
                copy(n, 1 + j, (*chip, c), me).wait_recv()
                fwd = copy(n, 4 + j, (*chip, c), sibling)
                fwd.start()
                passed.append(fwd)
        for n in range(n_arr):
            copy(n, 0, sibling, me).wait_recv()
            for j, chip in enumerate(chips):
                copy(n, 4 + j, (*chip, 1 - c), me).wait_recv()
        for cp in first + passed:
            cp.wait_send()
        for cp in mine:
            cp.wait()

    any_spec = pl.BlockSpec(memory_space=pl.ANY)
    return pl.pallas_call(
        body, name="gather_weights",
        in_specs=[any_spec] * n_arr, out_specs=[any_spec] * n_arr,
        out_shape=[jax.ShapeDtypeStruct((N_DEV, *sh.shape), sh.dtype) for sh in shards],
        scratch_shapes=[pltpu.SemaphoreType.DMA((n_arr, 7)), pltpu.SemaphoreType.DMA((n_arr, 7)),
                        pltpu.SemaphoreType.DMA((n_arr,))],
    )(*shards)


def _peers(x, y, c):
    out = []
    for r in range(1, N_DEV):
        fx, fy, fc = (r >> 2) & 1, (r >> 1) & 1, r & 1
        out.append(((1 - x) if fx else x, (1 - y) if fy else y, (1 - c) if fc else c))
    return out


def _exchange_blocks(grads):
    n_arr = len(grads)

    def body(*refs):
        srcs = refs[:n_arr]
        outs = refs[n_arr:2 * n_arr]
        send_sems, recv_sems, local_sems = refs[2 * n_arr:]
        x, y, c = _my_place()
        me = _block_of(x, y, c)
        peers = _peers(x, y, c)
        mine = [pltpu.make_async_copy(srcs[n].at[me], outs[n].at[me], local_sems.at[n]) for n in range(n_arr)]
        for cp in mine:
            cp.start()
        sends = []
        for n in range(n_arr):
            for r, peer in enumerate(peers):
                sends.append(pltpu.make_async_remote_copy(
                    src_ref=srcs[n].at[_block_of(*peer)], dst_ref=outs[n].at[me], send_sem=send_sems.at[n, r],
                    recv_sem=recv_sems.at[n, r], device_id=peer, device_id_type=MESH))
        for cp in sends:
            cp.start()
        for n in range(n_arr):
            for r, peer in enumerate(peers):
                pltpu.make_async_remote_copy(
                    src_ref=srcs[n].at[me], dst_ref=outs[n].at[_block_of(*peer)], send_sem=send_sems.at[n, r],
                    recv_sem=recv_sems.at[n, r], device_id=peer, device_id_type=MESH).wait_recv()
        for cp in sends:
            cp.wait_send()
        for cp in mine:
            cp.wait()

    any_spec = pl.BlockSpec(memory_space=pl.ANY)
    return pl.pallas_call(
        body, name="exchange_grads",
        in_specs=[any_spec] * n_arr, out_specs=[any_spec] * n_arr,
        out_shape=[jax.ShapeDtypeStruct(g.shape, g.dtype) for g in grads],
        scratch_shapes=[pltpu.SemaphoreType.DMA((n_arr, 7)), pltpu.SemaphoreType.DMA((n_arr, 7)),
                        pltpu.SemaphoreType.DMA((n_arr,))],
    )(*grads)


def _adamw_math(w, g, m, v):
    m = ADAM_B1 * m + (1.0 - ADAM_B1) * g
    v = ADAM_B2 * v + (1.0 - ADAM_B2) * (g * g)
    m_hat = m / (1.0 - ADAM_B1 ** ADAM_STEP)
    v_hat = v / (1.0 - ADAM_B2 ** ADAM_STEP)
    delta = -ADAM_LR * (m_hat / (jnp.sqrt(v_hat) + ADAM_EPS) + ADAM_WD * w)
    return delta, m, v


def _adamw(name, parts, w, m, v, *, grid, part_spec, w_spec):
    n_parts = parts.shape[0]

    def body(p_ref, w_ref, m_ref, v_ref, g_ref, d_ref, nm_ref, nv_ref):
        g = p_ref[0].astype(F32)
        for k in range(1, n_parts):
            g = g + p_ref[k].astype(F32)
        delta, nm, nv = _adamw_math(w_ref[...], g, m_ref[...], v_ref[...])
        g_ref[...] = g
        d_ref[...] = delta
        nm_ref[...] = nm
        nv_ref[...] = nv

    out = jax.ShapeDtypeStruct(w.shape, F32)
    return pl.pallas_call(
        body, name=name, grid=grid, in_specs=[part_spec, w_spec, w_spec, w_spec], out_specs=[w_spec] * 4,
        out_shape=[out] * 4, compiler_params=_params(*(("parallel",) * len(grid))),
    )(parts, w, m, v)


def _small_update(partials, triples, conv_rows):
    d = partials[-1].shape[-1]
    n_rep = len(triples)
    n_part = len(partials)
    rows = []
    for p in partials:
        rows.append(p.shape[0] * (p.shape[1] // d))
    offs = [sum(rows[:i]) for i in range(n_part)]
    total = -(-sum(rows) // 8) * 8

    def body(*refs):
        p_refs = refs[:n_part]
        wmv = refs[n_part:n_part + 3 * n_rep]
        outs = refs[n_part + 3 * n_rep:n_part + 3 * n_rep + 4 * n_rep + (n_part - n_rep)]
        buf, send_sems, recv_sems = refs[-3:]
        x, y, c = _my_place()
        me = _block_of(x, y, c)
        peers = _peers(x, y, c)
        mine = buf.at[me]
        if total > sum(rows):
            mine[sum(rows):total, :] = jnp.zeros((total - sum(rows), d), F32)
        for p_ref, off in zip(p_refs, offs):
            nr, nc = p_ref.shape[0], p_ref.shape[1] // d
            if nc == 1:
                mine[off:off + nr, :] = p_ref[...]
            else:
                for r in range(nr):
                    for q in range(nc):
                        mine[off + r * nc + q:off + r * nc + q + 1, :] = p_ref[r:r + 1, q * d:(q + 1) * d]
        sends =[pltpu.make_async_remote_copy(
            src_ref=buf.at[me], dst_ref=buf.at[me], send_sem=send_sems.at[r], recv_sem=recv_sems.at[r],
            device_id=peer, device_id_type=MESH) for r, peer in enumerate(peers)]
        for cp in sends:
            cp.start()
        for r, peer in enumerate(peers):
            pltpu.make_async_remote_copy(
                src_ref=buf.at[me], dst_ref=buf.at[_block_of(*peer)], send_sem=send_sems.at[r],
                recv_sem=recv_sems.at[r], device_id=peer, device_id_type=MESH).wait_recv()
        for cp in sends:
            cp.wait_send()
        tot = buf[0]
        for k in range(1, N_DEV):
            tot = tot + buf[k]
        buf[0] = tot
        for idx in range(n_part):
            nr, nc = p_refs[idx].shape[0], p_refs[idx].shape[1] // d
            if idx < n_rep:
                w_ref, m_ref, v_ref = wmv[3 * idx:3 * idx + 3]
                g_ref, d_ref, nm_ref, nv_ref = outs[4 * idx:4 * idx + 4]
            else:
                g_ref = outs[4 * n_rep + idx - n_rep]
            pieces = [(slice(0, nr), slice(0, d), offs[idx], nr)] if nc == 1 else [
                (slice(r, r + 1), slice(q * d, (q + 1) * d), offs[idx] + r * nc + q, 1)
                for r in range(nr) for q in range(nc)]
            for rws, cols, row, cnt in pieces:
                g = buf[0, row:row + cnt, :]
                g_ref[rws, cols] = g
                if idx < n_rep:
                    delta, nm, nv = _adamw_math(w_ref[rws, cols], g, m_ref[rws, cols], v_ref[rws, cols])
                    d_ref[rws, cols] = delta
                    nm_ref[rws, cols] = nm
                    nv_ref[rws, cols] = nv

    vm = pl.BlockSpec(memory_space=pltpu.VMEM)
    operands = list(partials)
    for t in triples:
        operands += list(t)
    out_shape = []
    for idx in range(n_rep):
        out_shape += [jax.ShapeDtypeStruct(partials[idx].shape, F32)] * 4
    for idx in range(n_rep, n_part):
        out_shape.append(jax.ShapeDtypeStruct(partials[idx].shape, F32))
    return pl.pallas_call(
        body, name="small_allreduce_adamw", in_specs=[vm] * len(operands), out_specs=[vm] * len(out_shape),
        out_shape=out_shape,
        scratch_shapes=[pltpu.VMEM((N_DEV, total, d), F32), pltpu.SemaphoreType.DMA((7,)), pltpu.SemaphoreType.DMA((7,))],
        compiler_params=pltpu.CompilerParams(vmem_limit_bytes=VMEM_LIMIT_BYTES),
    )(*operands)


def kernel(x, g_mix, w_in, b_in, conv_a, w_out_a, conv_b, conv_b_bias, ln_b_g, ln_b_b, w_out_b, b_out_b, w_pool, pool_scale, w_o, g_mlp, w_mlp1, w_mlp2, g_final, loss_target, m_g_mix, m_w_in, m_b_in, m_conv_a, m_w_out_a, m_conv_b, m_conv_b_bias, m_ln_b_g, m_ln_b_b, m_w_out_b, m_b_out_b, m_w_pool, m_pool_scale, m_w_o, m_g_mlp, m_w_mlp1, m_w_mlp2, m_g_final, v_g_mix, v_w_in, v_b_in, v_conv_a, v_w_out_a, v_conv_b, v_conv_b_bias, v_ln_b_g, v_ln_b_b, v_w_out_b, v_b_out_b, v_w_pool, v_pool_scale, v_w_o, v_g_mlp, v_w_mlp1, v_w_mlp2, v_g_final):
    _, s, d = x.shape
    n_layers = g_mix.shape[0]
    p_in = b_in.shape[1]
    ci = w_in.shape[2]
    c1 = w_mlp1.shape[2]
    rf = w_mlp2.shape[1]
    rd = w_out_a.shape[1]
    f = rf * N_DEV
    rp = rf + 3 * rd
    o_a, o_b, o_o = rf // rd, rf // rd + 1, rf // rd + 2
    gc = d // N_GROUPS
    ca_rows = 8
    tm = min(1024, s)
    tr = min(512, s)
    tx = min(256, s)
    tk = min(2048, s)
    tk_mlp = min(4096, s)

    row_pack = jnp.concatenate([w_mlp2, w_out_a, w_out_b, w_o], axis=1).astype(BF16)
    conv_pack = jnp.concatenate(
        [conv_a, jnp.zeros((n_layers, ca_rows - K_A, rd), F32), conv_b], axis=1)
    pool_sh = w_pool.astype(BF16)
    g_in, g_1, g_row, g_pool, g_conv = _gather_shards(
        [w_in.astype(BF16), w_mlp1.astype(BF16), row_pack, pool_sh, conv_pack])
    conv_full = jnp.transpose(g_conv, (1, 2, 0, 3)).reshape(n_layers, ca_rows + K_B, d)
    conv_a_f = conv_full[:, :K_A]
    conv_b_f = conv_full[:, ca_rows:]

    xs = [x[0]]
    saved = []
    row2 = lambda j, i: (i, 0)
    for l in range(n_layers):
        x0 = xs[-1]
        vec = lambda a: a[l:l + 1]
        h = _rms_fwd(f"rms_mix_{l}", x0, vec(g_mix), tr)
        proj = _mm(
            f"proj_{l}", h, g_in, grid=(N_DEV, s // tm), a_spec=pl.BlockSpec((tm, d), row2),
            b_spec=pl.BlockSpec((None, None, d, ci), lambda j, i: (j, l, 0, 0)),
            extras=(vec(b_in),), extra_specs=(pl.BlockSpec((1, ci), lambda j, i: (0, j)),),
            epilogue=lambda v, b: v + b, out_shape=jax.ShapeDtypeStruct((s, p_in), BF16),
            o_spec=pl.BlockSpec((tm, ci), lambda j, i: (i, j)), dims=NN)
        p_a, sw, p_c, cv = _mix_pre_fwd(f"mix_fwd_{l}", proj, conv_a_f[l], conv_b_f[l], vec(conv_b_bias), vec(ln_b_g),
                                        vec(ln_b_b), d, tx)

        def dd_weight(off):
            return pl.BlockSpec((N_DEV, None, rd, d), lambda j, i: (0, l, off, 0))

        y_a = _mm(f"out_a_{l}", p_a, g_row, grid=(1, s // tm), a_spec=pl.BlockSpec((tm, d), row2), b_spec=dd_weight(o_a),
                  out_shape=jax.ShapeDtypeStruct((s, d), BF16), o_spec=pl.BlockSpec((tm, d), row2), dims=NN)
        y_b = _mm(f"out_b_{l}", sw, g_row, grid=(1, s // tm), a_spec=pl.BlockSpec((tm, d), row2), b_spec=dd_weight(o_b),
                  extras=(vec(b_out_b),), extra_specs=(pl.BlockSpec((1, d), lambda j, i: (0, 0)),),
                  epilogue=lambda v, b: v + b,
                  out_shape=jax.ShapeDtypeStruct((s, d), BF16), o_spec=pl.BlockSpec((tm, d), row2), dims=NN)
        pw = _pool_mm(f"pool_{l}", p_c, g_pool, l, d, tm, NN)
        merged = _merge_fwd(f"merge_fwd_{l}", proj, y_a, y_b, pw, vec(pool_scale), d, tr)
        x1 = _mm(f"w_o_{l}", merged, g_row, grid=(1, s // tm), a_spec=pl.BlockSpec((tm, d), row2), b_spec=dd_weight(o_o),
                 extras=(x0,), extra_specs=(pl.BlockSpec((tm, d), row2),), epilogue=lambda v, r: v + r,
                 out_shape=jax.ShapeDtypeStruct((s, d), F32), o_spec=pl.BlockSpec((tm, d), row2), dims=NN)
        h2 = _rms_fwd(f"rms_mlp_{l}", x1, vec(g_mlp), tr)
        a_pre = _mm(f"mlp1_{l}", h2, g_1, grid=(N_DEV // 2, s // tm), a_spec=pl.BlockSpec((tm, d), row2),
                    b_spec=pl.BlockSpec((2, None, d, c1), lambda j, i: (j, l, 0, 0)), slabs="n",
                    out_shape=jax.ShapeDtypeStruct((s, f), BF16),
                    o_spec=pl.BlockSpec((tm, 2 * c1), lambda j, i: (i, j)), dims=NN)
        x2 = _mm(f"mlp2_{l}", a_pre, g_row, grid=(1, s // tr), a_spec=pl.BlockSpec((tr, f), row2),
                 b_spec=pl.BlockSpec((N_DEV, None, rf, d), lambda j, i: (0, l, 0, 0)), prologue=_relu_sq,
                 extras=(x1,), extra_specs=(pl.BlockSpec((tr, d), row2),), epilogue=lambda v, r: v + r,
                 out_shape=jax.ShapeDtypeStruct((s, d), F32), o_spec=pl.BlockSpec((tr, d), row2), dims=NN)
        saved.append((x0, h, proj, p_a, sw, p_c, cv, y_a, y_b, pw, merged, x1, h2, a_pre))
        xs.append(x2)

    loss_part, dx, dx16, dg_final = _loss_head(xs[-1], g_final.reshape(1, d), loss_target[0], tr)
    loss = lax.psum(loss_part[0, 0], ("x", "y", "c"))

    dg_in = dg_1 = dg_row = dg_pool = None
    small = [None] * n_layers
    for l in reversed(range(n_layers)):
        x0, h, proj, p_a, sw, p_c, cv, y_a, y_b, pw, merged, x1, h2, a_pre = saved[l]
        vec = lambda a: a[l:l + 1]
        row_shape = jax.ShapeDtypeStruct((N_DEV, n_layers, rp, d), BF16)

        def dd_weight(off):
            return pl.BlockSpec((N_DEV, None, rd, d), lambda j, i: (0, l, off, 0))

        def dd_grad(name, a, g, off, alias):
            return _mm(name, a, g, grid=(1, s // tk), a_spec=pl.BlockSpec((tk, d), lambda j, k: (k, 0)),
                       b_spec=pl.BlockSpec((tk, d), lambda j, k: (k, 0)), out_shape=row_shape,
                       o_spec=pl.BlockSpec((N_DEV, None, rd, d), lambda j, k: (0, l, off, 0)), dims=TN, nk=s // tk,
                       acc_shape=(d, d), alias_in=alias)

        d_a = _mm(f"d_act_{l}", dx16, g_row, grid=(N_DEV // 2, s // tm), a_spec=pl.BlockSpec((tm, d), row2),
                  b_spec=pl.BlockSpec((2, None, rf, d), lambda j, i: (j, l, 0, 0)), slabs="n",
                  extras=(a_pre,), extra_specs=(pl.BlockSpec((tm, 2 * rf), lambda j, i: (i, j)),),
                  epilogue=lambda v, a: v * (2.0 * jnp.maximum(a.astype(F32), 0.0)),
                  out_shape=jax.ShapeDtypeStruct((s, f), BF16),
                  o_spec=pl.BlockSpec((tm, 2 * rf), lambda j, i: (i, j)), dims=NT)
        dg_row = _mm(f"dw_mlp2_{l}", a_pre, dx16, grid=(N_DEV, s // tk_mlp),
                     a_spec=pl.BlockSpec((tk_mlp, rf), lambda j, k: (k, j)),
                     b_spec=pl.BlockSpec((tk_mlp, d), lambda j, k: (k, 0)), prologue=_relu_sq, out_shape=row_shape,
                     o_spec=pl.BlockSpec((None, None, rf, d), lambda j, k: (j, l, 0, 0)), dims=TN, nk=s // tk_mlp,
                     acc_shape=(rf, d), alias_in=dg_row)
        d_h2 = _mm(f"d_h2_{l}", d_a, g_1, grid=(1, s // tr), a_spec=pl.BlockSpec((tr, f), row2),
                   b_spec=pl.BlockSpec((N_DEV, None, d, c1), lambda j, i: (0, l, 0, 0)), slabs="k",
                   out_shape=jax.ShapeDtypeStruct((s, d), BF16), o_spec=pl.BlockSpec((tr, d), row2), dims=NT)
        dg_1 = _mm(f"dw_mlp1_{l}", h2, d_a, grid=(N_DEV, s // tk_mlp),
                   a_spec=pl.BlockSpec((tk_mlp, d), lambda j, k: (k, 0)),
                   b_spec=pl.BlockSpec((tk_mlp, c1), lambda j, k: (k, j)),
                   out_shape=jax.ShapeDtypeStruct((N_DEV, n_layers, d, c1), BF16),
                   o_spec=pl.BlockSpec((None, None, d, c1), lambda j, k: (j, l, 0, 0)), dims=TN, nk=s // tk_mlp,
                   acc_shape=(d, c1), alias_in=dg_1)
        dx, dx16, dg_mlp = _rms_bwd(f"rms_mlp_bwd_{l}", d_h2, x1, vec(g_mlp), dx, tr)
        d_merged = _mm(f"d_merged_{l}", dx16, g_row, grid=(1, s // tm), a_spec=pl.BlockSpec((tm, d), row2),
                       b_spec=dd_weight(o_o), out_shape=jax.ShapeDtypeStruct((s, d), BF16),
                       o_spec=pl.BlockSpec((tm, d), row2), dims=NT)
        dg_row = dd_grad(f"dw_o_{l}", merged, dx16, o_o, dg_row)
        d_ya, d_yb, d_pw, d_gates, d_bout, d_pscale = _merge_bwd(f"merge_bwd_{l}", d_merged, proj, y_a, y_b, pw,
                                                                 vec(pool_scale), d, tr)
        d_pa = _mm(f"d_pa_{l}", d_ya, g_row, grid=(1, s // tm), a_spec=pl.BlockSpec((tm, d), row2), b_spec=dd_weight(o_a),
                   out_shape=jax.ShapeDtypeStruct((s, d), BF16), o_spec=pl.BlockSpec((tm, d), row2), dims=NT)
        d_sw = _mm(f"d_sw_{l}", d_yb, g_row, grid=(1, s // tm), a_spec=pl.BlockSpec((tm, d), row2), b_spec=dd_weight(o_b),
                   out_shape=jax.ShapeDtypeStruct((s, d), BF16), o_spec=pl.BlockSpec((tm, d), row2), dims=NT)
        d_pc = _pool_mm(f"d_pool_{l}", d_pw, g_pool, l, d, tm, NT)
        dg_row = dd_grad(f"dw_out_a_{l}", p_a, d_ya, o_a, dg_row)
        dg_row = dd_grad(f"dw_out_b_{l}", sw, d_yb, o_b, dg_row)
        dg_pool = _pool_wgrad(f"dw_pool_{l}", p_c, d_pw, l, n_layers, d, tk, dg_pool)
        d_proj, d_bin, d_ca, d_cb, d_cbb, d_lng, d_lnb = _mix_pre_bwd(
            f"mix_bwd_{l}", proj, cv, d_pa, d_sw, d_pc, d_gates, conv_a_f[l], conv_b_f[l], vec(ln_b_g), vec(ln_b_b), d, tx)
        d_h = _mm(f"d_h_{l}", d_proj, g_in, grid=(s // tm, N_DEV // 2),
                  a_spec=pl.BlockSpec((tm, 2 * ci), lambda i, k: (i, k)),
                  b_spec=pl.BlockSpec((2, None, d, ci), lambda i, k: (k, l, 0, 0)), slabs="k",
                  out_shape=jax.ShapeDtypeStruct((s, d), BF16), o_spec=pl.BlockSpec((tm, d), lambda i, k: (i, 0)),
                  dims=NT, nk=N_DEV // 2, acc_shape=(tm, d))
        dg_in = _mm(f"dw_in_{l}", h, d_proj, grid=(N_DEV, s // tk), a_spec=pl.BlockSpec((tk, d), lambda j, k: (k, 0)),
                    b_spec=pl.BlockSpec((tk, ci), lambda j, k: (k, j)),
                    out_shape=jax.ShapeDtypeStruct((N_DEV, n_layers, d, ci), BF16),
                    o_spec=pl.BlockSpec((None, None, d, ci), lambda j, k: (j, l, 0, 0)), dims=TN, nk=s // tk,
                    acc_shape=(d, ci), alias_in=dg_in)
        dx, dx16, dg_mix = _rms_bwd(f"rms_mix_bwd_{l}", d_h, x0, vec(g_mix), dx, tr)
        small[l] = (dg_mix, d_bin, d_cbb, d_lng, d_lnb, d_bout, d_pscale, dg_mlp, d_ca, d_cb)

    grad_x = dx[None]

    names = ("g_mix", "b_in", "conv_b_bias", "ln_b_g", "ln_b_b", "b_out_b", "pool_scale", "g_mlp")
    given = dict(g_mix=(g_mix, m_g_mix, v_g_mix), b_in=(b_in, m_b_in, v_b_in),
                 conv_b_bias=(conv_b_bias, m_conv_b_bias, v_conv_b_bias), ln_b_g=(ln_b_g, m_ln_b_g, v_ln_b_g),
                 ln_b_b=(ln_b_b, m_ln_b_b, v_ln_b_b), b_out_b=(b_out_b, m_b_out_b, v_b_out_b),
                 pool_scale=(pool_scale, m_pool_scale, v_pool_scale), g_mlp=(g_mlp, m_g_mlp, v_g_mlp))
    partials, triples = [], []
    for i, nm in enumerate(names):
        partials.append(jnp.concatenate([small[l][i] for l in range(n_layers)], axis=0))
        triples.append(given[nm])
    partials.append(dg_final)
    triples.append(tuple(a.reshape(1, d) for a in (g_final, m_g_final, v_g_final)))
    partials.append(jnp.concatenate([small[l][8] for l in range(n_layers)], axis=0))
    partials.append(jnp.concatenate([small[l][9] for l in range(n_layers)], axis=0))
    outs = _small_update(partials, triples, 2)
    rep = {nm: outs[4 * i:4 * i + 4] for i, nm in enumerate(names)}
    rep["g_final"] = [a.reshape(d) for a in outs[4 * len(names):4 * len(names) + 4]]
    me = _block_of(*_my_place())
    gca = lax.dynamic_slice_in_dim(outs[-2].reshape(n_layers, K_A, d), me * rd, rd, axis=2)
    gcb = lax.dynamic_slice_in_dim(outs[-1].reshape(n_layers, K_B, d), me * rd, rd, axis=2)

    r_in, r_1, r_row, r_pool = _exchange_blocks([dg_in, dg_1, dg_row, dg_pool])
    tb = min(256, d)
    res = {}
    res["w_in"] = _adamw("adamw_w_in", r_in, w_in, m_w_in, v_w_in, grid=(n_layers, d // tb),
                         part_spec=pl.BlockSpec((N_DEV, None, tb, ci), lambda l, i: (0, l, i, 0)),
                         w_spec=pl.BlockSpec((None, tb, ci), lambda l, i: (l, i, 0)))
    res["w_mlp1"] = _adamw("adamw_w_mlp1", r_1, w_mlp1, m_w_mlp1, v_w_mlp1, grid=(n_layers, d // tb),
                           part_spec=pl.BlockSpec((N_DEV, None, tb, c1), lambda l, i: (0, l, i, 0)),
                           w_spec=pl.BlockSpec((None, tb, c1), lambda l, i: (l, i, 0)))
    tf = min(256, rf)
    res["w_mlp2"] = _adamw("adamw_w_mlp2", r_row, w_mlp2, m_w_mlp2, v_w_mlp2, grid=(n_layers, rf // tf),
                           part_spec=pl.BlockSpec((N_DEV, None, tf, d), lambda l, i: (0, l, i, 0)),
                           w_spec=pl.BlockSpec((None, tf, d), lambda l, i: (l, i, 0)))
    for nm, off, trip in (("w_out_a", o_a, (w_out_a, m_w_out_a, v_w_out_a)),
                          ("w_out_b", o_b, (w_out_b, m_w_out_b, v_w_out_b)), ("w_o", o_o, (w_o, m_w_o, v_w_o))):
        res[nm] = _adamw(f"adamw_{nm}", r_row, *trip, grid=(n_layers,),
                         part_spec=pl.BlockSpec((N_DEV, None, rd, d), functools.partial(lambda l, off: (0, l, off, 0), off=off)),
                         w_spec=pl.BlockSpec((None, rd, d), lambda l: (l, 0, 0)))
    res["w_pool"] = _adamw("adamw_w_pool", r_pool, w_pool, m_w_pool, v_w_pool, grid=(n_layers,),
                           part_spec=pl.BlockSpec((N_DEV, None, N_GROUPS, gc // N_DEV, gc), lambda l: (0, l, 0, 0, 0)),
                           w_spec=pl.BlockSpec((None, N_GROUPS, gc // N_DEV, gc), lambda l: (l, 0, 0, 0)))
    whole3 = lambda: (0, 0, 0)
    res["conv_a"] = _adamw("adamw_conv_a", gca[None], conv_a, m_conv_a, v_conv_a, grid=(),
                           part_spec=pl.BlockSpec((1, n_layers, K_A, rd), lambda: (0, 0, 0, 0)),
                           w_spec=pl.BlockSpec((n_layers, K_A, rd), whole3))
    res["conv_b"] = _adamw("adamw_conv_b", gcb[None], conv_b, m_conv_b, v_conv_b, grid=(),
                           part_spec=pl.BlockSpec((1, n_layers, K_B, rd), lambda: (0, 0, 0, 0)),
                           w_spec=pl.BlockSpec((n_layers, K_B, rd), whole3))
    res.update(rep)

    order = ("g_mix", "w_in", "b_in", "conv_a", "w_out_a", "conv_b", "conv_b_bias", "ln_b_g", "ln_b_b", "w_out_b",
             "b_out_b", "w_pool", "pool_scale", "w_o", "g_mlp", "w_mlp1", "w_mlp2", "g_final")
    out = [loss, grad_x]
    for kind in range(4):
        out += [res[nm][kind] for nm in order]
    return tuple(out)
```

```python
import jax
import jax.numpy as jnp
from jax import lax
from jax.experimental import pallas as pl
from jax.experimental.pallas import tpu as pltpu

F32 = jnp.float32
BF16 = jnp.bfloat16
MESH = pl.DeviceIdType.MESH

N_DEV = 8
EPS = 1e-6
K_A = 3
K_B = 31
POOL_WINDOWS = (2, 4, 8, 16)
N_GROUPS = len(POOL_WINDOWS)
HALO = 32
CHUNK = 16
SUBLANES = 8
TAP_GROUP = 4
ADAM_LR, ADAM_B1, ADAM_B2, ADAM_EPS, ADAM_WD, ADAM_STEP = 0.001, 0.9, 0.999, 1e-08, 0.01, 10
VMEM_LIMIT_BYTES = 56 * 1024 * 1024

NN = (((1,), (0,)), ((), ()))
NT = (((1,), (1,)), ((), ()))
TN = (((0,), (0,)), ((), ()))


def _params(*sem):
    return pltpu.CompilerParams(dimension_semantics=sem, vmem_limit_bytes=VMEM_LIMIT_BYTES)


def _sigmoid(v):
    return 1.0 / (1.0 + jnp.exp(-v))


def _mm(name, a, b, *, grid, a_spec, b_spec, out_shape, o_spec, dims, nk=1, acc_shape=None,
        extras=(), extra_specs=(), prologue=None, epilogue=None, alias_in=None, slabs=None):
    n_extra = len(extras)
    has_alias = alias_in is not None

    def body(*refs):
        a_ref, b_ref = refs[0], refs[1]
        ex = refs[2:2 + n_extra]
        o_ref = refs[2 + n_extra + (1 if has_alias else 0)]
        av = a_ref[...]
        if prologue is not None:
            av = prologue(av)
        av = av.astype(BF16)

        def finish(val, cols=None):
            if epilogue is not None:
                val = epilogue(val, *[e[...] if cols is None else e[:, cols] for e in ex])
            if cols is None:
                o_ref[...] = val.astype(o_ref.dtype).reshape(o_ref.shape)
            else:
                o_ref[:, cols] = val.astype(o_ref.dtype)

        if slabs == "n":
            for q in range(b_ref.shape[0]):
                pq = lax.dot_general(av, b_ref[q].astype(BF16), dims, preferred_element_type=F32)
                finish(pq, slice(q * pq.shape[1], (q + 1) * pq.shape[1]))
            return
        if slabs == "k":
            kc = av.shape[1] // b_ref.shape[0]
            p = None
            for q in range(b_ref.shape[0]):
                pq = lax.dot_general(av[:, q * kc:(q + 1) * kc], b_ref[q].astype(BF16), dims,
                                     preferred_element_type=F32)
                p = pq if p is None else p + pq
        else:
            bv = b_ref[...]
            bv = bv.reshape((-1, bv.shape[-1])).astype(BF16)
            p = lax.dot_general(av, bv, dims, preferred_element_type=F32)

        if nk == 1:
            finish(p)
        else:
            acc = refs[-1]
            k = pl.program_id(len(grid) - 1)

            @pl.when(k == 0)
            def _():
                acc[...] = p

            @pl.when(k > 0)
            def _():
                acc[...] += p

            @pl.when(k == nk - 1)
            def _():
                finish(acc[...])

    in_specs = [a_spec, b_spec, *extra_specs]
    operands = [a, b, *extras]
    aliases = {}
    if has_alias:
        in_specs.append(pl.BlockSpec(memory_space=pl.ANY))
        operands.append(alias_in)
        aliases = {len(operands) - 1: 0}
    sem = ("parallel",) * (len(grid) - 1) + (("arbitrary",) if nk > 1 else ("parallel",))
    return pl.pallas_call(
        body, name=name, grid=grid, in_specs=in_specs, out_specs=o_spec, out_shape=out_shape,
        scratch_shapes=[pltpu.VMEM(acc_shape, F32)] if nk > 1 else [],
        input_output_aliases=aliases, compiler_params=_params(*sem),
    )(*operands)


def _relu_sq(v):
    r = jnp.maximum(v, 0)
    return r * r


def _rms_fwd(name, x, g, tm):
    s, d = x.shape

    def body(x_ref, g_ref, h_ref):
        xv = x_ref[...]
        r = lax.rsqrt(jnp.mean(xv * xv, axis=-1, keepdims=True) + EPS)
        h_ref[...] = (xv * r * g_ref[...]).astype(h_ref.dtype)

    return pl.pallas_call(
        body, name=name, grid=(s // tm,),
        in_specs=[pl.BlockSpec((tm, d), lambda i: (i, 0)), pl.BlockSpec((1, d), lambda i: (0, 0))],
        out_specs=pl.BlockSpec((tm, d), lambda i: (i, 0)),
        out_shape=jax.ShapeDtypeStruct((s, d), BF16), compiler_params=_params("parallel"),
    )(x, g)


def _colsum8(v):
    return jnp.sum(v.reshape(v.shape[0] // 8, 8, v.shape[1]), axis=0)


def _rms_bwd(name, dh, x, g, dres, tm):
    s, d = x.shape
    n = s // tm

    def body(dh_ref, x_ref, g_ref, dr_ref, dx_ref, dx16_ref, dg_ref, acc):
        i = pl.program_id(0)
        xv = x_ref[...]
        r = lax.rsqrt(jnp.mean(xv * xv, axis=-1, keepdims=True) + EPS)
        xh = xv * r
        dhv = dh_ref[...].astype(F32)
        part = _colsum8(dhv * xh)

        @pl.when(i == 0)
        def _():
            acc[...] = part

        @pl.when(i > 0)
        def _():
            acc[...] += part

        dxh = dhv * g_ref[...]
        dx = r * (dxh - xh * jnp.mean(dxh * xh, axis=-1, keepdims=True))
        dx = dx + dr_ref[...]
        dx_ref[...] = dx
        dx16_ref[...] = dx.astype(BF16)

        @pl.when(i == n - 1)
        def _():
            dg_ref[...] = jnp.sum(acc[...], axis=0, keepdims=True)

    return pl.pallas_call(
        body, name=name, grid=(n,),
        in_specs=[pl.BlockSpec((tm, d), lambda i: (i, 0)), pl.BlockSpec((tm, d), lambda i: (i, 0)),
                  pl.BlockSpec((1, d), lambda i: (0, 0)), pl.BlockSpec((tm, d), lambda i: (i, 0))],
        out_specs=[pl.BlockSpec((tm, d), lambda i: (i, 0)), pl.BlockSpec((tm, d), lambda i: (i, 0)),
                   pl.BlockSpec((1, d), lambda i: (0, 0))],
        out_shape=[jax.ShapeDtypeStruct((s, d), F32), jax.ShapeDtypeStruct((s, d), BF16),
                   jax.ShapeDtypeStruct((1, d), F32)],
        scratch_shapes=[pltpu.VMEM((8, d), F32)], compiler_params=_params("arbitrary"),
    )(dh, x, g, dres)


def _loss_head(x, g, target, tm):
    s, d = x.shape
    n = s // tm

    def body(x_ref, g_ref, t_ref, loss_ref, dx_ref, dx16_ref, dg_ref, acc_l, acc_g):
        i = pl.program_id(0)
        xv = x_ref[...]
        r = lax.rsqrt(jnp.mean(xv * xv, axis=-1, keepdims=True) + EPS)
        xh = xv * r
        err = xh * g_ref[...] - t_ref[...]
        dy = err * (1.0 / d)
        lpart = _colsum8(err * err)
        gpart = _colsum8(dy * xh)

        @pl.when(i == 0)
        def _():
            acc_l[...] = lpart
            acc_g[...] = gpart

        @pl.when(i > 0)
        def _():
            acc_l[...] += lpart
            acc_g[...] += gpart

        dxh = dy * g_ref[...]
        dx = r * (dxh - xh * jnp.mean(dxh * xh, axis=-1, keepdims=True))
        dx_ref[...] = dx
        dx16_ref[...] = dx.astype(BF16)

        @pl.when(i == n - 1)
        def _():
            loss_ref[...] = (0.5 / d) * jnp.sum(jnp.sum(acc_l[...], axis=0, keepdims=True), axis=1, keepdims=True)
            dg_ref[...] = jnp.sum(acc_g[...], axis=0, keepdims=True)

    return pl.pallas_call(
        body, name="loss_head", grid=(n,),
        in_specs=[pl.BlockSpec((tm, d), lambda i: (i, 0)), pl.BlockSpec((1, d), lambda i: (0, 0)),
                  pl.BlockSpec((tm, d), lambda i: (i, 0))],
        out_specs=[pl.BlockSpec((1, 1), lambda i: (0, 0)), pl.BlockSpec((tm, d), lambda i: (i, 0)),
                   pl.BlockSpec((tm, d), lambda i: (i, 0)), pl.BlockSpec((1, d), lambda i: (0, 0))],
        out_shape=[jax.ShapeDtypeStruct((1, 1), F32), jax.ShapeDtypeStruct((s, d), F32),
                   jax.ShapeDtypeStruct((s, d), BF16), jax.ShapeDtypeStruct((1, d), F32)],
        scratch_shapes=[pltpu.VMEM((8, d), F32), pltpu.VMEM((8, d), F32)], compiler_params=_params("arbitrary"),
    )(x, g, target)


def _sec(ref, n, d):
    return ref[:, n * d:(n + 1) * d].astype(F32)


def _fill_shifts(sh, ext):
    rows = ext.shape[0] - SUBLANES
    for b in range(1, SUBLANES):
        sh[b - 1, 0:rows, :] = ext[b:b + rows, :]


def _shifted(sh, ext, off, n):
    b = off % SUBLANES
    if b == 0:
        return ext[off:off + n, :]
    return sh[b - 1, off - b:off - b + n, :]


def _pool_count(row0, rows, window):
    t = row0 + lax.broadcasted_iota(jnp.int32, (rows, 1), 0)
    return jnp.minimum(t + 1, window).astype(F32)


def _mix_pre_fwd(name, proj, conv_a, conv_b, conv_b_bias, ln_g, ln_b, d, tm):
    s = proj.shape[0]
    n = s // tm
    gc = d // N_GROUPS
    hb = tm // HALO

    def body(pj_ref, hp_ref, ca_ref, cb_ref, cbb_ref, lng_ref, lnb_ref, pa_ref, sw_ref, pc_ref, cv_ref,
             eua, eub, euc, sh):
        i = pl.program_id(0)
        keep = (i > 0).astype(F32)
        eua[0:HALO, :] = _sec(hp_ref, 1, d) * _sec(hp_ref, 2, d) * keep
        eub[0:HALO, :] = _sec(hp_ref, 3, d) * _sigmoid(_sec(hp_ref, 4, d)) * keep
        euc[0:HALO, :] = _sec(hp_ref, 5, d) * keep
        eua[HALO:HALO + tm, :] = _sec(pj_ref, 1, d) * _sec(pj_ref, 2, d)
        eub[HALO:HALO + tm, :] = _sec(pj_ref, 3, d) * _sigmoid(_sec(pj_ref, 4, d))
        euc[HALO:HALO + tm, :] = _sec(pj_ref, 5, d)
        _fill_shifts(sh, eub)
        for c in range(tm // CHUNK):
            r0 = c * CHUNK
            z = jnp.zeros((CHUNK, d), F32)
            for k in range(K_A):
                z = z + ca_ref[k:k + 1, :] * eua[HALO + r0 - (K_A - 1) + k:HALO + r0 - (K_A - 1) + k + CHUNK, :]
            pa_ref[r0:r0 + CHUNK, :] = (pj_ref[r0:r0 + CHUNK, 0:d].astype(F32) * z).astype(pa_ref.dtype)
            cv = jnp.zeros((CHUNK, d), F32) + cbb_ref[...]
            for k in range(K_B):
                cv = cv + cb_ref[k:k + 1, :] * _shifted(sh, eub, HALO + r0 - (K_B - 1) + k, CHUNK)
            cv_ref[r0:r0 + CHUNK, :] = cv.astype(cv_ref.dtype)
        cvv = cv_ref[...].astype(F32)
        mu = jnp.mean(cvv, axis=-1, keepdims=True)
        xc = cvv - mu
        xh = xc * lax.rsqrt(jnp.mean(xc * xc, axis=-1, keepdims=True) + EPS)
        ln = xh * lng_ref[...] + lnb_ref[...]
        sw_ref[...] = (ln * _sigmoid(ln)).astype(sw_ref.dtype)
        for gi, w in enumerate(POOL_WINDOWS):
            cols = slice(gi * gc, (gi + 1) * gc)
            tot = euc[HALO:HALO + tm, cols]
            for k in range(1, w):
                tot = tot + euc[HALO - k:HALO - k + tm, cols]
            cnt = _pool_count(i * tm, tm, w)
            pc_ref[:, cols] = (tot / cnt - euc[HALO:HALO + tm, cols]).astype(pc_ref.dtype)

    row = lambda i: (i, 0)
    fixed = lambda i: (0, 0)
    act = jax.ShapeDtypeStruct((s, d), BF16)
    return pl.pallas_call(
        body, name=name, grid=(n,),
        in_specs=[pl.BlockSpec((tm, 6 * d), row),
                  pl.BlockSpec((HALO, 6 * d), lambda i: (jnp.maximum(i * hb - 1, 0), 0)),
                  pl.BlockSpec((K_A, d), fixed), pl.BlockSpec((K_B, d), fixed), pl.BlockSpec((1, d), fixed),
                  pl.BlockSpec((1, d), fixed), pl.BlockSpec((1, d), fixed)],
        out_specs=[pl.BlockSpec((tm, d), row)] * 4,
        out_shape=[act, act, act, act],
        scratch_shapes=[pltpu.VMEM((tm + HALO, d), F32)] * 3 + [pltpu.VMEM((SUBLANES - 1, tm + HALO, d), F32)],
        compiler_params=_params("parallel"),
    )(proj, proj, conv_a, conv_b, conv_b_bias, ln_g, ln_b)


def _merge_fwd(name, proj, ya, yb, pw, scale, d, tm):
    s = proj.shape[0]

    def body(g_ref, ya_ref, yb_ref, pw_ref, sc_ref, o_ref):
        m = _sigmoid(_sec(g_ref, 0, d)) * ya_ref[...].astype(F32)
        m = m + _sigmoid(_sec(g_ref, 1, d)) * yb_ref[...].astype(F32)
        m = m + _sigmoid(_sec(g_ref, 2, d)) * (pw_ref[...].astype(F32) * sc_ref[...])
        o_ref[...] = m.astype(o_ref.dtype)

    row = lambda i: (i, 0)
    return pl.pallas_call(
        body, name=name, grid=(s // tm,),
        in_specs=[pl.BlockSpec((tm, 3 * d), lambda i: (i, 2)), pl.BlockSpec((tm, d), row), pl.BlockSpec((tm, d), row),
                  pl.BlockSpec((tm, d), row), pl.BlockSpec((1, d), lambda i: (0, 0))],
        out_specs=pl.BlockSpec((tm, d), row), out_shape=jax.ShapeDtypeStruct((s, d), BF16),
        compiler_params=_params("parallel"),
    )(proj, ya, yb, pw, scale)


def _merge_bwd(name, dm, proj, ya, yb, pw, scale, d, tm):
    s = proj.shape[0]
    n = s // tm

    def body(dm_ref, g_ref, ya_ref, yb_ref, pw_ref, sc_ref, dya_ref, dyb_ref, dpw_ref, dg_ref, dbo_ref, dsc_ref,
             acc_b, acc_s):
        i = pl.program_id(0)
        dmv = dm_ref[...].astype(F32)
        g0 = _sigmoid(_sec(g_ref, 0, d))
        dya_ref[...] = (dmv * g0).astype(dya_ref.dtype)
        dg_ref[:, 0:d] = (dmv * ya_ref[...].astype(F32) * g0 * (1.0 - g0)).astype(dg_ref.dtype)
        g1 = _sigmoid(_sec(g_ref, 1, d))
        dyb = dmv * g1
        dyb_ref[...] = dyb.astype(dyb_ref.dtype)
        dg_ref[:, d:2 * d] = (dmv * yb_ref[...].astype(F32) * g1 * (1.0 - g1)).astype(dg_ref.dtype)
        g2 = _sigmoid(_sec(g_ref, 2, d))
        pwv = pw_ref[...].astype(F32)
        dyc = dmv * g2
        dpw_ref[...] = (dyc * sc_ref[...]).astype(dpw_ref.dtype)
        dg_ref[:, 2 * d:3 * d] = (dmv * (pwv * sc_ref[...]) * g2 * (1.0 - g2)).astype(dg_ref.dtype)
        pb = _colsum8(dyb)
        ps = _colsum8(dyc * pwv)

        @pl.when(i == 0)
        def _():
            acc_b[...] = pb
            acc_s[...] = ps

        @pl.when(i > 0)
        def _():
            acc_b[...] += pb
            acc_s[...] += ps

        @pl.when(i == n - 1)
        def _():
            dbo_ref[...] = jnp.sum(acc_b[...], axis=0, keepdims=True)
            dsc_ref[...] = jnp.sum(acc_s[...], axis=0, keepdims=True)

    row = lambda i: (i, 0)
    fixed = lambda i: (0, 0)
    act = jax.ShapeDtypeStruct((s, d), BF16)
    vec = jax.ShapeDtypeStruct((1, d), F32)
    return pl.pallas_call(
        body, name=name, grid=(n,),
        in_specs=[pl.BlockSpec((tm, d), row), pl.BlockSpec((tm, 3 * d), lambda i: (i, 2)), pl.BlockSpec((tm, d), row),
                  pl.BlockSpec((tm, d), row), pl.BlockSpec((tm, d), row), pl.BlockSpec((1, d), fixed)],
        out_specs=[pl.BlockSpec((tm, d), row)] * 3 + [pl.BlockSpec((tm, 3 * d), row), pl.BlockSpec((1, d), fixed),
                                                      pl.BlockSpec((1, d), fixed)],
        out_shape=[act, act, act, jax.ShapeDtypeStruct((s, 3 * d), BF16), vec, vec],
        scratch_shapes=[pltpu.VMEM((8, d), F32)] * 2, compiler_params=_params("arbitrary"),
    )(dm, proj, ya, yb, pw, scale)


def _mix_pre_bwd(name, proj, cv, dpa, dsw, dpc, dgates, conv_a, conv_b, ln_g, ln_b, d, tm):
    s = proj.shape[0]
    n = s // tm
    gc = d // N_GROUPS
    hb = tm // HALO
    last_halo = s // HALO - 1
    te = tm + HALO

    def ln_bwd(cvv, dswv, lng, lnb):
        mu = jnp.mean(cvv, axis=-1, keepdims=True)
        xc = cvv - mu
        rstd = lax.rsqrt(jnp.mean(xc * xc, axis=-1, keepdims=True) + EPS)
        xh = xc * rstd
        ln = xh * lng + lnb
        sg = _sigmoid(ln)
        dln = dswv * (sg * (1.0 + ln * (1.0 - sg)))
        dxh = dln * lng
        dcv = rstd * (dxh - jnp.mean(dxh, axis=-1, keepdims=True) - xh * jnp.mean(dxh * xh, axis=-1, keepdims=True))
        return dcv, dln, xh

    def body(pj_ref, hp_ref, hf_ref, cv_ref, cvf_ref, dpa_ref, dpaf_ref, dsw_ref, dswf_ref, dpc_ref, dpcf_ref, dgt_ref,
             ca_ref, cb_ref, lng_ref, lnb_ref,
             dpj_ref, dbin_ref, dca_ref, dcb_ref, dcbb_ref, dlng_ref, dlnb_ref,
             eua, eub, edz, edcv, eq, sh, acc_bin, acc_ca, acc_cb, acc_v):
        i = pl.program_id(0)
        keep_p = (i > 0).astype(F32)
        keep_f = (i < n - 1).astype(F32)

        @pl.when(i == 0)
        def _():
            acc_bin[...] = jnp.zeros_like(acc_bin)
            acc_ca[...] = jnp.zeros_like(acc_ca)
            acc_cb[...] = jnp.zeros_like(acc_cb)
            acc_v[...] = jnp.zeros_like(acc_v)

        eua[0:HALO, :] = _sec(hp_ref, 1, d) * _sec(hp_ref, 2, d) * keep_p
        eub[0:HALO, :] = _sec(hp_ref, 3, d) * _sigmoid(_sec(hp_ref, 4, d)) * keep_p
        eua[HALO:te, :] = _sec(pj_ref, 1, d) * _sec(pj_ref, 2, d)
        eub[HALO:te, :] = _sec(pj_ref, 3, d) * _sigmoid(_sec(pj_ref, 4, d))
        edz[0:tm, :] = dpa_ref[...].astype(F32) * _sec(pj_ref, 0, d)
        edz[tm:te, :] = dpaf_ref[...].astype(F32) * _sec(hf_ref, 0, d) * keep_f
        dcv, dln, xh = ln_bwd(cv_ref[...].astype(F32), dsw_ref[...].astype(F32), lng_ref[...], lnb_ref[...])
        edcv[0:tm, :] = dcv
        acc_v[0:8, :] += _colsum8(dcv)
        acc_v[8:16, :] += _colsum8(dln * xh)
        acc_v[16:24, :] += _colsum8(dln)
        dcvf, _, _ = ln_bwd(cvf_ref[...].astype(F32), dswf_ref[...].astype(F32), lng_ref[...], lnb_ref[...])
        edcv[tm:te, :] = dcvf * keep_f
        for gi, w in enumerate(POOL_WINDOWS):
            cols = slice(gi * gc, (gi + 1) * gc)
            eq[0:tm, cols] = dpc_ref[:, cols].astype(F32) / _pool_count(i * tm, tm, w)
            eq[tm:te, cols] = dpcf_ref[:, cols].astype(F32) / _pool_count((i + 1) * tm, HALO, w) * keep_f

        def put(sec_idx, r0, val):
            dpj_ref[r0:r0 + CHUNK, sec_idx * d:(sec_idx + 1) * d] = val.astype(dpj_ref.dtype)
            acc_bin[:, sec_idx * d:(sec_idx + 1) * d] += _colsum8(val)

        _fill_shifts(sh, edcv)
        for c in range(tm // CHUNK):
            r0 = c * CHUNK
            rows = slice(r0, r0 + CHUNK)
            z = jnp.zeros((CHUNK, d), F32)
            dua = jnp.zeros((CHUNK, d), F32)
            for k in range(K_A):
                z = z + ca_ref[k:k + 1, :] * eua[HALO + r0 - (K_A - 1) + k:HALO + r0 - (K_A - 1) + k + CHUNK, :]
                dua = dua + ca_ref[k:k + 1, :] * edz[r0 + (K_A - 1) - k:r0 + (K_A - 1) - k + CHUNK, :]
            put(0, r0, dpa_ref[rows, :].astype(F32) * z)
            put(1, r0, dua * pj_ref[rows, 2 * d:3 * d].astype(F32))
            put(2, r0, dua * pj_ref[rows, d:2 * d].astype(F32))
            dub = jnp.zeros((CHUNK, d), F32)
            for k in range(K_B):
                dub = dub + cb_ref[k:k + 1, :] * _shifted(sh, edcv, r0 + (K_B - 1) - k, CHUNK)
            bval = pj_ref[rows, 3 * d:4 * d].astype(F32)
            sg = _sigmoid(pj_ref[rows, 4 * d:5 * d].astype(F32))
            put(3, r0, dub * sg)
            put(4, r0, dub * bval * sg * (1.0 - sg))
            for gi, w in enumerate(POOL_WINDOWS):
                cols = slice(gi * gc, (gi + 1) * gc)
                tot = eq[rows, cols]
                for k in range(1, w):
                    tot = tot + eq[r0 + k:r0 + k + CHUNK, cols]
                dci = tot - dpc_ref[rows, cols].astype(F32)
                dpj_ref[rows, 5 * d + gi * gc:5 * d + (gi + 1) * gc] = dci.astype(dpj_ref.dtype)
                acc_bin[:, 5 * d + gi * gc:5 * d + (gi + 1) * gc] += _colsum8(dci)
        for q in range(3):
            gv = dgt_ref[:, q * d:(q + 1) * d]
            dpj_ref[:, (6 + q) * d:(7 + q) * d] = gv
            acc_bin[:, (6 + q) * d:(7 + q) * d] += _colsum8(gv.astype(F32))
        for k in range(K_A):
            a = jnp.zeros((8, d), F32)
            for c in range(tm // CHUNK):
                r0 = c * CHUNK
                a = a + _colsum8(edz[r0:r0 + CHUNK, :] * eua[HALO + r0 - (K_A - 1) + k:HALO + r0 - (K_A - 1) + k + CHUNK, :])
            acc_ca[k] += a
        _fill_shifts(sh, eub)
        for k0 in range(0, K_B, TAP_GROUP):
            taps = range(k0, min(k0 + TAP_GROUP, K_B))
            a = {k: jnp.zeros((8, d), F32) for k in taps}
            for c in range(tm // CHUNK):
                r0 = c * CHUNK
                dc = edcv[r0:r0 + CHUNK, :]
                for k in taps:
                    a[k] = a[k] + _colsum8(dc * _shifted(sh, eub, HALO + r0 - (K_B - 1) + k, CHUNK))
            for k in taps:
                acc_cb[k] += a[k]

        @pl.when(i == n - 1)
        def _():
            dbin_ref[...] = jnp.sum(acc_bin[...], axis=0, keepdims=True)
            for k in range(K_A):
                dca_ref[k:k + 1, :] = jnp.sum(acc_ca[k], axis=0, keepdims=True)
            for k in range(K_B):
                dcb_ref[k:k + 1, :] = jnp.sum(acc_cb[k], axis=0, keepdims=True)
            dcbb_ref[...] = jnp.sum(acc_v[0:8, :], axis=0, keepdims=True)
            dlng_ref[...] = jnp.sum(acc_v[8:16, :], axis=0, keepdims=True)
            dlnb_ref[...] = jnp.sum(acc_v[16:24, :], axis=0, keepdims=True)

    row = lambda i: (i, 0)
    fixed = lambda i: (0, 0)
    past = lambda i: (jnp.maximum(i * hb - 1, 0), 0)
    fut = lambda i: (jnp.minimum((i + 1) * hb, last_halo), 0)
    vec = jax.ShapeDtypeStruct((1, d), F32)
    return pl.pallas_call(
        body, name=name, grid=(n,),
        in_specs=[pl.BlockSpec((tm, 6 * d), row), pl.BlockSpec((HALO, 6 * d), past), pl.BlockSpec((HALO, 6 * d), fut),
                  pl.BlockSpec((tm, d), row), pl.BlockSpec((HALO, d), fut),
                  pl.BlockSpec((tm, d), row), pl.BlockSpec((HALO, d), fut),
                  pl.BlockSpec((tm, d), row), pl.BlockSpec((HALO, d), fut),
                  pl.BlockSpec((tm, d), row), pl.BlockSpec((HALO, d), fut),
                  pl.BlockSpec((tm, 3 * d), row),
                  pl.BlockSpec((K_A, d), fixed), pl.BlockSpec((K_B, d), fixed), pl.BlockSpec((1, d), fixed),
                  pl.BlockSpec((1, d), fixed)],
        out_specs=[pl.BlockSpec((tm, 9 * d), row), pl.BlockSpec((1, 9 * d), fixed), pl.BlockSpec((K_A, d), fixed),
                   pl.BlockSpec((K_B, d), fixed), pl.BlockSpec((1, d), fixed), pl.BlockSpec((1, d), fixed),
                   pl.BlockSpec((1, d), fixed)],
        out_shape=[jax.ShapeDtypeStruct((s, 9 * d), BF16), jax.ShapeDtypeStruct((1, 9 * d), F32),
                   jax.ShapeDtypeStruct((K_A, d), F32), jax.ShapeDtypeStruct((K_B, d), F32), vec, vec, vec],
        scratch_shapes=[pltpu.VMEM((te, d), F32)] * 5 + [pltpu.VMEM((SUBLANES - 1, te, d), F32),
                                                         pltpu.VMEM((8, 9 * d), F32), pltpu.VMEM((K_A, 8, d), F32),
                                                         pltpu.VMEM((K_B, 8, d), F32), pltpu.VMEM((24, d), F32)],
        compiler_params=_params("arbitrary"),
    )(proj, proj, proj, cv, cv, dpa, dpa, dsw, dsw, dpc, dpc, dgates, conv_a, conv_b, ln_g, ln_b)


def _pool_mm(name, a, gpool, d, tm, dims):
    s = a.shape[0]
    gc = d // N_GROUPS

    def body(a_ref, w_ref, o_ref):
        for gi in range(N_GROUPS):
            w = w_ref[:, gi].reshape(gc, gc)
            o_ref[:, gi * gc:(gi + 1) * gc] = lax.dot_general(
                a_ref[:, gi * gc:(gi + 1) * gc], w, dims, preferred_element_type=F32).astype(o_ref.dtype)

    return pl.pallas_call(
        body, name=name, grid=(s // tm,),
        in_specs=[pl.BlockSpec((tm, d), lambda i: (i, 0)),
                  pl.BlockSpec((N_DEV, N_GROUPS, gc // N_DEV, gc), lambda i: (0, 0, 0, 0))],
        out_specs=pl.BlockSpec((tm, d), lambda i: (i, 0)), out_shape=jax.ShapeDtypeStruct((s, d), BF16),
        compiler_params=_params("parallel"),
    )(a, gpool)


def _pool_wgrad(name, p, dpw, d, tk):
    s = p.shape[0]
    gc = d // N_GROUPS
    n = s // tk

    def body(p_ref, g_ref, o_ref, acc):
        k = pl.program_id(0)
        for gi in range(N_GROUPS):
            cols = slice(gi * gc, (gi + 1) * gc)
            part = lax.dot_general(p_ref[:, cols], g_ref[:, cols], TN, preferred_element_type=F32)

            @pl.when(k == 0)
            def _():
                acc[gi] = part

            @pl.when(k > 0)
            def _():
                acc[gi] += part

        @pl.when(k == n - 1)
        def _():
            for gi in range(N_GROUPS):
                o_ref[:, gi] = acc[gi].astype(o_ref.dtype).reshape(N_DEV, gc // N_DEV, gc)

    return pl.pallas_call(
        body, name=name, grid=(n,),
        in_specs=[pl.BlockSpec((tk, d), lambda k: (k, 0)), pl.BlockSpec((tk, d), lambda k: (k, 0))],
        out_specs=pl.BlockSpec((N_DEV, N_GROUPS, gc // N_DEV, gc), lambda k: (0, 0, 0, 0)),
        out_shape=jax.ShapeDtypeStruct((N_DEV, N_GROUPS, gc // N_DEV, gc), BF16),
        scratch_shapes=[pltpu.VMEM((N_GROUPS, gc, gc), F32)], compiler_params=_params("arbitrary"),
    )(p, dpw)


def _my_place():
    x, y, c = lax.axis_index("x"), lax.axis_index("y"), lax.axis_index("c")
    return x, y, c


def _block_of(x, y, c):
    return 4 * x + 2 * y + c


def _gather_shards(shards):
    n_arr = len(shards)

    def body(*refs):
        srcs = refs[:n_arr]
        outs = refs[n_arr:2 * n_arr]
        send_sems, recv_sems, local_sems = refs[2 * n_arr:]
        x, y, c = _my_place()
        me, sibling = (x, y, c), (x, y, 1 - c)
        chips = [(1 - x, y), (x, 1 - y), (1 - x, 1 - y)]

        def copy(n, k, block, to, src=None):
            rows = outs[n].at[_block_of(*block)]
            return pltpu.make_async_remote_copy(
                src_ref=rows if src is None else src, dst_ref=rows, send_sem=send_sems.at[n, k],
                recv_sem=recv_sems.at[n, k], device_id=to, device_id_type=MESH)

        mine = [pltpu.make_async_copy(srcs[n], outs[n].at[_block_of(*me)], local_sems.at[n]) for n in range(n_arr)]
        for cp in mine:
            cp.start()
        first = []
        for n in range(n_arr):
            first.append(copy(n, 0, me, sibling, src=srcs[n]))
            first += [copy(n, 1 + j, me, (*chip, c), src=srcs[n]) for j, chip in enumerate(chips)]
        for cp in first:
            cp.start()
        passed = []
        for n in range(n_arr):
            for j, chip in enumerate(chips):
                copy(n, 1 + j, (*chip, c), me).wait_recv()
                fwd = copy(n, 4 + j, (*chip, c), sibling)
                fwd.start()
                passed.append(fwd)
        for n in range(n_arr):
            copy(n, 0, sibling, me).wait_recv()
            for j, chip in enumerate(chips):
                copy(n, 4 + j, (*chip, 1 - c), me).wait_recv()
        for cp in first + passed:
            cp.wait_send()
        for cp in mine:
            cp.wait()

    any_spec = pl.BlockSpec(memory_space=pl.ANY)
    return pl.pallas_call(
        body, name="gather_weights",
        in_specs=[any_spec] * n_arr, out_specs=[any_spec] * n_arr,
        out_shape=[jax.ShapeDtypeStruct((N_DEV, *sh.shape), sh.dtype) for sh in shards],
        scratch_shapes=[pltpu.SemaphoreType.DMA((n_arr, 7)), pltpu.SemaphoreType.DMA((n_arr, 7)),
                        pltpu.SemaphoreType.DMA((n_arr,))],
    )(*shards)


def _peers(x, y, c):
    out = []
    for r in range(1, N_DEV):
        fx, fy, fc = (r >> 2) & 1, (r >> 1) & 1, r & 1
        out.append(((1 - x) if fx else x, (1 - y) if fy else y, (1 - c) if fc else c))
    return out


HBM_SPEC = pl.BlockSpec(memory_space=pltpu.HBM)
SEM_SPEC = pl.BlockSpec(memory_space=pltpu.SEMAPHORE)
ANY_SPEC = pl.BlockSpec(memory_space=pl.ANY)
N_PEERS = N_DEV - 1


def _peer_copy(src_ref, land_ref, send_sems, recv_sems, i, r, peer, me, blockwise):
    src = src_ref.at[_block_of(*peer)] if blockwise else src_ref
    return pltpu.make_async_remote_copy(
        src_ref=src, dst_ref=land_ref.at[me], send_sem=send_sems.at[i * N_PEERS + r],
        recv_sem=recv_sems.at[i * N_PEERS + r], device_id=peer, device_id_type=MESH)


def _start_copies(name, srcs, after, blockwise):
    n = len(srcs)

    def body(*refs):
        s_in, l_in = refs[:n], refs[n:2 * n]
        send_sems, recv_sems = refs[2 * n + 1], refs[2 * n + 2]
        token = refs[-1]
        x, y, c = _my_place()
        me = _block_of(x, y, c)
        for i in range(n):
            for r, peer in enumerate(_peers(x, y, c)):
                _peer_copy(s_in[i], l_in[i], send_sems, recv_sems, i, r, peer, me, blockwise).start()
        token[...] = jnp.zeros_like(token)

    land_shapes = [s.shape if blockwise else (N_DEV, *s.shape) for s in srcs]
    lands = [pltpu.with_memory_space_constraint(lax.empty(sh, s.dtype), pltpu.HBM) for sh, s in zip(land_shapes, srcs)]
    ins = [pltpu.with_memory_space_constraint(s, pltpu.HBM) for s in srcs]
    out = pl.pallas_call(
        body, name=name,
        out_shape=(pltpu.SemaphoreType.DMA((n * N_PEERS,)), pltpu.SemaphoreType.DMA((n * N_PEERS,)),
                   *[pltpu.HBM(s.shape, s.dtype) for s in srcs],
                   *[pltpu.HBM(sh, s.dtype) for sh, s in zip(land_shapes, srcs)],
                   jax.ShapeDtypeStruct((8, 128), F32)),
        in_specs=[HBM_SPEC] * (2 * n) + [ANY_SPEC],
        out_specs=(SEM_SPEC, SEM_SPEC, *[HBM_SPEC] * (2 * n), pl.BlockSpec(memory_space=pltpu.VMEM)),
        input_output_aliases={i: 2 + i for i in range(2 * n)},
        compiler_params=pltpu.CompilerParams(has_side_effects=pltpu.SideEffectType.DATAFLOW_SIDE_EFFECTING),
    )(*ins, *lands, after)
    return dict(send=out[0], recv=out[1], srcs=list(out[2:2 + n]), lands=list(out[2 + n:2 + 2 * n]), token=out[-1])


def _wait_copies(name, state, after, blockwise):
    n = len(state["srcs"])

    def body(*refs):
        s_in, l_in = refs[:n], refs[n:2 * n]
        send_sems, recv_sems = refs[2 * n], refs[2 * n + 1]
        x, y, c = _my_place()
        me = _block_of(x, y, c)
        for i in range(n):
            for r, peer in enumerate(_peers(x, y, c)):
                cp = _peer_copy(s_in[i], l_in[i], send_sems, recv_sems, i, r, peer, me, blockwise)
                cp.wait_send()
                cp.wait_recv()

    both = state["srcs"] + state["lands"]
    out = pl.pallas_call(
        body, name=name, out_shape=tuple(pltpu.HBM(a.shape, a.dtype) for a in both),
        in_specs=[HBM_SPEC] * (2 * n) + [SEM_SPEC, SEM_SPEC, ANY_SPEC], out_specs=tuple([HBM_SPEC] * (2 * n)),
        input_output_aliases={i: i for i in range(2 * n)},
        compiler_params=pltpu.CompilerParams(has_side_effects=pltpu.SideEffectType.DATAFLOW_SIDE_EFFECTING),
    )(*both, state["send"], state["recv"], after)
    return list(out[:n]), list(out[n:])


def _place_own(name, lands, srcs, blockwise):
    n = len(lands)

    def body(*refs):
        s_in, l_out, sems = refs[n:2 * n], refs[2 * n:3 * n], refs[-1]
        me = _block_of(*_my_place())
        cps = [pltpu.make_async_copy(s_in[i].at[me] if blockwise else s_in[i], l_out[i].at[me], sems.at[i])
               for i in range(n)]
        for cp in cps:
            cp.start()
        for cp in cps:
            cp.wait()

    return pl.pallas_call(
        body, name=name, in_specs=[ANY_SPEC] * (2 * n), out_specs=[ANY_SPEC] * n,
        out_shape=[jax.ShapeDtypeStruct(a.shape, a.dtype) for a in lands],
        input_output_aliases={i: i for i in range(n)}, scratch_shapes=[pltpu.SemaphoreType.DMA((n,))],
    )(*lands, *srcs)


def _adamw_math(w, g, m, v):
    m = ADAM_B1 * m + (1.0 - ADAM_B1) * g
    v = ADAM_B2 * v + (1.0 - ADAM_B2) * (g * g)
    m_hat = m / (1.0 - ADAM_B1 ** ADAM_STEP)
    v_hat = v / (1.0 - ADAM_B2 ** ADAM_STEP)
    delta = -ADAM_LR * (m_hat / (jnp.sqrt(v_hat) + ADAM_EPS) + ADAM_WD * w)
    return delta, m, v


def _adamw(name, parts, w, m, v, *, grid, part_specs, w_spec):
    n_layers = len(parts)
    n_parts = parts[0].shape[0]

    def body(*refs):
        p_refs = refs[:n_layers]
        w_ref, m_ref, v_ref, g_ref, d_ref, nm_ref, nv_ref = refs[n_layers:]

        def total(p_ref):
            t = p_ref[0].astype(F32)
            for k in range(1, n_parts):
                t = t + p_ref[k].astype(F32)
            return t

        g = total(p_refs[0])
        for li in range(1, n_layers):
            g = jnp.where(pl.program_id(0) == li, total(p_refs[li]), g)
        delta, nm, nv = _adamw_math(w_ref[...], g, m_ref[...], v_ref[...])
        g_ref[...] = g
        d_ref[...] = delta
        nm_ref[...] = nm
        nv_ref[...] = nv

    out = jax.ShapeDtypeStruct(w.shape, F32)
    return pl.pallas_call(
        body, name=name, grid=grid, in_specs=[*part_specs, w_spec, w_spec, w_spec], out_specs=[w_spec] * 4,
        out_shape=[out] * 4, compiler_params=_params(*(("parallel",) * len(grid))),
    )(*parts, w, m, v)


def _layer_part_spec(layer, block, n_blocks, row_off=0):
    def index_map(l, i):
        ii = jnp.where(l == layer, i, jnp.where(l < layer, 0, n_blocks - 1))
        return (0, row_off + ii) + (0,) * (len(block) - 2)
    return pl.BlockSpec(block, index_map)


def _small_update(partials, triples, conv_rows):
    d = partials[-1].shape[-1]
    n_rep = len(triples)
    n_part = len(partials)
    rows = []
    for p in partials:
        rows.append(p.shape[0] * (p.shape[1] // d))
    offs = [sum(rows[:i]) for i in range(n_part)]
    total = -(-sum(rows) // 8) * 8

    def body(*refs):
        p_refs = refs[:n_part]
        wmv = refs[n_part:n_part + 3 * n_rep]
        outs = refs[n_part + 3 * n_rep:n_part + 3 * n_rep + 4 * n_rep + (n_part - n_rep)]
        buf, send_sems, recv_sems = refs[-3:]
        x, y, c = _my_place()
        me = _block_of(x, y, c)
        peers = _peers(x, y, c)
        mine = buf.at[me]
        if total > sum(rows):
            mine[sum(rows):total, :] = jnp.zeros((total - sum(rows), d), F32)
        for p_ref, off in zip(p_refs, offs):
            nr, nc = p_ref.shape[0], p_ref.shape[1] // d
            if nc == 1:
                mine[off:off + nr, :] = p_ref[...]
            else:
                for r in range(nr):
                    for q in range(nc):
                        mine[off + r * nc + q:off + r * nc + q + 1, :] = p_ref[r:r + 1, q * d:(q + 1) * d]
        sends =[pltpu.make_async_remote_copy(
            src_ref=buf.at[me], dst_ref=buf.at[me], send_sem=send_sems.at[r], recv_sem=recv_sems.at[r],
            device_id=peer, device_id_type=MESH) for r, peer in enumerate(peers)]
        for cp in sends:
            cp.start()
        for r, peer in enumerate(peers):
            pltpu.make_async_remote_copy(
                src_ref=buf.at[me], dst_ref=buf.at[_block_of(*peer)], send_sem=send_sems.at[r],
                recv_sem=recv_sems.at[r], device_id=peer, device_id_type=MESH).wait_recv()
        for cp in sends:
            cp.wait_send()
        tot = buf[0]
        for k in range(1, N_DEV):
            tot = tot + buf[k]
        buf[0] = tot
        for idx in range(n_part):
            nr, nc = p_refs[idx].shape[0], p_refs[idx].shape[1] // d
            if idx < n_rep:
                w_ref, m_ref, v_ref = wmv[3 * idx:3 * idx + 3]
                g_ref, d_ref, nm_ref, nv_ref = outs[4 * idx:4 * idx + 4]
            else:
                g_ref = outs[4 * n_rep + idx - n_rep]
            pieces = [(slice(0, nr), slice(0, d), offs[idx], nr)] if nc == 1 else [
                (slice(r, r + 1), slice(q * d, (q + 1) * d), offs[idx] + r * nc + q, 1)
                for r in range(nr) for q in range(nc)]
            for rws, cols, row, cnt in pieces:
                g = buf[0, row:row + cnt, :]
                g_ref[rws, cols] = g
                if idx < n_rep:
                    delta, nm, nv = _adamw_math(w_ref[rws, cols], g, m_ref[rws, cols], v_ref[rws, cols])
                    d_ref[rws, cols] = delta
                    nm_ref[rws, cols] = nm
                    nv_ref[rws, cols] = nv

    vm = pl.BlockSpec(memory_space=pltpu.VMEM)
    operands = list(partials)
    for t in triples:
        operands += list(t)
    out_shape = []
    for idx in range(n_rep):
        out_shape += [jax.ShapeDtypeStruct(partials[idx].shape, F32)] * 4
    for idx in range(n_rep, n_part):
        out_shape.append(jax.ShapeDtypeStruct(partials[idx].shape, F32))
    return pl.pallas_call(
        body, name="small_allreduce_adamw", in_specs=[vm] * len(operands), out_specs=[vm] * len(out_shape),
        out_shape=out_shape,
        scratch_shapes=[pltpu.VMEM((N_DEV, total, d), F32), pltpu.SemaphoreType.DMA((7,)), pltpu.SemaphoreType.DMA((7,))],
        compiler_params=pltpu.CompilerParams(vmem_limit_bytes=VMEM_LIMIT_BYTES),
    )(*operands)


def kernel(x, g_mix, w_in, b_in, conv_a, w_out_a, conv_b, conv_b_bias, ln_b_g, ln_b_b, w_out_b, b_out_b, w_pool, pool_scale, w_o, g_mlp, w_mlp1, w_mlp2, g_final, loss_target, m_g_mix, m_w_in, m_b_in, m_conv_a, m_w_out_a, m_conv_b, m_conv_b_bias, m_ln_b_g, m_ln_b_b, m_w_out_b, m_b_out_b, m_w_pool, m_pool_scale, m_w_o, m_g_mlp, m_w_mlp1, m_w_mlp2, m_g_final, v_g_mix, v_w_in, v_b_in, v_conv_a, v_w_out_a, v_conv_b, v_conv_b_bias, v_ln_b_g, v_ln_b_b, v_w_out_b, v_b_out_b, v_w_pool, v_pool_scale, v_w_o, v_g_mlp, v_w_mlp1, v_w_mlp2, v_g_final):
    _, s, d = x.shape
    n_layers = g_mix.shape[0]
    p_in = b_in.shape[1]
    ci = w_in.shape[2]
    c1 = w_mlp1.shape[2]
    rf = w_mlp2.shape[1]
    rd = w_out_a.shape[1]
    f = rf * N_DEV
    rp = rf + 3 * rd
    o_a, o_b, o_o = rf // rd, rf // rd + 1, rf // rd + 2
    gc = d // N_GROUPS
    ca_rows = 8
    tm = min(1024, s)
    tr = min(512, s)
    tx = min(256, s)
    tk = min(2048, s)
    tk_mlp = min(4096, s)

    def layer_shards(l):
        row_pack = jnp.concatenate([w_mlp2[l], w_out_a[l], w_out_b[l], w_o[l]], axis=0).astype(BF16)
        return [w_in[l].astype(BF16), w_mlp1[l].astype(BF16), row_pack, w_pool[l].astype(BF16)]

    conv_pack = jnp.concatenate(
        [conv_a, jnp.zeros((n_layers, ca_rows - K_A, rd), F32), conv_b], axis=1)
    *first_layer, g_conv = _gather_shards([*layer_shards(0), conv_pack])
    weights = [first_layer]
    conv_full = jnp.transpose(g_conv, (1, 2, 0, 3)).reshape(n_layers, ca_rows + K_B, d)
    conv_a_f = conv_full[:, :K_A]
    conv_b_f = conv_full[:, ca_rows:]
    in_flight = [_start_copies(f"gather_start_{l}", layer_shards(l), g_conv, blockwise=False)
                 for l in range(1, n_layers)]
    token = in_flight[0]["token"][0:1, 0:1] if in_flight else jnp.zeros((1, 1), F32)
    for st in in_flight[1:]:
        token = token + st["token"][0:1, 0:1]

    xs = [x[0]]
    saved = []
    row2 = lambda j, i: (i, 0)
    for l in range(n_layers):
        x0 = xs[-1]
        vec = lambda a: a[l:l + 1]
        if l > 0:
            srcs, lands = _wait_copies(f"gather_wait_{l}", in_flight[l - 1], x0, blockwise=False)
            weights.append(_place_own(f"gather_own_{l}", lands, srcs, blockwise=False))
        g_in, g_1, g_row, g_pool = weights[l]
        h = _rms_fwd(f"rms_mix_{l}", x0, vec(g_mix) + token if l == 0 else vec(g_mix), tr)
        proj = _mm(
            f"proj_{l}", h, g_in, grid=(N_DEV, s // tm), a_spec=pl.BlockSpec((tm, d), row2),
            b_spec=pl.BlockSpec((None, d, ci), lambda j, i: (j, 0, 0)),
            extras=(vec(b_in),), extra_specs=(pl.BlockSpec((1, ci), lambda j, i: (0, j)),),
            epilogue=lambda v, b: v + b, out_shape=jax.ShapeDtypeStruct((s, p_in), BF16),
            o_spec=pl.BlockSpec((tm, ci), lambda j, i: (i, j)), dims=NN)
        p_a, sw, p_c, cv = _mix_pre_fwd(f"mix_fwd_{l}", proj, conv_a_f[l], conv_b_f[l], vec(conv_b_bias), vec(ln_b_g),
                                        vec(ln_b_b), d, tx)

        def dd_weight(off):
            return pl.BlockSpec((N_DEV, rd, d), lambda j, i: (0, off, 0))

        y_a = _mm(f"out_a_{l}", p_a, g_row, grid=(1, s // tm), a_spec=pl.BlockSpec((tm, d), row2), b_spec=dd_weight(o_a),
                  out_shape=jax.ShapeDtypeStruct((s, d), BF16), o_spec=pl.BlockSpec((tm, d), row2), dims=NN)
        y_b = _mm(f"out_b_{l}", sw, g_row, grid=(1, s // tm), a_spec=pl.BlockSpec((tm, d), row2), b_spec=dd_weight(o_b),
                  extras=(vec(b_out_b),), extra_specs=(pl.BlockSpec((1, d), lambda j, i: (0, 0)),),
                  epilogue=lambda v, b: v + b,
                  out_shape=jax.ShapeDtypeStruct((s, d), BF16), o_spec=pl.BlockSpec((tm, d), row2), dims=NN)
        pw = _pool_mm(f"pool_{l}", p_c, g_pool, d, tm, NN)
        merged = _merge_fwd(f"merge_fwd_{l}", proj, y_a, y_b, pw, vec(pool_scale), d, tr)
        x1 = _mm(f"w_o_{l}", merged, g_row, grid=(1, s // tm), a_spec=pl.BlockSpec((tm, d), row2), b_spec=dd_weight(o_o),
                 extras=(x0,), extra_specs=(pl.BlockSpec((tm, d), row2),), epilogue=lambda v, r: v + r,
                 out_shape=jax.ShapeDtypeStruct((s, d), F32), o_spec=pl.BlockSpec((tm, d), row2), dims=NN)
        h2 = _rms_fwd(f"rms_mlp_{l}", x1, vec(g_mlp), tr)
        a_pre = _mm(f"mlp1_{l}", h2, g_1, grid=(N_DEV // 2, s // tm), a_spec=pl.BlockSpec((tm, d), row2),
                    b_spec=pl.BlockSpec((2, d, c1), lambda j, i: (j, 0, 0)), slabs="n",
                    out_shape=jax.ShapeDtypeStruct((s, f), BF16),
                    o_spec=pl.BlockSpec((tm, 2 * c1), lambda j, i: (i, j)), dims=NN)
        x2 = _mm(f"mlp2_{l}", a_pre, g_row, grid=(1, s // tr), a_spec=pl.BlockSpec((tr, f), row2),
                 b_spec=pl.BlockSpec((N_DEV, rf, d), lambda j, i: (0, 0, 0)), prologue=_relu_sq,
                 extras=(x1,), extra_specs=(pl.BlockSpec((tr, d), row2),), epilogue=lambda v, r: v + r,
                 out_shape=jax.ShapeDtypeStruct((s, d), F32), o_spec=pl.BlockSpec((tr, d), row2), dims=NN)
        saved.append((x0, h, proj, p_a, sw, p_c, cv, y_a, y_b, pw, merged, x1, h2, a_pre))
        xs.append(x2)

    loss_part, dx, dx16, dg_final = _loss_head(xs[-1], g_final.reshape(1, d), loss_target[0], tr)
    loss = lax.psum(loss_part[0, 0], ("x", "y", "c"))

    small = [None] * n_layers
    exchanges = [None] * n_layers
    for l in reversed(range(n_layers)):
        x0, h, proj, p_a, sw, p_c, cv, y_a, y_b, pw, merged, x1, h2, a_pre = saved[l]
        g_in, g_1, g_row, g_pool = weights[l]
        vec = lambda a: a[l:l + 1]
        row_shape = jax.ShapeDtypeStruct((N_DEV, rp, d), BF16)

        def dd_weight(off):
            return pl.BlockSpec((N_DEV, rd, d), lambda j, i: (0, off, 0))

        def dd_grad(name, a, g, off, alias):
            return _mm(name, a, g, grid=(1, s // tk), a_spec=pl.BlockSpec((tk, d), lambda j, k: (k, 0)),
                       b_spec=pl.BlockSpec((tk, d), lambda j, k: (k, 0)), out_shape=row_shape,
                       o_spec=pl.BlockSpec((N_DEV, rd, d), lambda j, k: (0, off, 0)), dims=TN, nk=s // tk,
                       acc_shape=(d, d), alias_in=alias)

        d_a = _mm(f"d_act_{l}", dx16, g_row, grid=(N_DEV // 2, s // tm), a_spec=pl.BlockSpec((tm, d), row2),
                  b_spec=pl.BlockSpec((2, rf, d), lambda j, i: (j, 0, 0)), slabs="n",
                  extras=(a_pre,), extra_specs=(pl.BlockSpec((tm, 2 * rf), lambda j, i: (i, j)),),
                  epilogue=lambda v, a: v * (2.0 * jnp.maximum(a.astype(F32), 0.0)),
                  out_shape=jax.ShapeDtypeStruct((s, f), BF16),
                  o_spec=pl.BlockSpec((tm, 2 * rf), lambda j, i: (i, j)), dims=NT)
        dg_row = _mm(f"dw_mlp2_{l}", a_pre, dx16, grid=(N_DEV, s // tk_mlp),
                     a_spec=pl.BlockSpec((tk_mlp, rf), lambda j, k: (k, j)),
                     b_spec=pl.BlockSpec((tk_mlp, d), lambda j, k: (k, 0)), prologue=_relu_sq, out_shape=row_shape,
                     o_spec=pl.BlockSpec((None, rf, d), lambda j, k: (j, 0, 0)), dims=TN, nk=s // tk_mlp,
                     acc_shape=(rf, d))
        d_h2 = _mm(f"d_h2_{l}", d_a, g_1, grid=(1, s // tr), a_spec=pl.BlockSpec((tr, f), row2),
                   b_spec=pl.BlockSpec((N_DEV, d, c1), lambda j, i: (0, 0, 0)), slabs="k",
                   out_shape=jax.ShapeDtypeStruct((s, d), BF16), o_spec=pl.BlockSpec((tr, d), row2), dims=NT)
        dg_1 = _mm(f"dw_mlp1_{l}", h2, d_a, grid=(N_DEV, s // tk_mlp),
                   a_spec=pl.BlockSpec((tk_mlp, d), lambda j, k: (k, 0)),
                   b_spec=pl.BlockSpec((tk_mlp, c1), lambda j, k: (k, j)),
                   out_shape=jax.ShapeDtypeStruct((N_DEV, d, c1), BF16),
                   o_spec=pl.BlockSpec((None, d, c1), lambda j, k: (j, 0, 0)), dims=TN, nk=s // tk_mlp,
                   acc_shape=(d, c1))
        dx, dx16, dg_mlp = _rms_bwd(f"rms_mlp_bwd_{l}", d_h2, x1, vec(g_mlp), dx, tr)
        d_merged = _mm(f"d_merged_{l}", dx16, g_row, grid=(1, s // tm), a_spec=pl.BlockSpec((tm, d), row2),
                       b_spec=dd_weight(o_o), out_shape=jax.ShapeDtypeStruct((s, d), BF16),
                       o_spec=pl.BlockSpec((tm, d), row2), dims=NT)
        dg_row = dd_grad(f"dw_o_{l}", merged, dx16, o_o, dg_row)
        d_ya, d_yb, d_pw, d_gates, d_bout, d_pscale = _merge_bwd(f"merge_bwd_{l}", d_merged, proj, y_a, y_b, pw,
                                                                 vec(pool_scale), d, tr)
        d_pa = _mm(f"d_pa_{l}", d_ya, g_row, grid=(1, s // tm), a_spec=pl.BlockSpec((tm, d), row2), b_spec=dd_weight(o_a),
                   out_shape=jax.ShapeDtypeStruct((s, d), BF16), o_spec=pl.BlockSpec((tm, d), row2), dims=NT)
        d_sw = _mm(f"d_sw_{l}", d_yb, g_row, grid=(1, s // tm), a_spec=pl.BlockSpec((tm, d), row2), b_spec=dd_weight(o_b),
                   out_shape=jax.ShapeDtypeStruct((s, d), BF16), o_spec=pl.BlockSpec((tm, d), row2), dims=NT)
        d_pc = _pool_mm(f"d_pool_{l}", d_pw, g_pool, d, tm, NT)
        dg_row = dd_grad(f"dw_out_a_{l}", p_a, d_ya, o_a, dg_row)
        dg_row = dd_grad(f"dw_out_b_{l}", sw, d_yb, o_b, dg_row)
        dg_pool = _pool_wgrad(f"dw_pool_{l}", p_c, d_pw, d, tk)
        rest_going = _start_copies(f"grads_start_rest_{l}", [dg_1, dg_row, dg_pool], vec(g_mlp), blockwise=True)
        d_proj, d_bin, d_ca, d_cb, d_cbb, d_lng, d_lnb = _mix_pre_bwd(
            f"mix_bwd_{l}", proj, cv, d_pa, d_sw, d_pc, d_gates, conv_a_f[l], conv_b_f[l],
            vec(ln_b_g) + rest_going["token"][0:1, 0:1], vec(ln_b_b), d, tx)
        dg_in = _mm(f"dw_in_{l}", h, d_proj, grid=(N_DEV, s // tk), a_spec=pl.BlockSpec((tk, d), lambda j, k: (k, 0)),
                    b_spec=pl.BlockSpec((tk, ci), lambda j, k: (k, j)),
                    out_shape=jax.ShapeDtypeStruct((N_DEV, d, ci), BF16),
                    o_spec=pl.BlockSpec((None, d, ci), lambda j, k: (j, 0, 0)), dims=TN, nk=s // tk,
                    acc_shape=(d, ci))
        in_going = _start_copies(f"grads_start_in_{l}", [dg_in], vec(g_mix), blockwise=True)
        d_h = _mm(f"d_h_{l}", d_proj, g_in, grid=(s // tm, N_DEV // 2),
                  a_spec=pl.BlockSpec((tm, 2 * ci), lambda i, k: (i, k)),
                  b_spec=pl.BlockSpec((2, d, ci), lambda i, k: (k, 0, 0)), slabs="k",
                  out_shape=jax.ShapeDtypeStruct((s, d), BF16), o_spec=pl.BlockSpec((tm, d), lambda i, k: (i, 0)),
                  dims=NT, nk=N_DEV // 2, acc_shape=(tm, d))
        dx, dx16, dg_mix = _rms_bwd(f"rms_mix_bwd_{l}", d_h, x0, vec(g_mix) + in_going["token"][0:1, 0:1], dx, tr)
        small[l] = (dg_mix, d_bin, d_cbb, d_lng, d_lnb, d_bout, d_pscale, dg_mlp, d_ca, d_cb)
        exchanges[l] = (in_going, rest_going)

    grad_x = dx[None]

    names = ("g_mix", "b_in", "conv_b_bias", "ln_b_g", "ln_b_b", "b_out_b", "pool_scale", "g_mlp")
    given = dict(g_mix=(g_mix, m_g_mix, v_g_mix), b_in=(b_in, m_b_in, v_b_in),
                 conv_b_bias=(conv_b_bias, m_conv_b_bias, v_conv_b_bias), ln_b_g=(ln_b_g, m_ln_b_g, v_ln_b_g),
                 ln_b_b=(ln_b_b, m_ln_b_b, v_ln_b_b), b_out_b=(b_out_b, m_b_out_b, v_b_out_b),
                 pool_scale=(pool_scale, m_pool_scale, v_pool_scale), g_mlp=(g_mlp, m_g_mlp, v_g_mlp))
    partials, triples = [], []
    for i, nm in enumerate(names):
        partials.append(jnp.concatenate([small[l][i] for l in range(n_layers)], axis=0))
        triples.append(given[nm])
    partials.append(dg_final)
    triples.append(tuple(a.reshape(1, d) for a in (g_final, m_g_final, v_g_final)))
    partials.append(jnp.concatenate([small[l][8] for l in range(n_layers)], axis=0))
    partials.append(jnp.concatenate([small[l][9] for l in range(n_layers)], axis=0))
    outs = _small_update(partials, triples, 2)
    rep = {nm: outs[4 * i:4 * i + 4] for i, nm in enumerate(names)}
    rep["g_final"] = [a.reshape(d) for a in outs[4 * len(names):4 * len(names) + 4]]
    me = _block_of(*_my_place())
    gca = lax.dynamic_slice_in_dim(outs[-2].reshape(n_layers, K_A, d), me * rd, rd, axis=2)
    gcb = lax.dynamic_slice_in_dim(outs[-1].reshape(n_layers, K_B, d), me * rd, rd, axis=2)

    r_in, r_1, r_row, r_pool = [], [], [], []
    for l in reversed(range(n_layers)):
        in_going, rest_going = exchanges[l]
        srcs_r, lands_r = _wait_copies(f"grads_wait_rest_{l}", rest_going, outs[0], blockwise=True)
        srcs_i, lands_i = _wait_copies(f"grads_wait_in_{l}", in_going, outs[0], blockwise=True)
        got = _place_own(f"grads_own_{l}", lands_i + lands_r, srcs_i + srcs_r, blockwise=True)
        for lst, arr in zip((r_in, r_1, r_row, r_pool), got):
            lst.insert(0, arr)
    tb = min(256, d)
    layers = range(n_layers)
    res = {}
    res["w_in"] = _adamw("adamw_w_in", r_in, w_in, m_w_in, v_w_in, grid=(n_layers, d // tb),
                         part_specs=[_layer_part_spec(li, (N_DEV, tb, ci), d // tb) for li in layers],
                         w_spec=pl.BlockSpec((None, tb, ci), lambda l, i: (l, i, 0)))
    res["w_mlp1"] = _adamw("adamw_w_mlp1", r_1, w_mlp1, m_w_mlp1, v_w_mlp1, grid=(n_layers, d // tb),
                           part_specs=[_layer_part_spec(li, (N_DEV, tb, c1), d // tb) for li in layers],
                           w_spec=pl.BlockSpec((None, tb, c1), lambda l, i: (l, i, 0)))
    tf = min(256, rf)
    res["w_mlp2"] = _adamw("adamw_w_mlp2", r_row, w_mlp2, m_w_mlp2, v_w_mlp2, grid=(n_layers, rf // tf),
                           part_specs=[_layer_part_spec(li, (N_DEV, tf, d), rf // tf) for li in layers],
                           w_spec=pl.BlockSpec((None, tf, d), lambda l, i: (l, i, 0)))
    for nm, off, trip in (("w_out_a", o_a, (w_out_a, m_w_out_a, v_w_out_a)),
                          ("w_out_b", o_b, (w_out_b, m_w_out_b, v_w_out_b)), ("w_o", o_o, (w_o, m_w_o, v_w_o))):
        res[nm] = _adamw(f"adamw_{nm}", r_row, *trip, grid=(n_layers, 1),
                         part_specs=[_layer_part_spec(li, (N_DEV, rd, d), 1, row_off=off) for li in layers],
                         w_spec=pl.BlockSpec((None, rd, d), lambda l, i: (l, 0, 0)))
    res["w_pool"] = _adamw("adamw_w_pool", r_pool, w_pool, m_w_pool, v_w_pool, grid=(n_layers, 1),
                           part_specs=[_layer_part_spec(li, (N_DEV, N_GROUPS, gc // N_DEV, gc), 1) for li in layers],
                           w_spec=pl.BlockSpec((None, N_GROUPS, gc // N_DEV, gc), lambda l, i: (l, 0, 0, 0)))
    whole3 = lambda: (0, 0, 0)
    res["conv_a"] = _adamw("adamw_conv_a", [gca[None]], conv_a, m_conv_a, v_conv_a, grid=(),
                           part_specs=[pl.BlockSpec((1, n_layers, K_A, rd), lambda: (0, 0, 0, 0))],
                           w_spec=pl.BlockSpec((n_layers, K_A, rd), whole3))
    res["conv_b"] = _adamw("adamw_conv_b", [gcb[None]], conv_b, m_conv_b, v_conv_b, grid=(),
                           part_specs=[pl.BlockSpec((1, n_layers, K_B, rd), lambda: (0, 0, 0, 0))],
                           w_spec=pl.BlockSpec((n_layers, K_B, rd), whole3))
    res.update(rep)

    order = ("g_mix", "w_in", "b_in", "conv_a", "w_out_a", "conv_b", "conv_b_bias", "ln_b_g", "ln_b_b", "w_out_b",
             "b_out_b", "w_pool", "pool_scale", "w_o", "g_mlp", "w_mlp1", "w_mlp2", "g_final")
    out = [loss, grad_x]
    for kind in range(4):
        out += [res[nm][kind] for nm in order]
    return tuple(out)
```

```python
import jax
import jax.numpy as jnp
from jax import lax
from jax.experimental import pallas as pl
from jax.experimental.pallas import tpu as pltpu

F32 = jnp.float32
BF16 = jnp.bfloat16
MESH = pl.DeviceIdType.MESH

N_DEV = 8
EPS = 1e-6
K_A = 3
K_B = 31
POOL_WINDOWS = (2, 4, 8, 16)
N_GROUPS = len(POOL_WINDOWS)
HALO = 32
CHUNK = 16
SUBLANES = 8
TAP_GROUP = 4
ADAM_LR, ADAM_B1, ADAM_B2, ADAM_EPS, ADAM_WD, ADAM_STEP = 0.001, 0.9, 0.999, 1e-08, 0.01, 10
VMEM_LIMIT_BYTES = 56 * 1024 * 1024

NN = (((1,), (0,)), ((), ()))
NT = (((1,), (1,)), ((), ()))
TN = (((0,), (0,)), ((), ()))


def _params(*sem):
    return pltpu.CompilerParams(dimension_semantics=sem, vmem_limit_bytes=VMEM_LIMIT_BYTES)


def _sigmoid(v):
    return 1.0 / (1.0 + jnp.exp(-v))


def _mm(name, a, b, *, grid, a_spec, b_spec, out_shape, o_spec, dims, nk=1, acc_shape=None,
        extras=(), extra_specs=(), prologue=None, epilogue=None, alias_in=None, slabs=None, after=None):
    n_extra = len(extras)
    has_alias = alias_in is not None
    n_unread = (1 if has_alias else 0) + (1 if after is not None else 0)

    def body(*refs):
        a_ref, b_ref = refs[0], refs[1]
        ex = refs[2:2 + n_extra]
        o_ref = refs[2 + n_extra + n_unread]
        av = a_ref[...]
        if prologue is not None:
            av = prologue(av)
        av = av.astype(BF16)

        def finish(val, cols=None):
            if epilogue is not None:
                val = epilogue(val, *[e[...] if cols is None else e[:, cols] for e in ex])
            if cols is None:
                o_ref[...] = val.astype(o_ref.dtype).reshape(o_ref.shape)
            else:
                o_ref[:, cols] = val.astype(o_ref.dtype)

        if slabs == "n":
            for q in range(b_ref.shape[0]):
                pq = lax.dot_general(av, b_ref[q].astype(BF16), dims, preferred_element_type=F32)
                finish(pq, slice(q * pq.shape[1], (q + 1) * pq.shape[1]))
            return
        if slabs == "k":
            kc = av.shape[1] // b_ref.shape[0]
            p = None
            for q in range(b_ref.shape[0]):
                pq = lax.dot_general(av[:, q * kc:(q + 1) * kc], b_ref[q].astype(BF16), dims,
                                     preferred_element_type=F32)
                p = pq if p is None else p + pq
        else:
            bv = b_ref[...]
            bv = bv.reshape((-1, bv.shape[-1])).astype(BF16)
            p = lax.dot_general(av, bv, dims, preferred_element_type=F32)

        if nk == 1:
            finish(p)
        else:
            acc = refs[-1]
            k = pl.program_id(len(grid) - 1)

            @pl.when(k == 0)
            def _():
                acc[...] = p

            @pl.when(k > 0)
            def _():
                acc[...] += p

            @pl.when(k == nk - 1)
            def _():
                finish(acc[...])

    in_specs = [a_spec, b_spec, *extra_specs]
    operands = [a, b, *extras]
    aliases = {}
    if has_alias:
        in_specs.append(pl.BlockSpec(memory_space=pl.ANY))
        operands.append(alias_in)
        aliases = {len(operands) - 1: 0}
    if after is not None:
        in_specs.append(pl.BlockSpec(memory_space=pl.ANY))
        operands.append(after)
    sem =("parallel",) * (len(grid) - 1) + (("arbitrary",) if nk > 1 else ("parallel",))
    return pl.pallas_call(
        body, name=name, grid=grid, in_specs=in_specs, out_specs=o_spec, out_shape=out_shape,
        scratch_shapes=[pltpu.VMEM(acc_shape, F32)] if nk > 1 else [],
        input_output_aliases=aliases, compiler_params=_params(*sem),
    )(*operands)


def _relu_sq(v):
    r = jnp.maximum(v, 0)
    return r * r


def _rms_fwd(name, x, g, tm):
    s, d = x.shape

    def body(x_ref, g_ref, h_ref):
        xv = x_ref[...]
        r = lax.rsqrt(jnp.mean(xv * xv, axis=-1, keepdims=True) + EPS)
        h_ref[...] = (xv * r * g_ref[...]).astype(h_ref.dtype)

    return pl.pallas_call(
        body, name=name, grid=(s // tm,),
        in_specs=[pl.BlockSpec((tm, d), lambda i: (i, 0)), pl.BlockSpec((1, d), lambda i: (0, 0))],
        out_specs=pl.BlockSpec((tm, d), lambda i: (i, 0)),
        out_shape=jax.ShapeDtypeStruct((s, d), BF16), compiler_params=_params("parallel"),
    )(x, g)


def _colsum8(v):
    return jnp.sum(v.reshape(v.shape[0] // 8, 8, v.shape[1]), axis=0)


def _rms_bwd(name, dh, x, g, dres, tm):
    s, d = x.shape
    n = s // tm

    def body(dh_ref, x_ref, g_ref, dr_ref, dx_ref, dx16_ref, dg_ref, acc):
        i = pl.program_id(0)
        xv = x_ref[...]
        r = lax.rsqrt(jnp.mean(xv * xv, axis=-1, keepdims=True) + EPS)
        xh = xv * r
        dhv = dh_ref[...].astype(F32)
        part = _colsum8(dhv * xh)

        @pl.when(i == 0)
        def _():
            acc[...] = part

        @pl.when(i > 0)
        def _():
            acc[...] += part

        dxh = dhv * g_ref[...]
        dx = r * (dxh - xh * jnp.mean(dxh * xh, axis=-1, keepdims=True))
        dx = dx + dr_ref[...]
        dx_ref[...] = dx
        dx16_ref[...] = dx.astype(BF16)

        @pl.when(i == n - 1)
        def _():
            dg_ref[...] = jnp.sum(acc[...], axis=0, keepdims=True)

    return pl.pallas_call(
        body, name=name, grid=(n,),
        in_specs=[pl.BlockSpec((tm, d), lambda i: (i, 0)), pl.BlockSpec((tm, d), lambda i: (i, 0)),
                  pl.BlockSpec((1, d), lambda i: (0, 0)), pl.BlockSpec((tm, d), lambda i: (i, 0))],
        out_specs=[pl.BlockSpec((tm, d), lambda i: (i, 0)), pl.BlockSpec((tm, d), lambda i: (i, 0)),
                   pl.BlockSpec((1, d), lambda i: (0, 0))],
        out_shape=[jax.ShapeDtypeStruct((s, d), F32), jax.ShapeDtypeStruct((s, d), BF16),
                   jax.ShapeDtypeStruct((1, d), F32)],
        scratch_shapes=[pltpu.VMEM((8, d), F32)], compiler_params=_params("arbitrary"),
    )(dh, x, g, dres)


def _loss_head(x, g, target, tm):
    s, d = x.shape
    n = s // tm

    def body(x_ref, g_ref, t_ref, loss_ref, dx_ref, dx16_ref, dg_ref, acc_l, acc_g):
        i = pl.program_id(0)
        xv = x_ref[...]
        r = lax.rsqrt(jnp.mean(xv * xv, axis=-1, keepdims=True) + EPS)
        xh = xv * r
        err = xh * g_ref[...] - t_ref[...]
        dy = err * (1.0 / d)
        lpart = _colsum8(err * err)
        gpart = _colsum8(dy * xh)

        @pl.when(i == 0)
        def _():
            acc_l[...] = lpart
            acc_g[...] = gpart

        @pl.when(i > 0)
        def _():
            acc_l[...] += lpart
            acc_g[...] += gpart

        dxh = dy * g_ref[...]
        dx = r * (dxh - xh * jnp.mean(dxh * xh, axis=-1, keepdims=True))
        dx_ref[...] = dx
        dx16_ref[...] = dx.astype(BF16)

        @pl.when(i == n - 1)
        def _():
            loss_ref[...] = (0.5 / d) * jnp.sum(jnp.sum(acc_l[...], axis=0, keepdims=True), axis=1, keepdims=True)
            dg_ref[...] = jnp.sum(acc_g[...], axis=0, keepdims=True)

    return pl.pallas_call(
        body, name="loss_head", grid=(n,),
        in_specs=[pl.BlockSpec((tm, d), lambda i: (i, 0)), pl.BlockSpec((1, d), lambda i: (0, 0)),
                  pl.BlockSpec((tm, d), lambda i: (i, 0))],
        out_specs=[pl.BlockSpec((1, 1), lambda i: (0, 0)), pl.BlockSpec((tm, d), lambda i: (i, 0)),
                   pl.BlockSpec((tm, d), lambda i: (i, 0)), pl.BlockSpec((1, d), lambda i: (0, 0))],
        out_shape=[jax.ShapeDtypeStruct((1, 1), F32), jax.ShapeDtypeStruct((s, d), F32),
                   jax.ShapeDtypeStruct((s, d), BF16), jax.ShapeDtypeStruct((1, d), F32)],
        scratch_shapes=[pltpu.VMEM((8, d), F32), pltpu.VMEM((8, d), F32)], compiler_params=_params("arbitrary"),
    )(x, g, target)


def _sec(ref, n, d):
    return ref[:, n * d:(n + 1) * d].astype(F32)


def _fill_shifts(sh, ext):
    rows = ext.shape[0] - SUBLANES
    for b in range(1, SUBLANES):
        sh[b - 1, 0:rows, :] = ext[b:b + rows, :]


def _shifted(sh, ext, off, n):
    b = off % SUBLANES
    if b == 0:
        return ext[off:off + n, :]
    return sh[b - 1, off - b:off - b + n, :]


def _pool_count(row0, rows, window):
    t = row0 + lax.broadcasted_iota(jnp.int32, (rows, 1), 0)
    return jnp.minimum(t + 1, window).astype(F32)


def _mix_pre_fwd(name, proj, conv_a, conv_b, conv_b_bias, ln_g, ln_b, d, tm):
    s = proj.shape[0]
    n = s // tm
    gc = d // N_GROUPS
    hb = tm // HALO

    def body(pj_ref, hp_ref, ca_ref, cb_ref, cbb_ref, lng_ref, lnb_ref, pa_ref, sw_ref, pc_ref, cv_ref,
             eua, eub, euc, sh):
        i = pl.program_id(0)
        keep = (i > 0).astype(F32)
        eua[0:HALO, :] = _sec(hp_ref, 1, d) * _sec(hp_ref, 2, d) * keep
        eub[0:HALO, :] = _sec(hp_ref, 3, d) * _sigmoid(_sec(hp_ref, 4, d)) * keep
        euc[0:HALO, :] = _sec(hp_ref, 5, d) * keep
        eua[HALO:HALO + tm, :] = _sec(pj_ref, 1, d) * _sec(pj_ref, 2, d)
        eub[HALO:HALO + tm, :] = _sec(pj_ref, 3, d) * _sigmoid(_sec(pj_ref, 4, d))
        euc[HALO:HALO + tm, :] = _sec(pj_ref, 5, d)
        _fill_shifts(sh, eub)
        for c in range(tm // CHUNK):
            r0 = c * CHUNK
            z = jnp.zeros((CHUNK, d), F32)
            for k in range(K_A):
                z = z + ca_ref[k:k + 1, :] * eua[HALO + r0 - (K_A - 1) + k:HALO + r0 - (K_A - 1) + k + CHUNK, :]
            pa_ref[r0:r0 + CHUNK, :] = (pj_ref[r0:r0 + CHUNK, 0:d].astype(F32) * z).astype(pa_ref.dtype)
            cv = jnp.zeros((CHUNK, d), F32) + cbb_ref[...]
            for k in range(K_B):
                cv = cv + cb_ref[k:k + 1, :] * _shifted(sh, eub, HALO + r0 - (K_B - 1) + k, CHUNK)
            cv_ref[r0:r0 + CHUNK, :] = cv.astype(cv_ref.dtype)
        cvv = cv_ref[...].astype(F32)
        mu = jnp.mean(cvv, axis=-1, keepdims=True)
        xc = cvv - mu
        xh = xc * lax.rsqrt(jnp.mean(xc * xc, axis=-1, keepdims=True) + EPS)
        ln = xh * lng_ref[...] + lnb_ref[...]
        sw_ref[...] = (ln * _sigmoid(ln)).astype(sw_ref.dtype)
        for gi, w in enumerate(POOL_WINDOWS):
            cols = slice(gi * gc, (gi + 1) * gc)
            tot = euc[HALO:HALO + tm, cols]
            for k in range(1, w):
                tot = tot + euc[HALO - k:HALO - k + tm, cols]
            cnt = _pool_count(i * tm, tm, w)
            pc_ref[:, cols] = (tot / cnt - euc[HALO:HALO + tm, cols]).astype(pc_ref.dtype)

    row = lambda i: (i, 0)
    fixed = lambda i: (0, 0)
    act = jax.ShapeDtypeStruct((s, d), BF16)
    return pl.pallas_call(
        body, name=name, grid=(n,),
        in_specs=[pl.BlockSpec((tm, 6 * d), row),
                  pl.BlockSpec((HALO, 6 * d), lambda i: (jnp.maximum(i * hb - 1, 0), 0)),
                  pl.BlockSpec((K_A, d), fixed), pl.BlockSpec((K_B, d), fixed), pl.BlockSpec((1, d), fixed),
                  pl.BlockSpec((1, d), fixed), pl.BlockSpec((1, d), fixed)],
        out_specs=[pl.BlockSpec((tm, d), row)] * 4,
        out_shape=[act, act, act, act],
        scratch_shapes=[pltpu.VMEM((tm + HALO, d), F32)] * 3 + [pltpu.VMEM((SUBLANES - 1, tm + HALO, d), F32)],
        compiler_params=_params("parallel"),
    )(proj, proj, conv_a, conv_b, conv_b_bias, ln_g, ln_b)


def _merge_fwd(name, proj, ya, yb, pw, scale, d, tm):
    s = proj.shape[0]

    def body(g_ref, ya_ref, yb_ref, pw_ref, sc_ref, o_ref):
        m = _sigmoid(_sec(g_ref, 0, d)) * ya_ref[...].astype(F32)
        m = m + _sigmoid(_sec(g_ref, 1, d)) * yb_ref[...].astype(F32)
        m = m + _sigmoid(_sec(g_ref, 2, d)) * (pw_ref[...].astype(F32) * sc_ref[...])
        o_ref[...] = m.astype(o_ref.dtype)

    row = lambda i: (i, 0)
    return pl.pallas_call(
        body, name=name, grid=(s // tm,),
        in_specs=[pl.BlockSpec((tm, 3 * d), lambda i: (i, 2)), pl.BlockSpec((tm, d), row), pl.BlockSpec((tm, d), row),
                  pl.BlockSpec((tm, d), row), pl.BlockSpec((1, d), lambda i: (0, 0))],
        out_specs=pl.BlockSpec((tm, d), row), out_shape=jax.ShapeDtypeStruct((s, d), BF16),
        compiler_params=_params("parallel"),
    )(proj, ya, yb, pw, scale)


def _merge_bwd(name, dm, proj, ya, yb, pw, scale, d, tm):
    s = proj.shape[0]
    n = s // tm

    def body(dm_ref, g_ref, ya_ref, yb_ref, pw_ref, sc_ref, dya_ref, dyb_ref, dpw_ref, dg_ref, dbo_ref, dsc_ref,
             acc_b, acc_s):
        i = pl.program_id(0)
        dmv = dm_ref[...].astype(F32)
        g0 = _sigmoid(_sec(g_ref, 0, d))
        dya_ref[...] = (dmv * g0).astype(dya_ref.dtype)
        dg_ref[:, 0:d] = (dmv * ya_ref[...].astype(F32) * g0 * (1.0 - g0)).astype(dg_ref.dtype)
        g1 = _sigmoid(_sec(g_ref, 1, d))
        dyb = dmv * g1
        dyb_ref[...] = dyb.astype(dyb_ref.dtype)
        dg_ref[:, d:2 * d] = (dmv * yb_ref[...].astype(F32) * g1 * (1.0 - g1)).astype(dg_ref.dtype)
        g2 = _sigmoid(_sec(g_ref, 2, d))
        pwv = pw_ref[...].astype(F32)
        dyc = dmv * g2
        dpw_ref[...] = (dyc * sc_ref[...]).astype(dpw_ref.dtype)
        dg_ref[:, 2 * d:3 * d] = (dmv * (pwv * sc_ref[...]) * g2 * (1.0 - g2)).astype(dg_ref.dtype)
        pb = _colsum8(dyb)
        ps = _colsum8(dyc * pwv)

        @pl.when(i == 0)
        def _():
            acc_b[...] = pb
            acc_s[...] = ps

        @pl.when(i > 0)
        def _():
            acc_b[...] += pb
            acc_s[...] += ps

        @pl.when(i == n - 1)
        def _():
            dbo_ref[...] = jnp.sum(acc_b[...], axis=0, keepdims=True)
            dsc_ref[...] = jnp.sum(acc_s[...], axis=0, keepdims=True)

    row = lambda i: (i, 0)
    fixed = lambda i: (0, 0)
    act = jax.ShapeDtypeStruct((s, d), BF16)
    vec = jax.ShapeDtypeStruct((1, d), F32)
    return pl.pallas_call(
        body, name=name, grid=(n,),
        in_specs=[pl.BlockSpec((tm, d), row), pl.BlockSpec((tm, 3 * d), lambda i: (i, 2)), pl.BlockSpec((tm, d), row),
                  pl.BlockSpec((tm, d), row), pl.BlockSpec((tm, d), row), pl.BlockSpec((1, d), fixed)],
        out_specs=[pl.BlockSpec((tm, d), row)] * 3 + [pl.BlockSpec((tm, 3 * d), row), pl.BlockSpec((1, d), fixed),
                                                      pl.BlockSpec((1, d), fixed)],
        out_shape=[act, act, act, jax.ShapeDtypeStruct((s, 3 * d), BF16), vec, vec],
        scratch_shapes=[pltpu.VMEM((8, d), F32)] * 2, compiler_params=_params("arbitrary"),
    )(dm, proj, ya, yb, pw, scale)


def _mix_pre_bwd(name, proj, cv, dpa, dsw, dpc, dgates, conv_a, conv_b, ln_g, ln_b, d, tm):
    s = proj.shape[0]
    n = s // tm
    gc = d // N_GROUPS
    hb = tm // HALO
    last_halo = s // HALO - 1
    te = tm + HALO

    def ln_bwd(cvv, dswv, lng, lnb):
        mu = jnp.mean(cvv, axis=-1, keepdims=True)
        xc = cvv - mu
        rstd = lax.rsqrt(jnp.mean(xc * xc, axis=-1, keepdims=True) + EPS)
        xh = xc * rstd
        ln = xh * lng + lnb
        sg = _sigmoid(ln)
        dln = dswv * (sg * (1.0 + ln * (1.0 - sg)))
        dxh = dln * lng
        dcv = rstd * (dxh - jnp.mean(dxh, axis=-1, keepdims=True) - xh * jnp.mean(dxh * xh, axis=-1, keepdims=True))
        return dcv, dln, xh

    def body(pj_ref, hp_ref, hf_ref, cv_ref, cvf_ref, dpa_ref, dpaf_ref, dsw_ref, dswf_ref, dpc_ref, dpcf_ref, dgt_ref,
             ca_ref, cb_ref, lng_ref, lnb_ref,
             dpj_ref, dbin_ref, dca_ref, dcb_ref, dcbb_ref, dlng_ref, dlnb_ref,
             eua, eub, edz, edcv, eq, sh, acc_bin, acc_ca, acc_cb, acc_v):
        i = pl.program_id(0)
        keep_p = (i > 0).astype(F32)
        keep_f = (i < n - 1).astype(F32)

        @pl.when(i == 0)
        def _():
            acc_bin[...] = jnp.zeros_like(acc_bin)
            acc_ca[...] = jnp.zeros_like(acc_ca)
            acc_cb[...] = jnp.zeros_like(acc_cb)
            acc_v[...] = jnp.zeros_like(acc_v)

        eua[0:HALO, :] = _sec(hp_ref, 1, d) * _sec(hp_ref, 2, d) * keep_p
        eub[0:HALO, :] = _sec(hp_ref, 3, d) * _sigmoid(_sec(hp_ref, 4, d)) * keep_p
        eua[HALO:te, :] = _sec(pj_ref, 1, d) * _sec(pj_ref, 2, d)
        eub[HALO:te, :] = _sec(pj_ref, 3, d) * _sigmoid(_sec(pj_ref, 4, d))
        edz[0:tm, :] = dpa_ref[...].astype(F32) * _sec(pj_ref, 0, d)
        edz[tm:te, :] = dpaf_ref[...].astype(F32) * _sec(hf_ref, 0, d) * keep_f
        dcv, dln, xh = ln_bwd(cv_ref[...].astype(F32), dsw_ref[...].astype(F32), lng_ref[...], lnb_ref[...])
        edcv[0:tm, :] = dcv
        acc_v[0:8, :] += _colsum8(dcv)
        acc_v[8:16, :] += _colsum8(dln * xh)
        acc_v[16:24, :] += _colsum8(dln)
        dcvf, _, _ = ln_bwd(cvf_ref[...].astype(F32), dswf_ref[...].astype(F32), lng_ref[...], lnb_ref[...])
        edcv[tm:te, :] = dcvf * keep_f
        for gi, w in enumerate(POOL_WINDOWS):
            cols = slice(gi * gc, (gi + 1) * gc)
            eq[0:tm, cols] = dpc_ref[:, cols].astype(F32) / _pool_count(i * tm, tm, w)
            eq[tm:te, cols] = dpcf_ref[:, cols].astype(F32) / _pool_count((i + 1) * tm, HALO, w) * keep_f

        def put(sec_idx, r0, val):
            dpj_ref[r0:r0 + CHUNK, sec_idx * d:(sec_idx + 1) * d] = val.astype(dpj_ref.dtype)
            acc_bin[:, sec_idx * d:(sec_idx + 1) * d] += _colsum8(val)

        _fill_shifts(sh, edcv)
        for c in range(tm // CHUNK):
            r0 = c * CHUNK
            rows = slice(r0, r0 + CHUNK)
            z = jnp.zeros((CHUNK, d), F32)
            dua = jnp.zeros((CHUNK, d), F32)
            for k in range(K_A):
                z = z + ca_ref[k:k + 1, :] * eua[HALO + r0 - (K_A - 1) + k:HALO + r0 - (K_A - 1) + k + CHUNK, :]
                dua = dua + ca_ref[k:k + 1, :] * edz[r0 + (K_A - 1) - k:r0 + (K_A - 1) - k + CHUNK, :]
            put(0, r0, dpa_ref[rows, :].astype(F32) * z)
            put(1, r0, dua * pj_ref[rows, 2 * d:3 * d].astype(F32))
            put(2, r0, dua * pj_ref[rows, d:2 * d].astype(F32))
            dub = jnp.zeros((CHUNK, d), F32)
            for k in range(K_B):
                dub = dub + cb_ref[k:k + 1, :] * _shifted(sh, edcv, r0 + (K_B - 1) - k, CHUNK)
            bval = pj_ref[rows, 3 * d:4 * d].astype(F32)
            sg = _sigmoid(pj_ref[rows, 4 * d:5 * d].astype(F32))
            put(3, r0, dub * sg)
            put(4, r0, dub * bval * sg * (1.0 - sg))
            for gi, w in enumerate(POOL_WINDOWS):
                cols = slice(gi * gc, (gi + 1) * gc)
                tot = eq[rows, cols]
                for k in range(1, w):
                    tot = tot + eq[r0 + k:r0 + k + CHUNK, cols]
                dci = tot - dpc_ref[rows, cols].astype(F32)
                dpj_ref[rows, 5 * d + gi * gc:5 * d + (gi + 1) * gc] = dci.astype(dpj_ref.dtype)
                acc_bin[:, 5 * d + gi * gc:5 * d + (gi + 1) * gc] += _colsum8(dci)
        for q in range(3):
            gv = dgt_ref[:, q * d:(q + 1) * d]
            dpj_ref[:, (6 + q) * d:(7 + q) * d] = gv
            acc_bin[:, (6 + q) * d:(7 + q) * d] += _colsum8(gv.astype(F32))
        for k in range(K_A):
            a = jnp.zeros((8, d), F32)
            for c in range(tm // CHUNK):
                r0 = c * CHUNK
                a = a + _colsum8(edz[r0:r0 + CHUNK, :] * eua[HALO + r0 - (K_A - 1) + k:HALO + r0 - (K_A - 1) + k + CHUNK, :])
            acc_ca[k] += a
        _fill_shifts(sh, eub)
        for k0 in range(0, K_B, TAP_GROUP):
            taps = range(k0, min(k0 + TAP_GROUP, K_B))
            a = {k: jnp.zeros((8, d), F32) for k in taps}
            for c in range(tm // CHUNK):
                r0 = c * CHUNK
                dc = edcv[r0:r0 + CHUNK, :]
                for k in taps:
                    a[k] = a[k] + _colsum8(dc * _shifted(sh, eub, HALO + r0 - (K_B - 1) + k, CHUNK))
            for k in taps:
                acc_cb[k] += a[k]

        @pl.when(i == n - 1)
        def _():
            dbin_ref[...] = jnp.sum(acc_bin[...], axis=0, keepdims=True)
            for k in range(K_A):
                dca_ref[k:k + 1, :] = jnp.sum(acc_ca[k], axis=0, keepdims=True)
            for k in range(K_B):
                dcb_ref[k:k + 1, :] = jnp.sum(acc_cb[k], axis=0, keepdims=True)
            dcbb_ref[...] = jnp.sum(acc_v[0:8, :], axis=0, keepdims=True)
            dlng_ref[...] = jnp.sum(acc_v[8:16, :], axis=0, keepdims=True)
            dlnb_ref[...] = jnp.sum(acc_v[16:24, :], axis=0, keepdims=True)

    row = lambda i: (i, 0)
    fixed = lambda i: (0, 0)
    past = lambda i: (jnp.maximum(i * hb - 1, 0), 0)
    fut = lambda i: (jnp.minimum((i + 1) * hb, last_halo), 0)
    vec = jax.ShapeDtypeStruct((1, d), F32)
    return pl.pallas_call(
        body, name=name, grid=(n,),
        in_specs=[pl.BlockSpec((tm, 6 * d), row), pl.BlockSpec((HALO, 6 * d), past), pl.BlockSpec((HALO, 6 * d), fut),
                  pl.BlockSpec((tm, d), row), pl.BlockSpec((HALO, d), fut),
                  pl.BlockSpec((tm, d), row), pl.BlockSpec((HALO, d), fut),
                  pl.BlockSpec((tm, d), row), pl.BlockSpec((HALO, d), fut),
                  pl.BlockSpec((tm, d), row), pl.BlockSpec((HALO, d), fut),
                  pl.BlockSpec((tm, 3 * d), row),
                  pl.BlockSpec((K_A, d), fixed), pl.BlockSpec((K_B, d), fixed), pl.BlockSpec((1, d), fixed),
                  pl.BlockSpec((1, d), fixed)],
        out_specs=[pl.BlockSpec((tm, 9 * d), row), pl.BlockSpec((1, 9 * d), fixed), pl.BlockSpec((K_A, d), fixed),
                   pl.BlockSpec((K_B, d), fixed), pl.BlockSpec((1, d), fixed), pl.BlockSpec((1, d), fixed),
                   pl.BlockSpec((1, d), fixed)],
        out_shape=[jax.ShapeDtypeStruct((s, 9 * d), BF16), jax.ShapeDtypeStruct((1, 9 * d), F32),
                   jax.ShapeDtypeStruct((K_A, d), F32), jax.ShapeDtypeStruct((K_B, d), F32), vec, vec, vec],
        scratch_shapes=[pltpu.VMEM((te, d), F32)] * 5 + [pltpu.VMEM((SUBLANES - 1, te, d), F32),
                                                         pltpu.VMEM((8, 9 * d), F32), pltpu.VMEM((K_A, 8, d), F32),
                                                         pltpu.VMEM((K_B, 8, d), F32), pltpu.VMEM((24, d), F32)],
        compiler_params=_params("arbitrary"),
    )(proj, proj, proj, cv, cv, dpa, dpa, dsw, dsw, dpc, dpc, dgates, conv_a, conv_b, ln_g, ln_b)


def _pool_mm(name, a, gpool, d, tm, dims):
    s = a.shape[0]
    gc = d // N_GROUPS

    def body(a_ref, w_ref, o_ref):
        for gi in range(N_GROUPS):
            w = w_ref[:, gi].reshape(gc, gc)
            o_ref[:, gi * gc:(gi + 1) * gc] = lax.dot_general(
                a_ref[:, gi * gc:(gi + 1) * gc], w, dims, preferred_element_type=F32).astype(o_ref.dtype)

    return pl.pallas_call(
        body, name=name, grid=(s // tm,),
        in_specs=[pl.BlockSpec((tm, d), lambda i: (i, 0)),
                  pl.BlockSpec((N_DEV, N_GROUPS, gc // N_DEV, gc), lambda i: (0, 0, 0, 0))],
        out_specs=pl.BlockSpec((tm, d), lambda i: (i, 0)), out_shape=jax.ShapeDtypeStruct((s, d), BF16),
        compiler_params=_params("parallel"),
    )(a, gpool)


def _pool_wgrad(name, p, dpw, d, tk):
    s = p.shape[0]
    gc = d // N_GROUPS
    n = s // tk

    def body(p_ref, g_ref, o_ref, acc):
        k = pl.program_id(0)
        for gi in range(N_GROUPS):
            cols = slice(gi * gc, (gi + 1) * gc)
            part = lax.dot_general(p_ref[:, cols], g_ref[:, cols], TN, preferred_element_type=F32)

            @pl.when(k == 0)
            def _():
                acc[gi] = part

            @pl.when(k > 0)
            def _():
                acc[gi] += part

        @pl.when(k == n - 1)
        def _():
            for gi in range(N_GROUPS):
                o_ref[:, gi] = acc[gi].astype(o_ref.dtype).reshape(N_DEV, gc // N_DEV, gc)

    return pl.pallas_call(
        body, name=name, grid=(n,),
        in_specs=[pl.BlockSpec((tk, d), lambda k: (k, 0)), pl.BlockSpec((tk, d), lambda k: (k, 0))],
        out_specs=pl.BlockSpec((N_DEV, N_GROUPS, gc // N_DEV, gc), lambda k: (0, 0, 0, 0)),
        out_shape=jax.ShapeDtypeStruct((N_DEV, N_GROUPS, gc // N_DEV, gc), BF16),
        scratch_shapes=[pltpu.VMEM((N_GROUPS, gc, gc), F32)], compiler_params=_params("arbitrary"),
    )(p, dpw)


def _my_place():
    x, y, c = lax.axis_index("x"), lax.axis_index("y"), lax.axis_index("c")
    return x, y, c


def _block_of(x, y, c):
    return 4 * x + 2 * y + c


def _gather_shards(shards):
    n_arr = len(shards)

    def body(*refs):
        srcs = refs[:n_arr]
        outs = refs[n_arr:2 * n_arr]
        send_sems, recv_sems, local_sems = refs[2 * n_arr:]
        x, y, c = _my_place()
        me, sibling = (x, y, c), (x, y, 1 - c)
        chips = [(1 - x, y), (x, 1 - y), (1 - x, 1 - y)]

        def copy(n, k, block, to, src=None):
            rows = outs[n].at[_block_of(*block)]
            return pltpu.make_async_remote_copy(
                src_ref=rows if src is None else src, dst_ref=rows, send_sem=send_sems.at[n, k],
                recv_sem=recv_sems.at[n, k], device_id=to, device_id_type=MESH)

        mine = [pltpu.make_async_copy(srcs[n], outs[n].at[_block_of(*me)], local_sems.at[n]) for n in range(n_arr)]
        for cp in mine:
            cp.start()
        first = []
        for n in range(n_arr):
            first.append(copy(n, 0, me, sibling, src=srcs[n]))
            first += [copy(n, 1 + j, me, (*chip, c), src=srcs[n]) for j, chip in enumerate(chips)]
        for cp in first:
            cp.start()
        passed = []
        for n in range(n_arr):
            for j, chip in enumerate(chips):
                copy(n, 1 + j, (*chip, c), me).wait_recv()
                fwd = copy(n, 4 + j, (*chip, c), sibling)
                fwd.start()
                passed.append(fwd)
        for n in range(n_arr):
            copy(n, 0, sibling, me).wait_recv()
            for j, chip in enumerate(chips):
                copy(n, 4 + j, (*chip, 1 - c), me).wait_recv()
        for cp in first + passed:
            cp.wait_send()
        for cp in mine:
            cp.wait()

    any_spec = pl.BlockSpec(memory_space=pl.ANY)
    return pl.pallas_call(
        body, name="gather_weights",
        in_specs=[any_spec] * n_arr, out_specs=[any_spec] * n_arr,
        out_shape=[jax.ShapeDtypeStruct((N_DEV, *sh.shape), sh.dtype) for sh in shards],
        scratch_shapes=[pltpu.SemaphoreType.DMA((n_arr, 7)), pltpu.SemaphoreType.DMA((n_arr, 7)),
                        pltpu.SemaphoreType.DMA((n_arr,))],
    )(*shards)


def _peers(x, y, c):
    out = []
    for r in range(1, N_DEV):
        fx, fy, fc = (r >> 2) & 1, (r >> 1) & 1, r & 1
        out.append(((1 - x) if fx else x, (1 - y) if fy else y, (1 - c) if fc else c))
    return out


HBM_SPEC = pl.BlockSpec(memory_space=pltpu.HBM)
SEM_SPEC = pl.BlockSpec(memory_space=pltpu.SEMAPHORE)
ANY_SPEC = pl.BlockSpec(memory_space=pl.ANY)
N_PEERS = N_DEV - 1


def _peer_copy(src_ref, land_ref, send_sems, recv_sems, i, r, peer, me, blockwise):
    src = src_ref.at[_block_of(*peer)] if blockwise else src_ref
    return pltpu.make_async_remote_copy(
        src_ref=src, dst_ref=land_ref.at[me], send_sem=send_sems.at[i * N_PEERS + r],
        recv_sem=recv_sems.at[i * N_PEERS + r], device_id=peer, device_id_type=MESH)


def _start_copies(name, srcs, after, blockwise):
    n = len(srcs)

    def body(*refs):
        s_in, l_in = refs[:n], refs[n:2 * n]
        send_sems, recv_sems = refs[2 * n + 1], refs[2 * n + 2]
        token = refs[-1]
        x, y, c = _my_place()
        me = _block_of(x, y, c)
        for i in range(n):
            for r, peer in enumerate(_peers(x, y, c)):
                _peer_copy(s_in[i], l_in[i], send_sems, recv_sems, i, r, peer, me, blockwise).start()
        token[...] = jnp.zeros_like(token)

    land_shapes = [s.shape if blockwise else (N_DEV, *s.shape) for s in srcs]
    lands = [pltpu.with_memory_space_constraint(lax.empty(sh, s.dtype), pltpu.HBM) for sh, s in zip(land_shapes, srcs)]
    ins = [pltpu.with_memory_space_constraint(s, pltpu.HBM) for s in srcs]
    out = pl.pallas_call(
        body, name=name,
        out_shape=(pltpu.SemaphoreType.DMA((n * N_PEERS,)), pltpu.SemaphoreType.DMA((n * N_PEERS,)),
                   *[pltpu.HBM(s.shape, s.dtype) for s in srcs],
                   *[pltpu.HBM(sh, s.dtype) for sh, s in zip(land_shapes, srcs)],
                   jax.ShapeDtypeStruct((8, 128), F32)),
        in_specs=[HBM_SPEC] * (2 * n) + [ANY_SPEC],
        out_specs=(SEM_SPEC, SEM_SPEC, *[HBM_SPEC] * (2 * n), pl.BlockSpec(memory_space=pltpu.VMEM)),
        input_output_aliases={i: 2 + i for i in range(2 * n)},
        compiler_params=pltpu.CompilerParams(has_side_effects=pltpu.SideEffectType.DATAFLOW_SIDE_EFFECTING),
    )(*ins, *lands, after)
    return dict(send=out[0], recv=out[1], srcs=list(out[2:2 + n]), lands=list(out[2 + n:2 + 2 * n]), token=out[-1])


def _wait_copies(name, state, after, blockwise):
    n = len(state["srcs"])

    def body(*refs):
        s_in, l_in = refs[:n], refs[n:2 * n]
        send_sems, recv_sems = refs[2 * n], refs[2 * n + 1]
        x, y, c = _my_place()
        me = _block_of(x, y, c)
        for i in range(n):
            for r, peer in enumerate(_peers(x, y, c)):
                cp = _peer_copy(s_in[i], l_in[i], send_sems, recv_sems, i, r, peer, me, blockwise)
                cp.wait_send()
                cp.wait_recv()

    both = state["srcs"] + state["lands"]
    out = pl.pallas_call(
        body, name=name, out_shape=tuple(pltpu.HBM(a.shape, a.dtype) for a in both),
        in_specs=[HBM_SPEC] * (2 * n) + [SEM_SPEC, SEM_SPEC, ANY_SPEC], out_specs=tuple([HBM_SPEC] * (2 * n)),
        input_output_aliases={i: i for i in range(2 * n)},
        compiler_params=pltpu.CompilerParams(has_side_effects=pltpu.SideEffectType.DATAFLOW_SIDE_EFFECTING),
    )(*both, state["send"], state["recv"], after)
    return list(out[:n]), list(out[n:])


COPY_BLOCK_BYTES = 2 * 1024 * 1024


def _place_own(name, lands, srcs, me, blockwise):
    out = []
    for i, (land, src) in enumerate(zip(lands, srcs)):
        part = land.shape[1:]
        row_bytes = land.dtype.itemsize
        for extent in part[1:]:
            row_bytes *= extent
        tr = part[0]
        while tr * row_bytes > COPY_BLOCK_BYTES and tr % 16 == 0:
            tr //= 2
        tail = (0,) * (len(part) - 1)

        def body(me_ref, s_ref, l_ref, o_ref):
            o_ref[...] = s_ref[...]

        if blockwise:
            s_spec = pl.BlockSpec((None, tr, *part[1:]), lambda j, me_ref: (me_ref[0], j, *tail))
        else:
            s_spec = pl.BlockSpec((tr, *part[1:]), lambda j, me_ref: (j, *tail))
        out.append(pl.pallas_call(
            body, name=f"{name}_{i}",
            grid_spec=pltpu.PrefetchScalarGridSpec(
                num_scalar_prefetch=1, grid=(part[0] // tr,), in_specs=[s_spec, ANY_SPEC],
                out_specs=pl.BlockSpec((None, tr, *part[1:]), lambda j, me_ref: (me_ref[0], j, *tail))),
            out_shape=jax.ShapeDtypeStruct(land.shape, land.dtype), input_output_aliases={2: 0},
            compiler_params=_params("parallel"),
        )(me, src, land))
    return out


def _adamw_math(w, g, m, v):
    m = ADAM_B1 * m + (1.0 - ADAM_B1) * g
    v = ADAM_B2 * v + (1.0 - ADAM_B2) * (g * g)
    m_hat = m / (1.0 - ADAM_B1 ** ADAM_STEP)
    v_hat = v / (1.0 - ADAM_B2 ** ADAM_STEP)
    delta = -ADAM_LR * (m_hat / (jnp.sqrt(v_hat) + ADAM_EPS) + ADAM_WD * w)
    return delta, m, v


def _adamw(name, parts, w, m, v, *, grid, part_specs, w_spec):
    n_layers = len(parts)
    n_parts = parts[0].shape[0]

    def body(*refs):
        p_refs = refs[:n_layers]
        w_ref, m_ref, v_ref, g_ref, d_ref, nm_ref, nv_ref = refs[n_layers:]

        def total(p_ref):
            t = p_ref[0].astype(F32)
            for k in range(1, n_parts):
                t = t + p_ref[k].astype(F32)
            return t

        g = total(p_refs[0])
        for li in range(1, n_layers):
            g = jnp.where(pl.program_id(0) == li, total(p_refs[li]), g)
        delta, nm, nv = _adamw_math(w_ref[...], g, m_ref[...], v_ref[...])
        g_ref[...] = g
        d_ref[...] = delta
        nm_ref[...] = nm
        nv_ref[...] = nv

    out = jax.ShapeDtypeStruct(w.shape, F32)
    return pl.pallas_call(
        body, name=name, grid=grid, in_specs=[*part_specs, w_spec, w_spec, w_spec], out_specs=[w_spec] * 4,
        out_shape=[out] * 4, compiler_params=_params(*(("parallel",) * len(grid))),
    )(*parts, w, m, v)


def _layer_part_spec(layer, block, n_blocks, row_off=0):
    def index_map(l, i):
        ii = jnp.where(l == layer, i, jnp.where(l < layer, 0, n_blocks - 1))
        return (0, row_off + ii) + (0,) * (len(block) - 2)
    return pl.BlockSpec(block, index_map)


def _small_update(partials, triples, conv_rows):
    d = partials[-1].shape[-1]
    n_rep = len(triples)
    n_part = len(partials)
    rows = []
    for p in partials:
        rows.append(p.shape[0] * (p.shape[1] // d))
    offs = [sum(rows[:i]) for i in range(n_part)]
    total = -(-sum(rows) // 8) * 8

    def body(*refs):
        p_refs = refs[:n_part]
        wmv = refs[n_part:n_part + 3 * n_rep]
        outs = refs[n_part + 3 * n_rep:n_part + 3 * n_rep + 4 * n_rep + (n_part - n_rep)]
        buf, send_sems, recv_sems = refs[-3:]
        x, y, c = _my_place()
        me = _block_of(x, y, c)
        peers = _peers(x, y, c)
        mine = buf.at[me]
        if total > sum(rows):
            mine[sum(rows):total, :] = jnp.zeros((total - sum(rows), d), F32)
        for p_ref, off in zip(p_refs, offs):
            nr, nc = p_ref.shape[0], p_ref.shape[1] // d
            if nc == 1:
                mine[off:off + nr, :] = p_ref[...]
            else:
                for r in range(nr):
                    for q in range(nc):
                        mine[off + r * nc + q:off + r * nc + q + 1, :] = p_ref[r:r + 1, q * d:(q + 1) * d]
        sends =[pltpu.make_async_remote_copy(
            src_ref=buf.at[me], dst_ref=buf.at[me], send_sem=send_sems.at[r], recv_sem=recv_sems.at[r],
            device_id=peer, device_id_type=MESH) for r, peer in enumerate(peers)]
        for cp in sends:
            cp.start()
        for r, peer in enumerate(peers):
            pltpu.make_async_remote_copy(
                src_ref=buf.at[me], dst_ref=buf.at[_block_of(*peer)], send_sem=send_sems.at[r],
                recv_sem=recv_sems.at[r], device_id=peer, device_id_type=MESH).wait_recv()
        for cp in sends:
            cp.wait_send()
        tot = buf[0]
        for k in range(1, N_DEV):
            tot = tot + buf[k]
        buf[0] = tot
        for idx in range(n_part):
            nr, nc = p_refs[idx].shape[0], p_refs[idx].shape[1] // d
            if idx < n_rep:
                w_ref, m_ref, v_ref = wmv[3 * idx:3 * idx + 3]
                g_ref, d_ref, nm_ref, nv_ref = outs[4 * idx:4 * idx + 4]
            else:
                g_ref = outs[4 * n_rep + idx - n_rep]
            pieces = [(slice(0, nr), slice(0, d), offs[idx], nr)] if nc == 1 else [
                (slice(r, r + 1), slice(q * d, (q + 1) * d), offs[idx] + r * nc + q, 1)
                for r in range(nr) for q in range(nc)]
            for rws, cols, row, cnt in pieces:
                g = buf[0, row:row + cnt, :]
                g_ref[rws, cols] = g
                if idx < n_rep:
                    delta, nm, nv = _adamw_math(w_ref[rws, cols], g, m_ref[rws, cols], v_ref[rws, cols])
                    d_ref[rws, cols] = delta
                    nm_ref[rws, cols] = nm
                    nv_ref[rws, cols] = nv

    vm = pl.BlockSpec(memory_space=pltpu.VMEM)
    operands = list(partials)
    for t in triples:
        operands += list(t)
    out_shape = []
    for idx in range(n_rep):
        out_shape += [jax.ShapeDtypeStruct(partials[idx].shape, F32)] * 4
    for idx in range(n_rep, n_part):
        out_shape.append(jax.ShapeDtypeStruct(partials[idx].shape, F32))
    return pl.pallas_call(
        body, name="small_allreduce_adamw", in_specs=[vm] * len(operands), out_specs=[vm] * len(out_shape),
        out_shape=out_shape,
        scratch_shapes=[pltpu.VMEM((N_DEV, total, d), F32), pltpu.SemaphoreType.DMA((7,)), pltpu.SemaphoreType.DMA((7,))],
        compiler_params=pltpu.CompilerParams(vmem_limit_bytes=VMEM_LIMIT_BYTES),
    )(*operands)


def kernel(x, g_mix, w_in, b_in, conv_a, w_out_a, conv_b, conv_b_bias, ln_b_g, ln_b_b, w_out_b, b_out_b, w_pool, pool_scale, w_o, g_mlp, w_mlp1, w_mlp2, g_final, loss_target, m_g_mix, m_w_in, m_b_in, m_conv_a, m_w_out_a, m_conv_b, m_conv_b_bias, m_ln_b_g, m_ln_b_b, m_w_out_b, m_b_out_b, m_w_pool, m_pool_scale, m_w_o, m_g_mlp, m_w_mlp1, m_w_mlp2, m_g_final, v_g_mix, v_w_in, v_b_in, v_conv_a, v_w_out_a, v_conv_b, v_conv_b_bias, v_ln_b_g, v_ln_b_b, v_w_out_b, v_b_out_b, v_w_pool, v_pool_scale, v_w_o, v_g_mlp, v_w_mlp1, v_w_mlp2, v_g_final):
    _, s, d = x.shape
    n_layers = g_mix.shape[0]
    p_in = b_in.shape[1]
    ci = w_in.shape[2]
    c1 = w_mlp1.shape[2]
    rf = w_mlp2.shape[1]
    rd = w_out_a.shape[1]
    f = rf * N_DEV
    rp = rf + 3 * rd
    o_a, o_b, o_o = rf // rd, rf // rd + 1, rf // rd + 2
    gc = d // N_GROUPS
    ca_rows = 8
    tm = min(1024, s)
    tr = min(512, s)
    tx = min(256, s)
    tk = min(2048, s)
    tk_mlp = min(4096, s)

    me_arr = jnp.reshape(_block_of(*_my_place()), (1,)).astype(jnp.int32)

    def layer_shards(l):
        row_pack = jnp.concatenate([w_mlp2[l], w_out_a[l], w_out_b[l], w_o[l]], axis=0).astype(BF16)
        return [w_in[l].astype(BF16), w_mlp1[l].astype(BF16), row_pack, w_pool[l].astype(BF16)]

    conv_pack = jnp.concatenate(
        [conv_a, jnp.zeros((n_layers, ca_rows - K_A, rd), F32), conv_b], axis=1)
    first_shards = layer_shards(0)
    g_in_first, g_conv = _gather_shards([first_shards[0], conv_pack])
    conv_full = jnp.transpose(g_conv, (1, 2, 0, 3)).reshape(n_layers, ca_rows + K_B, d)
    conv_a_f = conv_full[:, :K_A]
    conv_b_f = conv_full[:, ca_rows:]
    in_flight = [_start_copies("gather_start_rest_0", first_shards[1:], g_conv, blockwise=False)]
    for l in range(1, n_layers):
        in_flight.append(_start_copies(f"gather_start_{l}", layer_shards(l), in_flight[-1]["token"], blockwise=False))
    token = in_flight[-1]["token"][0:1, 0:1]

    xs = [x[0]]
    saved = []
    weights = []
    row2 = lambda j, i: (i, 0)
    for l in range(n_layers):
        x0 = xs[-1]
        vec = lambda a: a[l:l + 1]
        if l > 0:
            srcs, lands = _wait_copies(f"gather_wait_{l}", in_flight[l], x0, blockwise=False)
            g_in, g_1, g_row, g_pool = _place_own(f"gather_own_{l}", lands, srcs, me_arr, blockwise=False)
        else:
            g_in = g_in_first
        h = _rms_fwd(f"rms_mix_{l}", x0, vec(g_mix) + token if l == 0 else vec(g_mix), tr)
        proj = _mm(
            f"proj_{l}", h, g_in, grid=(N_DEV, s // tm), a_spec=pl.BlockSpec((tm, d), row2),
            b_spec=pl.BlockSpec((None, d, ci), lambda j, i: (j, 0, 0)),
            extras=(vec(b_in),), extra_specs=(pl.BlockSpec((1, ci), lambda j, i: (0, j)),),
            epilogue=lambda v, b: v + b, out_shape=jax.ShapeDtypeStruct((s, p_in), BF16),
            o_spec=pl.BlockSpec((tm, ci), lambda j, i: (i, j)), dims=NN)
        p_a, sw, p_c, cv = _mix_pre_fwd(f"mix_fwd_{l}", proj, conv_a_f[l], conv_b_f[l], vec(conv_b_bias), vec(ln_b_g),
                                        vec(ln_b_b), d, tx)
        if l == 0:
            srcs, lands = _wait_copies("gather_wait_rest_0", in_flight[0], cv, blockwise=False)
            g_1, g_row, g_pool = _place_own("gather_own_rest_0", lands, srcs, me_arr, blockwise=False)
        weights.append((g_in, g_1, g_row, g_pool))

        def dd_weight(off):
            return pl.BlockSpec((N_DEV, rd, d), lambda j, i: (0, off, 0))

        y_a =_mm(f"out_a_{l}", p_a, g_row, grid=(1, s // tm), a_spec=pl.BlockSpec((tm, d), row2), b_spec=dd_weight(o_a),
                  out_shape=jax.ShapeDtypeStruct((s, d), BF16), o_spec=pl.BlockSpec((tm, d), row2), dims=NN)
        y_b = _mm(f"out_b_{l}", sw, g_row, grid=(1, s // tm), a_spec=pl.BlockSpec((tm, d), row2), b_spec=dd_weight(o_b),
                  extras=(vec(b_out_b),), extra_specs=(pl.BlockSpec((1, d), lambda j, i: (0, 0)),),
                  epilogue=lambda v, b: v + b,
                  out_shape=jax.ShapeDtypeStruct((s, d), BF16), o_spec=pl.BlockSpec((tm, d), row2), dims=NN)
        pw = _pool_mm(f"pool_{l}", p_c, g_pool, d, tm, NN)
        merged = _merge_fwd(f"merge_fwd_{l}", proj, y_a, y_b, pw, vec(pool_scale), d, tr)
        x1 = _mm(f"w_o_{l}", merged, g_row, grid=(1, s // tm), a_spec=pl.BlockSpec((tm, d), row2), b_spec=dd_weight(o_o),
                 extras=(x0,), extra_specs=(pl.BlockSpec((tm, d), row2),), epilogue=lambda v, r: v + r,
                 out_shape=jax.ShapeDtypeStruct((s, d), F32), o_spec=pl.BlockSpec((tm, d), row2), dims=NN)
        h2 = _rms_fwd(f"rms_mlp_{l}", x1, vec(g_mlp), tr)
        a_pre = _mm(f"mlp1_{l}", h2, g_1, grid=(N_DEV // 2, s // tm), a_spec=pl.BlockSpec((tm, d), row2),
                    b_spec=pl.BlockSpec((2, d, c1), lambda j, i: (j, 0, 0)), slabs="n",
                    out_shape=jax.ShapeDtypeStruct((s, f), BF16),
                    o_spec=pl.BlockSpec((tm, 2 * c1), lambda j, i: (i, j)), dims=NN)
        x2 = _mm(f"mlp2_{l}", a_pre, g_row, grid=(1, s // tr), a_spec=pl.BlockSpec((tr, f), row2),
                 b_spec=pl.BlockSpec((N_DEV, rf, d), lambda j, i: (0, 0, 0)), prologue=_relu_sq,
                 extras=(x1,), extra_specs=(pl.BlockSpec((tr, d), row2),), epilogue=lambda v, r: v + r,
                 out_shape=jax.ShapeDtypeStruct((s, d), F32), o_spec=pl.BlockSpec((tr, d), row2), dims=NN)
        saved.append((x0, h, proj, p_a, sw, p_c, cv, y_a, y_b, pw, merged, x1, h2, a_pre))
        xs.append(x2)

    loss_part, dx, dx16, dg_final = _loss_head(xs[-1], g_final.reshape(1, d), loss_target[0], tr)
    loss = lax.psum(loss_part[0, 0], ("x", "y", "c"))

    small = [None] * n_layers
    exchanges = [None] * n_layers
    for l in reversed(range(n_layers)):
        x0, h, proj, p_a, sw, p_c, cv, y_a, y_b, pw, merged, x1, h2, a_pre = saved[l]
        g_in, g_1, g_row, g_pool = weights[l]
        vec = lambda a: a[l:l + 1]
        row_shape = jax.ShapeDtypeStruct((N_DEV, rp, d), BF16)

        def dd_weight(off):
            return pl.BlockSpec((N_DEV, rd, d), lambda j, i: (0, off, 0))

        def dd_grad(name, a, g, off, alias):
            return _mm(name, a, g, grid=(1, s // tk), a_spec=pl.BlockSpec((tk, d), lambda j, k: (k, 0)),
                       b_spec=pl.BlockSpec((tk, d), lambda j, k: (k, 0)), out_shape=row_shape,
                       o_spec=pl.BlockSpec((N_DEV, rd, d), lambda j, k: (0, off, 0)), dims=TN, nk=s // tk,
                       acc_shape=(d, d), alias_in=alias)

        d_a = _mm(f"d_act_{l}", dx16, g_row, grid=(N_DEV // 2, s // tm), a_spec=pl.BlockSpec((tm, d), row2),
                  b_spec=pl.BlockSpec((2, rf, d), lambda j, i: (j, 0, 0)), slabs="n",
                  extras=(a_pre,), extra_specs=(pl.BlockSpec((tm, 2 * rf), lambda j, i: (i, j)),),
                  epilogue=lambda v, a: v * (2.0 * jnp.maximum(a.astype(F32), 0.0)),
                  out_shape=jax.ShapeDtypeStruct((s, f), BF16),
                  o_spec=pl.BlockSpec((tm, 2 * rf), lambda j, i: (i, j)), dims=NT)
        dg_row = _mm(f"dw_mlp2_{l}", a_pre, dx16, grid=(N_DEV, s // tk_mlp),
                     a_spec=pl.BlockSpec((tk_mlp, rf), lambda j, k: (k, j)),
                     b_spec=pl.BlockSpec((tk_mlp, d), lambda j, k: (k, 0)), prologue=_relu_sq, out_shape=row_shape,
                     o_spec=pl.BlockSpec((None, rf, d), lambda j, k: (j, 0, 0)), dims=TN, nk=s // tk_mlp,
                     acc_shape=(rf, d))
        d_h2 = _mm(f"d_h2_{l}", d_a, g_1, grid=(1, s // tr), a_spec=pl.BlockSpec((tr, f), row2),
                   b_spec=pl.BlockSpec((N_DEV, d, c1), lambda j, i: (0, 0, 0)), slabs="k",
                   out_shape=jax.ShapeDtypeStruct((s, d), BF16), o_spec=pl.BlockSpec((tr, d), row2), dims=NT)
        dg_1 = _mm(f"dw_mlp1_{l}", h2, d_a, grid=(N_DEV, s // tk_mlp),
                   a_spec=pl.BlockSpec((tk_mlp, d), lambda j, k: (k, 0)),
                   b_spec=pl.BlockSpec((tk_mlp, c1), lambda j, k: (k, j)),
                   out_shape=jax.ShapeDtypeStruct((N_DEV, d, c1), BF16),
                   o_spec=pl.BlockSpec((None, d, c1), lambda j, k: (j, 0, 0)), dims=TN, nk=s // tk_mlp,
                   acc_shape=(d, c1))
        dx, dx16, dg_mlp = _rms_bwd(f"rms_mlp_bwd_{l}", d_h2, x1, vec(g_mlp), dx, tr)
        d_merged = _mm(f"d_merged_{l}", dx16, g_row, grid=(1, s // tm), a_spec=pl.BlockSpec((tm, d), row2),
                       b_spec=dd_weight(o_o), out_shape=jax.ShapeDtypeStruct((s, d), BF16),
                       o_spec=pl.BlockSpec((tm, d), row2), dims=NT)
        dg_row = dd_grad(f"dw_o_{l}", merged, dx16, o_o, dg_row)
        d_ya, d_yb, d_pw, d_gates, d_bout, d_pscale = _merge_bwd(f"merge_bwd_{l}", d_merged, proj, y_a, y_b, pw,
                                                                 vec(pool_scale), d, tr)
        d_pa = _mm(f"d_pa_{l}", d_ya, g_row, grid=(1, s // tm), a_spec=pl.BlockSpec((tm, d), row2), b_spec=dd_weight(o_a),
                   out_shape=jax.ShapeDtypeStruct((s, d), BF16), o_spec=pl.BlockSpec((tm, d), row2), dims=NT)
        d_sw = _mm(f"d_sw_{l}", d_yb, g_row, grid=(1, s // tm), a_spec=pl.BlockSpec((tm, d), row2), b_spec=dd_weight(o_b),
                   out_shape=jax.ShapeDtypeStruct((s, d), BF16), o_spec=pl.BlockSpec((tm, d), row2), dims=NT)
        d_pc = _pool_mm(f"d_pool_{l}", d_pw, g_pool, d, tm, NT)
        dg_row = dd_grad(f"dw_out_a_{l}", p_a, d_ya, o_a, dg_row)
        dg_row = dd_grad(f"dw_out_b_{l}", sw, d_yb, o_b, dg_row)
        dg_pool = _pool_wgrad(f"dw_pool_{l}", p_c, d_pw, d, tk)
        rest_going = _start_copies(f"grads_start_rest_{l}", [dg_1, dg_row, dg_pool], vec(g_mlp), blockwise=True)
        d_proj, d_bin, d_ca, d_cb, d_cbb, d_lng, d_lnb = _mix_pre_bwd(
            f"mix_bwd_{l}", proj, cv, d_pa, d_sw, d_pc, d_gates, conv_a_f[l], conv_b_f[l],
            vec(ln_b_g) + rest_going["token"][0:1, 0:1], vec(ln_b_b), d, tx)
        dg_in = _mm(f"dw_in_{l}", h, d_proj, grid=(N_DEV, s // tk), a_spec=pl.BlockSpec((tk, d), lambda j, k: (k, 0)),
                    b_spec=pl.BlockSpec((tk, ci), lambda j, k: (k, j)),
                    out_shape=jax.ShapeDtypeStruct((N_DEV, d, ci), BF16),
                    o_spec=pl.BlockSpec((None, d, ci), lambda j, k: (j, 0, 0)), dims=TN, nk=s // tk,
                    acc_shape=(d, ci))
        in_going = _start_copies(f"grads_start_in_{l}", [dg_in], vec(g_mix), blockwise=True)
        d_h = _mm(f"d_h_{l}", d_proj, g_in, grid=(s // tm, N_DEV // 2),
                  a_spec=pl.BlockSpec((tm, 2 * ci), lambda i, k: (i, k)),
                  b_spec=pl.BlockSpec((2, d, ci), lambda i, k: (k, 0, 0)), slabs="k",
                  out_shape=jax.ShapeDtypeStruct((s, d), BF16), o_spec=pl.BlockSpec((tm, d), lambda i, k: (i, 0)),
                  dims=NT, nk=N_DEV // 2, acc_shape=(tm, d), after=in_going["token"])
        dx, dx16, dg_mix = _rms_bwd(f"rms_mix_bwd_{l}", d_h, x0, vec(g_mix), dx, tr)
        small[l] = (dg_mix, d_bin, d_cbb, d_lng, d_lnb, d_bout, d_pscale, dg_mlp, d_ca, d_cb)
        exchanges[l] = (in_going, rest_going)

    grad_x = dx[None]

    names = ("g_mix", "b_in", "conv_b_bias", "ln_b_g", "ln_b_b", "b_out_b", "pool_scale", "g_mlp")
    given = dict(g_mix=(g_mix, m_g_mix, v_g_mix), b_in=(b_in, m_b_in, v_b_in),
                 conv_b_bias=(conv_b_bias, m_conv_b_bias, v_conv_b_bias), ln_b_g=(ln_b_g, m_ln_b_g, v_ln_b_g),
                 ln_b_b=(ln_b_b, m_ln_b_b, v_ln_b_b), b_out_b=(b_out_b, m_b_out_b, v_b_out_b),
                 pool_scale=(pool_scale, m_pool_scale, v_pool_scale), g_mlp=(g_mlp, m_g_mlp, v_g_mlp))
    partials, triples = [], []
    for i, nm in enumerate(names):
        partials.append(jnp.concatenate([small[l][i] for l in range(n_layers)], axis=0))
        triples.append(given[nm])
    partials.append(dg_final)
    triples.append(tuple(a.reshape(1, d) for a in (g_final, m_g_final, v_g_final)))
    partials.append(jnp.concatenate([small[l][8] for l in range(n_layers)], axis=0))
    partials.append(jnp.concatenate([small[l][9] for l in range(n_layers)], axis=0))
    outs = _small_update(partials, triples, 2)
    rep = {nm: outs[4 * i:4 * i + 4] for i, nm in enumerate(names)}
    rep["g_final"] = [a.reshape(d) for a in outs[4 * len(names):4 * len(names) + 4]]
    me = _block_of(*_my_place())
    gca = lax.dynamic_slice_in_dim(outs[-2].reshape(n_layers, K_A, d), me * rd, rd, axis=2)
    gcb = lax.dynamic_slice_in_dim(outs[-1].reshape(n_layers, K_B, d), me * rd, rd, axis=2)

    r_in, r_1, r_row, r_pool = [], [], [], []
    for l in reversed(range(n_layers)):
        in_going, rest_going = exchanges[l]
        srcs_r, lands_r = _wait_copies(f"grads_wait_rest_{l}", rest_going, outs[0], blockwise=True)
        srcs_i, lands_i = _wait_copies(f"grads_wait_in_{l}", in_going, outs[0], blockwise=True)
        got = _place_own(f"grads_own_{l}", lands_i + lands_r, srcs_i + srcs_r, me_arr, blockwise=True)
        for lst, arr in zip((r_in, r_1, r_row, r_pool), got):
            lst.insert(0, arr)
    tb = min(256, d)
    layers = range(n_layers)
    res = {}
    res["w_in"] = _adamw("adamw_w_in", r_in, w_in, m_w_in, v_w_in, grid=(n_layers, d // tb),
                         part_specs=[_layer_part_spec(li, (N_DEV, tb, ci), d // tb) for li in layers],
                         w_spec=pl.BlockSpec((None, tb, ci), lambda l, i: (l, i, 0)))
    res["w_mlp1"] = _adamw("adamw_w_mlp1", r_1, w_mlp1, m_w_mlp1, v_w_mlp1, grid=(n_layers, d // tb),
                           part_specs=[_layer_part_spec(li, (N_DEV, tb, c1), d // tb) for li in layers],
                           w_spec=pl.BlockSpec((None, tb, c1), lambda l, i: (l, i, 0)))
    tf = min(256, rf)
    res["w_mlp2"] = _adamw("adamw_w_mlp2", r_row, w_mlp2, m_w_mlp2, v_w_mlp2, grid=(n_layers, rf // tf),
                           part_specs=[_layer_part_spec(li, (N_DEV, tf, d), rf // tf) for li in layers],
                           w_spec=pl.BlockSpec((None, tf, d), lambda l, i: (l, i, 0)))
    for nm, off, trip in (("w_out_a", o_a, (w_out_a, m_w_out_a, v_w_out_a)),
                          ("w_out_b", o_b, (w_out_b, m_w_out_b, v_w_out_b)), ("w_o", o_o, (w_o, m_w_o, v_w_o))):
        res[nm] = _adamw(f"adamw_{nm}", r_row, *trip, grid=(n_layers, 1),
                         part_specs=[_layer_part_spec(li, (N_DEV, rd, d), 1, row_off=off) for li in layers],
                         w_spec=pl.BlockSpec((None, rd, d), lambda l, i: (l, 0, 0)))
    res["w_pool"] = _adamw("adamw_w_pool", r_pool, w_pool, m_w_pool, v_w_pool, grid=(n_layers, 1),
                           part_specs=[_layer_part_spec(li, (N_DEV, N_GROUPS, gc // N_DEV, gc), 1) for li in layers],
                           w_spec=pl.BlockSpec((None, N_GROUPS, gc // N_DEV, gc), lambda l, i: (l, 0, 0, 0)))
    whole3 = lambda: (0, 0, 0)
    res["conv_a"] = _adamw("adamw_conv_a", [gca[None]], conv_a, m_conv_a, v_conv_a, grid=(),
                           part_specs=[pl.BlockSpec((1, n_layers, K_A, rd), lambda: (0, 0, 0, 0))],
                           w_spec=pl.BlockSpec((n_layers, K_A, rd), whole3))
    res["conv_b"] = _adamw("adamw_conv_b", [gcb[None]], conv_b, m_conv_b, v_conv_b, grid=(),
                           part_specs=[pl.BlockSpec((1, n_layers, K_B, rd), lambda: (0, 0, 0, 0))],
                           w_spec=pl.BlockSpec((n_layers, K_B, rd), whole3))
    res.update(rep)

    order = ("g_mix", "w_in", "b_in", "conv_a", "w_out_a", "conv_b", "conv_b_bias", "ln_b_g", "ln_b_b", "w_out_b",
             "b_out_b", "w_pool", "pool_scale", "w_o", "g_mlp", "w_mlp1", "w_mlp2", "g_final")
    out = [loss, grad_x]
    for kind in range(4):
        out += [res[nm][kind] for nm in order]
    return tuple(out)
```

```python
import jax
import jax.numpy as jnp
from jax import lax
from jax.experimental import pallas as pl
from jax.experimental.pallas import tpu as pltpu

F32 = jnp.float32
BF16 = jnp.bfloat16
MESH = pl.DeviceIdType.MESH

N_DEV = 8
EPS = 1e-6
K_A = 3
K_B = 31
POOL_WINDOWS = (2, 4, 8, 16)
N_GROUPS = len(POOL_WINDOWS)
HALO = 32
CHUNK = 16
SUBLANES = 8
TAP_GROUP = 4
ADAM_LR, ADAM_B1, ADAM_B2, ADAM_EPS, ADAM_WD, ADAM_STEP = 0.001, 0.9, 0.999, 1e-08, 0.01, 10
VMEM_LIMIT_BYTES = 60 * 1024 * 1024

NN = (((1,), (0,)), ((), ()))
NT = (((1,), (1,)), ((), ()))
TN = (((0,), (0,)), ((), ()))


def _params(*sem):
    return pltpu.CompilerParams(dimension_semantics=sem, vmem_limit_bytes=VMEM_LIMIT_BYTES)


def _sigmoid(v):
    return 1.0 / (1.0 + jnp.exp(-v))


def _mm(name, a, b, *, grid, a_spec, b_spec, out_shape, o_spec, dims, nk=1, acc_shape=None,
        extras=(), extra_specs=(), prologue=None, epilogue=None, alias_in=None, slabs=None, after=None):
    n_extra = len(extras)
    has_alias = alias_in is not None
    n_unread = (1 if has_alias else 0) + (1 if after is not None else 0)

    def body(*refs):
        a_ref, b_ref = refs[0], refs[1]
        ex = refs[2:2 + n_extra]
        o_ref = refs[2 + n_extra + n_unread]
        av = a_ref[...]
        if prologue is not None:
            av = prologue(av)
        av = av.astype(BF16)

        def finish(val, cols=None):
            if epilogue is not None:
                val = epilogue(val, *[e[...] if cols is None else e[:, cols] for e in ex])
            if cols is None:
                o_ref[...] = val.astype(o_ref.dtype).reshape(o_ref.shape)
            else:
                o_ref[:, cols] = val.astype(o_ref.dtype)

        if slabs == "n":
            for q in range(b_ref.shape[0]):
                pq = lax.dot_general(av, b_ref[q].astype(BF16), dims, preferred_element_type=F32)
                finish(pq, slice(q * pq.shape[1], (q + 1) * pq.shape[1]))
            return
        if slabs == "k":
            kc = av.shape[1] // b_ref.shape[0]
            p = None
            for q in range(b_ref.shape[0]):
                pq = lax.dot_general(av[:, q * kc:(q + 1) * kc], b_ref[q].astype(BF16), dims,
                                     preferred_element_type=F32)
                p = pq if p is None else p + pq
        else:
            bv = b_ref[...]
            bv = bv.reshape((-1, bv.shape[-1])).astype(BF16)
            p = lax.dot_general(av, bv, dims, preferred_element_type=F32)

        if nk == 1:
            finish(p)
        else:
            acc = refs[-1]
            k = pl.program_id(len(grid) - 1)

            @pl.when(k == 0)
            def _():
                acc[...] = p

            @pl.when(k > 0)
            def _():
                acc[...] += p

            @pl.when(k == nk - 1)
            def _():
                finish(acc[...])

    in_specs = [a_spec, b_spec, *extra_specs]
    operands = [a, b, *extras]
    aliases = {}
    if has_alias:
        in_specs.append(pl.BlockSpec(memory_space=pl.ANY))
        operands.append(alias_in)
        aliases = {len(operands) - 1: 0}
    if after is not None:
        in_specs.append(pl.BlockSpec(memory_space=pl.ANY))
        operands.append(after)
    sem =("parallel",) * (len(grid) - 1) + (("arbitrary",) if nk > 1 else ("parallel",))
    return pl.pallas_call(
        body, name=name, grid=grid, in_specs=in_specs, out_specs=o_spec, out_shape=out_shape,
        scratch_shapes=[pltpu.VMEM(acc_shape, F32)] if nk > 1 else [],
        input_output_aliases=aliases, compiler_params=_params(*sem),
    )(*operands)


def _relu_sq(v):
    r = jnp.maximum(v, 0)
    return r * r


def _rms_fwd(name, x, g, tm):
    s, d = x.shape

    def body(x_ref, g_ref, h_ref):
        xv = x_ref[...]
        r = lax.rsqrt(jnp.mean(xv * xv, axis=-1, keepdims=True) + EPS)
        h_ref[...] = (xv * r * g_ref[...]).astype(h_ref.dtype)

    return pl.pallas_call(
        body, name=name, grid=(s // tm,),
        in_specs=[pl.BlockSpec((tm, d), lambda i: (i, 0)), pl.BlockSpec((1, d), lambda i: (0, 0))],
        out_specs=pl.BlockSpec((tm, d), lambda i: (i, 0)),
        out_shape=jax.ShapeDtypeStruct((s, d), BF16), compiler_params=_params("parallel"),
    )(x, g)


def _colsum8(v):
    return jnp.sum(v.reshape(v.shape[0] // 8, 8, v.shape[1]), axis=0)


def _rms_bwd(name, dh, x, g, dres, tm):
    s, d = x.shape
    n = s // tm

    def body(dh_ref, x_ref, g_ref, dr_ref, dx_ref, dx16_ref, dg_ref, acc):
        i = pl.program_id(0)
        xv = x_ref[...]
        r = lax.rsqrt(jnp.mean(xv * xv, axis=-1, keepdims=True) + EPS)
        xh = xv * r
        dhv = dh_ref[...].astype(F32)
        part = _colsum8(dhv * xh)

        @pl.when(i == 0)
        def _():
            acc[...] = part

        @pl.when(i > 0)
        def _():
            acc[...] += part

        dxh = dhv * g_ref[...]
        dx = r * (dxh - xh * jnp.mean(dxh * xh, axis=-1, keepdims=True))
        dx = dx + dr_ref[...]
        dx_ref[...] = dx
        dx16_ref[...] = dx.astype(BF16)

        @pl.when(i == n - 1)
        def _():
            dg_ref[...] = jnp.sum(acc[...], axis=0, keepdims=True)

    return pl.pallas_call(
        body, name=name, grid=(n,),
        in_specs=[pl.BlockSpec((tm, d), lambda i: (i, 0)), pl.BlockSpec((tm, d), lambda i: (i, 0)),
                  pl.BlockSpec((1, d), lambda i: (0, 0)), pl.BlockSpec((tm, d), lambda i: (i, 0))],
        out_specs=[pl.BlockSpec((tm, d), lambda i: (i, 0)), pl.BlockSpec((tm, d), lambda i: (i, 0)),
                   pl.BlockSpec((1, d), lambda i: (0, 0))],
        out_shape=[jax.ShapeDtypeStruct((s, d), F32), jax.ShapeDtypeStruct((s, d), BF16),
                   jax.ShapeDtypeStruct((1, d), F32)],
        scratch_shapes=[pltpu.VMEM((8, d), F32)], compiler_params=_params("arbitrary"),
    )(dh, x, g, dres)


def _loss_head(x, g, target, tm):
    s, d = x.shape
    n = s // tm

    def body(x_ref, g_ref, t_ref, loss_ref, dx_ref, dx16_ref, dg_ref, acc_l, acc_g):
        i = pl.program_id(0)
        xv = x_ref[...]
        r = lax.rsqrt(jnp.mean(xv * xv, axis=-1, keepdims=True) + EPS)
        xh = xv * r
        err = xh * g_ref[...] - t_ref[...]
        dy = err * (1.0 / d)
        lpart = _colsum8(err * err)
        gpart = _colsum8(dy * xh)

        @pl.when(i == 0)
        def _():
            acc_l[...] = lpart
            acc_g[...] = gpart

        @pl.when(i > 0)
        def _():
            acc_l[...] += lpart
            acc_g[...] += gpart

        dxh = dy * g_ref[...]
        dx = r * (dxh - xh * jnp.mean(dxh * xh, axis=-1, keepdims=True))
        dx_ref[...] = dx
        dx16_ref[...] = dx.astype(BF16)

        @pl.when(i == n - 1)
        def _():
            loss_ref[...] = (0.5 / d) * jnp.sum(jnp.sum(acc_l[...], axis=0, keepdims=True), axis=1, keepdims=True)
            dg_ref[...] = jnp.sum(acc_g[...], axis=0, keepdims=True)

    return pl.pallas_call(
        body, name="loss_head", grid=(n,),
        in_specs=[pl.BlockSpec((tm, d), lambda i: (i, 0)), pl.BlockSpec((1, d), lambda i: (0, 0)),
                  pl.BlockSpec((tm, d), lambda i: (i, 0))],
        out_specs=[pl.BlockSpec((1, 1), lambda i: (0, 0)), pl.BlockSpec((tm, d), lambda i: (i, 0)),
                   pl.BlockSpec((tm, d), lambda i: (i, 0)), pl.BlockSpec((1, d), lambda i: (0, 0))],
        out_shape=[jax.ShapeDtypeStruct((1, 1), F32), jax.ShapeDtypeStruct((s, d), F32),
                   jax.ShapeDtypeStruct((s, d), BF16), jax.ShapeDtypeStruct((1, d), F32)],
        scratch_shapes=[pltpu.VMEM((8, d), F32), pltpu.VMEM((8, d), F32)], compiler_params=_params("arbitrary"),
    )(x, g, target)


def _sec(ref, n, d):
    return ref[:, n * d:(n + 1) * d].astype(F32)


def _fill_shifts(sh, ext):
    rows = ext.shape[0] - SUBLANES
    for b in range(1, SUBLANES):
        sh[b - 1, 0:rows, :] = ext[b:b + rows, :]


def _shifted(sh, ext, off, n):
    b = off % SUBLANES
    if b == 0:
        return ext[off:off + n, :]
    return sh[b - 1, off - b:off - b + n, :]


def _pool_count(row0, rows, window):
    t = row0 + lax.broadcasted_iota(jnp.int32, (rows, 1), 0)
    return jnp.minimum(t + 1, window).astype(F32)


def _group_weight(w_ref, gi, gc):
    return w_ref[:, gi].reshape(gc, gc)


def _mixer_fwd(name, proj, x0, conv_a, conv_b, conv_b_bias, ln_g, ln_b, b_out_b, pool_scale, g_row, g_pool, offs, d, tm):
    s = proj.shape[0]
    n = s // tm
    gc = d // N_GROUPS
    hb = tm // HALO
    rd = d // N_DEV
    o_a, o_b, o_o = offs

    def body(pj_ref, hp_ref, x0_ref, ca_ref, cb_ref, cbb_ref, lng_ref, lnb_ref, bo_ref, sc_ref, wa_ref, wb_ref, wo_ref,
             wp_ref, pa_ref, sw_ref, pc_ref, cv_ref, ya_ref, yb_ref, pw_ref, mg_ref, x1_ref, eua, eub, euc, sh):
        i = pl.program_id(0)
        keep = (i > 0).astype(F32)
        eua[0:HALO, :] = _sec(hp_ref, 1, d) * _sec(hp_ref, 2, d) * keep
        eub[0:HALO, :] = _sec(hp_ref, 3, d) * _sigmoid(_sec(hp_ref, 4, d)) * keep
        euc[0:HALO, :] = _sec(hp_ref, 5, d) * keep
        eua[HALO:HALO + tm, :] = _sec(pj_ref, 1, d) * _sec(pj_ref, 2, d)
        eub[HALO:HALO + tm, :] = _sec(pj_ref, 3, d) * _sigmoid(_sec(pj_ref, 4, d))
        euc[HALO:HALO + tm, :] = _sec(pj_ref, 5, d)
        _fill_shifts(sh, eub)
        for c in range(tm // CHUNK):
            r0 = c * CHUNK
            z = jnp.zeros((CHUNK, d), F32)
            for k in range(K_A):
                z = z + ca_ref[k:k + 1, :] * eua[HALO + r0 - (K_A - 1) + k:HALO + r0 - (K_A - 1) + k + CHUNK, :]
            pa_ref[r0:r0 + CHUNK, :] = (pj_ref[r0:r0 + CHUNK, 0:d].astype(F32) * z).astype(pa_ref.dtype)
            cv = jnp.zeros((CHUNK, d), F32) + cbb_ref[...]
            for k in range(K_B):
                cv = cv + cb_ref[k:k + 1, :] * _shifted(sh, eub, HALO + r0 - (K_B - 1) + k, CHUNK)
            cv_ref[r0:r0 + CHUNK, :] = cv.astype(cv_ref.dtype)
        cvv = cv_ref[...].astype(F32)
        mu = jnp.mean(cvv, axis=-1, keepdims=True)
        xc = cvv - mu
        xh = xc * lax.rsqrt(jnp.mean(xc * xc, axis=-1, keepdims=True) + EPS)
        ln = xh * lng_ref[...] + lnb_ref[...]
        sw_ref[...] = (ln * _sigmoid(ln)).astype(sw_ref.dtype)
        for gi, w in enumerate(POOL_WINDOWS):
            cols = slice(gi * gc, (gi + 1) * gc)
            tot = euc[HALO:HALO + tm, cols]
            for k in range(1, w):
                tot = tot + euc[HALO - k:HALO - k + tm, cols]
            cnt = _pool_count(i * tm, tm, w)
            pc_ref[:, cols] = (tot / cnt - euc[HALO:HALO + tm, cols]).astype(pc_ref.dtype)
        ya_ref[...] = jnp.dot(pa_ref[...], wa_ref[...].reshape(d, d), preferred_element_type=F32).astype(ya_ref.dtype)
        yb_ref[...] = (jnp.dot(sw_ref[...], wb_ref[...].reshape(d, d), preferred_element_type=F32)
                       + bo_ref[...]).astype(yb_ref.dtype)
        for gi in range(N_GROUPS):
            cols = slice(gi * gc, (gi + 1) * gc)
            pw_ref[:, cols] = jnp.dot(pc_ref[:, cols], _group_weight(wp_ref, gi, gc),
                                      preferred_element_type=F32).astype(pw_ref.dtype)
        m = _sigmoid(_sec(pj_ref, 6, d)) * ya_ref[...].astype(F32)
        m = m + _sigmoid(_sec(pj_ref, 7, d)) * yb_ref[...].astype(F32)
        m = m + _sigmoid(_sec(pj_ref, 8, d)) * (pw_ref[...].astype(F32) * sc_ref[...])
        mg_ref[...] = m.astype(mg_ref.dtype)
        x1_ref[...] = x0_ref[...] + jnp.dot(mg_ref[...], wo_ref[...].reshape(d, d), preferred_element_type=F32)

    row = lambda i: (i, 0)
    fixed = lambda i: (0, 0)
    act = jax.ShapeDtypeStruct((s, d), BF16)

    def dd_weight(off):
        return pl.BlockSpec((N_DEV, rd, d), lambda i: (0, off, 0), pipeline_mode=pl.Buffered(1))

    return pl.pallas_call(
        body, name=name, grid=(n,),
        in_specs=[pl.BlockSpec((tm, 9 * d), row),
                  pl.BlockSpec((HALO, 6 * d), lambda i: (jnp.maximum(i * hb - 1, 0), 0)),
                  pl.BlockSpec((tm, d), row),
                  pl.BlockSpec((K_A, d), fixed), pl.BlockSpec((K_B, d), fixed), pl.BlockSpec((1, d), fixed),
                  pl.BlockSpec((1, d), fixed), pl.BlockSpec((1, d), fixed), pl.BlockSpec((1, d), fixed),
                  pl.BlockSpec((1, d), fixed), dd_weight(o_a), dd_weight(o_b), dd_weight(o_o),
                  pl.BlockSpec((N_DEV, N_GROUPS, gc // N_DEV, gc), lambda i: (0, 0, 0, 0),
                               pipeline_mode=pl.Buffered(1))],
        out_specs=[pl.BlockSpec((tm, d), row)] * 9,
        out_shape=[act] * 8 + [jax.ShapeDtypeStruct((s, d), F32)],
        scratch_shapes=[pltpu.VMEM((tm + HALO, d), F32)] * 3 + [pltpu.VMEM((SUBLANES - 1, tm + HALO, d), F32)],
        compiler_params=_params("parallel"),
    )(proj, proj, x0, conv_a, conv_b, conv_b_bias, ln_g, ln_b, b_out_b, pool_scale, g_row, g_row, g_row, g_pool)


def _merge_bwd(name, dx16, proj, ya, yb, pw, pool_scale, g_row, g_pool, offs, d, tm):
    s = proj.shape[0]
    n = s // tm
    gc = d // N_GROUPS
    rd = d // N_DEV
    o_a, o_b, o_o = offs

    def body(dx_ref, g_ref, ya_ref, yb_ref, pw_ref, sc_ref, wa_ref, wb_ref, wo_ref, wp_ref,
             dya_ref, dyb_ref, dpw_ref, dg_ref, dpa_ref, dsw_ref, dpc_ref, dbo_ref, dsc_ref, acc_b, acc_s):
        i = pl.program_id(0)
        dmv = lax.dot_general(dx_ref[...], wo_ref[...].reshape(d, d), NT,
                              preferred_element_type=F32).astype(BF16).astype(F32)
        scale = sc_ref[...]
        g0 = _sigmoid(_sec(g_ref, 0, d))
        dya_ref[...] = (dmv * g0).astype(dya_ref.dtype)
        dg_ref[:, 0:d] = (dmv * ya_ref[...].astype(F32) * g0 * (1.0 - g0)).astype(dg_ref.dtype)
        g1 = _sigmoid(_sec(g_ref, 1, d))
        dyb = dmv * g1
        dyb_ref[...] = dyb.astype(dyb_ref.dtype)
        dg_ref[:, d:2 * d] = (dmv * yb_ref[...].astype(F32) * g1 * (1.0 - g1)).astype(dg_ref.dtype)
        g2 = _sigmoid(_sec(g_ref, 2, d))
        pwv = pw_ref[...].astype(F32)
        dyc = dmv * g2
        dpw_ref[...] = (dyc * scale).astype(dpw_ref.dtype)
        dg_ref[:, 2 * d:3 * d] = (dmv * (pwv * scale) * g2 * (1.0 - g2)).astype(dg_ref.dtype)
        pb = _colsum8(dyb)
        ps = _colsum8(dyc * pwv)

        @pl.when(i == 0)
        def _():
            acc_b[...] = pb
            acc_s[...] = ps

        @pl.when(i > 0)
        def _():
            acc_b[...] += pb
            acc_s[...] += ps

        dpa_ref[...] = lax.dot_general(dya_ref[...], wa_ref[...].reshape(d, d), NT,
                                       preferred_element_type=F32).astype(dpa_ref.dtype)
        dsw_ref[...] = lax.dot_general(dyb_ref[...], wb_ref[...].reshape(d, d), NT,
                                       preferred_element_type=F32).astype(dsw_ref.dtype)
        for gi in range(N_GROUPS):
            cols = slice(gi * gc, (gi + 1) * gc)
            dpc_ref[:, cols] = lax.dot_general(dpw_ref[:, cols], _group_weight(wp_ref, gi, gc), NT,
                                               preferred_element_type=F32).astype(dpc_ref.dtype)

        @pl.when(i == n - 1)
        def _():
            dbo_ref[...] = jnp.sum(acc_b[...], axis=0, keepdims=True)
            dsc_ref[...] = jnp.sum(acc_s[...], axis=0, keepdims=True)

    row = lambda i: (i, 0)
    fixed = lambda i: (0, 0)
    act = jax.ShapeDtypeStruct((s, d), BF16)
    vec = jax.ShapeDtypeStruct((1, d), F32)

    def dd_weight(off):
        return pl.BlockSpec((N_DEV, rd, d), lambda i: (0, off, 0), pipeline_mode=pl.Buffered(1))

    return pl.pallas_call(
        body, name=name, grid=(n,),
        in_specs=[pl.BlockSpec((tm, d), row), pl.BlockSpec((tm, 3 * d), lambda i: (i, 2)), pl.BlockSpec((tm, d), row),
                  pl.BlockSpec((tm, d), row), pl.BlockSpec((tm, d), row), pl.BlockSpec((1, d), fixed),
                  dd_weight(o_a), dd_weight(o_b), dd_weight(o_o),
                  pl.BlockSpec((N_DEV, N_GROUPS, gc // N_DEV, gc), lambda i: (0, 0, 0, 0),
                               pipeline_mode=pl.Buffered(1))],
        out_specs=[pl.BlockSpec((tm, d), row)] * 3 + [pl.BlockSpec((tm, 3 * d), row)] + [pl.BlockSpec((tm, d), row)] * 3
                  + [pl.BlockSpec((1, d), fixed)] * 2,
        out_shape=[act, act, act, jax.ShapeDtypeStruct((s, 3 * d), BF16), act, act, act, vec, vec],
        scratch_shapes=[pltpu.VMEM((8, d), F32)] * 2, compiler_params=_params("arbitrary"),
    )(dx16, proj, ya, yb, pw, pool_scale, g_row, g_row, g_row, g_pool)


def _mix_pre_bwd(name, proj, cv, dpa, dsw, dpc, dgates, conv_a, conv_b, ln_g, ln_b, d, tm):
    s = proj.shape[0]
    n = s // tm
    gc = d // N_GROUPS
    hb = tm // HALO
    last_halo = s // HALO - 1
    te = tm + HALO

    def ln_bwd(cvv, dswv, lng, lnb):
        mu = jnp.mean(cvv, axis=-1, keepdims=True)
        xc = cvv - mu
        rstd = lax.rsqrt(jnp.mean(xc * xc, axis=-1, keepdims=True) + EPS)
        xh = xc * rstd
        ln = xh * lng + lnb
        sg = _sigmoid(ln)
        dln = dswv * (sg * (1.0 + ln * (1.0 - sg)))
        dxh = dln * lng
        dcv = rstd * (dxh - jnp.mean(dxh, axis=-1, keepdims=True) - xh * jnp.mean(dxh * xh, axis=-1, keepdims=True))
        return dcv, dln, xh

    def body(pj_ref, hp_ref, hf_ref, cv_ref, cvf_ref, dpa_ref, dpaf_ref, dsw_ref, dswf_ref, dpc_ref, dpcf_ref, dgt_ref,
             ca_ref, cb_ref, lng_ref, lnb_ref,
             dpj_ref, dbin_ref, dca_ref, dcb_ref, dcbb_ref, dlng_ref, dlnb_ref,
             eua, eub, edz, edcv, eq, sh, acc_bin, acc_ca, acc_cb, acc_v):
        i = pl.program_id(0)
        keep_p = (i > 0).astype(F32)
        keep_f = (i < n - 1).astype(F32)

        @pl.when(i == 0)
        def _():
            acc_bin[...] = jnp.zeros_like(acc_bin)
            acc_ca[...] = jnp.zeros_like(acc_ca)
            acc_cb[...] = jnp.zeros_like(acc_cb)
            acc_v[...] = jnp.zeros_like(acc_v)

        eua[0:HALO, :] = _sec(hp_ref, 1, d) * _sec(hp_ref, 2, d) * keep_p
        eub[0:HALO, :] = _sec(hp_ref, 3, d) * _sigmoid(_sec(hp_ref, 4, d)) * keep_p
        eua[HALO:te, :] = _sec(pj_ref, 1, d) * _sec(pj_ref, 2, d)
        eub[HALO:te, :] = _sec(pj_ref, 3, d) * _sigmoid(_sec(pj_ref, 4, d))
        edz[0:tm, :] = dpa_ref[...].astype(F32) * _sec(pj_ref, 0, d)
        edz[tm:te, :] = dpaf_ref[...].astype(F32) * _sec(hf_ref, 0, d) * keep_f
        dcv, dln, xh = ln_bwd(cv_ref[...].astype(F32), dsw_ref[...].astype(F32), lng_ref[...], lnb_ref[...])
        edcv[0:tm, :] = dcv
        acc_v[0:8, :] += _colsum8(dcv)
        acc_v[8:16, :] += _colsum8(dln * xh)
        acc_v[16:24, :] += _colsum8(dln)
        dcvf, _, _ = ln_bwd(cvf_ref[...].astype(F32), dswf_ref[...].astype(F32), lng_ref[...], lnb_ref[...])
        edcv[tm:te, :] = dcvf * keep_f
        for gi, w in enumerate(POOL_WINDOWS):
            cols = slice(gi * gc, (gi + 1) * gc)
            eq[0:tm, cols] = dpc_ref[:, cols].astype(F32) / _pool_count(i * tm, tm, w)
            eq[tm:te, cols] = dpcf_ref[:, cols].astype(F32) / _pool_count((i + 1) * tm, HALO, w) * keep_f

        def put(sec_idx, r0, val):
            dpj_ref[r0:r0 + CHUNK, sec_idx * d:(sec_idx + 1) * d] = val.astype(dpj_ref.dtype)
            acc_bin[:, sec_idx * d:(sec_idx + 1) * d] += _colsum8(val)

        _fill_shifts(sh, edcv)
        for c in range(tm // CHUNK):
            r0 = c * CHUNK
            rows = slice(r0, r0 + CHUNK)
            z = jnp.zeros((CHUNK, d), F32)
            dua = jnp.zeros((CHUNK, d), F32)
            for k in range(K_A):
                z = z + ca_ref[k:k + 1, :] * eua[HALO + r0 - (K_A - 1) + k:HALO + r0 - (K_A - 1) + k + CHUNK, :]
                dua = dua + ca_ref[k:k + 1, :] * edz[r0 + (K_A - 1) - k:r0 + (K_A - 1) - k + CHUNK, :]
            put(0, r0, dpa_ref[rows, :].astype(F32) * z)
            put(1, r0, dua * pj_ref[rows, 2 * d:3 * d].astype(F32))
            put(2, r0, dua * pj_ref[rows, d:2 * d].astype(F32))
            dub = jnp.zeros((CHUNK, d), F32)
            for k in range(K_B):
                dub = dub + cb_ref[k:k + 1, :] * _shifted(sh, edcv, r0 + (K_B - 1) - k, CHUNK)
            bval = pj_ref[rows, 3 * d:4 * d].astype(F32)
            sg = _sigmoid(pj_ref[rows, 4 * d:5 * d].astype(F32))
            put(3, r0, dub * sg)
            put(4, r0, dub * bval * sg * (1.0 - sg))
            for gi, w in enumerate(POOL_WINDOWS):
                cols = slice(gi * gc, (gi + 1) * gc)
                tot = eq[rows, cols]
                for k in range(1, w):
                    tot = tot + eq[r0 + k:r0 + k + CHUNK, cols]
                dci = tot - dpc_ref[rows, cols].astype(F32)
                dpj_ref[rows, 5 * d + gi * gc:5 * d + (gi + 1) * gc] = dci.astype(dpj_ref.dtype)
                acc_bin[:, 5 * d + gi * gc:5 * d + (gi + 1) * gc] += _colsum8(dci)
        for q in range(3):
            gv = dgt_ref[:, q * d:(q + 1) * d]
            dpj_ref[:, (6 + q) * d:(7 + q) * d] = gv
            acc_bin[:, (6 + q) * d:(7 + q) * d] += _colsum8(gv.astype(F32))
        for k in range(K_A):
            a = jnp.zeros((8, d), F32)
            for c in range(tm // CHUNK):
                r0 = c * CHUNK
                a = a + _colsum8(edz[r0:r0 + CHUNK, :] * eua[HALO + r0 - (K_A - 1) + k:HALO + r0 - (K_A - 1) + k + CHUNK, :])
            acc_ca[k] += a
        _fill_shifts(sh, eub)
        for k0 in range(0, K_B, TAP_GROUP):
            taps = range(k0, min(k0 + TAP_GROUP, K_B))
            a = {k: jnp.zeros((8, d), F32) for k in taps}
            for c in range(tm // CHUNK):
                r0 = c * CHUNK
                dc = edcv[r0:r0 + CHUNK, :]
                for k in taps:
                    a[k] = a[k] + _colsum8(dc * _shifted(sh, eub, HALO + r0 - (K_B - 1) + k, CHUNK))
            for k in taps:
                acc_cb[k] += a[k]

        @pl.when(i == n - 1)
        def _():
            dbin_ref[...] = jnp.sum(acc_bin[...], axis=0, keepdims=True)
            for k in range(K_A):
                dca_ref[k:k + 1, :] = jnp.sum(acc_ca[k], axis=0, keepdims=True)
            for k in range(K_B):
                dcb_ref[k:k + 1, :] = jnp.sum(acc_cb[k], axis=0, keepdims=True)
            dcbb_ref[...] = jnp.sum(acc_v[0:8, :], axis=0, keepdims=True)
            dlng_ref[...] = jnp.sum(acc_v[8:16, :], axis=0, keepdims=True)
            dlnb_ref[...] = jnp.sum(acc_v[16:24, :], axis=0, keepdims=True)

    row = lambda i: (i, 0)
    fixed = lambda i: (0, 0)
    past = lambda i: (jnp.maximum(i * hb - 1, 0), 0)
    fut = lambda i: (jnp.minimum((i + 1) * hb, last_halo), 0)
    vec = jax.ShapeDtypeStruct((1, d), F32)
    tile_and_halo = [pl.BlockSpec((tm, d), row), pl.BlockSpec((HALO, d), fut)]
    return pl.pallas_call(
        body, name=name, grid=(n,),
        in_specs=[pl.BlockSpec((tm, 6 * d), row), pl.BlockSpec((HALO, 6 * d), past), pl.BlockSpec((HALO, 6 * d), fut),
                  *tile_and_halo, *tile_and_halo, *tile_and_halo, *tile_and_halo,
                  pl.BlockSpec((tm, 3 * d), row),
                  pl.BlockSpec((K_A, d), fixed), pl.BlockSpec((K_B, d), fixed), pl.BlockSpec((1, d), fixed),
                  pl.BlockSpec((1, d), fixed)],
        out_specs=[pl.BlockSpec((tm, 9 * d), row), pl.BlockSpec((1, 9 * d), fixed), pl.BlockSpec((K_A, d), fixed),
                   pl.BlockSpec((K_B, d), fixed), pl.BlockSpec((1, d), fixed), pl.BlockSpec((1, d), fixed),
                   pl.BlockSpec((1, d), fixed)],
        out_shape=[jax.ShapeDtypeStruct((s, 9 * d), BF16), jax.ShapeDtypeStruct((1, 9 * d), F32),
                   jax.ShapeDtypeStruct((K_A, d), F32), jax.ShapeDtypeStruct((K_B, d), F32), vec, vec, vec],
        scratch_shapes=[pltpu.VMEM((te, d), F32)] * 5 + [pltpu.VMEM((SUBLANES - 1, te, d), F32),
                                                         pltpu.VMEM((8, 9 * d), F32), pltpu.VMEM((K_A, 8, d), F32),
                                                         pltpu.VMEM((K_B, 8, d), F32), pltpu.VMEM((24, d), F32)],
        compiler_params=_params("arbitrary"),
    )(proj, proj, proj, cv, cv, dpa, dpa, dsw, dsw, dpc, dpc, dgates, conv_a, conv_b, ln_g, ln_b)


def _pool_wgrad(name, p, dpw, d, tk):
    s = p.shape[0]
    gc = d // N_GROUPS
    n = s // tk

    def body(p_ref, g_ref, o_ref, acc):
        k = pl.program_id(0)
        for gi in range(N_GROUPS):
            cols = slice(gi * gc, (gi + 1) * gc)
            part = lax.dot_general(p_ref[:, cols], g_ref[:, cols], TN, preferred_element_type=F32)

            @pl.when(k == 0)
            def _():
                acc[gi] = part

            @pl.when(k > 0)
            def _():
                acc[gi] += part

        @pl.when(k == n - 1)
        def _():
            for gi in range(N_GROUPS):
                o_ref[:, gi] = acc[gi].astype(o_ref.dtype).reshape(N_DEV, gc // N_DEV, gc)

    return pl.pallas_call(
        body, name=name, grid=(n,),
        in_specs=[pl.BlockSpec((tk, d), lambda k: (k, 0)), pl.BlockSpec((tk, d), lambda k: (k, 0))],
        out_specs=pl.BlockSpec((N_DEV, N_GROUPS, gc // N_DEV, gc), lambda k: (0, 0, 0, 0)),
        out_shape=jax.ShapeDtypeStruct((N_DEV, N_GROUPS, gc // N_DEV, gc), BF16),
        scratch_shapes=[pltpu.VMEM((N_GROUPS, gc, gc), F32)], compiler_params=_params("arbitrary"),
    )(p, dpw)


def _my_place():
    x, y, c = lax.axis_index("x"), lax.axis_index("y"), lax.axis_index("c")
    return x, y, c


def _block_of(x, y, c):
    return 4 * x + 2 * y + c


def _gather_shards(shards):
    n_arr = len(shards)

    def body(*refs):
        srcs = refs[:n_arr]
        outs = refs[n_arr:2 * n_arr]
        send_sems, recv_sems, local_sems = refs[2 * n_arr:]
        x, y, c = _my_place()
        me, sibling = (x, y, c), (x, y, 1 - c)
        chips = [(1 - x, y), (x, 1 - y), (1 - x, 1 - y)]

        def copy(n, k, block, to, src=None):
            rows = outs[n].at[_block_of(*block)]
            return pltpu.make_async_remote_copy(
                src_ref=rows if src is None else src, dst_ref=rows, send_sem=send_sems.at[n, k],
                recv_sem=recv_sems.at[n, k], device_id=to, device_id_type=MESH)

        mine = [pltpu.make_async_copy(srcs[n], outs[n].at[_block_of(*me)], local_sems.at[n]) for n in range(n_arr)]
        for cp in mine:
            cp.start()
        first = []
        for n in range(n_arr):
            first.append(copy(n, 0, me, sibling, src=srcs[n]))
            first += [copy(n, 1 + j, me, (*chip, c), src=srcs[n]) for j, chip in enumerate(chips)]
        for cp in first:
            cp.start()
        passed = []
        for n in range(n_arr):
            for j, chip in enumerate(chips):
                copy(n, 1 + j, (*chip, c), me).wait_recv()
                fwd = copy(n, 4 + j, (*chip, c), sibling)
                fwd.start()
                passed.append(fwd)
        for n in range(n_arr):
            copy(n, 0, sibling, me).wait_recv()
            for j, chip in enumerate(chips):
                copy(n, 4 + j, (*chip, 1 - c), me).wait_recv()
        for cp in first + passed:
            cp.wait_send()
        for cp in mine:
            cp.wait()

    any_spec = pl.BlockSpec(memory_space=pl.ANY)
    return pl.pallas_call(
        body, name="gather_weights",
        in_specs=[any_spec] * n_arr, out_specs=[any_spec] * n_arr,
        out_shape=[jax.ShapeDtypeStruct((N_DEV, *sh.shape), sh.dtype) for sh in shards],
        scratch_shapes=[pltpu.SemaphoreType.DMA((n_arr, 7)), pltpu.SemaphoreType.DMA((n_arr, 7)),
                        pltpu.SemaphoreType.DMA((n_arr,))],
    )(*shards)


def _peers(x, y, c):
    out = []
    for r in range(1, N_DEV):
        fx, fy, fc = (r >> 2) & 1, (r >> 1) & 1, r & 1
        out.append(((1 - x) if fx else x, (1 - y) if fy else y, (1 - c) if fc else c))
    return out


HBM_SPEC = pl.BlockSpec(memory_space=pltpu.HBM)
SEM_SPEC = pl.BlockSpec(memory_space=pltpu.SEMAPHORE)
ANY_SPEC = pl.BlockSpec(memory_space=pl.ANY)
N_PEERS = N_DEV - 1


def _peer_copy(src_ref, land_ref, send_sems, recv_sems, i, r, peer, me, blockwise):
    src = src_ref.at[_block_of(*peer)] if blockwise else src_ref
    return pltpu.make_async_remote_copy(
        src_ref=src, dst_ref=land_ref.at[me], send_sem=send_sems.at[i * N_PEERS + r],
        recv_sem=recv_sems.at[i * N_PEERS + r], device_id=peer, device_id_type=MESH)


def _start_copies(name, srcs, after, blockwise):
    n = len(srcs)

    def body(*refs):
        s_in, l_in = refs[:n], refs[n:2 * n]
        send_sems, recv_sems = refs[2 * n + 1], refs[2 * n + 2]
        token = refs[-1]
        x, y, c = _my_place()
        me = _block_of(x, y, c)
        for i in range(n):
            for r, peer in enumerate(_peers(x, y, c)):
                _peer_copy(s_in[i], l_in[i], send_sems, recv_sems, i, r, peer, me, blockwise).start()
        token[...] = jnp.zeros_like(token)

    land_shapes = [s.shape if blockwise else (N_DEV, *s.shape) for s in srcs]
    lands = [pltpu.with_memory_space_constraint(lax.empty(sh, s.dtype), pltpu.HBM) for sh, s in zip(land_shapes, srcs)]
    ins = [pltpu.with_memory_space_constraint(s, pltpu.HBM) for s in srcs]
    out = pl.pallas_call(
        body, name=name,
        out_shape=(pltpu.SemaphoreType.DMA((n * N_PEERS,)), pltpu.SemaphoreType.DMA((n * N_PEERS,)),
                   *[pltpu.HBM(s.shape, s.dtype) for s in srcs],
                   *[pltpu.HBM(sh, s.dtype) for sh, s in zip(land_shapes, srcs)],
                   jax.ShapeDtypeStruct((8, 128), F32)),
        in_specs=[HBM_SPEC] * (2 * n) + [ANY_SPEC],
        out_specs=(SEM_SPEC, SEM_SPEC, *[HBM_SPEC] * (2 * n), pl.BlockSpec(memory_space=pltpu.VMEM)),
        input_output_aliases={i: 2 + i for i in range(2 * n)},
        compiler_params=pltpu.CompilerParams(has_side_effects=pltpu.SideEffectType.DATAFLOW_SIDE_EFFECTING),
    )(*ins, *lands, after)
    return dict(send=out[0], recv=out[1], srcs=list(out[2:2 + n]), lands=list(out[2 + n:2 + 2 * n]), token=out[-1])


def _wait_copies(name, state, after, blockwise):
    n = len(state["srcs"])

    def body(*refs):
        s_in, l_in = refs[:n], refs[n:2 * n]
        send_sems, recv_sems = refs[2 * n], refs[2 * n + 1]
        x, y, c = _my_place()
        me = _block_of(x, y, c)
        for i in range(n):
            for r, peer in enumerate(_peers(x, y, c)):
                cp = _peer_copy(s_in[i], l_in[i], send_sems, recv_sems, i, r, peer, me, blockwise)
                cp.wait_send()
                cp.wait_recv()

    both = state["srcs"] + state["lands"]
    out = pl.pallas_call(
        body, name=name, out_shape=tuple(pltpu.HBM(a.shape, a.dtype) for a in both),
        in_specs=[HBM_SPEC] * (2 * n) + [SEM_SPEC, SEM_SPEC, ANY_SPEC], out_specs=tuple([HBM_SPEC] * (2 * n)),
        input_output_aliases={i: i for i in range(2 * n)},
        compiler_params=pltpu.CompilerParams(has_side_effects=pltpu.SideEffectType.DATAFLOW_SIDE_EFFECTING),
    )(*both, state["send"], state["recv"], after)
    return list(out[:n]), list(out[n:])


COPY_BLOCK_BYTES = 2 * 1024 * 1024


def _place_own(name, lands, srcs, me, blockwise):
    out = []
    for i, (land, src) in enumerate(zip(lands, srcs)):
        part = land.shape[1:]
        row_bytes = land.dtype.itemsize
        for extent in part[1:]:
            row_bytes *= extent
        tr = part[0]
        while tr * row_bytes > COPY_BLOCK_BYTES and tr % 16 == 0:
            tr //= 2
        tail = (0,) * (len(part) - 1)

        def body(me_ref, s_ref, l_ref, o_ref):
            o_ref[...] = s_ref[...]

        if blockwise:
            s_spec = pl.BlockSpec((None, tr, *part[1:]), lambda j, me_ref: (me_ref[0], j, *tail))
        else:
            s_spec = pl.BlockSpec((tr, *part[1:]), lambda j, me_ref: (j, *tail))
        out.append(pl.pallas_call(
            body, name=f"{name}_{i}",
            grid_spec=pltpu.PrefetchScalarGridSpec(
                num_scalar_prefetch=1, grid=(part[0] // tr,), in_specs=[s_spec, ANY_SPEC],
                out_specs=pl.BlockSpec((None, tr, *part[1:]), lambda j, me_ref: (me_ref[0], j, *tail))),
            out_shape=jax.ShapeDtypeStruct(land.shape, land.dtype), input_output_aliases={2: 0},
            compiler_params=_params("parallel"),
        )(me, src, land))
    return out


def _adamw_math(w, g, m, v):
    m = ADAM_B1 * m + (1.0 - ADAM_B1) * g
    v = ADAM_B2 * v + (1.0 - ADAM_B2) * (g * g)
    m_hat = m / (1.0 - ADAM_B1 ** ADAM_STEP)
    v_hat = v / (1.0 - ADAM_B2 ** ADAM_STEP)
    delta = -ADAM_LR * (m_hat / (jnp.sqrt(v_hat) + ADAM_EPS) + ADAM_WD * w)
    return delta, m, v


def _adamw(name, parts, w, m, v, *, grid, part_specs, w_spec):
    n_layers = len(parts)
    n_parts = parts[0].shape[0]

    def body(*refs):
        p_refs = refs[:n_layers]
        w_ref, m_ref, v_ref, g_ref, d_ref, nm_ref, nv_ref = refs[n_layers:]

        def total(p_ref):
            t = p_ref[0].astype(F32)
            for k in range(1, n_parts):
                t = t + p_ref[k].astype(F32)
            return t

        g = total(p_refs[0])
        for li in range(1, n_layers):
            g = jnp.where(pl.program_id(0) == li, total(p_refs[li]), g)
        delta, nm, nv = _adamw_math(w_ref[...], g, m_ref[...], v_ref[...])
        g_ref[...] = g
        d_ref[...] = delta
        nm_ref[...] = nm
        nv_ref[...] = nv

    out = jax.ShapeDtypeStruct(w.shape, F32)
    return pl.pallas_call(
        body, name=name, grid=grid, in_specs=[*part_specs, w_spec, w_spec, w_spec], out_specs=[w_spec] * 4,
        out_shape=[out] * 4, compiler_params=_params(*(("parallel",) * len(grid))),
    )(*parts, w, m, v)


def _layer_part_spec(layer, block, n_blocks, row_off=0):
    def index_map(l, i):
        ii = jnp.where(l == layer, i, jnp.where(l < layer, 0, n_blocks - 1))
        return (0, row_off + ii) + (0,) * (len(block) - 2)
    return pl.BlockSpec(block, index_map)


def _small_update(partials, triples, conv_rows):
    d = partials[-1].shape[-1]
    n_rep = len(triples)
    n_part = len(partials)
    rows = []
    for p in partials:
        rows.append(p.shape[0] * (p.shape[1] // d))
    offs = [sum(rows[:i]) for i in range(n_part)]
    total = -(-sum(rows) // 8) * 8

    def body(*refs):
        p_refs = refs[:n_part]
        wmv = refs[n_part:n_part + 3 * n_rep]
        outs = refs[n_part + 3 * n_rep:n_part + 3 * n_rep + 4 * n_rep + (n_part - n_rep)]
        buf, send_sems, recv_sems = refs[-3:]
        x, y, c = _my_place()
        me = _block_of(x, y, c)
        peers = _peers(x, y, c)
        mine = buf.at[me]
        if total > sum(rows):
            mine[sum(rows):total, :] = jnp.zeros((total - sum(rows), d), F32)
        for p_ref, off in zip(p_refs, offs):
            nr, nc = p_ref.shape[0], p_ref.shape[1] // d
            if nc == 1:
                mine[off:off + nr, :] = p_ref[...]
            else:
                for r in range(nr):
                    for q in range(nc):
                        mine[off + r * nc + q:off + r * nc + q + 1, :] = p_ref[r:r + 1, q * d:(q + 1) * d]
        sends =[pltpu.make_async_remote_copy(
            src_ref=buf.at[me], dst_ref=buf.at[me], send_sem=send_sems.at[r], recv_sem=recv_sems.at[r],
            device_id=peer, device_id_type=MESH) for r, peer in enumerate(peers)]
        for cp in sends:
            cp.start()
        for r, peer in enumerate(peers):
            pltpu.make_async_remote_copy(
                src_ref=buf.at[me], dst_ref=buf.at[_block_of(*peer)], send_sem=send_sems.at[r],
                recv_sem=recv_sems.at[r], device_id=peer, device_id_type=MESH).wait_recv()
        for cp in sends:
            cp.wait_send()
        tot = buf[0]
        for k in range(1, N_DEV):
            tot = tot + buf[k]
        buf[0] = tot
        for idx in range(n_part):
            nr, nc = p_refs[idx].shape[0], p_refs[idx].shape[1] // d
            if idx < n_rep:
                w_ref, m_ref, v_ref = wmv[3 * idx:3 * idx + 3]
                g_ref, d_ref, nm_ref, nv_ref = outs[4 * idx:4 * idx + 4]
            else:
                g_ref = outs[4 * n_rep + idx - n_rep]
            pieces = [(slice(0, nr), slice(0, d), offs[idx], nr)] if nc == 1 else [
                (slice(r, r + 1), slice(q * d, (q + 1) * d), offs[idx] + r * nc + q, 1)
                for r in range(nr) for q in range(nc)]
            for rws, cols, row, cnt in pieces:
                g = buf[0, row:row + cnt, :]
                g_ref[rws, cols] = g
                if idx < n_rep:
                    delta, nm, nv = _adamw_math(w_ref[rws, cols], g, m_ref[rws, cols], v_ref[rws, cols])
                    d_ref[rws, cols] = delta
                    nm_ref[rws, cols] = nm
                    nv_ref[rws, cols] = nv

    vm = pl.BlockSpec(memory_space=pltpu.VMEM)
    operands = list(partials)
    for t in triples:
        operands += list(t)
    out_shape = []
    for idx in range(n_rep):
        out_shape += [jax.ShapeDtypeStruct(partials[idx].shape, F32)] * 4
    for idx in range(n_rep, n_part):
        out_shape.append(jax.ShapeDtypeStruct(partials[idx].shape, F32))
    return pl.pallas_call(
        body, name="small_allreduce_adamw", in_specs=[vm] * len(operands), out_specs=[vm] * len(out_shape),
        out_shape=out_shape,
        scratch_shapes=[pltpu.VMEM((N_DEV, total, d), F32), pltpu.SemaphoreType.DMA((7,)), pltpu.SemaphoreType.DMA((7,))],
        compiler_params=pltpu.CompilerParams(vmem_limit_bytes=VMEM_LIMIT_BYTES),
    )(*operands)


def kernel(x, g_mix, w_in, b_in, conv_a, w_out_a, conv_b, conv_b_bias, ln_b_g, ln_b_b, w_out_b, b_out_b, w_pool, pool_scale, w_o, g_mlp, w_mlp1, w_mlp2, g_final, loss_target, m_g_mix, m_w_in, m_b_in, m_conv_a, m_w_out_a, m_conv_b, m_conv_b_bias, m_ln_b_g, m_ln_b_b, m_w_out_b, m_b_out_b, m_w_pool, m_pool_scale, m_w_o, m_g_mlp, m_w_mlp1, m_w_mlp2, m_g_final, v_g_mix, v_w_in, v_b_in, v_conv_a, v_w_out_a, v_conv_b, v_conv_b_bias, v_ln_b_g, v_ln_b_b, v_w_out_b, v_b_out_b, v_w_pool, v_pool_scale, v_w_o, v_g_mlp, v_w_mlp1, v_w_mlp2, v_g_final):
    _, s, d = x.shape
    n_layers = g_mix.shape[0]
    p_in = b_in.shape[1]
    ci = w_in.shape[2]
    c1 = w_mlp1.shape[2]
    rf = w_mlp2.shape[1]
    rd = w_out_a.shape[1]
    f = rf * N_DEV
    rp = rf + 3 * rd
    o_a, o_b, o_o = rf // rd, rf // rd + 1, rf // rd + 2
    gc = d // N_GROUPS
    ca_rows = 8
    tm = min(1024, s)
    tr = min(512, s)
    tx = min(256, s)
    tk = min(2048, s)
    tk_mlp = min(4096, s)

    me_arr = jnp.reshape(_block_of(*_my_place()), (1,)).astype(jnp.int32)

    def layer_shards(l):
        row_pack = jnp.concatenate([w_mlp2[l], w_out_a[l], w_out_b[l], w_o[l]], axis=0).astype(BF16)
        return [w_in[l].astype(BF16), w_mlp1[l].astype(BF16), row_pack, w_pool[l].astype(BF16)]

    conv_pack = jnp.concatenate(
        [conv_a, jnp.zeros((n_layers, ca_rows - K_A, rd), F32), conv_b], axis=1)
    first_shards = layer_shards(0)
    g_in_first, g_conv = _gather_shards([first_shards[0], conv_pack])
    conv_full = jnp.transpose(g_conv, (1, 2, 0, 3)).reshape(n_layers, ca_rows + K_B, d)
    conv_a_f = conv_full[:, :K_A]
    conv_b_f = conv_full[:, ca_rows:]
    first_row_going = _start_copies("gather_start_row_0", first_shards[2:], g_conv, blockwise=False)
    in_flight = [_start_copies("gather_start_mlp1_0", first_shards[1:2], first_row_going["token"], blockwise=False)]
    for l in range(1, n_layers):
        in_flight.append(_start_copies(f"gather_start_{l}", layer_shards(l), in_flight[-1]["token"], blockwise=False))
    token = in_flight[-1]["token"][0:1, 0:1]

    xs = [x[0]]
    saved = []
    weights = []
    row2 = lambda j, i: (i, 0)
    for l in range(n_layers):
        x0 = xs[-1]
        vec = lambda a: a[l:l + 1]
        if l > 0:
            srcs, lands = _wait_copies(f"gather_wait_{l}", in_flight[l], x0, blockwise=False)
            g_in, g_1, g_row, g_pool = _place_own(f"gather_own_{l}", lands, srcs, me_arr, blockwise=False)
        else:
            g_in = g_in_first
        h = _rms_fwd(f"rms_mix_{l}", x0, vec(g_mix) + token if l == 0 else vec(g_mix), tr)
        proj = _mm(
            f"proj_{l}", h, g_in, grid=(N_DEV, s // tm), a_spec=pl.BlockSpec((tm, d), row2),
            b_spec=pl.BlockSpec((None, d, ci), lambda j, i: (j, 0, 0)),
            extras=(vec(b_in),), extra_specs=(pl.BlockSpec((1, ci), lambda j, i: (0, j)),),
            epilogue=lambda v, b: v + b, out_shape=jax.ShapeDtypeStruct((s, p_in), BF16),
            o_spec=pl.BlockSpec((tm, ci), lambda j, i: (i, j)), dims=NN)
        if l == 0:
            srcs, lands = _wait_copies("gather_wait_row_0", first_row_going, proj, blockwise=False)
            g_row, g_pool = _place_own("gather_own_row_0", lands, srcs, me_arr, blockwise=False)
        p_a, sw, p_c, cv, y_a, y_b, pw, merged, x1 = _mixer_fwd(
            f"mix_fwd_{l}", proj, x0, conv_a_f[l], conv_b_f[l], vec(conv_b_bias), vec(ln_b_g), vec(ln_b_b),
            vec(b_out_b), vec(pool_scale), g_row, g_pool, (o_a, o_b, o_o), d, tx)
        if l == 0:
            srcs, lands = _wait_copies("gather_wait_mlp1_0", in_flight[0], x1, blockwise=False)
            g_1, = _place_own("gather_own_mlp1_0", lands, srcs, me_arr, blockwise=False)
        weights.append((g_in, g_1, g_row, g_pool))
        h2 = _rms_fwd(f"rms_mlp_{l}", x1, vec(g_mlp), tr)
        a_pre = _mm(f"mlp1_{l}", h2, g_1, grid=(N_DEV // 2, s // tm), a_spec=pl.BlockSpec((tm, d), row2),
                    b_spec=pl.BlockSpec((2, d, c1), lambda j, i: (j, 0, 0)), slabs="n",
                    out_shape=jax.ShapeDtypeStruct((s, f), BF16),
                    o_spec=pl.BlockSpec((tm, 2 * c1), lambda j, i: (i, j)), dims=NN)
        x2 = _mm(f"mlp2_{l}", a_pre, g_row, grid=(1, s // tr), a_spec=pl.BlockSpec((tr, f), row2),
                 b_spec=pl.BlockSpec((N_DEV, rf, d), lambda j, i: (0, 0, 0)), prologue=_relu_sq,
                 extras=(x1,), extra_specs=(pl.BlockSpec((tr, d), row2),), epilogue=lambda v, r: v + r,
                 out_shape=jax.ShapeDtypeStruct((s, d), F32), o_spec=pl.BlockSpec((tr, d), row2), dims=NN)
        saved.append((x0, h, proj, p_a, sw, p_c, cv, y_a, y_b, pw, merged, x1, h2, a_pre))
        xs.append(x2)

    loss_part, dx, dx16, dg_final = _loss_head(xs[-1], g_final.reshape(1, d), loss_target[0], tr)
    loss = lax.psum(loss_part[0, 0], ("x", "y", "c"))

    small = [None] * n_layers
    exchanges = [None] * n_layers
    for l in reversed(range(n_layers)):
        x0, h, proj, p_a, sw, p_c, cv, y_a, y_b, pw, merged, x1, h2, a_pre = saved[l]
        g_in, g_1, g_row, g_pool = weights[l]
        vec = lambda a: a[l:l + 1]
        row_shape = jax.ShapeDtypeStruct((N_DEV, rp, d), BF16)

        def dd_grad(name, a, g, off, alias):
            return _mm(name, a, g, grid=(1, s // tk), a_spec=pl.BlockSpec((tk, d), lambda j, k: (k, 0)),
                       b_spec=pl.BlockSpec((tk, d), lambda j, k: (k, 0)), out_shape=row_shape,
                       o_spec=pl.BlockSpec((N_DEV, rd, d), lambda j, k: (0, off, 0)), dims=TN, nk=s // tk,
                       acc_shape=(d, d), alias_in=alias)

        d_a = _mm(f"d_act_{l}", dx16, g_row, grid=(N_DEV // 2, s // tm), a_spec=pl.BlockSpec((tm, d), row2),
                  b_spec=pl.BlockSpec((2, rf, d), lambda j, i: (j, 0, 0)), slabs="n",
                  extras=(a_pre,), extra_specs=(pl.BlockSpec((tm, 2 * rf), lambda j, i: (i, j)),),
                  epilogue=lambda v, a: v * (2.0 * jnp.maximum(a.astype(F32), 0.0)),
                  out_shape=jax.ShapeDtypeStruct((s, f), BF16),
                  o_spec=pl.BlockSpec((tm, 2 * rf), lambda j, i: (i, j)), dims=NT)
        dg_row = _mm(f"dw_mlp2_{l}", a_pre, dx16, grid=(N_DEV, s // tk_mlp),
                     a_spec=pl.BlockSpec((tk_mlp, rf), lambda j, k: (k, j)),
                     b_spec=pl.BlockSpec((tk_mlp, d), lambda j, k: (k, 0)), prologue=_relu_sq, out_shape=row_shape,
                     o_spec=pl.BlockSpec((None, rf, d), lambda j, k: (j, 0, 0)), dims=TN, nk=s // tk_mlp,
                     acc_shape=(rf, d))
        d_h2 = _mm(f"d_h2_{l}", d_a, g_1, grid=(1, s // tr), a_spec=pl.BlockSpec((tr, f), row2),
                   b_spec=pl.BlockSpec((N_DEV, d, c1), lambda j, i: (0, 0, 0)), slabs="k",
                   out_shape=jax.ShapeDtypeStruct((s, d), BF16), o_spec=pl.BlockSpec((tr, d), row2), dims=NT)
        dg_1 = _mm(f"dw_mlp1_{l}", h2, d_a, grid=(N_DEV, s // tk_mlp),
                   a_spec=pl.BlockSpec((tk_mlp, d), lambda j, k: (k, 0)),
                   b_spec=pl.BlockSpec((tk_mlp, c1), lambda j, k: (k, j)),
                   out_shape=jax.ShapeDtypeStruct((N_DEV, d, c1), BF16),
                   o_spec=pl.BlockSpec((None, d, c1), lambda j, k: (j, 0, 0)), dims=TN, nk=s // tk_mlp,
                   acc_shape=(d, c1))
        mlp1_going = _start_copies(f"grads_start_mlp1_{l}", [dg_1], vec(g_mlp), blockwise=True)
        dx, dx16, dg_mlp = _rms_bwd(f"rms_mlp_bwd_{l}", d_h2, x1, vec(g_mlp) + mlp1_going["token"][0:1, 0:1], dx, tr)
        d_ya, d_yb, d_pw, d_gates, d_pa, d_sw, d_pc, d_bout, d_pscale = _merge_bwd(
            f"merge_bwd_{l}", dx16, proj, y_a, y_b, pw, vec(pool_scale), g_row, g_pool, (o_a, o_b, o_o), d, tx)
        dg_row = dd_grad(f"dw_o_{l}", merged, dx16, o_o, dg_row)
        dg_row = dd_grad(f"dw_out_a_{l}", p_a, d_ya, o_a, dg_row)
        dg_row = dd_grad(f"dw_out_b_{l}", sw, d_yb, o_b, dg_row)
        dg_pool = _pool_wgrad(f"dw_pool_{l}", p_c, d_pw, d, tk)
        rest_going = _start_copies(f"grads_start_rest_{l}", [dg_row, dg_pool], vec(g_mlp), blockwise=True)
        d_proj, d_bin, d_ca, d_cb, d_cbb, d_lng, d_lnb = _mix_pre_bwd(
            f"mix_bwd_{l}", proj, cv, d_pa, d_sw, d_pc, d_gates, conv_a_f[l], conv_b_f[l],
            vec(ln_b_g) + rest_going["token"][0:1, 0:1], vec(ln_b_b), d, tx)
        dg_in =_mm(f"dw_in_{l}", h, d_proj, grid=(N_DEV, s // tk), a_spec=pl.BlockSpec((tk, d), lambda j, k: (k, 0)),
                    b_spec=pl.BlockSpec((tk, ci), lambda j, k: (k, j)),
                    out_shape=jax.ShapeDtypeStruct((N_DEV, d, ci), BF16),
                    o_spec=pl.BlockSpec((None, d, ci), lambda j, k: (j, 0, 0)), dims=TN, nk=s // tk,
                    acc_shape=(d, ci), after=rest_going["token"])
        in_going =_start_copies(f"grads_start_in_{l}", [dg_in], vec(g_mix), blockwise=True)
        d_h = _mm(f"d_h_{l}", d_proj, g_in, grid=(s // tm, N_DEV // 2),
                  a_spec=pl.BlockSpec((tm, 2 * ci), lambda i, k: (i, k)),
                  b_spec=pl.BlockSpec((2, d, ci), lambda i, k: (k, 0, 0)), slabs="k",
                  out_shape=jax.ShapeDtypeStruct((s, d), BF16), o_spec=pl.BlockSpec((tm, d), lambda i, k: (i, 0)),
                  dims=NT, nk=N_DEV // 2, acc_shape=(tm, d), after=in_going["token"])
        dx, dx16, dg_mix = _rms_bwd(f"rms_mix_bwd_{l}", d_h, x0, vec(g_mix), dx, tr)
        small[l] = (dg_mix, d_bin, d_cbb, d_lng, d_lnb, d_bout, d_pscale, dg_mlp, d_ca, d_cb)
        exchanges[l] = (in_going, mlp1_going, rest_going)

    grad_x = dx[None]

    names = ("g_mix", "b_in", "conv_b_bias", "ln_b_g", "ln_b_b", "b_out_b", "pool_scale", "g_mlp")
    given = dict(g_mix=(g_mix, m_g_mix, v_g_mix), b_in=(b_in, m_b_in, v_b_in),
                 conv_b_bias=(conv_b_bias, m_conv_b_bias, v_conv_b_bias), ln_b_g=(ln_b_g, m_ln_b_g, v_ln_b_g),
                 ln_b_b=(ln_b_b, m_ln_b_b, v_ln_b_b), b_out_b=(b_out_b, m_b_out_b, v_b_out_b),
                 pool_scale=(pool_scale, m_pool_scale, v_pool_scale), g_mlp=(g_mlp, m_g_mlp, v_g_mlp))
    partials, triples = [], []
    for i, nm in enumerate(names):
        partials.append(jnp.concatenate([small[l][i] for l in range(n_layers)], axis=0))
        triples.append(given[nm])
    partials.append(dg_final)
    triples.append(tuple(a.reshape(1, d) for a in (g_final, m_g_final, v_g_final)))
    partials.append(jnp.concatenate([small[l][8] for l in range(n_layers)], axis=0))
    partials.append(jnp.concatenate([small[l][9] for l in range(n_layers)], axis=0))
    outs = _small_update(partials, triples, 2)
    rep = {nm: outs[4 * i:4 * i + 4] for i, nm in enumerate(names)}
    rep["g_final"] = [a.reshape(d) for a in outs[4 * len(names):4 * len(names) + 4]]
    me = _block_of(*_my_place())
    gca = lax.dynamic_slice_in_dim(outs[-2].reshape(n_layers, K_A, d), me * rd, rd, axis=2)
    gcb = lax.dynamic_slice_in_dim(outs[-1].reshape(n_layers, K_B, d), me * rd, rd, axis=2)

    r_in, r_1, r_row, r_pool = [], [], [], []
    for l in reversed(range(n_layers)):
        in_going, mlp1_going, rest_going = exchanges[l]
        srcs_m, lands_m = _wait_copies(f"grads_wait_mlp1_{l}", mlp1_going, outs[0], blockwise=True)
        srcs_r, lands_r = _wait_copies(f"grads_wait_rest_{l}", rest_going, outs[0], blockwise=True)
        srcs_i, lands_i = _wait_copies(f"grads_wait_in_{l}", in_going, outs[0], blockwise=True)
        got = _place_own(f"grads_own_{l}", lands_i + lands_m + lands_r, srcs_i + srcs_m + srcs_r, me_arr, blockwise=True)
        for lst, arr in zip((r_in, r_1, r_row, r_pool), got):
            lst.insert(0, arr)
    tb = min(256, d)
    layers = range(n_layers)
    res = {}
    res["w_in"] = _adamw("adamw_w_in", r_in, w_in, m_w_in, v_w_in, grid=(n_layers, d // tb),
                         part_specs=[_layer_part_spec(li, (N_DEV, tb, ci), d // tb) for li in layers],
                         w_spec=pl.BlockSpec((None, tb, ci), lambda l, i: (l, i, 0)))
    res["w_mlp1"] = _adamw("adamw_w_mlp1", r_1, w_mlp1, m_w_mlp1, v_w_mlp1, grid=(n_layers, d // tb),
                           part_specs=[_layer_part_spec(li, (N_DEV, tb, c1), d // tb) for li in layers],
                           w_spec=pl.BlockSpec((None, tb, c1), lambda l, i: (l, i, 0)))
    tf = min(256, rf)
    res["w_mlp2"] = _adamw("adamw_w_mlp2", r_row, w_mlp2, m_w_mlp2, v_w_mlp2, grid=(n_layers, rf // tf),
                           part_specs=[_layer_part_spec(li, (N_DEV, tf, d), rf // tf) for li in layers],
                           w_spec=pl.BlockSpec((None, tf, d), lambda l, i: (l, i, 0)))
    for nm, off, trip in (("w_out_a", o_a, (w_out_a, m_w_out_a, v_w_out_a)),
                          ("w_out_b", o_b, (w_out_b, m_w_out_b, v_w_out_b)), ("w_o", o_o, (w_o, m_w_o, v_w_o))):
        res[nm] = _adamw(f"adamw_{nm}", r_row, *trip, grid=(n_layers, 1),
                         part_specs=[_layer_part_spec(li, (N_DEV, rd, d), 1, row_off=off) for li in layers],
                         w_spec=pl.BlockSpec((None, rd, d), lambda l, i: (l, 0, 0)))
    res["w_pool"] = _adamw("adamw_w_pool", r_pool, w_pool, m_w_pool, v_w_pool, grid=(n_layers, 1),
                           part_specs=[_layer_part_spec(li, (N_DEV, N_GROUPS, gc // N_DEV, gc), 1) for li in layers],
                           w_spec=pl.BlockSpec((None, N_GROUPS, gc // N_DEV, gc), lambda l, i: (l, 0, 0, 0)))
    whole3 = lambda: (0, 0, 0)
    res["conv_a"] = _adamw("adamw_conv_a", [gca[None]], conv_a, m_conv_a, v_conv_a, grid=(),
                           part_specs=[pl.BlockSpec((1, n_layers, K_A, rd), lambda: (0, 0, 0, 0))],
                           w_spec=pl.BlockSpec((n_layers, K_A, rd), whole3))
    res["conv_b"] = _adamw("adamw_conv_b", [gcb[None]], conv_b, m_conv_b, v_conv_b, grid=(),
                           part_specs=[pl.BlockSpec((1, n_layers, K_B, rd), lambda: (0, 0, 0, 0))],
                           w_spec=pl.BlockSpec((n_layers, K_B, rd), whole3))
    res.update(rep)

    order = ("g_mix", "w_in", "b_in", "conv_a", "w_out_a", "conv_b", "conv_b_bias", "ln_b_g", "ln_b_b", "w_out_b",
             "b_out_b", "w_pool", "pool_scale", "w_o", "g_mlp", "w_mlp1", "w_mlp2", "g_final")
    out = [loss, grad_x]
    for kind in range(4):
        out += [res[nm][kind] for nm in order]
    return tuple(out)
```

```python
import jax
import jax.numpy as jnp
from jax import lax
from jax.experimental import pallas as pl
from jax.experimental.pallas import tpu as pltpu

F32 = jnp.float32
BF16 = jnp.bfloat16
MESH = pl.DeviceIdType.MESH

N_DEV = 8
EPS = 1e-6
K_A = 3
K_B = 31
POOL_WINDOWS = (2, 4, 8, 16)
N_GROUPS = len(POOL_WINDOWS)
HALO = 32
CHUNK = 16
SUBLANES = 8
TAP_GROUP = 4
ADAM_LR, ADAM_B1, ADAM_B2, ADAM_EPS, ADAM_WD, ADAM_STEP = 0.001, 0.9, 0.999, 1e-08, 0.01, 10
VMEM_LIMIT_BYTES = 60 * 1024 * 1024

NN = (((1,), (0,)), ((), ()))
NT = (((1,), (1,)), ((), ()))
TN = (((0,), (0,)), ((), ()))


def _params(*sem):
    return pltpu.CompilerParams(dimension_semantics=sem, vmem_limit_bytes=VMEM_LIMIT_BYTES)


def _sigmoid(v):
    return 1.0 / (1.0 + jnp.exp(-v))


def _mm(name, a, b, *, grid, a_spec, b_spec, out_shape, o_spec, dims, nk=1, acc_shape=None,
        extras=(), extra_specs=(), prologue=None, epilogue=None, alias_in=None, slabs=None, after=None):
    n_extra = len(extras)
    has_alias = alias_in is not None
    n_unread = (1 if has_alias else 0) + (1 if after is not None else 0)

    def body(*refs):
        a_ref, b_ref = refs[0], refs[1]
        ex = refs[2:2 + n_extra]
        o_ref = refs[2 + n_extra + n_unread]
        av = a_ref[...]
        if prologue is not None:
            av = prologue(av)
        av = av.astype(BF16)

        def finish(val, cols=None):
            if epilogue is not None:
                val = epilogue(val, *[e[...] if cols is None else e[:, cols] for e in ex])
            if cols is None:
                o_ref[...] = val.astype(o_ref.dtype).reshape(o_ref.shape)
            else:
                o_ref[:, cols] = val.astype(o_ref.dtype)

        if slabs == "n":
            for q in range(b_ref.shape[0]):
                pq = lax.dot_general(av, b_ref[q].astype(BF16), dims, preferred_element_type=F32)
                finish(pq, slice(q * pq.shape[1], (q + 1) * pq.shape[1]))
            return
        if slabs == "k":
            kc = av.shape[1] // b_ref.shape[0]
            p = None
            for q in range(b_ref.shape[0]):
                pq = lax.dot_general(av[:, q * kc:(q + 1) * kc], b_ref[q].astype(BF16), dims,
                                     preferred_element_type=F32)
                p = pq if p is None else p + pq
        else:
            bv = b_ref[...]
            bv = bv.reshape((-1, bv.shape[-1])).astype(BF16)
            p = lax.dot_general(av, bv, dims, preferred_element_type=F32)

        if nk == 1:
            finish(p)
        else:
            acc = refs[-1]
            k = pl.program_id(len(grid) - 1)

            @pl.when(k == 0)
            def _():
                acc[...] = p

            @pl.when(k > 0)
            def _():
                acc[...] += p

            @pl.when(k == nk - 1)
            def _():
                finish(acc[...])

    in_specs = [a_spec, b_spec, *extra_specs]
    operands = [a, b, *extras]
    aliases = {}
    if has_alias:
        in_specs.append(pl.BlockSpec(memory_space=pl.ANY))
        operands.append(alias_in)
        aliases = {len(operands) - 1: 0}
    if after is not None:
        in_specs.append(pl.BlockSpec(memory_space=pl.ANY))
        operands.append(after)
    sem =("parallel",) * (len(grid) - 1) + (("arbitrary",) if nk > 1 else ("parallel",))
    return pl.pallas_call(
        body, name=name, grid=grid, in_specs=in_specs, out_specs=o_spec, out_shape=out_shape,
        scratch_shapes=[pltpu.VMEM(acc_shape, F32)] if nk > 1 else [],
        input_output_aliases=aliases, compiler_params=_params(*sem),
    )(*operands)


def _relu_sq(v):
    r = jnp.maximum(v, 0)
    return r * r


def _rms_fwd(name, x, g, tm):
    s, d = x.shape

    def body(x_ref, g_ref, h_ref):
        xv = x_ref[...]
        r = lax.rsqrt(jnp.mean(xv * xv, axis=-1, keepdims=True) + EPS)
        h_ref[...] = (xv * r * g_ref[...]).astype(h_ref.dtype)

    return pl.pallas_call(
        body, name=name, grid=(s // tm,),
        in_specs=[pl.BlockSpec((tm, d), lambda i: (i, 0)), pl.BlockSpec((1, d), lambda i: (0, 0))],
        out_specs=pl.BlockSpec((tm, d), lambda i: (i, 0)),
        out_shape=jax.ShapeDtypeStruct((s, d), BF16), compiler_params=_params("parallel"),
    )(x, g)


def _colsum8(v):
    return jnp.sum(v.reshape(v.shape[0] // 8, 8, v.shape[1]), axis=0)


def _rms_bwd(name, dh, x, g, dres, tm):
    s, d = x.shape
    n = s // tm

    def body(dh_ref, x_ref, g_ref, dr_ref, dx_ref, dx16_ref, dg_ref, acc):
        i = pl.program_id(0)
        xv = x_ref[...]
        r = lax.rsqrt(jnp.mean(xv * xv, axis=-1, keepdims=True) + EPS)
        xh = xv * r
        dhv = dh_ref[...].astype(F32)
        part = _colsum8(dhv * xh)

        @pl.when(i == 0)
        def _():
            acc[...] = part

        @pl.when(i > 0)
        def _():
            acc[...] += part

        dxh = dhv * g_ref[...]
        dx = r * (dxh - xh * jnp.mean(dxh * xh, axis=-1, keepdims=True))
        dx = dx + dr_ref[...]
        dx_ref[...] = dx
        dx16_ref[...] = dx.astype(BF16)

        @pl.when(i == n - 1)
        def _():
            dg_ref[...] = jnp.sum(acc[...], axis=0, keepdims=True)

    return pl.pallas_call(
        body, name=name, grid=(n,),
        in_specs=[pl.BlockSpec((tm, d), lambda i: (i, 0)), pl.BlockSpec((tm, d), lambda i: (i, 0)),
                  pl.BlockSpec((1, d), lambda i: (0, 0)), pl.BlockSpec((tm, d), lambda i: (i, 0))],
        out_specs=[pl.BlockSpec((tm, d), lambda i: (i, 0)), pl.BlockSpec((tm, d), lambda i: (i, 0)),
                   pl.BlockSpec((1, d), lambda i: (0, 0))],
        out_shape=[jax.ShapeDtypeStruct((s, d), F32), jax.ShapeDtypeStruct((s, d), BF16),
                   jax.ShapeDtypeStruct((1, d), F32)],
        scratch_shapes=[pltpu.VMEM((8, d), F32)], compiler_params=_params("arbitrary"),
    )(dh, x, g, dres)


def _loss_head(x, g, target, tm):
    s, d = x.shape
    n = s // tm

    def body(x_ref, g_ref, t_ref, loss_ref, dx_ref, dx16_ref, dg_ref, acc_l, acc_g):
        i = pl.program_id(0)
        xv = x_ref[...]
        r = lax.rsqrt(jnp.mean(xv * xv, axis=-1, keepdims=True) + EPS)
        xh = xv * r
        err = xh * g_ref[...] - t_ref[...]
        dy = err * (1.0 / d)
        lpart = _colsum8(err * err)
        gpart = _colsum8(dy * xh)

        @pl.when(i == 0)
        def _():
            acc_l[...] = lpart
            acc_g[...] = gpart

        @pl.when(i > 0)
        def _():
            acc_l[...] += lpart
            acc_g[...] += gpart

        dxh = dy * g_ref[...]
        dx = r * (dxh - xh * jnp.mean(dxh * xh, axis=-1, keepdims=True))
        dx_ref[...] = dx
        dx16_ref[...] = dx.astype(BF16)

        @pl.when(i == n - 1)
        def _():
            loss_ref[...] = (0.5 / d) * jnp.sum(jnp.sum(acc_l[...], axis=0, keepdims=True), axis=1, keepdims=True)
            dg_ref[...] = jnp.sum(acc_g[...], axis=0, keepdims=True)

    return pl.pallas_call(
        body, name="loss_head", grid=(n,),
        in_specs=[pl.BlockSpec((tm, d), lambda i: (i, 0)), pl.BlockSpec((1, d), lambda i: (0, 0)),
                  pl.BlockSpec((tm, d), lambda i: (i, 0))],
        out_specs=[pl.BlockSpec((1, 1), lambda i: (0, 0)), pl.BlockSpec((tm, d), lambda i: (i, 0)),
                   pl.BlockSpec((tm, d), lambda i: (i, 0)), pl.BlockSpec((1, d), lambda i: (0, 0))],
        out_shape=[jax.ShapeDtypeStruct((1, 1), F32), jax.ShapeDtypeStruct((s, d), F32),
                   jax.ShapeDtypeStruct((s, d), BF16), jax.ShapeDtypeStruct((1, d), F32)],
        scratch_shapes=[pltpu.VMEM((8, d), F32), pltpu.VMEM((8, d), F32)], compiler_params=_params("arbitrary"),
    )(x, g, target)


def _sec(ref, n, d):
    return ref[:, n * d:(n + 1) * d].astype(F32)


def _fill_shifts(sh, ext):
    rows = ext.shape[0] - SUBLANES
    for b in range(1, SUBLANES):
        sh[b - 1, 0:rows, :] = ext[b:b + rows, :]


def _shifted(sh, ext, off, n):
    b = off % SUBLANES
    if b == 0:
        return ext[off:off + n, :]
    return sh[b - 1, off - b:off - b + n, :]


def _pool_count(row0, rows, window):
    t = row0 + lax.broadcasted_iota(jnp.int32, (rows, 1), 0)
    return jnp.minimum(t + 1, window).astype(F32)


def _group_weight(w_ref, gi, gc):
    return w_ref[:, gi].reshape(gc, gc)


def _mixer_fwd(name, proj, x0, conv_a, conv_b, conv_b_bias, ln_g, ln_b, b_out_b, pool_scale, g_row, g_pool, offs, d, tm):
    s = proj.shape[0]
    n = s // tm
    gc = d // N_GROUPS
    hb = tm // HALO
    rd = d // N_DEV
    o_a, o_b, o_o = offs

    def body(pj_ref, hp_ref, x0_ref, ca_ref, cb_ref, cbb_ref, lng_ref, lnb_ref, bo_ref, sc_ref, wa_ref, wb_ref, wo_ref,
             wp_ref, pa_ref, sw_ref, pc_ref, cv_ref, ya_ref, yb_ref, pw_ref, mg_ref, x1_ref, eua, eub, euc, sh):
        i = pl.program_id(0)
        keep = (i > 0).astype(F32)
        eua[0:HALO, :] = _sec(hp_ref, 1, d) * _sec(hp_ref, 2, d) * keep
        eub[0:HALO, :] = _sec(hp_ref, 3, d) * _sigmoid(_sec(hp_ref, 4, d)) * keep
        euc[0:HALO, :] = _sec(hp_ref, 5, d) * keep
        eua[HALO:HALO + tm, :] = _sec(pj_ref, 1, d) * _sec(pj_ref, 2, d)
        eub[HALO:HALO + tm, :] = _sec(pj_ref, 3, d) * _sigmoid(_sec(pj_ref, 4, d))
        euc[HALO:HALO + tm, :] = _sec(pj_ref, 5, d)
        _fill_shifts(sh, eub)
        for c in range(tm // CHUNK):
            r0 = c * CHUNK
            z = jnp.zeros((CHUNK, d), F32)
            for k in range(K_A):
                z = z + ca_ref[k:k + 1, :] * eua[HALO + r0 - (K_A - 1) + k:HALO + r0 - (K_A - 1) + k + CHUNK, :]
            pa_ref[r0:r0 + CHUNK, :] = (pj_ref[r0:r0 + CHUNK, 0:d].astype(F32) * z).astype(pa_ref.dtype)
            cv = jnp.zeros((CHUNK, d), F32) + cbb_ref[...]
            for k in range(K_B):
                cv = cv + cb_ref[k:k + 1, :] * _shifted(sh, eub, HALO + r0 - (K_B - 1) + k, CHUNK)
            cv_ref[r0:r0 + CHUNK, :] = cv.astype(cv_ref.dtype)
        cvv = cv_ref[...].astype(F32)
        mu = jnp.mean(cvv, axis=-1, keepdims=True)
        xc = cvv - mu
        xh = xc * lax.rsqrt(jnp.mean(xc * xc, axis=-1, keepdims=True) + EPS)
        ln = xh * lng_ref[...] + lnb_ref[...]
        sw_ref[...] = (ln * _sigmoid(ln)).astype(sw_ref.dtype)
        for gi, w in enumerate(POOL_WINDOWS):
            cols = slice(gi * gc, (gi + 1) * gc)
            tot = euc[HALO:HALO + tm, cols]
            for k in range(1, w):
                tot = tot + euc[HALO - k:HALO - k + tm, cols]
            cnt = _pool_count(i * tm, tm, w)
            pc_ref[:, cols] = (tot / cnt - euc[HALO:HALO + tm, cols]).astype(pc_ref.dtype)
        ya_ref[...] = jnp.dot(pa_ref[...], wa_ref[...].reshape(d, d), preferred_element_type=F32).astype(ya_ref.dtype)
        yb_ref[...] = (jnp.dot(sw_ref[...], wb_ref[...].reshape(d, d), preferred_element_type=F32)
                       + bo_ref[...]).astype(yb_ref.dtype)
        for gi in range(N_GROUPS):
            cols = slice(gi * gc, (gi + 1) * gc)
            pw_ref[:, cols] = jnp.dot(pc_ref[:, cols], _group_weight(wp_ref, gi, gc),
                                      preferred_element_type=F32).astype(pw_ref.dtype)
        m = _sigmoid(_sec(pj_ref, 6, d)) * ya_ref[...].astype(F32)
        m = m + _sigmoid(_sec(pj_ref, 7, d)) * yb_ref[...].astype(F32)
        m = m + _sigmoid(_sec(pj_ref, 8, d)) * (pw_ref[...].astype(F32) * sc_ref[...])
        mg_ref[...] = m.astype(mg_ref.dtype)
        x1_ref[...] = x0_ref[...] + jnp.dot(mg_ref[...], wo_ref[...].reshape(d, d), preferred_element_type=F32)

    row = lambda i: (i, 0)
    fixed = lambda i: (0, 0)
    act = jax.ShapeDtypeStruct((s, d), BF16)

    def dd_weight(off):
        return pl.BlockSpec((N_DEV, rd, d), lambda i: (0, off, 0), pipeline_mode=pl.Buffered(1))

    return pl.pallas_call(
        body, name=name, grid=(n,),
        in_specs=[pl.BlockSpec((tm, 9 * d), row),
                  pl.BlockSpec((HALO, 6 * d), lambda i: (jnp.maximum(i * hb - 1, 0), 0)),
                  pl.BlockSpec((tm, d), row),
                  pl.BlockSpec((K_A, d), fixed), pl.BlockSpec((K_B, d), fixed), pl.BlockSpec((1, d), fixed),
                  pl.BlockSpec((1, d), fixed), pl.BlockSpec((1, d), fixed), pl.BlockSpec((1, d), fixed),
                  pl.BlockSpec((1, d), fixed), dd_weight(o_a), dd_weight(o_b), dd_weight(o_o),
                  pl.BlockSpec((N_DEV, N_GROUPS, gc // N_DEV, gc), lambda i: (0, 0, 0, 0),
                               pipeline_mode=pl.Buffered(1))],
        out_specs=[pl.BlockSpec((tm, d), row)] * 9,
        out_shape=[act] * 8 + [jax.ShapeDtypeStruct((s, d), F32)],
        scratch_shapes=[pltpu.VMEM((tm + HALO, d), F32)] * 3 + [pltpu.VMEM((SUBLANES - 1, tm + HALO, d), F32)],
        compiler_params=_params("parallel"),
    )(proj, proj, x0, conv_a, conv_b, conv_b_bias, ln_g, ln_b, b_out_b, pool_scale, g_row, g_row, g_row, g_pool)


def _merge_bwd(name, dx16, proj, ya, yb, pw, pool_scale, g_row, g_pool, offs, d, tm):
    s = proj.shape[0]
    n = s // tm
    gc = d // N_GROUPS
    rd = d // N_DEV
    o_a, o_b, o_o = offs

    def body(dx_ref, g_ref, ya_ref, yb_ref, pw_ref, sc_ref, wa_ref, wb_ref, wo_ref, wp_ref,
             dya_ref, dyb_ref, dpw_ref, dg_ref, dpa_ref, dsw_ref, dpc_ref, dbo_ref, dsc_ref, acc_b, acc_s):
        i = pl.program_id(0)
        dmv = lax.dot_general(dx_ref[...], wo_ref[...].reshape(d, d), NT,
                              preferred_element_type=F32).astype(BF16).astype(F32)
        scale = sc_ref[...]
        g0 = _sigmoid(_sec(g_ref, 0, d))
        dya_ref[...] = (dmv * g0).astype(dya_ref.dtype)
        dg_ref[:, 0:d] = (dmv * ya_ref[...].astype(F32) * g0 * (1.0 - g0)).astype(dg_ref.dtype)
        g1 = _sigmoid(_sec(g_ref, 1, d))
        dyb = dmv * g1
        dyb_ref[...] = dyb.astype(dyb_ref.dtype)
        dg_ref[:, d:2 * d] = (dmv * yb_ref[...].astype(F32) * g1 * (1.0 - g1)).astype(dg_ref.dtype)
        g2 = _sigmoid(_sec(g_ref, 2, d))
        pwv = pw_ref[...].astype(F32)
        dyc = dmv * g2
        dpw_ref[...] = (dyc * scale).astype(dpw_ref.dtype)
        dg_ref[:, 2 * d:3 * d] = (dmv * (pwv * scale) * g2 * (1.0 - g2)).astype(dg_ref.dtype)
        pb = _colsum8(dyb)
        ps = _colsum8(dyc * pwv)

        @pl.when(i == 0)
        def _():
            acc_b[...] = pb
            acc_s[...] = ps

        @pl.when(i > 0)
        def _():
            acc_b[...] += pb
            acc_s[...] += ps

        dpa_ref[...] = lax.dot_general(dya_ref[...], wa_ref[...].reshape(d, d), NT,
                                       preferred_element_type=F32).astype(dpa_ref.dtype)
        dsw_ref[...] = lax.dot_general(dyb_ref[...], wb_ref[...].reshape(d, d), NT,
                                       preferred_element_type=F32).astype(dsw_ref.dtype)
        for gi in range(N_GROUPS):
            cols = slice(gi * gc, (gi + 1) * gc)
            dpc_ref[:, cols] = lax.dot_general(dpw_ref[:, cols], _group_weight(wp_ref, gi, gc), NT,
                                               preferred_element_type=F32).astype(dpc_ref.dtype)

        @pl.when(i == n - 1)
        def _():
            dbo_ref[...] = jnp.sum(acc_b[...], axis=0, keepdims=True)
            dsc_ref[...] = jnp.sum(acc_s[...], axis=0, keepdims=True)

    row = lambda i: (i, 0)
    fixed = lambda i: (0, 0)
    act = jax.ShapeDtypeStruct((s, d), BF16)
    vec = jax.ShapeDtypeStruct((1, d), F32)

    def dd_weight(off):
        return pl.BlockSpec((N_DEV, rd, d), lambda i: (0, off, 0), pipeline_mode=pl.Buffered(1))

    return pl.pallas_call(
        body, name=name, grid=(n,),
        in_specs=[pl.BlockSpec((tm, d), row), pl.BlockSpec((tm, 3 * d), lambda i: (i, 2)), pl.BlockSpec((tm, d), row),
                  pl.BlockSpec((tm, d), row), pl.BlockSpec((tm, d), row), pl.BlockSpec((1, d), fixed),
                  dd_weight(o_a), dd_weight(o_b), dd_weight(o_o),
                  pl.BlockSpec((N_DEV, N_GROUPS, gc // N_DEV, gc), lambda i: (0, 0, 0, 0),
                               pipeline_mode=pl.Buffered(1))],
        out_specs=[pl.BlockSpec((tm, d), row)] * 3 + [pl.BlockSpec((tm, 3 * d), row)] + [pl.BlockSpec((tm, d), row)] * 3
                  + [pl.BlockSpec((1, d), fixed)] * 2,
        out_shape=[act, act, act, jax.ShapeDtypeStruct((s, 3 * d), BF16), act, act, act, vec, vec],
        scratch_shapes=[pltpu.VMEM((8, d), F32)] * 2, compiler_params=_params("arbitrary"),
    )(dx16, proj, ya, yb, pw, pool_scale, g_row, g_row, g_row, g_pool)


def _mix_pre_bwd(name, proj, cv, dpa, dsw, dpc, dgates, conv_a, conv_b, ln_g, ln_b, d, tm):
    s = proj.shape[0]
    n = s // tm
    gc = d // N_GROUPS
    hb = tm // HALO
    last_halo = s // HALO - 1
    te = tm + HALO

    def ln_bwd(cvv, dswv, lng, lnb):
        mu = jnp.mean(cvv, axis=-1, keepdims=True)
        xc = cvv - mu
        rstd = lax.rsqrt(jnp.mean(xc * xc, axis=-1, keepdims=True) + EPS)
        xh = xc * rstd
        ln = xh * lng + lnb
        sg = _sigmoid(ln)
        dln = dswv * (sg * (1.0 + ln * (1.0 - sg)))
        dxh = dln * lng
        dcv = rstd * (dxh - jnp.mean(dxh, axis=-1, keepdims=True) - xh * jnp.mean(dxh * xh, axis=-1, keepdims=True))
        return dcv, dln, xh

    def body(pj_ref, hp_ref, hf_ref, cv_ref, cvf_ref, dpa_ref, dpaf_ref, dsw_ref, dswf_ref, dpc_ref, dpcf_ref, dgt_ref,
             ca_ref, cb_ref, lng_ref, lnb_ref,
             dpj_ref, dbin_ref, dca_ref, dcb_ref, dcbb_ref, dlng_ref, dlnb_ref,
             eua, eub, edz, edcv, eq, sh, acc_bin, acc_ca, acc_cb, acc_v):
        i = pl.program_id(0)
        keep_p = (i > 0).astype(F32)
        keep_f = (i < n - 1).astype(F32)

        @pl.when(i == 0)
        def _():
            acc_bin[...] = jnp.zeros_like(acc_bin)
            acc_ca[...] = jnp.zeros_like(acc_ca)
            acc_cb[...] = jnp.zeros_like(acc_cb)
            acc_v[...] = jnp.zeros_like(acc_v)

        eua[0:HALO, :] = _sec(hp_ref, 1, d) * _sec(hp_ref, 2, d) * keep_p
        eub[0:HALO, :] = _sec(hp_ref, 3, d) * _sigmoid(_sec(hp_ref, 4, d)) * keep_p
        eua[HALO:te, :] = _sec(pj_ref, 1, d) * _sec(pj_ref, 2, d)
        eub[HALO:te, :] = _sec(pj_ref, 3, d) * _sigmoid(_sec(pj_ref, 4, d))
        edz[0:tm, :] = dpa_ref[...].astype(F32) * _sec(pj_ref, 0, d)
        edz[tm:te, :] = dpaf_ref[...].astype(F32) * _sec(hf_ref, 0, d) * keep_f
        dcv, dln, xh = ln_bwd(cv_ref[...].astype(F32), dsw_ref[...].astype(F32), lng_ref[...], lnb_ref[...])
        edcv[0:tm, :] = dcv
        acc_v[0:8, :] += _colsum8(dcv)
        acc_v[8:16, :] += _colsum8(dln * xh)
        acc_v[16:24, :] += _colsum8(dln)
        dcvf, _, _ = ln_bwd(cvf_ref[...].astype(F32), dswf_ref[...].astype(F32), lng_ref[...], lnb_ref[...])
        edcv[tm:te, :] = dcvf * keep_f
        for gi, w in enumerate(POOL_WINDOWS):
            cols = slice(gi * gc, (gi + 1) * gc)
            eq[0:tm, cols] = dpc_ref[:, cols].astype(F32) / _pool_count(i * tm, tm, w)
            eq[tm:te, cols] = dpcf_ref[:, cols].astype(F32) / _pool_count((i + 1) * tm, HALO, w) * keep_f

        def put(sec_idx, r0, val):
            dpj_ref[r0:r0 + CHUNK, sec_idx * d:(sec_idx + 1) * d] = val.astype(dpj_ref.dtype)
            acc_bin[:, sec_idx * d:(sec_idx + 1) * d] += _colsum8(val)

        _fill_shifts(sh, edcv)
        for c in range(tm // CHUNK):
            r0 = c * CHUNK
            rows = slice(r0, r0 + CHUNK)
            z = jnp.zeros((CHUNK, d), F32)
            dua = jnp.zeros((CHUNK, d), F32)
            for k in range(K_A):
                z = z + ca_ref[k:k + 1, :] * eua[HALO + r0 - (K_A - 1) + k:HALO + r0 - (K_A - 1) + k + CHUNK, :]
                dua = dua + ca_ref[k:k + 1, :] * edz[r0 + (K_A - 1) - k:r0 + (K_A - 1) - k + CHUNK, :]
            put(0, r0, dpa_ref[rows, :].astype(F32) * z)
            put(1, r0, dua * pj_ref[rows, 2 * d:3 * d].astype(F32))
            put(2, r0, dua * pj_ref[rows, d:2 * d].astype(F32))
            dub = jnp.zeros((CHUNK, d), F32)
            for k in range(K_B):
                dub = dub + cb_ref[k:k + 1, :] * _shifted(sh, edcv, r0 + (K_B - 1) - k, CHUNK)
            bval = pj_ref[rows, 3 * d:4 * d].astype(F32)
            sg = _sigmoid(pj_ref[rows, 4 * d:5 * d].astype(F32))
            put(3, r0, dub * sg)
            put(4, r0, dub * bval * sg * (1.0 - sg))
            for gi, w in enumerate(POOL_WINDOWS):
                cols = slice(gi * gc, (gi + 1) * gc)
                tot = eq[rows, cols]
                for k in range(1, w):
                    tot = tot + eq[r0 + k:r0 + k + CHUNK, cols]
                dci = tot - dpc_ref[rows, cols].astype(F32)
                dpj_ref[rows, 5 * d + gi * gc:5 * d + (gi + 1) * gc] = dci.astype(dpj_ref.dtype)
                acc_bin[:, 5 * d + gi * gc:5 * d + (gi + 1) * gc] += _colsum8(dci)
        for q in range(3):
            gv = dgt_ref[:, q * d:(q + 1) * d]
            dpj_ref[:, (6 + q) * d:(7 + q) * d] = gv
            acc_bin[:, (6 + q) * d:(7 + q) * d] += _colsum8(gv.astype(F32))
        for k in range(K_A):
            a = jnp.zeros((8, d), F32)
            for c in range(tm // CHUNK):
                r0 = c * CHUNK
                a = a + _colsum8(edz[r0:r0 + CHUNK, :] * eua[HALO + r0 - (K_A - 1) + k:HALO + r0 - (K_A - 1) + k + CHUNK, :])
            acc_ca[k] += a
        _fill_shifts(sh, eub)
        for k0 in range(0, K_B, TAP_GROUP):
            taps = range(k0, min(k0 + TAP_GROUP, K_B))
            a = {k: jnp.zeros((8, d), F32) for k in taps}
            for c in range(tm // CHUNK):
                r0 = c * CHUNK
                dc = edcv[r0:r0 + CHUNK, :]
                for k in taps:
                    a[k] = a[k] + _colsum8(dc * _shifted(sh, eub, HALO + r0 - (K_B - 1) + k, CHUNK))
            for k in taps:
                acc_cb[k] += a[k]

        @pl.when(i == n - 1)
        def _():
            dbin_ref[...] = jnp.sum(acc_bin[...], axis=0, keepdims=True)
            for k in range(K_A):
                dca_ref[k:k + 1, :] = jnp.sum(acc_ca[k], axis=0, keepdims=True)
            for k in range(K_B):
                dcb_ref[k:k + 1, :] = jnp.sum(acc_cb[k], axis=0, keepdims=True)
            dcbb_ref[...] = jnp.sum(acc_v[0:8, :], axis=0, keepdims=True)
            dlng_ref[...] = jnp.sum(acc_v[8:16, :], axis=0, keepdims=True)
            dlnb_ref[...] = jnp.sum(acc_v[16:24, :], axis=0, keepdims=True)

    row = lambda i: (i, 0)
    fixed = lambda i: (0, 0)
    past = lambda i: (jnp.maximum(i * hb - 1, 0), 0)
    fut = lambda i: (jnp.minimum((i + 1) * hb, last_halo), 0)
    vec = jax.ShapeDtypeStruct((1, d), F32)
    tile_and_halo = [pl.BlockSpec((tm, d), row), pl.BlockSpec((HALO, d), fut)]
    return pl.pallas_call(
        body, name=name, grid=(n,),
        in_specs=[pl.BlockSpec((tm, 6 * d), row), pl.BlockSpec((HALO, 6 * d), past), pl.BlockSpec((HALO, 6 * d), fut),
                  *tile_and_halo, *tile_and_halo, *tile_and_halo, *tile_and_halo,
                  pl.BlockSpec((tm, 3 * d), row),
                  pl.BlockSpec((K_A, d), fixed), pl.BlockSpec((K_B, d), fixed), pl.BlockSpec((1, d), fixed),
                  pl.BlockSpec((1, d), fixed)],
        out_specs=[pl.BlockSpec((tm, 9 * d), row), pl.BlockSpec((1, 9 * d), fixed), pl.BlockSpec((K_A, d), fixed),
                   pl.BlockSpec((K_B, d), fixed), pl.BlockSpec((1, d), fixed), pl.BlockSpec((1, d), fixed),
                   pl.BlockSpec((1, d), fixed)],
        out_shape=[jax.ShapeDtypeStruct((s, 9 * d), BF16), jax.ShapeDtypeStruct((1, 9 * d), F32),
                   jax.ShapeDtypeStruct((K_A, d), F32), jax.ShapeDtypeStruct((K_B, d), F32), vec, vec, vec],
        scratch_shapes=[pltpu.VMEM((te, d), F32)] * 5 + [pltpu.VMEM((SUBLANES - 1, te, d), F32),
                                                         pltpu.VMEM((8, 9 * d), F32), pltpu.VMEM((K_A, 8, d), F32),
                                                         pltpu.VMEM((K_B, 8, d), F32), pltpu.VMEM((24, d), F32)],
        compiler_params=_params("arbitrary"),
    )(proj, proj, proj, cv, cv, dpa, dpa, dsw, dsw, dpc, dpc, dgates, conv_a, conv_b, ln_g, ln_b)


def _pool_wgrad(name, p, dpw, d, tk):
    s = p.shape[0]
    gc = d // N_GROUPS
    n = s // tk

    def body(p_ref, g_ref, o_ref, acc):
        k = pl.program_id(0)
        for gi in range(N_GROUPS):
            cols = slice(gi * gc, (gi + 1) * gc)
            part = lax.dot_general(p_ref[:, cols], g_ref[:, cols], TN, preferred_element_type=F32)

            @pl.when(k == 0)
            def _():
                acc[gi] = part

            @pl.when(k > 0)
            def _():
                acc[gi] += part

        @pl.when(k == n - 1)
        def _():
            for gi in range(N_GROUPS):
                o_ref[:, gi] = acc[gi].astype(o_ref.dtype).reshape(N_DEV, gc // N_DEV, gc)

    return pl.pallas_call(
        body, name=name, grid=(n,),
        in_specs=[pl.BlockSpec((tk, d), lambda k: (k, 0)), pl.BlockSpec((tk, d), lambda k: (k, 0))],
        out_specs=pl.BlockSpec((N_DEV, N_GROUPS, gc // N_DEV, gc), lambda k: (0, 0, 0, 0)),
        out_shape=jax.ShapeDtypeStruct((N_DEV, N_GROUPS, gc // N_DEV, gc), BF16),
        scratch_shapes=[pltpu.VMEM((N_GROUPS, gc, gc), F32)], compiler_params=_params("arbitrary"),
    )(p, dpw)


def _my_place():
    x, y, c = lax.axis_index("x"), lax.axis_index("y"), lax.axis_index("c")
    return x, y, c


def _block_of(x, y, c):
    return 4 * x + 2 * y + c


def _slot(ref, k, paired):
    if not paired:
        return ref.at[k]
    cols = ref.shape[-1] // 2
    return ref.at[k // 2, :, pl.ds(pl.multiple_of((k % 2) * cols, 128), cols)]


def _slot_shape(shape, paired):
    return (N_DEV // 2, shape[0], 2 * shape[1]) if paired else (N_DEV, *shape)


def _gather_shards(shards, paired):
    n_arr = len(shards)

    def body(*refs):
        srcs = refs[:n_arr]
        outs = refs[n_arr:2 * n_arr]
        send_sems, recv_sems, local_sems = refs[2 * n_arr:]
        x, y, c = _my_place()
        me, sibling = (x, y, c), (x, y, 1 - c)
        chips = [(1 - x, y), (x, 1 - y), (1 - x, 1 - y)]

        def copy(n, k, block, to, src=None):
            rows = _slot(outs[n], _block_of(*block), paired[n])
            return pltpu.make_async_remote_copy(
                src_ref=rows if src is None else src, dst_ref=rows, send_sem=send_sems.at[n, k],
                recv_sem=recv_sems.at[n, k], device_id=to, device_id_type=MESH)

        mine = [pltpu.make_async_copy(srcs[n], _slot(outs[n], _block_of(*me), paired[n]), local_sems.at[n])
                for n in range(n_arr)]
        for cp in mine:
            cp.start()
        first = []
        for n in range(n_arr):
            first.append(copy(n, 0, me, sibling, src=srcs[n]))
            first += [copy(n, 1 + j, me, (*chip, c), src=srcs[n]) for j, chip in enumerate(chips)]
        for cp in first:
            cp.start()
        passed = []
        for n in range(n_arr):
            for j, chip in enumerate(chips):
                copy(n, 1 + j, (*chip, c), me).wait_recv()
                fwd = copy(n, 4 + j, (*chip, c), sibling)
                fwd.start()
                passed.append(fwd)
        for n in range(n_arr):
            copy(n, 0, sibling, me).wait_recv()
            for j, chip in enumerate(chips):
                copy(n, 4 + j, (*chip, 1 - c), me).wait_recv()
        for cp in first + passed:
            cp.wait_send()
        for cp in mine:
            cp.wait()

    any_spec = pl.BlockSpec(memory_space=pl.ANY)
    return pl.pallas_call(
        body, name="gather_weights",
        in_specs=[any_spec] * n_arr, out_specs=[any_spec] * n_arr,
        out_shape=[jax.ShapeDtypeStruct(_slot_shape(sh.shape, p), sh.dtype) for sh, p in zip(shards, paired)],
        scratch_shapes=[pltpu.SemaphoreType.DMA((n_arr, 7)), pltpu.SemaphoreType.DMA((n_arr, 7)),
                        pltpu.SemaphoreType.DMA((n_arr,))],
    )(*shards)


def _peers(x, y, c):
    out = []
    for r in range(1, N_DEV):
        fx, fy, fc = (r >> 2) & 1, (r >> 1) & 1, r & 1
        out.append(((1 - x) if fx else x, (1 - y) if fy else y, (1 - c) if fc else c))
    return out


HBM_SPEC = pl.BlockSpec(memory_space=pltpu.HBM)
SEM_SPEC = pl.BlockSpec(memory_space=pltpu.SEMAPHORE)
ANY_SPEC = pl.BlockSpec(memory_space=pl.ANY)
N_PEERS = N_DEV - 1


def _peer_copy(src_ref, land_ref, send_sems, recv_sems, i, r, peer, me, blockwise, paired):
    src = _slot(src_ref, _block_of(*peer), paired) if blockwise else src_ref
    dst = land_ref.at[me] if blockwise else _slot(land_ref, me, paired)
    return pltpu.make_async_remote_copy(
        src_ref=src, dst_ref=dst, send_sem=send_sems.at[i * N_PEERS + r],
        recv_sem=recv_sems.at[i * N_PEERS + r], device_id=peer, device_id_type=MESH)


def _block_shape(shape, paired):
    return (shape[1], shape[2] // 2) if paired else tuple(shape[1:])


def _start_copies(name, srcs, after, blockwise, paired=None):
    n = len(srcs)
    paired = paired or [False] * n

    def body(*refs):
        s_in, l_in = refs[:n], refs[n:2 * n]
        send_sems, recv_sems = refs[2 * n + 1], refs[2 * n + 2]
        token = refs[-1]
        x, y, c = _my_place()
        me = _block_of(x, y, c)
        for i in range(n):
            for r, peer in enumerate(_peers(x, y, c)):
                _peer_copy(s_in[i], l_in[i], send_sems, recv_sems, i, r, peer, me, blockwise, paired[i]).start()
        token[...] = jnp.zeros_like(token)

    land_shapes = [(N_DEV, *_block_shape(s.shape, p)) if blockwise else _slot_shape(s.shape, p)
                   for s, p in zip(srcs, paired)]
    lands = [pltpu.with_memory_space_constraint(lax.empty(sh, s.dtype), pltpu.HBM) for sh, s in zip(land_shapes, srcs)]
    ins = [pltpu.with_memory_space_constraint(s, pltpu.HBM) for s in srcs]
    out = pl.pallas_call(
        body, name=name,
        out_shape=(pltpu.SemaphoreType.DMA((n * N_PEERS,)), pltpu.SemaphoreType.DMA((n * N_PEERS,)),
                   *[pltpu.HBM(s.shape, s.dtype) for s in srcs],
                   *[pltpu.HBM(sh, s.dtype) for sh, s in zip(land_shapes, srcs)],
                   jax.ShapeDtypeStruct((8, 128), F32)),
        in_specs=[HBM_SPEC] * (2 * n) + [ANY_SPEC],
        out_specs=(SEM_SPEC, SEM_SPEC, *[HBM_SPEC] * (2 * n), pl.BlockSpec(memory_space=pltpu.VMEM)),
        input_output_aliases={i: 2 + i for i in range(2 * n)},
        compiler_params=pltpu.CompilerParams(has_side_effects=pltpu.SideEffectType.DATAFLOW_SIDE_EFFECTING),
    )(*ins, *lands, after)
    return dict(send=out[0], recv=out[1], srcs=list(out[2:2 + n]), lands=list(out[2 + n:2 + 2 * n]), token=out[-1],
                paired=paired)


def _wait_copies(name, state, after, blockwise):
    n = len(state["srcs"])
    paired = state["paired"]

    def body(*refs):
        s_in, l_in = refs[:n], refs[n:2 * n]
        send_sems, recv_sems = refs[2 * n], refs[2 * n + 1]
        x, y, c = _my_place()
        me = _block_of(x, y, c)
        for i in range(n):
            for r, peer in enumerate(_peers(x, y, c)):
                cp = _peer_copy(s_in[i], l_in[i], send_sems, recv_sems, i, r, peer, me, blockwise, paired[i])
                cp.wait_send()
                cp.wait_recv()

    both = state["srcs"] + state["lands"]
    out = pl.pallas_call(
        body, name=name, out_shape=tuple(pltpu.HBM(a.shape, a.dtype) for a in both),
        in_specs=[HBM_SPEC] * (2 * n) + [SEM_SPEC, SEM_SPEC, ANY_SPEC], out_specs=tuple([HBM_SPEC] * (2 * n)),
        input_output_aliases={i: i for i in range(2 * n)},
        compiler_params=pltpu.CompilerParams(has_side_effects=pltpu.SideEffectType.DATAFLOW_SIDE_EFFECTING),
    )(*both, state["send"], state["recv"], after)
    return list(out[:n]), list(out[n:])


COPY_BLOCK_BYTES = 2 * 1024 * 1024


def _place_own(name, lands, srcs, me, blockwise, paired=None):
    out = []
    paired = paired or [False] * len(lands)
    for i, (land, src) in enumerate(zip(lands, srcs)):
        in_slots = src if blockwise else land
        part = _block_shape(in_slots.shape, paired[i])
        row_bytes = land.dtype.itemsize
        for extent in part[1:]:
            row_bytes *= extent
        tr = part[0]
        while tr * row_bytes > COPY_BLOCK_BYTES and tr % 16 == 0:
            tr //= 2
        tail = (0,) * (len(part) - 1)

        def body(me_ref, s_ref, l_ref, o_ref):
            o_ref[...] = s_ref[...]

        if paired[i]:
            slot_spec = pl.BlockSpec((None, tr, part[1]), lambda j, me_ref: (me_ref[0] // 2, j, me_ref[0] % 2))
        else:
            slot_spec = pl.BlockSpec((None, tr, *part[1:]), lambda j, me_ref: (me_ref[0], j, *tail))
        if blockwise:
            s_spec = slot_spec
            o_spec = pl.BlockSpec((None, tr, *part[1:]), lambda j, me_ref: (me_ref[0], j, *tail))
        else:
            s_spec = pl.BlockSpec((tr, *part[1:]), lambda j, me_ref: (j, *tail))
            o_spec = slot_spec
        out.append(pl.pallas_call(
            body, name=f"{name}_{i}",
            grid_spec=pltpu.PrefetchScalarGridSpec(
                num_scalar_prefetch=1, grid=(part[0] // tr,), in_specs=[s_spec, ANY_SPEC], out_specs=o_spec),
            out_shape=jax.ShapeDtypeStruct(land.shape, land.dtype), input_output_aliases={2: 0},
            compiler_params=_params("parallel"),
        )(me, src, land))
    return out


def _adamw_math(w, g, m, v):
    m = ADAM_B1 * m + (1.0 - ADAM_B1) * g
    v = ADAM_B2 * v + (1.0 - ADAM_B2) * (g * g)
    m_hat = m / (1.0 - ADAM_B1 ** ADAM_STEP)
    v_hat = v / (1.0 - ADAM_B2 ** ADAM_STEP)
    delta = -ADAM_LR * (m_hat / (jnp.sqrt(v_hat) + ADAM_EPS) + ADAM_WD * w)
    return delta, m, v


def _adamw(name, parts, w, m, v, *, grid, part_specs, w_spec):
    n_layers = len(parts)
    n_parts = parts[0].shape[0]

    def body(*refs):
        p_refs = refs[:n_layers]
        w_ref, m_ref, v_ref, g_ref, d_ref, nm_ref, nv_ref = refs[n_layers:]

        def total(p_ref):
            t = p_ref[0].astype(F32)
            for k in range(1, n_parts):
                t = t + p_ref[k].astype(F32)
            return t

        g = total(p_refs[0])
        for li in range(1, n_layers):
            g = jnp.where(pl.program_id(0) == li, total(p_refs[li]), g)
        delta, nm, nv = _adamw_math(w_ref[...], g, m_ref[...], v_ref[...])
        g_ref[...] = g
        d_ref[...] = delta
        nm_ref[...] = nm
        nv_ref[...] = nv

    out = jax.ShapeDtypeStruct(w.shape, F32)
    return pl.pallas_call(
        body, name=name, grid=grid, in_specs=[*part_specs, w_spec, w_spec, w_spec], out_specs=[w_spec] * 4,
        out_shape=[out] * 4, compiler_params=_params(*(("parallel",) * len(grid))),
    )(*parts, w, m, v)


def _layer_part_spec(layer, block, n_blocks, row_off=0):
    def index_map(l, i):
        ii = jnp.where(l == layer, i, jnp.where(l < layer, 0, n_blocks - 1))
        return (0, row_off + ii) + (0,) * (len(block) - 2)
    return pl.BlockSpec(block, index_map)


def _small_update(partials, triples, conv_rows):
    d = partials[-1].shape[-1]
    n_rep = len(triples)
    n_part = len(partials)
    rows = []
    for p in partials:
        rows.append(p.shape[0] * (p.shape[1] // d))
    offs = [sum(rows[:i]) for i in range(n_part)]
    total = -(-sum(rows) // 8) * 8

    def body(*refs):
        p_refs = refs[:n_part]
        wmv = refs[n_part:n_part + 3 * n_rep]
        outs = refs[n_part + 3 * n_rep:n_part + 3 * n_rep + 4 * n_rep + (n_part - n_rep)]
        buf, send_sems, recv_sems = refs[-3:]
        x, y, c = _my_place()
        me = _block_of(x, y, c)
        peers = _peers(x, y, c)
        mine = buf.at[me]
        if total > sum(rows):
            mine[sum(rows):total, :] = jnp.zeros((total - sum(rows), d), F32)
        for p_ref, off in zip(p_refs, offs):
            nr, nc = p_ref.shape[0], p_ref.shape[1] // d
            if nc == 1:
                mine[off:off + nr, :] = p_ref[...]
            else:
                for r in range(nr):
                    for q in range(nc):
                        mine[off + r * nc + q:off + r * nc + q + 1, :] = p_ref[r:r + 1, q * d:(q + 1) * d]
        sends =[pltpu.make_async_remote_copy(
            src_ref=buf.at[me], dst_ref=buf.at[me], send_sem=send_sems.at[r], recv_sem=recv_sems.at[r],
            device_id=peer, device_id_type=MESH) for r, peer in enumerate(peers)]
        for cp in sends:
            cp.start()
        for r, peer in enumerate(peers):
            pltpu.make_async_remote_copy(
                src_ref=buf.at[me], dst_ref=buf.at[_block_of(*peer)], send_sem=send_sems.at[r],
                recv_sem=recv_sems.at[r], device_id=peer, device_id_type=MESH).wait_recv()
        for cp in sends:
            cp.wait_send()
        tot = buf[0]
        for k in range(1, N_DEV):
            tot = tot + buf[k]
        buf[0] = tot
        for idx in range(n_part):
            nr, nc = p_refs[idx].shape[0], p_refs[idx].shape[1] // d
            if idx < n_rep:
                w_ref, m_ref, v_ref = wmv[3 * idx:3 * idx + 3]
                g_ref, d_ref, nm_ref, nv_ref = outs[4 * idx:4 * idx + 4]
            else:
                g_ref = outs[4 * n_rep + idx - n_rep]
            pieces = [(slice(0, nr), slice(0, d), offs[idx], nr)] if nc == 1 else [
                (slice(r, r + 1), slice(q * d, (q + 1) * d), offs[idx] + r * nc + q, 1)
                for r in range(nr) for q in range(nc)]
            for rws, cols, row, cnt in pieces:
                g = buf[0, row:row + cnt, :]
                g_ref[rws, cols] = g
                if idx < n_rep:
                    delta, nm, nv = _adamw_math(w_ref[rws, cols], g, m_ref[rws, cols], v_ref[rws, cols])
                    d_ref[rws, cols] = delta
                    nm_ref[rws, cols] = nm
                    nv_ref[rws, cols] = nv

    vm = pl.BlockSpec(memory_space=pltpu.VMEM)
    operands = list(partials)
    for t in triples:
        operands += list(t)
    out_shape = []
    for idx in range(n_rep):
        out_shape += [jax.ShapeDtypeStruct(partials[idx].shape, F32)] * 4
    for idx in range(n_rep, n_part):
        out_shape.append(jax.ShapeDtypeStruct(partials[idx].shape, F32))
    return pl.pallas_call(
        body, name="small_allreduce_adamw", in_specs=[vm] * len(operands), out_specs=[vm] * len(out_shape),
        out_shape=out_shape,
        scratch_shapes=[pltpu.VMEM((N_DEV, total, d), F32), pltpu.SemaphoreType.DMA((7,)), pltpu.SemaphoreType.DMA((7,))],
        compiler_params=pltpu.CompilerParams(vmem_limit_bytes=VMEM_LIMIT_BYTES),
    )(*operands)


def kernel(x, g_mix, w_in, b_in, conv_a, w_out_a, conv_b, conv_b_bias, ln_b_g, ln_b_b, w_out_b, b_out_b, w_pool, pool_scale, w_o, g_mlp, w_mlp1, w_mlp2, g_final, loss_target, m_g_mix, m_w_in, m_b_in, m_conv_a, m_w_out_a, m_conv_b, m_conv_b_bias, m_ln_b_g, m_ln_b_b, m_w_out_b, m_b_out_b, m_w_pool, m_pool_scale, m_w_o, m_g_mlp, m_w_mlp1, m_w_mlp2, m_g_final, v_g_mix, v_w_in, v_b_in, v_conv_a, v_w_out_a, v_conv_b, v_conv_b_bias, v_ln_b_g, v_ln_b_b, v_w_out_b, v_b_out_b, v_w_pool, v_pool_scale, v_w_o, v_g_mlp, v_w_mlp1, v_w_mlp2, v_g_final):
    _, s, d = x.shape
    n_layers = g_mix.shape[0]
    p_in = b_in.shape[1]
    ci = w_in.shape[2]
    c1 = w_mlp1.shape[2]
    rf = w_mlp2.shape[1]
    rd = w_out_a.shape[1]
    f = rf * N_DEV
    rp = rf + 3 * rd
    o_a, o_b, o_o = rf // rd, rf // rd + 1, rf // rd + 2
    gc = d // N_GROUPS
    ca_rows = 8
    tm = min(1024, s)
    tr = min(512, s)
    tx = min(256, s)
    tk = min(2048, s)
    tk_mlp = min(4096, s)
    tk_in = min(1024, s)

    me_arr = jnp.reshape(_block_of(*_my_place()), (1,)).astype(jnp.int32)

    def layer_shards(l):
        row_pack = jnp.concatenate([w_mlp2[l], w_out_a[l], w_out_b[l], w_o[l]], axis=0).astype(BF16)
        return [w_in[l].astype(BF16), w_mlp1[l].astype(BF16), row_pack, w_pool[l].astype(BF16)]

    conv_pack = jnp.concatenate(
        [conv_a, jnp.zeros((n_layers, ca_rows - K_A, rd), F32), conv_b], axis=1)
    first_shards = layer_shards(0)
    layer_pairing = [True, False, False, False]
    g_in_first, g_conv = _gather_shards([first_shards[0], conv_pack], [True, False])
    conv_full = jnp.transpose(g_conv, (1, 2, 0, 3)).reshape(n_layers, ca_rows + K_B, d)
    conv_a_f = conv_full[:, :K_A]
    conv_b_f = conv_full[:, ca_rows:]
    first_row_going = _start_copies("gather_start_row_0", first_shards[2:], g_conv, blockwise=False)
    in_flight = [_start_copies("gather_start_mlp1_0", first_shards[1:2], first_row_going["token"], blockwise=False)]
    for l in range(1, n_layers):
        in_flight.append(_start_copies(f"gather_start_{l}", layer_shards(l), in_flight[-1]["token"], blockwise=False,
                                       paired=layer_pairing))
    token = in_flight[-1]["token"][0:1, 0:1]

    xs = [x[0]]
    saved = []
    weights = []
    row2 = lambda j, i: (i, 0)
    for l in range(n_layers):
        x0 = xs[-1]
        vec = lambda a: a[l:l + 1]
        if l > 0:
            srcs, lands = _wait_copies(f"gather_wait_{l}", in_flight[l], x0, blockwise=False)
            g_in, g_1, g_row, g_pool = _place_own(f"gather_own_{l}", lands, srcs, me_arr, blockwise=False,
                                                  paired=layer_pairing)
        else:
            g_in = g_in_first
        h = _rms_fwd(f"rms_mix_{l}", x0, vec(g_mix) + token if l == 0 else vec(g_mix), tr)
        proj = _mm(
            f"proj_{l}", h, g_in, grid=(N_DEV // 2, s // tm), a_spec=pl.BlockSpec((tm, d), row2),
            b_spec=pl.BlockSpec((None, d, 2 * ci), lambda j, i: (j, 0, 0)),
            extras=(vec(b_in),), extra_specs=(pl.BlockSpec((1, 2 * ci), lambda j, i: (0, j)),),
            epilogue=lambda v, b: v + b, out_shape=jax.ShapeDtypeStruct((s, p_in), BF16),
            o_spec=pl.BlockSpec((tm, 2 * ci), lambda j, i: (i, j)), dims=NN)
        if l == 0:
            srcs, lands = _wait_copies("gather_wait_row_0", first_row_going, proj, blockwise=False)
            g_row, g_pool = _place_own("gather_own_row_0", lands, srcs, me_arr, blockwise=False)
        p_a, sw, p_c, cv, y_a, y_b, pw, merged, x1 = _mixer_fwd(
            f"mix_fwd_{l}", proj, x0, conv_a_f[l], conv_b_f[l], vec(conv_b_bias), vec(ln_b_g), vec(ln_b_b),
            vec(b_out_b), vec(pool_scale), g_row, g_pool, (o_a, o_b, o_o), d, tx)
        if l == 0:
            srcs, lands = _wait_copies("gather_wait_mlp1_0", in_flight[0], x1, blockwise=False)
            g_1, = _place_own("gather_own_mlp1_0", lands, srcs, me_arr, blockwise=False)
        weights.append((g_in, g_1, g_row, g_pool))
        h2 = _rms_fwd(f"rms_mlp_{l}", x1, vec(g_mlp), tr)
        a_pre = _mm(f"mlp1_{l}", h2, g_1, grid=(N_DEV // 2, s // tm), a_spec=pl.BlockSpec((tm, d), row2),
                    b_spec=pl.BlockSpec((2, d, c1), lambda j, i: (j, 0, 0)), slabs="n",
                    out_shape=jax.ShapeDtypeStruct((s, f), BF16),
                    o_spec=pl.BlockSpec((tm, 2 * c1), lambda j, i: (i, j)), dims=NN)
        x2 = _mm(f"mlp2_{l}", a_pre, g_row, grid=(1, s // tr), a_spec=pl.BlockSpec((tr, f), row2),
                 b_spec=pl.BlockSpec((N_DEV, rf, d), lambda j, i: (0, 0, 0)), prologue=_relu_sq,
                 extras=(x1,), extra_specs=(pl.BlockSpec((tr, d), row2),), epilogue=lambda v, r: v + r,
                 out_shape=jax.ShapeDtypeStruct((s, d), F32), o_spec=pl.BlockSpec((tr, d), row2), dims=NN)
        saved.append((x0, h, proj, p_a, sw, p_c, cv, y_a, y_b, pw, merged, x1, h2, a_pre))
        xs.append(x2)

    loss_part, dx, dx16, dg_final = _loss_head(xs[-1], g_final.reshape(1, d), loss_target[0], tr)
    loss = lax.psum(loss_part[0, 0], ("x", "y", "c"))

    small = [None] * n_layers
    exchanges = [None] * n_layers
    for l in reversed(range(n_layers)):
        x0, h, proj, p_a, sw, p_c, cv, y_a, y_b, pw, merged, x1, h2, a_pre = saved[l]
        g_in, g_1, g_row, g_pool = weights[l]
        vec = lambda a: a[l:l + 1]
        row_shape = jax.ShapeDtypeStruct((N_DEV, rp, d), BF16)

        def dd_grad(name, a, g, off, alias):
            return _mm(name, a, g, grid=(1, s // tk), a_spec=pl.BlockSpec((tk, d), lambda j, k: (k, 0)),
                       b_spec=pl.BlockSpec((tk, d), lambda j, k: (k, 0)), out_shape=row_shape,
                       o_spec=pl.BlockSpec((N_DEV, rd, d), lambda j, k: (0, off, 0)), dims=TN, nk=s // tk,
                       acc_shape=(d, d), alias_in=alias)

        d_a = _mm(f"d_act_{l}", dx16, g_row, grid=(N_DEV // 2, s // tm), a_spec=pl.BlockSpec((tm, d), row2),
                  b_spec=pl.BlockSpec((2, rf, d), lambda j, i: (j, 0, 0)), slabs="n",
                  extras=(a_pre,), extra_specs=(pl.BlockSpec((tm, 2 * rf), lambda j, i: (i, j)),),
                  epilogue=lambda v, a: v * (2.0 * jnp.maximum(a.astype(F32), 0.0)),
                  out_shape=jax.ShapeDtypeStruct((s, f), BF16),
                  o_spec=pl.BlockSpec((tm, 2 * rf), lambda j, i: (i, j)), dims=NT)
        dg_row = _mm(f"dw_mlp2_{l}", a_pre, dx16, grid=(N_DEV, s // tk_mlp),
                     a_spec=pl.BlockSpec((tk_mlp, rf), lambda j, k: (k, j)),
                     b_spec=pl.BlockSpec((tk_mlp, d), lambda j, k: (k, 0)), prologue=_relu_sq, out_shape=row_shape,
                     o_spec=pl.BlockSpec((None, rf, d), lambda j, k: (j, 0, 0)), dims=TN, nk=s // tk_mlp,
                     acc_shape=(rf, d))
        d_h2 = _mm(f"d_h2_{l}", d_a, g_1, grid=(1, s // tr), a_spec=pl.BlockSpec((tr, f), row2),
                   b_spec=pl.BlockSpec((N_DEV, d, c1), lambda j, i: (0, 0, 0)), slabs="k",
                   out_shape=jax.ShapeDtypeStruct((s, d), BF16), o_spec=pl.BlockSpec((tr, d), row2), dims=NT)
        dg_1 = _mm(f"dw_mlp1_{l}", h2, d_a, grid=(N_DEV, s // tk_mlp),
                   a_spec=pl.BlockSpec((tk_mlp, d), lambda j, k: (k, 0)),
                   b_spec=pl.BlockSpec((tk_mlp, c1), lambda j, k: (k, j)),
                   out_shape=jax.ShapeDtypeStruct((N_DEV, d, c1), BF16),
                   o_spec=pl.BlockSpec((None, d, c1), lambda j, k: (j, 0, 0)), dims=TN, nk=s // tk_mlp,
                   acc_shape=(d, c1))
        mlp1_going = _start_copies(f"grads_start_mlp1_{l}", [dg_1], vec(g_mlp), blockwise=True)
        dx, dx16, dg_mlp = _rms_bwd(f"rms_mlp_bwd_{l}", d_h2, x1, vec(g_mlp) + mlp1_going["token"][0:1, 0:1], dx, tr)
        d_ya, d_yb, d_pw, d_gates, d_pa, d_sw, d_pc, d_bout, d_pscale = _merge_bwd(
            f"merge_bwd_{l}", dx16, proj, y_a, y_b, pw, vec(pool_scale), g_row, g_pool, (o_a, o_b, o_o), d, tx)
        dg_row = dd_grad(f"dw_o_{l}", merged, dx16, o_o, dg_row)
        dg_row = dd_grad(f"dw_out_a_{l}", p_a, d_ya, o_a, dg_row)
        dg_row = dd_grad(f"dw_out_b_{l}", sw, d_yb, o_b, dg_row)
        dg_pool = _pool_wgrad(f"dw_pool_{l}", p_c, d_pw, d, tk)
        rest_going = _start_copies(f"grads_start_rest_{l}", [dg_row, dg_pool], vec(g_mlp), blockwise=True)
        d_proj, d_bin, d_ca, d_cb, d_cbb, d_lng, d_lnb = _mix_pre_bwd(
            f"mix_bwd_{l}", proj, cv, d_pa, d_sw, d_pc, d_gates, conv_a_f[l], conv_b_f[l],
            vec(ln_b_g) + rest_going["token"][0:1, 0:1], vec(ln_b_b), d, tx)
        dg_in = _mm(f"dw_in_{l}", h, d_proj, grid=(N_DEV // 2, s // tk_in),
                    a_spec=pl.BlockSpec((tk_in, d), lambda j, k: (k, 0)),
                    b_spec=pl.BlockSpec((tk_in, 2 * ci), lambda j, k: (k, j)),
                    out_shape=jax.ShapeDtypeStruct((N_DEV // 2, d, 2 * ci), BF16),
                    o_spec=pl.BlockSpec((None, d, 2 * ci), lambda j, k: (j, 0, 0)), dims=TN, nk=s // tk_in,
                    acc_shape=(d, 2 * ci), after=rest_going["token"])
        in_going = _start_copies(f"grads_start_in_{l}", [dg_in], vec(g_mix), blockwise=True, paired=[True])
        d_h = _mm(f"d_h_{l}", d_proj, g_in, grid=(s // tm, N_DEV // 2),
                  a_spec=pl.BlockSpec((tm, 2 * ci), lambda i, k: (i, k)),
                  b_spec=pl.BlockSpec((None, d, 2 * ci), lambda i, k: (k, 0, 0)),
                  out_shape=jax.ShapeDtypeStruct((s, d), BF16), o_spec=pl.BlockSpec((tm, d), lambda i, k: (i, 0)),
                  dims=NT, nk=N_DEV // 2, acc_shape=(tm, d), after=in_going["token"])
        dx, dx16, dg_mix = _rms_bwd(f"rms_mix_bwd_{l}", d_h, x0, vec(g_mix), dx, tr)
        small[l] = (dg_mix, d_bin, d_cbb, d_lng, d_lnb, d_bout, d_pscale, dg_mlp, d_ca, d_cb)
        exchanges[l] = (in_going, mlp1_going, rest_going)

    grad_x = dx[None]

    names = ("g_mix", "b_in", "conv_b_bias", "ln_b_g", "ln_b_b", "b_out_b", "pool_scale", "g_mlp")
    given = dict(g_mix=(g_mix, m_g_mix, v_g_mix), b_in=(b_in, m_b_in, v_b_in),
                 conv_b_bias=(conv_b_bias, m_conv_b_bias, v_conv_b_bias), ln_b_g=(ln_b_g, m_ln_b_g, v_ln_b_g),
                 ln_b_b=(ln_b_b, m_ln_b_b, v_ln_b_b), b_out_b=(b_out_b, m_b_out_b, v_b_out_b),
                 pool_scale=(pool_scale, m_pool_scale, v_pool_scale), g_mlp=(g_mlp, m_g_mlp, v_g_mlp))
    partials, triples = [], []
    for i, nm in enumerate(names):
        partials.append(jnp.concatenate([small[l][i] for l in range(n_layers)], axis=0))
        triples.append(given[nm])
    partials.append(dg_final)
    triples.append(tuple(a.reshape(1, d) for a in (g_final, m_g_final, v_g_final)))
    partials.append(jnp.concatenate([small[l][8] for l in range(n_layers)], axis=0))
    partials.append(jnp.concatenate([small[l][9] for l in range(n_layers)], axis=0))
    outs = _small_update(partials, triples, 2)
    rep = {nm: outs[4 * i:4 * i + 4] for i, nm in enumerate(names)}
    rep["g_final"] = [a.reshape(d) for a in outs[4 * len(names):4 * len(names) + 4]]
    me = _block_of(*_my_place())
    gca = lax.dynamic_slice_in_dim(outs[-2].reshape(n_layers, K_A, d), me * rd, rd, axis=2)
    gcb = lax.dynamic_slice_in_dim(outs[-1].reshape(n_layers, K_B, d), me * rd, rd, axis=2)

    r_in, r_1, r_row, r_pool = [], [], [], []
    for l in reversed(range(n_layers)):
        in_going, mlp1_going, rest_going = exchanges[l]
        srcs_m, lands_m = _wait_copies(f"grads_wait_mlp1_{l}", mlp1_going, outs[0], blockwise=True)
        srcs_r, lands_r = _wait_copies(f"grads_wait_rest_{l}", rest_going, outs[0], blockwise=True)
        srcs_i, lands_i = _wait_copies(f"grads_wait_in_{l}", in_going, outs[0], blockwise=True)
        got = _place_own(f"grads_own_{l}", lands_i + lands_m + lands_r, srcs_i + srcs_m + srcs_r, me_arr, blockwise=True,
                         paired=layer_pairing)
        for lst, arr in zip((r_in, r_1, r_row, r_pool), got):
            lst.insert(0, arr)
    tb = min(256, d)
    layers = range(n_layers)
    res = {}
    res["w_in"] = _adamw("adamw_w_in", r_in, w_in, m_w_in, v_w_in, grid=(n_layers, d // tb),
                         part_specs=[_layer_part_spec(li, (N_DEV, tb, ci), d // tb) for li in layers],
                         w_spec=pl.BlockSpec((None, tb, ci), lambda l, i: (l, i, 0)))
    res["w_mlp1"] = _adamw("adamw_w_mlp1", r_1, w_mlp1, m_w_mlp1, v_w_mlp1, grid=(n_layers, d // tb),
                           part_specs=[_layer_part_spec(li, (N_DEV, tb, c1), d // tb) for li in layers],
                           w_spec=pl.BlockSpec((None, tb, c1), lambda l, i: (l, i, 0)))
    tf = min(256, rf)
    res["w_mlp2"] = _adamw("adamw_w_mlp2", r_row, w_mlp2, m_w_mlp2, v_w_mlp2, grid=(n_layers, rf // tf),
                           part_specs=[_layer_part_spec(li, (N_DEV, tf, d), rf // tf) for li in layers],
                           w_spec=pl.BlockSpec((None, tf, d), lambda l, i: (l, i, 0)))
    for nm, off, trip in (("w_out_a", o_a, (w_out_a, m_w_out_a, v_w_out_a)),
                          ("w_out_b", o_b, (w_out_b, m_w_out_b, v_w_out_b)), ("w_o", o_o, (w_o, m_w_o, v_w_o))):
        res[nm] = _adamw(f"adamw_{nm}", r_row, *trip, grid=(n_layers, 1),
                         part_specs=[_layer_part_spec(li, (N_DEV, rd, d), 1, row_off=off) for li in layers],
                         w_spec=pl.BlockSpec((None, rd, d), lambda l, i: (l, 0, 0)))
    res["w_pool"] = _adamw("adamw_w_pool", r_pool, w_pool, m_w_pool, v_w_pool, grid=(n_layers, 1),
                           part_specs=[_layer_part_spec(li, (N_DEV, N_GROUPS, gc // N_DEV, gc), 1) for li in layers],
                           w_spec=pl.BlockSpec((None, N_GROUPS, gc // N_DEV, gc), lambda l, i: (l, 0, 0, 0)))
    whole3 = lambda: (0, 0, 0)
    res["conv_a"] = _adamw("adamw_conv_a", [gca[None]], conv_a, m_conv_a, v_conv_a, grid=(),
                           part_specs=[pl.BlockSpec((1, n_layers, K_A, rd), lambda: (0, 0, 0, 0))],
                           w_spec=pl.BlockSpec((n_layers, K_A, rd), whole3))
    res["conv_b"] = _adamw("adamw_conv_b", [gcb[None]], conv_b, m_conv_b, v_conv_b, grid=(),
                           part_specs=[pl.BlockSpec((1, n_layers, K_B, rd), lambda: (0, 0, 0, 0))],
                           w_spec=pl.BlockSpec((n_layers, K_B, rd), whole3))
    res.update(rep)

    order = ("g_mix", "w_in", "b_in", "conv_a", "w_out_a", "conv_b", "conv_b_bias", "ln_b_g", "ln_b_b", "w_out_b",
             "b_out_b", "w_pool", "pool_scale", "w_o", "g_mlp", "w_mlp1", "w_mlp2", "g_final")
    out = [loss, grad_x]
    for kind in range(4):
        out += [res[nm][kind] for nm in order]
    return tuple(out)
```

```python
import jax
import jax.numpy as jnp
from jax import lax
from jax.experimental import pallas as pl
from jax.experimental.pallas import tpu as pltpu

F32 = jnp.float32
BF16 = jnp.bfloat16
MESH = pl.DeviceIdType.MESH

N_DEV = 8
EPS = 1e-6
K_A = 3
K_B = 31
POOL_WINDOWS = (2, 4, 8, 16)
N_GROUPS = len(POOL_WINDOWS)
HALO = 32
CHUNK = 16
SUBLANES = 8
TAP_GROUP = 4
ADAM_LR, ADAM_B1, ADAM_B2, ADAM_EPS, ADAM_WD, ADAM_STEP = 0.001, 0.9, 0.999, 1e-08, 0.01, 10
VMEM_LIMIT_BYTES = 60 * 1024 * 1024

NN = (((1,), (0,)), ((), ()))
NT = (((1,), (1,)), ((), ()))
TN = (((0,), (0,)), ((), ()))


def _params(*sem):
    return pltpu.CompilerParams(dimension_semantics=sem, vmem_limit_bytes=VMEM_LIMIT_BYTES)


def _sigmoid(v):
    return 1.0 / (1.0 + jnp.exp(-v))


def _mm(name, a, b, *, grid, a_spec, b_spec, out_shape, o_spec, dims, nk=1, acc_shape=None,
        extras=(), extra_specs=(), prologue=None, epilogue=None, alias_in=None, slabs=None, after=None):
    n_extra = len(extras)
    has_alias = alias_in is not None
    n_unread = (1 if has_alias else 0) + (1 if after is not None else 0)

    def body(*refs):
        a_ref, b_ref = refs[0], refs[1]
        ex = refs[2:2 + n_extra]
        o_ref = refs[2 + n_extra + n_unread]
        av = a_ref[...]
        if prologue is not None:
            av = prologue(av)
        av = av.astype(BF16)

        def finish(val, cols=None):
            if epilogue is not None:
                val = epilogue(val, *[e[...] if cols is None else e[:, cols] for e in ex])
            if cols is None:
                o_ref[...] = val.astype(o_ref.dtype).reshape(o_ref.shape)
            else:
                o_ref[:, cols] = val.astype(o_ref.dtype)

        if slabs == "n":
            for q in range(b_ref.shape[0]):
                pq = lax.dot_general(av, b_ref[q].astype(BF16), dims, preferred_element_type=F32)
                finish(pq, slice(q * pq.shape[1], (q + 1) * pq.shape[1]))
            return
        if slabs == "k":
            kc = av.shape[1] // b_ref.shape[0]
            p = None
            for q in range(b_ref.shape[0]):
                pq = lax.dot_general(av[:, q * kc:(q + 1) * kc], b_ref[q].astype(BF16), dims,
                                     preferred_element_type=F32)
                p = pq if p is None else p + pq
        else:
            bv = b_ref[...]
            bv = bv.reshape((-1, bv.shape[-1])).astype(BF16)
            p = lax.dot_general(av, bv, dims, preferred_element_type=F32)

        if nk == 1:
            finish(p)
        else:
            acc = refs[-1]
            k = pl.program_id(len(grid) - 1)

            @pl.when(k == 0)
            def _():
                acc[...] = p

            @pl.when(k > 0)
            def _():
                acc[...] += p

            @pl.when(k == nk - 1)
            def _():
                finish(acc[...])

    in_specs = [a_spec, b_spec, *extra_specs]
    operands = [a, b, *extras]
    aliases = {}
    if has_alias:
        in_specs.append(pl.BlockSpec(memory_space=pl.ANY))
        operands.append(alias_in)
        aliases = {len(operands) - 1: 0}
    if after is not None:
        in_specs.append(pl.BlockSpec(memory_space=pl.ANY))
        operands.append(after)
    sem =("parallel",) * (len(grid) - 1) + (("arbitrary",) if nk > 1 else ("parallel",))
    return pl.pallas_call(
        body, name=name, grid=grid, in_specs=in_specs, out_specs=o_spec, out_shape=out_shape,
        scratch_shapes=[pltpu.VMEM(acc_shape, F32)] if nk > 1 else [],
        input_output_aliases=aliases, compiler_params=_params(*sem),
    )(*operands)


def _relu_sq(v):
    r = jnp.maximum(v, 0)
    return r * r


def _rms_fwd(name, x, g, tm):
    s, d = x.shape

    def body(x_ref, g_ref, h_ref):
        xv = x_ref[...]
        r = lax.rsqrt(jnp.mean(xv * xv, axis=-1, keepdims=True) + EPS)
        h_ref[...] = (xv * r * g_ref[...]).astype(h_ref.dtype)

    return pl.pallas_call(
        body, name=name, grid=(s // tm,),
        in_specs=[pl.BlockSpec((tm, d), lambda i: (i, 0)), pl.BlockSpec((1, d), lambda i: (0, 0))],
        out_specs=pl.BlockSpec((tm, d), lambda i: (i, 0)),
        out_shape=jax.ShapeDtypeStruct((s, d), BF16), compiler_params=_params("parallel"),
    )(x, g)


def _colsum8(v):
    return jnp.sum(v.reshape(v.shape[0] // 8, 8, v.shape[1]), axis=0)


def _rms_bwd(name, dh, x, g, dres, tm):
    s, d = x.shape
    n = s // tm

    def body(dh_ref, x_ref, g_ref, dr_ref, dx_ref, dx16_ref, dg_ref, acc):
        i = pl.program_id(0)
        xv = x_ref[...]
        r = lax.rsqrt(jnp.mean(xv * xv, axis=-1, keepdims=True) + EPS)
        xh = xv * r
        dhv = dh_ref[...].astype(F32)
        part = _colsum8(dhv * xh)

        @pl.when(i == 0)
        def _():
            acc[...] = part

        @pl.when(i > 0)
        def _():
            acc[...] += part

        dxh = dhv * g_ref[...]
        dx = r * (dxh - xh * jnp.mean(dxh * xh, axis=-1, keepdims=True))
        dx = dx + dr_ref[...]
        dx_ref[...] = dx
        dx16_ref[...] = dx.astype(BF16)

        @pl.when(i == n - 1)
        def _():
            dg_ref[...] = jnp.sum(acc[...], axis=0, keepdims=True)

    return pl.pallas_call(
        body, name=name, grid=(n,),
        in_specs=[pl.BlockSpec((tm, d), lambda i: (i, 0)), pl.BlockSpec((tm, d), lambda i: (i, 0)),
                  pl.BlockSpec((1, d), lambda i: (0, 0)), pl.BlockSpec((tm, d), lambda i: (i, 0))],
        out_specs=[pl.BlockSpec((tm, d), lambda i: (i, 0)), pl.BlockSpec((tm, d), lambda i: (i, 0)),
                   pl.BlockSpec((1, d), lambda i: (0, 0))],
        out_shape=[jax.ShapeDtypeStruct((s, d), F32), jax.ShapeDtypeStruct((s, d), BF16),
                   jax.ShapeDtypeStruct((1, d), F32)],
        scratch_shapes=[pltpu.VMEM((8, d), F32)], compiler_params=_params("arbitrary"),
    )(dh, x, g, dres)


def _loss_head(x, g, target, tm):
    s, d = x.shape
    n = s // tm

    def body(x_ref, g_ref, t_ref, loss_ref, dx_ref, dx16_ref, dg_ref, acc_l, acc_g):
        i = pl.program_id(0)
        xv = x_ref[...]
        r = lax.rsqrt(jnp.mean(xv * xv, axis=-1, keepdims=True) + EPS)
        xh = xv * r
        err = xh * g_ref[...] - t_ref[...]
        dy = err * (1.0 / d)
        lpart = _colsum8(err * err)
        gpart = _colsum8(dy * xh)

        @pl.when(i == 0)
        def _():
            acc_l[...] = lpart
            acc_g[...] = gpart

        @pl.when(i > 0)
        def _():
            acc_l[...] += lpart
            acc_g[...] += gpart

        dxh = dy * g_ref[...]
        dx = r * (dxh - xh * jnp.mean(dxh * xh, axis=-1, keepdims=True))
        dx_ref[...] = dx
        dx16_ref[...] = dx.astype(BF16)

        @pl.when(i == n - 1)
        def _():
            loss_ref[...] = (0.5 / d) * jnp.sum(jnp.sum(acc_l[...], axis=0, keepdims=True), axis=1, keepdims=True)
            dg_ref[...] = jnp.sum(acc_g[...], axis=0, keepdims=True)

    return pl.pallas_call(
        body, name="loss_head", grid=(n,),
        in_specs=[pl.BlockSpec((tm, d), lambda i: (i, 0)), pl.BlockSpec((1, d), lambda i: (0, 0)),
                  pl.BlockSpec((tm, d), lambda i: (i, 0))],
        out_specs=[pl.BlockSpec((1, 1), lambda i: (0, 0)), pl.BlockSpec((tm, d), lambda i: (i, 0)),
                   pl.BlockSpec((tm, d), lambda i: (i, 0)), pl.BlockSpec((1, d), lambda i: (0, 0))],
        out_shape=[jax.ShapeDtypeStruct((1, 1), F32), jax.ShapeDtypeStruct((s, d), F32),
                   jax.ShapeDtypeStruct((s, d), BF16), jax.ShapeDtypeStruct((1, d), F32)],
        scratch_shapes=[pltpu.VMEM((8, d), F32), pltpu.VMEM((8, d), F32)], compiler_params=_params("arbitrary"),
    )(x, g, target)


def _sec(ref, n, d):
    return ref[:, n * d:(n + 1) * d].astype(F32)


def _fill_shifts(sh, ext):
    rows = ext.shape[0] - SUBLANES
    for b in range(1, SUBLANES):
        sh[b - 1, 0:rows, :] = ext[b:b + rows, :]


def _shifted(sh, ext, off, n):
    b = off % SUBLANES
    if b == 0:
        return ext[off:off + n, :]
    return sh[b - 1, off - b:off - b + n, :]


def _pool_count(row0, rows, window):
    t = row0 + lax.broadcasted_iota(jnp.int32, (rows, 1), 0)
    return jnp.minimum(t + 1, window).astype(F32)


def _group_weight(w_ref, gi, gc):
    return w_ref[:, gi].reshape(gc, gc)


def _mixer_fwd(name, proj, x0, conv_a, conv_b, conv_b_bias, ln_g, ln_b, b_out_b, pool_scale, g_row, g_pool, offs, d, tm):
    s = proj.shape[0]
    n = s // tm
    gc = d // N_GROUPS
    hb = tm // HALO
    rd = d // N_DEV
    o_a, o_b, o_o = offs

    def body(pj_ref, hp_ref, x0_ref, ca_ref, cb_ref, cbb_ref, lng_ref, lnb_ref, bo_ref, sc_ref, wa_ref, wb_ref, wo_ref,
             wp_ref, pa_ref, sw_ref, pc_ref, cv_ref, ya_ref, yb_ref, pw_ref, mg_ref, x1_ref, eua, eub, euc, sh):
        i = pl.program_id(0)
        keep = (i > 0).astype(F32)
        eua[0:HALO, :] = _sec(hp_ref, 1, d) * _sec(hp_ref, 2, d) * keep
        eub[0:HALO, :] = _sec(hp_ref, 3, d) * _sigmoid(_sec(hp_ref, 4, d)) * keep
        euc[0:HALO, :] = _sec(hp_ref, 5, d) * keep
        eua[HALO:HALO + tm, :] = _sec(pj_ref, 1, d) * _sec(pj_ref, 2, d)
        eub[HALO:HALO + tm, :] = _sec(pj_ref, 3, d) * _sigmoid(_sec(pj_ref, 4, d))
        euc[HALO:HALO + tm, :] = _sec(pj_ref, 5, d)
        _fill_shifts(sh, eub)
        for c in range(tm // CHUNK):
            r0 = c * CHUNK
            z = jnp.zeros((CHUNK, d), F32)
            for k in range(K_A):
                z = z + ca_ref[k:k + 1, :] * eua[HALO + r0 - (K_A - 1) + k:HALO + r0 - (K_A - 1) + k + CHUNK, :]
            pa_ref[r0:r0 + CHUNK, :] = (pj_ref[r0:r0 + CHUNK, 0:d].astype(F32) * z).astype(pa_ref.dtype)
            cv = jnp.zeros((CHUNK, d), F32) + cbb_ref[...]
            for k in range(K_B):
                cv = cv + cb_ref[k:k + 1, :] * _shifted(sh, eub, HALO + r0 - (K_B - 1) + k, CHUNK)
            cv_ref[r0:r0 + CHUNK, :] = cv.astype(cv_ref.dtype)
        cvv = cv_ref[...].astype(F32)
        mu = jnp.mean(cvv, axis=-1, keepdims=True)
        xc = cvv - mu
        xh = xc * lax.rsqrt(jnp.mean(xc * xc, axis=-1, keepdims=True) + EPS)
        ln = xh * lng_ref[...] + lnb_ref[...]
        sw_ref[...] = (ln * _sigmoid(ln)).astype(sw_ref.dtype)
        for gi, w in enumerate(POOL_WINDOWS):
            cols = slice(gi * gc, (gi + 1) * gc)
            tot = euc[HALO:HALO + tm, cols]
            for k in range(1, w):
                tot = tot + euc[HALO - k:HALO - k + tm, cols]
            cnt = _pool_count(i * tm, tm, w)
            pc_ref[:, cols] = (tot / cnt - euc[HALO:HALO + tm, cols]).astype(pc_ref.dtype)
        ya_ref[...] = jnp.dot(pa_ref[...], wa_ref[...].reshape(d, d), preferred_element_type=F32).astype(ya_ref.dtype)
        yb_ref[...] = (jnp.dot(sw_ref[...], wb_ref[...].reshape(d, d), preferred_element_type=F32)
                       + bo_ref[...]).astype(yb_ref.dtype)
        for gi in range(N_GROUPS):
            cols = slice(gi * gc, (gi + 1) * gc)
            pw_ref[:, cols] = jnp.dot(pc_ref[:, cols], _group_weight(wp_ref, gi, gc),
                                      preferred_element_type=F32).astype(pw_ref.dtype)
        m = _sigmoid(_sec(pj_ref, 6, d)) * ya_ref[...].astype(F32)
        m = m + _sigmoid(_sec(pj_ref, 7, d)) * yb_ref[...].astype(F32)
        m = m + _sigmoid(_sec(pj_ref, 8, d)) * (pw_ref[...].astype(F32) * sc_ref[...])
        mg_ref[...] = m.astype(mg_ref.dtype)
        x1_ref[...] = x0_ref[...] + jnp.dot(mg_ref[...], wo_ref[...].reshape(d, d), preferred_element_type=F32)

    row = lambda i: (i, 0)
    fixed = lambda i: (0, 0)
    act = jax.ShapeDtypeStruct((s, d), BF16)

    def dd_weight(off):
        return pl.BlockSpec((N_DEV, rd, d), lambda i: (0, off, 0), pipeline_mode=pl.Buffered(1))

    return pl.pallas_call(
        body, name=name, grid=(n,),
        in_specs=[pl.BlockSpec((tm, 9 * d), row),
                  pl.BlockSpec((HALO, 6 * d), lambda i: (jnp.maximum(i * hb - 1, 0), 0)),
                  pl.BlockSpec((tm, d), row),
                  pl.BlockSpec((K_A, d), fixed), pl.BlockSpec((K_B, d), fixed), pl.BlockSpec((1, d), fixed),
                  pl.BlockSpec((1, d), fixed), pl.BlockSpec((1, d), fixed), pl.BlockSpec((1, d), fixed),
                  pl.BlockSpec((1, d), fixed), dd_weight(o_a), dd_weight(o_b), dd_weight(o_o),
                  pl.BlockSpec((N_DEV, N_GROUPS, gc // N_DEV, gc), lambda i: (0, 0, 0, 0),
                               pipeline_mode=pl.Buffered(1))],
        out_specs=[pl.BlockSpec((tm, d), row)] * 9,
        out_shape=[act] * 8 + [jax.ShapeDtypeStruct((s, d), F32)],
        scratch_shapes=[pltpu.VMEM((tm + HALO, d), F32)] * 3 + [pltpu.VMEM((SUBLANES - 1, tm + HALO, d), F32)],
        compiler_params=_params("parallel"),
    )(proj, proj, x0, conv_a, conv_b, conv_b_bias, ln_g, ln_b, b_out_b, pool_scale, g_row, g_row, g_row, g_pool)


def _merge_bwd(name, dx16, proj, ya, yb, pw, pool_scale, g_row, g_pool, offs, d, tm):
    s = proj.shape[0]
    n = s // tm
    gc = d // N_GROUPS
    rd = d // N_DEV
    o_a, o_b, o_o = offs

    def body(dx_ref, g_ref, ya_ref, yb_ref, pw_ref, sc_ref, wa_ref, wb_ref, wo_ref, wp_ref,
             dya_ref, dyb_ref, dpw_ref, dg_ref, dpa_ref, dsw_ref, dpc_ref, dbo_ref, dsc_ref, acc_b, acc_s):
        i = pl.program_id(0)
        dmv = lax.dot_general(dx_ref[...], wo_ref[...].reshape(d, d), NT,
                              preferred_element_type=F32).astype(BF16).astype(F32)
        scale = sc_ref[...]
        g0 = _sigmoid(_sec(g_ref, 0, d))
        dya_ref[...] = (dmv * g0).astype(dya_ref.dtype)
        dg_ref[:, 0:d] = (dmv * ya_ref[...].astype(F32) * g0 * (1.0 - g0)).astype(dg_ref.dtype)
        g1 = _sigmoid(_sec(g_ref, 1, d))
        dyb = dmv * g1
        dyb_ref[...] = dyb.astype(dyb_ref.dtype)
        dg_ref[:, d:2 * d] = (dmv * yb_ref[...].astype(F32) * g1 * (1.0 - g1)).astype(dg_ref.dtype)
        g2 = _sigmoid(_sec(g_ref, 2, d))
        pwv = pw_ref[...].astype(F32)
        dyc = dmv * g2
        dpw_ref[...] = (dyc * scale).astype(dpw_ref.dtype)
        dg_ref[:, 2 * d:3 * d] = (dmv * (pwv * scale) * g2 * (1.0 - g2)).astype(dg_ref.dtype)
        pb = _colsum8(dyb)
        ps = _colsum8(dyc * pwv)

        @pl.when(i == 0)
        def _():
            acc_b[...] = pb
            acc_s[...] = ps

        @pl.when(i > 0)
        def _():
            acc_b[...] += pb
            acc_s[...] += ps

        dpa_ref[...] = lax.dot_general(dya_ref[...], wa_ref[...].reshape(d, d), NT,
                                       preferred_element_type=F32).astype(dpa_ref.dtype)
        dsw_ref[...] = lax.dot_general(dyb_ref[...], wb_ref[...].reshape(d, d), NT,
                                       preferred_element_type=F32).astype(dsw_ref.dtype)
        for gi in range(N_GROUPS):
            cols = slice(gi * gc, (gi + 1) * gc)
            dpc_ref[:, cols] = lax.dot_general(dpw_ref[:, cols], _group_weight(wp_ref, gi, gc), NT,
                                               preferred_element_type=F32).astype(dpc_ref.dtype)

        @pl.when(i == n - 1)
        def _():
            dbo_ref[...] = jnp.sum(acc_b[...], axis=0, keepdims=True)
            dsc_ref[...] = jnp.sum(acc_s[...], axis=0, keepdims=True)

    row = lambda i: (i, 0)
    fixed = lambda i: (0, 0)
    act = jax.ShapeDtypeStruct((s, d), BF16)
    vec = jax.ShapeDtypeStruct((1, d), F32)

    def dd_weight(off):
        return pl.BlockSpec((N_DEV, rd, d), lambda i: (0, off, 0), pipeline_mode=pl.Buffered(1))

    return pl.pallas_call(
        body, name=name, grid=(n,),
        in_specs=[pl.BlockSpec((tm, d), row), pl.BlockSpec((tm, 3 * d), lambda i: (i, 2)), pl.BlockSpec((tm, d), row),
                  pl.BlockSpec((tm, d), row), pl.BlockSpec((tm, d), row), pl.BlockSpec((1, d), fixed),
                  dd_weight(o_a), dd_weight(o_b), dd_weight(o_o),
                  pl.BlockSpec((N_DEV, N_GROUPS, gc // N_DEV, gc), lambda i: (0, 0, 0, 0),
                               pipeline_mode=pl.Buffered(1))],
        out_specs=[pl.BlockSpec((tm, d), row)] * 3 + [pl.BlockSpec((tm, 3 * d), row)] + [pl.BlockSpec((tm, d), row)] * 3
                  + [pl.BlockSpec((1, d), fixed)] * 2,
        out_shape=[act, act, act, jax.ShapeDtypeStruct((s, 3 * d), BF16), act, act, act, vec, vec],
        scratch_shapes=[pltpu.VMEM((8, d), F32)] * 2, compiler_params=_params("arbitrary"),
    )(dx16, proj, ya, yb, pw, pool_scale, g_row, g_row, g_row, g_pool)


def _mix_pre_bwd(name, proj, cv, dpa, dsw, dpc, dgates, conv_a, conv_b, ln_g, ln_b, d, tm):
    s = proj.shape[0]
    n = s // tm
    gc = d // N_GROUPS
    hb = tm // HALO
    last_halo = s // HALO - 1
    te = tm + HALO

    def ln_bwd(cvv, dswv, lng, lnb):
        mu = jnp.mean(cvv, axis=-1, keepdims=True)
        xc = cvv - mu
        rstd = lax.rsqrt(jnp.mean(xc * xc, axis=-1, keepdims=True) + EPS)
        xh = xc * rstd
        ln = xh * lng + lnb
        sg = _sigmoid(ln)
        dln = dswv * (sg * (1.0 + ln * (1.0 - sg)))
        dxh = dln * lng
        dcv = rstd * (dxh - jnp.mean(dxh, axis=-1, keepdims=True) - xh * jnp.mean(dxh * xh, axis=-1, keepdims=True))
        return dcv, dln, xh

    def body(pj_ref, hp_ref, hf_ref, cv_ref, cvf_ref, dpa_ref, dpaf_ref, dsw_ref, dswf_ref, dpc_ref, dpcf_ref, dgt_ref,
             ca_ref, cb_ref, lng_ref, lnb_ref,
             dpj_ref, dbin_ref, dca_ref, dcb_ref, dcbb_ref, dlng_ref, dlnb_ref,
             eua, eub, edz, edcv, eq, sh, dub_s, acc_bin, acc_ca, acc_cb, acc_v):
        i = pl.program_id(0)
        keep_p = (i > 0).astype(F32)
        keep_f = (i < n - 1).astype(F32)

        @pl.when(i == 0)
        def _():
            acc_bin[...] = jnp.zeros_like(acc_bin)
            acc_ca[...] = jnp.zeros_like(acc_ca)
            acc_cb[...] = jnp.zeros_like(acc_cb)
            acc_v[...] = jnp.zeros_like(acc_v)

        eua[0:HALO, :] = _sec(hp_ref, 1, d) * _sec(hp_ref, 2, d) * keep_p
        eua[HALO:te, :] = _sec(pj_ref, 1, d) * _sec(pj_ref, 2, d)
        eub[HALO:te, :] = _sec(pj_ref, 3, d) * _sigmoid(_sec(pj_ref, 4, d))
        edz[0:tm, :] = dpa_ref[...].astype(F32) * _sec(pj_ref, 0, d)
        edz[tm:te, :] = dpaf_ref[...].astype(F32) * _sec(hf_ref, 0, d) * keep_f
        dcv, dln, xh = ln_bwd(cv_ref[...].astype(F32), dsw_ref[...].astype(F32), lng_ref[...], lnb_ref[...])
        edcv[0:tm, :] = dcv
        acc_v[0:8, :] += _colsum8(dcv)
        acc_v[8:16, :] += _colsum8(dln * xh)
        acc_v[16:24, :] += _colsum8(dln)
        dcvf, _, _ = ln_bwd(cvf_ref[...].astype(F32), dswf_ref[...].astype(F32), lng_ref[...], lnb_ref[...])
        edcv[tm:te, :] = dcvf * keep_f
        for gi, w in enumerate(POOL_WINDOWS):
            cols = slice(gi * gc, (gi + 1) * gc)
            eq[0:tm, cols] = dpc_ref[:, cols].astype(F32) / _pool_count(i * tm, tm, w)
            eq[tm:te, cols] = dpcf_ref[:, cols].astype(F32) / _pool_count((i + 1) * tm, HALO, w) * keep_f

        def put(sec_idx, r0, val):
            dpj_ref[r0:r0 + CHUNK, sec_idx * d:(sec_idx + 1) * d] = val.astype(dpj_ref.dtype)
            acc_bin[:, sec_idx * d:(sec_idx + 1) * d] += _colsum8(val)

        _fill_shifts(sh, edcv)
        for k0 in range(0, K_B, TAP_GROUP):
            taps = range(k0, min(k0 + TAP_GROUP, K_B))
            a = {k: jnp.zeros((8, d), F32) for k in taps}
            for c in range(tm // CHUNK):
                r0 = c * CHUNK
                ub = eub[HALO + r0:HALO + r0 + CHUNK, :]
                part = None
                for k in taps:
                    t = _shifted(sh, edcv, r0 + (K_B - 1) - k, CHUNK)
                    part = cb_ref[k:k + 1, :] * t if part is None else part + cb_ref[k:k + 1, :] * t
                    a[k] = a[k] + _colsum8(ub * t)
                if k0 == 0:
                    dub_s[r0:r0 + CHUNK, :] = part
                else:
                    dub_s[r0:r0 + CHUNK, :] += part
            for k in taps:
                acc_cb[k] += a[k]
        wa = [jnp.zeros((8, d), F32) for _ in range(K_A)]
        for c in range(tm // CHUNK):
            r0 = c * CHUNK
            rows = slice(r0, r0 + CHUNK)
            z = jnp.zeros((CHUNK, d), F32)
            dua = jnp.zeros((CHUNK, d), F32)
            ua = eua[HALO + r0:HALO + r0 + CHUNK, :]
            for k in range(K_A):
                z = z + ca_ref[k:k + 1, :] * eua[HALO + r0 - (K_A - 1) + k:HALO + r0 - (K_A - 1) + k + CHUNK, :]
                t = edz[r0 + (K_A - 1) - k:r0 + (K_A - 1) - k + CHUNK, :]
                dua = dua + ca_ref[k:k + 1, :] * t
                wa[k] = wa[k] + _colsum8(ua * t)
            put(0, r0, dpa_ref[rows, :].astype(F32) * z)
            put(1, r0, dua * pj_ref[rows, 2 * d:3 * d].astype(F32))
            put(2, r0, dua * pj_ref[rows, d:2 * d].astype(F32))
            dub = dub_s[rows, :]
            bval = pj_ref[rows, 3 * d:4 * d].astype(F32)
            sg = _sigmoid(pj_ref[rows, 4 * d:5 * d].astype(F32))
            put(3, r0, dub * sg)
            put(4, r0, dub * bval * sg * (1.0 - sg))
            for gi, w in enumerate(POOL_WINDOWS):
                cols = slice(gi * gc, (gi + 1) * gc)
                tot = eq[rows, cols]
                for k in range(1, w):
                    tot = tot + eq[r0 + k:r0 + k + CHUNK, cols]
                dci = tot - dpc_ref[rows, cols].astype(F32)
                dpj_ref[rows, 5 * d + gi * gc:5 * d + (gi + 1) * gc] = dci.astype(dpj_ref.dtype)
                acc_bin[:, 5 * d + gi * gc:5 * d + (gi + 1) * gc] += _colsum8(dci)
        for q in range(3):
            gv = dgt_ref[:, q * d:(q + 1) * d]
            dpj_ref[:, (6 + q) * d:(7 + q) * d] = gv
            acc_bin[:, (6 + q) * d:(7 + q) * d] += _colsum8(gv.astype(F32))
        for k in range(K_A):
            acc_ca[k] += wa[k]

        @pl.when(i == n - 1)
        def _():
            dbin_ref[...] = jnp.sum(acc_bin[...], axis=0, keepdims=True)
            for k in range(K_A):
                dca_ref[k:k + 1, :] = jnp.sum(acc_ca[k], axis=0, keepdims=True)
            for k in range(K_B):
                dcb_ref[k:k + 1, :] = jnp.sum(acc_cb[k], axis=0, keepdims=True)
            dcbb_ref[...] = jnp.sum(acc_v[0:8, :], axis=0, keepdims=True)
            dlng_ref[...] = jnp.sum(acc_v[8:16, :], axis=0, keepdims=True)
            dlnb_ref[...] = jnp.sum(acc_v[16:24, :], axis=0, keepdims=True)

    row = lambda i: (i, 0)
    fixed = lambda i: (0, 0)
    past = lambda i: (jnp.maximum(i * hb - 1, 0), 0)
    fut = lambda i: (jnp.minimum((i + 1) * hb, last_halo), 0)
    vec = jax.ShapeDtypeStruct((1, d), F32)
    tile_and_halo = [pl.BlockSpec((tm, d), row), pl.BlockSpec((HALO, d), fut)]
    return pl.pallas_call(
        body, name=name, grid=(n,),
        in_specs=[pl.BlockSpec((tm, 6 * d), row), pl.BlockSpec((HALO, 6 * d), past), pl.BlockSpec((HALO, 6 * d), fut),
                  *tile_and_halo, *tile_and_halo, *tile_and_halo, *tile_and_halo,
                  pl.BlockSpec((tm, 3 * d), row),
                  pl.BlockSpec((K_A, d), fixed), pl.BlockSpec((K_B, d), fixed), pl.BlockSpec((1, d), fixed),
                  pl.BlockSpec((1, d), fixed)],
        out_specs=[pl.BlockSpec((tm, 9 * d), row), pl.BlockSpec((1, 9 * d), fixed), pl.BlockSpec((K_A, d), fixed),
                   pl.BlockSpec((K_B, d), fixed), pl.BlockSpec((1, d), fixed), pl.BlockSpec((1, d), fixed),
                   pl.BlockSpec((1, d), fixed)],
        out_shape=[jax.ShapeDtypeStruct((s, 9 * d), BF16), jax.ShapeDtypeStruct((1, 9 * d), F32),
                   jax.ShapeDtypeStruct((K_A, d), F32), jax.ShapeDtypeStruct((K_B, d), F32), vec, vec, vec],
        scratch_shapes=[pltpu.VMEM((te, d), F32)] * 5 + [pltpu.VMEM((SUBLANES - 1, te, d), F32),
                                                         pltpu.VMEM((tm, d), F32),
                                                         pltpu.VMEM((8, 9 * d), F32), pltpu.VMEM((K_A, 8, d), F32),
                                                         pltpu.VMEM((K_B, 8, d), F32), pltpu.VMEM((24, d), F32)],
        compiler_params=_params("arbitrary"),
    )(proj, proj, proj, cv, cv, dpa, dpa, dsw, dsw, dpc, dpc, dgates, conv_a, conv_b, ln_g, ln_b)


def _pool_wgrad(name, p, dpw, d, tk):
    s = p.shape[0]
    gc = d // N_GROUPS
    n = s // tk

    def body(p_ref, g_ref, o_ref, acc):
        k = pl.program_id(0)
        for gi in range(N_GROUPS):
            cols = slice(gi * gc, (gi + 1) * gc)
            part = lax.dot_general(p_ref[:, cols], g_ref[:, cols], TN, preferred_element_type=F32)

            @pl.when(k == 0)
            def _():
                acc[gi] = part

            @pl.when(k > 0)
            def _():
                acc[gi] += part

        @pl.when(k == n - 1)
        def _():
            for gi in range(N_GROUPS):
                o_ref[:, gi] = acc[gi].astype(o_ref.dtype).reshape(N_DEV, gc // N_DEV, gc)

    return pl.pallas_call(
        body, name=name, grid=(n,),
        in_specs=[pl.BlockSpec((tk, d), lambda k: (k, 0)), pl.BlockSpec((tk, d), lambda k: (k, 0))],
        out_specs=pl.BlockSpec((N_DEV, N_GROUPS, gc // N_DEV, gc), lambda k: (0, 0, 0, 0)),
        out_shape=jax.ShapeDtypeStruct((N_DEV, N_GROUPS, gc // N_DEV, gc), BF16),
        scratch_shapes=[pltpu.VMEM((N_GROUPS, gc, gc), F32)], compiler_params=_params("arbitrary"),
    )(p, dpw)


def _my_place():
    x, y, c = lax.axis_index("x"), lax.axis_index("y"), lax.axis_index("c")
    return x, y, c


def _block_of(x, y, c):
    return 4 * x + 2 * y + c


def _slot(ref, k, paired):
    if not paired:
        return ref.at[k]
    cols = ref.shape[-1] // 2
    return ref.at[k // 2, :, pl.ds(pl.multiple_of((k % 2) * cols, 128), cols)]


def _slot_shape(shape, paired):
    return (N_DEV // 2, shape[0], 2 * shape[1]) if paired else (N_DEV, *shape)


def _gather_shards(shards, paired):
    n_arr = len(shards)

    def body(*refs):
        srcs = refs[:n_arr]
        outs = refs[n_arr:2 * n_arr]
        send_sems, recv_sems, local_sems = refs[2 * n_arr:]
        x, y, c = _my_place()
        me, sibling = (x, y, c), (x, y, 1 - c)
        chips = [(1 - x, y), (x, 1 - y), (1 - x, 1 - y)]

        def copy(n, k, block, to, src=None):
            rows = _slot(outs[n], _block_of(*block), paired[n])
            return pltpu.make_async_remote_copy(
                src_ref=rows if src is None else src, dst_ref=rows, send_sem=send_sems.at[n, k],
                recv_sem=recv_sems.at[n, k], device_id=to, device_id_type=MESH)

        mine = [pltpu.make_async_copy(srcs[n], _slot(outs[n], _block_of(*me), paired[n]), local_sems.at[n])
                for n in range(n_arr)]
        for cp in mine:
            cp.start()
        first = []
        for n in range(n_arr):
            first.append(copy(n, 0, me, sibling, src=srcs[n]))
            first += [copy(n, 1 + j, me, (*chip, c), src=srcs[n]) for j, chip in enumerate(chips)]
        for cp in first:
            cp.start()
        passed = []
        for n in range(n_arr):
            for j, chip in enumerate(chips):
                copy(n, 1 + j, (*chip, c), me).wait_recv()
                fwd = copy(n, 4 + j, (*chip, c), sibling)
                fwd.start()
                passed.append(fwd)
        for n in range(n_arr):
            copy(n, 0, sibling, me).wait_recv()
            for j, chip in enumerate(chips):
                copy(n, 4 + j, (*chip, 1 - c), me).wait_recv()
        for cp in first + passed:
            cp.wait_send()
        for cp in mine:
            cp.wait()

    any_spec = pl.BlockSpec(memory_space=pl.ANY)
    return pl.pallas_call(
        body, name="gather_weights",
        in_specs=[any_spec] * n_arr, out_specs=[any_spec] * n_arr,
        out_shape=[jax.ShapeDtypeStruct(_slot_shape(sh.shape, p), sh.dtype) for sh, p in zip(shards, paired)],
        scratch_shapes=[pltpu.SemaphoreType.DMA((n_arr, 7)), pltpu.SemaphoreType.DMA((n_arr, 7)),
                        pltpu.SemaphoreType.DMA((n_arr,))],
    )(*shards)


def _peers(x, y, c):
    out = []
    for r in range(1, N_DEV):
        fx, fy, fc = (r >> 2) & 1, (r >> 1) & 1, r & 1
        out.append(((1 - x) if fx else x, (1 - y) if fy else y, (1 - c) if fc else c))
    return out


HBM_SPEC = pl.BlockSpec(memory_space=pltpu.HBM)
SEM_SPEC = pl.BlockSpec(memory_space=pltpu.SEMAPHORE)
ANY_SPEC = pl.BlockSpec(memory_space=pl.ANY)
N_PEERS = N_DEV - 1


def _peer_copy(src_ref, land_ref, send_sems, recv_sems, i, r, peer, me, blockwise, paired):
    src = _slot(src_ref, _block_of(*peer), paired) if blockwise else src_ref
    dst = land_ref.at[me] if blockwise else _slot(land_ref, me, paired)
    return pltpu.make_async_remote_copy(
        src_ref=src, dst_ref=dst, send_sem=send_sems.at[i * N_PEERS + r],
        recv_sem=recv_sems.at[i * N_PEERS + r], device_id=peer, device_id_type=MESH)


def _block_shape(shape, paired):
    return (shape[1], shape[2] // 2) if paired else tuple(shape[1:])


def _start_copies(name, srcs, after, blockwise, paired=None):
    n = len(srcs)
    paired = paired or [False] * n

    def body(*refs):
        s_in, l_in = refs[:n], refs[n:2 * n]
        send_sems, recv_sems = refs[2 * n + 1], refs[2 * n + 2]
        token = refs[-1]
        x, y, c = _my_place()
        me = _block_of(x, y, c)
        for i in range(n):
            for r, peer in enumerate(_peers(x, y, c)):
                _peer_copy(s_in[i], l_in[i], send_sems, recv_sems, i, r, peer, me, blockwise, paired[i]).start()
        token[...] = jnp.zeros_like(token)

    land_shapes = [(N_DEV, *_block_shape(s.shape, p)) if blockwise else _slot_shape(s.shape, p)
                   for s, p in zip(srcs, paired)]
    lands = [pltpu.with_memory_space_constraint(lax.empty(sh, s.dtype), pltpu.HBM) for sh, s in zip(land_shapes, srcs)]
    ins = [pltpu.with_memory_space_constraint(s, pltpu.HBM) for s in srcs]
    out = pl.pallas_call(
        body, name=name,
        out_shape=(pltpu.SemaphoreType.DMA((n * N_PEERS,)), pltpu.SemaphoreType.DMA((n * N_PEERS,)),
                   *[pltpu.HBM(s.shape, s.dtype) for s in srcs],
                   *[pltpu.HBM(sh, s.dtype) for sh, s in zip(land_shapes, srcs)],
                   jax.ShapeDtypeStruct((8, 128), F32)),
        in_specs=[HBM_SPEC] * (2 * n) + [ANY_SPEC],
        out_specs=(SEM_SPEC, SEM_SPEC, *[HBM_SPEC] * (2 * n), pl.BlockSpec(memory_space=pltpu.VMEM)),
        input_output_aliases={i: 2 + i for i in range(2 * n)},
        compiler_params=pltpu.CompilerParams(has_side_effects=pltpu.SideEffectType.DATAFLOW_SIDE_EFFECTING),
    )(*ins, *lands, after)
    return dict(send=out[0], recv=out[1], srcs=list(out[2:2 + n]), lands=list(out[2 + n:2 + 2 * n]), token=out[-1],
                paired=paired)


def _wait_copies(name, state, after, blockwise):
    n = len(state["srcs"])
    paired = state["paired"]

    def body(*refs):
        s_in, l_in = refs[:n], refs[n:2 * n]
        send_sems, recv_sems = refs[2 * n], refs[2 * n + 1]
        x, y, c = _my_place()
        me = _block_of(x, y, c)
        for i in range(n):
            for r, peer in enumerate(_peers(x, y, c)):
                cp = _peer_copy(s_in[i], l_in[i], send_sems, recv_sems, i, r, peer, me, blockwise, paired[i])
                cp.wait_send()
                cp.wait_recv()

    both = state["srcs"] + state["lands"]
    out = pl.pallas_call(
        body, name=name, out_shape=tuple(pltpu.HBM(a.shape, a.dtype) for a in both),
        in_specs=[HBM_SPEC] * (2 * n) + [SEM_SPEC, SEM_SPEC, ANY_SPEC], out_specs=tuple([HBM_SPEC] * (2 * n)),
        input_output_aliases={i: i for i in range(2 * n)},
        compiler_params=pltpu.CompilerParams(has_side_effects=pltpu.SideEffectType.DATAFLOW_SIDE_EFFECTING),
    )(*both, state["send"], state["recv"], after)
    return list(out[:n]), list(out[n:])


COPY_BLOCK_BYTES = 2 * 1024 * 1024


def _place_own(name, lands, srcs, me, blockwise, paired=None):
    out = []
    paired = paired or [False] * len(lands)
    for i, (land, src) in enumerate(zip(lands, srcs)):
        in_slots = src if blockwise else land
        part = _block_shape(in_slots.shape, paired[i])
        row_bytes = land.dtype.itemsize
        for extent in part[1:]:
            row_bytes *= extent
        tr = part[0]
        while tr * row_bytes > COPY_BLOCK_BYTES and tr % 16 == 0:
            tr //= 2
        tail = (0,) * (len(part) - 1)

        def body(me_ref, s_ref, l_ref, o_ref):
            o_ref[...] = s_ref[...]

        if paired[i]:
            slot_spec = pl.BlockSpec((None, tr, part[1]), lambda j, me_ref: (me_ref[0] // 2, j, me_ref[0] % 2))
        else:
            slot_spec = pl.BlockSpec((None, tr, *part[1:]), lambda j, me_ref: (me_ref[0], j, *tail))
        if blockwise:
            s_spec = slot_spec
            o_spec = pl.BlockSpec((None, tr, *part[1:]), lambda j, me_ref: (me_ref[0], j, *tail))
        else:
            s_spec = pl.BlockSpec((tr, *part[1:]), lambda j, me_ref: (j, *tail))
            o_spec = slot_spec
        out.append(pl.pallas_call(
            body, name=f"{name}_{i}",
            grid_spec=pltpu.PrefetchScalarGridSpec(
                num_scalar_prefetch=1, grid=(part[0] // tr,), in_specs=[s_spec, ANY_SPEC], out_specs=o_spec),
            out_shape=jax.ShapeDtypeStruct(land.shape, land.dtype), input_output_aliases={2: 0},
            compiler_params=_params("parallel"),
        )(me, src, land))
    return out


def _adamw_math(w, g, m, v):
    m = ADAM_B1 * m + (1.0 - ADAM_B1) * g
    v = ADAM_B2 * v + (1.0 - ADAM_B2) * (g * g)
    m_hat = m / (1.0 - ADAM_B1 ** ADAM_STEP)
    v_hat = v / (1.0 - ADAM_B2 ** ADAM_STEP)
    delta = -ADAM_LR * (m_hat / (jnp.sqrt(v_hat) + ADAM_EPS) + ADAM_WD * w)
    return delta, m, v


def _adamw(name, parts, w, m, v, *, grid, part_specs, w_spec):
    n_layers = len(parts)
    n_parts = parts[0].shape[0]

    def body(*refs):
        p_refs = refs[:n_layers]
        w_ref, m_ref, v_ref, g_ref, d_ref, nm_ref, nv_ref = refs[n_layers:]

        def total(p_ref):
            t = p_ref[0].astype(F32)
            for k in range(1, n_parts):
                t = t + p_ref[k].astype(F32)
            return t

        g = total(p_refs[0])
        for li in range(1, n_layers):
            g = jnp.where(pl.program_id(0) == li, total(p_refs[li]), g)
        delta, nm, nv = _adamw_math(w_ref[...], g, m_ref[...], v_ref[...])
        g_ref[...] = g
        d_ref[...] = delta
        nm_ref[...] = nm
        nv_ref[...] = nv

    out = jax.ShapeDtypeStruct(w.shape, F32)
    return pl.pallas_call(
        body, name=name, grid=grid, in_specs=[*part_specs, w_spec, w_spec, w_spec], out_specs=[w_spec] * 4,
        out_shape=[out] * 4, compiler_params=_params(*(("parallel",) * len(grid))),
    )(*parts, w, m, v)


def _layer_part_spec(layer, block, n_blocks, row_off=0):
    def index_map(l, i):
        ii = jnp.where(l == layer, i, jnp.where(l < layer, 0, n_blocks - 1))
        return (0, row_off + ii) + (0,) * (len(block) - 2)
    return pl.BlockSpec(block, index_map)


def _small_update(partials, triples, conv_rows):
    d = partials[-1].shape[-1]
    n_rep = len(triples)
    n_part = len(partials)
    rows = []
    for p in partials:
        rows.append(p.shape[0] * (p.shape[1] // d))
    offs = [sum(rows[:i]) for i in range(n_part)]
    total = -(-sum(rows) // 8) * 8

    def body(*refs):
        p_refs = refs[:n_part]
        wmv = refs[n_part:n_part + 3 * n_rep]
        outs = refs[n_part + 3 * n_rep:n_part + 3 * n_rep + 4 * n_rep + (n_part - n_rep)]
        buf, send_sems, recv_sems = refs[-3:]
        x, y, c = _my_place()
        me = _block_of(x, y, c)
        peers = _peers(x, y, c)
        mine = buf.at[me]
        if total > sum(rows):
            mine[sum(rows):total, :] = jnp.zeros((total - sum(rows), d), F32)
        for p_ref, off in zip(p_refs, offs):
            nr, nc = p_ref.shape[0], p_ref.shape[1] // d
            if nc == 1:
                mine[off:off + nr, :] = p_ref[...]
            else:
                for r in range(nr):
                    for q in range(nc):
                        mine[off + r * nc + q:off + r * nc + q + 1, :] = p_ref[r:r + 1, q * d:(q + 1) * d]
        sends =[pltpu.make_async_remote_copy(
            src_ref=buf.at[me], dst_ref=buf.at[me], send_sem=send_sems.at[r], recv_sem=recv_sems.at[r],
            device_id=peer, device_id_type=MESH) for r, peer in enumerate(peers)]
        for cp in sends:
            cp.start()
        for r, peer in enumerate(peers):
            pltpu.make_async_remote_copy(
                src_ref=buf.at[me], dst_ref=buf.at[_block_of(*peer)], send_sem=send_sems.at[r],
                recv_sem=recv_sems.at[r], device_id=peer, device_id_type=MESH).wait_recv()
        for cp in sends:
            cp.wait_send()
        tot = buf[0]
        for k in range(1, N_DEV):
            tot = tot + buf[k]
        buf[0] = tot
        for idx in range(n_part):
            nr, nc = p_refs[idx].shape[0], p_refs[idx].shape[1] // d
            if idx < n_rep:
                w_ref, m_ref, v_ref = wmv[3 * idx:3 * idx + 3]
                g_ref, d_ref, nm_ref, nv_ref = outs[4 * idx:4 * idx + 4]
            else:
                g_ref = outs[4 * n_rep + idx - n_rep]
            pieces = [(slice(0, nr), slice(0, d), offs[idx], nr)] if nc == 1 else [
                (slice(r, r + 1), slice(q * d, (q + 1) * d), offs[idx] + r * nc + q, 1)
                for r in range(nr) for q in range(nc)]
            for rws, cols, row, cnt in pieces:
                g = buf[0, row:row + cnt, :]
                g_ref[rws, cols] = g
                if idx < n_rep:
                    delta, nm, nv = _adamw_math(w_ref[rws, cols], g, m_ref[rws, cols], v_ref[rws, cols])
                    d_ref[rws, cols] = delta
                    nm_ref[rws, cols] = nm
                    nv_ref[rws, cols] = nv

    vm = pl.BlockSpec(memory_space=pltpu.VMEM)
    operands = list(partials)
    for t in triples:
        operands += list(t)
    out_shape = []
    for idx in range(n_rep):
        out_shape += [jax.ShapeDtypeStruct(partials[idx].shape, F32)] * 4
    for idx in range(n_rep, n_part):
        out_shape.append(jax.ShapeDtypeStruct(partials[idx].shape, F32))
    return pl.pallas_call(
        body, name="small_allreduce_adamw", in_specs=[vm] * len(operands), out_specs=[vm] * len(out_shape),
        out_shape=out_shape,
        scratch_shapes=[pltpu.VMEM((N_DEV, total, d), F32), pltpu.SemaphoreType.DMA((7,)), pltpu.SemaphoreType.DMA((7,))],
        compiler_params=pltpu.CompilerParams(vmem_limit_bytes=VMEM_LIMIT_BYTES),
    )(*operands)


def kernel(x, g_mix, w_in, b_in, conv_a, w_out_a, conv_b, conv_b_bias, ln_b_g, ln_b_b, w_out_b, b_out_b, w_pool, pool_scale, w_o, g_mlp, w_mlp1, w_mlp2, g_final, loss_target, m_g_mix, m_w_in, m_b_in, m_conv_a, m_w_out_a, m_conv_b, m_conv_b_bias, m_ln_b_g, m_ln_b_b, m_w_out_b, m_b_out_b, m_w_pool, m_pool_scale, m_w_o, m_g_mlp, m_w_mlp1, m_w_mlp2, m_g_final, v_g_mix, v_w_in, v_b_in, v_conv_a, v_w_out_a, v_conv_b, v_conv_b_bias, v_ln_b_g, v_ln_b_b, v_w_out_b, v_b_out_b, v_w_pool, v_pool_scale, v_w_o, v_g_mlp, v_w_mlp1, v_w_mlp2, v_g_final):
    _, s, d = x.shape
    n_layers = g_mix.shape[0]
    p_in = b_in.shape[1]
    ci = w_in.shape[2]
    c1 = w_mlp1.shape[2]
    rf = w_mlp2.shape[1]
    rd = w_out_a.shape[1]
    f = rf * N_DEV
    rp = rf + 3 * rd
    o_a, o_b, o_o = rf // rd, rf // rd + 1, rf // rd + 2
    gc = d // N_GROUPS
    ca_rows = 8
    tm = min(1024, s)
    tr = min(512, s)
    tx = min(256, s)
    tk = min(2048, s)
    tk_mlp = min(4096, s)
    tk_in = min(2048, s)

    me_arr = jnp.reshape(_block_of(*_my_place()), (1,)).astype(jnp.int32)

    def layer_shards(l):
        row_pack = jnp.concatenate([w_mlp2[l], w_out_a[l], w_out_b[l], w_o[l]], axis=0).astype(BF16)
        return [w_in[l].astype(BF16), w_mlp1[l].astype(BF16), row_pack, w_pool[l].astype(BF16)]

    conv_pack = jnp.concatenate(
        [conv_a, jnp.zeros((n_layers, ca_rows - K_A, rd), F32), conv_b], axis=1)
    first_shards = layer_shards(0)
    layer_pairing = [True, False, False, False]
    g_in_first, g_conv = _gather_shards([first_shards[0], conv_pack], [True, False])
    conv_full = jnp.transpose(g_conv, (1, 2, 0, 3)).reshape(n_layers, ca_rows + K_B, d)
    conv_a_f = conv_full[:, :K_A]
    conv_b_f = conv_full[:, ca_rows:]
    first_row_going = _start_copies("gather_start_row_0", first_shards[2:], g_conv, blockwise=False)
    in_flight = [_start_copies("gather_start_mlp1_0", first_shards[1:2], first_row_going["token"], blockwise=False)]
    for l in range(1, n_layers):
        in_flight.append(_start_copies(f"gather_start_{l}", layer_shards(l), in_flight[-1]["token"], blockwise=False,
                                       paired=layer_pairing))
    token = in_flight[-1]["token"][0:1, 0:1]

    xs = [x[0]]
    saved = []
    weights = []
    row2 = lambda j, i: (i, 0)
    for l in range(n_layers):
        x0 = xs[-1]
        vec = lambda a: a[l:l + 1]
        if l > 0:
            srcs, lands = _wait_copies(f"gather_wait_{l}", in_flight[l], x0, blockwise=False)
            g_in, g_1, g_row, g_pool = _place_own(f"gather_own_{l}", lands, srcs, me_arr, blockwise=False,
                                                  paired=layer_pairing)
        else:
            g_in = g_in_first
        h = _rms_fwd(f"rms_mix_{l}", x0, vec(g_mix) + token if l == 0 else vec(g_mix), tr)
        proj = _mm(
            f"proj_{l}", h, g_in, grid=(N_DEV // 2, s // tm), a_spec=pl.BlockSpec((tm, d), row2),
            b_spec=pl.BlockSpec((None, d, 2 * ci), lambda j, i: (j, 0, 0)),
            extras=(vec(b_in),), extra_specs=(pl.BlockSpec((1, 2 * ci), lambda j, i: (0, j)),),
            epilogue=lambda v, b: v + b, out_shape=jax.ShapeDtypeStruct((s, p_in), BF16),
            o_spec=pl.BlockSpec((tm, 2 * ci), lambda j, i: (i, j)), dims=NN)
        if l == 0:
            srcs, lands = _wait_copies("gather_wait_row_0", first_row_going, proj, blockwise=False)
            g_row, g_pool = _place_own("gather_own_row_0", lands, srcs, me_arr, blockwise=False)
        p_a, sw, p_c, cv, y_a, y_b, pw, merged, x1 = _mixer_fwd(
            f"mix_fwd_{l}", proj, x0, conv_a_f[l], conv_b_f[l], vec(conv_b_bias), vec(ln_b_g), vec(ln_b_b),
            vec(b_out_b), vec(pool_scale), g_row, g_pool, (o_a, o_b, o_o), d, tx)
        if l == 0:
            srcs, lands = _wait_copies("gather_wait_mlp1_0", in_flight[0], x1, blockwise=False)
            g_1, = _place_own("gather_own_mlp1_0", lands, srcs, me_arr, blockwise=False)
        weights.append((g_in, g_1, g_row, g_pool))
        h2 = _rms_fwd(f"rms_mlp_{l}", x1, vec(g_mlp), tr)
        a_pre = _mm(f"mlp1_{l}", h2, g_1, grid=(N_DEV // 2, s // tm), a_spec=pl.BlockSpec((tm, d), row2),
                    b_spec=pl.BlockSpec((2, d, c1), lambda j, i: (j, 0, 0)), slabs="n",
                    out_shape=jax.ShapeDtypeStruct((s, f), BF16),
                    o_spec=pl.BlockSpec((tm, 2 * c1), lambda j, i: (i, j)), dims=NN)
        x2 = _mm(f"mlp2_{l}", a_pre, g_row, grid=(1, s // tr), a_spec=pl.BlockSpec((tr, f), row2),
                 b_spec=pl.BlockSpec((N_DEV, rf, d), lambda j, i: (0, 0, 0)), prologue=_relu_sq,
                 extras=(x1,), extra_specs=(pl.BlockSpec((tr, d), row2),), epilogue=lambda v, r: v + r,
                 out_shape=jax.ShapeDtypeStruct((s, d), F32), o_spec=pl.BlockSpec((tr, d), row2), dims=NN)
        saved.append((x0, h, proj, p_a, sw, p_c, cv, y_a, y_b, pw, merged, x1, h2, a_pre))
        xs.append(x2)

    loss_part, dx, dx16, dg_final = _loss_head(xs[-1], g_final.reshape(1, d), loss_target[0], tr)
    loss = lax.psum(loss_part[0, 0], ("x", "y", "c"))

    small = [None] * n_layers
    exchanges = [None] * n_layers
    for l in reversed(range(n_layers)):
        x0, h, proj, p_a, sw, p_c, cv, y_a, y_b, pw, merged, x1, h2, a_pre = saved[l]
        g_in, g_1, g_row, g_pool = weights[l]
        vec = lambda a: a[l:l + 1]
        row_shape = jax.ShapeDtypeStruct((N_DEV, rp, d), BF16)

        def dd_grad(name, a, g, off, alias):
            return _mm(name, a, g, grid=(1, s // tk), a_spec=pl.BlockSpec((tk, d), lambda j, k: (k, 0)),
                       b_spec=pl.BlockSpec((tk, d), lambda j, k: (k, 0)), out_shape=row_shape,
                       o_spec=pl.BlockSpec((N_DEV, rd, d), lambda j, k: (0, off, 0)), dims=TN, nk=s // tk,
                       acc_shape=(d, d), alias_in=alias)

        d_a = _mm(f"d_act_{l}", dx16, g_row, grid=(N_DEV // 2, s // tm), a_spec=pl.BlockSpec((tm, d), row2),
                  b_spec=pl.BlockSpec((2, rf, d), lambda j, i: (j, 0, 0)), slabs="n",
                  extras=(a_pre,), extra_specs=(pl.BlockSpec((tm, 2 * rf), lambda j, i: (i, j)),),
                  epilogue=lambda v, a: v * (2.0 * jnp.maximum(a.astype(F32), 0.0)),
                  out_shape=jax.ShapeDtypeStruct((s, f), BF16),
                  o_spec=pl.BlockSpec((tm, 2 * rf), lambda j, i: (i, j)), dims=NT)
        dg_row = _mm(f"dw_mlp2_{l}", a_pre, dx16, grid=(N_DEV, s // tk_mlp),
                     a_spec=pl.BlockSpec((tk_mlp, rf), lambda j, k: (k, j)),
                     b_spec=pl.BlockSpec((tk_mlp, d), lambda j, k: (k, 0)), prologue=_relu_sq, out_shape=row_shape,
                     o_spec=pl.BlockSpec((None, rf, d), lambda j, k: (j, 0, 0)), dims=TN, nk=s // tk_mlp,
                     acc_shape=(rf, d))
        d_h2 = _mm(f"d_h2_{l}", d_a, g_1, grid=(1, s // tr), a_spec=pl.BlockSpec((tr, f), row2),
                   b_spec=pl.BlockSpec((N_DEV, d, c1), lambda j, i: (0, 0, 0)), slabs="k",
                   out_shape=jax.ShapeDtypeStruct((s, d), BF16), o_spec=pl.BlockSpec((tr, d), row2), dims=NT)
        dg_1 = _mm(f"dw_mlp1_{l}", h2, d_a, grid=(N_DEV, s // tk_mlp),
                   a_spec=pl.BlockSpec((tk_mlp, d), lambda j, k: (k, 0)),
                   b_spec=pl.BlockSpec((tk_mlp, c1), lambda j, k: (k, j)),
                   out_shape=jax.ShapeDtypeStruct((N_DEV, d, c1), BF16),
                   o_spec=pl.BlockSpec((None, d, c1), lambda j, k: (j, 0, 0)), dims=TN, nk=s // tk_mlp,
                   acc_shape=(d, c1))
        mlp1_going = _start_copies(f"grads_start_mlp1_{l}", [dg_1], vec(g_mlp), blockwise=True)
        dx, dx16, dg_mlp = _rms_bwd(f"rms_mlp_bwd_{l}", d_h2, x1, vec(g_mlp) + mlp1_going["token"][0:1, 0:1], dx, tr)
        d_ya, d_yb, d_pw, d_gates, d_pa, d_sw, d_pc, d_bout, d_pscale = _merge_bwd(
            f"merge_bwd_{l}", dx16, proj, y_a, y_b, pw, vec(pool_scale), g_row, g_pool, (o_a, o_b, o_o), d, tx)
        dg_row = dd_grad(f"dw_o_{l}", merged, dx16, o_o, dg_row)
        dg_row = dd_grad(f"dw_out_a_{l}", p_a, d_ya, o_a, dg_row)
        dg_row = dd_grad(f"dw_out_b_{l}", sw, d_yb, o_b, dg_row)
        dg_pool = _pool_wgrad(f"dw_pool_{l}", p_c, d_pw, d, tk)
        rest_going = _start_copies(f"grads_start_rest_{l}", [dg_row, dg_pool], vec(g_mlp), blockwise=True)
        d_proj, d_bin, d_ca, d_cb, d_cbb, d_lng, d_lnb = _mix_pre_bwd(
            f"mix_bwd_{l}", proj, cv, d_pa, d_sw, d_pc, d_gates, conv_a_f[l], conv_b_f[l],
            vec(ln_b_g) + rest_going["token"][0:1, 0:1], vec(ln_b_b), d, tx)
        dg_in = _mm(f"dw_in_{l}", h, d_proj, grid=(N_DEV // 2, s // tk_in),
                    a_spec=pl.BlockSpec((tk_in, d), lambda j, k: (k, 0)),
                    b_spec=pl.BlockSpec((tk_in, 2 * ci), lambda j, k: (k, j)),
                    out_shape=jax.ShapeDtypeStruct((N_DEV // 2, d, 2 * ci), BF16),
                    o_spec=pl.BlockSpec((None, d, 2 * ci), lambda j, k: (j, 0, 0)), dims=TN, nk=s // tk_in,
                    acc_shape=(d, 2 * ci), after=rest_going["token"])
        in_going = _start_copies(f"grads_start_in_{l}", [dg_in], vec(g_mix), blockwise=True, paired=[True])
        d_h = _mm(f"d_h_{l}", d_proj, g_in, grid=(s // tm, N_DEV // 2),
                  a_spec=pl.BlockSpec((tm, 2 * ci), lambda i, k: (i, k)),
                  b_spec=pl.BlockSpec((None, d, 2 * ci), lambda i, k: (k, 0, 0)),
                  out_shape=jax.ShapeDtypeStruct((s, d), BF16), o_spec=pl.BlockSpec((tm, d), lambda i, k: (i, 0)),
                  dims=NT, nk=N_DEV // 2, acc_shape=(tm, d), after=in_going["token"])
        dx, dx16, dg_mix = _rms_bwd(f"rms_mix_bwd_{l}", d_h, x0, vec(g_mix), dx, tr)
        small[l] = (dg_mix, d_bin, d_cbb, d_lng, d_lnb, d_bout, d_pscale, dg_mlp, d_ca, d_cb)
        exchanges[l] = (in_going, mlp1_going, rest_going)

    grad_x = dx[None]

    names = ("g_mix", "b_in", "conv_b_bias", "ln_b_g", "ln_b_b", "b_out_b", "pool_scale", "g_mlp")
    given = dict(g_mix=(g_mix, m_g_mix, v_g_mix), b_in=(b_in, m_b_in, v_b_in),
                 conv_b_bias=(conv_b_bias, m_conv_b_bias, v_conv_b_bias), ln_b_g=(ln_b_g, m_ln_b_g, v_ln_b_g),
                 ln_b_b=(ln_b_b, m_ln_b_b, v_ln_b_b), b_out_b=(b_out_b, m_b_out_b, v_b_out_b),
                 pool_scale=(pool_scale, m_pool_scale, v_pool_scale), g_mlp=(g_mlp, m_g_mlp, v_g_mlp))
    partials, triples = [], []
    for i, nm in enumerate(names):
        partials.append(jnp.concatenate([small[l][i] for l in range(n_layers)], axis=0))
        triples.append(given[nm])
    partials.append(dg_final)
    triples.append(tuple(a.reshape(1, d) for a in (g_final, m_g_final, v_g_final)))
    partials.append(jnp.concatenate([small[l][8] for l in range(n_layers)], axis=0))
    partials.append(jnp.concatenate([small[l][9] for l in range(n_layers)], axis=0))
    outs = _small_update(partials, triples, 2)
    rep = {nm: outs[4 * i:4 * i + 4] for i, nm in enumerate(names)}
    rep["g_final"] = [a.reshape(d) for a in outs[4 * len(names):4 * len(names) + 4]]
    me = _block_of(*_my_place())
    gca = lax.dynamic_slice_in_dim(outs[-2].reshape(n_layers, K_A, d), me * rd, rd, axis=2)
    gcb = lax.dynamic_slice_in_dim(outs[-1].reshape(n_layers, K_B, d), me * rd, rd, axis=2)

    r_in, r_1, r_row, r_pool = [], [], [], []
    for l in reversed(range(n_layers)):
        in_going, mlp1_going, rest_going = exchanges[l]
        srcs_m, lands_m = _wait_copies(f"grads_wait_mlp1_{l}", mlp1_going, outs[0], blockwise=True)
        srcs_r, lands_r = _wait_copies(f"grads_wait_rest_{l}", rest_going, outs[0], blockwise=True)
        srcs_i, lands_i = _wait_copies(f"grads_wait_in_{l}", in_going, outs[0], blockwise=True)
        got = _place_own(f"grads_own_{l}", lands_i + lands_m + lands_r, srcs_i + srcs_m + srcs_r, me_arr, blockwise=True,
                         paired=layer_pairing)
        for lst, arr in zip((r_in, r_1, r_row, r_pool), got):
            lst.insert(0, arr)
    tb = min(256, d)
    layers = range(n_layers)
    res = {}
    res["w_in"] = _adamw("adamw_w_in", r_in, w_in, m_w_in, v_w_in, grid=(n_layers, d // tb),
                         part_specs=[_layer_part_spec(li, (N_DEV, tb, ci), d // tb) for li in layers],
                         w_spec=pl.BlockSpec((None, tb, ci), lambda l, i: (l, i, 0)))
    res["w_mlp1"] = _adamw("adamw_w_mlp1", r_1, w_mlp1, m_w_mlp1, v_w_mlp1, grid=(n_layers, d // tb),
                           part_specs=[_layer_part_spec(li, (N_DEV, tb, c1), d // tb) for li in layers],
                           w_spec=pl.BlockSpec((None, tb, c1), lambda l, i: (l, i, 0)))
    tf = min(256, rf)
    res["w_mlp2"] = _adamw("adamw_w_mlp2", r_row, w_mlp2, m_w_mlp2, v_w_mlp2, grid=(n_layers, rf // tf),
                           part_specs=[_layer_part_spec(li, (N_DEV, tf, d), rf // tf) for li in layers],
                           w_spec=pl.BlockSpec((None, tf, d), lambda l, i: (l, i, 0)))
    for nm, off, trip in (("w_out_a", o_a, (w_out_a, m_w_out_a, v_w_out_a)),
                          ("w_out_b", o_b, (w_out_b, m_w_out_b, v_w_out_b)), ("w_o", o_o, (w_o, m_w_o, v_w_o))):
        res[nm] = _adamw(f"adamw_{nm}", r_row, *trip, grid=(n_layers, 1),
                         part_specs=[_layer_part_spec(li, (N_DEV, rd, d), 1, row_off=off) for li in layers],
                         w_spec=pl.BlockSpec((None, rd, d), lambda l, i: (l, 0, 0)))
    res["w_pool"] = _adamw("adamw_w_pool", r_pool, w_pool, m_w_pool, v_w_pool, grid=(n_layers, 1),
                           part_specs=[_layer_part_spec(li, (N_DEV, N_GROUPS, gc // N_DEV, gc), 1) for li in layers],
                           w_spec=pl.BlockSpec((None, N_GROUPS, gc // N_DEV, gc), lambda l, i: (l, 0, 0, 0)))
    whole3 = lambda: (0, 0, 0)
    res["conv_a"] = _adamw("adamw_conv_a", [gca[None]], conv_a, m_conv_a, v_conv_a, grid=(),
                           part_specs=[pl.BlockSpec((1, n_layers, K_A, rd), lambda: (0, 0, 0, 0))],
                           w_spec=pl.BlockSpec((n_layers, K_A, rd), whole3))
    res["conv_b"] = _adamw("adamw_conv_b", [gcb[None]], conv_b, m_conv_b, v_conv_b, grid=(),
                           part_specs=[pl.BlockSpec((1, n_layers, K_B, rd), lambda: (0, 0, 0, 0))],
                           w_spec=pl.BlockSpec((n_layers, K_B, rd), whole3))
    res.update(rep)

    order = ("g_mix", "w_in", "b_in", "conv_a", "w_out_a", "conv_b", "conv_b_bias", "ln_b_g", "ln_b_b", "w_out_b",
             "b_out_b", "w_pool", "pool_scale", "w_o", "g_mlp", "w_mlp1", "w_mlp2", "g_final")
    out = [loss, grad_x]
    for kind in range(4):
        out += [res[nm][kind] for nm in order]
    return tuple(out)
```

```python
import jax
import jax.numpy as jnp
from jax import lax
from jax.experimental import pallas as pl
from jax.experimental.pallas import tpu as pltpu

F32 = jnp.float32
BF16 = jnp.bfloat16
MESH = pl.DeviceIdType.MESH

N_DEV = 8
EPS = 1e-6
K_A = 3
K_B = 31
POOL_WINDOWS = (2, 4, 8, 16)
N_GROUPS = len(POOL_WINDOWS)
HALO = 32
CHUNK = 16
SUBLANES = 8
TAP_GROUP = 4
ADAM_LR, ADAM_B1, ADAM_B2, ADAM_EPS, ADAM_WD, ADAM_STEP = 0.001, 0.9, 0.999, 1e-08, 0.01, 10
VMEM_LIMIT_BYTES = 60 * 1024 * 1024

NN = (((1,), (0,)), ((), ()))
NT = (((1,), (1,)), ((), ()))
TN = (((0,), (0,)), ((), ()))


def _params(*sem):
    return pltpu.CompilerParams(dimension_semantics=sem, vmem_limit_bytes=VMEM_LIMIT_BYTES)


def _sigmoid(v):
    return 1.0 / (1.0 + jnp.exp(-v))


def _mm(name, a, b, *, grid, a_spec, b_spec, out_shape, o_spec, dims, nk=1, acc_shape=None,
        extras=(), extra_specs=(), prologue=None, epilogue=None, alias_in=None, slabs=None, after=None,
        rms_bwd=None):
    n_extra = len(extras)
    has_alias = alias_in is not None
    n_unread = (1 if has_alias else 0) + (1 if after is not None else 0)

    def body(*refs):
        a_ref, b_ref = refs[0], refs[1]
        ex = refs[2:2 + n_extra]
        o_ref = refs[2 + n_extra + n_unread]
        av = a_ref[...]
        if prologue is not None:
            av = prologue(av)
        av = av.astype(BF16)

        def finish_rms(val):
            x_ref, g_ref, dr_ref = ex
            dx_ref, dx16_ref, dg_ref = refs[2 + n_extra + n_unread:5 + n_extra + n_unread]
            acc_g = refs[-1]
            row_axis, n_rows = rms_bwd
            ri = pl.program_id(row_axis)
            xv = x_ref[...]
            r = lax.rsqrt(jnp.mean(xv * xv, axis=-1, keepdims=True) + EPS)
            xh = xv * r
            part = _colsum8(val * xh)

            @pl.when(ri == 0)
            def _():
                acc_g[...] = part

            @pl.when(ri > 0)
            def _():
                acc_g[...] += part

            dxh = val * g_ref[...]
            dx = r * (dxh - xh * jnp.mean(dxh * xh, axis=-1, keepdims=True)) + dr_ref[...]
            dx_ref[...] = dx
            dx16_ref[...] = dx.astype(BF16)

            @pl.when(ri == n_rows - 1)
            def _():
                dg_ref[...] = jnp.sum(acc_g[...], axis=0, keepdims=True)

        def finish(val, cols=None):
            if rms_bwd is not None:
                return finish_rms(val)
            if epilogue is not None:
                val = epilogue(val, *[e[...] if cols is None else e[:, cols] for e in ex])
            if cols is None:
                o_ref[...] = val.astype(o_ref.dtype).reshape(o_ref.shape)
            else:
                o_ref[:, cols] = val.astype(o_ref.dtype)

        if slabs == "n":
            for q in range(b_ref.shape[0]):
                pq = lax.dot_general(av, b_ref[q].astype(BF16), dims, preferred_element_type=F32)
                finish(pq, slice(q * pq.shape[1], (q + 1) * pq.shape[1]))
            return
        if slabs == "k":
            kc = av.shape[1] // b_ref.shape[0]
            p = None
            for q in range(b_ref.shape[0]):
                pq = lax.dot_general(av[:, q * kc:(q + 1) * kc], b_ref[q].astype(BF16), dims,
                                     preferred_element_type=F32)
                p = pq if p is None else p + pq
        else:
            bv = b_ref[...]
            bv = bv.reshape((-1, bv.shape[-1])).astype(BF16)
            p = lax.dot_general(av, bv, dims, preferred_element_type=F32)

        if nk == 1:
            finish(p)
        else:
            acc = refs[-2] if rms_bwd is not None else refs[-1]
            k = pl.program_id(len(grid) - 1)

            @pl.when(k == 0)
            def _():
                acc[...] = p

            @pl.when(k > 0)
            def _():
                acc[...] += p

            @pl.when(k == nk - 1)
            def _():
                finish(acc[...])

    in_specs = [a_spec, b_spec, *extra_specs]
    operands = [a, b, *extras]
    aliases = {}
    if has_alias:
        in_specs.append(pl.BlockSpec(memory_space=pl.ANY))
        operands.append(alias_in)
        aliases = {len(operands) - 1: 0}
    if after is not None:
        in_specs.append(pl.BlockSpec(memory_space=pl.ANY))
        operands.append(after)
    sem = ("parallel",) * (len(grid) - 1) + (("arbitrary",) if nk > 1 else ("parallel",))
    scratch = [pltpu.VMEM(acc_shape, F32)] if nk > 1 else []
    if rms_bwd is not None:
        sem = ("arbitrary",) * len(grid)
        scratch.append(pltpu.VMEM((8, extras[0].shape[-1]), F32))
    return pl.pallas_call(
        body, name=name, grid=grid, in_specs=in_specs, out_specs=o_spec, out_shape=out_shape,
        scratch_shapes=scratch, input_output_aliases=aliases, compiler_params=_params(*sem),
    )(*operands)


def _relu_sq(v):
    r = jnp.maximum(v, 0)
    return r * r


def _rms_fwd(name, x, g, tm):
    s, d = x.shape

    def body(x_ref, g_ref, h_ref):
        xv = x_ref[...]
        r = lax.rsqrt(jnp.mean(xv * xv, axis=-1, keepdims=True) + EPS)
        h_ref[...] = (xv * r * g_ref[...]).astype(h_ref.dtype)

    return pl.pallas_call(
        body, name=name, grid=(s // tm,),
        in_specs=[pl.BlockSpec((tm, d), lambda i: (i, 0)), pl.BlockSpec((1, d), lambda i: (0, 0))],
        out_specs=pl.BlockSpec((tm, d), lambda i: (i, 0)),
        out_shape=jax.ShapeDtypeStruct((s, d), BF16), compiler_params=_params("parallel"),
    )(x, g)


def _colsum8(v):
    return jnp.sum(v.reshape(v.shape[0] // 8, 8, v.shape[1]), axis=0)


def _rms_bwd(name, dh, x, g, dres, tm):
    s, d = x.shape
    n = s // tm

    def body(dh_ref, x_ref, g_ref, dr_ref, dx_ref, dx16_ref, dg_ref, acc):
        i = pl.program_id(0)
        xv = x_ref[...]
        r = lax.rsqrt(jnp.mean(xv * xv, axis=-1, keepdims=True) + EPS)
        xh = xv * r
        dhv = dh_ref[...].astype(F32)
        part = _colsum8(dhv * xh)

        @pl.when(i == 0)
        def _():
            acc[...] = part

        @pl.when(i > 0)
        def _():
            acc[...] += part

        dxh = dhv * g_ref[...]
        dx = r * (dxh - xh * jnp.mean(dxh * xh, axis=-1, keepdims=True))
        dx = dx + dr_ref[...]
        dx_ref[...] = dx
        dx16_ref[...] = dx.astype(BF16)

        @pl.when(i == n - 1)
        def _():
            dg_ref[...] = jnp.sum(acc[...], axis=0, keepdims=True)

    return pl.pallas_call(
        body, name=name, grid=(n,),
        in_specs=[pl.BlockSpec((tm, d), lambda i: (i, 0)), pl.BlockSpec((tm, d), lambda i: (i, 0)),
                  pl.BlockSpec((1, d), lambda i: (0, 0)), pl.BlockSpec((tm, d), lambda i: (i, 0))],
        out_specs=[pl.BlockSpec((tm, d), lambda i: (i, 0)), pl.BlockSpec((tm, d), lambda i: (i, 0)),
                   pl.BlockSpec((1, d), lambda i: (0, 0))],
        out_shape=[jax.ShapeDtypeStruct((s, d), F32), jax.ShapeDtypeStruct((s, d), BF16),
                   jax.ShapeDtypeStruct((1, d), F32)],
        scratch_shapes=[pltpu.VMEM((8, d), F32)], compiler_params=_params("arbitrary"),
    )(dh, x, g, dres)


def _loss_head(x, g, target, tm):
    s, d = x.shape
    n = s // tm

    def body(x_ref, g_ref, t_ref, loss_ref, dx_ref, dx16_ref, dg_ref, acc_l, acc_g):
        i = pl.program_id(0)
        xv = x_ref[...]
        r = lax.rsqrt(jnp.mean(xv * xv, axis=-1, keepdims=True) + EPS)
        xh = xv * r
        err = xh * g_ref[...] - t_ref[...]
        dy = err * (1.0 / d)
        lpart = _colsum8(err * err)
        gpart = _colsum8(dy * xh)

        @pl.when(i == 0)
        def _():
            acc_l[...] = lpart
            acc_g[...] = gpart

        @pl.when(i > 0)
        def _():
            acc_l[...] += lpart
            acc_g[...] += gpart

        dxh = dy * g_ref[...]
        dx = r * (dxh - xh * jnp.mean(dxh * xh, axis=-1, keepdims=True))
        dx_ref[...] = dx
        dx16_ref[...] = dx.astype(BF16)

        @pl.when(i == n - 1)
        def _():
            loss_ref[...] = (0.5 / d) * jnp.sum(jnp.sum(acc_l[...], axis=0, keepdims=True), axis=1, keepdims=True)
            dg_ref[...] = jnp.sum(acc_g[...], axis=0, keepdims=True)

    return pl.pallas_call(
        body, name="loss_head", grid=(n,),
        in_specs=[pl.BlockSpec((tm, d), lambda i: (i, 0)), pl.BlockSpec((1, d), lambda i: (0, 0)),
                  pl.BlockSpec((tm, d), lambda i: (i, 0))],
        out_specs=[pl.BlockSpec((1, 1), lambda i: (0, 0)), pl.BlockSpec((tm, d), lambda i: (i, 0)),
                   pl.BlockSpec((tm, d), lambda i: (i, 0)), pl.BlockSpec((1, d), lambda i: (0, 0))],
        out_shape=[jax.ShapeDtypeStruct((1, 1), F32), jax.ShapeDtypeStruct((s, d), F32),
                   jax.ShapeDtypeStruct((s, d), BF16), jax.ShapeDtypeStruct((1, d), F32)],
        scratch_shapes=[pltpu.VMEM((8, d), F32), pltpu.VMEM((8, d), F32)], compiler_params=_params("arbitrary"),
    )(x, g, target)


def _sec(ref, n, d):
    return ref[:, n * d:(n + 1) * d].astype(F32)


def _fill_shifts(sh, ext):
    rows = ext.shape[0] - SUBLANES
    for b in range(1, SUBLANES):
        sh[b - 1, 0:rows, :] = ext[b:b + rows, :]


def _shifted(sh, ext, off, n):
    b = off % SUBLANES
    if b == 0:
        return ext[off:off + n, :]
    return sh[b - 1, off - b:off - b + n, :]


def _pool_count(row0, rows, window):
    t = row0 + lax.broadcasted_iota(jnp.int32, (rows, 1), 0)
    return jnp.minimum(t + 1, window).astype(F32)


def _group_weight(w_ref, gi, gc):
    return w_ref[:, gi].reshape(gc, gc)


def _mixer_fwd(name, proj, x0, conv_a, conv_b, conv_b_bias, ln_g, ln_b, b_out_b, pool_scale, g_next, g_row, g_pool, offs,
               d, tm):
    s = proj.shape[0]
    n = s // tm
    gc = d // N_GROUPS
    hb = tm // HALO
    rd = d // N_DEV
    o_a, o_b, o_o = offs

    def body(pj_ref, hp_ref, x0_ref, ca_ref, cb_ref, cbb_ref, lng_ref, lnb_ref, bo_ref, sc_ref, gn_ref, wa_ref, wb_ref,
             wo_ref, wp_ref, pa_ref, sw_ref, pc_ref, cv_ref, ya_ref, yb_ref, pw_ref, mg_ref, x1_ref, h2_ref,
             eua, eub, euc, sh):
        i = pl.program_id(0)
        keep = (i > 0).astype(F32)
        eua[0:HALO, :] = _sec(hp_ref, 1, d) * _sec(hp_ref, 2, d) * keep
        eub[0:HALO, :] = _sec(hp_ref, 3, d) * _sigmoid(_sec(hp_ref, 4, d)) * keep
        euc[0:HALO, :] = _sec(hp_ref, 5, d) * keep
        eua[HALO:HALO + tm, :] = _sec(pj_ref, 1, d) * _sec(pj_ref, 2, d)
        eub[HALO:HALO + tm, :] = _sec(pj_ref, 3, d) * _sigmoid(_sec(pj_ref, 4, d))
        euc[HALO:HALO + tm, :] = _sec(pj_ref, 5, d)
        _fill_shifts(sh, eub)
        for c in range(tm // CHUNK):
            r0 = c * CHUNK
            z = jnp.zeros((CHUNK, d), F32)
            for k in range(K_A):
                z = z + ca_ref[k:k + 1, :] * eua[HALO + r0 - (K_A - 1) + k:HALO + r0 - (K_A - 1) + k + CHUNK, :]
            pa_ref[r0:r0 + CHUNK, :] = (pj_ref[r0:r0 + CHUNK, 0:d].astype(F32) * z).astype(pa_ref.dtype)
            cv = jnp.zeros((CHUNK, d), F32) + cbb_ref[...]
            for k in range(K_B):
                cv = cv + cb_ref[k:k + 1, :] * _shifted(sh, eub, HALO + r0 - (K_B - 1) + k, CHUNK)
            cv_ref[r0:r0 + CHUNK, :] = cv.astype(cv_ref.dtype)
        cvv = cv_ref[...].astype(F32)
        mu = jnp.mean(cvv, axis=-1, keepdims=True)
        xc = cvv - mu
        xh = xc * lax.rsqrt(jnp.mean(xc * xc, axis=-1, keepdims=True) + EPS)
        ln = xh * lng_ref[...] + lnb_ref[...]
        sw_ref[...] = (ln * _sigmoid(ln)).astype(sw_ref.dtype)
        for gi, w in enumerate(POOL_WINDOWS):
            cols = slice(gi * gc, (gi + 1) * gc)
            tot = euc[HALO:HALO + tm, cols]
            for k in range(1, w):
                tot = tot + euc[HALO - k:HALO - k + tm, cols]
            cnt = _pool_count(i * tm, tm, w)
            pc_ref[:, cols] = (tot / cnt - euc[HALO:HALO + tm, cols]).astype(pc_ref.dtype)
        ya_ref[...] = jnp.dot(pa_ref[...], wa_ref[...].reshape(d, d), preferred_element_type=F32).astype(ya_ref.dtype)
        yb_ref[...] = (jnp.dot(sw_ref[...], wb_ref[...].reshape(d, d), preferred_element_type=F32)
                       + bo_ref[...]).astype(yb_ref.dtype)
        for gi in range(N_GROUPS):
            cols = slice(gi * gc, (gi + 1) * gc)
            pw_ref[:, cols] = jnp.dot(pc_ref[:, cols], _group_weight(wp_ref, gi, gc),
                                      preferred_element_type=F32).astype(pw_ref.dtype)
        m = _sigmoid(_sec(pj_ref, 6, d)) * ya_ref[...].astype(F32)
        m = m + _sigmoid(_sec(pj_ref, 7, d)) * yb_ref[...].astype(F32)
        m = m + _sigmoid(_sec(pj_ref, 8, d)) * (pw_ref[...].astype(F32) * sc_ref[...])
        mg_ref[...] = m.astype(mg_ref.dtype)
        x1 = x0_ref[...] + jnp.dot(mg_ref[...], wo_ref[...].reshape(d, d), preferred_element_type=F32)
        x1_ref[...] = x1
        h2_ref[...] = (x1 * lax.rsqrt(jnp.mean(x1 * x1, axis=-1, keepdims=True) + EPS) * gn_ref[...]).astype(h2_ref.dtype)

    row = lambda i: (i, 0)
    fixed = lambda i: (0, 0)
    act = jax.ShapeDtypeStruct((s, d), BF16)

    def dd_weight(off):
        return pl.BlockSpec((N_DEV, rd, d), lambda i: (0, off, 0), pipeline_mode=pl.Buffered(1))

    return pl.pallas_call(
        body, name=name, grid=(n,),
        in_specs=[pl.BlockSpec((tm, 9 * d), row),
                  pl.BlockSpec((HALO, 6 * d), lambda i: (jnp.maximum(i * hb - 1, 0), 0)),
                  pl.BlockSpec((tm, d), row),
                  pl.BlockSpec((K_A, d), fixed), pl.BlockSpec((K_B, d), fixed), pl.BlockSpec((1, d), fixed),
                  pl.BlockSpec((1, d), fixed), pl.BlockSpec((1, d), fixed), pl.BlockSpec((1, d), fixed),
                  pl.BlockSpec((1, d), fixed), pl.BlockSpec((1, d), fixed), dd_weight(o_a), dd_weight(o_b), dd_weight(o_o),
                  pl.BlockSpec((N_DEV, N_GROUPS, gc // N_DEV, gc), lambda i: (0, 0, 0, 0),
                               pipeline_mode=pl.Buffered(1))],
        out_specs=[pl.BlockSpec((tm, d), row)] * 10,
        out_shape=[act] * 8 + [jax.ShapeDtypeStruct((s, d), F32), act],
        scratch_shapes=[pltpu.VMEM((tm + HALO, d), F32)] * 3 + [pltpu.VMEM((SUBLANES - 1, tm + HALO, d), F32)],
        compiler_params=_params("parallel"),
    )(proj, proj, x0, conv_a, conv_b, conv_b_bias, ln_g, ln_b, b_out_b, pool_scale, g_next, g_row, g_row, g_row,
      g_pool)


def _merge_bwd(name, dx16, proj, ya, yb, pw, pool_scale, g_row, g_pool, offs, d, tm):
    s = proj.shape[0]
    n = s // tm
    gc = d // N_GROUPS
    rd = d // N_DEV
    o_a, o_b, o_o = offs

    def body(dx_ref, g_ref, ya_ref, yb_ref, pw_ref, sc_ref, wa_ref, wb_ref, wo_ref, wp_ref,
             dya_ref, dyb_ref, dpw_ref, dg_ref, dpa_ref, dsw_ref, dpc_ref, dbo_ref, dsc_ref, acc_b, acc_s):
        i = pl.program_id(0)
        dmv = lax.dot_general(dx_ref[...], wo_ref[...].reshape(d, d), NT,
                              preferred_element_type=F32).astype(BF16).astype(F32)
        scale = sc_ref[...]
        g0 = _sigmoid(_sec(g_ref, 0, d))
        dya_ref[...] = (dmv * g0).astype(dya_ref.dtype)
        dg_ref[:, 0:d] = (dmv * ya_ref[...].astype(F32) * g0 * (1.0 - g0)).astype(dg_ref.dtype)
        g1 = _sigmoid(_sec(g_ref, 1, d))
        dyb = dmv * g1
        dyb_ref[...] = dyb.astype(dyb_ref.dtype)
        dg_ref[:, d:2 * d] = (dmv * yb_ref[...].astype(F32) * g1 * (1.0 - g1)).astype(dg_ref.dtype)
        g2 = _sigmoid(_sec(g_ref, 2, d))
        pwv = pw_ref[...].astype(F32)
        dyc = dmv * g2
        dpw_ref[...] = (dyc * scale).astype(dpw_ref.dtype)
        dg_ref[:, 2 * d:3 * d] = (dmv * (pwv * scale) * g2 * (1.0 - g2)).astype(dg_ref.dtype)
        pb = _colsum8(dyb)
        ps = _colsum8(dyc * pwv)

        @pl.when(i == 0)
        def _():
            acc_b[...] = pb
            acc_s[...] = ps

        @pl.when(i > 0)
        def _():
            acc_b[...] += pb
            acc_s[...] += ps

        dpa_ref[...] = lax.dot_general(dya_ref[...], wa_ref[...].reshape(d, d), NT,
                                       preferred_element_type=F32).astype(dpa_ref.dtype)
        dsw_ref[...] = lax.dot_general(dyb_ref[...], wb_ref[...].reshape(d, d), NT,
                                       preferred_element_type=F32).astype(dsw_ref.dtype)
        for gi in range(N_GROUPS):
            cols = slice(gi * gc, (gi + 1) * gc)
            dpc_ref[:, cols] = lax.dot_general(dpw_ref[:, cols], _group_weight(wp_ref, gi, gc), NT,
                                               preferred_element_type=F32).astype(dpc_ref.dtype)

        @pl.when(i == n - 1)
        def _():
            dbo_ref[...] = jnp.sum(acc_b[...], axis=0, keepdims=True)
            dsc_ref[...] = jnp.sum(acc_s[...], axis=0, keepdims=True)

    row = lambda i: (i, 0)
    fixed = lambda i: (0, 0)
    act = jax.ShapeDtypeStruct((s, d), BF16)
    vec = jax.ShapeDtypeStruct((1, d), F32)

    def dd_weight(off):
        return pl.BlockSpec((N_DEV, rd, d), lambda i: (0, off, 0), pipeline_mode=pl.Buffered(1))

    return pl.pallas_call(
        body, name=name, grid=(n,),
        in_specs=[pl.BlockSpec((tm, d), row), pl.BlockSpec((tm, 3 * d), lambda i: (i, 2)), pl.BlockSpec((tm, d), row),
                  pl.BlockSpec((tm, d), row), pl.BlockSpec((tm, d), row), pl.BlockSpec((1, d), fixed),
                  dd_weight(o_a), dd_weight(o_b), dd_weight(o_o),
                  pl.BlockSpec((N_DEV, N_GROUPS, gc // N_DEV, gc), lambda i: (0, 0, 0, 0),
                               pipeline_mode=pl.Buffered(1))],
        out_specs=[pl.BlockSpec((tm, d), row)] * 3 + [pl.BlockSpec((tm, 3 * d), row)] + [pl.BlockSpec((tm, d), row)] * 3
                  + [pl.BlockSpec((1, d), fixed)] * 2,
        out_shape=[act, act, act, jax.ShapeDtypeStruct((s, 3 * d), BF16), act, act, act, vec, vec],
        scratch_shapes=[pltpu.VMEM((8, d), F32)] * 2, compiler_params=_params("arbitrary"),
    )(dx16, proj, ya, yb, pw, pool_scale, g_row, g_row, g_row, g_pool)


def _mix_pre_bwd(name, proj, cv, dpa, dsw, dpc, dgates, conv_a, conv_b, ln_g, ln_b, d, tm):
    s = proj.shape[0]
    n = s // tm
    gc = d // N_GROUPS
    hb = tm // HALO
    last_halo = s // HALO - 1
    te = tm + HALO

    def ln_bwd(cvv, dswv, lng, lnb):
        mu = jnp.mean(cvv, axis=-1, keepdims=True)
        xc = cvv - mu
        rstd = lax.rsqrt(jnp.mean(xc * xc, axis=-1, keepdims=True) + EPS)
        xh = xc * rstd
        ln = xh * lng + lnb
        sg = _sigmoid(ln)
        dln = dswv * (sg * (1.0 + ln * (1.0 - sg)))
        dxh = dln * lng
        dcv = rstd * (dxh - jnp.mean(dxh, axis=-1, keepdims=True) - xh * jnp.mean(dxh * xh, axis=-1, keepdims=True))
        return dcv, dln, xh

    def body(pj_ref, hp_ref, hf_ref, cv_ref, cvf_ref, dpa_ref, dpaf_ref, dsw_ref, dswf_ref, dpc_ref, dpcf_ref, dgt_ref,
             ca_ref, cb_ref, lng_ref, lnb_ref,
             dpj_ref, dbin_ref, dca_ref, dcb_ref, dcbb_ref, dlng_ref, dlnb_ref,
             eua, eub, edz, edcv, eq, sh, dub_s, acc_bin, acc_ca, acc_cb, acc_v):
        i = pl.program_id(0)
        keep_p = (i > 0).astype(F32)
        keep_f = (i < n - 1).astype(F32)

        @pl.when(i == 0)
        def _():
            acc_bin[...] = jnp.zeros_like(acc_bin)
            acc_ca[...] = jnp.zeros_like(acc_ca)
            acc_cb[...] = jnp.zeros_like(acc_cb)
            acc_v[...] = jnp.zeros_like(acc_v)

        eua[0:HALO, :] = _sec(hp_ref, 1, d) * _sec(hp_ref, 2, d) * keep_p
        eua[HALO:te, :] = _sec(pj_ref, 1, d) * _sec(pj_ref, 2, d)
        eub[HALO:te, :] = _sec(pj_ref, 3, d) * _sigmoid(_sec(pj_ref, 4, d))
        edz[0:tm, :] = dpa_ref[...].astype(F32) * _sec(pj_ref, 0, d)
        edz[tm:te, :] = dpaf_ref[...].astype(F32) * _sec(hf_ref, 0, d) * keep_f
        dcv, dln, xh = ln_bwd(cv_ref[...].astype(F32), dsw_ref[...].astype(F32), lng_ref[...], lnb_ref[...])
        edcv[0:tm, :] = dcv
        acc_v[0:8, :] += _colsum8(dcv)
        acc_v[8:16, :] += _colsum8(dln * xh)
        acc_v[16:24, :] += _colsum8(dln)
        dcvf, _, _ = ln_bwd(cvf_ref[...].astype(F32), dswf_ref[...].astype(F32), lng_ref[...], lnb_ref[...])
        edcv[tm:te, :] = dcvf * keep_f
        for gi, w in enumerate(POOL_WINDOWS):
            cols = slice(gi * gc, (gi + 1) * gc)
            eq[0:tm, cols] = dpc_ref[:, cols].astype(F32) / _pool_count(i * tm, tm, w)
            eq[tm:te, cols] = dpcf_ref[:, cols].astype(F32) / _pool_count((i + 1) * tm, HALO, w) * keep_f

        def put(sec_idx, r0, val):
            dpj_ref[r0:r0 + CHUNK, sec_idx * d:(sec_idx + 1) * d] = val.astype(dpj_ref.dtype)
            acc_bin[:, sec_idx * d:(sec_idx + 1) * d] += _colsum8(val)

        _fill_shifts(sh, edcv)
        for k0 in range(0, K_B, TAP_GROUP):
            taps = range(k0, min(k0 + TAP_GROUP, K_B))
            a = {k: jnp.zeros((8, d), F32) for k in taps}
            for c in range(tm // CHUNK):
                r0 = c * CHUNK
                ub = eub[HALO + r0:HALO + r0 + CHUNK, :]
                part = None
                for k in taps:
                    t = _shifted(sh, edcv, r0 + (K_B - 1) - k, CHUNK)
                    part = cb_ref[k:k + 1, :] * t if part is None else part + cb_ref[k:k + 1, :] * t
                    a[k] = a[k] + _colsum8(ub * t)
                if k0 == 0:
                    dub_s[r0:r0 + CHUNK, :] = part
                else:
                    dub_s[r0:r0 + CHUNK, :] += part
            for k in taps:
                acc_cb[k] += a[k]
        wa = [jnp.zeros((8, d), F32) for _ in range(K_A)]
        for c in range(tm // CHUNK):
            r0 = c * CHUNK
            rows = slice(r0, r0 + CHUNK)
            z = jnp.zeros((CHUNK, d), F32)
            dua = jnp.zeros((CHUNK, d), F32)
            ua = eua[HALO + r0:HALO + r0 + CHUNK, :]
            for k in range(K_A):
                z = z + ca_ref[k:k + 1, :] * eua[HALO + r0 - (K_A - 1) + k:HALO + r0 - (K_A - 1) + k + CHUNK, :]
                t = edz[r0 + (K_A - 1) - k:r0 + (K_A - 1) - k + CHUNK, :]
                dua = dua + ca_ref[k:k + 1, :] * t
                wa[k] = wa[k] + _colsum8(ua * t)
            put(0, r0, dpa_ref[rows, :].astype(F32) * z)
            put(1, r0, dua * pj_ref[rows, 2 * d:3 * d].astype(F32))
            put(2, r0, dua * pj_ref[rows, d:2 * d].astype(F32))
            dub = dub_s[rows, :]
            bval = pj_ref[rows, 3 * d:4 * d].astype(F32)
            sg = _sigmoid(pj_ref[rows, 4 * d:5 * d].astype(F32))
            put(3, r0, dub * sg)
            put(4, r0, dub * bval * sg * (1.0 - sg))
            for gi, w in enumerate(POOL_WINDOWS):
                cols = slice(gi * gc, (gi + 1) * gc)
                tot = eq[rows, cols]
                for k in range(1, w):
                    tot = tot + eq[r0 + k:r0 + k + CHUNK, cols]
                dci = tot - dpc_ref[rows, cols].astype(F32)
                dpj_ref[rows, 5 * d + gi * gc:5 * d + (gi + 1) * gc] = dci.astype(dpj_ref.dtype)
                acc_bin[:, 5 * d + gi * gc:5 * d + (gi + 1) * gc] += _colsum8(dci)
        for q in range(3):
            gv = dgt_ref[:, q * d:(q + 1) * d]
            dpj_ref[:, (6 + q) * d:(7 + q) * d] = gv
            acc_bin[:, (6 + q) * d:(7 + q) * d] += _colsum8(gv.astype(F32))
        for k in range(K_A):
            acc_ca[k] += wa[k]

        @pl.when(i == n - 1)
        def _():
            dbin_ref[...] = jnp.sum(acc_bin[...], axis=0, keepdims=True)
            for k in range(K_A):
                dca_ref[k:k + 1, :] = jnp.sum(acc_ca[k], axis=0, keepdims=True)
            for k in range(K_B):
                dcb_ref[k:k + 1, :] = jnp.sum(acc_cb[k], axis=0, keepdims=True)
            dcbb_ref[...] = jnp.sum(acc_v[0:8, :], axis=0, keepdims=True)
            dlng_ref[...] = jnp.sum(acc_v[8:16, :], axis=0, keepdims=True)
            dlnb_ref[...] = jnp.sum(acc_v[16:24, :], axis=0, keepdims=True)

    row = lambda i: (i, 0)
    fixed = lambda i: (0, 0)
    past = lambda i: (jnp.maximum(i * hb - 1, 0), 0)
    fut = lambda i: (jnp.minimum((i + 1) * hb, last_halo), 0)
    vec = jax.ShapeDtypeStruct((1, d), F32)
    tile_and_halo = [pl.BlockSpec((tm, d), row), pl.BlockSpec((HALO, d), fut)]
    return pl.pallas_call(
        body, name=name, grid=(n,),
        in_specs=[pl.BlockSpec((tm, 6 * d), row), pl.BlockSpec((HALO, 6 * d), past), pl.BlockSpec((HALO, 6 * d), fut),
                  *tile_and_halo, *tile_and_halo, *tile_and_halo, *tile_and_halo,
                  pl.BlockSpec((tm, 3 * d), row),
                  pl.BlockSpec((K_A, d), fixed), pl.BlockSpec((K_B, d), fixed), pl.BlockSpec((1, d), fixed),
                  pl.BlockSpec((1, d), fixed)],
        out_specs=[pl.BlockSpec((tm, 9 * d), row), pl.BlockSpec((1, 9 * d), fixed), pl.BlockSpec((K_A, d), fixed),
                   pl.BlockSpec((K_B, d), fixed), pl.BlockSpec((1, d), fixed), pl.BlockSpec((1, d), fixed),
                   pl.BlockSpec((1, d), fixed)],
        out_shape=[jax.ShapeDtypeStruct((s, 9 * d), BF16), jax.ShapeDtypeStruct((1, 9 * d), F32),
                   jax.ShapeDtypeStruct((K_A, d), F32), jax.ShapeDtypeStruct((K_B, d), F32), vec, vec, vec],
        scratch_shapes=[pltpu.VMEM((te, d), F32)] * 5 + [pltpu.VMEM((SUBLANES - 1, te, d), F32),
                                                         pltpu.VMEM((tm, d), F32),
                                                         pltpu.VMEM((8, 9 * d), F32), pltpu.VMEM((K_A, 8, d), F32),
                                                         pltpu.VMEM((K_B, 8, d), F32), pltpu.VMEM((24, d), F32)],
        compiler_params=_params("arbitrary"),
    )(proj, proj, proj, cv, cv, dpa, dpa, dsw, dsw, dpc, dpc, dgates, conv_a, conv_b, ln_g, ln_b)


def _pool_wgrad(name, p, dpw, d, tk):
    s = p.shape[0]
    gc = d // N_GROUPS
    n = s // tk

    def body(p_ref, g_ref, o_ref, acc):
        k = pl.program_id(0)
        for gi in range(N_GROUPS):
            cols = slice(gi * gc, (gi + 1) * gc)
            part = lax.dot_general(p_ref[:, cols], g_ref[:, cols], TN, preferred_element_type=F32)

            @pl.when(k == 0)
            def _():
                acc[gi] = part

            @pl.when(k > 0)
            def _():
                acc[gi] += part

        @pl.when(k == n - 1)
        def _():
            for gi in range(N_GROUPS):
                o_ref[:, gi] = acc[gi].astype(o_ref.dtype).reshape(N_DEV, gc // N_DEV, gc)

    return pl.pallas_call(
        body, name=name, grid=(n,),
        in_specs=[pl.BlockSpec((tk, d), lambda k: (k, 0)), pl.BlockSpec((tk, d), lambda k: (k, 0))],
        out_specs=pl.BlockSpec((N_DEV, N_GROUPS, gc // N_DEV, gc), lambda k: (0, 0, 0, 0)),
        out_shape=jax.ShapeDtypeStruct((N_DEV, N_GROUPS, gc // N_DEV, gc), BF16),
        scratch_shapes=[pltpu.VMEM((N_GROUPS, gc, gc), F32)], compiler_params=_params("arbitrary"),
    )(p, dpw)


def _my_place():
    x, y, c = lax.axis_index("x"), lax.axis_index("y"), lax.axis_index("c")
    return x, y, c


def _block_of(x, y, c):
    return 4 * x + 2 * y + c


def _slot(ref, k, paired):
    if not paired:
        return ref.at[k]
    cols = ref.shape[-1] // 2
    return ref.at[k // 2, :, pl.ds(pl.multiple_of((k % 2) * cols, 128), cols)]


def _slot_shape(shape, paired):
    return (N_DEV // 2, shape[0], 2 * shape[1]) if paired else (N_DEV, *shape)


def _gather_shards(shards, paired):
    n_arr = len(shards)

    def body(*refs):
        srcs = refs[:n_arr]
        outs = refs[n_arr:2 * n_arr]
        send_sems, recv_sems, local_sems = refs[2 * n_arr:]
        x, y, c = _my_place()
        me, sibling = (x, y, c), (x, y, 1 - c)
        chips = [(1 - x, y), (x, 1 - y), (1 - x, 1 - y)]

        def copy(n, k, block, to, src=None):
            rows = _slot(outs[n], _block_of(*block), paired[n])
            return pltpu.make_async_remote_copy(
                src_ref=rows if src is None else src, dst_ref=rows, send_sem=send_sems.at[n, k],
                recv_sem=recv_sems.at[n, k], device_id=to, device_id_type=MESH)

        mine = [pltpu.make_async_copy(srcs[n], _slot(outs[n], _block_of(*me), paired[n]), local_sems.at[n])
                for n in range(n_arr)]
        for cp in mine:
            cp.start()
        first = []
        for n in range(n_arr):
            first.append(copy(n, 0, me, sibling, src=srcs[n]))
            first += [copy(n, 1 + j, me, (*chip, c), src=srcs[n]) for j, chip in enumerate(chips)]
        for cp in first:
            cp.start()
        passed = []
        for n in range(n_arr):
            for j, chip in enumerate(chips):
                copy(n, 1 + j, (*chip, c), me).wait_recv()
                fwd = copy(n, 4 + j, (*chip, c), sibling)
                fwd.start()
                passed.append(fwd)
        for n in range(n_arr):
            copy(n, 0, sibling, me).wait_recv()
            for j, chip in enumerate(chips):
                copy(n, 4 + j, (*chip, 1 - c), me).wait_recv()
        for cp in first + passed:
            cp.wait_send()
        for cp in mine:
            cp.wait()

    any_spec = pl.BlockSpec(memory_space=pl.ANY)
    return pl.pallas_call(
        body, name="gather_weights",
        in_specs=[any_spec] * n_arr, out_specs=[any_spec] * n_arr,
        out_shape=[jax.ShapeDtypeStruct(_slot_shape(sh.shape, p), sh.dtype) for sh, p in zip(shards, paired)],
        scratch_shapes=[pltpu.SemaphoreType.DMA((n_arr, 7)), pltpu.SemaphoreType.DMA((n_arr, 7)),
                        pltpu.SemaphoreType.DMA((n_arr,))],
    )(*shards)


def _peers(x, y, c):
    out = []
    for r in range(1, N_DEV):
        fx, fy, fc = (r >> 2) & 1, (r >> 1) & 1, r & 1
        out.append(((1 - x) if fx else x, (1 - y) if fy else y, (1 - c) if fc else c))
    return out


HBM_SPEC = pl.BlockSpec(memory_space=pltpu.HBM)
SEM_SPEC = pl.BlockSpec(memory_space=pltpu.SEMAPHORE)
ANY_SPEC = pl.BlockSpec(memory_space=pl.ANY)
N_PEERS = N_DEV - 1


def _peer_copy(src_ref, land_ref, send_sems, recv_sems, i, r, peer, me, blockwise, paired):
    src = _slot(src_ref, _block_of(*peer), paired) if blockwise else src_ref
    dst = land_ref.at[me] if blockwise else _slot(land_ref, me, paired)
    return pltpu.make_async_remote_copy(
        src_ref=src, dst_ref=dst, send_sem=send_sems.at[i * N_PEERS + r],
        recv_sem=recv_sems.at[i * N_PEERS + r], device_id=peer, device_id_type=MESH)


def _block_shape(shape, paired):
    return (shape[1], shape[2] // 2) if paired else tuple(shape[1:])


def _start_copies(name, srcs, after, blockwise, paired=None):
    n = len(srcs)
    paired = paired or [False] * n

    def body(*refs):
        s_in, l_in = refs[:n], refs[n:2 * n]
        send_sems, recv_sems = refs[2 * n + 1], refs[2 * n + 2]
        token = refs[-1]
        x, y, c = _my_place()
        me = _block_of(x, y, c)
        for i in range(n):
            for r, peer in enumerate(_peers(x, y, c)):
                _peer_copy(s_in[i], l_in[i], send_sems, recv_sems, i, r, peer, me, blockwise, paired[i]).start()
        token[...] = jnp.zeros_like(token)

    land_shapes = [(N_DEV, *_block_shape(s.shape, p)) if blockwise else _slot_shape(s.shape, p)
                   for s, p in zip(srcs, paired)]
    lands = [pltpu.with_memory_space_constraint(lax.empty(sh, s.dtype), pltpu.HBM) for sh, s in zip(land_shapes, srcs)]
    ins = [pltpu.with_memory_space_constraint(s, pltpu.HBM) for s in srcs]
    out = pl.pallas_call(
        body, name=name,
        out_shape=(pltpu.SemaphoreType.DMA((n * N_PEERS,)), pltpu.SemaphoreType.DMA((n * N_PEERS,)),
                   *[pltpu.HBM(s.shape, s.dtype) for s in srcs],
                   *[pltpu.HBM(sh, s.dtype) for sh, s in zip(land_shapes, srcs)],
                   jax.ShapeDtypeStruct((8, 128), F32)),
        in_specs=[HBM_SPEC] * (2 * n) + [ANY_SPEC],
        out_specs=(SEM_SPEC, SEM_SPEC, *[HBM_SPEC] * (2 * n), pl.BlockSpec(memory_space=pltpu.VMEM)),
        input_output_aliases={i: 2 + i for i in range(2 * n)},
        compiler_params=pltpu.CompilerParams(has_side_effects=pltpu.SideEffectType.DATAFLOW_SIDE_EFFECTING),
    )(*ins, *lands, after)
    return dict(send=out[0], recv=out[1], srcs=list(out[2:2 + n]), lands=list(out[2 + n:2 + 2 * n]), token=out[-1],
                paired=paired)


def _wait_copies(name, state, after, blockwise):
    n = len(state["srcs"])
    paired = state["paired"]

    def body(*refs):
        s_in, l_in = refs[:n], refs[n:2 * n]
        send_sems, recv_sems = refs[2 * n], refs[2 * n + 1]
        x, y, c = _my_place()
        me = _block_of(x, y, c)
        for i in range(n):
            for r, peer in enumerate(_peers(x, y, c)):
                cp = _peer_copy(s_in[i], l_in[i], send_sems, recv_sems, i, r, peer, me, blockwise, paired[i])
                cp.wait_send()
                cp.wait_recv()

    both = state["srcs"] + state["lands"]
    out = pl.pallas_call(
        body, name=name, out_shape=tuple(pltpu.HBM(a.shape, a.dtype) for a in both),
        in_specs=[HBM_SPEC] * (2 * n) + [SEM_SPEC, SEM_SPEC, ANY_SPEC], out_specs=tuple([HBM_SPEC] * (2 * n)),
        input_output_aliases={i: i for i in range(2 * n)},
        compiler_params=pltpu.CompilerParams(has_side_effects=pltpu.SideEffectType.DATAFLOW_SIDE_EFFECTING),
    )(*both, state["send"], state["recv"], after)
    return list(out[:n]), list(out[n:])


COPY_BLOCK_BYTES = 2 * 1024 * 1024


def _place_own(name, lands, srcs, me, blockwise, paired=None):
    out = []
    paired = paired or [False] * len(lands)
    for i, (land, src) in enumerate(zip(lands, srcs)):
        in_slots = src if blockwise else land
        part = _block_shape(in_slots.shape, paired[i])
        row_bytes = land.dtype.itemsize
        for extent in part[1:]:
            row_bytes *= extent
        tr = part[0]
        while tr * row_bytes > COPY_BLOCK_BYTES and tr % 16 == 0:
            tr //= 2
        tail = (0,) * (len(part) - 1)

        def body(me_ref, s_ref, l_ref, o_ref):
            o_ref[...] = s_ref[...]

        if paired[i]:
            slot_spec = pl.BlockSpec((None, tr, part[1]), lambda j, me_ref: (me_ref[0] // 2, j, me_ref[0] % 2))
        else:
            slot_spec = pl.BlockSpec((None, tr, *part[1:]), lambda j, me_ref: (me_ref[0], j, *tail))
        if blockwise:
            s_spec = slot_spec
            o_spec = pl.BlockSpec((None, tr, *part[1:]), lambda j, me_ref: (me_ref[0], j, *tail))
        else:
            s_spec = pl.BlockSpec((tr, *part[1:]), lambda j, me_ref: (j, *tail))
            o_spec = slot_spec
        out.append(pl.pallas_call(
            body, name=f"{name}_{i}",
            grid_spec=pltpu.PrefetchScalarGridSpec(
                num_scalar_prefetch=1, grid=(part[0] // tr,), in_specs=[s_spec, ANY_SPEC], out_specs=o_spec),
            out_shape=jax.ShapeDtypeStruct(land.shape, land.dtype), input_output_aliases={2: 0},
            compiler_params=_params("parallel"),
        )(me, src, land))
    return out


def _adamw_math(w, g, m, v):
    m = ADAM_B1 * m + (1.0 - ADAM_B1) * g
    v = ADAM_B2 * v + (1.0 - ADAM_B2) * (g * g)
    m_hat = m / (1.0 - ADAM_B1 ** ADAM_STEP)
    v_hat = v / (1.0 - ADAM_B2 ** ADAM_STEP)
    delta = -ADAM_LR * (m_hat / (jnp.sqrt(v_hat) + ADAM_EPS) + ADAM_WD * w)
    return delta, m, v


def _adamw(name, parts, w, m, v, *, grid, part_specs, w_spec):
    n_layers = len(parts)
    n_parts = parts[0].shape[0]

    def body(*refs):
        p_refs = refs[:n_layers]
        w_ref, m_ref, v_ref, g_ref, d_ref, nm_ref, nv_ref = refs[n_layers:]

        def total(p_ref):
            t = p_ref[0].astype(F32)
            for k in range(1, n_parts):
                t = t + p_ref[k].astype(F32)
            return t

        g = total(p_refs[0])
        for li in range(1, n_layers):
            g = jnp.where(pl.program_id(0) == li, total(p_refs[li]), g)
        delta, nm, nv = _adamw_math(w_ref[...], g, m_ref[...], v_ref[...])
        g_ref[...] = g
        d_ref[...] = delta
        nm_ref[...] = nm
        nv_ref[...] = nv

    out = jax.ShapeDtypeStruct(w.shape, F32)
    return pl.pallas_call(
        body, name=name, grid=grid, in_specs=[*part_specs, w_spec, w_spec, w_spec], out_specs=[w_spec] * 4,
        out_shape=[out] * 4, compiler_params=_params(*(("parallel",) * len(grid))),
    )(*parts, w, m, v)


def _layer_part_spec(layer, block, n_blocks, row_off=0):
    def index_map(l, i):
        ii = jnp.where(l == layer, i, jnp.where(l < layer, 0, n_blocks - 1))
        return (0, row_off + ii) + (0,) * (len(block) - 2)
    return pl.BlockSpec(block, index_map)


def _small_update(partials, triples, conv_rows):
    d = partials[-1].shape[-1]
    n_rep = len(triples)
    n_part = len(partials)
    rows = []
    for p in partials:
        rows.append(p.shape[0] * (p.shape[1] // d))
    offs = [sum(rows[:i]) for i in range(n_part)]
    total = -(-sum(rows) // 8) * 8

    def body(*refs):
        p_refs = refs[:n_part]
        wmv = refs[n_part:n_part + 3 * n_rep]
        outs = refs[n_part + 3 * n_rep:n_part + 3 * n_rep + 4 * n_rep + (n_part - n_rep)]
        buf, send_sems, recv_sems = refs[-3:]
        x, y, c = _my_place()
        me = _block_of(x, y, c)
        peers = _peers(x, y, c)
        mine = buf.at[me]
        if total > sum(rows):
            mine[sum(rows):total, :] = jnp.zeros((total - sum(rows), d), F32)
        for p_ref, off in zip(p_refs, offs):
            nr, nc = p_ref.shape[0], p_ref.shape[1] // d
            if nc == 1:
                mine[off:off + nr, :] = p_ref[...]
            else:
                for r in range(nr):
                    for q in range(nc):
                        mine[off + r * nc + q:off + r * nc + q + 1, :] = p_ref[r:r + 1, q * d:(q + 1) * d]
        sends =[pltpu.make_async_remote_copy(
            src_ref=buf.at[me], dst_ref=buf.at[me], send_sem=send_sems.at[r], recv_sem=recv_sems.at[r],
            device_id=peer, device_id_type=MESH) for r, peer in enumerate(peers)]
        for cp in sends:
            cp.start()
        for r, peer in enumerate(peers):
            pltpu.make_async_remote_copy(
                src_ref=buf.at[me], dst_ref=buf.at[_block_of(*peer)], send_sem=send_sems.at[r],
                recv_sem=recv_sems.at[r], device_id=peer, device_id_type=MESH).wait_recv()
        for cp in sends:
            cp.wait_send()
        tot = buf[0]
        for k in range(1, N_DEV):
            tot = tot + buf[k]
        buf[0] = tot
        for idx in range(n_part):
            nr, nc = p_refs[idx].shape[0], p_refs[idx].shape[1] // d
            if idx < n_rep:
                w_ref, m_ref, v_ref = wmv[3 * idx:3 * idx + 3]
                g_ref, d_ref, nm_ref, nv_ref = outs[4 * idx:4 * idx + 4]
            else:
                g_ref = outs[4 * n_rep + idx - n_rep]
            pieces = [(slice(0, nr), slice(0, d), offs[idx], nr)] if nc == 1 else [
                (slice(r, r + 1), slice(q * d, (q + 1) * d), offs[idx] + r * nc + q, 1)
                for r in range(nr) for q in range(nc)]
            for rws, cols, row, cnt in pieces:
                g = buf[0, row:row + cnt, :]
                g_ref[rws, cols] = g
                if idx < n_rep:
                    delta, nm, nv = _adamw_math(w_ref[rws, cols], g, m_ref[rws, cols], v_ref[rws, cols])
                    d_ref[rws, cols] = delta
                    nm_ref[rws, cols] = nm
                    nv_ref[rws, cols] = nv

    vm = pl.BlockSpec(memory_space=pltpu.VMEM)
    operands = list(partials)
    for t in triples:
        operands += list(t)
    out_shape = []
    for idx in range(n_rep):
        out_shape += [jax.ShapeDtypeStruct(partials[idx].shape, F32)] * 4
    for idx in range(n_rep, n_part):
        out_shape.append(jax.ShapeDtypeStruct(partials[idx].shape, F32))
    return pl.pallas_call(
        body, name="small_allreduce_adamw", in_specs=[vm] * len(operands), out_specs=[vm] * len(out_shape),
        out_shape=out_shape,
        scratch_shapes=[pltpu.VMEM((N_DEV, total, d), F32), pltpu.SemaphoreType.DMA((7,)), pltpu.SemaphoreType.DMA((7,))],
        compiler_params=pltpu.CompilerParams(vmem_limit_bytes=VMEM_LIMIT_BYTES),
    )(*operands)


def kernel(x, g_mix, w_in, b_in, conv_a, w_out_a, conv_b, conv_b_bias, ln_b_g, ln_b_b, w_out_b, b_out_b, w_pool, pool_scale, w_o, g_mlp, w_mlp1, w_mlp2, g_final, loss_target, m_g_mix, m_w_in, m_b_in, m_conv_a, m_w_out_a, m_conv_b, m_conv_b_bias, m_ln_b_g, m_ln_b_b, m_w_out_b, m_b_out_b, m_w_pool, m_pool_scale, m_w_o, m_g_mlp, m_w_mlp1, m_w_mlp2, m_g_final, v_g_mix, v_w_in, v_b_in, v_conv_a, v_w_out_a, v_conv_b, v_conv_b_bias, v_ln_b_g, v_ln_b_b, v_w_out_b, v_b_out_b, v_w_pool, v_pool_scale, v_w_o, v_g_mlp, v_w_mlp1, v_w_mlp2, v_g_final):
    _, s, d = x.shape
    n_layers = g_mix.shape[0]
    p_in = b_in.shape[1]
    ci = w_in.shape[2]
    c1 = w_mlp1.shape[2]
    rf = w_mlp2.shape[1]
    rd = w_out_a.shape[1]
    f = rf * N_DEV
    rp = rf + 3 * rd
    o_a, o_b, o_o = rf // rd, rf // rd + 1, rf // rd + 2
    gc = d // N_GROUPS
    ca_rows = 8
    tm = min(1024, s)
    tr = min(512, s)
    tx = min(256, s)
    tk = min(2048, s)
    tk_mlp = min(4096, s)
    tk_in = min(2048, s)

    me_arr = jnp.reshape(_block_of(*_my_place()), (1,)).astype(jnp.int32)

    def layer_shards(l):
        row_pack = jnp.concatenate([w_mlp2[l], w_out_a[l], w_out_b[l], w_o[l]], axis=0).astype(BF16)
        return [w_in[l].astype(BF16), w_mlp1[l].astype(BF16), row_pack, w_pool[l].astype(BF16)]

    conv_pack = jnp.concatenate(
        [conv_a, jnp.zeros((n_layers, ca_rows - K_A, rd), F32), conv_b], axis=1)
    first_shards = layer_shards(0)
    layer_pairing = [True, False, False, False]
    g_in_first, g_conv = _gather_shards([first_shards[0], conv_pack], [True, False])
    conv_full = jnp.transpose(g_conv, (1, 2, 0, 3)).reshape(n_layers, ca_rows + K_B, d)
    conv_a_f = conv_full[:, :K_A]
    conv_b_f = conv_full[:, ca_rows:]
    first_row_going = _start_copies("gather_start_row_0", first_shards[2:], g_conv, blockwise=False)
    in_flight = [_start_copies("gather_start_mlp1_0", first_shards[1:2], first_row_going["token"], blockwise=False)]
    for l in range(1, n_layers):
        in_flight.append(_start_copies(f"gather_start_{l}", layer_shards(l), in_flight[-1]["token"], blockwise=False,
                                       paired=layer_pairing))
    token = in_flight[-1]["token"][0:1, 0:1]

    xs = [x[0]]
    saved = []
    weights = []
    row2 = lambda j, i: (i, 0)
    for l in range(n_layers):
        x0 = xs[-1]
        vec = lambda a: a[l:l + 1]
        if l > 0:
            srcs, lands = _wait_copies(f"gather_wait_{l}", in_flight[l], x0, blockwise=False)
            g_in, g_1, g_row, g_pool = _place_own(f"gather_own_{l}", lands, srcs, me_arr, blockwise=False,
                                                  paired=layer_pairing)
        else:
            g_in = g_in_first
        h = _rms_fwd(f"rms_mix_{l}", x0, vec(g_mix) + token if l == 0 else vec(g_mix), tr)
        proj = _mm(
            f"proj_{l}", h, g_in, grid=(N_DEV // 2, s // tm), a_spec=pl.BlockSpec((tm, d), row2),
            b_spec=pl.BlockSpec((None, d, 2 * ci), lambda j, i: (j, 0, 0)),
            extras=(vec(b_in),), extra_specs=(pl.BlockSpec((1, 2 * ci), lambda j, i: (0, j)),),
            epilogue=lambda v, b: v + b, out_shape=jax.ShapeDtypeStruct((s, p_in), BF16),
            o_spec=pl.BlockSpec((tm, 2 * ci), lambda j, i: (i, j)), dims=NN)
        if l == 0:
            srcs, lands = _wait_copies("gather_wait_row_0", first_row_going, proj, blockwise=False)
            g_row, g_pool = _place_own("gather_own_row_0", lands, srcs, me_arr, blockwise=False)
        p_a, sw, p_c, cv, y_a, y_b, pw, merged, x1, h2 = _mixer_fwd(
            f"mix_fwd_{l}", proj, x0, conv_a_f[l], conv_b_f[l], vec(conv_b_bias), vec(ln_b_g), vec(ln_b_b),
            vec(b_out_b), vec(pool_scale), vec(g_mlp), g_row, g_pool, (o_a, o_b, o_o), d, tx)
        if l == 0:
            srcs, lands = _wait_copies("gather_wait_mlp1_0", in_flight[0], x1, blockwise=False)
            g_1, = _place_own("gather_own_mlp1_0", lands, srcs, me_arr, blockwise=False)
        weights.append((g_in, g_1, g_row, g_pool))
        a_pre = _mm(f"mlp1_{l}", h2, g_1, grid=(N_DEV // 2, s // tm), a_spec=pl.BlockSpec((tm, d), row2),
                    b_spec=pl.BlockSpec((2, d, c1), lambda j, i: (j, 0, 0)), slabs="n",
                    out_shape=jax.ShapeDtypeStruct((s, f), BF16),
                    o_spec=pl.BlockSpec((tm, 2 * c1), lambda j, i: (i, j)), dims=NN)
        x2 = _mm(f"mlp2_{l}", a_pre, g_row, grid=(1, s // tr), a_spec=pl.BlockSpec((tr, f), row2),
                 b_spec=pl.BlockSpec((N_DEV, rf, d), lambda j, i: (0, 0, 0)), prologue=_relu_sq,
                 extras=(x1,), extra_specs=(pl.BlockSpec((tr, d), row2),), epilogue=lambda v, r: v + r,
                 out_shape=jax.ShapeDtypeStruct((s, d), F32), o_spec=pl.BlockSpec((tr, d), row2), dims=NN)
        saved.append((x0, h, proj, p_a, sw, p_c, cv, y_a, y_b, pw, merged, x1, h2, a_pre))
        xs.append(x2)

    loss_part, dx, dx16, dg_final = _loss_head(xs[-1], g_final.reshape(1, d), loss_target[0], tr)
    loss = lax.psum(loss_part[0, 0], ("x", "y", "c"))

    small = [None] * n_layers
    exchanges = [None] * n_layers
    for l in reversed(range(n_layers)):
        x0, h, proj, p_a, sw, p_c, cv, y_a, y_b, pw, merged, x1, h2, a_pre = saved[l]
        g_in, g_1, g_row, g_pool = weights[l]
        vec = lambda a: a[l:l + 1]
        row_shape = jax.ShapeDtypeStruct((N_DEV, rp, d), BF16)

        def dd_grad(name, a, g, off, alias):
            return _mm(name, a, g, grid=(1, s // tk), a_spec=pl.BlockSpec((tk, d), lambda j, k: (k, 0)),
                       b_spec=pl.BlockSpec((tk, d), lambda j, k: (k, 0)), out_shape=row_shape,
                       o_spec=pl.BlockSpec((N_DEV, rd, d), lambda j, k: (0, off, 0)), dims=TN, nk=s // tk,
                       acc_shape=(d, d), alias_in=alias)

        d_a = _mm(f"d_act_{l}", dx16, g_row, grid=(N_DEV // 2, s // tm), a_spec=pl.BlockSpec((tm, d), row2),
                  b_spec=pl.BlockSpec((2, rf, d), lambda j, i: (j, 0, 0)), slabs="n",
                  extras=(a_pre,), extra_specs=(pl.BlockSpec((tm, 2 * rf), lambda j, i: (i, j)),),
                  epilogue=lambda v, a: v * (2.0 * jnp.maximum(a.astype(F32), 0.0)),
                  out_shape=jax.ShapeDtypeStruct((s, f), BF16),
                  o_spec=pl.BlockSpec((tm, 2 * rf), lambda j, i: (i, j)), dims=NT)
        dg_row = _mm(f"dw_mlp2_{l}", a_pre, dx16, grid=(N_DEV, s // tk_mlp),
                     a_spec=pl.BlockSpec((tk_mlp, rf), lambda j, k: (k, j)),
                     b_spec=pl.BlockSpec((tk_mlp, d), lambda j, k: (k, 0)), prologue=_relu_sq, out_shape=row_shape,
                     o_spec=pl.BlockSpec((None, rf, d), lambda j, k: (j, 0, 0)), dims=TN, nk=s // tk_mlp,
                     acc_shape=(rf, d))
        dg_1 = _mm(f"dw_mlp1_{l}", h2, d_a, grid=(N_DEV, s // tk_mlp),
                   a_spec=pl.BlockSpec((tk_mlp, d), lambda j, k: (k, 0)),
                   b_spec=pl.BlockSpec((tk_mlp, c1), lambda j, k: (k, j)),
                   out_shape=jax.ShapeDtypeStruct((N_DEV, d, c1), BF16),
                   o_spec=pl.BlockSpec((None, d, c1), lambda j, k: (j, 0, 0)), dims=TN, nk=s // tk_mlp,
                   acc_shape=(d, c1))
        mlp1_going = _start_copies(f"grads_start_mlp1_{l}", [dg_1], vec(g_mlp), blockwise=True)
        stream = [jax.ShapeDtypeStruct((s, d), F32), jax.ShapeDtypeStruct((s, d), BF16), jax.ShapeDtypeStruct((1, d), F32)]
        dx, dx16, dg_mlp = _mm(
            f"d_h2_{l}", d_a, g_1, grid=(1, s // tr), a_spec=pl.BlockSpec((tr, f), row2),
            b_spec=pl.BlockSpec((N_DEV, d, c1), lambda j, i: (0, 0, 0), pipeline_mode=pl.Buffered(1)), slabs="k",
            extras=(x1, vec(g_mlp), dx),
            extra_specs=(pl.BlockSpec((tr, d), row2), pl.BlockSpec((1, d), lambda j, i: (0, 0)),
                         pl.BlockSpec((tr, d), row2)),
            out_shape=stream, o_spec=[pl.BlockSpec((tr, d), row2), pl.BlockSpec((tr, d), row2),
                                      pl.BlockSpec((1, d), lambda j, i: (0, 0))],
            dims=NT, rms_bwd=(1, s // tr), after=mlp1_going["token"])
        d_ya, d_yb, d_pw, d_gates, d_pa, d_sw, d_pc, d_bout, d_pscale = _merge_bwd(
            f"merge_bwd_{l}", dx16, proj, y_a, y_b, pw, vec(pool_scale), g_row, g_pool, (o_a, o_b, o_o), d, tx)
        dg_row = dd_grad(f"dw_o_{l}", merged, dx16, o_o, dg_row)
        dg_row = dd_grad(f"dw_out_a_{l}", p_a, d_ya, o_a, dg_row)
        dg_row = dd_grad(f"dw_out_b_{l}", sw, d_yb, o_b, dg_row)
        dg_pool = _pool_wgrad(f"dw_pool_{l}", p_c, d_pw, d, tk)
        rest_going = _start_copies(f"grads_start_rest_{l}", [dg_row, dg_pool], vec(g_mlp), blockwise=True)
        d_proj, d_bin, d_ca, d_cb, d_cbb, d_lng, d_lnb = _mix_pre_bwd(
            f"mix_bwd_{l}", proj, cv, d_pa, d_sw, d_pc, d_gates, conv_a_f[l], conv_b_f[l],
            vec(ln_b_g) + rest_going["token"][0:1, 0:1], vec(ln_b_b), d, tx)
        dg_in = _mm(f"dw_in_{l}", h, d_proj, grid=(N_DEV // 2, s // tk_in),
                    a_spec=pl.BlockSpec((tk_in, d), lambda j, k: (k, 0)),
                    b_spec=pl.BlockSpec((tk_in, 2 * ci), lambda j, k: (k, j)),
                    out_shape=jax.ShapeDtypeStruct((N_DEV // 2, d, 2 * ci), BF16),
                    o_spec=pl.BlockSpec((None, d, 2 * ci), lambda j, k: (j, 0, 0)), dims=TN, nk=s // tk_in,
                    acc_shape=(d, 2 * ci), after=rest_going["token"])
        in_going = _start_copies(f"grads_start_in_{l}", [dg_in], vec(g_mix), blockwise=True, paired=[True])
        rows_ik = lambda i, k: (i, 0)
        dx, dx16, dg_mix = _mm(
            f"d_h_{l}", d_proj, g_in, grid=(s // tr, N_DEV // 2), a_spec=pl.BlockSpec((tr, 2 * ci), lambda i, k: (i, k)),
            b_spec=pl.BlockSpec((None, d, 2 * ci), lambda i, k: (k, 0, 0)),
            extras=(x0, vec(g_mix), dx),
            extra_specs=(pl.BlockSpec((tr, d), rows_ik), pl.BlockSpec((1, d), lambda i, k: (0, 0)),
                         pl.BlockSpec((tr, d), rows_ik)),
            out_shape=stream, o_spec=[pl.BlockSpec((tr, d), rows_ik), pl.BlockSpec((tr, d), rows_ik),
                                      pl.BlockSpec((1, d), lambda i, k: (0, 0))],
            dims=NT, nk=N_DEV // 2, acc_shape=(tr, d), rms_bwd=(0, s // tr), after=in_going["token"])
        small[l] = (dg_mix, d_bin, d_cbb, d_lng, d_lnb, d_bout, d_pscale, dg_mlp, d_ca, d_cb)
        exchanges[l] = (in_going, mlp1_going, rest_going)

    grad_x = dx[None]

    names = ("g_mix", "b_in", "conv_b_bias", "ln_b_g", "ln_b_b", "b_out_b", "pool_scale", "g_mlp")
    given = dict(g_mix=(g_mix, m_g_mix, v_g_mix), b_in=(b_in, m_b_in, v_b_in),
                 conv_b_bias=(conv_b_bias, m_conv_b_bias, v_conv_b_bias), ln_b_g=(ln_b_g, m_ln_b_g, v_ln_b_g),
                 ln_b_b=(ln_b_b, m_ln_b_b, v_ln_b_b), b_out_b=(b_out_b, m_b_out_b, v_b_out_b),
                 pool_scale=(pool_scale, m_pool_scale, v_pool_scale), g_mlp=(g_mlp, m_g_mlp, v_g_mlp))
    partials, triples = [], []
    for i, nm in enumerate(names):
        partials.append(jnp.concatenate([small[l][i] for l in range(n_layers)], axis=0))
        triples.append(given[nm])
    partials.append(dg_final)
    triples.append(tuple(a.reshape(1, d) for a in (g_final, m_g_final, v_g_final)))
    partials.append(jnp.concatenate([small[l][8] for l in range(n_layers)], axis=0))
    partials.append(jnp.concatenate([small[l][9] for l in range(n_layers)], axis=0))
    outs = _small_update(partials, triples, 2)
    rep = {nm: outs[4 * i:4 * i + 4] for i, nm in enumerate(names)}
    rep["g_final"] = [a.reshape(d) for a in outs[4 * len(names):4 * len(names) + 4]]
    me = _block_of(*_my_place())
    gca = lax.dynamic_slice_in_dim(outs[-2].reshape(n_layers, K_A, d), me * rd, rd, axis=2)
    gcb = lax.dynamic_slice_in_dim(outs[-1].reshape(n_layers, K_B, d), me * rd, rd, axis=2)

    r_in, r_1, r_row, r_pool = [], [], [], []
    for l in reversed(range(n_layers)):
        in_going, mlp1_going, rest_going = exchanges[l]
        srcs_m, lands_m = _wait_copies(f"grads_wait_mlp1_{l}", mlp1_going, outs[0], blockwise=True)
        srcs_r, lands_r = _wait_copies(f"grads_wait_rest_{l}", rest_going, outs[0], blockwise=True)
        srcs_i, lands_i = _wait_copies(f"grads_wait_in_{l}", in_going, outs[0], blockwise=True)
        got = _place_own(f"grads_own_{l}", lands_i + lands_m + lands_r, srcs_i + srcs_m + srcs_r, me_arr, blockwise=True,
                         paired=layer_pairing)
        for lst, arr in zip((r_in, r_1, r_row, r_pool), got):
            lst.insert(0, arr)
    tb = min(256, d)
    layers = range(n_layers)
    res = {}
    res["w_in"] = _adamw("adamw_w_in", r_in, w_in, m_w_in, v_w_in, grid=(n_layers, d // tb),
                         part_specs=[_layer_part_spec(li, (N_DEV, tb, ci), d // tb) for li in layers],
                         w_spec=pl.BlockSpec((None, tb, ci), lambda l, i: (l, i, 0)))
    res["w_mlp1"] = _adamw("adamw_w_mlp1", r_1, w_mlp1, m_w_mlp1, v_w_mlp1, grid=(n_layers, d // tb),
                           part_specs=[_layer_part_spec(li, (N_DEV, tb, c1), d // tb) for li in layers],
                           w_spec=pl.BlockSpec((None, tb, c1), lambda l, i: (l, i, 0)))
    tf = min(256, rf)
    res["w_mlp2"] = _adamw("adamw_w_mlp2", r_row, w_mlp2, m_w_mlp2, v_w_mlp2, grid=(n_layers, rf // tf),
                           part_specs=[_layer_part_spec(li, (N_DEV, tf, d), rf // tf) for li in layers],
                           w_spec=pl.BlockSpec((None, tf, d), lambda l, i: (l, i, 0)))
    for nm, off, trip in (("w_out_a", o_a, (w_out_a, m_w_out_a, v_w_out_a)),
                          ("w_out_b", o_b, (w_out_b, m_w_out_b, v_w_out_b)), ("w_o", o_o, (w_o, m_w_o, v_w_o))):
        res[nm] = _adamw(f"adamw_{nm}", r_row, *trip, grid=(n_layers, 1),
                         part_specs=[_layer_part_spec(li, (N_DEV, rd, d), 1, row_off=off) for li in layers],
                         w_spec=pl.BlockSpec((None, rd, d), lambda l, i: (l, 0, 0)))
    res["w_pool"] = _adamw("adamw_w_pool", r_pool, w_pool, m_w_pool, v_w_pool, grid=(n_layers, 1),
                           part_specs=[_layer_part_spec(li, (N_DEV, N_GROUPS, gc // N_DEV, gc), 1) for li in layers],
                           w_spec=pl.BlockSpec((None, N_GROUPS, gc // N_DEV, gc), lambda l, i: (l, 0, 0, 0)))
    whole3 = lambda: (0, 0, 0)
    res["conv_a"] = _adamw("adamw_conv_a", [gca[None]], conv_a, m_conv_a, v_conv_a, grid=(),
                           part_specs=[pl.BlockSpec((1, n_layers, K_A, rd), lambda: (0, 0, 0, 0))],
                           w_spec=pl.BlockSpec((n_layers, K_A, rd), whole3))
    res["conv_b"] = _adamw("adamw_conv_b", [gcb[None]], conv_b, m_conv_b, v_conv_b, grid=(),
                           part_specs=[pl.BlockSpec((1, n_layers, K_B, rd), lambda: (0, 0, 0, 0))],
                           w_spec=pl.BlockSpec((n_layers, K_B, rd), whole3))
    res.update(rep)

    order = ("g_mix", "w_in", "b_in", "conv_a", "w_out_a", "conv_b", "conv_b_bias", "ln_b_g", "ln_b_b", "w_out_b",
             "b_out_b", "w_pool", "pool_scale", "w_o", "g_mlp", "w_mlp1", "w_mlp2", "g_final")
    out = [loss, grad_x]
    for kind in range(4):
        out += [res[nm][kind] for nm in order]
    return tuple(out)
```

```python
import jax
import jax.numpy as jnp
from jax import lax
from jax.experimental import pallas as pl
from jax.experimental.pallas import tpu as pltpu

F32 = jnp.float32
BF16 = jnp.bfloat16
MESH = pl.DeviceIdType.MESH

N_DEV = 8
EPS = 1e-6
K_A = 3
K_B = 31
POOL_WINDOWS = (2, 4, 8, 16)
N_GROUPS = len(POOL_WINDOWS)
HALO = 32
CHUNK = 16
SUBLANES = 8
TAP_GROUP = 4
ADAM_LR, ADAM_B1, ADAM_B2, ADAM_EPS, ADAM_WD, ADAM_STEP = 0.001, 0.9, 0.999, 1e-08, 0.01, 10
VMEM_LIMIT_BYTES = 60 * 1024 * 1024

NN = (((1,), (0,)), ((), ()))
NT = (((1,), (1,)), ((), ()))
TN = (((0,), (0,)), ((), ()))


def _params(*sem):
    return pltpu.CompilerParams(dimension_semantics=sem, vmem_limit_bytes=VMEM_LIMIT_BYTES)


def _sigmoid(v):
    return 1.0 / (1.0 + jnp.exp(-v))


def _mm(name, a, b, *, grid, a_spec, b_spec, out_shape, o_spec, dims, nk=1, acc_shape=None,
        extras=(), extra_specs=(), prologue=None, epilogue=None, alias_in=None, slabs=None, after=None,
        rms_bwd=None):
    n_extra = len(extras)
    has_alias = alias_in is not None
    n_unread = (1 if has_alias else 0) + (1 if after is not None else 0)

    def body(*refs):
        a_ref, b_ref = refs[0], refs[1]
        ex = refs[2:2 + n_extra]
        o_ref = refs[2 + n_extra + n_unread]
        av = a_ref[...]
        if prologue is not None:
            av = prologue(av)
        av = av.astype(BF16)

        def finish_rms(val):
            x_ref, g_ref, dr_ref = ex
            dx_ref, dx16_ref, dg_ref = refs[2 + n_extra + n_unread:5 + n_extra + n_unread]
            acc_g = refs[-1]
            row_axis, n_rows = rms_bwd
            ri = pl.program_id(row_axis)
            xv = x_ref[...]
            r = lax.rsqrt(jnp.mean(xv * xv, axis=-1, keepdims=True) + EPS)
            xh = xv * r
            part = _colsum8(val * xh)

            @pl.when(ri == 0)
            def _():
                acc_g[...] = part

            @pl.when(ri > 0)
            def _():
                acc_g[...] += part

            dxh = val * g_ref[...]
            dx = r * (dxh - xh * jnp.mean(dxh * xh, axis=-1, keepdims=True)) + dr_ref[...]
            dx_ref[...] = dx
            dx16_ref[...] = dx.astype(BF16)

            @pl.when(ri == n_rows - 1)
            def _():
                dg_ref[...] = jnp.sum(acc_g[...], axis=0, keepdims=True)

        def finish(val, cols=None):
            if rms_bwd is not None:
                return finish_rms(val)
            if epilogue is not None:
                val = epilogue(val, *[e[...] if cols is None else e[:, cols] for e in ex])
            if cols is None:
                o_ref[...] = val.astype(o_ref.dtype).reshape(o_ref.shape)
            else:
                o_ref[:, cols] = val.astype(o_ref.dtype)

        if slabs == "n":
            for q in range(b_ref.shape[0]):
                pq = lax.dot_general(av, b_ref[q].astype(BF16), dims, preferred_element_type=F32)
                finish(pq, slice(q * pq.shape[1], (q + 1) * pq.shape[1]))
            return
        if slabs == "k":
            kc = av.shape[1] // b_ref.shape[0]
            p = None
            for q in range(b_ref.shape[0]):
                pq = lax.dot_general(av[:, q * kc:(q + 1) * kc], b_ref[q].astype(BF16), dims,
                                     preferred_element_type=F32)
                p = pq if p is None else p + pq
        else:
            bv = b_ref[...]
            bv = bv.reshape((-1, bv.shape[-1])).astype(BF16)
            p = lax.dot_general(av, bv, dims, preferred_element_type=F32)

        if nk == 1:
            finish(p)
        else:
            acc = refs[-2] if rms_bwd is not None else refs[-1]
            k = pl.program_id(len(grid) - 1)

            @pl.when(k == 0)
            def _():
                acc[...] = p

            @pl.when(k > 0)
            def _():
                acc[...] += p

            @pl.when(k == nk - 1)
            def _():
                finish(acc[...])

    in_specs = [a_spec, b_spec, *extra_specs]
    operands = [a, b, *extras]
    aliases = {}
    if has_alias:
        in_specs.append(pl.BlockSpec(memory_space=pl.ANY))
        operands.append(alias_in)
        aliases = {len(operands) - 1: 0}
    if after is not None:
        in_specs.append(pl.BlockSpec(memory_space=pl.ANY))
        operands.append(after)
    sem = ("parallel",) * (len(grid) - 1) + (("arbitrary",) if nk > 1 else ("parallel",))
    scratch = [pltpu.VMEM(acc_shape, F32)] if nk > 1 else []
    if rms_bwd is not None:
        sem = ("arbitrary",) * len(grid)
        scratch.append(pltpu.VMEM((8, extras[0].shape[-1]), F32))
    return pl.pallas_call(
        body, name=name, grid=grid, in_specs=in_specs, out_specs=o_spec, out_shape=out_shape,
        scratch_shapes=scratch, input_output_aliases=aliases, compiler_params=_params(*sem),
    )(*operands)


def _relu_sq(v):
    r = jnp.maximum(v, 0)
    return r * r


def _rms_fwd(name, x, g, tm):
    s, d = x.shape

    def body(x_ref, g_ref, h_ref):
        xv = x_ref[...]
        r = lax.rsqrt(jnp.mean(xv * xv, axis=-1, keepdims=True) + EPS)
        h_ref[...] = (xv * r * g_ref[...]).astype(h_ref.dtype)

    return pl.pallas_call(
        body, name=name, grid=(s // tm,),
        in_specs=[pl.BlockSpec((tm, d), lambda i: (i, 0)), pl.BlockSpec((1, d), lambda i: (0, 0))],
        out_specs=pl.BlockSpec((tm, d), lambda i: (i, 0)),
        out_shape=jax.ShapeDtypeStruct((s, d), BF16), compiler_params=_params("parallel"),
    )(x, g)


def _colsum8(v):
    return jnp.sum(v.reshape(v.shape[0] // 8, 8, v.shape[1]), axis=0)


def _loss_head(x, g, target, tm):
    s, d = x.shape
    n = s // tm

    def body(x_ref, g_ref, t_ref, loss_ref, dx_ref, dx16_ref, dg_ref, acc_l, acc_g):
        i = pl.program_id(0)
        xv = x_ref[...]
        r = lax.rsqrt(jnp.mean(xv * xv, axis=-1, keepdims=True) + EPS)
        xh = xv * r
        err = xh * g_ref[...] - t_ref[...]
        dy = err * (1.0 / d)
        lpart = _colsum8(err * err)
        gpart = _colsum8(dy * xh)

        @pl.when(i == 0)
        def _():
            acc_l[...] = lpart
            acc_g[...] = gpart

        @pl.when(i > 0)
        def _():
            acc_l[...] += lpart
            acc_g[...] += gpart

        dxh = dy * g_ref[...]
        dx = r * (dxh - xh * jnp.mean(dxh * xh, axis=-1, keepdims=True))
        dx_ref[...] = dx
        dx16_ref[...] = dx.astype(BF16)

        @pl.when(i == n - 1)
        def _():
            loss_ref[...] = (0.5 / d) * jnp.sum(jnp.sum(acc_l[...], axis=0, keepdims=True), axis=1, keepdims=True)
            dg_ref[...] = jnp.sum(acc_g[...], axis=0, keepdims=True)

    return pl.pallas_call(
        body, name="loss_head", grid=(n,),
        in_specs=[pl.BlockSpec((tm, d), lambda i: (i, 0)), pl.BlockSpec((1, d), lambda i: (0, 0)),
                  pl.BlockSpec((tm, d), lambda i: (i, 0))],
        out_specs=[pl.BlockSpec((1, 1), lambda i: (0, 0)), pl.BlockSpec((tm, d), lambda i: (i, 0)),
                   pl.BlockSpec((tm, d), lambda i: (i, 0)), pl.BlockSpec((1, d), lambda i: (0, 0))],
        out_shape=[jax.ShapeDtypeStruct((1, 1), F32), jax.ShapeDtypeStruct((s, d), F32),
                   jax.ShapeDtypeStruct((s, d), BF16), jax.ShapeDtypeStruct((1, d), F32)],
        scratch_shapes=[pltpu.VMEM((8, d), F32), pltpu.VMEM((8, d), F32)], compiler_params=_params("arbitrary"),
    )(x, g, target)


def _sec(ref, n, d):
    return ref[:, n * d:(n + 1) * d].astype(F32)


def _fill_shifts(sh, ext):
    rows = ext.shape[0] - SUBLANES
    for b in range(1, SUBLANES):
        sh[b - 1, 0:rows, :] = ext[b:b + rows, :]


def _shifted(sh, ext, off, n):
    b = off % SUBLANES
    if b == 0:
        return ext[off:off + n, :]
    return sh[b - 1, off - b:off - b + n, :]


def _pool_count(row0, rows, window):
    t = row0 + lax.broadcasted_iota(jnp.int32, (rows, 1), 0)
    return jnp.minimum(t + 1, window).astype(F32)


def _group_weight(w_ref, gi, gc):
    return w_ref[:, gi].reshape(gc, gc)


def _mixer_fwd(name, proj, x0, conv_a, conv_b, conv_b_bias, ln_g, ln_b, b_out_b, pool_scale, g_next, g_row, g_pool, offs,
               d, tm):
    s = proj.shape[0]
    n = s // tm
    gc = d // N_GROUPS
    hb = tm // HALO
    rd = d // N_DEV
    o_a, o_b, o_o = offs

    def body(pj_ref, hp_ref, x0_ref, ca_ref, cb_ref, cbb_ref, lng_ref, lnb_ref, bo_ref, sc_ref, gn_ref, wa_ref, wb_ref,
             wo_ref, wp_ref, pa_ref, sw_ref, pc_ref, cv_ref, ya_ref, yb_ref, pw_ref, mg_ref, x1_ref, h2_ref,
             eua, eub, euc, sh):
        i = pl.program_id(0)
        keep = (i > 0).astype(F32)
        eua[0:HALO, :] = _sec(hp_ref, 1, d) * _sec(hp_ref, 2, d) * keep
        eub[0:HALO, :] = _sec(hp_ref, 3, d) * _sigmoid(_sec(hp_ref, 4, d)) * keep
        euc[0:HALO, :] = _sec(hp_ref, 5, d) * keep
        eua[HALO:HALO + tm, :] = _sec(pj_ref, 1, d) * _sec(pj_ref, 2, d)
        eub[HALO:HALO + tm, :] = _sec(pj_ref, 3, d) * _sigmoid(_sec(pj_ref, 4, d))
        euc[HALO:HALO + tm, :] = _sec(pj_ref, 5, d)
        _fill_shifts(sh, eub)
        for c in range(tm // CHUNK):
            r0 = c * CHUNK
            z = jnp.zeros((CHUNK, d), F32)
            for k in range(K_A):
                z = z + ca_ref[k:k + 1, :] * eua[HALO + r0 - (K_A - 1) + k:HALO + r0 - (K_A - 1) + k + CHUNK, :]
            pa_ref[r0:r0 + CHUNK, :] = (pj_ref[r0:r0 + CHUNK, 0:d].astype(F32) * z).astype(pa_ref.dtype)
            cv = jnp.zeros((CHUNK, d), F32) + cbb_ref[...]
            for k in range(K_B):
                cv = cv + cb_ref[k:k + 1, :] * _shifted(sh, eub, HALO + r0 - (K_B - 1) + k, CHUNK)
            cv_ref[r0:r0 + CHUNK, :] = cv.astype(cv_ref.dtype)
        cvv = cv_ref[...].astype(F32)
        mu = jnp.mean(cvv, axis=-1, keepdims=True)
        xc = cvv - mu
        xh = xc * lax.rsqrt(jnp.mean(xc * xc, axis=-1, keepdims=True) + EPS)
        ln = xh * lng_ref[...] + lnb_ref[...]
        sw_ref[...] = (ln * _sigmoid(ln)).astype(sw_ref.dtype)
        for gi, w in enumerate(POOL_WINDOWS):
            cols = slice(gi * gc, (gi + 1) * gc)
            tot = euc[HALO:HALO + tm, cols]
            for k in range(1, w):
                tot = tot + euc[HALO - k:HALO - k + tm, cols]
            cnt = _pool_count(i * tm, tm, w)
            pc_ref[:, cols] = (tot / cnt - euc[HALO:HALO + tm, cols]).astype(pc_ref.dtype)
        ya_ref[...] = jnp.dot(pa_ref[...], wa_ref[...].reshape(d, d), preferred_element_type=F32).astype(ya_ref.dtype)
        yb_ref[...] = (jnp.dot(sw_ref[...], wb_ref[...].reshape(d, d), preferred_element_type=F32)
                       + bo_ref[...]).astype(yb_ref.dtype)
        for gi in range(N_GROUPS):
            cols = slice(gi * gc, (gi + 1) * gc)
            pw_ref[:, cols] = jnp.dot(pc_ref[:, cols], _group_weight(wp_ref, gi, gc),
                                      preferred_element_type=F32).astype(pw_ref.dtype)
        m = _sigmoid(_sec(pj_ref, 6, d)) * ya_ref[...].astype(F32)
        m = m + _sigmoid(_sec(pj_ref, 7, d)) * yb_ref[...].astype(F32)
        m = m + _sigmoid(_sec(pj_ref, 8, d)) * (pw_ref[...].astype(F32) * sc_ref[...])
        mg_ref[...] = m.astype(mg_ref.dtype)
        x1 = x0_ref[...] + jnp.dot(mg_ref[...], wo_ref[...].reshape(d, d), preferred_element_type=F32)
        x1_ref[...] = x1
        h2_ref[...] = (x1 * lax.rsqrt(jnp.mean(x1 * x1, axis=-1, keepdims=True) + EPS) * gn_ref[...]).astype(h2_ref.dtype)

    row = lambda i: (i, 0)
    fixed = lambda i: (0, 0)
    act = jax.ShapeDtypeStruct((s, d), BF16)

    def dd_weight(off):
        return pl.BlockSpec((N_DEV, rd, d), lambda i: (0, off, 0), pipeline_mode=pl.Buffered(1))

    return pl.pallas_call(
        body, name=name, grid=(n,),
        in_specs=[pl.BlockSpec((tm, 9 * d), row),
                  pl.BlockSpec((HALO, 6 * d), lambda i: (jnp.maximum(i * hb - 1, 0), 0)),
                  pl.BlockSpec((tm, d), row),
                  pl.BlockSpec((K_A, d), fixed), pl.BlockSpec((K_B, d), fixed), pl.BlockSpec((1, d), fixed),
                  pl.BlockSpec((1, d), fixed), pl.BlockSpec((1, d), fixed), pl.BlockSpec((1, d), fixed),
                  pl.BlockSpec((1, d), fixed), pl.BlockSpec((1, d), fixed), dd_weight(o_a), dd_weight(o_b), dd_weight(o_o),
                  pl.BlockSpec((N_DEV, N_GROUPS, gc // N_DEV, gc), lambda i: (0, 0, 0, 0),
                               pipeline_mode=pl.Buffered(1))],
        out_specs=[pl.BlockSpec((tm, d), row)] * 10,
        out_shape=[act] * 8 + [jax.ShapeDtypeStruct((s, d), F32), act],
        scratch_shapes=[pltpu.VMEM((tm + HALO, d), F32)] * 3 + [pltpu.VMEM((SUBLANES - 1, tm + HALO, d), F32)],
        compiler_params=_params("parallel"),
    )(proj, proj, x0, conv_a, conv_b, conv_b_bias, ln_g, ln_b, b_out_b, pool_scale, g_next, g_row, g_row, g_row,
      g_pool)


def _merge_bwd(name, dx16, proj, ya, yb, pw, pool_scale, g_row, g_pool, offs, d, tm):
    s = proj.shape[0]
    n = s // tm
    gc = d // N_GROUPS
    rd = d // N_DEV
    o_a, o_b, o_o = offs

    def body(dx_ref, g_ref, ya_ref, yb_ref, pw_ref, sc_ref, wa_ref, wb_ref, wo_ref, wp_ref,
             dya_ref, dyb_ref, dpw_ref, dg_ref, dpa_ref, dsw_ref, dpc_ref, dbo_ref, dsc_ref, acc_b, acc_s):
        i = pl.program_id(0)
        dmv = lax.dot_general(dx_ref[...], wo_ref[...].reshape(d, d), NT,
                              preferred_element_type=F32).astype(BF16).astype(F32)
        scale = sc_ref[...]
        g0 = _sigmoid(_sec(g_ref, 0, d))
        dya_ref[...] = (dmv * g0).astype(dya_ref.dtype)
        dg_ref[:, 0:d] = (dmv * ya_ref[...].astype(F32) * g0 * (1.0 - g0)).astype(dg_ref.dtype)
        g1 = _sigmoid(_sec(g_ref, 1, d))
        dyb = dmv * g1
        dyb_ref[...] = dyb.astype(dyb_ref.dtype)
        dg_ref[:, d:2 * d] = (dmv * yb_ref[...].astype(F32) * g1 * (1.0 - g1)).astype(dg_ref.dtype)
        g2 = _sigmoid(_sec(g_ref, 2, d))
        pwv = pw_ref[...].astype(F32)
        dyc = dmv * g2
        dpw_ref[...] = (dyc * scale).astype(dpw_ref.dtype)
        dg_ref[:, 2 * d:3 * d] = (dmv * (pwv * scale) * g2 * (1.0 - g2)).astype(dg_ref.dtype)
        pb = _colsum8(dyb)
        ps = _colsum8(dyc * pwv)

        @pl.when(i == 0)
        def _():
            acc_b[...] = pb
            acc_s[...] = ps

        @pl.when(i > 0)
        def _():
            acc_b[...] += pb
            acc_s[...] += ps

        dpa_ref[...] = lax.dot_general(dya_ref[...], wa_ref[...].reshape(d, d), NT,
                                       preferred_element_type=F32).astype(dpa_ref.dtype)
        dsw_ref[...] = lax.dot_general(dyb_ref[...], wb_ref[...].reshape(d, d), NT,
                                       preferred_element_type=F32).astype(dsw_ref.dtype)
        for gi in range(N_GROUPS):
            cols = slice(gi * gc, (gi + 1) * gc)
            dpc_ref[:, cols] = lax.dot_general(dpw_ref[:, cols], _group_weight(wp_ref, gi, gc), NT,
                                               preferred_element_type=F32).astype(dpc_ref.dtype)

        @pl.when(i == n - 1)
        def _():
            dbo_ref[...] = jnp.sum(acc_b[...], axis=0, keepdims=True)
            dsc_ref[...] = jnp.sum(acc_s[...], axis=0, keepdims=True)

    row = lambda i: (i, 0)
    fixed = lambda i: (0, 0)
    act = jax.ShapeDtypeStruct((s, d), BF16)
    vec = jax.ShapeDtypeStruct((1, d), F32)

    def dd_weight(off):
        return pl.BlockSpec((N_DEV, rd, d), lambda i: (0, off, 0), pipeline_mode=pl.Buffered(1))

    return pl.pallas_call(
        body, name=name, grid=(n,),
        in_specs=[pl.BlockSpec((tm, d), row), pl.BlockSpec((tm, 3 * d), lambda i: (i, 2)), pl.BlockSpec((tm, d), row),
                  pl.BlockSpec((tm, d), row), pl.BlockSpec((tm, d), row), pl.BlockSpec((1, d), fixed),
                  dd_weight(o_a), dd_weight(o_b), dd_weight(o_o),
                  pl.BlockSpec((N_DEV, N_GROUPS, gc // N_DEV, gc), lambda i: (0, 0, 0, 0),
                               pipeline_mode=pl.Buffered(1))],
        out_specs=[pl.BlockSpec((tm, d), row)] * 3 + [pl.BlockSpec((tm, 3 * d), row)] + [pl.BlockSpec((tm, d), row)] * 3
                  + [pl.BlockSpec((1, d), fixed)] * 2,
        out_shape=[act, act, act, jax.ShapeDtypeStruct((s, 3 * d), BF16), act, act, act, vec, vec],
        scratch_shapes=[pltpu.VMEM((8, d), F32)] * 2, compiler_params=_params("arbitrary"),
    )(dx16, proj, ya, yb, pw, pool_scale, g_row, g_row, g_row, g_pool)


def _mix_pre_bwd(name, proj, cv, dpa, dsw, dpc, dgates, conv_a, conv_b, ln_g, ln_b, d, tm):
    s = proj.shape[0]
    n = s // tm
    gc = d // N_GROUPS
    hb = tm // HALO
    last_halo = s // HALO - 1
    te = tm + HALO

    def ln_bwd(cvv, dswv, lng, lnb):
        mu = jnp.mean(cvv, axis=-1, keepdims=True)
        xc = cvv - mu
        rstd = lax.rsqrt(jnp.mean(xc * xc, axis=-1, keepdims=True) + EPS)
        xh = xc * rstd
        ln = xh * lng + lnb
        sg = _sigmoid(ln)
        dln = dswv * (sg * (1.0 + ln * (1.0 - sg)))
        dxh = dln * lng
        dcv = rstd * (dxh - jnp.mean(dxh, axis=-1, keepdims=True) - xh * jnp.mean(dxh * xh, axis=-1, keepdims=True))
        return dcv, dln, xh

    def body(pj_ref, hp_ref, hf_ref, cv_ref, cvf_ref, dpa_ref, dpaf_ref, dsw_ref, dswf_ref, dpc_ref, dpcf_ref, dgt_ref,
             ca_ref, cb_ref, lng_ref, lnb_ref,
             dpj_ref, dbin_ref, dca_ref, dcb_ref, dcbb_ref, dlng_ref, dlnb_ref,
             eua, eub, edz, edcv, eq, sh, dub_s, acc_bin, acc_ca, acc_cb, acc_v):
        i = pl.program_id(0)
        keep_p = (i > 0).astype(F32)
        keep_f = (i < n - 1).astype(F32)

        @pl.when(i == 0)
        def _():
            acc_bin[...] = jnp.zeros_like(acc_bin)
            acc_ca[...] = jnp.zeros_like(acc_ca)
            acc_cb[...] = jnp.zeros_like(acc_cb)
            acc_v[...] = jnp.zeros_like(acc_v)

        eua[0:HALO, :] = _sec(hp_ref, 1, d) * _sec(hp_ref, 2, d) * keep_p
        eua[HALO:te, :] = _sec(pj_ref, 1, d) * _sec(pj_ref, 2, d)
        eub[HALO:te, :] = _sec(pj_ref, 3, d) * _sigmoid(_sec(pj_ref, 4, d))
        edz[0:tm, :] = dpa_ref[...].astype(F32) * _sec(pj_ref, 0, d)
        edz[tm:te, :] = dpaf_ref[...].astype(F32) * _sec(hf_ref, 0, d) * keep_f
        dcv, dln, xh = ln_bwd(cv_ref[...].astype(F32), dsw_ref[...].astype(F32), lng_ref[...], lnb_ref[...])
        edcv[0:tm, :] = dcv
        acc_v[0:8, :] += _colsum8(dcv)
        acc_v[8:16, :] += _colsum8(dln * xh)
        acc_v[16:24, :] += _colsum8(dln)
        dcvf, _, _ = ln_bwd(cvf_ref[...].astype(F32), dswf_ref[...].astype(F32), lng_ref[...], lnb_ref[...])
        edcv[tm:te, :] = dcvf * keep_f
        for gi, w in enumerate(POOL_WINDOWS):
            cols = slice(gi * gc, (gi + 1) * gc)
            eq[0:tm, cols] = dpc_ref[:, cols].astype(F32) / _pool_count(i * tm, tm, w)
            eq[tm:te, cols] = dpcf_ref[:, cols].astype(F32) / _pool_count((i + 1) * tm, HALO, w) * keep_f

        def put(sec_idx, r0, val):
            dpj_ref[r0:r0 + CHUNK, sec_idx * d:(sec_idx + 1) * d] = val.astype(dpj_ref.dtype)
            acc_bin[:, sec_idx * d:(sec_idx + 1) * d] += _colsum8(val)

        _fill_shifts(sh, edcv)
        for k0 in range(0, K_B, TAP_GROUP):
            taps = range(k0, min(k0 + TAP_GROUP, K_B))
            a = {k: jnp.zeros((8, d), F32) for k in taps}
            for c in range(tm // CHUNK):
                r0 = c * CHUNK
                ub = eub[HALO + r0:HALO + r0 + CHUNK, :]
                part = None
                for k in taps:
                    t = _shifted(sh, edcv, r0 + (K_B - 1) - k, CHUNK)
                    part = cb_ref[k:k + 1, :] * t if part is None else part + cb_ref[k:k + 1, :] * t
                    a[k] = a[k] + _colsum8(ub * t)
                if k0 == 0:
                    dub_s[r0:r0 + CHUNK, :] = part
                else:
                    dub_s[r0:r0 + CHUNK, :] += part
            for k in taps:
                acc_cb[k] += a[k]
        wa = [jnp.zeros((8, d), F32) for _ in range(K_A)]
        for c in range(tm // CHUNK):
            r0 = c * CHUNK
            rows = slice(r0, r0 + CHUNK)
            z = jnp.zeros((CHUNK, d), F32)
            dua = jnp.zeros((CHUNK, d), F32)
            ua = eua[HALO + r0:HALO + r0 + CHUNK, :]
            for k in range(K_A):
                z = z + ca_ref[k:k + 1, :] * eua[HALO + r0 - (K_A - 1) + k:HALO + r0 - (K_A - 1) + k + CHUNK, :]
                t = edz[r0 + (K_A - 1) - k:r0 + (K_A - 1) - k + CHUNK, :]
                dua = dua + ca_ref[k:k + 1, :] * t
                wa[k] = wa[k] + _colsum8(ua * t)
            put(0, r0, dpa_ref[rows, :].astype(F32) * z)
            put(1, r0, dua * pj_ref[rows, 2 * d:3 * d].astype(F32))
            put(2, r0, dua * pj_ref[rows, d:2 * d].astype(F32))
            dub = dub_s[rows, :]
            bval = pj_ref[rows, 3 * d:4 * d].astype(F32)
            sg = _sigmoid(pj_ref[rows, 4 * d:5 * d].astype(F32))
            put(3, r0, dub * sg)
            put(4, r0, dub * bval * sg * (1.0 - sg))
            for gi, w in enumerate(POOL_WINDOWS):
                cols = slice(gi * gc, (gi + 1) * gc)
                tot = eq[rows, cols]
                for k in range(1, w):
                    tot = tot + eq[r0 + k:r0 + k + CHUNK, cols]
                dci = tot - dpc_ref[rows, cols].astype(F32)
                dpj_ref[rows, 5 * d + gi * gc:5 * d + (gi + 1) * gc] = dci.astype(dpj_ref.dtype)
                acc_bin[:, 5 * d + gi * gc:5 * d + (gi + 1) * gc] += _colsum8(dci)
        for q in range(3):
            gv = dgt_ref[:, q * d:(q + 1) * d]
            dpj_ref[:, (6 + q) * d:(7 + q) * d] = gv
            acc_bin[:, (6 + q) * d:(7 + q) * d] += _colsum8(gv.astype(F32))
        for k in range(K_A):
            acc_ca[k] += wa[k]

        @pl.when(i == n - 1)
        def _():
            dbin_ref[...] = jnp.sum(acc_bin[...], axis=0, keepdims=True)
            for k in range(K_A):
                dca_ref[k:k + 1, :] = jnp.sum(acc_ca[k], axis=0, keepdims=True)
            for k in range(K_B):
                dcb_ref[k:k + 1, :] = jnp.sum(acc_cb[k], axis=0, keepdims=True)
            dcbb_ref[...] = jnp.sum(acc_v[0:8, :], axis=0, keepdims=True)
            dlng_ref[...] = jnp.sum(acc_v[8:16, :], axis=0, keepdims=True)
            dlnb_ref[...] = jnp.sum(acc_v[16:24, :], axis=0, keepdims=True)

    row = lambda i: (i, 0)
    fixed = lambda i: (0, 0)
    past = lambda i: (jnp.maximum(i * hb - 1, 0), 0)
    fut = lambda i: (jnp.minimum((i + 1) * hb, last_halo), 0)
    vec = jax.ShapeDtypeStruct((1, d), F32)
    tile_and_halo = [pl.BlockSpec((tm, d), row), pl.BlockSpec((HALO, d), fut)]
    return pl.pallas_call(
        body, name=name, grid=(n,),
        in_specs=[pl.BlockSpec((tm, 6 * d), row), pl.BlockSpec((HALO, 6 * d), past), pl.BlockSpec((HALO, 6 * d), fut),
                  *tile_and_halo, *tile_and_halo, *tile_and_halo, *tile_and_halo,
                  pl.BlockSpec((tm, 3 * d), row),
                  pl.BlockSpec((K_A, d), fixed), pl.BlockSpec((K_B, d), fixed), pl.BlockSpec((1, d), fixed),
                  pl.BlockSpec((1, d), fixed)],
        out_specs=[pl.BlockSpec((tm, 9 * d), row), pl.BlockSpec((1, 9 * d), fixed), pl.BlockSpec((K_A, d), fixed),
                   pl.BlockSpec((K_B, d), fixed), pl.BlockSpec((1, d), fixed), pl.BlockSpec((1, d), fixed),
                   pl.BlockSpec((1, d), fixed)],
        out_shape=[jax.ShapeDtypeStruct((s, 9 * d), BF16), jax.ShapeDtypeStruct((1, 9 * d), F32),
                   jax.ShapeDtypeStruct((K_A, d), F32), jax.ShapeDtypeStruct((K_B, d), F32), vec, vec, vec],
        scratch_shapes=[pltpu.VMEM((te, d), F32)] * 5 + [pltpu.VMEM((SUBLANES - 1, te, d), F32),
                                                         pltpu.VMEM((tm, d), F32),
                                                         pltpu.VMEM((8, 9 * d), F32), pltpu.VMEM((K_A, 8, d), F32),
                                                         pltpu.VMEM((K_B, 8, d), F32), pltpu.VMEM((24, d), F32)],
        compiler_params=_params("arbitrary"),
    )(proj, proj, proj, cv, cv, dpa, dpa, dsw, dsw, dpc, dpc, dgates, conv_a, conv_b, ln_g, ln_b)


def _pool_wgrad(name, p, dpw, d, tk):
    s = p.shape[0]
    gc = d // N_GROUPS
    n = s // tk

    def body(p_ref, g_ref, o_ref, acc):
        k = pl.program_id(0)
        for gi in range(N_GROUPS):
            cols = slice(gi * gc, (gi + 1) * gc)
            part = lax.dot_general(p_ref[:, cols], g_ref[:, cols], TN, preferred_element_type=F32)

            @pl.when(k == 0)
            def _():
                acc[gi] = part

            @pl.when(k > 0)
            def _():
                acc[gi] += part

        @pl.when(k == n - 1)
        def _():
            for gi in range(N_GROUPS):
                o_ref[:, gi] = acc[gi].astype(o_ref.dtype).reshape(N_DEV, gc // N_DEV, gc)

    return pl.pallas_call(
        body, name=name, grid=(n,),
        in_specs=[pl.BlockSpec((tk, d), lambda k: (k, 0)), pl.BlockSpec((tk, d), lambda k: (k, 0))],
        out_specs=pl.BlockSpec((N_DEV, N_GROUPS, gc // N_DEV, gc), lambda k: (0, 0, 0, 0)),
        out_shape=jax.ShapeDtypeStruct((N_DEV, N_GROUPS, gc // N_DEV, gc), BF16),
        scratch_shapes=[pltpu.VMEM((N_GROUPS, gc, gc), F32)], compiler_params=_params("arbitrary"),
    )(p, dpw)


def _my_place():
    x, y, c = lax.axis_index("x"), lax.axis_index("y"), lax.axis_index("c")
    return x, y, c


def _block_of(x, y, c):
    return 4 * x + 2 * y + c


def _slot(ref, k, paired):
    if not paired:
        return ref.at[k]
    cols = ref.shape[-1] // 2
    return ref.at[k // 2, :, pl.ds(pl.multiple_of((k % 2) * cols, 128), cols)]


def _slot_shape(shape, paired):
    return (N_DEV // 2, shape[0], 2 * shape[1]) if paired else (N_DEV, *shape)


def _gather_shards(shards, paired):
    n_arr = len(shards)

    def body(*refs):
        srcs = refs[:n_arr]
        outs = refs[n_arr:2 * n_arr]
        send_sems, recv_sems, local_sems = refs[2 * n_arr:]
        x, y, c = _my_place()
        me, sibling = (x, y, c), (x, y, 1 - c)
        chips = [(1 - x, y), (x, 1 - y), (1 - x, 1 - y)]

        def copy(n, k, block, to, src=None):
            rows = _slot(outs[n], _block_of(*block), paired[n])
            return pltpu.make_async_remote_copy(
                src_ref=rows if src is None else src, dst_ref=rows, send_sem=send_sems.at[n, k],
                recv_sem=recv_sems.at[n, k], device_id=to, device_id_type=MESH)

        mine = [pltpu.make_async_copy(srcs[n], _slot(outs[n], _block_of(*me), paired[n]), local_sems.at[n])
                for n in range(n_arr)]
        for cp in mine:
            cp.start()
        first = []
        for n in range(n_arr):
            first.append(copy(n, 0, me, sibling, src=srcs[n]))
            first += [copy(n, 1 + j, me, (*chip, c), src=srcs[n]) for j, chip in enumerate(chips)]
        for cp in first:
            cp.start()
        passed = []
        for n in range(n_arr):
            for j, chip in enumerate(chips):
                copy(n, 1 + j, (*chip, c), me).wait_recv()
                fwd = copy(n, 4 + j, (*chip, c), sibling)
                fwd.start()
                passed.append(fwd)
        for n in range(n_arr):
            copy(n, 0, sibling, me).wait_recv()
            for j, chip in enumerate(chips):
                copy(n, 4 + j, (*chip, 1 - c), me).wait_recv()
        for cp in first + passed:
            cp.wait_send()
        for cp in mine:
            cp.wait()

    any_spec = pl.BlockSpec(memory_space=pl.ANY)
    return pl.pallas_call(
        body, name="gather_weights",
        in_specs=[any_spec] * n_arr, out_specs=[any_spec] * n_arr,
        out_shape=[jax.ShapeDtypeStruct(_slot_shape(sh.shape, p), sh.dtype) for sh, p in zip(shards, paired)],
        scratch_shapes=[pltpu.SemaphoreType.DMA((n_arr, 7)), pltpu.SemaphoreType.DMA((n_arr, 7)),
                        pltpu.SemaphoreType.DMA((n_arr,))],
    )(*shards)


def _peers(x, y, c):
    out = []
    for r in range(1, N_DEV):
        fx, fy, fc = (r >> 2) & 1, (r >> 1) & 1, r & 1
        out.append(((1 - x) if fx else x, (1 - y) if fy else y, (1 - c) if fc else c))
    return out


HBM_SPEC = pl.BlockSpec(memory_space=pltpu.HBM)
SEM_SPEC = pl.BlockSpec(memory_space=pltpu.SEMAPHORE)
ANY_SPEC = pl.BlockSpec(memory_space=pl.ANY)
N_PEERS = N_DEV - 1


def _peer_copy(src_ref, land_ref, send_sems, recv_sems, i, r, peer, me, blockwise, paired):
    src = _slot(src_ref, _block_of(*peer), paired) if blockwise else src_ref
    dst = land_ref.at[me] if blockwise else _slot(land_ref, me, paired)
    return pltpu.make_async_remote_copy(
        src_ref=src, dst_ref=dst, send_sem=send_sems.at[i * N_PEERS + r],
        recv_sem=recv_sems.at[i * N_PEERS + r], device_id=peer, device_id_type=MESH)


def _block_shape(shape, paired):
    return (shape[1], shape[2] // 2) if paired else tuple(shape[1:])


def _start_copies(name, srcs, after, blockwise, paired=None):
    n = len(srcs)
    paired = paired or [False] * n

    def body(*refs):
        s_in, l_in = refs[:n], refs[n:2 * n]
        send_sems, recv_sems = refs[2 * n + 1], refs[2 * n + 2]
        token = refs[-1]
        x, y, c = _my_place()
        me = _block_of(x, y, c)
        for i in range(n):
            for r, peer in enumerate(_peers(x, y, c)):
                _peer_copy(s_in[i], l_in[i], send_sems, recv_sems, i, r, peer, me, blockwise, paired[i]).start()
        token[...] = jnp.zeros_like(token)

    land_shapes = [(N_DEV, *_block_shape(s.shape, p)) if blockwise else _slot_shape(s.shape, p)
                   for s, p in zip(srcs, paired)]
    lands = [pltpu.with_memory_space_constraint(lax.empty(sh, s.dtype), pltpu.HBM) for sh, s in zip(land_shapes, srcs)]
    ins = [pltpu.with_memory_space_constraint(s, pltpu.HBM) for s in srcs]
    out = pl.pallas_call(
        body, name=name,
        out_shape=(pltpu.SemaphoreType.DMA((n * N_PEERS,)), pltpu.SemaphoreType.DMA((n * N_PEERS,)),
                   *[pltpu.HBM(s.shape, s.dtype) for s in srcs],
                   *[pltpu.HBM(sh, s.dtype) for sh, s in zip(land_shapes, srcs)],
                   jax.ShapeDtypeStruct((8, 128), F32)),
        in_specs=[HBM_SPEC] * (2 * n) + [ANY_SPEC],
        out_specs=(SEM_SPEC, SEM_SPEC, *[HBM_SPEC] * (2 * n), pl.BlockSpec(memory_space=pltpu.VMEM)),
        input_output_aliases={i: 2 + i for i in range(2 * n)},
        compiler_params=pltpu.CompilerParams(has_side_effects=pltpu.SideEffectType.DATAFLOW_SIDE_EFFECTING),
    )(*ins, *lands, after)
    return dict(send=out[0], recv=out[1], srcs=list(out[2:2 + n]), lands=list(out[2 + n:2 + 2 * n]), token=out[-1],
                paired=paired)


def _wait_copies(name, state, after, blockwise):
    n = len(state["srcs"])
    paired = state["paired"]

    def body(*refs):
        s_in, l_in = refs[:n], refs[n:2 * n]
        send_sems, recv_sems = refs[2 * n], refs[2 * n + 1]
        x, y, c = _my_place()
        me = _block_of(x, y, c)
        for i in range(n):
            for r, peer in enumerate(_peers(x, y, c)):
                cp = _peer_copy(s_in[i], l_in[i], send_sems, recv_sems, i, r, peer, me, blockwise, paired[i])
                cp.wait_send()
                cp.wait_recv()

    both = state["srcs"] + state["lands"]
    out = pl.pallas_call(
        body, name=name, out_shape=tuple(pltpu.HBM(a.shape, a.dtype) for a in both),
        in_specs=[HBM_SPEC] * (2 * n) + [SEM_SPEC, SEM_SPEC, ANY_SPEC], out_specs=tuple([HBM_SPEC] * (2 * n)),
        input_output_aliases={i: i for i in range(2 * n)},
        compiler_params=pltpu.CompilerParams(has_side_effects=pltpu.SideEffectType.DATAFLOW_SIDE_EFFECTING),
    )(*both, state["send"], state["recv"], after)
    return list(out[:n]), list(out[n:])


COPY_BLOCK_BYTES = 2 * 1024 * 1024


def _place_own(name, lands, srcs, me, blockwise, paired=None):
    out = []
    paired = paired or [False] * len(lands)
    for i, (land, src) in enumerate(zip(lands, srcs)):
        in_slots = src if blockwise else land
        part = _block_shape(in_slots.shape, paired[i])
        row_bytes = land.dtype.itemsize
        for extent in part[1:]:
            row_bytes *= extent
        tr = part[0]
        while tr * row_bytes > COPY_BLOCK_BYTES and tr % 16 == 0:
            tr //= 2
        tail = (0,) * (len(part) - 1)

        def body(me_ref, s_ref, l_ref, o_ref):
            o_ref[...] = s_ref[...]

        if paired[i]:
            slot_spec = pl.BlockSpec((None, tr, part[1]), lambda j, me_ref: (me_ref[0] // 2, j, me_ref[0] % 2))
        else:
            slot_spec = pl.BlockSpec((None, tr, *part[1:]), lambda j, me_ref: (me_ref[0], j, *tail))
        if blockwise:
            s_spec = slot_spec
            o_spec = pl.BlockSpec((None, tr, *part[1:]), lambda j, me_ref: (me_ref[0], j, *tail))
        else:
            s_spec = pl.BlockSpec((tr, *part[1:]), lambda j, me_ref: (j, *tail))
            o_spec = slot_spec
        out.append(pl.pallas_call(
            body, name=f"{name}_{i}",
            grid_spec=pltpu.PrefetchScalarGridSpec(
                num_scalar_prefetch=1, grid=(part[0] // tr,), in_specs=[s_spec, ANY_SPEC], out_specs=o_spec),
            out_shape=jax.ShapeDtypeStruct(land.shape, land.dtype), input_output_aliases={2: 0},
            compiler_params=_params("parallel"),
        )(me, src, land))
    return out


def _adamw_math(w, g, m, v):
    m = ADAM_B1 * m + (1.0 - ADAM_B1) * g
    v = ADAM_B2 * v + (1.0 - ADAM_B2) * (g * g)
    m_hat = m / (1.0 - ADAM_B1 ** ADAM_STEP)
    v_hat = v / (1.0 - ADAM_B2 ** ADAM_STEP)
    delta = -ADAM_LR * (m_hat / (jnp.sqrt(v_hat) + ADAM_EPS) + ADAM_WD * w)
    return delta, m, v


def _adamw(name, parts, w, m, v, *, grid, part_specs, w_spec):
    n_layers = len(parts)
    n_parts = parts[0].shape[0]

    def body(*refs):
        p_refs = refs[:n_layers]
        w_ref, m_ref, v_ref, g_ref, d_ref, nm_ref, nv_ref = refs[n_layers:]

        def total(p_ref):
            t = p_ref[0].astype(F32)
            for k in range(1, n_parts):
                t = t + p_ref[k].astype(F32)
            return t

        g = total(p_refs[0])
        for li in range(1, n_layers):
            g = jnp.where(pl.program_id(0) == li, total(p_refs[li]), g)
        delta, nm, nv = _adamw_math(w_ref[...], g, m_ref[...], v_ref[...])
        g_ref[...] = g
        d_ref[...] = delta
        nm_ref[...] = nm
        nv_ref[...] = nv

    out = jax.ShapeDtypeStruct(w.shape, F32)
    return pl.pallas_call(
        body, name=name, grid=grid, in_specs=[*part_specs, w_spec, w_spec, w_spec], out_specs=[w_spec] * 4,
        out_shape=[out] * 4, compiler_params=_params(*(("parallel",) * len(grid))),
    )(*parts, w, m, v)


def _layer_part_spec(layer, block, n_blocks, row_off=0):
    def index_map(l, i):
        ii = jnp.where(l == layer, i, jnp.where(l < layer, 0, n_blocks - 1))
        return (0, row_off + ii) + (0,) * (len(block) - 2)
    return pl.BlockSpec(block, index_map)


def _small_update(partials, triples, conv_rows):
    d = partials[-1].shape[-1]
    n_rep = len(triples)
    n_part = len(partials)
    rows = []
    for p in partials:
        rows.append(p.shape[0] * (p.shape[1] // d))
    offs = [sum(rows[:i]) for i in range(n_part)]
    total = -(-sum(rows) // 8) * 8

    def body(*refs):
        p_refs = refs[:n_part]
        wmv = refs[n_part:n_part + 3 * n_rep]
        outs = refs[n_part + 3 * n_rep:n_part + 3 * n_rep + 4 * n_rep + (n_part - n_rep)]
        buf, send_sems, recv_sems = refs[-3:]
        x, y, c = _my_place()
        me = _block_of(x, y, c)
        peers = _peers(x, y, c)
        mine = buf.at[me]
        if total > sum(rows):
            mine[sum(rows):total, :] = jnp.zeros((total - sum(rows), d), F32)
        for p_ref, off in zip(p_refs, offs):
            nr, nc = p_ref.shape[0], p_ref.shape[1] // d
            if nc == 1:
                mine[off:off + nr, :] = p_ref[...]
            else:
                for r in range(nr):
                    for q in range(nc):
                        mine[off + r * nc + q:off + r * nc + q + 1, :] = p_ref[r:r + 1, q * d:(q + 1) * d]
        sends =[pltpu.make_async_remote_copy(
            src_ref=buf.at[me], dst_ref=buf.at[me], send_sem=send_sems.at[r], recv_sem=recv_sems.at[r],
            device_id=peer, device_id_type=MESH) for r, peer in enumerate(peers)]
        for cp in sends:
            cp.start()
        for r, peer in enumerate(peers):
            pltpu.make_async_remote_copy(
                src_ref=buf.at[me], dst_ref=buf.at[_block_of(*peer)], send_sem=send_sems.at[r],
                recv_sem=recv_sems.at[r], device_id=peer, device_id_type=MESH).wait_recv()
        for cp in sends:
            cp.wait_send()
        tot = buf[0]
        for k in range(1, N_DEV):
            tot = tot + buf[k]
        buf[0] = tot
        for idx in range(n_part):
            nr, nc = p_refs[idx].shape[0], p_refs[idx].shape[1] // d
            if idx < n_rep:
                w_ref, m_ref, v_ref = wmv[3 * idx:3 * idx + 3]
                g_ref, d_ref, nm_ref, nv_ref = outs[4 * idx:4 * idx + 4]
            else:
                g_ref = outs[4 * n_rep + idx - n_rep]
            pieces = [(slice(0, nr), slice(0, d), offs[idx], nr)] if nc == 1 else [
                (slice(r, r + 1), slice(q * d, (q + 1) * d), offs[idx] + r * nc + q, 1)
                for r in range(nr) for q in range(nc)]
            for rws, cols, row, cnt in pieces:
                g = buf[0, row:row + cnt, :]
                g_ref[rws, cols] = g
                if idx < n_rep:
                    delta, nm, nv = _adamw_math(w_ref[rws, cols], g, m_ref[rws, cols], v_ref[rws, cols])
                    d_ref[rws, cols] = delta
                    nm_ref[rws, cols] = nm
                    nv_ref[rws, cols] = nv

    vm = pl.BlockSpec(memory_space=pltpu.VMEM)
    operands = list(partials)
    for t in triples:
        operands += list(t)
    out_shape = []
    for idx in range(n_rep):
        out_shape += [jax.ShapeDtypeStruct(partials[idx].shape, F32)] * 4
    for idx in range(n_rep, n_part):
        out_shape.append(jax.ShapeDtypeStruct(partials[idx].shape, F32))
    return pl.pallas_call(
        body, name="small_allreduce_adamw", in_specs=[vm] * len(operands), out_specs=[vm] * len(out_shape),
        out_shape=out_shape,
        scratch_shapes=[pltpu.VMEM((N_DEV, total, d), F32), pltpu.SemaphoreType.DMA((7,)), pltpu.SemaphoreType.DMA((7,))],
        compiler_params=pltpu.CompilerParams(vmem_limit_bytes=VMEM_LIMIT_BYTES),
    )(*operands)


def kernel(x, g_mix, w_in, b_in, conv_a, w_out_a, conv_b, conv_b_bias, ln_b_g, ln_b_b, w_out_b, b_out_b, w_pool, pool_scale, w_o, g_mlp, w_mlp1, w_mlp2, g_final, loss_target, m_g_mix, m_w_in, m_b_in, m_conv_a, m_w_out_a, m_conv_b, m_conv_b_bias, m_ln_b_g, m_ln_b_b, m_w_out_b, m_b_out_b, m_w_pool, m_pool_scale, m_w_o, m_g_mlp, m_w_mlp1, m_w_mlp2, m_g_final, v_g_mix, v_w_in, v_b_in, v_conv_a, v_w_out_a, v_conv_b, v_conv_b_bias, v_ln_b_g, v_ln_b_b, v_w_out_b, v_b_out_b, v_w_pool, v_pool_scale, v_w_o, v_g_mlp, v_w_mlp1, v_w_mlp2, v_g_final):
    _, s, d = x.shape
    n_layers = g_mix.shape[0]
    p_in = b_in.shape[1]
    ci = w_in.shape[2]
    c1 = w_mlp1.shape[2]
    rf = w_mlp2.shape[1]
    rd = w_out_a.shape[1]
    f = rf * N_DEV
    rp = rf + 3 * rd
    o_a, o_b, o_o = rf // rd, rf // rd + 1, rf // rd + 2
    gc = d // N_GROUPS
    ca_rows = 8
    tm = min(1024, s)
    tr = min(512, s)
    tx = min(256, s)
    tk = min(2048, s)
    tk_mlp = min(4096, s)
    tk_in = min(2048, s)

    me_arr = jnp.reshape(_block_of(*_my_place()), (1,)).astype(jnp.int32)

    def layer_shards(l):
        row_pack = jnp.concatenate([w_mlp2[l], w_out_a[l], w_out_b[l], w_o[l]], axis=0).astype(BF16)
        return [w_in[l].astype(BF16), w_mlp1[l].astype(BF16), row_pack, w_pool[l].astype(BF16)]

    conv_pack = jnp.concatenate(
        [conv_a, jnp.zeros((n_layers, ca_rows - K_A, rd), F32), conv_b], axis=1)
    first_shards = layer_shards(0)
    layer_pairing = [True, False, False, False]
    g_in_first, g_conv = _gather_shards([first_shards[0], conv_pack], [True, False])
    conv_full = jnp.transpose(g_conv, (1, 2, 0, 3)).reshape(n_layers, ca_rows + K_B, d)
    conv_a_f = conv_full[:, :K_A]
    conv_b_f = conv_full[:, ca_rows:]
    first_row_going = _start_copies("gather_start_row_0", first_shards[2:], g_conv, blockwise=False)
    in_flight = [_start_copies("gather_start_mlp1_0", first_shards[1:2], first_row_going["token"], blockwise=False)]
    for l in range(1, n_layers):
        in_flight.append(_start_copies(f"gather_start_{l}", layer_shards(l), in_flight[-1]["token"], blockwise=False,
                                       paired=layer_pairing))
    token = in_flight[-1]["token"][0:1, 0:1]

    xs = [x[0]]
    saved = []
    weights = []
    row2 = lambda j, i: (i, 0)
    for l in range(n_layers):
        x0 = xs[-1]
        vec = lambda a: a[l:l + 1]
        if l > 0:
            srcs, lands = _wait_copies(f"gather_wait_{l}", in_flight[l], x0, blockwise=False)
            g_in, g_1, g_row, g_pool = _place_own(f"gather_own_{l}", lands, srcs, me_arr, blockwise=False,
                                                  paired=layer_pairing)
        else:
            g_in = g_in_first
        h = _rms_fwd(f"rms_mix_{l}", x0, vec(g_mix) + token if l == 0 else vec(g_mix), tr)
        proj = _mm(
            f"proj_{l}", h, g_in, grid=(N_DEV // 2, s // tm), a_spec=pl.BlockSpec((tm, d), row2),
            b_spec=pl.BlockSpec((None, d, 2 * ci), lambda j, i: (j, 0, 0)),
            extras=(vec(b_in),), extra_specs=(pl.BlockSpec((1, 2 * ci), lambda j, i: (0, j)),),
            epilogue=lambda v, b: v + b, out_shape=jax.ShapeDtypeStruct((s, p_in), BF16),
            o_spec=pl.BlockSpec((tm, 2 * ci), lambda j, i: (i, j)), dims=NN)
        if l == 0:
            srcs, lands = _wait_copies("gather_wait_row_0", first_row_going, proj, blockwise=False)
            g_row, g_pool = _place_own("gather_own_row_0", lands, srcs, me_arr, blockwise=False)
        p_a, sw, p_c, cv, y_a, y_b, pw, merged, x1, h2 = _mixer_fwd(
            f"mix_fwd_{l}", proj, x0, conv_a_f[l], conv_b_f[l], vec(conv_b_bias), vec(ln_b_g), vec(ln_b_b),
            vec(b_out_b), vec(pool_scale), vec(g_mlp), g_row, g_pool, (o_a, o_b, o_o), d, tx)
        if l == 0:
            srcs, lands = _wait_copies("gather_wait_mlp1_0", in_flight[0], x1, blockwise=False)
            g_1, = _place_own("gather_own_mlp1_0", lands, srcs, me_arr, blockwise=False)
        weights.append((g_in, g_1, g_row, g_pool))
        a_pre = _mm(f"mlp1_{l}", h2, g_1, grid=(N_DEV // 2, s // tm), a_spec=pl.BlockSpec((tm, d), row2),
                    b_spec=pl.BlockSpec((2, d, c1), lambda j, i: (j, 0, 0)), slabs="n",
                    out_shape=jax.ShapeDtypeStruct((s, f), BF16),
                    o_spec=pl.BlockSpec((tm, 2 * c1), lambda j, i: (i, j)), dims=NN)
        x2 = _mm(f"mlp2_{l}", a_pre, g_row, grid=(1, s // tr), a_spec=pl.BlockSpec((tr, f), row2),
                 b_spec=pl.BlockSpec((N_DEV, rf, d), lambda j, i: (0, 0, 0)), prologue=_relu_sq,
                 extras=(x1,), extra_specs=(pl.BlockSpec((tr, d), row2),), epilogue=lambda v, r: v + r,
                 out_shape=jax.ShapeDtypeStruct((s, d), F32), o_spec=pl.BlockSpec((tr, d), row2), dims=NN)
        saved.append((x0, h, proj, p_a, sw, p_c, cv, y_a, y_b, pw, merged, x1, h2, a_pre))
        xs.append(x2)

    loss_part, dx, dx16, dg_final = _loss_head(xs[-1], g_final.reshape(1, d), loss_target[0], tr)
    loss = lax.psum(loss_part[0, 0], ("x", "y", "c"))

    small = [None] * n_layers
    exchanges = [None] * n_layers
    for l in reversed(range(n_layers)):
        x0, h, proj, p_a, sw, p_c, cv, y_a, y_b, pw, merged, x1, h2, a_pre = saved[l]
        g_in, g_1, g_row, g_pool = weights[l]
        vec = lambda a: a[l:l + 1]
        row_shape = jax.ShapeDtypeStruct((N_DEV, rp, d), BF16)

        def dd_grad(name, a, g, off, alias):
            return _mm(name, a, g, grid=(1, s // tk), a_spec=pl.BlockSpec((tk, d), lambda j, k: (k, 0)),
                       b_spec=pl.BlockSpec((tk, d), lambda j, k: (k, 0)), out_shape=row_shape,
                       o_spec=pl.BlockSpec((N_DEV, rd, d), lambda j, k: (0, off, 0)), dims=TN, nk=s // tk,
                       acc_shape=(d, d), alias_in=alias)

        d_a = _mm(f"d_act_{l}", dx16, g_row, grid=(N_DEV // 2, s // tm), a_spec=pl.BlockSpec((tm, d), row2),
                  b_spec=pl.BlockSpec((2, rf, d), lambda j, i: (j, 0, 0)), slabs="n",
                  extras=(a_pre,), extra_specs=(pl.BlockSpec((tm, 2 * rf), lambda j, i: (i, j)),),
                  epilogue=lambda v, a: v * (2.0 * jnp.maximum(a.astype(F32), 0.0)),
                  out_shape=jax.ShapeDtypeStruct((s, f), BF16),
                  o_spec=pl.BlockSpec((tm, 2 * rf), lambda j, i: (i, j)), dims=NT)
        dg_row = _mm(f"dw_mlp2_{l}", a_pre, dx16, grid=(N_DEV, s // tk_mlp),
                     a_spec=pl.BlockSpec((tk_mlp, rf), lambda j, k: (k, j)),
                     b_spec=pl.BlockSpec((tk_mlp, d), lambda j, k: (k, 0)), prologue=_relu_sq, out_shape=row_shape,
                     o_spec=pl.BlockSpec((None, rf, d), lambda j, k: (j, 0, 0)), dims=TN, nk=s // tk_mlp,
                     acc_shape=(rf, d))
        dg_1 = _mm(f"dw_mlp1_{l}", h2, d_a, grid=(N_DEV, s // tk_mlp),
                   a_spec=pl.BlockSpec((tk_mlp, d), lambda j, k: (k, 0)),
                   b_spec=pl.BlockSpec((tk_mlp, c1), lambda j, k: (k, j)),
                   out_shape=jax.ShapeDtypeStruct((N_DEV, d, c1), BF16),
                   o_spec=pl.BlockSpec((None, d, c1), lambda j, k: (j, 0, 0)), dims=TN, nk=s // tk_mlp,
                   acc_shape=(d, c1))
        mlp1_going = _start_copies(f"grads_start_mlp1_{l}", [dg_1], vec(g_mlp), blockwise=True)
        stream = [jax.ShapeDtypeStruct((s, d), F32), jax.ShapeDtypeStruct((s, d), BF16), jax.ShapeDtypeStruct((1, d), F32)]
        dx, dx16, dg_mlp = _mm(
            f"d_h2_{l}", d_a, g_1, grid=(1, s // tr), a_spec=pl.BlockSpec((tr, f), row2),
            b_spec=pl.BlockSpec((N_DEV, d, c1), lambda j, i: (0, 0, 0), pipeline_mode=pl.Buffered(1)), slabs="k",
            extras=(x1, vec(g_mlp), dx),
            extra_specs=(pl.BlockSpec((tr, d), row2), pl.BlockSpec((1, d), lambda j, i: (0, 0)),
                         pl.BlockSpec((tr, d), row2)),
            out_shape=stream, o_spec=[pl.BlockSpec((tr, d), row2), pl.BlockSpec((tr, d), row2),
                                      pl.BlockSpec((1, d), lambda j, i: (0, 0))],
            dims=NT, rms_bwd=(1, s // tr), after=mlp1_going["token"])
        d_ya, d_yb, d_pw, d_gates, d_pa, d_sw, d_pc, d_bout, d_pscale = _merge_bwd(
            f"merge_bwd_{l}", dx16, proj, y_a, y_b, pw, vec(pool_scale), g_row, g_pool, (o_a, o_b, o_o), d, tx)
        dg_row = dd_grad(f"dw_o_{l}", merged, dx16, o_o, dg_row)
        dg_row = dd_grad(f"dw_out_a_{l}", p_a, d_ya, o_a, dg_row)
        dg_row = dd_grad(f"dw_out_b_{l}", sw, d_yb, o_b, dg_row)
        dg_pool = _pool_wgrad(f"dw_pool_{l}", p_c, d_pw, d, tk)
        rest_going = _start_copies(f"grads_start_rest_{l}", [dg_row, dg_pool], vec(g_mlp), blockwise=True)
        d_proj, d_bin, d_ca, d_cb, d_cbb, d_lng, d_lnb = _mix_pre_bwd(
            f"mix_bwd_{l}", proj, cv, d_pa, d_sw, d_pc, d_gates, conv_a_f[l], conv_b_f[l],
            vec(ln_b_g) + rest_going["token"][0:1, 0:1], vec(ln_b_b), d, tx)
        dg_in = _mm(f"dw_in_{l}", h, d_proj, grid=(N_DEV // 2, s // tk_in),
                    a_spec=pl.BlockSpec((tk_in, d), lambda j, k: (k, 0)),
                    b_spec=pl.BlockSpec((tk_in, 2 * ci), lambda j, k: (k, j)),
                    out_shape=jax.ShapeDtypeStruct((N_DEV // 2, d, 2 * ci), BF16),
                    o_spec=pl.BlockSpec((None, d, 2 * ci), lambda j, k: (j, 0, 0)), dims=TN, nk=s // tk_in,
                    acc_shape=(d, 2 * ci), after=rest_going["token"])
        in_going = _start_copies(f"grads_start_in_{l}", [dg_in], vec(g_mix), blockwise=True, paired=[True])
        rows_ik = lambda i, k: (i, 0)
        once = dict(pipeline_mode=pl.Buffered(1))
        dx, dx16, dg_mix = _mm(
            f"d_h_{l}", d_proj, g_in, grid=(s // tm, N_DEV // 2), a_spec=pl.BlockSpec((tm, 2 * ci), lambda i, k: (i, k)),
            b_spec=pl.BlockSpec((None, d, 2 * ci), lambda i, k: (k, 0, 0)),
            extras=(x0, vec(g_mix), dx),
            extra_specs=(pl.BlockSpec((tm, d), rows_ik, **once), pl.BlockSpec((1, d), lambda i, k: (0, 0)),
                         pl.BlockSpec((tm, d), rows_ik, **once)),
            out_shape=stream, o_spec=[pl.BlockSpec((tm, d), rows_ik), pl.BlockSpec((tm, d), rows_ik),
                                      pl.BlockSpec((1, d), lambda i, k: (0, 0))],
            dims=NT, nk=N_DEV // 2, acc_shape=(tm, d), rms_bwd=(0, s // tm), after=in_going["token"])
        small[l] = (dg_mix, d_bin, d_cbb, d_lng, d_lnb, d_bout, d_pscale, dg_mlp, d_ca, d_cb)
        exchanges[l] = (in_going, mlp1_going, rest_going)

    grad_x = dx[None]

    names = ("g_mix", "b_in", "conv_b_bias", "ln_b_g", "ln_b_b", "b_out_b", "pool_scale", "g_mlp")
    given = dict(g_mix=(g_mix, m_g_mix, v_g_mix), b_in=(b_in, m_b_in, v_b_in),
                 conv_b_bias=(conv_b_bias, m_conv_b_bias, v_conv_b_bias), ln_b_g=(ln_b_g, m_ln_b_g, v_ln_b_g),
                 ln_b_b=(ln_b_b, m_ln_b_b, v_ln_b_b), b_out_b=(b_out_b, m_b_out_b, v_b_out_b),
                 pool_scale=(pool_scale, m_pool_scale, v_pool_scale), g_mlp=(g_mlp, m_g_mlp, v_g_mlp))
    partials, triples = [], []
    for i, nm in enumerate(names):
        partials.append(jnp.concatenate([small[l][i] for l in range(n_layers)], axis=0))
        triples.append(given[nm])
    partials.append(dg_final)
    triples.append(tuple(a.reshape(1, d) for a in (g_final, m_g_final, v_g_final)))
    partials.append(jnp.concatenate([small[l][8] for l in range(n_layers)], axis=0))
    partials.append(jnp.concatenate([small[l][9] for l in range(n_layers)], axis=0))
    outs = _small_update(partials, triples, 2)
    rep = {nm: outs[4 * i:4 * i + 4] for i, nm in enumerate(names)}
    rep["g_final"] = [a.reshape(d) for a in outs[4 * len(names):4 * len(names) + 4]]
    me = _block_of(*_my_place())
    gca = lax.dynamic_slice_in_dim(outs[-2].reshape(n_layers, K_A, d), me * rd, rd, axis=2)
    gcb = lax.dynamic_slice_in_dim(outs[-1].reshape(n_layers, K_B, d), me * rd, rd, axis=2)

    r_in, r_1, r_row, r_pool = [], [], [], []
    for l in reversed(range(n_layers)):
        in_going, mlp1_going, rest_going = exchanges[l]
        srcs_m, lands_m = _wait_copies(f"grads_wait_mlp1_{l}", mlp1_going, outs[0], blockwise=True)
        srcs_r, lands_r = _wait_copies(f"grads_wait_rest_{l}", rest_going, outs[0], blockwise=True)
        srcs_i, lands_i = _wait_copies(f"grads_wait_in_{l}", in_going, outs[0], blockwise=True)
        got = _place_own(f"grads_own_{l}", lands_i + lands_m + lands_r, srcs_i + srcs_m + srcs_r, me_arr, blockwise=True,
                         paired=layer_pairing)
        for lst, arr in zip((r_in, r_1, r_row, r_pool), got):
            lst.insert(0, arr)
    tb = min(256, d)
    layers = range(n_layers)
    res = {}
    res["w_in"] = _adamw("adamw_w_in", r_in, w_in, m_w_in, v_w_in, grid=(n_layers, d // tb),
                         part_specs=[_layer_part_spec(li, (N_DEV, tb, ci), d // tb) for li in layers],
                         w_spec=pl.BlockSpec((None, tb, ci), lambda l, i: (l, i, 0)))
    res["w_mlp1"] = _adamw("adamw_w_mlp1", r_1, w_mlp1, m_w_mlp1, v_w_mlp1, grid=(n_layers, d // tb),
                           part_specs=[_layer_part_spec(li, (N_DEV, tb, c1), d // tb) for li in layers],
                           w_spec=pl.BlockSpec((None, tb, c1), lambda l, i: (l, i, 0)))
    tf = min(256, rf)
    res["w_mlp2"] = _adamw("adamw_w_mlp2", r_row, w_mlp2, m_w_mlp2, v_w_mlp2, grid=(n_layers, rf // tf),
                           part_specs=[_layer_part_spec(li, (N_DEV, tf, d), rf // tf) for li in layers],
                           w_spec=pl.BlockSpec((None, tf, d), lambda l, i: (l, i, 0)))
    for nm, off, trip in (("w_out_a", o_a, (w_out_a, m_w_out_a, v_w_out_a)),
                          ("w_out_b", o_b, (w_out_b, m_w_out_b, v_w_out_b)), ("w_o", o_o, (w_o, m_w_o, v_w_o))):
        res[nm] = _adamw(f"adamw_{nm}", r_row, *trip, grid=(n_layers, 1),
                         part_specs=[_layer_part_spec(li, (N_DEV, rd, d), 1, row_off=off) for li in layers],
                         w_spec=pl.BlockSpec((None, rd, d), lambda l, i: (l, 0, 0)))
    res["w_pool"] = _adamw("adamw_w_pool", r_pool, w_pool, m_w_pool, v_w_pool, grid=(n_layers, 1),
                           part_specs=[_layer_part_spec(li, (N_DEV, N_GROUPS, gc // N_DEV, gc), 1) for li in layers],
                           w_spec=pl.BlockSpec((None, N_GROUPS, gc // N_DEV, gc), lambda l, i: (l, 0, 0, 0)))
    whole3 = lambda: (0, 0, 0)
    res["conv_a"] = _adamw("adamw_conv_a", [gca[None]], conv_a, m_conv_a, v_conv_a, grid=(),
                           part_specs=[pl.BlockSpec((1, n_layers, K_A, rd), lambda: (0, 0, 0, 0))],
                           w_spec=pl.BlockSpec((n_layers, K_A, rd), whole3))
    res["conv_b"] = _adamw("adamw_conv_b", [gcb[None]], conv_b, m_conv_b, v_conv_b, grid=(),
                           part_specs=[pl.BlockSpec((1, n_layers, K_B, rd), lambda: (0, 0, 0, 0))],
                           w_spec=pl.BlockSpec((n_layers, K_B, rd), whole3))
    res.update(rep)

    order = ("g_mix", "w_in", "b_in", "conv_a", "w_out_a", "conv_b", "conv_b_bias", "ln_b_g", "ln_b_b", "w_out_b",
             "b_out_b", "w_pool", "pool_scale", "w_o", "g_mlp", "w_mlp1", "w_mlp2", "g_final")
    out = [loss, grad_x]
    for kind in range(4):
        out += [res[nm][kind] for nm in order]
    return tuple(out)
```

```python
import jax
import jax.numpy as jnp
from jax import lax
from jax.experimental import pallas as pl
from jax.experimental.pallas import tpu as pltpu

F32 = jnp.float32
BF16 = jnp.bfloat16
MESH = pl.DeviceIdType.MESH

N_DEV = 8
EPS = 1e-6
K_A = 3
K_B = 31
POOL_WINDOWS = (2, 4, 8, 16)
N_GROUPS = len(POOL_WINDOWS)
HALO = 32
CHUNK = 16
SUBLANES = 8
TAP_GROUP = 4
ADAM_LR, ADAM_B1, ADAM_B2, ADAM_EPS, ADAM_WD, ADAM_STEP = 0.001, 0.9, 0.999, 1e-08, 0.01, 10
VMEM_LIMIT_BYTES = 60 * 1024 * 1024

NN = (((1,), (0,)), ((), ()))
NT = (((1,), (1,)), ((), ()))
TN = (((0,), (0,)), ((), ()))


def _params(*sem):
    return pltpu.CompilerParams(dimension_semantics=sem, vmem_limit_bytes=VMEM_LIMIT_BYTES)


def _sigmoid(v):
    return 1.0 / (1.0 + jnp.exp(-v))


def _mm(name, a, b, *, grid, a_spec, b_spec, out_shape, o_spec, dims, nk=1, acc_shape=None,
        extras=(), extra_specs=(), prologue=None, epilogue=None, alias_in=None, slabs=None, after=None,
        rms_bwd=None):
    n_extra = len(extras)
    has_alias = alias_in is not None
    n_unread = (1 if has_alias else 0) + (1 if after is not None else 0)

    def body(*refs):
        a_ref, b_ref = refs[0], refs[1]
        ex = refs[2:2 + n_extra]
        o_ref = refs[2 + n_extra + n_unread]
        av = a_ref[...]
        if prologue is not None:
            av = prologue(av)
        av = av.astype(BF16)

        def finish_rms(val):
            x_ref, g_ref, dr_ref = ex
            dx_ref, dx16_ref, dg_ref = refs[2 + n_extra + n_unread:5 + n_extra + n_unread]
            acc_g = refs[-1]
            row_axis, n_rows = rms_bwd
            ri = pl.program_id(row_axis)
            xv = x_ref[...]
            r = lax.rsqrt(jnp.mean(xv * xv, axis=-1, keepdims=True) + EPS)
            xh = xv * r
            part = _colsum8(val * xh)

            @pl.when(ri == 0)
            def _():
                acc_g[...] = part

            @pl.when(ri > 0)
            def _():
                acc_g[...] += part

            dxh = val * g_ref[...]
            dx = r * (dxh - xh * jnp.mean(dxh * xh, axis=-1, keepdims=True)) + dr_ref[...]
            dx_ref[...] = dx
            dx16_ref[...] = dx.astype(BF16)

            @pl.when(ri == n_rows - 1)
            def _():
                dg_ref[...] = jnp.sum(acc_g[...], axis=0, keepdims=True)

        def finish(val, cols=None):
            if rms_bwd is not None:
                return finish_rms(val)
            if epilogue is not None:
                val = epilogue(val, *[e[...] if cols is None else e[:, cols] for e in ex])
            if cols is None:
                o_ref[...] = val.astype(o_ref.dtype).reshape(o_ref.shape)
            else:
                o_ref[:, cols] = val.astype(o_ref.dtype)

        if slabs == "n":
            for q in range(b_ref.shape[0]):
                pq = lax.dot_general(av, b_ref[q].astype(BF16), dims, preferred_element_type=F32)
                finish(pq, slice(q * pq.shape[1], (q + 1) * pq.shape[1]))
            return
        if slabs == "k":
            kc = av.shape[1] // b_ref.shape[0]
            p = None
            for q in range(b_ref.shape[0]):
                pq = lax.dot_general(av[:, q * kc:(q + 1) * kc], b_ref[q].astype(BF16), dims,
                                     preferred_element_type=F32)
                p = pq if p is None else p + pq
        else:
            bv = b_ref[...]
            bv = bv.reshape((-1, bv.shape[-1])).astype(BF16)
            p = lax.dot_general(av, bv, dims, preferred_element_type=F32)

        if nk == 1:
            finish(p)
        else:
            acc = refs[-2] if rms_bwd is not None else refs[-1]
            k = pl.program_id(len(grid) - 1)

            @pl.when(k == 0)
            def _():
                acc[...] = p

            @pl.when(k > 0)
            def _():
                acc[...] += p

            @pl.when(k == nk - 1)
            def _():
                finish(acc[...])

    in_specs = [a_spec, b_spec, *extra_specs]
    operands = [a, b, *extras]
    aliases = {}
    if has_alias:
        in_specs.append(pl.BlockSpec(memory_space=pl.ANY))
        operands.append(alias_in)
        aliases = {len(operands) - 1: 0}
    if after is not None:
        in_specs.append(pl.BlockSpec(memory_space=pl.ANY))
        operands.append(after)
    sem = ("parallel",) * (len(grid) - 1) + (("arbitrary",) if nk > 1 else ("parallel",))
    scratch = [pltpu.VMEM(acc_shape, F32)] if nk > 1 else []
    if rms_bwd is not None:
        sem = ("arbitrary",) * len(grid)
        scratch.append(pltpu.VMEM((8, extras[0].shape[-1]), F32))
    return pl.pallas_call(
        body, name=name, grid=grid, in_specs=in_specs, out_specs=o_spec, out_shape=out_shape,
        scratch_shapes=scratch, input_output_aliases=aliases, compiler_params=_params(*sem),
    )(*operands)


def _relu_sq(v):
    r = jnp.maximum(v, 0)
    return r * r


def _rms_fwd(name, x, g, tm):
    s, d = x.shape

    def body(x_ref, g_ref, h_ref):
        xv = x_ref[...]
        r = lax.rsqrt(jnp.mean(xv * xv, axis=-1, keepdims=True) + EPS)
        h_ref[...] = (xv * r * g_ref[...]).astype(h_ref.dtype)

    return pl.pallas_call(
        body, name=name, grid=(s // tm,),
        in_specs=[pl.BlockSpec((tm, d), lambda i: (i, 0)), pl.BlockSpec((1, d), lambda i: (0, 0))],
        out_specs=pl.BlockSpec((tm, d), lambda i: (i, 0)),
        out_shape=jax.ShapeDtypeStruct((s, d), BF16), compiler_params=_params("parallel"),
    )(x, g)


def _colsum8(v):
    return jnp.sum(v.reshape(v.shape[0] // 8, 8, v.shape[1]), axis=0)


def _loss_head(x, g, target, tm):
    s, d = x.shape
    n = s // tm

    def body(x_ref, g_ref, t_ref, loss_ref, dx_ref, dx16_ref, dg_ref, acc_l, acc_g):
        i = pl.program_id(0)
        xv = x_ref[...]
        r = lax.rsqrt(jnp.mean(xv * xv, axis=-1, keepdims=True) + EPS)
        xh = xv * r
        err = xh * g_ref[...] - t_ref[...]
        dy = err * (1.0 / d)
        lpart = _colsum8(err * err)
        gpart = _colsum8(dy * xh)

        @pl.when(i == 0)
        def _():
            acc_l[...] = lpart
            acc_g[...] = gpart

        @pl.when(i > 0)
        def _():
            acc_l[...] += lpart
            acc_g[...] += gpart

        dxh = dy * g_ref[...]
        dx = r * (dxh - xh * jnp.mean(dxh * xh, axis=-1, keepdims=True))
        dx_ref[...] = dx
        dx16_ref[...] = dx.astype(BF16)

        @pl.when(i == n - 1)
        def _():
            loss_ref[...] = (0.5 / d) * jnp.sum(jnp.sum(acc_l[...], axis=0, keepdims=True), axis=1, keepdims=True)
            dg_ref[...] = jnp.sum(acc_g[...], axis=0, keepdims=True)

    return pl.pallas_call(
        body, name="loss_head", grid=(n,),
        in_specs=[pl.BlockSpec((tm, d), lambda i: (i, 0)), pl.BlockSpec((1, d), lambda i: (0, 0)),
                  pl.BlockSpec((tm, d), lambda i: (i, 0))],
        out_specs=[pl.BlockSpec((1, 1), lambda i: (0, 0)), pl.BlockSpec((tm, d), lambda i: (i, 0)),
                   pl.BlockSpec((tm, d), lambda i: (i, 0)), pl.BlockSpec((1, d), lambda i: (0, 0))],
        out_shape=[jax.ShapeDtypeStruct((1, 1), F32), jax.ShapeDtypeStruct((s, d), F32),
                   jax.ShapeDtypeStruct((s, d), BF16), jax.ShapeDtypeStruct((1, d), F32)],
        scratch_shapes=[pltpu.VMEM((8, d), F32), pltpu.VMEM((8, d), F32)], compiler_params=_params("arbitrary"),
    )(x, g, target)


def _sec(ref, n, d):
    return ref[:, n * d:(n + 1) * d].astype(F32)


def _fill_shifts(sh, ext):
    rows = ext.shape[0] - SUBLANES
    for b in range(1, SUBLANES):
        sh[b - 1, 0:rows, :] = ext[b:b + rows, :]


def _shifted(sh, ext, off, n):
    b = off % SUBLANES
    if b == 0:
        return ext[off:off + n, :]
    return sh[b - 1, off - b:off - b + n, :]


def _spread_taps(dst, w_ref):
    for k in range(w_ref.shape[0]):
        dst[k] = jnp.broadcast_to(w_ref[k:k + 1, :], dst.shape[1:])


def _times_tap(x, tap):
    return (x.reshape(x.shape[0] // SUBLANES, SUBLANES, x.shape[1]) * tap[None]).reshape(x.shape)


def _pool_count(row0, rows, window):
    t = row0 + lax.broadcasted_iota(jnp.int32, (rows, 1), 0)
    return jnp.minimum(t + 1, window).astype(F32)


def _group_weight(w_ref, gi, gc):
    return w_ref[:, gi].reshape(gc, gc)


def _mixer_fwd(name, proj, x0, conv_a, conv_b, conv_b_bias, ln_g, ln_b, b_out_b, pool_scale, g_next, g_row, g_pool, offs,
               d, tm):
    s = proj.shape[0]
    n = s // tm
    gc = d // N_GROUPS
    hb = tm // HALO
    rd = d // N_DEV
    o_a, o_b, o_o = offs

    def body(pj_ref, hp_ref, x0_ref, ca_ref, cb_ref, cbb_ref, lng_ref, lnb_ref, bo_ref, sc_ref, gn_ref, wa_ref, wb_ref,
             wo_ref, wp_ref, pa_ref, sw_ref, pc_ref, cv_ref, ya_ref, yb_ref, pw_ref, mg_ref, x1_ref, h2_ref,
             eua, eub, euc, sh, taps_a, taps_b):
        i = pl.program_id(0)
        keep = (i > 0).astype(F32)
        _spread_taps(taps_a, ca_ref)
        _spread_taps(taps_b, cb_ref)
        eua[0:HALO, :] = _sec(hp_ref, 1, d) * _sec(hp_ref, 2, d) * keep
        eub[0:HALO, :] = _sec(hp_ref, 3, d) * _sigmoid(_sec(hp_ref, 4, d)) * keep
        euc[0:HALO, :] = _sec(hp_ref, 5, d) * keep
        eua[HALO:HALO + tm, :] = _sec(pj_ref, 1, d) * _sec(pj_ref, 2, d)
        eub[HALO:HALO + tm, :] = _sec(pj_ref, 3, d) * _sigmoid(_sec(pj_ref, 4, d))
        euc[HALO:HALO + tm, :] = _sec(pj_ref, 5, d)
        _fill_shifts(sh, eub)
        for c in range(tm // CHUNK):
            r0 = c * CHUNK
            z = jnp.zeros((CHUNK, d), F32)
            for k in range(K_A):
                z = z + _times_tap(eua[HALO + r0 - (K_A - 1) + k:HALO + r0 - (K_A - 1) + k + CHUNK, :], taps_a[k])
            pa_ref[r0:r0 + CHUNK, :] = (pj_ref[r0:r0 + CHUNK, 0:d].astype(F32) * z).astype(pa_ref.dtype)
            cv = jnp.zeros((CHUNK, d), F32) + cbb_ref[...]
            for k in range(K_B):
                cv = cv + _times_tap(_shifted(sh, eub, HALO + r0 - (K_B - 1) + k, CHUNK), taps_b[k])
            cv_ref[r0:r0 + CHUNK, :] = cv.astype(cv_ref.dtype)
        cvv = cv_ref[...].astype(F32)
        mu = jnp.mean(cvv, axis=-1, keepdims=True)
        xc = cvv - mu
        xh = xc * lax.rsqrt(jnp.mean(xc * xc, axis=-1, keepdims=True) + EPS)
        ln = xh * lng_ref[...] + lnb_ref[...]
        sw_ref[...] = (ln * _sigmoid(ln)).astype(sw_ref.dtype)
        for gi, w in enumerate(POOL_WINDOWS):
            cols = slice(gi * gc, (gi + 1) * gc)
            tot = euc[HALO:HALO + tm, cols]
            for k in range(1, w):
                tot = tot + euc[HALO - k:HALO - k + tm, cols]
            cnt = _pool_count(i * tm, tm, w)
            pc_ref[:, cols] = (tot / cnt - euc[HALO:HALO + tm, cols]).astype(pc_ref.dtype)
        ya_ref[...] = jnp.dot(pa_ref[...], wa_ref[...].reshape(d, d), preferred_element_type=F32).astype(ya_ref.dtype)
        yb_ref[...] = (jnp.dot(sw_ref[...], wb_ref[...].reshape(d, d), preferred_element_type=F32)
                       + bo_ref[...]).astype(yb_ref.dtype)
        for gi in range(N_GROUPS):
            cols = slice(gi * gc, (gi + 1) * gc)
            pw_ref[:, cols] = jnp.dot(pc_ref[:, cols], _group_weight(wp_ref, gi, gc),
                                      preferred_element_type=F32).astype(pw_ref.dtype)
        m = _sigmoid(_sec(pj_ref, 6, d)) * ya_ref[...].astype(F32)
        m = m + _sigmoid(_sec(pj_ref, 7, d)) * yb_ref[...].astype(F32)
        m = m + _sigmoid(_sec(pj_ref, 8, d)) * (pw_ref[...].astype(F32) * sc_ref[...])
        mg_ref[...] = m.astype(mg_ref.dtype)
        x1 = x0_ref[...] + jnp.dot(mg_ref[...], wo_ref[...].reshape(d, d), preferred_element_type=F32)
        x1_ref[...] = x1
        h2_ref[...] = (x1 * lax.rsqrt(jnp.mean(x1 * x1, axis=-1, keepdims=True) + EPS) * gn_ref[...]).astype(h2_ref.dtype)

    row = lambda i: (i, 0)
    fixed = lambda i: (0, 0)
    act = jax.ShapeDtypeStruct((s, d), BF16)

    def dd_weight(off):
        return pl.BlockSpec((N_DEV, rd, d), lambda i: (0, off, 0), pipeline_mode=pl.Buffered(1))

    return pl.pallas_call(
        body, name=name, grid=(n,),
        in_specs=[pl.BlockSpec((tm, 9 * d), row),
                  pl.BlockSpec((HALO, 6 * d), lambda i: (jnp.maximum(i * hb - 1, 0), 0)),
                  pl.BlockSpec((tm, d), row),
                  pl.BlockSpec((K_A, d), fixed), pl.BlockSpec((K_B, d), fixed), pl.BlockSpec((1, d), fixed),
                  pl.BlockSpec((1, d), fixed), pl.BlockSpec((1, d), fixed), pl.BlockSpec((1, d), fixed),
                  pl.BlockSpec((1, d), fixed), pl.BlockSpec((1, d), fixed), dd_weight(o_a), dd_weight(o_b), dd_weight(o_o),
                  pl.BlockSpec((N_DEV, N_GROUPS, gc // N_DEV, gc), lambda i: (0, 0, 0, 0),
                               pipeline_mode=pl.Buffered(1))],
        out_specs=[pl.BlockSpec((tm, d), row)] * 10,
        out_shape=[act] * 8 + [jax.ShapeDtypeStruct((s, d), F32), act],
        scratch_shapes=[pltpu.VMEM((tm + HALO, d), F32)] * 3 + [pltpu.VMEM((SUBLANES - 1, tm + HALO, d), F32),
                                                                pltpu.VMEM((K_A, SUBLANES, d), F32),
                                                                pltpu.VMEM((K_B, SUBLANES, d), F32)],
        compiler_params=_params("parallel"),
    )(proj, proj, x0, conv_a, conv_b, conv_b_bias, ln_g, ln_b, b_out_b, pool_scale, g_next, g_row, g_row, g_row,
      g_pool)


def _merge_bwd(name, dx16, proj, ya, yb, pw, pool_scale, g_row, g_pool, offs, d, tm):
    s = proj.shape[0]
    n = s // tm
    gc = d // N_GROUPS
    rd = d // N_DEV
    o_a, o_b, o_o = offs

    def body(dx_ref, g_ref, ya_ref, yb_ref, pw_ref, sc_ref, wa_ref, wb_ref, wo_ref, wp_ref,
             dya_ref, dyb_ref, dpw_ref, dg_ref, dpa_ref, dsw_ref, dpc_ref, dbo_ref, dsc_ref, acc_b, acc_s):
        i = pl.program_id(0)
        dmv = lax.dot_general(dx_ref[...], wo_ref[...].reshape(d, d), NT,
                              preferred_element_type=F32).astype(BF16).astype(F32)
        scale = sc_ref[...]
        g0 = _sigmoid(_sec(g_ref, 0, d))
        dya_ref[...] = (dmv * g0).astype(dya_ref.dtype)
        dg_ref[:, 0:d] = (dmv * ya_ref[...].astype(F32) * g0 * (1.0 - g0)).astype(dg_ref.dtype)
        g1 = _sigmoid(_sec(g_ref, 1, d))
        dyb = dmv * g1
        dyb_ref[...] = dyb.astype(dyb_ref.dtype)
        dg_ref[:, d:2 * d] = (dmv * yb_ref[...].astype(F32) * g1 * (1.0 - g1)).astype(dg_ref.dtype)
        g2 = _sigmoid(_sec(g_ref, 2, d))
        pwv = pw_ref[...].astype(F32)
        dyc = dmv * g2
        dpw_ref[...] = (dyc * scale).astype(dpw_ref.dtype)
        dg_ref[:, 2 * d:3 * d] = (dmv * (pwv * scale) * g2 * (1.0 - g2)).astype(dg_ref.dtype)
        pb = _colsum8(dyb)
        ps = _colsum8(dyc * pwv)

        @pl.when(i == 0)
        def _():
            acc_b[...] = pb
            acc_s[...] = ps

        @pl.when(i > 0)
        def _():
            acc_b[...] += pb
            acc_s[...] += ps

        dpa_ref[...] = lax.dot_general(dya_ref[...], wa_ref[...].reshape(d, d), NT,
                                       preferred_element_type=F32).astype(dpa_ref.dtype)
        dsw_ref[...] = lax.dot_general(dyb_ref[...], wb_ref[...].reshape(d, d), NT,
                                       preferred_element_type=F32).astype(dsw_ref.dtype)
        for gi in range(N_GROUPS):
            cols = slice(gi * gc, (gi + 1) * gc)
            dpc_ref[:, cols] = lax.dot_general(dpw_ref[:, cols], _group_weight(wp_ref, gi, gc), NT,
                                               preferred_element_type=F32).astype(dpc_ref.dtype)

        @pl.when(i == n - 1)
        def _():
            dbo_ref[...] = jnp.sum(acc_b[...], axis=0, keepdims=True)
            dsc_ref[...] = jnp.sum(acc_s[...], axis=0, keepdims=True)

    row = lambda i: (i, 0)
    fixed = lambda i: (0, 0)
    act = jax.ShapeDtypeStruct((s, d), BF16)
    vec = jax.ShapeDtypeStruct((1, d), F32)

    def dd_weight(off):
        return pl.BlockSpec((N_DEV, rd, d), lambda i: (0, off, 0), pipeline_mode=pl.Buffered(1))

    return pl.pallas_call(
        body, name=name, grid=(n,),
        in_specs=[pl.BlockSpec((tm, d), row), pl.BlockSpec((tm, 3 * d), lambda i: (i, 2)), pl.BlockSpec((tm, d), row),
                  pl.BlockSpec((tm, d), row), pl.BlockSpec((tm, d), row), pl.BlockSpec((1, d), fixed),
                  dd_weight(o_a), dd_weight(o_b), dd_weight(o_o),
                  pl.BlockSpec((N_DEV, N_GROUPS, gc // N_DEV, gc), lambda i: (0, 0, 0, 0),
                               pipeline_mode=pl.Buffered(1))],
        out_specs=[pl.BlockSpec((tm, d), row)] * 3 + [pl.BlockSpec((tm, 3 * d), row)] + [pl.BlockSpec((tm, d), row)] * 3
                  + [pl.BlockSpec((1, d), fixed)] * 2,
        out_shape=[act, act, act, jax.ShapeDtypeStruct((s, 3 * d), BF16), act, act, act, vec, vec],
        scratch_shapes=[pltpu.VMEM((8, d), F32)] * 2, compiler_params=_params("arbitrary"),
    )(dx16, proj, ya, yb, pw, pool_scale, g_row, g_row, g_row, g_pool)


def _mix_pre_bwd(name, proj, cv, dpa, dsw, dpc, dgates, conv_a, conv_b, ln_g, ln_b, d, tm):
    s = proj.shape[0]
    n = s // tm
    gc = d // N_GROUPS
    hb = tm // HALO
    last_halo = s // HALO - 1
    te = tm + HALO

    def ln_bwd(cvv, dswv, lng, lnb):
        mu = jnp.mean(cvv, axis=-1, keepdims=True)
        xc = cvv - mu
        rstd = lax.rsqrt(jnp.mean(xc * xc, axis=-1, keepdims=True) + EPS)
        xh = xc * rstd
        ln = xh * lng + lnb
        sg = _sigmoid(ln)
        dln = dswv * (sg * (1.0 + ln * (1.0 - sg)))
        dxh = dln * lng
        dcv = rstd * (dxh - jnp.mean(dxh, axis=-1, keepdims=True) - xh * jnp.mean(dxh * xh, axis=-1, keepdims=True))
        return dcv, dln, xh

    def body(pj_ref, hp_ref, hf_ref, cv_ref, cvf_ref, dpa_ref, dpaf_ref, dsw_ref, dswf_ref, dpc_ref, dpcf_ref, dgt_ref,
             ca_ref, cb_ref, lng_ref, lnb_ref,
             dpj_ref, dbin_ref, dca_ref, dcb_ref, dcbb_ref, dlng_ref, dlnb_ref,
             eua, eub, edz, edcv, eq, sh, dub_s, taps_a, taps_b, acc_bin, acc_ca, acc_cb, acc_v):
        i = pl.program_id(0)
        keep_p = (i > 0).astype(F32)
        keep_f = (i < n - 1).astype(F32)
        _spread_taps(taps_a, ca_ref)
        _spread_taps(taps_b, cb_ref)

        @pl.when(i == 0)
        def _():
            acc_bin[...] = jnp.zeros_like(acc_bin)
            acc_ca[...] = jnp.zeros_like(acc_ca)
            acc_cb[...] = jnp.zeros_like(acc_cb)
            acc_v[...] = jnp.zeros_like(acc_v)

        eua[0:HALO, :] = _sec(hp_ref, 1, d) * _sec(hp_ref, 2, d) * keep_p
        eua[HALO:te, :] = _sec(pj_ref, 1, d) * _sec(pj_ref, 2, d)
        eub[HALO:te, :] = _sec(pj_ref, 3, d) * _sigmoid(_sec(pj_ref, 4, d))
        edz[0:tm, :] = dpa_ref[...].astype(F32) * _sec(pj_ref, 0, d)
        edz[tm:te, :] = dpaf_ref[...].astype(F32) * _sec(hf_ref, 0, d) * keep_f
        dcv, dln, xh = ln_bwd(cv_ref[...].astype(F32), dsw_ref[...].astype(F32), lng_ref[...], lnb_ref[...])
        edcv[0:tm, :] = dcv
        acc_v[0:8, :] += _colsum8(dcv)
        acc_v[8:16, :] += _colsum8(dln * xh)
        acc_v[16:24, :] += _colsum8(dln)
        dcvf, _, _ = ln_bwd(cvf_ref[...].astype(F32), dswf_ref[...].astype(F32), lng_ref[...], lnb_ref[...])
        edcv[tm:te, :] = dcvf * keep_f
        for gi, w in enumerate(POOL_WINDOWS):
            cols = slice(gi * gc, (gi + 1) * gc)
            eq[0:tm, cols] = dpc_ref[:, cols].astype(F32) / _pool_count(i * tm, tm, w)
            eq[tm:te, cols] = dpcf_ref[:, cols].astype(F32) / _pool_count((i + 1) * tm, HALO, w) * keep_f

        def put(sec_idx, r0, val):
            dpj_ref[r0:r0 + CHUNK, sec_idx * d:(sec_idx + 1) * d] = val.astype(dpj_ref.dtype)
            acc_bin[:, sec_idx * d:(sec_idx + 1) * d] += _colsum8(val)

        _fill_shifts(sh, edcv)
        for k0 in range(0, K_B, TAP_GROUP):
            taps = range(k0, min(k0 + TAP_GROUP, K_B))
            a = {k: jnp.zeros((8, d), F32) for k in taps}
            for c in range(tm // CHUNK):
                r0 = c * CHUNK
                ub = eub[HALO + r0:HALO + r0 + CHUNK, :]
                part = None
                for k in taps:
                    t = _shifted(sh, edcv, r0 + (K_B - 1) - k, CHUNK)
                    part = _times_tap(t, taps_b[k]) if part is None else part + _times_tap(t, taps_b[k])
                    a[k] = a[k] + _colsum8(ub * t)
                if k0 == 0:
                    dub_s[r0:r0 + CHUNK, :] = part
                else:
                    dub_s[r0:r0 + CHUNK, :] += part
            for k in taps:
                acc_cb[k] += a[k]
        wa = [jnp.zeros((8, d), F32) for _ in range(K_A)]
        for c in range(tm // CHUNK):
            r0 = c * CHUNK
            rows = slice(r0, r0 + CHUNK)
            z = jnp.zeros((CHUNK, d), F32)
            dua = jnp.zeros((CHUNK, d), F32)
            ua = eua[HALO + r0:HALO + r0 + CHUNK, :]
            for k in range(K_A):
                z = z + _times_tap(eua[HALO + r0 - (K_A - 1) + k:HALO + r0 - (K_A - 1) + k + CHUNK, :], taps_a[k])
                t = edz[r0 + (K_A - 1) - k:r0 + (K_A - 1) - k + CHUNK, :]
                dua = dua + _times_tap(t, taps_a[k])
                wa[k] = wa[k] + _colsum8(ua * t)
            put(0, r0, dpa_ref[rows, :].astype(F32) * z)
            put(1, r0, dua * pj_ref[rows, 2 * d:3 * d].astype(F32))
            put(2, r0, dua * pj_ref[rows, d:2 * d].astype(F32))
            dub = dub_s[rows, :]
            bval = pj_ref[rows, 3 * d:4 * d].astype(F32)
            sg = _sigmoid(pj_ref[rows, 4 * d:5 * d].astype(F32))
            put(3, r0, dub * sg)
            put(4, r0, dub * bval * sg * (1.0 - sg))
            for gi, w in enumerate(POOL_WINDOWS):
                cols = slice(gi * gc, (gi + 1) * gc)
                tot = eq[rows, cols]
                for k in range(1, w):
                    tot = tot + eq[r0 + k:r0 + k + CHUNK, cols]
                dci = tot - dpc_ref[rows, cols].astype(F32)
                dpj_ref[rows, 5 * d + gi * gc:5 * d + (gi + 1) * gc] = dci.astype(dpj_ref.dtype)
                acc_bin[:, 5 * d + gi * gc:5 * d + (gi + 1) * gc] += _colsum8(dci)
        for q in range(3):
            gv = dgt_ref[:, q * d:(q + 1) * d]
            dpj_ref[:, (6 + q) * d:(7 + q) * d] = gv
            acc_bin[:, (6 + q) * d:(7 + q) * d] += _colsum8(gv.astype(F32))
        for k in range(K_A):
            acc_ca[k] += wa[k]

        @pl.when(i == n - 1)
        def _():
            dbin_ref[...] = jnp.sum(acc_bin[...], axis=0, keepdims=True)
            for k in range(K_A):
                dca_ref[k:k + 1, :] = jnp.sum(acc_ca[k], axis=0, keepdims=True)
            for k in range(K_B):
                dcb_ref[k:k + 1, :] = jnp.sum(acc_cb[k], axis=0, keepdims=True)
            dcbb_ref[...] = jnp.sum(acc_v[0:8, :], axis=0, keepdims=True)
            dlng_ref[...] = jnp.sum(acc_v[8:16, :], axis=0, keepdims=True)
            dlnb_ref[...] = jnp.sum(acc_v[16:24, :], axis=0, keepdims=True)

    row = lambda i: (i, 0)
    fixed = lambda i: (0, 0)
    past = lambda i: (jnp.maximum(i * hb - 1, 0), 0)
    fut = lambda i: (jnp.minimum((i + 1) * hb, last_halo), 0)
    vec = jax.ShapeDtypeStruct((1, d), F32)
    tile_and_halo = [pl.BlockSpec((tm, d), row), pl.BlockSpec((HALO, d), fut)]
    return pl.pallas_call(
        body, name=name, grid=(n,),
        in_specs=[pl.BlockSpec((tm, 6 * d), row), pl.BlockSpec((HALO, 6 * d), past), pl.BlockSpec((HALO, 6 * d), fut),
                  *tile_and_halo, *tile_and_halo, *tile_and_halo, *tile_and_halo,
                  pl.BlockSpec((tm, 3 * d), row),
                  pl.BlockSpec((K_A, d), fixed), pl.BlockSpec((K_B, d), fixed), pl.BlockSpec((1, d), fixed),
                  pl.BlockSpec((1, d), fixed)],
        out_specs=[pl.BlockSpec((tm, 9 * d), row), pl.BlockSpec((1, 9 * d), fixed), pl.BlockSpec((K_A, d), fixed),
                   pl.BlockSpec((K_B, d), fixed), pl.BlockSpec((1, d), fixed), pl.BlockSpec((1, d), fixed),
                   pl.BlockSpec((1, d), fixed)],
        out_shape=[jax.ShapeDtypeStruct((s, 9 * d), BF16), jax.ShapeDtypeStruct((1, 9 * d), F32),
                   jax.ShapeDtypeStruct((K_A, d), F32), jax.ShapeDtypeStruct((K_B, d), F32), vec, vec, vec],
        scratch_shapes=[pltpu.VMEM((te, d), F32)] * 5 + [pltpu.VMEM((SUBLANES - 1, te, d), F32),
                                                         pltpu.VMEM((tm, d), F32),
                                                         pltpu.VMEM((K_A, SUBLANES, d), F32),
                                                         pltpu.VMEM((K_B, SUBLANES, d), F32),
                                                         pltpu.VMEM((8, 9 * d), F32), pltpu.VMEM((K_A, 8, d), F32),
                                                         pltpu.VMEM((K_B, 8, d), F32), pltpu.VMEM((24, d), F32)],
        compiler_params=_params("arbitrary"),
    )(proj, proj, proj, cv, cv, dpa, dpa, dsw, dsw, dpc, dpc, dgates, conv_a, conv_b, ln_g, ln_b)


def _pool_wgrad(name, p, dpw, d, tk):
    s = p.shape[0]
    gc = d // N_GROUPS
    n = s // tk

    def body(p_ref, g_ref, o_ref, acc):
        k = pl.program_id(0)
        for gi in range(N_GROUPS):
            cols = slice(gi * gc, (gi + 1) * gc)
            part = lax.dot_general(p_ref[:, cols], g_ref[:, cols], TN, preferred_element_type=F32)

            @pl.when(k == 0)
            def _():
                acc[gi] = part

            @pl.when(k > 0)
            def _():
                acc[gi] += part

        @pl.when(k == n - 1)
        def _():
            for gi in range(N_GROUPS):
                o_ref[:, gi] = acc[gi].astype(o_ref.dtype).reshape(N_DEV, gc // N_DEV, gc)

    return pl.pallas_call(
        body, name=name, grid=(n,),
        in_specs=[pl.BlockSpec((tk, d), lambda k: (k, 0)), pl.BlockSpec((tk, d), lambda k: (k, 0))],
        out_specs=pl.BlockSpec((N_DEV, N_GROUPS, gc // N_DEV, gc), lambda k: (0, 0, 0, 0)),
        out_shape=jax.ShapeDtypeStruct((N_DEV, N_GROUPS, gc // N_DEV, gc), BF16),
        scratch_shapes=[pltpu.VMEM((N_GROUPS, gc, gc), F32)], compiler_params=_params("arbitrary"),
    )(p, dpw)


def _my_place():
    x, y, c = lax.axis_index("x"), lax.axis_index("y"), lax.axis_index("c")
    return x, y, c


def _block_of(x, y, c):
    return 4 * x + 2 * y + c


def _slot(ref, k, paired):
    if not paired:
        return ref.at[k]
    cols = ref.shape[-1] // 2
    return ref.at[k // 2, :, pl.ds(pl.multiple_of((k % 2) * cols, 128), cols)]


def _slot_shape(shape, paired):
    return (N_DEV // 2, shape[0], 2 * shape[1]) if paired else (N_DEV, *shape)


def _gather_shards(shards, paired):
    n_arr = len(shards)

    def body(*refs):
        srcs = refs[:n_arr]
        outs = refs[n_arr:2 * n_arr]
        send_sems, recv_sems, local_sems = refs[2 * n_arr:]
        x, y, c = _my_place()
        me, sibling = (x, y, c), (x, y, 1 - c)
        chips = [(1 - x, y), (x, 1 - y), (1 - x, 1 - y)]

        def copy(n, k, block, to, src=None):
            rows = _slot(outs[n], _block_of(*block), paired[n])
            return pltpu.make_async_remote_copy(
                src_ref=rows if src is None else src, dst_ref=rows, send_sem=send_sems.at[n, k],
                recv_sem=recv_sems.at[n, k], device_id=to, device_id_type=MESH)

        mine = [pltpu.make_async_copy(srcs[n], _slot(outs[n], _block_of(*me), paired[n]), local_sems.at[n])
                for n in range(n_arr)]
        for cp in mine:
            cp.start()
        first = []
        for n in range(n_arr):
            first.append(copy(n, 0, me, sibling, src=srcs[n]))
            first += [copy(n, 1 + j, me, (*chip, c), src=srcs[n]) for j, chip in enumerate(chips)]
        for cp in first:
            cp.start()
        passed = []
        for n in range(n_arr):
            for j, chip in enumerate(chips):
                copy(n, 1 + j, (*chip, c), me).wait_recv()
                fwd = copy(n, 4 + j, (*chip, c), sibling)
                fwd.start()
                passed.append(fwd)
        for n in range(n_arr):
            copy(n, 0, sibling, me).wait_recv()
            for j, chip in enumerate(chips):
                copy(n, 4 + j, (*chip, 1 - c), me).wait_recv()
        for cp in first + passed:
            cp.wait_send()
        for cp in mine:
            cp.wait()

    any_spec = pl.BlockSpec(memory_space=pl.ANY)
    return pl.pallas_call(
        body, name="gather_weights",
        in_specs=[any_spec] * n_arr, out_specs=[any_spec] * n_arr,
        out_shape=[jax.ShapeDtypeStruct(_slot_shape(sh.shape, p), sh.dtype) for sh, p in zip(shards, paired)],
        scratch_shapes=[pltpu.SemaphoreType.DMA((n_arr, 7)), pltpu.SemaphoreType.DMA((n_arr, 7)),
                        pltpu.SemaphoreType.DMA((n_arr,))],
    )(*shards)


def _peers(x, y, c):
    out = []
    for r in range(1, N_DEV):
        fx, fy, fc = (r >> 2) & 1, (r >> 1) & 1, r & 1
        out.append(((1 - x) if fx else x, (1 - y) if fy else y, (1 - c) if fc else c))
    return out


HBM_SPEC = pl.BlockSpec(memory_space=pltpu.HBM)
SEM_SPEC = pl.BlockSpec(memory_space=pltpu.SEMAPHORE)
ANY_SPEC = pl.BlockSpec(memory_space=pl.ANY)
N_PEERS = N_DEV - 1


def _peer_copy(src_ref, land_ref, send_sems, recv_sems, i, r, peer, me, blockwise, paired):
    src = _slot(src_ref, _block_of(*peer), paired) if blockwise else src_ref
    dst = land_ref.at[me] if blockwise else _slot(land_ref, me, paired)
    return pltpu.make_async_remote_copy(
        src_ref=src, dst_ref=dst, send_sem=send_sems.at[i * N_PEERS + r],
        recv_sem=recv_sems.at[i * N_PEERS + r], device_id=peer, device_id_type=MESH)


def _block_shape(shape, paired):
    return (shape[1], shape[2] // 2) if paired else tuple(shape[1:])


def _start_copies(name, srcs, after, blockwise, paired=None):
    n = len(srcs)
    paired = paired or [False] * n

    def body(*refs):
        s_in, l_in = refs[:n], refs[n:2 * n]
        send_sems, recv_sems = refs[2 * n + 1], refs[2 * n + 2]
        token = refs[-1]
        x, y, c = _my_place()
        me = _block_of(x, y, c)
        for i in range(n):
            for r, peer in enumerate(_peers(x, y, c)):
                _peer_copy(s_in[i], l_in[i], send_sems, recv_sems, i, r, peer, me, blockwise, paired[i]).start()
        token[...] = jnp.zeros_like(token)

    land_shapes = [(N_DEV, *_block_shape(s.shape, p)) if blockwise else _slot_shape(s.shape, p)
                   for s, p in zip(srcs, paired)]
    lands = [pltpu.with_memory_space_constraint(lax.empty(sh, s.dtype), pltpu.HBM) for sh, s in zip(land_shapes, srcs)]
    ins = [pltpu.with_memory_space_constraint(s, pltpu.HBM) for s in srcs]
    out = pl.pallas_call(
        body, name=name,
        out_shape=(pltpu.SemaphoreType.DMA((n * N_PEERS,)), pltpu.SemaphoreType.DMA((n * N_PEERS,)),
                   *[pltpu.HBM(s.shape, s.dtype) for s in srcs],
                   *[pltpu.HBM(sh, s.dtype) for sh, s in zip(land_shapes, srcs)],
                   jax.ShapeDtypeStruct((8, 128), F32)),
        in_specs=[HBM_SPEC] * (2 * n) + [ANY_SPEC],
        out_specs=(SEM_SPEC, SEM_SPEC, *[HBM_SPEC] * (2 * n), pl.BlockSpec(memory_space=pltpu.VMEM)),
        input_output_aliases={i: 2 + i for i in range(2 * n)},
        compiler_params=pltpu.CompilerParams(has_side_effects=pltpu.SideEffectType.DATAFLOW_SIDE_EFFECTING),
    )(*ins, *lands, after)
    return dict(send=out[0], recv=out[1], srcs=list(out[2:2 + n]), lands=list(out[2 + n:2 + 2 * n]), token=out[-1],
                paired=paired)


def _wait_copies(name, state, after, blockwise):
    n = len(state["srcs"])
    paired = state["paired"]

    def body(*refs):
        s_in, l_in = refs[:n], refs[n:2 * n]
        send_sems, recv_sems = refs[2 * n], refs[2 * n + 1]
        x, y, c = _my_place()
        me = _block_of(x, y, c)
        for i in range(n):
            for r, peer in enumerate(_peers(x, y, c)):
                cp = _peer_copy(s_in[i], l_in[i], send_sems, recv_sems, i, r, peer, me, blockwise, paired[i])
                cp.wait_send()
                cp.wait_recv()

    both = state["srcs"] + state["lands"]
    out = pl.pallas_call(
        body, name=name, out_shape=tuple(pltpu.HBM(a.shape, a.dtype) for a in both),
        in_specs=[HBM_SPEC] * (2 * n) + [SEM_SPEC, SEM_SPEC, ANY_SPEC], out_specs=tuple([HBM_SPEC] * (2 * n)),
        input_output_aliases={i: i for i in range(2 * n)},
        compiler_params=pltpu.CompilerParams(has_side_effects=pltpu.SideEffectType.DATAFLOW_SIDE_EFFECTING),
    )(*both, state["send"], state["recv"], after)
    return list(out[:n]), list(out[n:])


COPY_BLOCK_BYTES = 2 * 1024 * 1024


def _place_own(name, lands, srcs, me, blockwise, paired=None):
    out = []
    paired = paired or [False] * len(lands)
    for i, (land, src) in enumerate(zip(lands, srcs)):
        in_slots = src if blockwise else land
        part = _block_shape(in_slots.shape, paired[i])
        row_bytes = land.dtype.itemsize
        for extent in part[1:]:
            row_bytes *= extent
        tr = part[0]
        while tr * row_bytes > COPY_BLOCK_BYTES and tr % 16 == 0:
            tr //= 2
        tail = (0,) * (len(part) - 1)

        def body(me_ref, s_ref, l_ref, o_ref):
            o_ref[...] = s_ref[...]

        if paired[i]:
            slot_spec = pl.BlockSpec((None, tr, part[1]), lambda j, me_ref: (me_ref[0] // 2, j, me_ref[0] % 2))
        else:
            slot_spec = pl.BlockSpec((None, tr, *part[1:]), lambda j, me_ref: (me_ref[0], j, *tail))
        if blockwise:
            s_spec = slot_spec
            o_spec = pl.BlockSpec((None, tr, *part[1:]), lambda j, me_ref: (me_ref[0], j, *tail))
        else:
            s_spec = pl.BlockSpec((tr, *part[1:]), lambda j, me_ref: (j, *tail))
            o_spec = slot_spec
        out.append(pl.pallas_call(
            body, name=f"{name}_{i}",
            grid_spec=pltpu.PrefetchScalarGridSpec(
                num_scalar_prefetch=1, grid=(part[0] // tr,), in_specs=[s_spec, ANY_SPEC], out_specs=o_spec),
            out_shape=jax.ShapeDtypeStruct(land.shape, land.dtype), input_output_aliases={2: 0},
            compiler_params=_params("parallel"),
        )(me, src, land))
    return out


def _adamw_math(w, g, m, v):
    m = ADAM_B1 * m + (1.0 - ADAM_B1) * g
    v = ADAM_B2 * v + (1.0 - ADAM_B2) * (g * g)
    m_hat = m / (1.0 - ADAM_B1 ** ADAM_STEP)
    v_hat = v / (1.0 - ADAM_B2 ** ADAM_STEP)
    delta = -ADAM_LR * (m_hat / (jnp.sqrt(v_hat) + ADAM_EPS) + ADAM_WD * w)
    return delta, m, v


def _adamw(name, parts, w, m, v, *, grid, part_specs, w_spec):
    n_layers = len(parts)
    n_parts = parts[0].shape[0]

    def body(*refs):
        p_refs = refs[:n_layers]
        w_ref, m_ref, v_ref, g_ref, d_ref, nm_ref, nv_ref = refs[n_layers:]

        def total(p_ref):
            t = p_ref[0].astype(F32)
            for k in range(1, n_parts):
                t = t + p_ref[k].astype(F32)
            return t

        g = total(p_refs[0])
        for li in range(1, n_layers):
            g = jnp.where(pl.program_id(0) == li, total(p_refs[li]), g)
        delta, nm, nv = _adamw_math(w_ref[...], g, m_ref[...], v_ref[...])
        g_ref[...] = g
        d_ref[...] = delta
        nm_ref[...] = nm
        nv_ref[...] = nv

    out = jax.ShapeDtypeStruct(w.shape, F32)
    return pl.pallas_call(
        body, name=name, grid=grid, in_specs=[*part_specs, w_spec, w_spec, w_spec], out_specs=[w_spec] * 4,
        out_shape=[out] * 4, compiler_params=_params(*(("parallel",) * len(grid))),
    )(*parts, w, m, v)


def _layer_part_spec(layer, block, n_blocks, row_off=0):
    def index_map(l, i):
        ii = jnp.where(l == layer, i, jnp.where(l < layer, 0, n_blocks - 1))
        return (0, row_off + ii) + (0,) * (len(block) - 2)
    return pl.BlockSpec(block, index_map)


def _small_update(partials, triples, conv_rows):
    d = partials[-1].shape[-1]
    n_rep = len(triples)
    n_part = len(partials)
    rows = []
    for p in partials:
        rows.append(p.shape[0] * (p.shape[1] // d))
    offs = [sum(rows[:i]) for i in range(n_part)]
    total = -(-sum(rows) // 8) * 8

    def body(*refs):
        p_refs = refs[:n_part]
        wmv = refs[n_part:n_part + 3 * n_rep]
        outs = refs[n_part + 3 * n_rep:n_part + 3 * n_rep + 4 * n_rep + (n_part - n_rep)]
        buf, send_sems, recv_sems = refs[-3:]
        x, y, c = _my_place()
        me = _block_of(x, y, c)
        peers = _peers(x, y, c)
        mine = buf.at[me]
        if total > sum(rows):
            mine[sum(rows):total, :] = jnp.zeros((total - sum(rows), d), F32)
        for p_ref, off in zip(p_refs, offs):
            nr, nc = p_ref.shape[0], p_ref.shape[1] // d
            if nc == 1:
                mine[off:off + nr, :] = p_ref[...]
            else:
                for r in range(nr):
                    for q in range(nc):
                        mine[off + r * nc + q:off + r * nc + q + 1, :] = p_ref[r:r + 1, q * d:(q + 1) * d]
        sends =[pltpu.make_async_remote_copy(
            src_ref=buf.at[me], dst_ref=buf.at[me], send_sem=send_sems.at[r], recv_sem=recv_sems.at[r],
            device_id=peer, device_id_type=MESH) for r, peer in enumerate(peers)]
        for cp in sends:
            cp.start()
        for r, peer in enumerate(peers):
            pltpu.make_async_remote_copy(
                src_ref=buf.at[me], dst_ref=buf.at[_block_of(*peer)], send_sem=send_sems.at[r],
                recv_sem=recv_sems.at[r], device_id=peer, device_id_type=MESH).wait_recv()
        for cp in sends:
            cp.wait_send()
        tot = buf[0]
        for k in range(1, N_DEV):
            tot = tot + buf[k]
        buf[0] = tot
        for idx in range(n_part):
            nr, nc = p_refs[idx].shape[0], p_refs[idx].shape[1] // d
            if idx < n_rep:
                w_ref, m_ref, v_ref = wmv[3 * idx:3 * idx + 3]
                g_ref, d_ref, nm_ref, nv_ref = outs[4 * idx:4 * idx + 4]
            else:
                g_ref = outs[4 * n_rep + idx - n_rep]
            pieces = [(slice(0, nr), slice(0, d), offs[idx], nr)] if nc == 1 else [
                (slice(r, r + 1), slice(q * d, (q + 1) * d), offs[idx] + r * nc + q, 1)
                for r in range(nr) for q in range(nc)]
            for rws, cols, row, cnt in pieces:
                g = buf[0, row:row + cnt, :]
                g_ref[rws, cols] = g
                if idx < n_rep:
                    delta, nm, nv = _adamw_math(w_ref[rws, cols], g, m_ref[rws, cols], v_ref[rws, cols])
                    d_ref[rws, cols] = delta
                    nm_ref[rws, cols] = nm
                    nv_ref[rws, cols] = nv

    vm = pl.BlockSpec(memory_space=pltpu.VMEM)
    operands = list(partials)
    for t in triples:
        operands += list(t)
    out_shape = []
    for idx in range(n_rep):
        out_shape += [jax.ShapeDtypeStruct(partials[idx].shape, F32)] * 4
    for idx in range(n_rep, n_part):
        out_shape.append(jax.ShapeDtypeStruct(partials[idx].shape, F32))
    return pl.pallas_call(
        body, name="small_allreduce_adamw", in_specs=[vm] * len(operands), out_specs=[vm] * len(out_shape),
        out_shape=out_shape,
        scratch_shapes=[pltpu.VMEM((N_DEV, total, d), F32), pltpu.SemaphoreType.DMA((7,)), pltpu.SemaphoreType.DMA((7,))],
        compiler_params=pltpu.CompilerParams(vmem_limit_bytes=VMEM_LIMIT_BYTES),
    )(*operands)


def kernel(x, g_mix, w_in, b_in, conv_a, w_out_a, conv_b, conv_b_bias, ln_b_g, ln_b_b, w_out_b, b_out_b, w_pool, pool_scale, w_o, g_mlp, w_mlp1, w_mlp2, g_final, loss_target, m_g_mix, m_w_in, m_b_in, m_conv_a, m_w_out_a, m_conv_b, m_conv_b_bias, m_ln_b_g, m_ln_b_b, m_w_out_b, m_b_out_b, m_w_pool, m_pool_scale, m_w_o, m_g_mlp, m_w_mlp1, m_w_mlp2, m_g_final, v_g_mix, v_w_in, v_b_in, v_conv_a, v_w_out_a, v_conv_b, v_conv_b_bias, v_ln_b_g, v_ln_b_b, v_w_out_b, v_b_out_b, v_w_pool, v_pool_scale, v_w_o, v_g_mlp, v_w_mlp1, v_w_mlp2, v_g_final):
    _, s, d = x.shape
    n_layers = g_mix.shape[0]
    p_in = b_in.shape[1]
    ci = w_in.shape[2]
    c1 = w_mlp1.shape[2]
    rf = w_mlp2.shape[1]
    rd = w_out_a.shape[1]
    f = rf * N_DEV
    rp = rf + 3 * rd
    o_a, o_b, o_o = rf // rd, rf // rd + 1, rf // rd + 2
    gc = d // N_GROUPS
    ca_rows = 8
    tm = min(1024, s)
    tr = min(512, s)
    tx = min(256, s)
    tk = min(2048, s)
    tk_mlp = min(4096, s)
    tk_in = min(2048, s)

    me_arr = jnp.reshape(_block_of(*_my_place()), (1,)).astype(jnp.int32)

    def layer_shards(l):
        row_pack = jnp.concatenate([w_mlp2[l], w_out_a[l], w_out_b[l], w_o[l]], axis=0).astype(BF16)
        return [w_in[l].astype(BF16), w_mlp1[l].astype(BF16), row_pack, w_pool[l].astype(BF16)]

    conv_pack = jnp.concatenate(
        [conv_a, jnp.zeros((n_layers, ca_rows - K_A, rd), F32), conv_b], axis=1)
    first_shards = layer_shards(0)
    layer_pairing = [True, False, False, False]
    g_in_first, g_conv = _gather_shards([first_shards[0], conv_pack], [True, False])
    conv_full = jnp.transpose(g_conv, (1, 2, 0, 3)).reshape(n_layers, ca_rows + K_B, d)
    conv_a_f = conv_full[:, :K_A]
    conv_b_f = conv_full[:, ca_rows:]
    first_row_going = _start_copies("gather_start_row_0", first_shards[2:], g_conv, blockwise=False)
    in_flight = [_start_copies("gather_start_mlp1_0", first_shards[1:2], first_row_going["token"], blockwise=False)]
    for l in range(1, n_layers):
        in_flight.append(_start_copies(f"gather_start_{l}", layer_shards(l), in_flight[-1]["token"], blockwise=False,
                                       paired=layer_pairing))
    token = in_flight[-1]["token"][0:1, 0:1]

    xs = [x[0]]
    saved = []
    weights = []
    row2 = lambda j, i: (i, 0)
    for l in range(n_layers):
        x0 = xs[-1]
        vec = lambda a: a[l:l + 1]
        if l > 0:
            srcs, lands = _wait_copies(f"gather_wait_{l}", in_flight[l], x0, blockwise=False)
            g_in, g_1, g_row, g_pool = _place_own(f"gather_own_{l}", lands, srcs, me_arr, blockwise=False,
                                                  paired=layer_pairing)
        else:
            g_in = g_in_first
        h = _rms_fwd(f"rms_mix_{l}", x0, vec(g_mix) + token if l == 0 else vec(g_mix), tr)
        proj = _mm(
            f"proj_{l}", h, g_in, grid=(N_DEV // 2, s // tm), a_spec=pl.BlockSpec((tm, d), row2),
            b_spec=pl.BlockSpec((None, d, 2 * ci), lambda j, i: (j, 0, 0)),
            extras=(vec(b_in),), extra_specs=(pl.BlockSpec((1, 2 * ci), lambda j, i: (0, j)),),
            epilogue=lambda v, b: v + b, out_shape=jax.ShapeDtypeStruct((s, p_in), BF16),
            o_spec=pl.BlockSpec((tm, 2 * ci), lambda j, i: (i, j)), dims=NN)
        if l == 0:
            srcs, lands = _wait_copies("gather_wait_row_0", first_row_going, proj, blockwise=False)
            g_row, g_pool = _place_own("gather_own_row_0", lands, srcs, me_arr, blockwise=False)
        p_a, sw, p_c, cv, y_a, y_b, pw, merged, x1, h2 = _mixer_fwd(
            f"mix_fwd_{l}", proj, x0, conv_a_f[l], conv_b_f[l], vec(conv_b_bias), vec(ln_b_g), vec(ln_b_b),
            vec(b_out_b), vec(pool_scale), vec(g_mlp), g_row, g_pool, (o_a, o_b, o_o), d, tx)
        if l == 0:
            srcs, lands = _wait_copies("gather_wait_mlp1_0", in_flight[0], x1, blockwise=False)
            g_1, = _place_own("gather_own_mlp1_0", lands, srcs, me_arr, blockwise=False)
        weights.append((g_in, g_1, g_row, g_pool))
        a_pre = _mm(f"mlp1_{l}", h2, g_1, grid=(N_DEV // 2, s // tm), a_spec=pl.BlockSpec((tm, d), row2),
                    b_spec=pl.BlockSpec((2, d, c1), lambda j, i: (j, 0, 0)), slabs="n",
                    out_shape=jax.ShapeDtypeStruct((s, f), BF16),
                    o_spec=pl.BlockSpec((tm, 2 * c1), lambda j, i: (i, j)), dims=NN)
        x2 = _mm(f"mlp2_{l}", a_pre, g_row, grid=(1, s // tr), a_spec=pl.BlockSpec((tr, f), row2),
                 b_spec=pl.BlockSpec((N_DEV, rf, d), lambda j, i: (0, 0, 0)), prologue=_relu_sq,
                 extras=(x1,), extra_specs=(pl.BlockSpec((tr, d), row2),), epilogue=lambda v, r: v + r,
                 out_shape=jax.ShapeDtypeStruct((s, d), F32), o_spec=pl.BlockSpec((tr, d), row2), dims=NN)
        saved.append((x0, h, proj, p_a, sw, p_c, cv, y_a, y_b, pw, merged, x1, h2, a_pre))
        xs.append(x2)

    loss_part, dx, dx16, dg_final = _loss_head(xs[-1], g_final.reshape(1, d), loss_target[0], tr)
    loss = lax.psum(loss_part[0, 0], ("x", "y", "c"))

    small = [None] * n_layers
    exchanges = [None] * n_layers
    for l in reversed(range(n_layers)):
        x0, h, proj, p_a, sw, p_c, cv, y_a, y_b, pw, merged, x1, h2, a_pre = saved[l]
        g_in, g_1, g_row, g_pool = weights[l]
        vec = lambda a: a[l:l + 1]
        row_shape = jax.ShapeDtypeStruct((N_DEV, rp, d), BF16)

        def dd_grad(name, a, g, off, alias):
            return _mm(name, a, g, grid=(1, s // tk), a_spec=pl.BlockSpec((tk, d), lambda j, k: (k, 0)),
                       b_spec=pl.BlockSpec((tk, d), lambda j, k: (k, 0)), out_shape=row_shape,
                       o_spec=pl.BlockSpec((N_DEV, rd, d), lambda j, k: (0, off, 0)), dims=TN, nk=s // tk,
                       acc_shape=(d, d), alias_in=alias)

        d_a = _mm(f"d_act_{l}", dx16, g_row, grid=(N_DEV // 2, s // tm), a_spec=pl.BlockSpec((tm, d), row2),
                  b_spec=pl.BlockSpec((2, rf, d), lambda j, i: (j, 0, 0)), slabs="n",
                  extras=(a_pre,), extra_specs=(pl.BlockSpec((tm, 2 * rf), lambda j, i: (i, j)),),
                  epilogue=lambda v, a: v * (2.0 * jnp.maximum(a.astype(F32), 0.0)),
                  out_shape=jax.ShapeDtypeStruct((s, f), BF16),
                  o_spec=pl.BlockSpec((tm, 2 * rf), lambda j, i: (i, j)), dims=NT)
        dg_row = _mm(f"dw_mlp2_{l}", a_pre, dx16, grid=(N_DEV, s // tk_mlp),
                     a_spec=pl.BlockSpec((tk_mlp, rf), lambda j, k: (k, j)),
                     b_spec=pl.BlockSpec((tk_mlp, d), lambda j, k: (k, 0)), prologue=_relu_sq, out_shape=row_shape,
                     o_spec=pl.BlockSpec((None, rf, d), lambda j, k: (j, 0, 0)), dims=TN, nk=s // tk_mlp,
                     acc_shape=(rf, d))
        dg_1 = _mm(f"dw_mlp1_{l}", h2, d_a, grid=(N_DEV, s // tk_mlp),
                   a_spec=pl.BlockSpec((tk_mlp, d), lambda j, k: (k, 0)),
                   b_spec=pl.BlockSpec((tk_mlp, c1), lambda j, k: (k, j)),
                   out_shape=jax.ShapeDtypeStruct((N_DEV, d, c1), BF16),
                   o_spec=pl.BlockSpec((None, d, c1), lambda j, k: (j, 0, 0)), dims=TN, nk=s // tk_mlp,
                   acc_shape=(d, c1))
        mlp1_going = _start_copies(f"grads_start_mlp1_{l}", [dg_1], vec(g_mlp), blockwise=True)
        stream = [jax.ShapeDtypeStruct((s, d), F32), jax.ShapeDtypeStruct((s, d), BF16), jax.ShapeDtypeStruct((1, d), F32)]
        dx, dx16, dg_mlp = _mm(
            f"d_h2_{l}", d_a, g_1, grid=(1, s // tr), a_spec=pl.BlockSpec((tr, f), row2),
            b_spec=pl.BlockSpec((N_DEV, d, c1), lambda j, i: (0, 0, 0), pipeline_mode=pl.Buffered(1)), slabs="k",
            extras=(x1, vec(g_mlp), dx),
            extra_specs=(pl.BlockSpec((tr, d), row2), pl.BlockSpec((1, d), lambda j, i: (0, 0)),
                         pl.BlockSpec((tr, d), row2)),
            out_shape=stream, o_spec=[pl.BlockSpec((tr, d), row2), pl.BlockSpec((tr, d), row2),
                                      pl.BlockSpec((1, d), lambda j, i: (0, 0))],
            dims=NT, rms_bwd=(1, s // tr), after=mlp1_going["token"])
        d_ya, d_yb, d_pw, d_gates, d_pa, d_sw, d_pc, d_bout, d_pscale = _merge_bwd(
            f"merge_bwd_{l}", dx16, proj, y_a, y_b, pw, vec(pool_scale), g_row, g_pool, (o_a, o_b, o_o), d, tx)
        dg_row = dd_grad(f"dw_o_{l}", merged, dx16, o_o, dg_row)
        dg_row = dd_grad(f"dw_out_a_{l}", p_a, d_ya, o_a, dg_row)
        dg_row = dd_grad(f"dw_out_b_{l}", sw, d_yb, o_b, dg_row)
        dg_pool = _pool_wgrad(f"dw_pool_{l}", p_c, d_pw, d, tk)
        rest_going = _start_copies(f"grads_start_rest_{l}", [dg_row, dg_pool], vec(g_mlp), blockwise=True)
        d_proj, d_bin, d_ca, d_cb, d_cbb, d_lng, d_lnb = _mix_pre_bwd(
            f"mix_bwd_{l}", proj, cv, d_pa, d_sw, d_pc, d_gates, conv_a_f[l], conv_b_f[l],
            vec(ln_b_g) + rest_going["token"][0:1, 0:1], vec(ln_b_b), d, tx)
        dg_in = _mm(f"dw_in_{l}", h, d_proj, grid=(N_DEV // 2, s // tk_in),
                    a_spec=pl.BlockSpec((tk_in, d), lambda j, k: (k, 0)),
                    b_spec=pl.BlockSpec((tk_in, 2 * ci), lambda j, k: (k, j)),
                    out_shape=jax.ShapeDtypeStruct((N_DEV // 2, d, 2 * ci), BF16),
                    o_spec=pl.BlockSpec((None, d, 2 * ci), lambda j, k: (j, 0, 0)), dims=TN, nk=s // tk_in,
                    acc_shape=(d, 2 * ci), after=rest_going["token"])
        in_going = _start_copies(f"grads_start_in_{l}", [dg_in], vec(g_mix), blockwise=True, paired=[True])
        rows_ik = lambda i, k: (i, 0)
        once = dict(pipeline_mode=pl.Buffered(1))
        dx, dx16, dg_mix = _mm(
            f"d_h_{l}", d_proj, g_in, grid=(s // tm, N_DEV // 2), a_spec=pl.BlockSpec((tm, 2 * ci), lambda i, k: (i, k)),
            b_spec=pl.BlockSpec((None, d, 2 * ci), lambda i, k: (k, 0, 0)),
            extras=(x0, vec(g_mix), dx),
            extra_specs=(pl.BlockSpec((tm, d), rows_ik, **once), pl.BlockSpec((1, d), lambda i, k: (0, 0)),
                         pl.BlockSpec((tm, d), rows_ik, **once)),
            out_shape=stream, o_spec=[pl.BlockSpec((tm, d), rows_ik), pl.BlockSpec((tm, d), rows_ik),
                                      pl.BlockSpec((1, d), lambda i, k: (0, 0))],
            dims=NT, nk=N_DEV // 2, acc_shape=(tm, d), rms_bwd=(0, s // tm), after=in_going["token"])
        small[l] = (dg_mix, d_bin, d_cbb, d_lng, d_lnb, d_bout, d_pscale, dg_mlp, d_ca, d_cb)
        exchanges[l] = (in_going, mlp1_going, rest_going)

    grad_x = dx[None]

    names = ("g_mix", "b_in", "conv_b_bias", "ln_b_g", "ln_b_b", "b_out_b", "pool_scale", "g_mlp")
    given = dict(g_mix=(g_mix, m_g_mix, v_g_mix), b_in=(b_in, m_b_in, v_b_in),
                 conv_b_bias=(conv_b_bias, m_conv_b_bias, v_conv_b_bias), ln_b_g=(ln_b_g, m_ln_b_g, v_ln_b_g),
                 ln_b_b=(ln_b_b, m_ln_b_b, v_ln_b_b), b_out_b=(b_out_b, m_b_out_b, v_b_out_b),
                 pool_scale=(pool_scale, m_pool_scale, v_pool_scale), g_mlp=(g_mlp, m_g_mlp, v_g_mlp))
    partials, triples = [], []
    for i, nm in enumerate(names):
        partials.append(jnp.concatenate([small[l][i] for l in range(n_layers)], axis=0))
        triples.append(given[nm])
    partials.append(dg_final)
    triples.append(tuple(a.reshape(1, d) for a in (g_final, m_g_final, v_g_final)))
    partials.append(jnp.concatenate([small[l][8] for l in range(n_layers)], axis=0))
    partials.append(jnp.concatenate([small[l][9] for l in range(n_layers)], axis=0))
    outs = _small_update(partials, triples, 2)
    rep = {nm: outs[4 * i:4 * i + 4] for i, nm in enumerate(names)}
    rep["g_final"] = [a.reshape(d) for a in outs[4 * len(names):4 * len(names) + 4]]
    me = _block_of(*_my_place())
    gca = lax.dynamic_slice_in_dim(outs[-2].reshape(n_layers, K_A, d), me * rd, rd, axis=2)
    gcb = lax.dynamic_slice_in_dim(outs[-1].reshape(n_layers, K_B, d), me * rd, rd, axis=2)

    r_in, r_1, r_row, r_pool = [], [], [], []
    for l in reversed(range(n_layers)):
        in_going, mlp1_going, rest_going = exchanges[l]
        srcs_m, lands_m = _wait_copies(f"grads_wait_mlp1_{l}", mlp1_going, outs[0], blockwise=True)
        srcs_r, lands_r = _wait_copies(f"grads_wait_rest_{l}", rest_going, outs[0], blockwise=True)
        srcs_i, lands_i = _wait_copies(f"grads_wait_in_{l}", in_going, outs[0], blockwise=True)
        got = _place_own(f"grads_own_{l}", lands_i + lands_m + lands_r, srcs_i + srcs_m + srcs_r, me_arr, blockwise=True,
                         paired=layer_pairing)
        for lst, arr in zip((r_in, r_1, r_row, r_pool), got):
            lst.insert(0, arr)
    tb = min(256, d)
    layers = range(n_layers)
    res = {}
    res["w_in"] = _adamw("adamw_w_in", r_in, w_in, m_w_in, v_w_in, grid=(n_layers, d // tb),
                         part_specs=[_layer_part_spec(li, (N_DEV, tb, ci), d // tb) for li in layers],
                         w_spec=pl.BlockSpec((None, tb, ci), lambda l, i: (l, i, 0)))
    res["w_mlp1"] = _adamw("adamw_w_mlp1", r_1, w_mlp1, m_w_mlp1, v_w_mlp1, grid=(n_layers, d // tb),
                           part_specs=[_layer_part_spec(li, (N_DEV, tb, c1), d // tb) for li in layers],
                           w_spec=pl.BlockSpec((None, tb, c1), lambda l, i: (l, i, 0)))
    tf = min(256, rf)
    res["w_mlp2"] = _adamw("adamw_w_mlp2", r_row, w_mlp2, m_w_mlp2, v_w_mlp2, grid=(n_layers, rf // tf),
                           part_specs=[_layer_part_spec(li, (N_DEV, tf, d), rf // tf) for li in layers],
                           w_spec=pl.BlockSpec((None, tf, d), lambda l, i: (l, i, 0)))
    for nm, off, trip in (("w_out_a", o_a, (w_out_a, m_w_out_a, v_w_out_a)),
                          ("w_out_b", o_b, (w_out_b, m_w_out_b, v_w_out_b)), ("w_o", o_o, (w_o, m_w_o, v_w_o))):
        res[nm] = _adamw(f"adamw_{nm}", r_row, *trip, grid=(n_layers, 1),
                         part_specs=[_layer_part_spec(li, (N_DEV, rd, d), 1, row_off=off) for li in layers],
                         w_spec=pl.BlockSpec((None, rd, d), lambda l, i: (l, 0, 0)))
    res["w_pool"] = _adamw("adamw_w_pool", r_pool, w_pool, m_w_pool, v_w_pool, grid=(n_layers, 1),
                           part_specs=[_layer_part_spec(li, (N_DEV, N_GROUPS, gc // N_DEV, gc), 1) for li in layers],
                           w_spec=pl.BlockSpec((None, N_GROUPS, gc // N_DEV, gc), lambda l, i: (l, 0, 0, 0)))
    whole3 = lambda: (0, 0, 0)
    res["conv_a"] = _adamw("adamw_conv_a", [gca[None]], conv_a, m_conv_a, v_conv_a, grid=(),
                           part_specs=[pl.BlockSpec((1, n_layers, K_A, rd), lambda: (0, 0, 0, 0))],
                           w_spec=pl.BlockSpec((n_layers, K_A, rd), whole3))
    res["conv_b"] = _adamw("adamw_conv_b", [gcb[None]], conv_b, m_conv_b, v_conv_b, grid=(),
                           part_specs=[pl.BlockSpec((1, n_layers, K_B, rd), lambda: (0, 0, 0, 0))],
                           w_spec=pl.BlockSpec((n_layers, K_B, rd), whole3))
    res.update(rep)

    order = ("g_mix", "w_in", "b_in", "conv_a", "w_out_a", "conv_b", "conv_b_bias", "ln_b_g", "ln_b_b", "w_out_b",
             "b_out_b", "w_pool", "pool_scale", "w_o", "g_mlp", "w_mlp1", "w_mlp2", "g_final")
    out = [loss, grad_x]
    for kind in range(4):
        out += [res[nm][kind] for nm in order]
    return tuple(out)
```

```python
import jax
import jax.numpy as jnp
from jax import lax
from jax.experimental import pallas as pl
from jax.experimental.pallas import tpu as pltpu

F32 = jnp.float32
BF16 = jnp.bfloat16
MESH = pl.DeviceIdType.MESH

N_DEV = 8
EPS = 1e-6
K_A = 3
K_B = 31
POOL_WINDOWS = (2, 4, 8, 16)
N_GROUPS = len(POOL_WINDOWS)
HALO = 32
CHUNK = 16
SUBLANES = 8
TAP_GROUP = 4
ADAM_LR, ADAM_B1, ADAM_B2, ADAM_EPS, ADAM_WD, ADAM_STEP = 0.001, 0.9, 0.999, 1e-08, 0.01, 10
VMEM_LIMIT_BYTES = 60 * 1024 * 1024

NN = (((1,), (0,)), ((), ()))
NT = (((1,), (1,)), ((), ()))
TN = (((0,), (0,)), ((), ()))


def _params(*sem):
    return pltpu.CompilerParams(dimension_semantics=sem, vmem_limit_bytes=VMEM_LIMIT_BYTES)


def _sigmoid(v):
    return 1.0 / (1.0 + jnp.exp(-v))


def _mm(name, a, b, *, grid, a_spec, b_spec, out_shape, o_spec, dims, nk=1, acc_shape=None,
        extras=(), extra_specs=(), prologue=None, epilogue=None, alias_in=None, slabs=None, after=None,
        rms_bwd=None):
    n_extra = len(extras)
    has_alias = alias_in is not None
    n_unread = (1 if has_alias else 0) + (1 if after is not None else 0)

    def body(*refs):
        a_ref, b_ref = refs[0], refs[1]
        ex = refs[2:2 + n_extra]
        o_ref = refs[2 + n_extra + n_unread]
        av = a_ref[...]
        if prologue is not None:
            av = prologue(av)
        av = av.astype(BF16)

        def finish_rms(val):
            x_ref, g_ref, dr_ref = ex
            dx_ref, dx16_ref, dg_ref = refs[2 + n_extra + n_unread:5 + n_extra + n_unread]
            acc_g = refs[-1]
            row_axis, n_rows = rms_bwd
            ri = pl.program_id(row_axis)
            xv = x_ref[...]
            r = lax.rsqrt(jnp.mean(xv * xv, axis=-1, keepdims=True) + EPS)
            xh = xv * r
            part = _colsum8(val * xh)

            @pl.when(ri == 0)
            def _():
                acc_g[...] = part

            @pl.when(ri > 0)
            def _():
                acc_g[...] += part

            dxh = val * g_ref[...]
            dx = r * (dxh - xh * jnp.mean(dxh * xh, axis=-1, keepdims=True)) + dr_ref[...]
            dx_ref[...] = dx
            dx16_ref[...] = dx.astype(BF16)

            @pl.when(ri == n_rows - 1)
            def _():
                dg_ref[...] = jnp.sum(acc_g[...], axis=0, keepdims=True)

        def finish(val, cols=None):
            if rms_bwd is not None:
                return finish_rms(val)
            if epilogue is not None:
                val = epilogue(val, *[e[...] if cols is None else e[:, cols] for e in ex])
            if cols is None:
                o_ref[...] = val.astype(o_ref.dtype).reshape(o_ref.shape)
            else:
                o_ref[:, cols] = val.astype(o_ref.dtype)

        if slabs == "n":
            for q in range(b_ref.shape[0]):
                pq = lax.dot_general(av, b_ref[q].astype(BF16), dims, preferred_element_type=F32)
                finish(pq, slice(q * pq.shape[1], (q + 1) * pq.shape[1]))
            return
        if slabs == "k":
            kc = av.shape[1] // b_ref.shape[0]
            p = None
            for q in range(b_ref.shape[0]):
                pq = lax.dot_general(av[:, q * kc:(q + 1) * kc], b_ref[q].astype(BF16), dims,
                                     preferred_element_type=F32)
                p = pq if p is None else p + pq
        else:
            bv = b_ref[...]
            bv = bv.reshape((-1, bv.shape[-1])).astype(BF16)
            p = lax.dot_general(av, bv, dims, preferred_element_type=F32)

        if nk == 1:
            finish(p)
        else:
            acc = refs[-2] if rms_bwd is not None else refs[-1]
            k = pl.program_id(len(grid) - 1)

            @pl.when(k == 0)
            def _():
                acc[...] = p

            @pl.when(k > 0)
            def _():
                acc[...] += p

            @pl.when(k == nk - 1)
            def _():
                finish(acc[...])

    in_specs = [a_spec, b_spec, *extra_specs]
    operands = [a, b, *extras]
    aliases = {}
    if has_alias:
        in_specs.append(pl.BlockSpec(memory_space=pl.ANY))
        operands.append(alias_in)
        aliases = {len(operands) - 1: 0}
    if after is not None:
        in_specs.append(pl.BlockSpec(memory_space=pl.ANY))
        operands.append(after)
    sem = ("parallel",) * (len(grid) - 1) + (("arbitrary",) if nk > 1 else ("parallel",))
    scratch = [pltpu.VMEM(acc_shape, F32)] if nk > 1 else []
    if rms_bwd is not None:
        sem = ("arbitrary",) * len(grid)
        scratch.append(pltpu.VMEM((8, extras[0].shape[-1]), F32))
    return pl.pallas_call(
        body, name=name, grid=grid, in_specs=in_specs, out_specs=o_spec, out_shape=out_shape,
        scratch_shapes=scratch, input_output_aliases=aliases, compiler_params=_params(*sem),
    )(*operands)


def _relu_sq(v):
    r = jnp.maximum(v, 0)
    return r * r


def _rms_fwd(name, x, g, tm):
    s, d = x.shape

    def body(x_ref, g_ref, h_ref):
        xv = x_ref[...]
        r = lax.rsqrt(jnp.mean(xv * xv, axis=-1, keepdims=True) + EPS)
        h_ref[...] = (xv * r * g_ref[...]).astype(h_ref.dtype)

    return pl.pallas_call(
        body, name=name, grid=(s // tm,),
        in_specs=[pl.BlockSpec((tm, d), lambda i: (i, 0)), pl.BlockSpec((1, d), lambda i: (0, 0))],
        out_specs=pl.BlockSpec((tm, d), lambda i: (i, 0)),
        out_shape=jax.ShapeDtypeStruct((s, d), BF16), compiler_params=_params("parallel"),
    )(x, g)


def _colsum8(v):
    return jnp.sum(v.reshape(v.shape[0] // 8, 8, v.shape[1]), axis=0)


def _loss_head(x, g, target, tm):
    s, d = x.shape
    n = s // tm

    def body(x_ref, g_ref, t_ref, loss_ref, dx_ref, dx16_ref, dg_ref, acc_l, acc_g):
        i = pl.program_id(0)
        xv = x_ref[...]
        r = lax.rsqrt(jnp.mean(xv * xv, axis=-1, keepdims=True) + EPS)
        xh = xv * r
        err = xh * g_ref[...] - t_ref[...]
        dy = err * (1.0 / d)
        lpart = _colsum8(err * err)
        gpart = _colsum8(dy * xh)

        @pl.when(i == 0)
        def _():
            acc_l[...] = lpart
            acc_g[...] = gpart

        @pl.when(i > 0)
        def _():
            acc_l[...] += lpart
            acc_g[...] += gpart

        dxh = dy * g_ref[...]
        dx = r * (dxh - xh * jnp.mean(dxh * xh, axis=-1, keepdims=True))
        dx_ref[...] = dx
        dx16_ref[...] = dx.astype(BF16)

        @pl.when(i == n - 1)
        def _():
            loss_ref[...] = (0.5 / d) * jnp.sum(jnp.sum(acc_l[...], axis=0, keepdims=True), axis=1, keepdims=True)
            dg_ref[...] = jnp.sum(acc_g[...], axis=0, keepdims=True)

    return pl.pallas_call(
        body, name="loss_head", grid=(n,),
        in_specs=[pl.BlockSpec((tm, d), lambda i: (i, 0)), pl.BlockSpec((1, d), lambda i: (0, 0)),
                  pl.BlockSpec((tm, d), lambda i: (i, 0))],
        out_specs=[pl.BlockSpec((1, 1), lambda i: (0, 0)), pl.BlockSpec((tm, d), lambda i: (i, 0)),
                   pl.BlockSpec((tm, d), lambda i: (i, 0)), pl.BlockSpec((1, d), lambda i: (0, 0))],
        out_shape=[jax.ShapeDtypeStruct((1, 1), F32), jax.ShapeDtypeStruct((s, d), F32),
                   jax.ShapeDtypeStruct((s, d), BF16), jax.ShapeDtypeStruct((1, d), F32)],
        scratch_shapes=[pltpu.VMEM((8, d), F32), pltpu.VMEM((8, d), F32)], compiler_params=_params("arbitrary"),
    )(x, g, target)


def _sec(ref, n, d):
    return ref[:, n * d:(n + 1) * d].astype(F32)


def _fill_shifts(sh, ext):
    rows = ext.shape[0] - SUBLANES
    for b in range(1, SUBLANES):
        sh[b - 1, 0:rows, :] = ext[b:b + rows, :]


def _shifted(sh, ext, off, n):
    b = off % SUBLANES
    if b == 0:
        return ext[off:off + n, :]
    return sh[b - 1, off - b:off - b + n, :]


def _spread_taps(dst, w_ref):
    for k in range(w_ref.shape[0]):
        dst[k] = jnp.broadcast_to(w_ref[k:k + 1, :], dst.shape[1:])


def _times_tap(x, tap):
    return (x.reshape(x.shape[0] // SUBLANES, SUBLANES, x.shape[1]) * tap[None]).reshape(x.shape)


def _window_sums(src, s_a, s_b, gc, first, rows, back):
    n = src.shape[0]
    sign = -1 if back else 1
    lo = [SUBLANES * j if back else 0 for j in range(4)]
    hi = [n if back else n - SUBLANES * j for j in range(4)]
    sl = lambda j, shift: slice(lo[j] + shift, hi[j] + shift)
    s_a[sl(1, 0), :] = src[sl(1, 0), :] + src[sl(1, sign * 1), :]
    out = [s_a[first:first + rows, 0:gc]]
    s_b[sl(2, 0), gc:] = s_a[sl(2, 0), gc:] + s_a[sl(2, sign * 2), gc:]
    out.append(s_b[first:first + rows, gc:2 * gc])
    s_a[sl(3, 0), 2 * gc:] = s_b[sl(3, 0), 2 * gc:] + s_b[sl(3, sign * 4), 2 * gc:]
    out.append(s_a[first:first + rows, 2 * gc:3 * gc])
    out.append(s_a[first:first + rows, 3 * gc:] + s_a[first + sign * SUBLANES:first + sign * SUBLANES + rows, 3 * gc:])
    return out


def _pool_count(row0, rows, window):
    t = row0 + lax.broadcasted_iota(jnp.int32, (rows, 1), 0)
    return jnp.minimum(t + 1, window).astype(F32)


def _group_weight(w_ref, gi, gc):
    return w_ref[:, gi].reshape(gc, gc)


def _mixer_fwd(name, proj, x0, conv_a, conv_b, conv_b_bias, ln_g, ln_b, b_out_b, pool_scale, g_next, g_row, g_pool, offs,
               d, tm):
    s = proj.shape[0]
    n = s // tm
    gc = d // N_GROUPS
    hb = tm // HALO
    rd = d // N_DEV
    o_a, o_b, o_o = offs

    def body(pj_ref, hp_ref, x0_ref, ca_ref, cb_ref, cbb_ref, lng_ref, lnb_ref, bo_ref, sc_ref, gn_ref, wa_ref, wb_ref,
             wo_ref, wp_ref, pa_ref, sw_ref, pc_ref, cv_ref, ya_ref, yb_ref, pw_ref, mg_ref, x1_ref, h2_ref,
             eua, eub, euc, sh, taps_a, taps_b, sum_a, sum_b):
        i = pl.program_id(0)
        keep = (i > 0).astype(F32)
        _spread_taps(taps_a, ca_ref)
        _spread_taps(taps_b, cb_ref)
        eua[0:HALO, :] = _sec(hp_ref, 1, d) * _sec(hp_ref, 2, d) * keep
        eub[0:HALO, :] = _sec(hp_ref, 3, d) * _sigmoid(_sec(hp_ref, 4, d)) * keep
        euc[0:HALO, :] = _sec(hp_ref, 5, d) * keep
        eua[HALO:HALO + tm, :] = _sec(pj_ref, 1, d) * _sec(pj_ref, 2, d)
        eub[HALO:HALO + tm, :] = _sec(pj_ref, 3, d) * _sigmoid(_sec(pj_ref, 4, d))
        euc[HALO:HALO + tm, :] = _sec(pj_ref, 5, d)
        _fill_shifts(sh, eub)
        for c in range(tm // CHUNK):
            r0 = c * CHUNK
            z = jnp.zeros((CHUNK, d), F32)
            for k in range(K_A):
                z = z + _times_tap(eua[HALO + r0 - (K_A - 1) + k:HALO + r0 - (K_A - 1) + k + CHUNK, :], taps_a[k])
            pa_ref[r0:r0 + CHUNK, :] = (pj_ref[r0:r0 + CHUNK, 0:d].astype(F32) * z).astype(pa_ref.dtype)
            cv = jnp.zeros((CHUNK, d), F32) + cbb_ref[...]
            for k in range(K_B):
                cv = cv + _times_tap(_shifted(sh, eub, HALO + r0 - (K_B - 1) + k, CHUNK), taps_b[k])
            cv_ref[r0:r0 + CHUNK, :] = cv.astype(cv_ref.dtype)
        cvv = cv_ref[...].astype(F32)
        mu = jnp.mean(cvv, axis=-1, keepdims=True)
        xc = cvv - mu
        xh = xc * lax.rsqrt(jnp.mean(xc * xc, axis=-1, keepdims=True) + EPS)
        ln = xh * lng_ref[...] + lnb_ref[...]
        sw_ref[...] = (ln * _sigmoid(ln)).astype(sw_ref.dtype)
        sums = _window_sums(euc, sum_a, sum_b, gc, HALO, tm, back=True)
        for gi, w in enumerate(POOL_WINDOWS):
            cols = slice(gi * gc, (gi + 1) * gc)
            cnt = _pool_count(i * tm, tm, w)
            pc_ref[:, cols] = (sums[gi] / cnt - euc[HALO:HALO + tm, cols]).astype(pc_ref.dtype)
        ya_ref[...] = jnp.dot(pa_ref[...], wa_ref[...].reshape(d, d), preferred_element_type=F32).astype(ya_ref.dtype)
        yb_ref[...] = (jnp.dot(sw_ref[...], wb_ref[...].reshape(d, d), preferred_element_type=F32)
                       + bo_ref[...]).astype(yb_ref.dtype)
        for gi in range(N_GROUPS):
            cols = slice(gi * gc, (gi + 1) * gc)
            pw_ref[:, cols] = jnp.dot(pc_ref[:, cols], _group_weight(wp_ref, gi, gc),
                                      preferred_element_type=F32).astype(pw_ref.dtype)
        m = _sigmoid(_sec(pj_ref, 6, d)) * ya_ref[...].astype(F32)
        m = m + _sigmoid(_sec(pj_ref, 7, d)) * yb_ref[...].astype(F32)
        m = m + _sigmoid(_sec(pj_ref, 8, d)) * (pw_ref[...].astype(F32) * sc_ref[...])
        mg_ref[...] = m.astype(mg_ref.dtype)
        x1 = x0_ref[...] + jnp.dot(mg_ref[...], wo_ref[...].reshape(d, d), preferred_element_type=F32)
        x1_ref[...] = x1
        h2_ref[...] = (x1 * lax.rsqrt(jnp.mean(x1 * x1, axis=-1, keepdims=True) + EPS) * gn_ref[...]).astype(h2_ref.dtype)

    row = lambda i: (i, 0)
    fixed = lambda i: (0, 0)
    act = jax.ShapeDtypeStruct((s, d), BF16)

    def dd_weight(off):
        return pl.BlockSpec((N_DEV, rd, d), lambda i: (0, off, 0), pipeline_mode=pl.Buffered(1))

    return pl.pallas_call(
        body, name=name, grid=(n,),
        in_specs=[pl.BlockSpec((tm, 9 * d), row),
                  pl.BlockSpec((HALO, 6 * d), lambda i: (jnp.maximum(i * hb - 1, 0), 0)),
                  pl.BlockSpec((tm, d), row),
                  pl.BlockSpec((K_A, d), fixed), pl.BlockSpec((K_B, d), fixed), pl.BlockSpec((1, d), fixed),
                  pl.BlockSpec((1, d), fixed), pl.BlockSpec((1, d), fixed), pl.BlockSpec((1, d), fixed),
                  pl.BlockSpec((1, d), fixed), pl.BlockSpec((1, d), fixed), dd_weight(o_a), dd_weight(o_b), dd_weight(o_o),
                  pl.BlockSpec((N_DEV, N_GROUPS, gc // N_DEV, gc), lambda i: (0, 0, 0, 0),
                               pipeline_mode=pl.Buffered(1))],
        out_specs=[pl.BlockSpec((tm, d), row)] * 10,
        out_shape=[act] * 8 + [jax.ShapeDtypeStruct((s, d), F32), act],
        scratch_shapes=[pltpu.VMEM((tm + HALO, d), F32)] * 3 + [pltpu.VMEM((SUBLANES - 1, tm + HALO, d), F32),
                                                                pltpu.VMEM((K_A, SUBLANES, d), F32),
                                                                pltpu.VMEM((K_B, SUBLANES, d), F32)]
                       + [pltpu.VMEM((tm + HALO, d), F32)] * 2,
        compiler_params=_params("parallel"),
    )(proj, proj, x0, conv_a, conv_b, conv_b_bias, ln_g, ln_b, b_out_b, pool_scale, g_next, g_row, g_row, g_row,
      g_pool)


def _merge_bwd(name, dx16, proj, ya, yb, pw, pool_scale, g_row, g_pool, offs, d, tm):
    s = proj.shape[0]
    n = s // tm
    gc = d // N_GROUPS
    rd = d // N_DEV
    o_a, o_b, o_o = offs

    def body(dx_ref, g_ref, ya_ref, yb_ref, pw_ref, sc_ref, wa_ref, wb_ref, wo_ref, wp_ref,
             dya_ref, dyb_ref, dpw_ref, dg_ref, dpa_ref, dsw_ref, dpc_ref, dbo_ref, dsc_ref, acc_b, acc_s):
        i = pl.program_id(0)
        dmv = lax.dot_general(dx_ref[...], wo_ref[...].reshape(d, d), NT,
                              preferred_element_type=F32).astype(BF16).astype(F32)
        scale = sc_ref[...]
        g0 = _sigmoid(_sec(g_ref, 0, d))
        dya_ref[...] = (dmv * g0).astype(dya_ref.dtype)
        dg_ref[:, 0:d] = (dmv * ya_ref[...].astype(F32) * g0 * (1.0 - g0)).astype(dg_ref.dtype)
        g1 = _sigmoid(_sec(g_ref, 1, d))
        dyb = dmv * g1
        dyb_ref[...] = dyb.astype(dyb_ref.dtype)
        dg_ref[:, d:2 * d] = (dmv * yb_ref[...].astype(F32) * g1 * (1.0 - g1)).astype(dg_ref.dtype)
        g2 = _sigmoid(_sec(g_ref, 2, d))
        pwv = pw_ref[...].astype(F32)
        dyc = dmv * g2
        dpw_ref[...] = (dyc * scale).astype(dpw_ref.dtype)
        dg_ref[:, 2 * d:3 * d] = (dmv * (pwv * scale) * g2 * (1.0 - g2)).astype(dg_ref.dtype)
        pb = _colsum8(dyb)
        ps = _colsum8(dyc * pwv)

        @pl.when(i == 0)
        def _():
            acc_b[...] = pb
            acc_s[...] = ps

        @pl.when(i > 0)
        def _():
            acc_b[...] += pb
            acc_s[...] += ps

        dpa_ref[...] = lax.dot_general(dya_ref[...], wa_ref[...].reshape(d, d), NT,
                                       preferred_element_type=F32).astype(dpa_ref.dtype)
        dsw_ref[...] = lax.dot_general(dyb_ref[...], wb_ref[...].reshape(d, d), NT,
                                       preferred_element_type=F32).astype(dsw_ref.dtype)
        for gi in range(N_GROUPS):
            cols = slice(gi * gc, (gi + 1) * gc)
            dpc_ref[:, cols] = lax.dot_general(dpw_ref[:, cols], _group_weight(wp_ref, gi, gc), NT,
                                               preferred_element_type=F32).astype(dpc_ref.dtype)

        @pl.when(i == n - 1)
        def _():
            dbo_ref[...] = jnp.sum(acc_b[...], axis=0, keepdims=True)
            dsc_ref[...] = jnp.sum(acc_s[...], axis=0, keepdims=True)

    row = lambda i: (i, 0)
    fixed = lambda i: (0, 0)
    act = jax.ShapeDtypeStruct((s, d), BF16)
    vec = jax.ShapeDtypeStruct((1, d), F32)

    def dd_weight(off):
        return pl.BlockSpec((N_DEV, rd, d), lambda i: (0, off, 0), pipeline_mode=pl.Buffered(1))

    return pl.pallas_call(
        body, name=name, grid=(n,),
        in_specs=[pl.BlockSpec((tm, d), row), pl.BlockSpec((tm, 3 * d), lambda i: (i, 2)), pl.BlockSpec((tm, d), row),
                  pl.BlockSpec((tm, d), row), pl.BlockSpec((tm, d), row), pl.BlockSpec((1, d), fixed),
                  dd_weight(o_a), dd_weight(o_b), dd_weight(o_o),
                  pl.BlockSpec((N_DEV, N_GROUPS, gc // N_DEV, gc), lambda i: (0, 0, 0, 0),
                               pipeline_mode=pl.Buffered(1))],
        out_specs=[pl.BlockSpec((tm, d), row)] * 3 + [pl.BlockSpec((tm, 3 * d), row)] + [pl.BlockSpec((tm, d), row)] * 3
                  + [pl.BlockSpec((1, d), fixed)] * 2,
        out_shape=[act, act, act, jax.ShapeDtypeStruct((s, 3 * d), BF16), act, act, act, vec, vec],
        scratch_shapes=[pltpu.VMEM((8, d), F32)] * 2, compiler_params=_params("arbitrary"),
    )(dx16, proj, ya, yb, pw, pool_scale, g_row, g_row, g_row, g_pool)


def _mix_pre_bwd(name, proj, cv, dpa, dsw, dpc, dgates, conv_a, conv_b, ln_g, ln_b, d, tm):
    s = proj.shape[0]
    n = s // tm
    gc = d // N_GROUPS
    hb = tm // HALO
    last_halo = s // HALO - 1
    te = tm + HALO

    def ln_bwd(cvv, dswv, lng, lnb):
        mu = jnp.mean(cvv, axis=-1, keepdims=True)
        xc = cvv - mu
        rstd = lax.rsqrt(jnp.mean(xc * xc, axis=-1, keepdims=True) + EPS)
        xh = xc * rstd
        ln = xh * lng + lnb
        sg = _sigmoid(ln)
        dln = dswv * (sg * (1.0 + ln * (1.0 - sg)))
        dxh = dln * lng
        dcv = rstd * (dxh - jnp.mean(dxh, axis=-1, keepdims=True) - xh * jnp.mean(dxh * xh, axis=-1, keepdims=True))
        return dcv, dln, xh

    def body(pj_ref, hp_ref, hf_ref, cv_ref, cvf_ref, dpa_ref, dpaf_ref, dsw_ref, dswf_ref, dpc_ref, dpcf_ref, dgt_ref,
             ca_ref, cb_ref, lng_ref, lnb_ref,
             dpj_ref, dbin_ref, dca_ref, dcb_ref, dcbb_ref, dlng_ref, dlnb_ref,
             eua, eub, edz, edcv, eq, sh, dub_s, taps_a, taps_b, sum_a, sum_b, acc_bin, acc_ca, acc_cb, acc_v):
        i = pl.program_id(0)
        keep_p = (i > 0).astype(F32)
        keep_f = (i < n - 1).astype(F32)
        _spread_taps(taps_a, ca_ref)
        _spread_taps(taps_b, cb_ref)

        @pl.when(i == 0)
        def _():
            acc_bin[...] = jnp.zeros_like(acc_bin)
            acc_ca[...] = jnp.zeros_like(acc_ca)
            acc_cb[...] = jnp.zeros_like(acc_cb)
            acc_v[...] = jnp.zeros_like(acc_v)

        eua[0:HALO, :] = _sec(hp_ref, 1, d) * _sec(hp_ref, 2, d) * keep_p
        eua[HALO:te, :] = _sec(pj_ref, 1, d) * _sec(pj_ref, 2, d)
        eub[HALO:te, :] = _sec(pj_ref, 3, d) * _sigmoid(_sec(pj_ref, 4, d))
        edz[0:tm, :] = dpa_ref[...].astype(F32) * _sec(pj_ref, 0, d)
        edz[tm:te, :] = dpaf_ref[...].astype(F32) * _sec(hf_ref, 0, d) * keep_f
        dcv, dln, xh = ln_bwd(cv_ref[...].astype(F32), dsw_ref[...].astype(F32), lng_ref[...], lnb_ref[...])
        edcv[0:tm, :] = dcv
        acc_v[0:8, :] += _colsum8(dcv)
        acc_v[8:16, :] += _colsum8(dln * xh)
        acc_v[16:24, :] += _colsum8(dln)
        dcvf, _, _ = ln_bwd(cvf_ref[...].astype(F32), dswf_ref[...].astype(F32), lng_ref[...], lnb_ref[...])
        edcv[tm:te, :] = dcvf * keep_f
        for gi, w in enumerate(POOL_WINDOWS):
            cols = slice(gi * gc, (gi + 1) * gc)
            eq[0:tm, cols] = dpc_ref[:, cols].astype(F32) / _pool_count(i * tm, tm, w)
            eq[tm:te, cols] = dpcf_ref[:, cols].astype(F32) / _pool_count((i + 1) * tm, HALO, w) * keep_f

        def put(sec_idx, r0, val):
            dpj_ref[r0:r0 + CHUNK, sec_idx * d:(sec_idx + 1) * d] = val.astype(dpj_ref.dtype)
            acc_bin[:, sec_idx * d:(sec_idx + 1) * d] += _colsum8(val)

        _fill_shifts(sh, edcv)
        for k0 in range(0, K_B, TAP_GROUP):
            taps = range(k0, min(k0 + TAP_GROUP, K_B))
            a = {k: jnp.zeros((8, d), F32) for k in taps}
            for c in range(tm // CHUNK):
                r0 = c * CHUNK
                ub = eub[HALO + r0:HALO + r0 + CHUNK, :]
                part = None
                for k in taps:
                    t = _shifted(sh, edcv, r0 + (K_B - 1) - k, CHUNK)
                    part = _times_tap(t, taps_b[k]) if part is None else part + _times_tap(t, taps_b[k])
                    a[k] = a[k] + _colsum8(ub * t)
                if k0 == 0:
                    dub_s[r0:r0 + CHUNK, :] = part
                else:
                    dub_s[r0:r0 + CHUNK, :] += part
            for k in taps:
                acc_cb[k] += a[k]
        wa = [jnp.zeros((8, d), F32) for _ in range(K_A)]
        for c in range(tm // CHUNK):
            r0 = c * CHUNK
            rows = slice(r0, r0 + CHUNK)
            z = jnp.zeros((CHUNK, d), F32)
            dua = jnp.zeros((CHUNK, d), F32)
            ua = eua[HALO + r0:HALO + r0 + CHUNK, :]
            for k in range(K_A):
                z = z + _times_tap(eua[HALO + r0 - (K_A - 1) + k:HALO + r0 - (K_A - 1) + k + CHUNK, :], taps_a[k])
                t = edz[r0 + (K_A - 1) - k:r0 + (K_A - 1) - k + CHUNK, :]
                dua = dua + _times_tap(t, taps_a[k])
                wa[k] = wa[k] + _colsum8(ua * t)
            put(0, r0, dpa_ref[rows, :].astype(F32) * z)
            put(1, r0, dua * pj_ref[rows, 2 * d:3 * d].astype(F32))
            put(2, r0, dua * pj_ref[rows, d:2 * d].astype(F32))
            dub = dub_s[rows, :]
            bval = pj_ref[rows, 3 * d:4 * d].astype(F32)
            sg = _sigmoid(pj_ref[rows, 4 * d:5 * d].astype(F32))
            put(3, r0, dub * sg)
            put(4, r0, dub * bval * sg * (1.0 - sg))
        sums = _window_sums(eq, sum_a, sum_b, gc, 0, tm, back=False)
        for gi in range(N_GROUPS):
            cols = slice(gi * gc, (gi + 1) * gc)
            dci = sums[gi] - dpc_ref[:, cols].astype(F32)
            dpj_ref[:, 5 * d + gi * gc:5 * d + (gi + 1) * gc] = dci.astype(dpj_ref.dtype)
            acc_bin[:, 5 * d + gi * gc:5 * d + (gi + 1) * gc] += _colsum8(dci)
        for q in range(3):
            gv = dgt_ref[:, q * d:(q + 1) * d]
            dpj_ref[:, (6 + q) * d:(7 + q) * d] = gv
            acc_bin[:, (6 + q) * d:(7 + q) * d] += _colsum8(gv.astype(F32))
        for k in range(K_A):
            acc_ca[k] += wa[k]

        @pl.when(i == n - 1)
        def _():
            dbin_ref[...] = jnp.sum(acc_bin[...], axis=0, keepdims=True)
            for k in range(K_A):
                dca_ref[k:k + 1, :] = jnp.sum(acc_ca[k], axis=0, keepdims=True)
            for k in range(K_B):
                dcb_ref[k:k + 1, :] = jnp.sum(acc_cb[k], axis=0, keepdims=True)
            dcbb_ref[...] = jnp.sum(acc_v[0:8, :], axis=0, keepdims=True)
            dlng_ref[...] = jnp.sum(acc_v[8:16, :], axis=0, keepdims=True)
            dlnb_ref[...] = jnp.sum(acc_v[16:24, :], axis=0, keepdims=True)

    row = lambda i: (i, 0)
    fixed = lambda i: (0, 0)
    past = lambda i: (jnp.maximum(i * hb - 1, 0), 0)
    fut = lambda i: (jnp.minimum((i + 1) * hb, last_halo), 0)
    vec = jax.ShapeDtypeStruct((1, d), F32)
    tile_and_halo = [pl.BlockSpec((tm, d), row), pl.BlockSpec((HALO, d), fut)]
    return pl.pallas_call(
        body, name=name, grid=(n,),
        in_specs=[pl.BlockSpec((tm, 6 * d), row), pl.BlockSpec((HALO, 6 * d), past), pl.BlockSpec((HALO, 6 * d), fut),
                  *tile_and_halo, *tile_and_halo, *tile_and_halo, *tile_and_halo,
                  pl.BlockSpec((tm, 3 * d), row),
                  pl.BlockSpec((K_A, d), fixed), pl.BlockSpec((K_B, d), fixed), pl.BlockSpec((1, d), fixed),
                  pl.BlockSpec((1, d), fixed)],
        out_specs=[pl.BlockSpec((tm, 9 * d), row), pl.BlockSpec((1, 9 * d), fixed), pl.BlockSpec((K_A, d), fixed),
                   pl.BlockSpec((K_B, d), fixed), pl.BlockSpec((1, d), fixed), pl.BlockSpec((1, d), fixed),
                   pl.BlockSpec((1, d), fixed)],
        out_shape=[jax.ShapeDtypeStruct((s, 9 * d), BF16), jax.ShapeDtypeStruct((1, 9 * d), F32),
                   jax.ShapeDtypeStruct((K_A, d), F32), jax.ShapeDtypeStruct((K_B, d), F32), vec, vec, vec],
        scratch_shapes=[pltpu.VMEM((te, d), F32)] * 5 + [pltpu.VMEM((SUBLANES - 1, te, d), F32),
                                                         pltpu.VMEM((tm, d), F32),
                                                         pltpu.VMEM((K_A, SUBLANES, d), F32),
                                                         pltpu.VMEM((K_B, SUBLANES, d), F32),
                                                         pltpu.VMEM((te, d), F32), pltpu.VMEM((te, d), F32),
                                                         pltpu.VMEM((8, 9 * d), F32), pltpu.VMEM((K_A, 8, d), F32),
                                                         pltpu.VMEM((K_B, 8, d), F32), pltpu.VMEM((24, d), F32)],
        compiler_params=_params("arbitrary"),
    )(proj, proj, proj, cv, cv, dpa, dpa, dsw, dsw, dpc, dpc, dgates, conv_a, conv_b, ln_g, ln_b)


def _pool_wgrad(name, p, dpw, d, tk):
    s = p.shape[0]
    gc = d // N_GROUPS
    n = s // tk

    def body(p_ref, g_ref, o_ref, acc):
        k = pl.program_id(0)
        for gi in range(N_GROUPS):
            cols = slice(gi * gc, (gi + 1) * gc)
            part = lax.dot_general(p_ref[:, cols], g_ref[:, cols], TN, preferred_element_type=F32)

            @pl.when(k == 0)
            def _():
                acc[gi] = part

            @pl.when(k > 0)
            def _():
                acc[gi] += part

        @pl.when(k == n - 1)
        def _():
            for gi in range(N_GROUPS):
                o_ref[:, gi] = acc[gi].astype(o_ref.dtype).reshape(N_DEV, gc // N_DEV, gc)

    return pl.pallas_call(
        body, name=name, grid=(n,),
        in_specs=[pl.BlockSpec((tk, d), lambda k: (k, 0)), pl.BlockSpec((tk, d), lambda k: (k, 0))],
        out_specs=pl.BlockSpec((N_DEV, N_GROUPS, gc // N_DEV, gc), lambda k: (0, 0, 0, 0)),
        out_shape=jax.ShapeDtypeStruct((N_DEV, N_GROUPS, gc // N_DEV, gc), BF16),
        scratch_shapes=[pltpu.VMEM((N_GROUPS, gc, gc), F32)], compiler_params=_params("arbitrary"),
    )(p, dpw)


def _my_place():
    x, y, c = lax.axis_index("x"), lax.axis_index("y"), lax.axis_index("c")
    return x, y, c


def _block_of(x, y, c):
    return 4 * x + 2 * y + c


def _slot(ref, k, paired):
    if not paired:
        return ref.at[k]
    cols = ref.shape[-1] // 2
    return ref.at[k // 2, :, pl.ds(pl.multiple_of((k % 2) * cols, 128), cols)]


def _slot_shape(shape, paired):
    return (N_DEV // 2, shape[0], 2 * shape[1]) if paired else (N_DEV, *shape)


def _gather_shards(shards, paired):
    n_arr = len(shards)

    def body(*refs):
        srcs = refs[:n_arr]
        outs = refs[n_arr:2 * n_arr]
        send_sems, recv_sems, local_sems = refs[2 * n_arr:]
        x, y, c = _my_place()
        me, sibling = (x, y, c), (x, y, 1 - c)
        chips = [(1 - x, y), (x, 1 - y), (1 - x, 1 - y)]

        def copy(n, k, block, to, src=None):
            rows = _slot(outs[n], _block_of(*block), paired[n])
            return pltpu.make_async_remote_copy(
                src_ref=rows if src is None else src, dst_ref=rows, send_sem=send_sems.at[n, k],
                recv_sem=recv_sems.at[n, k], device_id=to, device_id_type=MESH)

        mine = [pltpu.make_async_copy(srcs[n], _slot(outs[n], _block_of(*me), paired[n]), local_sems.at[n])
                for n in range(n_arr)]
        for cp in mine:
            cp.start()
        first = []
        for n in range(n_arr):
            first.append(copy(n, 0, me, sibling, src=srcs[n]))
            first += [copy(n, 1 + j, me, (*chip, c), src=srcs[n]) for j, chip in enumerate(chips)]
        for cp in first:
            cp.start()
        passed = []
        for n in range(n_arr):
            for j, chip in enumerate(chips):
                copy(n, 1 + j, (*chip, c), me).wait_recv()
                fwd = copy(n, 4 + j, (*chip, c), sibling)
                fwd.start()
                passed.append(fwd)
        for n in range(n_arr):
            copy(n, 0, sibling, me).wait_recv()
            for j, chip in enumerate(chips):
                copy(n, 4 + j, (*chip, 1 - c), me).wait_recv()
        for cp in first + passed:
            cp.wait_send()
        for cp in mine:
            cp.wait()

    any_spec = pl.BlockSpec(memory_space=pl.ANY)
    return pl.pallas_call(
        body, name="gather_weights",
        in_specs=[any_spec] * n_arr, out_specs=[any_spec] * n_arr,
        out_shape=[jax.ShapeDtypeStruct(_slot_shape(sh.shape, p), sh.dtype) for sh, p in zip(shards, paired)],
        scratch_shapes=[pltpu.SemaphoreType.DMA((n_arr, 7)), pltpu.SemaphoreType.DMA((n_arr, 7)),
                        pltpu.SemaphoreType.DMA((n_arr,))],
    )(*shards)


def _peers(x, y, c):
    out = []
    for r in range(1, N_DEV):
        fx, fy, fc = (r >> 2) & 1, (r >> 1) & 1, r & 1
        out.append(((1 - x) if fx else x, (1 - y) if fy else y, (1 - c) if fc else c))
    return out


HBM_SPEC = pl.BlockSpec(memory_space=pltpu.HBM)
SEM_SPEC = pl.BlockSpec(memory_space=pltpu.SEMAPHORE)
ANY_SPEC = pl.BlockSpec(memory_space=pl.ANY)
N_PEERS = N_DEV - 1


def _peer_copy(src_ref, land_ref, send_sems, recv_sems, i, r, peer, me, blockwise, paired):
    src = _slot(src_ref, _block_of(*peer), paired) if blockwise else src_ref
    dst = land_ref.at[me] if blockwise else _slot(land_ref, me, paired)
    return pltpu.make_async_remote_copy(
        src_ref=src, dst_ref=dst, send_sem=send_sems.at[i * N_PEERS + r],
        recv_sem=recv_sems.at[i * N_PEERS + r], device_id=peer, device_id_type=MESH)


def _block_shape(shape, paired):
    return (shape[1], shape[2] // 2) if paired else tuple(shape[1:])


def _start_copies(name, srcs, after, blockwise, paired=None):
    n = len(srcs)
    paired = paired or [False] * n

    def body(*refs):
        s_in, l_in = refs[:n], refs[n:2 * n]
        send_sems, recv_sems = refs[2 * n + 1], refs[2 * n + 2]
        token = refs[-1]
        x, y, c = _my_place()
        me = _block_of(x, y, c)
        for i in range(n):
            for r, peer in enumerate(_peers(x, y, c)):
                _peer_copy(s_in[i], l_in[i], send_sems, recv_sems, i, r, peer, me, blockwise, paired[i]).start()
        token[...] = jnp.zeros_like(token)

    land_shapes = [(N_DEV, *_block_shape(s.shape, p)) if blockwise else _slot_shape(s.shape, p)
                   for s, p in zip(srcs, paired)]
    lands = [pltpu.with_memory_space_constraint(lax.empty(sh, s.dtype), pltpu.HBM) for sh, s in zip(land_shapes, srcs)]
    ins = [pltpu.with_memory_space_constraint(s, pltpu.HBM) for s in srcs]
    out = pl.pallas_call(
        body, name=name,
        out_shape=(pltpu.SemaphoreType.DMA((n * N_PEERS,)), pltpu.SemaphoreType.DMA((n * N_PEERS,)),
                   *[pltpu.HBM(s.shape, s.dtype) for s in srcs],
                   *[pltpu.HBM(sh, s.dtype) for sh, s in zip(land_shapes, srcs)],
                   jax.ShapeDtypeStruct((8, 128), F32)),
        in_specs=[HBM_SPEC] * (2 * n) + [ANY_SPEC],
        out_specs=(SEM_SPEC, SEM_SPEC, *[HBM_SPEC] * (2 * n), pl.BlockSpec(memory_space=pltpu.VMEM)),
        input_output_aliases={i: 2 + i for i in range(2 * n)},
        compiler_params=pltpu.CompilerParams(has_side_effects=pltpu.SideEffectType.DATAFLOW_SIDE_EFFECTING),
    )(*ins, *lands, after)
    return dict(send=out[0], recv=out[1], srcs=list(out[2:2 + n]), lands=list(out[2 + n:2 + 2 * n]), token=out[-1],
                paired=paired)


def _wait_copies(name, state, after, blockwise):
    n = len(state["srcs"])
    paired = state["paired"]

    def body(*refs):
        s_in, l_in = refs[:n], refs[n:2 * n]
        send_sems, recv_sems = refs[2 * n], refs[2 * n + 1]
        x, y, c = _my_place()
        me = _block_of(x, y, c)
        for i in range(n):
            for r, peer in enumerate(_peers(x, y, c)):
                cp = _peer_copy(s_in[i], l_in[i], send_sems, recv_sems, i, r, peer, me, blockwise, paired[i])
                cp.wait_send()
                cp.wait_recv()

    both = state["srcs"] + state["lands"]
    out = pl.pallas_call(
        body, name=name, out_shape=tuple(pltpu.HBM(a.shape, a.dtype) for a in both),
        in_specs=[HBM_SPEC] * (2 * n) + [SEM_SPEC, SEM_SPEC, ANY_SPEC], out_specs=tuple([HBM_SPEC] * (2 * n)),
        input_output_aliases={i: i for i in range(2 * n)},
        compiler_params=pltpu.CompilerParams(has_side_effects=pltpu.SideEffectType.DATAFLOW_SIDE_EFFECTING),
    )(*both, state["send"], state["recv"], after)
    return list(out[:n]), list(out[n:])


COPY_BLOCK_BYTES = 2 * 1024 * 1024


def _place_own(name, lands, srcs, me, blockwise, paired=None):
    out = []
    paired = paired or [False] * len(lands)
    for i, (land, src) in enumerate(zip(lands, srcs)):
        in_slots = src if blockwise else land
        part = _block_shape(in_slots.shape, paired[i])
        row_bytes = land.dtype.itemsize
        for extent in part[1:]:
            row_bytes *= extent
        tr = part[0]
        while tr * row_bytes > COPY_BLOCK_BYTES and tr % 16 == 0:
            tr //= 2
        tail = (0,) * (len(part) - 1)

        def body(me_ref, s_ref, l_ref, o_ref):
            o_ref[...] = s_ref[...]

        if paired[i]:
            slot_spec = pl.BlockSpec((None, tr, part[1]), lambda j, me_ref: (me_ref[0] // 2, j, me_ref[0] % 2))
        else:
            slot_spec = pl.BlockSpec((None, tr, *part[1:]), lambda j, me_ref: (me_ref[0], j, *tail))
        if blockwise:
            s_spec = slot_spec
            o_spec = pl.BlockSpec((None, tr, *part[1:]), lambda j, me_ref: (me_ref[0], j, *tail))
        else:
            s_spec = pl.BlockSpec((tr, *part[1:]), lambda j, me_ref: (j, *tail))
            o_spec = slot_spec
        out.append(pl.pallas_call(
            body, name=f"{name}_{i}",
            grid_spec=pltpu.PrefetchScalarGridSpec(
                num_scalar_prefetch=1, grid=(part[0] // tr,), in_specs=[s_spec, ANY_SPEC], out_specs=o_spec),
            out_shape=jax.ShapeDtypeStruct(land.shape, land.dtype), input_output_aliases={2: 0},
            compiler_params=_params("parallel"),
        )(me, src, land))
    return out


def _adamw_math(w, g, m, v):
    m = ADAM_B1 * m + (1.0 - ADAM_B1) * g
    v = ADAM_B2 * v + (1.0 - ADAM_B2) * (g * g)
    m_hat = m / (1.0 - ADAM_B1 ** ADAM_STEP)
    v_hat = v / (1.0 - ADAM_B2 ** ADAM_STEP)
    delta = -ADAM_LR * (m_hat / (jnp.sqrt(v_hat) + ADAM_EPS) + ADAM_WD * w)
    return delta, m, v


def _adamw(name, parts, w, m, v, *, grid, part_specs, w_spec):
    n_layers = len(parts)
    n_parts = parts[0].shape[0]

    def body(*refs):
        p_refs = refs[:n_layers]
        w_ref, m_ref, v_ref, g_ref, d_ref, nm_ref, nv_ref = refs[n_layers:]

        def total(p_ref):
            t = p_ref[0].astype(F32)
            for k in range(1, n_parts):
                t = t + p_ref[k].astype(F32)
            return t

        g = total(p_refs[0])
        for li in range(1, n_layers):
            g = jnp.where(pl.program_id(0) == li, total(p_refs[li]), g)
        delta, nm, nv = _adamw_math(w_ref[...], g, m_ref[...], v_ref[...])
        g_ref[...] = g
        d_ref[...] = delta
        nm_ref[...] = nm
        nv_ref[...] = nv

    out = jax.ShapeDtypeStruct(w.shape, F32)
    return pl.pallas_call(
        body, name=name, grid=grid, in_specs=[*part_specs, w_spec, w_spec, w_spec], out_specs=[w_spec] * 4,
        out_shape=[out] * 4, compiler_params=_params(*(("parallel",) * len(grid))),
    )(*parts, w, m, v)


def _layer_part_spec(layer, block, n_blocks, row_off=0):
    def index_map(l, i):
        ii = jnp.where(l == layer, i, jnp.where(l < layer, 0, n_blocks - 1))
        return (0, row_off + ii) + (0,) * (len(block) - 2)
    return pl.BlockSpec(block, index_map)


def _small_update(partials, triples, conv_rows):
    d = partials[-1].shape[-1]
    n_rep = len(triples)
    n_part = len(partials)
    rows = []
    for p in partials:
        rows.append(p.shape[0] * (p.shape[1] // d))
    offs = [sum(rows[:i]) for i in range(n_part)]
    total = -(-sum(rows) // 8) * 8

    def body(*refs):
        p_refs = refs[:n_part]
        wmv = refs[n_part:n_part + 3 * n_rep]
        outs = refs[n_part + 3 * n_rep:n_part + 3 * n_rep + 4 * n_rep + (n_part - n_rep)]
        buf, send_sems, recv_sems = refs[-3:]
        x, y, c = _my_place()
        me = _block_of(x, y, c)
        peers = _peers(x, y, c)
        mine = buf.at[me]
        if total > sum(rows):
            mine[sum(rows):total, :] = jnp.zeros((total - sum(rows), d), F32)
        for p_ref, off in zip(p_refs, offs):
            nr, nc = p_ref.shape[0], p_ref.shape[1] // d
            if nc == 1:
                mine[off:off + nr, :] = p_ref[...]
            else:
                for r in range(nr):
                    for q in range(nc):
                        mine[off + r * nc + q:off + r * nc + q + 1, :] = p_ref[r:r + 1, q * d:(q + 1) * d]
        sends =[pltpu.make_async_remote_copy(
            src_ref=buf.at[me], dst_ref=buf.at[me], send_sem=send_sems.at[r], recv_sem=recv_sems.at[r],
            device_id=peer, device_id_type=MESH) for r, peer in enumerate(peers)]
        for cp in sends:
            cp.start()
        for r, peer in enumerate(peers):
            pltpu.make_async_remote_copy(
                src_ref=buf.at[me], dst_ref=buf.at[_block_of(*peer)], send_sem=send_sems.at[r],
                recv_sem=recv_sems.at[r], device_id=peer, device_id_type=MESH).wait_recv()
        for cp in sends:
            cp.wait_send()
        tot = buf[0]
        for k in range(1, N_DEV):
            tot = tot + buf[k]
        buf[0] = tot
        for idx in range(n_part):
            nr, nc = p_refs[idx].shape[0], p_refs[idx].shape[1] // d
            if idx < n_rep:
                w_ref, m_ref, v_ref = wmv[3 * idx:3 * idx + 3]
                g_ref, d_ref, nm_ref, nv_ref = outs[4 * idx:4 * idx + 4]
            else:
                g_ref = outs[4 * n_rep + idx - n_rep]
            pieces = [(slice(0, nr), slice(0, d), offs[idx], nr)] if nc == 1 else [
                (slice(r, r + 1), slice(q * d, (q + 1) * d), offs[idx] + r * nc + q, 1)
                for r in range(nr) for q in range(nc)]
            for rws, cols, row, cnt in pieces:
                g = buf[0, row:row + cnt, :]
                g_ref[rws, cols] = g
                if idx < n_rep:
                    delta, nm, nv = _adamw_math(w_ref[rws, cols], g, m_ref[rws, cols], v_ref[rws, cols])
                    d_ref[rws, cols] = delta
                    nm_ref[rws, cols] = nm
                    nv_ref[rws, cols] = nv

    vm = pl.BlockSpec(memory_space=pltpu.VMEM)
    operands = list(partials)
    for t in triples:
        operands += list(t)
    out_shape = []
    for idx in range(n_rep):
        out_shape += [jax.ShapeDtypeStruct(partials[idx].shape, F32)] * 4
    for idx in range(n_rep, n_part):
        out_shape.append(jax.ShapeDtypeStruct(partials[idx].shape, F32))
    return pl.pallas_call(
        body, name="small_allreduce_adamw", in_specs=[vm] * len(operands), out_specs=[vm] * len(out_shape),
        out_shape=out_shape,
        scratch_shapes=[pltpu.VMEM((N_DEV, total, d), F32), pltpu.SemaphoreType.DMA((7,)), pltpu.SemaphoreType.DMA((7,))],
        compiler_params=pltpu.CompilerParams(vmem_limit_bytes=VMEM_LIMIT_BYTES),
    )(*operands)


def kernel(x, g_mix, w_in, b_in, conv_a, w_out_a, conv_b, conv_b_bias, ln_b_g, ln_b_b, w_out_b, b_out_b, w_pool, pool_scale, w_o, g_mlp, w_mlp1, w_mlp2, g_final, loss_target, m_g_mix, m_w_in, m_b_in, m_conv_a, m_w_out_a, m_conv_b, m_conv_b_bias, m_ln_b_g, m_ln_b_b, m_w_out_b, m_b_out_b, m_w_pool, m_pool_scale, m_w_o, m_g_mlp, m_w_mlp1, m_w_mlp2, m_g_final, v_g_mix, v_w_in, v_b_in, v_conv_a, v_w_out_a, v_conv_b, v_conv_b_bias, v_ln_b_g, v_ln_b_b, v_w_out_b, v_b_out_b, v_w_pool, v_pool_scale, v_w_o, v_g_mlp, v_w_mlp1, v_w_mlp2, v_g_final):
    _, s, d = x.shape
    n_layers = g_mix.shape[0]
    p_in = b_in.shape[1]
    ci = w_in.shape[2]
    c1 = w_mlp1.shape[2]
    rf = w_mlp2.shape[1]
    rd = w_out_a.shape[1]
    f = rf * N_DEV
    rp = rf + 3 * rd
    o_a, o_b, o_o = rf // rd, rf // rd + 1, rf // rd + 2
    gc = d // N_GROUPS
    ca_rows = 8
    tm = min(1024, s)
    tr = min(512, s)
    tx = min(256, s)
    tk = min(2048, s)
    tk_mlp = min(4096, s)
    tk_in = min(2048, s)

    me_arr = jnp.reshape(_block_of(*_my_place()), (1,)).astype(jnp.int32)

    def layer_shards(l):
        row_pack = jnp.concatenate([w_mlp2[l], w_out_a[l], w_out_b[l], w_o[l]], axis=0).astype(BF16)
        return [w_in[l].astype(BF16), w_mlp1[l].astype(BF16), row_pack, w_pool[l].astype(BF16)]

    conv_pack = jnp.concatenate(
        [conv_a, jnp.zeros((n_layers, ca_rows - K_A, rd), F32), conv_b], axis=1)
    first_shards = layer_shards(0)
    layer_pairing = [True, False, False, False]
    g_in_first, g_conv = _gather_shards([first_shards[0], conv_pack], [True, False])
    conv_full = jnp.transpose(g_conv, (1, 2, 0, 3)).reshape(n_layers, ca_rows + K_B, d)
    conv_a_f = conv_full[:, :K_A]
    conv_b_f = conv_full[:, ca_rows:]
    first_row_going = _start_copies("gather_start_row_0", first_shards[2:], g_conv, blockwise=False)
    in_flight = [_start_copies("gather_start_mlp1_0", first_shards[1:2], first_row_going["token"], blockwise=False)]
    for l in range(1, n_layers):
        in_flight.append(_start_copies(f"gather_start_{l}", layer_shards(l), in_flight[-1]["token"], blockwise=False,
                                       paired=layer_pairing))
    token = in_flight[-1]["token"][0:1, 0:1]

    xs = [x[0]]
    saved = []
    weights = []
    row2 = lambda j, i: (i, 0)
    for l in range(n_layers):
        x0 = xs[-1]
        vec = lambda a: a[l:l + 1]
        if l > 0:
            srcs, lands = _wait_copies(f"gather_wait_{l}", in_flight[l], x0, blockwise=False)
            g_in, g_1, g_row, g_pool = _place_own(f"gather_own_{l}", lands, srcs, me_arr, blockwise=False,
                                                  paired=layer_pairing)
        else:
            g_in = g_in_first
        h = _rms_fwd(f"rms_mix_{l}", x0, vec(g_mix) + token if l == 0 else vec(g_mix), tr)
        proj = _mm(
            f"proj_{l}", h, g_in, grid=(N_DEV // 2, s // tm), a_spec=pl.BlockSpec((tm, d), row2),
            b_spec=pl.BlockSpec((None, d, 2 * ci), lambda j, i: (j, 0, 0)),
            extras=(vec(b_in),), extra_specs=(pl.BlockSpec((1, 2 * ci), lambda j, i: (0, j)),),
            epilogue=lambda v, b: v + b, out_shape=jax.ShapeDtypeStruct((s, p_in), BF16),
            o_spec=pl.BlockSpec((tm, 2 * ci), lambda j, i: (i, j)), dims=NN)
        if l == 0:
            srcs, lands = _wait_copies("gather_wait_row_0", first_row_going, proj, blockwise=False)
            g_row, g_pool = _place_own("gather_own_row_0", lands, srcs, me_arr, blockwise=False)
        p_a, sw, p_c, cv, y_a, y_b, pw, merged, x1, h2 = _mixer_fwd(
            f"mix_fwd_{l}", proj, x0, conv_a_f[l], conv_b_f[l], vec(conv_b_bias), vec(ln_b_g), vec(ln_b_b),
            vec(b_out_b), vec(pool_scale), vec(g_mlp), g_row, g_pool, (o_a, o_b, o_o), d, tx)
        if l == 0:
            srcs, lands = _wait_copies("gather_wait_mlp1_0", in_flight[0], x1, blockwise=False)
            g_1, = _place_own("gather_own_mlp1_0", lands, srcs, me_arr, blockwise=False)
        weights.append((g_in, g_1, g_row, g_pool))
        a_pre = _mm(f"mlp1_{l}", h2, g_1, grid=(N_DEV // 2, s // tm), a_spec=pl.BlockSpec((tm, d), row2),
                    b_spec=pl.BlockSpec((2, d, c1), lambda j, i: (j, 0, 0)), slabs="n",
                    out_shape=jax.ShapeDtypeStruct((s, f), BF16),
                    o_spec=pl.BlockSpec((tm, 2 * c1), lambda j, i: (i, j)), dims=NN)
        x2 = _mm(f"mlp2_{l}", a_pre, g_row, grid=(1, s // tr), a_spec=pl.BlockSpec((tr, f), row2),
                 b_spec=pl.BlockSpec((N_DEV, rf, d), lambda j, i: (0, 0, 0)), prologue=_relu_sq,
                 extras=(x1,), extra_specs=(pl.BlockSpec((tr, d), row2),), epilogue=lambda v, r: v + r,
                 out_shape=jax.ShapeDtypeStruct((s, d), F32), o_spec=pl.BlockSpec((tr, d), row2), dims=NN)
        saved.append((x0, h, proj, p_a, sw, p_c, cv, y_a, y_b, pw, merged, x1, h2, a_pre))
        xs.append(x2)

    loss_part, dx, dx16, dg_final = _loss_head(xs[-1], g_final.reshape(1, d), loss_target[0], tr)
    loss = lax.psum(loss_part[0, 0], ("x", "y", "c"))

    small = [None] * n_layers
    exchanges = [None] * n_layers
    for l in reversed(range(n_layers)):
        x0, h, proj, p_a, sw, p_c, cv, y_a, y_b, pw, merged, x1, h2, a_pre = saved[l]
        g_in, g_1, g_row, g_pool = weights[l]
        vec = lambda a: a[l:l + 1]
        row_shape = jax.ShapeDtypeStruct((N_DEV, rp, d), BF16)

        def dd_grad(name, a, g, off, alias):
            return _mm(name, a, g, grid=(1, s // tk), a_spec=pl.BlockSpec((tk, d), lambda j, k: (k, 0)),
                       b_spec=pl.BlockSpec((tk, d), lambda j, k: (k, 0)), out_shape=row_shape,
                       o_spec=pl.BlockSpec((N_DEV, rd, d), lambda j, k: (0, off, 0)), dims=TN, nk=s // tk,
                       acc_shape=(d, d), alias_in=alias)

        d_a = _mm(f"d_act_{l}", dx16, g_row, grid=(N_DEV // 2, s // tm), a_spec=pl.BlockSpec((tm, d), row2),
                  b_spec=pl.BlockSpec((2, rf, d), lambda j, i: (j, 0, 0)), slabs="n",
                  extras=(a_pre,), extra_specs=(pl.BlockSpec((tm, 2 * rf), lambda j, i: (i, j)),),
                  epilogue=lambda v, a: v * (2.0 * jnp.maximum(a.astype(F32), 0.0)),
                  out_shape=jax.ShapeDtypeStruct((s, f), BF16),
                  o_spec=pl.BlockSpec((tm, 2 * rf), lambda j, i: (i, j)), dims=NT)
        dg_row = _mm(f"dw_mlp2_{l}", a_pre, dx16, grid=(N_DEV, s // tk_mlp),
                     a_spec=pl.BlockSpec((tk_mlp, rf), lambda j, k: (k, j)),
                     b_spec=pl.BlockSpec((tk_mlp, d), lambda j, k: (k, 0)), prologue=_relu_sq, out_shape=row_shape,
                     o_spec=pl.BlockSpec((None, rf, d), lambda j, k: (j, 0, 0)), dims=TN, nk=s // tk_mlp,
                     acc_shape=(rf, d))
        dg_1 = _mm(f"dw_mlp1_{l}", h2, d_a, grid=(N_DEV, s // tk_mlp),
                   a_spec=pl.BlockSpec((tk_mlp, d), lambda j, k: (k, 0)),
                   b_spec=pl.BlockSpec((tk_mlp, c1), lambda j, k: (k, j)),
                   out_shape=jax.ShapeDtypeStruct((N_DEV, d, c1), BF16),
                   o_spec=pl.BlockSpec((None, d, c1), lambda j, k: (j, 0, 0)), dims=TN, nk=s // tk_mlp,
                   acc_shape=(d, c1))
        mlp1_going = _start_copies(f"grads_start_mlp1_{l}", [dg_1], vec(g_mlp), blockwise=True)
        stream = [jax.ShapeDtypeStruct((s, d), F32), jax.ShapeDtypeStruct((s, d), BF16), jax.ShapeDtypeStruct((1, d), F32)]
        dx, dx16, dg_mlp = _mm(
            f"d_h2_{l}", d_a, g_1, grid=(1, s // tr), a_spec=pl.BlockSpec((tr, f), row2),
            b_spec=pl.BlockSpec((N_DEV, d, c1), lambda j, i: (0, 0, 0), pipeline_mode=pl.Buffered(1)), slabs="k",
            extras=(x1, vec(g_mlp), dx),
            extra_specs=(pl.BlockSpec((tr, d), row2), pl.BlockSpec((1, d), lambda j, i: (0, 0)),
                         pl.BlockSpec((tr, d), row2)),
            out_shape=stream, o_spec=[pl.BlockSpec((tr, d), row2), pl.BlockSpec((tr, d), row2),
                                      pl.BlockSpec((1, d), lambda j, i: (0, 0))],
            dims=NT, rms_bwd=(1, s // tr), after=mlp1_going["token"])
        d_ya, d_yb, d_pw, d_gates, d_pa, d_sw, d_pc, d_bout, d_pscale = _merge_bwd(
            f"merge_bwd_{l}", dx16, proj, y_a, y_b, pw, vec(pool_scale), g_row, g_pool, (o_a, o_b, o_o), d, tx)
        dg_row = dd_grad(f"dw_o_{l}", merged, dx16, o_o, dg_row)
        dg_row = dd_grad(f"dw_out_a_{l}", p_a, d_ya, o_a, dg_row)
        dg_row = dd_grad(f"dw_out_b_{l}", sw, d_yb, o_b, dg_row)
        dg_pool = _pool_wgrad(f"dw_pool_{l}", p_c, d_pw, d, tk)
        rest_going = _start_copies(f"grads_start_rest_{l}", [dg_row, dg_pool], vec(g_mlp), blockwise=True)
        d_proj, d_bin, d_ca, d_cb, d_cbb, d_lng, d_lnb = _mix_pre_bwd(
            f"mix_bwd_{l}", proj, cv, d_pa, d_sw, d_pc, d_gates, conv_a_f[l], conv_b_f[l],
            vec(ln_b_g) + rest_going["token"][0:1, 0:1], vec(ln_b_b), d, tx)
        dg_in = _mm(f"dw_in_{l}", h, d_proj, grid=(N_DEV // 2, s // tk_in),
                    a_spec=pl.BlockSpec((tk_in, d), lambda j, k: (k, 0)),
                    b_spec=pl.BlockSpec((tk_in, 2 * ci), lambda j, k: (k, j)),
                    out_shape=jax.ShapeDtypeStruct((N_DEV // 2, d, 2 * ci), BF16),
                    o_spec=pl.BlockSpec((None, d, 2 * ci), lambda j, k: (j, 0, 0)), dims=TN, nk=s // tk_in,
                    acc_shape=(d, 2 * ci), after=rest_going["token"])
        in_going = _start_copies(f"grads_start_in_{l}", [dg_in], vec(g_mix), blockwise=True, paired=[True])
        rows_ik = lambda i, k: (i, 0)
        once = dict(pipeline_mode=pl.Buffered(1))
        dx, dx16, dg_mix = _mm(
            f"d_h_{l}", d_proj, g_in, grid=(s // tm, N_DEV // 2), a_spec=pl.BlockSpec((tm, 2 * ci), lambda i, k: (i, k)),
            b_spec=pl.BlockSpec((None, d, 2 * ci), lambda i, k: (k, 0, 0)),
            extras=(x0, vec(g_mix), dx),
            extra_specs=(pl.BlockSpec((tm, d), rows_ik, **once), pl.BlockSpec((1, d), lambda i, k: (0, 0)),
                         pl.BlockSpec((tm, d), rows_ik, **once)),
            out_shape=stream, o_spec=[pl.BlockSpec((tm, d), rows_ik), pl.BlockSpec((tm, d), rows_ik),
                                      pl.BlockSpec((1, d), lambda i, k: (0, 0))],
            dims=NT, nk=N_DEV // 2, acc_shape=(tm, d), rms_bwd=(0, s // tm), after=in_going["token"])
        small[l] = (dg_mix, d_bin, d_cbb, d_lng, d_lnb, d_bout, d_pscale, dg_mlp, d_ca, d_cb)
        exchanges[l] = (in_going, mlp1_going, rest_going)

    grad_x = dx[None]

    names = ("g_mix", "b_in", "conv_b_bias", "ln_b_g", "ln_b_b", "b_out_b", "pool_scale", "g_mlp")
    given = dict(g_mix=(g_mix, m_g_mix, v_g_mix), b_in=(b_in, m_b_in, v_b_in),
                 conv_b_bias=(conv_b_bias, m_conv_b_bias, v_conv_b_bias), ln_b_g=(ln_b_g, m_ln_b_g, v_ln_b_g),
                 ln_b_b=(ln_b_b, m_ln_b_b, v_ln_b_b), b_out_b=(b_out_b, m_b_out_b, v_b_out_b),
                 pool_scale=(pool_scale, m_pool_scale, v_pool_scale), g_mlp=(g_mlp, m_g_mlp, v_g_mlp))
    partials, triples = [], []
    for i, nm in enumerate(names):
        partials.append(jnp.concatenate([small[l][i] for l in range(n_layers)], axis=0))
        triples.append(given[nm])
    partials.append(dg_final)
    triples.append(tuple(a.reshape(1, d) for a in (g_final, m_g_final, v_g_final)))
    partials.append(jnp.concatenate([small[l][8] for l in range(n_layers)], axis=0))
    partials.append(jnp.concatenate([small[l][9] for l in range(n_layers)], axis=0))
    outs = _small_update(partials, triples, 2)
    rep = {nm: outs[4 * i:4 * i + 4] for i, nm in enumerate(names)}
    rep["g_final"] = [a.reshape(d) for a in outs[4 * len(names):4 * len(names) + 4]]
    me = _block_of(*_my_place())
    gca = lax.dynamic_slice_in_dim(outs[-2].reshape(n_layers, K_A, d), me * rd, rd, axis=2)
    gcb = lax.dynamic_slice_in_dim(outs[-1].reshape(n_layers, K_B, d), me * rd, rd, axis=2)

    r_in, r_1, r_row, r_pool = [], [], [], []
    for l in reversed(range(n_layers)):
        in_going, mlp1_going, rest_going = exchanges[l]
        srcs_m, lands_m = _wait_copies(f"grads_wait_mlp1_{l}", mlp1_going, outs[0], blockwise=True)
        srcs_r, lands_r = _wait_copies(f"grads_wait_rest_{l}", rest_going, outs[0], blockwise=True)
        srcs_i, lands_i = _wait_copies(f"grads_wait_in_{l}", in_going, outs[0], blockwise=True)
        got = _place_own(f"grads_own_{l}", lands_i + lands_m + lands_r, srcs_i + srcs_m + srcs_r, me_arr, blockwise=True,
                         paired=layer_pairing)
        for lst, arr in zip((r_in, r_1, r_row, r_pool), got):
            lst.insert(0, arr)
    tb = min(256, d)
    layers = range(n_layers)
    res = {}
    res["w_in"] = _adamw("adamw_w_in", r_in, w_in, m_w_in, v_w_in, grid=(n_layers, d // tb),
                         part_specs=[_layer_part_spec(li, (N_DEV, tb, ci), d // tb) for li in layers],
                         w_spec=pl.BlockSpec((None, tb, ci), lambda l, i: (l, i, 0)))
    res["w_mlp1"] = _adamw("adamw_w_mlp1", r_1, w_mlp1, m_w_mlp1, v_w_mlp1, grid=(n_layers, d // tb),
                           part_specs=[_layer_part_spec(li, (N_DEV, tb, c1), d // tb) for li in layers],
                           w_spec=pl.BlockSpec((None, tb, c1), lambda l, i: (l, i, 0)))
    tf = min(256, rf)
    res["w_mlp2"] = _adamw("adamw_w_mlp2", r_row, w_mlp2, m_w_mlp2, v_w_mlp2, grid=(n_layers, rf // tf),
                           part_specs=[_layer_part_spec(li, (N_DEV, tf, d), rf // tf) for li in layers],
                           w_spec=pl.BlockSpec((None, tf, d), lambda l, i: (l, i, 0)))
    for nm, off, trip in (("w_out_a", o_a, (w_out_a, m_w_out_a, v_w_out_a)),
                          ("w_out_b", o_b, (w_out_b, m_w_out_b, v_w_out_b)), ("w_o", o_o, (w_o, m_w_o, v_w_o))):
        res[nm] = _adamw(f"adamw_{nm}", r_row, *trip, grid=(n_layers, 1),
                         part_specs=[_layer_part_spec(li, (N_DEV, rd, d), 1, row_off=off) for li in layers],
                         w_spec=pl.BlockSpec((None, rd, d), lambda l, i: (l, 0, 0)))
    res["w_pool"] = _adamw("adamw_w_pool", r_pool, w_pool, m_w_pool, v_w_pool, grid=(n_layers, 1),
                           part_specs=[_layer_part_spec(li, (N_DEV, N_GROUPS, gc // N_DEV, gc), 1) for li in layers],
                           w_spec=pl.BlockSpec((None, N_GROUPS, gc // N_DEV, gc), lambda l, i: (l, 0, 0, 0)))
    whole3 = lambda: (0, 0, 0)
    res["conv_a"] = _adamw("adamw_conv_a", [gca[None]], conv_a, m_conv_a, v_conv_a, grid=(),
                           part_specs=[pl.BlockSpec((1, n_layers, K_A, rd), lambda: (0, 0, 0, 0))],
                           w_spec=pl.BlockSpec((n_layers, K_A, rd), whole3))
    res["conv_b"] = _adamw("adamw_conv_b", [gcb[None]], conv_b, m_conv_b, v_conv_b, grid=(),
                           part_specs=[pl.BlockSpec((1, n_layers, K_B, rd), lambda: (0, 0, 0, 0))],
                           w_spec=pl.BlockSpec((n_layers, K_B, rd), whole3))
    res.update(rep)

    order = ("g_mix", "w_in", "b_in", "conv_a", "w_out_a", "conv_b", "conv_b_bias", "ln_b_g", "ln_b_b", "w_out_b",
             "b_out_b", "w_pool", "pool_scale", "w_o", "g_mlp", "w_mlp1", "w_mlp2", "g_final")
    out = [loss, grad_x]
    for kind in range(4):
        out += [res[nm][kind] for nm in order]
    return tuple(out)
```

```python
import jax
import jax.numpy as jnp
from jax import lax
from jax.experimental import pallas as pl
from jax.experimental.pallas import tpu as pltpu

F32 = jnp.float32
BF16 = jnp.bfloat16
MESH = pl.DeviceIdType.MESH

N_DEV = 8
EPS = 1e-6
K_A = 3
K_B = 31
POOL_WINDOWS = (2, 4, 8, 16)
N_GROUPS = len(POOL_WINDOWS)
HALO = 32
CHUNK = 16
SUBLANES = 8
TAP_GROUP = 16
ADAM_LR, ADAM_B1, ADAM_B2, ADAM_EPS, ADAM_WD, ADAM_STEP = 0.001, 0.9, 0.999, 1e-08, 0.01, 10
VMEM_LIMIT_BYTES = 60 * 1024 * 1024

NN = (((1,), (0,)), ((), ()))
NT = (((1,), (1,)), ((), ()))
TN = (((0,), (0,)), ((), ()))


def _params(*sem):
    return pltpu.CompilerParams(dimension_semantics=sem, vmem_limit_bytes=VMEM_LIMIT_BYTES)


def _sigmoid(v):
    return 1.0 / (1.0 + jnp.exp(-v))


def _mm(name, a, b, *, grid, a_spec, b_spec, out_shape, o_spec, dims, nk=1, acc_shape=None,
        extras=(), extra_specs=(), prologue=None, epilogue=None, alias_in=None, slabs=None, after=None,
        rms_bwd=None):
    n_extra = len(extras)
    has_alias = alias_in is not None
    n_unread = (1 if has_alias else 0) + (1 if after is not None else 0)

    def body(*refs):
        a_ref, b_ref = refs[0], refs[1]
        ex = refs[2:2 + n_extra]
        o_ref = refs[2 + n_extra + n_unread]
        av = a_ref[...]
        if prologue is not None:
            av = prologue(av)
        av = av.astype(BF16)

        def finish_rms(val):
            x_ref, g_ref, dr_ref = ex
            dx_ref, dx16_ref, dg_ref = refs[2 + n_extra + n_unread:5 + n_extra + n_unread]
            acc_g = refs[-1]
            row_axis, n_rows = rms_bwd
            ri = pl.program_id(row_axis)
            xv = x_ref[...]
            r = lax.rsqrt(jnp.mean(xv * xv, axis=-1, keepdims=True) + EPS)
            xh = xv * r
            part = _colsum8(val * xh)

            @pl.when(ri == 0)
            def _():
                acc_g[...] = part

            @pl.when(ri > 0)
            def _():
                acc_g[...] += part

            dxh = val * g_ref[...]
            dx = r * (dxh - xh * jnp.mean(dxh * xh, axis=-1, keepdims=True)) + dr_ref[...]
            dx_ref[...] = dx
            dx16_ref[...] = dx.astype(BF16)

            @pl.when(ri == n_rows - 1)
            def _():
                dg_ref[...] = jnp.sum(acc_g[...], axis=0, keepdims=True)

        def finish(val, cols=None):
            if rms_bwd is not None:
                return finish_rms(val)
            if epilogue is not None:
                val = epilogue(val, *[e[...] if cols is None else e[:, cols] for e in ex])
            if cols is None:
                o_ref[...] = val.astype(o_ref.dtype).reshape(o_ref.shape)
            else:
                o_ref[:, cols] = val.astype(o_ref.dtype)

        if slabs == "n":
            for q in range(b_ref.shape[0]):
                pq = lax.dot_general(av, b_ref[q].astype(BF16), dims, preferred_element_type=F32)
                finish(pq, slice(q * pq.shape[1], (q + 1) * pq.shape[1]))
            return
        if slabs == "k":
            kc = av.shape[1] // b_ref.shape[0]
            p = None
            for q in range(b_ref.shape[0]):
                pq = lax.dot_general(av[:, q * kc:(q + 1) * kc], b_ref[q].astype(BF16), dims,
                                     preferred_element_type=F32)
                p = pq if p is None else p + pq
        else:
            bv = b_ref[...]
            bv = bv.reshape((-1, bv.shape[-1])).astype(BF16)
            p = lax.dot_general(av, bv, dims, preferred_element_type=F32)

        if nk == 1:
            finish(p)
        else:
            acc = refs[-2] if rms_bwd is not None else refs[-1]
            k = pl.program_id(len(grid) - 1)

            @pl.when(k == 0)
            def _():
                acc[...] = p

            @pl.when(k > 0)
            def _():
                acc[...] += p

            @pl.when(k == nk - 1)
            def _():
                finish(acc[...])

    in_specs = [a_spec, b_spec, *extra_specs]
    operands = [a, b, *extras]
    aliases = {}
    if has_alias:
        in_specs.append(pl.BlockSpec(memory_space=pl.ANY))
        operands.append(alias_in)
        aliases = {len(operands) - 1: 0}
    if after is not None:
        in_specs.append(pl.BlockSpec(memory_space=pl.ANY))
        operands.append(after)
    sem = ("parallel",) * (len(grid) - 1) + (("arbitrary",) if nk > 1 else ("parallel",))
    scratch = [pltpu.VMEM(acc_shape, F32)] if nk > 1 else []
    if rms_bwd is not None:
        sem = ("arbitrary",) * len(grid)
        scratch.append(pltpu.VMEM((8, extras[0].shape[-1]), F32))
    return pl.pallas_call(
        body, name=name, grid=grid, in_specs=in_specs, out_specs=o_spec, out_shape=out_shape,
        scratch_shapes=scratch, input_output_aliases=aliases, compiler_params=_params(*sem),
    )(*operands)


def _relu_sq(v):
    r = jnp.maximum(v, 0)
    return r * r


def _rms_fwd(name, x, g, tm):
    s, d = x.shape

    def body(x_ref, g_ref, h_ref):
        xv = x_ref[...]
        r = lax.rsqrt(jnp.mean(xv * xv, axis=-1, keepdims=True) + EPS)
        h_ref[...] = (xv * r * g_ref[...]).astype(h_ref.dtype)

    return pl.pallas_call(
        body, name=name, grid=(s // tm,),
        in_specs=[pl.BlockSpec((tm, d), lambda i: (i, 0)), pl.BlockSpec((1, d), lambda i: (0, 0))],
        out_specs=pl.BlockSpec((tm, d), lambda i: (i, 0)),
        out_shape=jax.ShapeDtypeStruct((s, d), BF16), compiler_params=_params("parallel"),
    )(x, g)


def _colsum8(v):
    return jnp.sum(v.reshape(v.shape[0] // 8, 8, v.shape[1]), axis=0)


def _loss_head(x, g, target, tm):
    s, d = x.shape
    n = s // tm

    def body(x_ref, g_ref, t_ref, loss_ref, dx_ref, dx16_ref, dg_ref, acc_l, acc_g):
        i = pl.program_id(0)
        xv = x_ref[...]
        r = lax.rsqrt(jnp.mean(xv * xv, axis=-1, keepdims=True) + EPS)
        xh = xv * r
        err = xh * g_ref[...] - t_ref[...]
        dy = err * (1.0 / d)
        lpart = _colsum8(err * err)
        gpart = _colsum8(dy * xh)

        @pl.when(i == 0)
        def _():
            acc_l[...] = lpart
            acc_g[...] = gpart

        @pl.when(i > 0)
        def _():
            acc_l[...] += lpart
            acc_g[...] += gpart

        dxh = dy * g_ref[...]
        dx = r * (dxh - xh * jnp.mean(dxh * xh, axis=-1, keepdims=True))
        dx_ref[...] = dx
        dx16_ref[...] = dx.astype(BF16)

        @pl.when(i == n - 1)
        def _():
            loss_ref[...] = (0.5 / d) * jnp.sum(jnp.sum(acc_l[...], axis=0, keepdims=True), axis=1, keepdims=True)
            dg_ref[...] = jnp.sum(acc_g[...], axis=0, keepdims=True)

    return pl.pallas_call(
        body, name="loss_head", grid=(n,),
        in_specs=[pl.BlockSpec((tm, d), lambda i: (i, 0)), pl.BlockSpec((1, d), lambda i: (0, 0)),
                  pl.BlockSpec((tm, d), lambda i: (i, 0))],
        out_specs=[pl.BlockSpec((1, 1), lambda i: (0, 0)), pl.BlockSpec((tm, d), lambda i: (i, 0)),
                   pl.BlockSpec((tm, d), lambda i: (i, 0)), pl.BlockSpec((1, d), lambda i: (0, 0))],
        out_shape=[jax.ShapeDtypeStruct((1, 1), F32), jax.ShapeDtypeStruct((s, d), F32),
                   jax.ShapeDtypeStruct((s, d), BF16), jax.ShapeDtypeStruct((1, d), F32)],
        scratch_shapes=[pltpu.VMEM((8, d), F32), pltpu.VMEM((8, d), F32)], compiler_params=_params("arbitrary"),
    )(x, g, target)


def _sec(ref, n, d):
    return ref[:, n * d:(n + 1) * d].astype(F32)


def _fill_shifts(sh, ext):
    rows = ext.shape[0] - SUBLANES
    for b in range(1, SUBLANES):
        sh[b - 1, 0:rows, :] = ext[b:b + rows, :]


def _shifted(sh, ext, off, n):
    b = off % SUBLANES
    if b == 0:
        return ext[off:off + n, :]
    return sh[b - 1, off - b:off - b + n, :]


def _spread_taps(dst, w_ref):
    for k in range(w_ref.shape[0]):
        dst[k] = jnp.broadcast_to(w_ref[k:k + 1, :], dst.shape[1:])


def _times_tap(x, tap):
    return (x.reshape(x.shape[0] // SUBLANES, SUBLANES, x.shape[1]) * tap[None]).reshape(x.shape)


def _window_sums(src, s_a, s_b, gc, first, rows, back):
    n = src.shape[0]
    sign = -1 if back else 1
    lo = [SUBLANES * j if back else 0 for j in range(4)]
    hi = [n if back else n - SUBLANES * j for j in range(4)]
    sl = lambda j, shift: slice(lo[j] + shift, hi[j] + shift)
    s_a[sl(1, 0), :] = src[sl(1, 0), :] + src[sl(1, sign * 1), :]
    out = [s_a[first:first + rows, 0:gc]]
    s_b[sl(2, 0), gc:] = s_a[sl(2, 0), gc:] + s_a[sl(2, sign * 2), gc:]
    out.append(s_b[first:first + rows, gc:2 * gc])
    s_a[sl(3, 0), 2 * gc:] = s_b[sl(3, 0), 2 * gc:] + s_b[sl(3, sign * 4), 2 * gc:]
    out.append(s_a[first:first + rows, 2 * gc:3 * gc])
    out.append(s_a[first:first + rows, 3 * gc:] + s_a[first + sign * SUBLANES:first + sign * SUBLANES + rows, 3 * gc:])
    return out


def _pool_count(row0, rows, window):
    t = row0 + lax.broadcasted_iota(jnp.int32, (rows, 1), 0)
    return jnp.minimum(t + 1, window).astype(F32)


def _group_weight(w_ref, gi, gc):
    return w_ref[:, gi].reshape(gc, gc)


def _mixer_fwd(name, proj, x0, conv_a, conv_b, conv_b_bias, ln_g, ln_b, b_out_b, pool_scale, g_next, g_row, g_pool, offs,
               d, tm):
    s = proj.shape[0]
    n = s // tm
    gc = d // N_GROUPS
    hb = tm // HALO
    rd = d // N_DEV
    o_a, o_b, o_o = offs

    def body(pj_ref, hp_ref, x0_ref, ca_ref, cb_ref, cbb_ref, lng_ref, lnb_ref, bo_ref, sc_ref, gn_ref, wa_ref, wb_ref,
             wo_ref, wp_ref, pa_ref, sw_ref, pc_ref, cv_ref, ya_ref, yb_ref, pw_ref, mg_ref, x1_ref, h2_ref,
             eua, eub, euc, sh, taps_a, taps_b, sum_a, sum_b):
        i = pl.program_id(0)
        keep = (i > 0).astype(F32)
        _spread_taps(taps_a, ca_ref)
        _spread_taps(taps_b, cb_ref)
        eua[0:HALO, :] = _sec(hp_ref, 1, d) * _sec(hp_ref, 2, d) * keep
        eub[0:HALO, :] = _sec(hp_ref, 3, d) * _sigmoid(_sec(hp_ref, 4, d)) * keep
        euc[0:HALO, :] = _sec(hp_ref, 5, d) * keep
        eua[HALO:HALO + tm, :] = _sec(pj_ref, 1, d) * _sec(pj_ref, 2, d)
        eub[HALO:HALO + tm, :] = _sec(pj_ref, 3, d) * _sigmoid(_sec(pj_ref, 4, d))
        euc[HALO:HALO + tm, :] = _sec(pj_ref, 5, d)
        _fill_shifts(sh, eub)
        for c in range(tm // CHUNK):
            r0 = c * CHUNK
            z = jnp.zeros((CHUNK, d), F32)
            for k in range(K_A):
                z = z + _times_tap(eua[HALO + r0 - (K_A - 1) + k:HALO + r0 - (K_A - 1) + k + CHUNK, :], taps_a[k])
            pa_ref[r0:r0 + CHUNK, :] = (pj_ref[r0:r0 + CHUNK, 0:d].astype(F32) * z).astype(pa_ref.dtype)
            cv = jnp.zeros((CHUNK, d), F32) + cbb_ref[...]
            for k in range(K_B):
                cv = cv + _times_tap(_shifted(sh, eub, HALO + r0 - (K_B - 1) + k, CHUNK), taps_b[k])
            cv_ref[r0:r0 + CHUNK, :] = cv.astype(cv_ref.dtype)
        cvv = cv_ref[...].astype(F32)
        mu = jnp.mean(cvv, axis=-1, keepdims=True)
        xc = cvv - mu
        xh = xc * lax.rsqrt(jnp.mean(xc * xc, axis=-1, keepdims=True) + EPS)
        ln = xh * lng_ref[...] + lnb_ref[...]
        sw_ref[...] = (ln * _sigmoid(ln)).astype(sw_ref.dtype)
        sums = _window_sums(euc, sum_a, sum_b, gc, HALO, tm, back=True)
        for gi, w in enumerate(POOL_WINDOWS):
            cols = slice(gi * gc, (gi + 1) * gc)
            cnt = _pool_count(i * tm, tm, w)
            pc_ref[:, cols] = (sums[gi] / cnt - euc[HALO:HALO + tm, cols]).astype(pc_ref.dtype)
        ya_ref[...] = jnp.dot(pa_ref[...], wa_ref[...].reshape(d, d), preferred_element_type=F32).astype(ya_ref.dtype)
        yb_ref[...] = (jnp.dot(sw_ref[...], wb_ref[...].reshape(d, d), preferred_element_type=F32)
                       + bo_ref[...]).astype(yb_ref.dtype)
        for gi in range(N_GROUPS):
            cols = slice(gi * gc, (gi + 1) * gc)
            pw_ref[:, cols] = jnp.dot(pc_ref[:, cols], _group_weight(wp_ref, gi, gc),
                                      preferred_element_type=F32).astype(pw_ref.dtype)
        m = _sigmoid(_sec(pj_ref, 6, d)) * ya_ref[...].astype(F32)
        m = m + _sigmoid(_sec(pj_ref, 7, d)) * yb_ref[...].astype(F32)
        m = m + _sigmoid(_sec(pj_ref, 8, d)) * (pw_ref[...].astype(F32) * sc_ref[...])
        mg_ref[...] = m.astype(mg_ref.dtype)
        x1 = x0_ref[...] + jnp.dot(mg_ref[...], wo_ref[...].reshape(d, d), preferred_element_type=F32)
        x1_ref[...] = x1
        h2_ref[...] = (x1 * lax.rsqrt(jnp.mean(x1 * x1, axis=-1, keepdims=True) + EPS) * gn_ref[...]).astype(h2_ref.dtype)

    row = lambda i: (i, 0)
    fixed = lambda i: (0, 0)
    act = jax.ShapeDtypeStruct((s, d), BF16)

    def dd_weight(off):
        return pl.BlockSpec((N_DEV, rd, d), lambda i: (0, off, 0), pipeline_mode=pl.Buffered(1))

    return pl.pallas_call(
        body, name=name, grid=(n,),
        in_specs=[pl.BlockSpec((tm, 9 * d), row),
                  pl.BlockSpec((HALO, 6 * d), lambda i: (jnp.maximum(i * hb - 1, 0), 0)),
                  pl.BlockSpec((tm, d), row),
                  pl.BlockSpec((K_A, d), fixed), pl.BlockSpec((K_B, d), fixed), pl.BlockSpec((1, d), fixed),
                  pl.BlockSpec((1, d), fixed), pl.BlockSpec((1, d), fixed), pl.BlockSpec((1, d), fixed),
                  pl.BlockSpec((1, d), fixed), pl.BlockSpec((1, d), fixed), dd_weight(o_a), dd_weight(o_b), dd_weight(o_o),
                  pl.BlockSpec((N_DEV, N_GROUPS, gc // N_DEV, gc), lambda i: (0, 0, 0, 0),
                               pipeline_mode=pl.Buffered(1))],
        out_specs=[pl.BlockSpec((tm, d), row)] * 10,
        out_shape=[act] * 8 + [jax.ShapeDtypeStruct((s, d), F32), act],
        scratch_shapes=[pltpu.VMEM((tm + HALO, d), F32)] * 3 + [pltpu.VMEM((SUBLANES - 1, tm + HALO, d), F32),
                                                                pltpu.VMEM((K_A, SUBLANES, d), F32),
                                                                pltpu.VMEM((K_B, SUBLANES, d), F32)]
                       + [pltpu.VMEM((tm + HALO, d), F32)] * 2,
        compiler_params=_params("parallel"),
    )(proj, proj, x0, conv_a, conv_b, conv_b_bias, ln_g, ln_b, b_out_b, pool_scale, g_next, g_row, g_row, g_row,
      g_pool)


def _merge_bwd(name, dx16, proj, ya, yb, pw, pool_scale, g_row, g_pool, offs, d, tm):
    s = proj.shape[0]
    n = s // tm
    gc = d // N_GROUPS
    rd = d // N_DEV
    o_a, o_b, o_o = offs

    def body(dx_ref, g_ref, ya_ref, yb_ref, pw_ref, sc_ref, wa_ref, wb_ref, wo_ref, wp_ref,
             dya_ref, dyb_ref, dpw_ref, dg_ref, dpa_ref, dsw_ref, dpc_ref, dbo_ref, dsc_ref, acc_b, acc_s):
        i = pl.program_id(0)
        dmv = lax.dot_general(dx_ref[...], wo_ref[...].reshape(d, d), NT,
                              preferred_element_type=F32).astype(BF16).astype(F32)
        scale = sc_ref[...]
        g0 = _sigmoid(_sec(g_ref, 0, d))
        dya_ref[...] = (dmv * g0).astype(dya_ref.dtype)
        dg_ref[:, 0:d] = (dmv * ya_ref[...].astype(F32) * g0 * (1.0 - g0)).astype(dg_ref.dtype)
        g1 = _sigmoid(_sec(g_ref, 1, d))
        dyb = dmv * g1
        dyb_ref[...] = dyb.astype(dyb_ref.dtype)
        dg_ref[:, d:2 * d] = (dmv * yb_ref[...].astype(F32) * g1 * (1.0 - g1)).astype(dg_ref.dtype)
        g2 = _sigmoid(_sec(g_ref, 2, d))
        pwv = pw_ref[...].astype(F32)
        dyc = dmv * g2
        dpw_ref[...] = (dyc * scale).astype(dpw_ref.dtype)
        dg_ref[:, 2 * d:3 * d] = (dmv * (pwv * scale) * g2 * (1.0 - g2)).astype(dg_ref.dtype)
        pb = _colsum8(dyb)
        ps = _colsum8(dyc * pwv)

        @pl.when(i == 0)
        def _():
            acc_b[...] = pb
            acc_s[...] = ps

        @pl.when(i > 0)
        def _():
            acc_b[...] += pb
            acc_s[...] += ps

        dpa_ref[...] = lax.dot_general(dya_ref[...], wa_ref[...].reshape(d, d), NT,
                                       preferred_element_type=F32).astype(dpa_ref.dtype)
        dsw_ref[...] = lax.dot_general(dyb_ref[...], wb_ref[...].reshape(d, d), NT,
                                       preferred_element_type=F32).astype(dsw_ref.dtype)
        for gi in range(N_GROUPS):
            cols = slice(gi * gc, (gi + 1) * gc)
            dpc_ref[:, cols] = lax.dot_general(dpw_ref[:, cols], _group_weight(wp_ref, gi, gc), NT,
                                               preferred_element_type=F32).astype(dpc_ref.dtype)

        @pl.when(i == n - 1)
        def _():
            dbo_ref[...] = jnp.sum(acc_b[...], axis=0, keepdims=True)
            dsc_ref[...] = jnp.sum(acc_s[...], axis=0, keepdims=True)

    row = lambda i: (i, 0)
    fixed = lambda i: (0, 0)
    act = jax.ShapeDtypeStruct((s, d), BF16)
    vec = jax.ShapeDtypeStruct((1, d), F32)

    def dd_weight(off):
        return pl.BlockSpec((N_DEV, rd, d), lambda i: (0, off, 0), pipeline_mode=pl.Buffered(1))

    return pl.pallas_call(
        body, name=name, grid=(n,),
        in_specs=[pl.BlockSpec((tm, d), row), pl.BlockSpec((tm, 3 * d), lambda i: (i, 2)), pl.BlockSpec((tm, d), row),
                  pl.BlockSpec((tm, d), row), pl.BlockSpec((tm, d), row), pl.BlockSpec((1, d), fixed),
                  dd_weight(o_a), dd_weight(o_b), dd_weight(o_o),
                  pl.BlockSpec((N_DEV, N_GROUPS, gc // N_DEV, gc), lambda i: (0, 0, 0, 0),
                               pipeline_mode=pl.Buffered(1))],
        out_specs=[pl.BlockSpec((tm, d), row)] * 3 + [pl.BlockSpec((tm, 3 * d), row)] + [pl.BlockSpec((tm, d), row)] * 3
                  + [pl.BlockSpec((1, d), fixed)] * 2,
        out_shape=[act, act, act, jax.ShapeDtypeStruct((s, 3 * d), BF16), act, act, act, vec, vec],
        scratch_shapes=[pltpu.VMEM((8, d), F32)] * 2, compiler_params=_params("arbitrary"),
    )(dx16, proj, ya, yb, pw, pool_scale, g_row, g_row, g_row, g_pool)


def _mix_pre_bwd(name, proj, cv, dpa, dsw, dpc, dgates, conv_a, conv_b, ln_g, ln_b, d, tm):
    s = proj.shape[0]
    n = s // tm
    gc = d // N_GROUPS
    hb = tm // HALO
    last_halo = s // HALO - 1
    te = tm + HALO

    def ln_bwd(cvv, dswv, lng, lnb):
        mu = jnp.mean(cvv, axis=-1, keepdims=True)
        xc = cvv - mu
        rstd = lax.rsqrt(jnp.mean(xc * xc, axis=-1, keepdims=True) + EPS)
        xh = xc * rstd
        ln = xh * lng + lnb
        sg = _sigmoid(ln)
        dln = dswv * (sg * (1.0 + ln * (1.0 - sg)))
        dxh = dln * lng
        dcv = rstd * (dxh - jnp.mean(dxh, axis=-1, keepdims=True) - xh * jnp.mean(dxh * xh, axis=-1, keepdims=True))
        return dcv, dln, xh

    def body(pj_ref, hp_ref, hf_ref, cv_ref, cvf_ref, dpa_ref, dpaf_ref, dsw_ref, dswf_ref, dpc_ref, dpcf_ref, dgt_ref,
             ca_ref, cb_ref, lng_ref, lnb_ref,
             dpj_ref, dbin_ref, dca_ref, dcb_ref, dcbb_ref, dlng_ref, dlnb_ref,
             eua, eub, edz, edcv, eq, sh, dub_s, taps_a, taps_b, sum_a, sum_b, acc_bin, acc_ca, acc_cb, acc_v):
        i = pl.program_id(0)
        keep_p = (i > 0).astype(F32)
        keep_f = (i < n - 1).astype(F32)
        _spread_taps(taps_a, ca_ref)
        _spread_taps(taps_b, cb_ref)

        @pl.when(i == 0)
        def _():
            acc_bin[...] = jnp.zeros_like(acc_bin)
            acc_ca[...] = jnp.zeros_like(acc_ca)
            acc_cb[...] = jnp.zeros_like(acc_cb)
            acc_v[...] = jnp.zeros_like(acc_v)

        eua[0:HALO, :] = _sec(hp_ref, 1, d) * _sec(hp_ref, 2, d) * keep_p
        eua[HALO:te, :] = _sec(pj_ref, 1, d) * _sec(pj_ref, 2, d)
        eub[HALO:te, :] = _sec(pj_ref, 3, d) * _sigmoid(_sec(pj_ref, 4, d))
        edz[0:tm, :] = dpa_ref[...].astype(F32) * _sec(pj_ref, 0, d)
        edz[tm:te, :] = dpaf_ref[...].astype(F32) * _sec(hf_ref, 0, d) * keep_f
        dcv, dln, xh = ln_bwd(cv_ref[...].astype(F32), dsw_ref[...].astype(F32), lng_ref[...], lnb_ref[...])
        edcv[0:tm, :] = dcv
        acc_v[0:8, :] += _colsum8(dcv)
        acc_v[8:16, :] += _colsum8(dln * xh)
        acc_v[16:24, :] += _colsum8(dln)
        dcvf, _, _ = ln_bwd(cvf_ref[...].astype(F32), dswf_ref[...].astype(F32), lng_ref[...], lnb_ref[...])
        edcv[tm:te, :] = dcvf * keep_f
        for gi, w in enumerate(POOL_WINDOWS):
            cols = slice(gi * gc, (gi + 1) * gc)
            eq[0:tm, cols] = dpc_ref[:, cols].astype(F32) / _pool_count(i * tm, tm, w)
            eq[tm:te, cols] = dpcf_ref[:, cols].astype(F32) / _pool_count((i + 1) * tm, HALO, w) * keep_f

        def put(sec_idx, r0, val):
            dpj_ref[r0:r0 + CHUNK, sec_idx * d:(sec_idx + 1) * d] = val.astype(dpj_ref.dtype)
            acc_bin[:, sec_idx * d:(sec_idx + 1) * d] += _colsum8(val)

        _fill_shifts(sh, edcv)
        for k0 in range(0, K_B, TAP_GROUP):
            taps = range(k0, min(k0 + TAP_GROUP, K_B))
            a = {k: jnp.zeros((8, d), F32) for k in taps}
            for c in range(tm // CHUNK):
                r0 = c * CHUNK
                ub = eub[HALO + r0:HALO + r0 + CHUNK, :]
                part = None
                for k in taps:
                    t = _shifted(sh, edcv, r0 + (K_B - 1) - k, CHUNK)
                    part = _times_tap(t, taps_b[k]) if part is None else part + _times_tap(t, taps_b[k])
                    a[k] = a[k] + _colsum8(ub * t)
                if k0 == 0:
                    dub_s[r0:r0 + CHUNK, :] = part
                else:
                    dub_s[r0:r0 + CHUNK, :] += part
            for k in taps:
                acc_cb[k] += a[k]
        wa = [jnp.zeros((8, d), F32) for _ in range(K_A)]
        for c in range(tm // CHUNK):
            r0 = c * CHUNK
            rows = slice(r0, r0 + CHUNK)
            z = jnp.zeros((CHUNK, d), F32)
            dua = jnp.zeros((CHUNK, d), F32)
            ua = eua[HALO + r0:HALO + r0 + CHUNK, :]
            for k in range(K_A):
                z = z + _times_tap(eua[HALO + r0 - (K_A - 1) + k:HALO + r0 - (K_A - 1) + k + CHUNK, :], taps_a[k])
                t = edz[r0 + (K_A - 1) - k:r0 + (K_A - 1) - k + CHUNK, :]
                dua = dua + _times_tap(t, taps_a[k])
                wa[k] = wa[k] + _colsum8(ua * t)
            put(0, r0, dpa_ref[rows, :].astype(F32) * z)
            put(1, r0, dua * pj_ref[rows, 2 * d:3 * d].astype(F32))
            put(2, r0, dua * pj_ref[rows, d:2 * d].astype(F32))
            dub = dub_s[rows, :]
            bval = pj_ref[rows, 3 * d:4 * d].astype(F32)
            sg = _sigmoid(pj_ref[rows, 4 * d:5 * d].astype(F32))
            put(3, r0, dub * sg)
            put(4, r0, dub * bval * sg * (1.0 - sg))
        sums = _window_sums(eq, sum_a, sum_b, gc, 0, tm, back=False)
        for gi in range(N_GROUPS):
            cols = slice(gi * gc, (gi + 1) * gc)
            dci = sums[gi] - dpc_ref[:, cols].astype(F32)
            dpj_ref[:, 5 * d + gi * gc:5 * d + (gi + 1) * gc] = dci.astype(dpj_ref.dtype)
            acc_bin[:, 5 * d + gi * gc:5 * d + (gi + 1) * gc] += _colsum8(dci)
        for q in range(3):
            gv = dgt_ref[:, q * d:(q + 1) * d]
            dpj_ref[:, (6 + q) * d:(7 + q) * d] = gv
            acc_bin[:, (6 + q) * d:(7 + q) * d] += _colsum8(gv.astype(F32))
        for k in range(K_A):
            acc_ca[k] += wa[k]

        @pl.when(i == n - 1)
        def _():
            dbin_ref[...] = jnp.sum(acc_bin[...], axis=0, keepdims=True)
            for k in range(K_A):
                dca_ref[k:k + 1, :] = jnp.sum(acc_ca[k], axis=0, keepdims=True)
            for k in range(K_B):
                dcb_ref[k:k + 1, :] = jnp.sum(acc_cb[k], axis=0, keepdims=True)
            dcbb_ref[...] = jnp.sum(acc_v[0:8, :], axis=0, keepdims=True)
            dlng_ref[...] = jnp.sum(acc_v[8:16, :], axis=0, keepdims=True)
            dlnb_ref[...] = jnp.sum(acc_v[16:24, :], axis=0, keepdims=True)

    row = lambda i: (i, 0)
    fixed = lambda i: (0, 0)
    past = lambda i: (jnp.maximum(i * hb - 1, 0), 0)
    fut = lambda i: (jnp.minimum((i + 1) * hb, last_halo), 0)
    vec = jax.ShapeDtypeStruct((1, d), F32)
    tile_and_halo = [pl.BlockSpec((tm, d), row), pl.BlockSpec((HALO, d), fut)]
    return pl.pallas_call(
        body, name=name, grid=(n,),
        in_specs=[pl.BlockSpec((tm, 6 * d), row), pl.BlockSpec((HALO, 6 * d), past), pl.BlockSpec((HALO, 6 * d), fut),
                  *tile_and_halo, *tile_and_halo, *tile_and_halo, *tile_and_halo,
                  pl.BlockSpec((tm, 3 * d), row),
                  pl.BlockSpec((K_A, d), fixed), pl.BlockSpec((K_B, d), fixed), pl.BlockSpec((1, d), fixed),
                  pl.BlockSpec((1, d), fixed)],
        out_specs=[pl.BlockSpec((tm, 9 * d), row), pl.BlockSpec((1, 9 * d), fixed), pl.BlockSpec((K_A, d), fixed),
                   pl.BlockSpec((K_B, d), fixed), pl.BlockSpec((1, d), fixed), pl.BlockSpec((1, d), fixed),
                   pl.BlockSpec((1, d), fixed)],
        out_shape=[jax.ShapeDtypeStruct((s, 9 * d), BF16), jax.ShapeDtypeStruct((1, 9 * d), F32),
                   jax.ShapeDtypeStruct((K_A, d), F32), jax.ShapeDtypeStruct((K_B, d), F32), vec, vec, vec],
        scratch_shapes=[pltpu.VMEM((te, d), F32)] * 5 + [pltpu.VMEM((SUBLANES - 1, te, d), F32),
                                                         pltpu.VMEM((tm, d), F32),
                                                         pltpu.VMEM((K_A, SUBLANES, d), F32),
                                                         pltpu.VMEM((K_B, SUBLANES, d), F32),
                                                         pltpu.VMEM((te, d), F32), pltpu.VMEM((te, d), F32),
                                                         pltpu.VMEM((8, 9 * d), F32), pltpu.VMEM((K_A, 8, d), F32),
                                                         pltpu.VMEM((K_B, 8, d), F32), pltpu.VMEM((24, d), F32)],
        compiler_params=_params("arbitrary"),
    )(proj, proj, proj, cv, cv, dpa, dpa, dsw, dsw, dpc, dpc, dgates, conv_a, conv_b, ln_g, ln_b)


def _pool_wgrad(name, p, dpw, d, tk):
    s = p.shape[0]
    gc = d // N_GROUPS
    n = s // tk

    def body(p_ref, g_ref, o_ref, acc):
        k = pl.program_id(0)
        for gi in range(N_GROUPS):
            cols = slice(gi * gc, (gi + 1) * gc)
            part = lax.dot_general(p_ref[:, cols], g_ref[:, cols], TN, preferred_element_type=F32)

            @pl.when(k == 0)
            def _():
                acc[gi] = part

            @pl.when(k > 0)
            def _():
                acc[gi] += part

        @pl.when(k == n - 1)
        def _():
            for gi in range(N_GROUPS):
                o_ref[:, gi] = acc[gi].astype(o_ref.dtype).reshape(N_DEV, gc // N_DEV, gc)

    return pl.pallas_call(
        body, name=name, grid=(n,),
        in_specs=[pl.BlockSpec((tk, d), lambda k: (k, 0)), pl.BlockSpec((tk, d), lambda k: (k, 0))],
        out_specs=pl.BlockSpec((N_DEV, N_GROUPS, gc // N_DEV, gc), lambda k: (0, 0, 0, 0)),
        out_shape=jax.ShapeDtypeStruct((N_DEV, N_GROUPS, gc // N_DEV, gc), BF16),
        scratch_shapes=[pltpu.VMEM((N_GROUPS, gc, gc), F32)], compiler_params=_params("arbitrary"),
    )(p, dpw)


def _my_place():
    x, y, c = lax.axis_index("x"), lax.axis_index("y"), lax.axis_index("c")
    return x, y, c


def _block_of(x, y, c):
    return 4 * x + 2 * y + c


def _slot(ref, k, paired):
    if not paired:
        return ref.at[k]
    cols = ref.shape[-1] // 2
    return ref.at[k // 2, :, pl.ds(pl.multiple_of((k % 2) * cols, 128), cols)]


def _slot_shape(shape, paired):
    return (N_DEV // 2, shape[0], 2 * shape[1]) if paired else (N_DEV, *shape)


def _gather_shards(shards, paired):
    n_arr = len(shards)

    def body(*refs):
        srcs = refs[:n_arr]
        outs = refs[n_arr:2 * n_arr]
        send_sems, recv_sems, local_sems = refs[2 * n_arr:]
        x, y, c = _my_place()
        me, sibling = (x, y, c), (x, y, 1 - c)
        chips = [(1 - x, y), (x, 1 - y), (1 - x, 1 - y)]

        def copy(n, k, block, to, src=None):
            rows = _slot(outs[n], _block_of(*block), paired[n])
            return pltpu.make_async_remote_copy(
                src_ref=rows if src is None else src, dst_ref=rows, send_sem=send_sems.at[n, k],
                recv_sem=recv_sems.at[n, k], device_id=to, device_id_type=MESH)

        mine = [pltpu.make_async_copy(srcs[n], _slot(outs[n], _block_of(*me), paired[n]), local_sems.at[n])
                for n in range(n_arr)]
        for cp in mine:
            cp.start()
        first = []
        for n in range(n_arr):
            first.append(copy(n, 0, me, sibling, src=srcs[n]))
            first += [copy(n, 1 + j, me, (*chip, c), src=srcs[n]) for j, chip in enumerate(chips)]
        for cp in first:
            cp.start()
        passed = []
        for n in range(n_arr):
            for j, chip in enumerate(chips):
                copy(n, 1 + j, (*chip, c), me).wait_recv()
                fwd = copy(n, 4 + j, (*chip, c), sibling)
                fwd.start()
                passed.append(fwd)
        for n in range(n_arr):
            copy(n, 0, sibling, me).wait_recv()
            for j, chip in enumerate(chips):
                copy(n, 4 + j, (*chip, 1 - c), me).wait_recv()
        for cp in first + passed:
            cp.wait_send()
        for cp in mine:
            cp.wait()

    any_spec = pl.BlockSpec(memory_space=pl.ANY)
    return pl.pallas_call(
        body, name="gather_weights",
        in_specs=[any_spec] * n_arr, out_specs=[any_spec] * n_arr,
        out_shape=[jax.ShapeDtypeStruct(_slot_shape(sh.shape, p), sh.dtype) for sh, p in zip(shards, paired)],
        scratch_shapes=[pltpu.SemaphoreType.DMA((n_arr, 7)), pltpu.SemaphoreType.DMA((n_arr, 7)),
                        pltpu.SemaphoreType.DMA((n_arr,))],
    )(*shards)


def _peers(x, y, c):
    out = []
    for r in range(1, N_DEV):
        fx, fy, fc = (r >> 2) & 1, (r >> 1) & 1, r & 1
        out.append(((1 - x) if fx else x, (1 - y) if fy else y, (1 - c) if fc else c))
    return out


HBM_SPEC = pl.BlockSpec(memory_space=pltpu.HBM)
SEM_SPEC = pl.BlockSpec(memory_space=pltpu.SEMAPHORE)
ANY_SPEC = pl.BlockSpec(memory_space=pl.ANY)
N_PEERS = N_DEV - 1


def _peer_copy(src_ref, land_ref, send_sems, recv_sems, i, r, peer, me, blockwise, paired):
    src = _slot(src_ref, _block_of(*peer), paired) if blockwise else src_ref
    dst = land_ref.at[me] if blockwise else _slot(land_ref, me, paired)
    return pltpu.make_async_remote_copy(
        src_ref=src, dst_ref=dst, send_sem=send_sems.at[i * N_PEERS + r],
        recv_sem=recv_sems.at[i * N_PEERS + r], device_id=peer, device_id_type=MESH)


def _block_shape(shape, paired):
    return (shape[1], shape[2] // 2) if paired else tuple(shape[1:])


def _start_copies(name, srcs, after, blockwise, paired=None):
    n = len(srcs)
    paired = paired or [False] * n

    def body(*refs):
        s_in, l_in = refs[:n], refs[n:2 * n]
        send_sems, recv_sems = refs[2 * n + 1], refs[2 * n + 2]
        token = refs[-1]
        x, y, c = _my_place()
        me = _block_of(x, y, c)
        for i in range(n):
            for r, peer in enumerate(_peers(x, y, c)):
                _peer_copy(s_in[i], l_in[i], send_sems, recv_sems, i, r, peer, me, blockwise, paired[i]).start()
        token[...] = jnp.zeros_like(token)

    land_shapes = [(N_DEV, *_block_shape(s.shape, p)) if blockwise else _slot_shape(s.shape, p)
                   for s, p in zip(srcs, paired)]
    lands = [pltpu.with_memory_space_constraint(lax.empty(sh, s.dtype), pltpu.HBM) for sh, s in zip(land_shapes, srcs)]
    ins = [pltpu.with_memory_space_constraint(s, pltpu.HBM) for s in srcs]
    out = pl.pallas_call(
        body, name=name,
        out_shape=(pltpu.SemaphoreType.DMA((n * N_PEERS,)), pltpu.SemaphoreType.DMA((n * N_PEERS,)),
                   *[pltpu.HBM(s.shape, s.dtype) for s in srcs],
                   *[pltpu.HBM(sh, s.dtype) for sh, s in zip(land_shapes, srcs)],
                   jax.ShapeDtypeStruct((8, 128), F32)),
        in_specs=[HBM_SPEC] * (2 * n) + [ANY_SPEC],
        out_specs=(SEM_SPEC, SEM_SPEC, *[HBM_SPEC] * (2 * n), pl.BlockSpec(memory_space=pltpu.VMEM)),
        input_output_aliases={i: 2 + i for i in range(2 * n)},
        compiler_params=pltpu.CompilerParams(has_side_effects=pltpu.SideEffectType.DATAFLOW_SIDE_EFFECTING),
    )(*ins, *lands, after)
    return dict(send=out[0], recv=out[1], srcs=list(out[2:2 + n]), lands=list(out[2 + n:2 + 2 * n]), token=out[-1],
                paired=paired)


def _wait_copies(name, state, after, blockwise):
    n = len(state["srcs"])
    paired = state["paired"]

    def body(*refs):
        s_in, l_in = refs[:n], refs[n:2 * n]
        send_sems, recv_sems = refs[2 * n], refs[2 * n + 1]
        x, y, c = _my_place()
        me = _block_of(x, y, c)
        for i in range(n):
            for r, peer in enumerate(_peers(x, y, c)):
                cp = _peer_copy(s_in[i], l_in[i], send_sems, recv_sems, i, r, peer, me, blockwise, paired[i])
                cp.wait_send()
                cp.wait_recv()

    both = state["srcs"] + state["lands"]
    out = pl.pallas_call(
        body, name=name, out_shape=tuple(pltpu.HBM(a.shape, a.dtype) for a in both),
        in_specs=[HBM_SPEC] * (2 * n) + [SEM_SPEC, SEM_SPEC, ANY_SPEC], out_specs=tuple([HBM_SPEC] * (2 * n)),
        input_output_aliases={i: i for i in range(2 * n)},
        compiler_params=pltpu.CompilerParams(has_side_effects=pltpu.SideEffectType.DATAFLOW_SIDE_EFFECTING),
    )(*both, state["send"], state["recv"], after)
    return list(out[:n]), list(out[n:])


COPY_BLOCK_BYTES = 2 * 1024 * 1024


def _place_own(name, lands, srcs, me, blockwise, paired=None):
    out = []
    paired = paired or [False] * len(lands)
    for i, (land, src) in enumerate(zip(lands, srcs)):
        in_slots = src if blockwise else land
        part = _block_shape(in_slots.shape, paired[i])
        row_bytes = land.dtype.itemsize
        for extent in part[1:]:
            row_bytes *= extent
        tr = part[0]
        while tr * row_bytes > COPY_BLOCK_BYTES and tr % 16 == 0:
            tr //= 2
        tail = (0,) * (len(part) - 1)

        def body(me_ref, s_ref, l_ref, o_ref):
            o_ref[...] = s_ref[...]

        if paired[i]:
            slot_spec = pl.BlockSpec((None, tr, part[1]), lambda j, me_ref: (me_ref[0] // 2, j, me_ref[0] % 2))
        else:
            slot_spec = pl.BlockSpec((None, tr, *part[1:]), lambda j, me_ref: (me_ref[0], j, *tail))
        if blockwise:
            s_spec = slot_spec
            o_spec = pl.BlockSpec((None, tr, *part[1:]), lambda j, me_ref: (me_ref[0], j, *tail))
        else:
            s_spec = pl.BlockSpec((tr, *part[1:]), lambda j, me_ref: (j, *tail))
            o_spec = slot_spec
        out.append(pl.pallas_call(
            body, name=f"{name}_{i}",
            grid_spec=pltpu.PrefetchScalarGridSpec(
                num_scalar_prefetch=1, grid=(part[0] // tr,), in_specs=[s_spec, ANY_SPEC], out_specs=o_spec),
            out_shape=jax.ShapeDtypeStruct(land.shape, land.dtype), input_output_aliases={2: 0},
            compiler_params=_params("parallel"),
        )(me, src, land))
    return out


def _adamw_math(w, g, m, v):
    m = ADAM_B1 * m + (1.0 - ADAM_B1) * g
    v = ADAM_B2 * v + (1.0 - ADAM_B2) * (g * g)
    m_hat = m / (1.0 - ADAM_B1 ** ADAM_STEP)
    v_hat = v / (1.0 - ADAM_B2 ** ADAM_STEP)
    delta = -ADAM_LR * (m_hat / (jnp.sqrt(v_hat) + ADAM_EPS) + ADAM_WD * w)
    return delta, m, v


def _adamw(name, parts, w, m, v, *, grid, part_specs, w_spec):
    n_layers = len(parts)
    n_parts = parts[0].shape[0]

    def body(*refs):
        p_refs = refs[:n_layers]
        w_ref, m_ref, v_ref, g_ref, d_ref, nm_ref, nv_ref = refs[n_layers:]

        def total(p_ref):
            t = p_ref[0].astype(F32)
            for k in range(1, n_parts):
                t = t + p_ref[k].astype(F32)
            return t

        g = total(p_refs[0])
        for li in range(1, n_layers):
            g = jnp.where(pl.program_id(0) == li, total(p_refs[li]), g)
        delta, nm, nv = _adamw_math(w_ref[...], g, m_ref[...], v_ref[...])
        g_ref[...] = g
        d_ref[...] = delta
        nm_ref[...] = nm
        nv_ref[...] = nv

    out = jax.ShapeDtypeStruct(w.shape, F32)
    return pl.pallas_call(
        body, name=name, grid=grid, in_specs=[*part_specs, w_spec, w_spec, w_spec], out_specs=[w_spec] * 4,
        out_shape=[out] * 4, compiler_params=_params(*(("parallel",) * len(grid))),
    )(*parts, w, m, v)


def _layer_part_spec(layer, block, n_blocks, row_off=0):
    def index_map(l, i):
        ii = jnp.where(l == layer, i, jnp.where(l < layer, 0, n_blocks - 1))
        return (0, row_off + ii) + (0,) * (len(block) - 2)
    return pl.BlockSpec(block, index_map)


def _small_update(partials, triples):
    d = partials[-1].shape[-1]
    n_rep = len(triples)
    n_part = len(partials)
    rows = []
    for p in partials:
        rows.append(p.shape[0] * (p.shape[1] // d))
    offs = [sum(rows[:i]) for i in range(n_part)]
    total = -(-sum(rows) // 8) * 8

    def body(*refs):
        p_refs = refs[:n_part]
        wmv = refs[n_part:n_part + 3 * n_rep]
        outs = refs[n_part + 3 * n_rep:n_part + 3 * n_rep + 4 * n_rep + (n_part - n_rep)]
        buf, send_sems, recv_sems = refs[-3:]
        x, y, c = _my_place()
        me = _block_of(x, y, c)
        peers = _peers(x, y, c)
        mine = buf.at[me]
        if total > sum(rows):
            mine[sum(rows):total, :] = jnp.zeros((total - sum(rows), d), F32)
        for p_ref, off in zip(p_refs, offs):
            nr, nc = p_ref.shape[0], p_ref.shape[1] // d
            if nc == 1:
                mine[off:off + nr, :] = p_ref[...]
            else:
                for r in range(nr):
                    for q in range(nc):
                        mine[off + r * nc + q:off + r * nc + q + 1, :] = p_ref[r:r + 1, q * d:(q + 1) * d]
        sends =[pltpu.make_async_remote_copy(
            src_ref=buf.at[me], dst_ref=buf.at[me], send_sem=send_sems.at[r], recv_sem=recv_sems.at[r],
            device_id=peer, device_id_type=MESH) for r, peer in enumerate(peers)]
        for cp in sends:
            cp.start()
        for r, peer in enumerate(peers):
            pltpu.make_async_remote_copy(
                src_ref=buf.at[me], dst_ref=buf.at[_block_of(*peer)], send_sem=send_sems.at[r],
                recv_sem=recv_sems.at[r], device_id=peer, device_id_type=MESH).wait_recv()
        for cp in sends:
            cp.wait_send()
        tot = buf[0]
        for k in range(1, N_DEV):
            tot = tot + buf[k]
        buf[0] = tot
        for idx in range(n_part):
            nr, nc = p_refs[idx].shape[0], p_refs[idx].shape[1] // d
            if idx < n_rep:
                w_ref, m_ref, v_ref = wmv[3 * idx:3 * idx + 3]
                g_ref, d_ref, nm_ref, nv_ref = outs[4 * idx:4 * idx + 4]
            else:
                g_ref = outs[4 * n_rep + idx - n_rep]
            pieces = [(slice(0, nr), slice(0, d), offs[idx], nr)] if nc == 1 else [
                (slice(r, r + 1), slice(q * d, (q + 1) * d), offs[idx] + r * nc + q, 1)
                for r in range(nr) for q in range(nc)]
            for rws, cols, row, cnt in pieces:
                g = buf[0, row:row + cnt, :]
                g_ref[rws, cols] = g
                if idx < n_rep:
                    delta, nm, nv = _adamw_math(w_ref[rws, cols], g, m_ref[rws, cols], v_ref[rws, cols])
                    d_ref[rws, cols] = delta
                    nm_ref[rws, cols] = nm
                    nv_ref[rws, cols] = nv

    vm = pl.BlockSpec(memory_space=pltpu.VMEM)
    operands = list(partials)
    for t in triples:
        operands += list(t)
    out_shape = []
    for idx in range(n_rep):
        out_shape += [jax.ShapeDtypeStruct(partials[idx].shape, F32)] * 4
    for idx in range(n_rep, n_part):
        out_shape.append(jax.ShapeDtypeStruct(partials[idx].shape, F32))
    return pl.pallas_call(
        body, name="small_allreduce_adamw", in_specs=[vm] * len(operands), out_specs=[vm] * len(out_shape),
        out_shape=out_shape,
        scratch_shapes=[pltpu.VMEM((N_DEV, total, d), F32), pltpu.SemaphoreType.DMA((7,)), pltpu.SemaphoreType.DMA((7,))],
        compiler_params=pltpu.CompilerParams(vmem_limit_bytes=VMEM_LIMIT_BYTES),
    )(*operands)


def kernel(x, g_mix, w_in, b_in, conv_a, w_out_a, conv_b, conv_b_bias, ln_b_g, ln_b_b, w_out_b, b_out_b, w_pool, pool_scale, w_o, g_mlp, w_mlp1, w_mlp2, g_final, loss_target, m_g_mix, m_w_in, m_b_in, m_conv_a, m_w_out_a, m_conv_b, m_conv_b_bias, m_ln_b_g, m_ln_b_b, m_w_out_b, m_b_out_b, m_w_pool, m_pool_scale, m_w_o, m_g_mlp, m_w_mlp1, m_w_mlp2, m_g_final, v_g_mix, v_w_in, v_b_in, v_conv_a, v_w_out_a, v_conv_b, v_conv_b_bias, v_ln_b_g, v_ln_b_b, v_w_out_b, v_b_out_b, v_w_pool, v_pool_scale, v_w_o, v_g_mlp, v_w_mlp1, v_w_mlp2, v_g_final):
    _, s, d = x.shape
    n_layers = g_mix.shape[0]
    p_in = b_in.shape[1]
    ci = w_in.shape[2]
    c1 = w_mlp1.shape[2]
    rf = w_mlp2.shape[1]
    rd = w_out_a.shape[1]
    f = rf * N_DEV
    rp = rf + 3 * rd
    o_a, o_b, o_o = rf // rd, rf // rd + 1, rf // rd + 2
    gc = d // N_GROUPS
    ca_rows = 8
    tm = min(1024, s)
    tr = min(512, s)
    tx = min(256, s)
    tk = min(2048, s)
    tk_mlp = min(4096, s)
    tk_in = min(2048, s)

    me_arr = jnp.reshape(_block_of(*_my_place()), (1,)).astype(jnp.int32)

    def layer_shards(l):
        row_pack = jnp.concatenate([w_mlp2[l], w_out_a[l], w_out_b[l], w_o[l]], axis=0).astype(BF16)
        return [w_in[l].astype(BF16), w_mlp1[l].astype(BF16), row_pack, w_pool[l].astype(BF16)]

    conv_pack = jnp.concatenate(
        [conv_a, jnp.zeros((n_layers, ca_rows - K_A, rd), F32), conv_b], axis=1)
    first_shards = layer_shards(0)
    layer_pairing = [True, False, False, False]
    g_in_first, g_conv = _gather_shards([first_shards[0], conv_pack], [True, False])
    conv_full = jnp.transpose(g_conv, (1, 2, 0, 3)).reshape(n_layers, ca_rows + K_B, d)
    conv_a_f = conv_full[:, :K_A]
    conv_b_f = conv_full[:, ca_rows:]
    first_row_going = _start_copies("gather_start_row_0", first_shards[2:], g_conv, blockwise=False)
    in_flight = [_start_copies("gather_start_mlp1_0", first_shards[1:2], first_row_going["token"], blockwise=False)]
    for l in range(1, n_layers):
        in_flight.append(_start_copies(f"gather_start_{l}", layer_shards(l), in_flight[-1]["token"], blockwise=False,
                                       paired=layer_pairing))
    token = in_flight[-1]["token"][0:1, 0:1]

    xs = [x[0]]
    saved = []
    weights = []
    row2 = lambda j, i: (i, 0)
    for l in range(n_layers):
        x0 = xs[-1]
        vec = lambda a: a[l:l + 1]
        if l > 0:
            srcs, lands = _wait_copies(f"gather_wait_{l}", in_flight[l], x0, blockwise=False)
            g_in, g_1, g_row, g_pool = _place_own(f"gather_own_{l}", lands, srcs, me_arr, blockwise=False,
                                                  paired=layer_pairing)
        else:
            g_in = g_in_first
        h = _rms_fwd(f"rms_mix_{l}", x0, vec(g_mix) + token if l == 0 else vec(g_mix), tr)
        proj = _mm(
            f"proj_{l}", h, g_in, grid=(N_DEV // 2, s // tm), a_spec=pl.BlockSpec((tm, d), row2),
            b_spec=pl.BlockSpec((None, d, 2 * ci), lambda j, i: (j, 0, 0)),
            extras=(vec(b_in),), extra_specs=(pl.BlockSpec((1, 2 * ci), lambda j, i: (0, j)),),
            epilogue=lambda v, b: v + b, out_shape=jax.ShapeDtypeStruct((s, p_in), BF16),
            o_spec=pl.BlockSpec((tm, 2 * ci), lambda j, i: (i, j)), dims=NN)
        if l == 0:
            srcs, lands = _wait_copies("gather_wait_row_0", first_row_going, proj, blockwise=False)
            g_row, g_pool = _place_own("gather_own_row_0", lands, srcs, me_arr, blockwise=False)
        p_a, sw, p_c, cv, y_a, y_b, pw, merged, x1, h2 = _mixer_fwd(
            f"mix_fwd_{l}", proj, x0, conv_a_f[l], conv_b_f[l], vec(conv_b_bias), vec(ln_b_g), vec(ln_b_b),
            vec(b_out_b), vec(pool_scale), vec(g_mlp), g_row, g_pool, (o_a, o_b, o_o), d, tx)
        if l == 0:
            srcs, lands = _wait_copies("gather_wait_mlp1_0", in_flight[0], x1, blockwise=False)
            g_1, = _place_own("gather_own_mlp1_0", lands, srcs, me_arr, blockwise=False)
        weights.append((g_in, g_1, g_row, g_pool))
        a_pre = _mm(f"mlp1_{l}", h2, g_1, grid=(N_DEV // 2, s // tm), a_spec=pl.BlockSpec((tm, d), row2),
                    b_spec=pl.BlockSpec((2, d, c1), lambda j, i: (j, 0, 0)), slabs="n",
                    out_shape=jax.ShapeDtypeStruct((s, f), BF16),
                    o_spec=pl.BlockSpec((tm, 2 * c1), lambda j, i: (i, j)), dims=NN)
        x2 = _mm(f"mlp2_{l}", a_pre, g_row, grid=(1, s // tr), a_spec=pl.BlockSpec((tr, f), row2),
                 b_spec=pl.BlockSpec((N_DEV, rf, d), lambda j, i: (0, 0, 0)), prologue=_relu_sq,
                 extras=(x1,), extra_specs=(pl.BlockSpec((tr, d), row2),), epilogue=lambda v, r: v + r,
                 out_shape=jax.ShapeDtypeStruct((s, d), F32), o_spec=pl.BlockSpec((tr, d), row2), dims=NN)
        saved.append((x0, h, proj, p_a, sw, p_c, cv, y_a, y_b, pw, merged, x1, h2, a_pre))
        xs.append(x2)

    loss_part, dx, dx16, dg_final = _loss_head(xs[-1], g_final.reshape(1, d), loss_target[0], tr)
    loss = lax.psum(loss_part[0, 0], ("x", "y", "c"))

    small = [None] * n_layers
    exchanges = [None] * n_layers
    for l in reversed(range(n_layers)):
        x0, h, proj, p_a, sw, p_c, cv, y_a, y_b, pw, merged, x1, h2, a_pre = saved[l]
        g_in, g_1, g_row, g_pool = weights[l]
        vec = lambda a: a[l:l + 1]
        row_shape = jax.ShapeDtypeStruct((N_DEV, rp, d), BF16)

        def dd_grad(name, a, g, off, alias):
            return _mm(name, a, g, grid=(1, s // tk), a_spec=pl.BlockSpec((tk, d), lambda j, k: (k, 0)),
                       b_spec=pl.BlockSpec((tk, d), lambda j, k: (k, 0)), out_shape=row_shape,
                       o_spec=pl.BlockSpec((N_DEV, rd, d), lambda j, k: (0, off, 0)), dims=TN, nk=s // tk,
                       acc_shape=(d, d), alias_in=alias)

        d_a = _mm(f"d_act_{l}", dx16, g_row, grid=(N_DEV // 2, s // tm), a_spec=pl.BlockSpec((tm, d), row2),
                  b_spec=pl.BlockSpec((2, rf, d), lambda j, i: (j, 0, 0)), slabs="n",
                  extras=(a_pre,), extra_specs=(pl.BlockSpec((tm, 2 * rf), lambda j, i: (i, j)),),
                  epilogue=lambda v, a: v * (2.0 * jnp.maximum(a.astype(F32), 0.0)),
                  out_shape=jax.ShapeDtypeStruct((s, f), BF16),
                  o_spec=pl.BlockSpec((tm, 2 * rf), lambda j, i: (i, j)), dims=NT)
        dg_row = _mm(f"dw_mlp2_{l}", a_pre, dx16, grid=(N_DEV, s // tk_mlp),
                     a_spec=pl.BlockSpec((tk_mlp, rf), lambda j, k: (k, j)),
                     b_spec=pl.BlockSpec((tk_mlp, d), lambda j, k: (k, 0)), prologue=_relu_sq, out_shape=row_shape,
                     o_spec=pl.BlockSpec((None, rf, d), lambda j, k: (j, 0, 0)), dims=TN, nk=s // tk_mlp,
                     acc_shape=(rf, d))
        dg_1 = _mm(f"dw_mlp1_{l}", h2, d_a, grid=(N_DEV, s // tk_mlp),
                   a_spec=pl.BlockSpec((tk_mlp, d), lambda j, k: (k, 0)),
                   b_spec=pl.BlockSpec((tk_mlp, c1), lambda j, k: (k, j)),
                   out_shape=jax.ShapeDtypeStruct((N_DEV, d, c1), BF16),
                   o_spec=pl.BlockSpec((None, d, c1), lambda j, k: (j, 0, 0)), dims=TN, nk=s // tk_mlp,
                   acc_shape=(d, c1))
        mlp1_going = _start_copies(f"grads_start_mlp1_{l}", [dg_1], vec(g_mlp), blockwise=True)
        stream = [jax.ShapeDtypeStruct((s, d), F32), jax.ShapeDtypeStruct((s, d), BF16), jax.ShapeDtypeStruct((1, d), F32)]
        dx, dx16, dg_mlp = _mm(
            f"d_h2_{l}", d_a, g_1, grid=(1, s // tr), a_spec=pl.BlockSpec((tr, f), row2),
            b_spec=pl.BlockSpec((N_DEV, d, c1), lambda j, i: (0, 0, 0), pipeline_mode=pl.Buffered(1)), slabs="k",
            extras=(x1, vec(g_mlp), dx),
            extra_specs=(pl.BlockSpec((tr, d), row2), pl.BlockSpec((1, d), lambda j, i: (0, 0)),
                         pl.BlockSpec((tr, d), row2)),
            out_shape=stream, o_spec=[pl.BlockSpec((tr, d), row2), pl.BlockSpec((tr, d), row2),
                                      pl.BlockSpec((1, d), lambda j, i: (0, 0))],
            dims=NT, rms_bwd=(1, s // tr), after=mlp1_going["token"])
        d_ya, d_yb, d_pw, d_gates, d_pa, d_sw, d_pc, d_bout, d_pscale = _merge_bwd(
            f"merge_bwd_{l}", dx16, proj, y_a, y_b, pw, vec(pool_scale), g_row, g_pool, (o_a, o_b, o_o), d, tx)
        dg_row = dd_grad(f"dw_o_{l}", merged, dx16, o_o, dg_row)
        dg_row = dd_grad(f"dw_out_a_{l}", p_a, d_ya, o_a, dg_row)
        dg_row = dd_grad(f"dw_out_b_{l}", sw, d_yb, o_b, dg_row)
        dg_pool = _pool_wgrad(f"dw_pool_{l}", p_c, d_pw, d, tk)
        rest_going = _start_copies(f"grads_start_rest_{l}", [dg_row, dg_pool], vec(g_mlp), blockwise=True)
        d_proj, d_bin, d_ca, d_cb, d_cbb, d_lng, d_lnb = _mix_pre_bwd(
            f"mix_bwd_{l}", proj, cv, d_pa, d_sw, d_pc, d_gates, conv_a_f[l], conv_b_f[l],
            vec(ln_b_g) + rest_going["token"][0:1, 0:1], vec(ln_b_b), d, tx)
        dg_in = _mm(f"dw_in_{l}", h, d_proj, grid=(N_DEV // 2, s // tk_in),
                    a_spec=pl.BlockSpec((tk_in, d), lambda j, k: (k, 0)),
                    b_spec=pl.BlockSpec((tk_in, 2 * ci), lambda j, k: (k, j)),
                    out_shape=jax.ShapeDtypeStruct((N_DEV // 2, d, 2 * ci), BF16),
                    o_spec=pl.BlockSpec((None, d, 2 * ci), lambda j, k: (j, 0, 0)), dims=TN, nk=s // tk_in,
                    acc_shape=(d, 2 * ci), after=rest_going["token"])
        in_going = _start_copies(f"grads_start_in_{l}", [dg_in], vec(g_mix), blockwise=True, paired=[True])
        rows_ik = lambda i, k: (i, 0)
        once = dict(pipeline_mode=pl.Buffered(1))
        dx, dx16, dg_mix = _mm(
            f"d_h_{l}", d_proj, g_in, grid=(s // tm, N_DEV // 2), a_spec=pl.BlockSpec((tm, 2 * ci), lambda i, k: (i, k)),
            b_spec=pl.BlockSpec((None, d, 2 * ci), lambda i, k: (k, 0, 0)),
            extras=(x0, vec(g_mix), dx),
            extra_specs=(pl.BlockSpec((tm, d), rows_ik, **once), pl.BlockSpec((1, d), lambda i, k: (0, 0)),
                         pl.BlockSpec((tm, d), rows_ik, **once)),
            out_shape=stream, o_spec=[pl.BlockSpec((tm, d), rows_ik), pl.BlockSpec((tm, d), rows_ik),
                                      pl.BlockSpec((1, d), lambda i, k: (0, 0))],
            dims=NT, nk=N_DEV // 2, acc_shape=(tm, d), rms_bwd=(0, s // tm), after=in_going["token"])
        small[l] = (dg_mix, d_bin, d_cbb, d_lng, d_lnb, d_bout, d_pscale, dg_mlp, d_ca, d_cb)
        exchanges[l] = (in_going, mlp1_going, rest_going)

    grad_x = dx[None]

    names = ("g_mix", "b_in", "conv_b_bias", "ln_b_g", "ln_b_b", "b_out_b", "pool_scale", "g_mlp")
    given = dict(g_mix=(g_mix, m_g_mix, v_g_mix), b_in=(b_in, m_b_in, v_b_in),
                 conv_b_bias=(conv_b_bias, m_conv_b_bias, v_conv_b_bias), ln_b_g=(ln_b_g, m_ln_b_g, v_ln_b_g),
                 ln_b_b=(ln_b_b, m_ln_b_b, v_ln_b_b), b_out_b=(b_out_b, m_b_out_b, v_b_out_b),
                 pool_scale=(pool_scale, m_pool_scale, v_pool_scale), g_mlp=(g_mlp, m_g_mlp, v_g_mlp))
    partials, triples = [], []
    for i, nm in enumerate(names):
        partials.append(jnp.concatenate([small[l][i] for l in range(n_layers)], axis=0))
        triples.append(given[nm])
    partials.append(dg_final)
    triples.append(tuple(a.reshape(1, d) for a in (g_final, m_g_final, v_g_final)))
    partials.append(jnp.concatenate([small[l][8] for l in range(n_layers)], axis=0))
    partials.append(jnp.concatenate([small[l][9] for l in range(n_layers)], axis=0))
    outs = _small_update(partials, triples)
    rep = {nm: outs[4 * i:4 * i + 4] for i, nm in enumerate(names)}
    rep["g_final"] = [a.reshape(d) for a in outs[4 * len(names):4 * len(names) + 4]]
    me = _block_of(*_my_place())
    gca = lax.dynamic_slice_in_dim(outs[-2].reshape(n_layers, K_A, d), me * rd, rd, axis=2)
    gcb = lax.dynamic_slice_in_dim(outs[-1].reshape(n_layers, K_B, d), me * rd, rd, axis=2)

    r_in, r_1, r_row, r_pool = [], [], [], []
    for l in reversed(range(n_layers)):
        in_going, mlp1_going, rest_going = exchanges[l]
        srcs_m, lands_m = _wait_copies(f"grads_wait_mlp1_{l}", mlp1_going, outs[0], blockwise=True)
        srcs_r, lands_r = _wait_copies(f"grads_wait_rest_{l}", rest_going, outs[0], blockwise=True)
        srcs_i, lands_i = _wait_copies(f"grads_wait_in_{l}", in_going, outs[0], blockwise=True)
        got = _place_own(f"grads_own_{l}", lands_i + lands_m + lands_r, srcs_i + srcs_m + srcs_r, me_arr, blockwise=True,
                         paired=layer_pairing)
        for lst, arr in zip((r_in, r_1, r_row, r_pool), got):
            lst.insert(0, arr)
    tb = min(256, d)
    layers = range(n_layers)
    res = {}
    res["w_in"] = _adamw("adamw_w_in", r_in, w_in, m_w_in, v_w_in, grid=(n_layers, d // tb),
                         part_specs=[_layer_part_spec(li, (N_DEV, tb, ci), d // tb) for li in layers],
                         w_spec=pl.BlockSpec((None, tb, ci), lambda l, i: (l, i, 0)))
    res["w_mlp1"] = _adamw("adamw_w_mlp1", r_1, w_mlp1, m_w_mlp1, v_w_mlp1, grid=(n_layers, d // tb),
                           part_specs=[_layer_part_spec(li, (N_DEV, tb, c1), d // tb) for li in layers],
                           w_spec=pl.BlockSpec((None, tb, c1), lambda l, i: (l, i, 0)))
    tf = min(256, rf)
    res["w_mlp2"] = _adamw("adamw_w_mlp2", r_row, w_mlp2, m_w_mlp2, v_w_mlp2, grid=(n_layers, rf // tf),
                           part_specs=[_layer_part_spec(li, (N_DEV, tf, d), rf // tf) for li in layers],
                           w_spec=pl.BlockSpec((None, tf, d), lambda l, i: (l, i, 0)))
    for nm, off, trip in (("w_out_a", o_a, (w_out_a, m_w_out_a, v_w_out_a)),
                          ("w_out_b", o_b, (w_out_b, m_w_out_b, v_w_out_b)), ("w_o", o_o, (w_o, m_w_o, v_w_o))):
        res[nm] = _adamw(f"adamw_{nm}", r_row, *trip, grid=(n_layers, 1),
                         part_specs=[_layer_part_spec(li, (N_DEV, rd, d), 1, row_off=off) for li in layers],
                         w_spec=pl.BlockSpec((None, rd, d), lambda l, i: (l, 0, 0)))
    res["w_pool"] = _adamw("adamw_w_pool", r_pool, w_pool, m_w_pool, v_w_pool, grid=(n_layers, 1),
                           part_specs=[_layer_part_spec(li, (N_DEV, N_GROUPS, gc // N_DEV, gc), 1) for li in layers],
                           w_spec=pl.BlockSpec((None, N_GROUPS, gc // N_DEV, gc), lambda l, i: (l, 0, 0, 0)))
    whole3 = lambda: (0, 0, 0)
    res["conv_a"] = _adamw("adamw_conv_a", [gca[None]], conv_a, m_conv_a, v_conv_a, grid=(),
                           part_specs=[pl.BlockSpec((1, n_layers, K_A, rd), lambda: (0, 0, 0, 0))],
                           w_spec=pl.BlockSpec((n_layers, K_A, rd), whole3))
    res["conv_b"] = _adamw("adamw_conv_b", [gcb[None]], conv_b, m_conv_b, v_conv_b, grid=(),
                           part_specs=[pl.BlockSpec((1, n_layers, K_B, rd), lambda: (0, 0, 0, 0))],
                           w_spec=pl.BlockSpec((n_layers, K_B, rd), whole3))
    res.update(rep)

    order = ("g_mix", "w_in", "b_in", "conv_a", "w_out_a", "conv_b", "conv_b_bias", "ln_b_g", "ln_b_b", "w_out_b",
             "b_out_b", "w_pool", "pool_scale", "w_o", "g_mlp", "w_mlp1", "w_mlp2", "g_final")
    out = [loss, grad_x]
    for kind in range(4):
        out += [res[nm][kind] for nm in order]
    return tuple(out)
```

```python
import jax
import jax.numpy as jnp
from jax import lax
from jax.experimental import pallas as pl
from jax.experimental.pallas import tpu as pltpu

F32 = jnp.float32
BF16 = jnp.bfloat16
MESH = pl.DeviceIdType.MESH

N_DEV = 8
EPS = 1e-6
K_A = 3
K_B = 31
POOL_WINDOWS = (2, 4, 8, 16)
N_GROUPS = len(POOL_WINDOWS)
HALO = 32
CHUNK = 16
FWD_CHUNK = 32
SUBLANES = 8
TAP_GROUP = 16
ADAM_LR, ADAM_B1, ADAM_B2, ADAM_EPS, ADAM_WD, ADAM_STEP = 0.001, 0.9, 0.999, 1e-08, 0.01, 10
VMEM_LIMIT_BYTES = 60 * 1024 * 1024

NN = (((1,), (0,)), ((), ()))
NT = (((1,), (1,)), ((), ()))
TN = (((0,), (0,)), ((), ()))


def _params(*sem):
    return pltpu.CompilerParams(dimension_semantics=sem, vmem_limit_bytes=VMEM_LIMIT_BYTES)


def _sigmoid(v):
    return 1.0 / (1.0 + jnp.exp(-v))


def _mm(name, a, b, *, grid, a_spec, b_spec, out_shape, o_spec, dims, nk=1, acc_shape=None,
        extras=(), extra_specs=(), prologue=None, epilogue=None, alias_in=None, slabs=None, after=None,
        rms_bwd=None):
    n_extra = len(extras)
    has_alias = alias_in is not None
    n_unread = (1 if has_alias else 0) + (1 if after is not None else 0)

    def body(*refs):
        a_ref, b_ref = refs[0], refs[1]
        ex = refs[2:2 + n_extra]
        o_ref = refs[2 + n_extra + n_unread]
        av = a_ref[...]
        if prologue is not None:
            av = prologue(av)
        av = av.astype(BF16)

        def finish_rms(val):
            x_ref, g_ref, dr_ref = ex
            dx_ref, dx16_ref, dg_ref = refs[2 + n_extra + n_unread:5 + n_extra + n_unread]
            acc_g = refs[-1]
            row_axis, n_rows = rms_bwd
            ri = pl.program_id(row_axis)
            xv = x_ref[...]
            r = lax.rsqrt(jnp.mean(xv * xv, axis=-1, keepdims=True) + EPS)
            xh = xv * r
            part = _colsum8(val * xh)

            @pl.when(ri == 0)
            def _():
                acc_g[...] = part

            @pl.when(ri > 0)
            def _():
                acc_g[...] += part

            dxh = val * g_ref[...]
            dx = r * (dxh - xh * jnp.mean(dxh * xh, axis=-1, keepdims=True)) + dr_ref[...]
            dx_ref[...] = dx
            dx16_ref[...] = dx.astype(BF16)

            @pl.when(ri == n_rows - 1)
            def _():
                dg_ref[...] = jnp.sum(acc_g[...], axis=0, keepdims=True)

        def finish(val, cols=None):
            if rms_bwd is not None:
                return finish_rms(val)
            if epilogue is not None:
                val = epilogue(val, *[e[...] if cols is None else e[:, cols] for e in ex])
            if cols is None:
                o_ref[...] = val.astype(o_ref.dtype).reshape(o_ref.shape)
            else:
                o_ref[:, cols] = val.astype(o_ref.dtype)

        if slabs == "n":
            for q in range(b_ref.shape[0]):
                pq = lax.dot_general(av, b_ref[q].astype(BF16), dims, preferred_element_type=F32)
                finish(pq, slice(q * pq.shape[1], (q + 1) * pq.shape[1]))
            return
        if slabs == "k":
            kc = av.shape[1] // b_ref.shape[0]
            p = None
            for q in range(b_ref.shape[0]):
                pq = lax.dot_general(av[:, q * kc:(q + 1) * kc], b_ref[q].astype(BF16), dims,
                                     preferred_element_type=F32)
                p = pq if p is None else p + pq
        else:
            bv = b_ref[...]
            bv = bv.reshape((-1, bv.shape[-1])).astype(BF16)
            p = lax.dot_general(av, bv, dims, preferred_element_type=F32)

        if nk == 1:
            finish(p)
        else:
            acc = refs[-2] if rms_bwd is not None else refs[-1]
            k = pl.program_id(len(grid) - 1)

            @pl.when(k == 0)
            def _():
                acc[...] = p

            @pl.when(k > 0)
            def _():
                acc[...] += p

            @pl.when(k == nk - 1)
            def _():
                finish(acc[...])

    in_specs = [a_spec, b_spec, *extra_specs]
    operands = [a, b, *extras]
    aliases = {}
    if has_alias:
        in_specs.append(pl.BlockSpec(memory_space=pl.ANY))
        operands.append(alias_in)
        aliases = {len(operands) - 1: 0}
    if after is not None:
        in_specs.append(pl.BlockSpec(memory_space=pl.ANY))
        operands.append(after)
    sem = ("parallel",) * (len(grid) - 1) + (("arbitrary",) if nk > 1 else ("parallel",))
    scratch = [pltpu.VMEM(acc_shape, F32)] if nk > 1 else []
    if rms_bwd is not None:
        sem = ("arbitrary",) * len(grid)
        scratch.append(pltpu.VMEM((8, extras[0].shape[-1]), F32))
    return pl.pallas_call(
        body, name=name, grid=grid, in_specs=in_specs, out_specs=o_spec, out_shape=out_shape,
        scratch_shapes=scratch, input_output_aliases=aliases, compiler_params=_params(*sem),
    )(*operands)


def _relu_sq(v):
    r = jnp.maximum(v, 0)
    return r * r


def _rms_fwd(name, x, g, tm):
    s, d = x.shape

    def body(x_ref, g_ref, h_ref):
        xv = x_ref[...]
        r = lax.rsqrt(jnp.mean(xv * xv, axis=-1, keepdims=True) + EPS)
        h_ref[...] = (xv * r * g_ref[...]).astype(h_ref.dtype)

    return pl.pallas_call(
        body, name=name, grid=(s // tm,),
        in_specs=[pl.BlockSpec((tm, d), lambda i: (i, 0)), pl.BlockSpec((1, d), lambda i: (0, 0))],
        out_specs=pl.BlockSpec((tm, d), lambda i: (i, 0)),
        out_shape=jax.ShapeDtypeStruct((s, d), BF16), compiler_params=_params("parallel"),
    )(x, g)


def _colsum8(v):
    return jnp.sum(v.reshape(v.shape[0] // 8, 8, v.shape[1]), axis=0)


def _loss_head(x, g, target, tm):
    s, d = x.shape
    n = s // tm

    def body(x_ref, g_ref, t_ref, loss_ref, dx_ref, dx16_ref, dg_ref, acc_l, acc_g):
        i = pl.program_id(0)
        xv = x_ref[...]
        r = lax.rsqrt(jnp.mean(xv * xv, axis=-1, keepdims=True) + EPS)
        xh = xv * r
        err = xh * g_ref[...] - t_ref[...]
        dy = err * (1.0 / d)
        lpart = _colsum8(err * err)
        gpart = _colsum8(dy * xh)

        @pl.when(i == 0)
        def _():
            acc_l[...] = lpart
            acc_g[...] = gpart

        @pl.when(i > 0)
        def _():
            acc_l[...] += lpart
            acc_g[...] += gpart

        dxh = dy * g_ref[...]
        dx = r * (dxh - xh * jnp.mean(dxh * xh, axis=-1, keepdims=True))
        dx_ref[...] = dx
        dx16_ref[...] = dx.astype(BF16)

        @pl.when(i == n - 1)
        def _():
            loss_ref[...] = (0.5 / d) * jnp.sum(jnp.sum(acc_l[...], axis=0, keepdims=True), axis=1, keepdims=True)
            dg_ref[...] = jnp.sum(acc_g[...], axis=0, keepdims=True)

    return pl.pallas_call(
        body, name="loss_head", grid=(n,),
        in_specs=[pl.BlockSpec((tm, d), lambda i: (i, 0)), pl.BlockSpec((1, d), lambda i: (0, 0)),
                  pl.BlockSpec((tm, d), lambda i: (i, 0))],
        out_specs=[pl.BlockSpec((1, 1), lambda i: (0, 0)), pl.BlockSpec((tm, d), lambda i: (i, 0)),
                   pl.BlockSpec((tm, d), lambda i: (i, 0)), pl.BlockSpec((1, d), lambda i: (0, 0))],
        out_shape=[jax.ShapeDtypeStruct((1, 1), F32), jax.ShapeDtypeStruct((s, d), F32),
                   jax.ShapeDtypeStruct((s, d), BF16), jax.ShapeDtypeStruct((1, d), F32)],
        scratch_shapes=[pltpu.VMEM((8, d), F32), pltpu.VMEM((8, d), F32)], compiler_params=_params("arbitrary"),
    )(x, g, target)


def _sec(ref, n, d):
    return ref[:, n * d:(n + 1) * d].astype(F32)


def _fill_shifts(sh, ext):
    rows = ext.shape[0] - SUBLANES
    for b in range(1, SUBLANES):
        sh[b - 1, 0:rows, :] = ext[b:b + rows, :]


def _shifted(sh, ext, off, n):
    b = off % SUBLANES
    if b == 0:
        return ext[off:off + n, :]
    return sh[b - 1, off - b:off - b + n, :]


def _spread_taps(dst, w_ref):
    for k in range(w_ref.shape[0]):
        dst[k] = jnp.broadcast_to(w_ref[k:k + 1, :], dst.shape[1:])


def _times_tap(x, tap):
    return (x.reshape(x.shape[0] // SUBLANES, SUBLANES, x.shape[1]) * tap[None]).reshape(x.shape)


def _window_sums(src, s_a, s_b, gc, first, rows, back):
    n = src.shape[0]
    sign = -1 if back else 1
    lo = [SUBLANES * j if back else 0 for j in range(4)]
    hi = [n if back else n - SUBLANES * j for j in range(4)]
    sl = lambda j, shift: slice(lo[j] + shift, hi[j] + shift)
    s_a[sl(1, 0), :] = src[sl(1, 0), :] + src[sl(1, sign * 1), :]
    out = [s_a[first:first + rows, 0:gc]]
    s_b[sl(2, 0), gc:] = s_a[sl(2, 0), gc:] + s_a[sl(2, sign * 2), gc:]
    out.append(s_b[first:first + rows, gc:2 * gc])
    s_a[sl(3, 0), 2 * gc:] = s_b[sl(3, 0), 2 * gc:] + s_b[sl(3, sign * 4), 2 * gc:]
    out.append(s_a[first:first + rows, 2 * gc:3 * gc])
    out.append(s_a[first:first + rows, 3 * gc:] + s_a[first + sign * SUBLANES:first + sign * SUBLANES + rows, 3 * gc:])
    return out


def _pool_count(row0, rows, window):
    t = row0 + lax.broadcasted_iota(jnp.int32, (rows, 1), 0)
    return jnp.minimum(t + 1, window).astype(F32)


def _group_weight(w_ref, gi, gc):
    return w_ref[:, gi].reshape(gc, gc)


def _mixer_fwd(name, proj, x0, conv_a, conv_b, conv_b_bias, ln_g, ln_b, b_out_b, pool_scale, g_next, g_row, g_pool, offs,
               d, tm):
    s = proj.shape[0]
    n = s // tm
    gc = d // N_GROUPS
    hb = tm // HALO
    rd = d // N_DEV
    o_a, o_b, o_o = offs

    def body(pj_ref, hp_ref, x0_ref, ca_ref, cb_ref, cbb_ref, lng_ref, lnb_ref, bo_ref, sc_ref, gn_ref, wa_ref, wb_ref,
             wo_ref, wp_ref, pa_ref, sw_ref, pc_ref, cv_ref, ya_ref, yb_ref, pw_ref, mg_ref, x1_ref, h2_ref,
             eua, eub, euc, sh, taps_a, taps_b, sum_a, sum_b):
        i = pl.program_id(0)
        keep = (i > 0).astype(F32)
        _spread_taps(taps_a, ca_ref)
        _spread_taps(taps_b, cb_ref)
        eua[0:HALO, :] = _sec(hp_ref, 1, d) * _sec(hp_ref, 2, d) * keep
        eub[0:HALO, :] = _sec(hp_ref, 3, d) * _sigmoid(_sec(hp_ref, 4, d)) * keep
        euc[0:HALO, :] = _sec(hp_ref, 5, d) * keep
        eua[HALO:HALO + tm, :] = _sec(pj_ref, 1, d) * _sec(pj_ref, 2, d)
        eub[HALO:HALO + tm, :] = _sec(pj_ref, 3, d) * _sigmoid(_sec(pj_ref, 4, d))
        euc[HALO:HALO + tm, :] = _sec(pj_ref, 5, d)
        _fill_shifts(sh, eub)
        for c in range(tm // FWD_CHUNK):
            r0 = c * FWD_CHUNK
            z = jnp.zeros((FWD_CHUNK, d), F32)
            for k in range(K_A):
                z = z + _times_tap(eua[HALO + r0 - (K_A - 1) + k:HALO + r0 - (K_A - 1) + k + FWD_CHUNK, :], taps_a[k])
            pa_ref[r0:r0 + FWD_CHUNK, :] = (pj_ref[r0:r0 + FWD_CHUNK, 0:d].astype(F32) * z).astype(pa_ref.dtype)
            cv = jnp.zeros((FWD_CHUNK, d), F32) + cbb_ref[...]
            for k in range(K_B):
                cv = cv + _times_tap(_shifted(sh, eub, HALO + r0 - (K_B - 1) + k, FWD_CHUNK), taps_b[k])
            cv_ref[r0:r0 + FWD_CHUNK, :] = cv.astype(cv_ref.dtype)
        cvv = cv_ref[...].astype(F32)
        mu = jnp.mean(cvv, axis=-1, keepdims=True)
        xc = cvv - mu
        xh = xc * lax.rsqrt(jnp.mean(xc * xc, axis=-1, keepdims=True) + EPS)
        ln = xh * lng_ref[...] + lnb_ref[...]
        sw_ref[...] = (ln * _sigmoid(ln)).astype(sw_ref.dtype)
        sums = _window_sums(euc, sum_a, sum_b, gc, HALO, tm, back=True)
        for gi, w in enumerate(POOL_WINDOWS):
            cols = slice(gi * gc, (gi + 1) * gc)
            cnt = _pool_count(i * tm, tm, w)
            pc_ref[:, cols] = (sums[gi] / cnt - euc[HALO:HALO + tm, cols]).astype(pc_ref.dtype)
        ya_ref[...] = jnp.dot(pa_ref[...], wa_ref[...].reshape(d, d), preferred_element_type=F32).astype(ya_ref.dtype)
        yb_ref[...] = (jnp.dot(sw_ref[...], wb_ref[...].reshape(d, d), preferred_element_type=F32)
                       + bo_ref[...]).astype(yb_ref.dtype)
        for gi in range(N_GROUPS):
            cols = slice(gi * gc, (gi + 1) * gc)
            pw_ref[:, cols] = jnp.dot(pc_ref[:, cols], _group_weight(wp_ref, gi, gc),
                                      preferred_element_type=F32).astype(pw_ref.dtype)
        m = _sigmoid(_sec(pj_ref, 6, d)) * ya_ref[...].astype(F32)
        m = m + _sigmoid(_sec(pj_ref, 7, d)) * yb_ref[...].astype(F32)
        m = m + _sigmoid(_sec(pj_ref, 8, d)) * (pw_ref[...].astype(F32) * sc_ref[...])
        mg_ref[...] = m.astype(mg_ref.dtype)
        x1 = x0_ref[...] + jnp.dot(mg_ref[...], wo_ref[...].reshape(d, d), preferred_element_type=F32)
        x1_ref[...] = x1
        h2_ref[...] = (x1 * lax.rsqrt(jnp.mean(x1 * x1, axis=-1, keepdims=True) + EPS) * gn_ref[...]).astype(h2_ref.dtype)

    row = lambda i: (i, 0)
    fixed = lambda i: (0, 0)
    act = jax.ShapeDtypeStruct((s, d), BF16)

    def dd_weight(off):
        return pl.BlockSpec((N_DEV, rd, d), lambda i: (0, off, 0), pipeline_mode=pl.Buffered(1))

    return pl.pallas_call(
        body, name=name, grid=(n,),
        in_specs=[pl.BlockSpec((tm, 9 * d), row),
                  pl.BlockSpec((HALO, 6 * d), lambda i: (jnp.maximum(i * hb - 1, 0), 0)),
                  pl.BlockSpec((tm, d), row),
                  pl.BlockSpec((K_A, d), fixed), pl.BlockSpec((K_B, d), fixed), pl.BlockSpec((1, d), fixed),
                  pl.BlockSpec((1, d), fixed), pl.BlockSpec((1, d), fixed), pl.BlockSpec((1, d), fixed),
                  pl.BlockSpec((1, d), fixed), pl.BlockSpec((1, d), fixed), dd_weight(o_a), dd_weight(o_b), dd_weight(o_o),
                  pl.BlockSpec((N_DEV, N_GROUPS, gc // N_DEV, gc), lambda i: (0, 0, 0, 0),
                               pipeline_mode=pl.Buffered(1))],
        out_specs=[pl.BlockSpec((tm, d), row)] * 10,
        out_shape=[act] * 8 + [jax.ShapeDtypeStruct((s, d), F32), act],
        scratch_shapes=[pltpu.VMEM((tm + HALO, d), F32)] * 3 + [pltpu.VMEM((SUBLANES - 1, tm + HALO, d), F32),
                                                                pltpu.VMEM((K_A, SUBLANES, d), F32),
                                                                pltpu.VMEM((K_B, SUBLANES, d), F32)]
                       + [pltpu.VMEM((tm + HALO, d), F32)] * 2,
        compiler_params=_params("parallel"),
    )(proj, proj, x0, conv_a, conv_b, conv_b_bias, ln_g, ln_b, b_out_b, pool_scale, g_next, g_row, g_row, g_row,
      g_pool)


def _merge_bwd(name, dx16, proj, ya, yb, pw, pool_scale, g_row, g_pool, offs, d, tm):
    s = proj.shape[0]
    n = s // tm
    gc = d // N_GROUPS
    rd = d // N_DEV
    o_a, o_b, o_o = offs

    def body(dx_ref, g_ref, ya_ref, yb_ref, pw_ref, sc_ref, wa_ref, wb_ref, wo_ref, wp_ref,
             dya_ref, dyb_ref, dpw_ref, dg_ref, dpa_ref, dsw_ref, dpc_ref, dbo_ref, dsc_ref, acc_b, acc_s):
        i = pl.program_id(0)
        dmv = lax.dot_general(dx_ref[...], wo_ref[...].reshape(d, d), NT,
                              preferred_element_type=F32).astype(BF16).astype(F32)
        scale = sc_ref[...]
        g0 = _sigmoid(_sec(g_ref, 0, d))
        dya_ref[...] = (dmv * g0).astype(dya_ref.dtype)
        dg_ref[:, 0:d] = (dmv * ya_ref[...].astype(F32) * g0 * (1.0 - g0)).astype(dg_ref.dtype)
        g1 = _sigmoid(_sec(g_ref, 1, d))
        dyb = dmv * g1
        dyb_ref[...] = dyb.astype(dyb_ref.dtype)
        dg_ref[:, d:2 * d] = (dmv * yb_ref[...].astype(F32) * g1 * (1.0 - g1)).astype(dg_ref.dtype)
        g2 = _sigmoid(_sec(g_ref, 2, d))
        pwv = pw_ref[...].astype(F32)
        dyc = dmv * g2
        dpw_ref[...] = (dyc * scale).astype(dpw_ref.dtype)
        dg_ref[:, 2 * d:3 * d] = (dmv * (pwv * scale) * g2 * (1.0 - g2)).astype(dg_ref.dtype)
        pb = _colsum8(dyb)
        ps = _colsum8(dyc * pwv)

        @pl.when(i == 0)
        def _():
            acc_b[...] = pb
            acc_s[...] = ps

        @pl.when(i > 0)
        def _():
            acc_b[...] += pb
            acc_s[...] += ps

        dpa_ref[...] = lax.dot_general(dya_ref[...], wa_ref[...].reshape(d, d), NT,
                                       preferred_element_type=F32).astype(dpa_ref.dtype)
        dsw_ref[...] = lax.dot_general(dyb_ref[...], wb_ref[...].reshape(d, d), NT,
                                       preferred_element_type=F32).astype(dsw_ref.dtype)
        for gi in range(N_GROUPS):
            cols = slice(gi * gc, (gi + 1) * gc)
            dpc_ref[:, cols] = lax.dot_general(dpw_ref[:, cols], _group_weight(wp_ref, gi, gc), NT,
                                               preferred_element_type=F32).astype(dpc_ref.dtype)

        @pl.when(i == n - 1)
        def _():
            dbo_ref[...] = jnp.sum(acc_b[...], axis=0, keepdims=True)
            dsc_ref[...] = jnp.sum(acc_s[...], axis=0, keepdims=True)

    row = lambda i: (i, 0)
    fixed = lambda i: (0, 0)
    act = jax.ShapeDtypeStruct((s, d), BF16)
    vec = jax.ShapeDtypeStruct((1, d), F32)

    def dd_weight(off):
        return pl.BlockSpec((N_DEV, rd, d), lambda i: (0, off, 0), pipeline_mode=pl.Buffered(1))

    return pl.pallas_call(
        body, name=name, grid=(n,),
        in_specs=[pl.BlockSpec((tm, d), row), pl.BlockSpec((tm, 3 * d), lambda i: (i, 2)), pl.BlockSpec((tm, d), row),
                  pl.BlockSpec((tm, d), row), pl.BlockSpec((tm, d), row), pl.BlockSpec((1, d), fixed),
                  dd_weight(o_a), dd_weight(o_b), dd_weight(o_o),
                  pl.BlockSpec((N_DEV, N_GROUPS, gc // N_DEV, gc), lambda i: (0, 0, 0, 0),
                               pipeline_mode=pl.Buffered(1))],
        out_specs=[pl.BlockSpec((tm, d), row)] * 3 + [pl.BlockSpec((tm, 3 * d), row)] + [pl.BlockSpec((tm, d), row)] * 3
                  + [pl.BlockSpec((1, d), fixed)] * 2,
        out_shape=[act, act, act, jax.ShapeDtypeStruct((s, 3 * d), BF16), act, act, act, vec, vec],
        scratch_shapes=[pltpu.VMEM((8, d), F32)] * 2, compiler_params=_params("arbitrary"),
    )(dx16, proj, ya, yb, pw, pool_scale, g_row, g_row, g_row, g_pool)


def _mix_pre_bwd(name, proj, cv, dpa, dsw, dpc, dgates, conv_a, conv_b, ln_g, ln_b, d, tm):
    s = proj.shape[0]
    n = s // tm
    gc = d // N_GROUPS
    hb = tm // HALO
    last_halo = s // HALO - 1
    te = tm + HALO

    def ln_bwd(cvv, dswv, lng, lnb):
        mu = jnp.mean(cvv, axis=-1, keepdims=True)
        xc = cvv - mu
        rstd = lax.rsqrt(jnp.mean(xc * xc, axis=-1, keepdims=True) + EPS)
        xh = xc * rstd
        ln = xh * lng + lnb
        sg = _sigmoid(ln)
        dln = dswv * (sg * (1.0 + ln * (1.0 - sg)))
        dxh = dln * lng
        dcv = rstd * (dxh - jnp.mean(dxh, axis=-1, keepdims=True) - xh * jnp.mean(dxh * xh, axis=-1, keepdims=True))
        return dcv, dln, xh

    def body(pj_ref, hp_ref, hf_ref, cv_ref, cvf_ref, dpa_ref, dpaf_ref, dsw_ref, dswf_ref, dpc_ref, dpcf_ref, dgt_ref,
             ca_ref, cb_ref, lng_ref, lnb_ref,
             dpj_ref, dbin_ref, dca_ref, dcb_ref, dcbb_ref, dlng_ref, dlnb_ref,
             eua, eub, edz, edcv, eq, sh, dub_s, taps_a, taps_b, sum_a, sum_b, acc_bin, acc_ca, acc_cb, acc_v):
        i = pl.program_id(0)
        keep_p = (i > 0).astype(F32)
        keep_f = (i < n - 1).astype(F32)
        _spread_taps(taps_a, ca_ref)
        _spread_taps(taps_b, cb_ref)

        @pl.when(i == 0)
        def _():
            acc_bin[...] = jnp.zeros_like(acc_bin)
            acc_ca[...] = jnp.zeros_like(acc_ca)
            acc_cb[...] = jnp.zeros_like(acc_cb)
            acc_v[...] = jnp.zeros_like(acc_v)

        eua[0:HALO, :] = _sec(hp_ref, 1, d) * _sec(hp_ref, 2, d) * keep_p
        eua[HALO:te, :] = _sec(pj_ref, 1, d) * _sec(pj_ref, 2, d)
        eub[HALO:te, :] = _sec(pj_ref, 3, d) * _sigmoid(_sec(pj_ref, 4, d))
        edz[0:tm, :] = dpa_ref[...].astype(F32) * _sec(pj_ref, 0, d)
        edz[tm:te, :] = dpaf_ref[...].astype(F32) * _sec(hf_ref, 0, d) * keep_f
        dcv, dln, xh = ln_bwd(cv_ref[...].astype(F32), dsw_ref[...].astype(F32), lng_ref[...], lnb_ref[...])
        edcv[0:tm, :] = dcv
        acc_v[0:8, :] += _colsum8(dcv)
        acc_v[8:16, :] += _colsum8(dln * xh)
        acc_v[16:24, :] += _colsum8(dln)
        dcvf, _, _ = ln_bwd(cvf_ref[...].astype(F32), dswf_ref[...].astype(F32), lng_ref[...], lnb_ref[...])
        edcv[tm:te, :] = dcvf * keep_f
        for gi, w in enumerate(POOL_WINDOWS):
            cols = slice(gi * gc, (gi + 1) * gc)
            eq[0:tm, cols] = dpc_ref[:, cols].astype(F32) / _pool_count(i * tm, tm, w)
            eq[tm:te, cols] = dpcf_ref[:, cols].astype(F32) / _pool_count((i + 1) * tm, HALO, w) * keep_f

        def put(sec_idx, r0, val):
            dpj_ref[r0:r0 + CHUNK, sec_idx * d:(sec_idx + 1) * d] = val.astype(dpj_ref.dtype)
            acc_bin[:, sec_idx * d:(sec_idx + 1) * d] += _colsum8(val)

        _fill_shifts(sh, edcv)
        for k0 in range(0, K_B, TAP_GROUP):
            taps = range(k0, min(k0 + TAP_GROUP, K_B))
            a = {k: jnp.zeros((8, d), F32) for k in taps}
            for c in range(tm // CHUNK):
                r0 = c * CHUNK
                ub = eub[HALO + r0:HALO + r0 + CHUNK, :]
                part = None
                for k in taps:
                    t = _shifted(sh, edcv, r0 + (K_B - 1) - k, CHUNK)
                    part = _times_tap(t, taps_b[k]) if part is None else part + _times_tap(t, taps_b[k])
                    a[k] = a[k] + _colsum8(ub * t)
                if k0 == 0:
                    dub_s[r0:r0 + CHUNK, :] = part
                else:
                    dub_s[r0:r0 + CHUNK, :] += part
            for k in taps:
                acc_cb[k] += a[k]
        wa = [jnp.zeros((8, d), F32) for _ in range(K_A)]
        for c in range(tm // CHUNK):
            r0 = c * CHUNK
            rows = slice(r0, r0 + CHUNK)
            z = jnp.zeros((CHUNK, d), F32)
            dua = jnp.zeros((CHUNK, d), F32)
            ua = eua[HALO + r0:HALO + r0 + CHUNK, :]
            for k in range(K_A):
                z = z + _times_tap(eua[HALO + r0 - (K_A - 1) + k:HALO + r0 - (K_A - 1) + k + CHUNK, :], taps_a[k])
                t = edz[r0 + (K_A - 1) - k:r0 + (K_A - 1) - k + CHUNK, :]
                dua = dua + _times_tap(t, taps_a[k])
                wa[k] = wa[k] + _colsum8(ua * t)
            put(0, r0, dpa_ref[rows, :].astype(F32) * z)
            put(1, r0, dua * pj_ref[rows, 2 * d:3 * d].astype(F32))
            put(2, r0, dua * pj_ref[rows, d:2 * d].astype(F32))
            dub = dub_s[rows, :]
            bval = pj_ref[rows, 3 * d:4 * d].astype(F32)
            sg = _sigmoid(pj_ref[rows, 4 * d:5 * d].astype(F32))
            put(3, r0, dub * sg)
            put(4, r0, dub * bval * sg * (1.0 - sg))
        sums = _window_sums(eq, sum_a, sum_b, gc, 0, tm, back=False)
        for gi in range(N_GROUPS):
            cols = slice(gi * gc, (gi + 1) * gc)
            dci = sums[gi] - dpc_ref[:, cols].astype(F32)
            dpj_ref[:, 5 * d + gi * gc:5 * d + (gi + 1) * gc] = dci.astype(dpj_ref.dtype)
            acc_bin[:, 5 * d + gi * gc:5 * d + (gi + 1) * gc] += _colsum8(dci)
        for q in range(3):
            gv = dgt_ref[:, q * d:(q + 1) * d]
            dpj_ref[:, (6 + q) * d:(7 + q) * d] = gv
            acc_bin[:, (6 + q) * d:(7 + q) * d] += _colsum8(gv.astype(F32))
        for k in range(K_A):
            acc_ca[k] += wa[k]

        @pl.when(i == n - 1)
        def _():
            dbin_ref[...] = jnp.sum(acc_bin[...], axis=0, keepdims=True)
            for k in range(K_A):
                dca_ref[k:k + 1, :] = jnp.sum(acc_ca[k], axis=0, keepdims=True)
            for k in range(K_B):
                dcb_ref[k:k + 1, :] = jnp.sum(acc_cb[k], axis=0, keepdims=True)
            dcbb_ref[...] = jnp.sum(acc_v[0:8, :], axis=0, keepdims=True)
            dlng_ref[...] = jnp.sum(acc_v[8:16, :], axis=0, keepdims=True)
            dlnb_ref[...] = jnp.sum(acc_v[16:24, :], axis=0, keepdims=True)

    row = lambda i: (i, 0)
    fixed = lambda i: (0, 0)
    past = lambda i: (jnp.maximum(i * hb - 1, 0), 0)
    fut = lambda i: (jnp.minimum((i + 1) * hb, last_halo), 0)
    vec = jax.ShapeDtypeStruct((1, d), F32)
    tile_and_halo = [pl.BlockSpec((tm, d), row), pl.BlockSpec((HALO, d), fut)]
    return pl.pallas_call(
        body, name=name, grid=(n,),
        in_specs=[pl.BlockSpec((tm, 6 * d), row), pl.BlockSpec((HALO, 6 * d), past), pl.BlockSpec((HALO, 6 * d), fut),
                  *tile_and_halo, *tile_and_halo, *tile_and_halo, *tile_and_halo,
                  pl.BlockSpec((tm, 3 * d), row),
                  pl.BlockSpec((K_A, d), fixed), pl.BlockSpec((K_B, d), fixed), pl.BlockSpec((1, d), fixed),
                  pl.BlockSpec((1, d), fixed)],
        out_specs=[pl.BlockSpec((tm, 9 * d), row), pl.BlockSpec((1, 9 * d), fixed), pl.BlockSpec((K_A, d), fixed),
                   pl.BlockSpec((K_B, d), fixed), pl.BlockSpec((1, d), fixed), pl.BlockSpec((1, d), fixed),
                   pl.BlockSpec((1, d), fixed)],
        out_shape=[jax.ShapeDtypeStruct((s, 9 * d), BF16), jax.ShapeDtypeStruct((1, 9 * d), F32),
                   jax.ShapeDtypeStruct((K_A, d), F32), jax.ShapeDtypeStruct((K_B, d), F32), vec, vec, vec],
        scratch_shapes=[pltpu.VMEM((te, d), F32)] * 5 + [pltpu.VMEM((SUBLANES - 1, te, d), F32),
                                                         pltpu.VMEM((tm, d), F32),
                                                         pltpu.VMEM((K_A, SUBLANES, d), F32),
                                                         pltpu.VMEM((K_B, SUBLANES, d), F32),
                                                         pltpu.VMEM((te, d), F32), pltpu.VMEM((te, d), F32),
                                                         pltpu.VMEM((8, 9 * d), F32), pltpu.VMEM((K_A, 8, d), F32),
                                                         pltpu.VMEM((K_B, 8, d), F32), pltpu.VMEM((24, d), F32)],
        compiler_params=_params("arbitrary"),
    )(proj, proj, proj, cv, cv, dpa, dpa, dsw, dsw, dpc, dpc, dgates, conv_a, conv_b, ln_g, ln_b)


def _pool_wgrad(name, p, dpw, d, tk):
    s = p.shape[0]
    gc = d // N_GROUPS
    n = s // tk

    def body(p_ref, g_ref, o_ref, acc):
        k = pl.program_id(0)
        for gi in range(N_GROUPS):
            cols = slice(gi * gc, (gi + 1) * gc)
            part = lax.dot_general(p_ref[:, cols], g_ref[:, cols], TN, preferred_element_type=F32)

            @pl.when(k == 0)
            def _():
                acc[gi] = part

            @pl.when(k > 0)
            def _():
                acc[gi] += part

        @pl.when(k == n - 1)
        def _():
            for gi in range(N_GROUPS):
                o_ref[:, gi] = acc[gi].astype(o_ref.dtype).reshape(N_DEV, gc // N_DEV, gc)

    return pl.pallas_call(
        body, name=name, grid=(n,),
        in_specs=[pl.BlockSpec((tk, d), lambda k: (k, 0)), pl.BlockSpec((tk, d), lambda k: (k, 0))],
        out_specs=pl.BlockSpec((N_DEV, N_GROUPS, gc // N_DEV, gc), lambda k: (0, 0, 0, 0)),
        out_shape=jax.ShapeDtypeStruct((N_DEV, N_GROUPS, gc // N_DEV, gc), BF16),
        scratch_shapes=[pltpu.VMEM((N_GROUPS, gc, gc), F32)], compiler_params=_params("arbitrary"),
    )(p, dpw)


def _my_place():
    x, y, c = lax.axis_index("x"), lax.axis_index("y"), lax.axis_index("c")
    return x, y, c


def _block_of(x, y, c):
    return 4 * x + 2 * y + c


def _slot(ref, k, paired):
    if not paired:
        return ref.at[k]
    cols = ref.shape[-1] // 2
    return ref.at[k // 2, :, pl.ds(pl.multiple_of((k % 2) * cols, 128), cols)]


def _slot_shape(shape, paired):
    return (N_DEV // 2, shape[0], 2 * shape[1]) if paired else (N_DEV, *shape)


def _gather_shards(shards, paired):
    n_arr = len(shards)

    def body(*refs):
        srcs = refs[:n_arr]
        outs = refs[n_arr:2 * n_arr]
        send_sems, recv_sems, local_sems = refs[2 * n_arr:]
        x, y, c = _my_place()
        me, sibling = (x, y, c), (x, y, 1 - c)
        chips = [(1 - x, y), (x, 1 - y), (1 - x, 1 - y)]

        def copy(n, k, block, to, src=None):
            rows = _slot(outs[n], _block_of(*block), paired[n])
            return pltpu.make_async_remote_copy(
                src_ref=rows if src is None else src, dst_ref=rows, send_sem=send_sems.at[n, k],
                recv_sem=recv_sems.at[n, k], device_id=to, device_id_type=MESH)

        mine = [pltpu.make_async_copy(srcs[n], _slot(outs[n], _block_of(*me), paired[n]), local_sems.at[n])
                for n in range(n_arr)]
        for cp in mine:
            cp.start()
        first = []
        for n in range(n_arr):
            first.append(copy(n, 0, me, sibling, src=srcs[n]))
            first += [copy(n, 1 + j, me, (*chip, c), src=srcs[n]) for j, chip in enumerate(chips)]
        for cp in first:
            cp.start()
        passed = []
        for n in range(n_arr):
            for j, chip in enumerate(chips):
                copy(n, 1 + j, (*chip, c), me).wait_recv()
                fwd = copy(n, 4 + j, (*chip, c), sibling)
                fwd.start()
                passed.append(fwd)
        for n in range(n_arr):
            copy(n, 0, sibling, me).wait_recv()
            for j, chip in enumerate(chips):
                copy(n, 4 + j, (*chip, 1 - c), me).wait_recv()
        for cp in first + passed:
            cp.wait_send()
        for cp in mine:
            cp.wait()

    any_spec = pl.BlockSpec(memory_space=pl.ANY)
    return pl.pallas_call(
        body, name="gather_weights",
        in_specs=[any_spec] * n_arr, out_specs=[any_spec] * n_arr,
        out_shape=[jax.ShapeDtypeStruct(_slot_shape(sh.shape, p), sh.dtype) for sh, p in zip(shards, paired)],
        scratch_shapes=[pltpu.SemaphoreType.DMA((n_arr, 7)), pltpu.SemaphoreType.DMA((n_arr, 7)),
                        pltpu.SemaphoreType.DMA((n_arr,))],
    )(*shards)


def _peers(x, y, c):
    out = []
    for r in range(1, N_DEV):
        fx, fy, fc = (r >> 2) & 1, (r >> 1) & 1, r & 1
        out.append(((1 - x) if fx else x, (1 - y) if fy else y, (1 - c) if fc else c))
    return out


HBM_SPEC = pl.BlockSpec(memory_space=pltpu.HBM)
SEM_SPEC = pl.BlockSpec(memory_space=pltpu.SEMAPHORE)
ANY_SPEC = pl.BlockSpec(memory_space=pl.ANY)
N_PEERS = N_DEV - 1


def _peer_copy(src_ref, land_ref, send_sems, recv_sems, i, r, peer, me, blockwise, paired):
    src = _slot(src_ref, _block_of(*peer), paired) if blockwise else src_ref
    dst = land_ref.at[me] if blockwise else _slot(land_ref, me, paired)
    return pltpu.make_async_remote_copy(
        src_ref=src, dst_ref=dst, send_sem=send_sems.at[i * N_PEERS + r],
        recv_sem=recv_sems.at[i * N_PEERS + r], device_id=peer, device_id_type=MESH)


def _block_shape(shape, paired):
    return (shape[1], shape[2] // 2) if paired else tuple(shape[1:])


def _start_copies(name, srcs, after, blockwise, paired=None):
    n = len(srcs)
    paired = paired or [False] * n

    def body(*refs):
        s_in, l_in = refs[:n], refs[n:2 * n]
        send_sems, recv_sems = refs[2 * n + 1], refs[2 * n + 2]
        token = refs[-1]
        x, y, c = _my_place()
        me = _block_of(x, y, c)
        for i in range(n):
            for r, peer in enumerate(_peers(x, y, c)):
                _peer_copy(s_in[i], l_in[i], send_sems, recv_sems, i, r, peer, me, blockwise, paired[i]).start()
        token[...] = jnp.zeros_like(token)

    land_shapes = [(N_DEV, *_block_shape(s.shape, p)) if blockwise else _slot_shape(s.shape, p)
                   for s, p in zip(srcs, paired)]
    lands = [pltpu.with_memory_space_constraint(lax.empty(sh, s.dtype), pltpu.HBM) for sh, s in zip(land_shapes, srcs)]
    ins = [pltpu.with_memory_space_constraint(s, pltpu.HBM) for s in srcs]
    out = pl.pallas_call(
        body, name=name,
        out_shape=(pltpu.SemaphoreType.DMA((n * N_PEERS,)), pltpu.SemaphoreType.DMA((n * N_PEERS,)),
                   *[pltpu.HBM(s.shape, s.dtype) for s in srcs],
                   *[pltpu.HBM(sh, s.dtype) for sh, s in zip(land_shapes, srcs)],
                   jax.ShapeDtypeStruct((8, 128), F32)),
        in_specs=[HBM_SPEC] * (2 * n) + [ANY_SPEC],
        out_specs=(SEM_SPEC, SEM_SPEC, *[HBM_SPEC] * (2 * n), pl.BlockSpec(memory_space=pltpu.VMEM)),
        input_output_aliases={i: 2 + i for i in range(2 * n)},
        compiler_params=pltpu.CompilerParams(has_side_effects=pltpu.SideEffectType.DATAFLOW_SIDE_EFFECTING),
    )(*ins, *lands, after)
    return dict(send=out[0], recv=out[1], srcs=list(out[2:2 + n]), lands=list(out[2 + n:2 + 2 * n]), token=out[-1],
                paired=paired)


def _wait_copies(name, state, after, blockwise):
    n = len(state["srcs"])
    paired = state["paired"]

    def body(*refs):
        s_in, l_in = refs[:n], refs[n:2 * n]
        send_sems, recv_sems = refs[2 * n], refs[2 * n + 1]
        x, y, c = _my_place()
        me = _block_of(x, y, c)
        for i in range(n):
            for r, peer in enumerate(_peers(x, y, c)):
                cp = _peer_copy(s_in[i], l_in[i], send_sems, recv_sems, i, r, peer, me, blockwise, paired[i])
                cp.wait_send()
                cp.wait_recv()

    both = state["srcs"] + state["lands"]
    out = pl.pallas_call(
        body, name=name, out_shape=tuple(pltpu.HBM(a.shape, a.dtype) for a in both),
        in_specs=[HBM_SPEC] * (2 * n) + [SEM_SPEC, SEM_SPEC, ANY_SPEC], out_specs=tuple([HBM_SPEC] * (2 * n)),
        input_output_aliases={i: i for i in range(2 * n)},
        compiler_params=pltpu.CompilerParams(has_side_effects=pltpu.SideEffectType.DATAFLOW_SIDE_EFFECTING),
    )(*both, state["send"], state["recv"], after)
    return list(out[:n]), list(out[n:])


COPY_BLOCK_BYTES = 2 * 1024 * 1024


def _place_own(name, lands, srcs, me, blockwise, paired=None):
    out = []
    paired = paired or [False] * len(lands)
    for i, (land, src) in enumerate(zip(lands, srcs)):
        in_slots = src if blockwise else land
        part = _block_shape(in_slots.shape, paired[i])
        row_bytes = land.dtype.itemsize
        for extent in part[1:]:
            row_bytes *= extent
        tr = part[0]
        while tr * row_bytes > COPY_BLOCK_BYTES and tr % 16 == 0:
            tr //= 2
        tail = (0,) * (len(part) - 1)

        def body(me_ref, s_ref, l_ref, o_ref):
            o_ref[...] = s_ref[...]

        if paired[i]:
            slot_spec = pl.BlockSpec((None, tr, part[1]), lambda j, me_ref: (me_ref[0] // 2, j, me_ref[0] % 2))
        else:
            slot_spec = pl.BlockSpec((None, tr, *part[1:]), lambda j, me_ref: (me_ref[0], j, *tail))
        if blockwise:
            s_spec = slot_spec
            o_spec = pl.BlockSpec((None, tr, *part[1:]), lambda j, me_ref: (me_ref[0], j, *tail))
        else:
            s_spec = pl.BlockSpec((tr, *part[1:]), lambda j, me_ref: (j, *tail))
            o_spec = slot_spec
        out.append(pl.pallas_call(
            body, name=f"{name}_{i}",
            grid_spec=pltpu.PrefetchScalarGridSpec(
                num_scalar_prefetch=1, grid=(part[0] // tr,), in_specs=[s_spec, ANY_SPEC], out_specs=o_spec),
            out_shape=jax.ShapeDtypeStruct(land.shape, land.dtype), input_output_aliases={2: 0},
            compiler_params=_params("parallel"),
        )(me, src, land))
    return out


def _adamw_math(w, g, m, v):
    m = ADAM_B1 * m + (1.0 - ADAM_B1) * g
    v = ADAM_B2 * v + (1.0 - ADAM_B2) * (g * g)
    m_hat = m / (1.0 - ADAM_B1 ** ADAM_STEP)
    v_hat = v / (1.0 - ADAM_B2 ** ADAM_STEP)
    delta = -ADAM_LR * (m_hat / (jnp.sqrt(v_hat) + ADAM_EPS) + ADAM_WD * w)
    return delta, m, v


def _adamw(name, parts, w, m, v, *, grid, part_specs, w_spec):
    n_layers = len(parts)
    n_parts = parts[0].shape[0]

    def body(*refs):
        p_refs = refs[:n_layers]
        w_ref, m_ref, v_ref, g_ref, d_ref, nm_ref, nv_ref = refs[n_layers:]

        def total(p_ref):
            t = p_ref[0].astype(F32)
            for k in range(1, n_parts):
                t = t + p_ref[k].astype(F32)
            return t

        g = total(p_refs[0])
        for li in range(1, n_layers):
            g = jnp.where(pl.program_id(0) == li, total(p_refs[li]), g)
        delta, nm, nv = _adamw_math(w_ref[...], g, m_ref[...], v_ref[...])
        g_ref[...] = g
        d_ref[...] = delta
        nm_ref[...] = nm
        nv_ref[...] = nv

    out = jax.ShapeDtypeStruct(w.shape, F32)
    return pl.pallas_call(
        body, name=name, grid=grid, in_specs=[*part_specs, w_spec, w_spec, w_spec], out_specs=[w_spec] * 4,
        out_shape=[out] * 4, compiler_params=_params(*(("parallel",) * len(grid))),
    )(*parts, w, m, v)


def _layer_part_spec(layer, block, n_blocks, row_off=0):
    def index_map(l, i):
        ii = jnp.where(l == layer, i, jnp.where(l < layer, 0, n_blocks - 1))
        return (0, row_off + ii) + (0,) * (len(block) - 2)
    return pl.BlockSpec(block, index_map)


def _small_update(partials, triples):
    d = partials[-1].shape[-1]
    n_rep = len(triples)
    n_part = len(partials)
    rows = []
    for p in partials:
        rows.append(p.shape[0] * (p.shape[1] // d))
    offs = [sum(rows[:i]) for i in range(n_part)]
    total = -(-sum(rows) // 8) * 8

    def body(*refs):
        p_refs = refs[:n_part]
        wmv = refs[n_part:n_part + 3 * n_rep]
        outs = refs[n_part + 3 * n_rep:n_part + 3 * n_rep + 4 * n_rep + (n_part - n_rep)]
        buf, send_sems, recv_sems = refs[-3:]
        x, y, c = _my_place()
        me = _block_of(x, y, c)
        peers = _peers(x, y, c)
        mine = buf.at[me]
        if total > sum(rows):
            mine[sum(rows):total, :] = jnp.zeros((total - sum(rows), d), F32)
        for p_ref, off in zip(p_refs, offs):
            nr, nc = p_ref.shape[0], p_ref.shape[1] // d
            if nc == 1:
                mine[off:off + nr, :] = p_ref[...]
            else:
                for r in range(nr):
                    for q in range(nc):
                        mine[off + r * nc + q:off + r * nc + q + 1, :] = p_ref[r:r + 1, q * d:(q + 1) * d]
        sends =[pltpu.make_async_remote_copy(
            src_ref=buf.at[me], dst_ref=buf.at[me], send_sem=send_sems.at[r], recv_sem=recv_sems.at[r],
            device_id=peer, device_id_type=MESH) for r, peer in enumerate(peers)]
        for cp in sends:
            cp.start()
        for r, peer in enumerate(peers):
            pltpu.make_async_remote_copy(
                src_ref=buf.at[me], dst_ref=buf.at[_block_of(*peer)], send_sem=send_sems.at[r],
                recv_sem=recv_sems.at[r], device_id=peer, device_id_type=MESH).wait_recv()
        for cp in sends:
            cp.wait_send()
        tot = buf[0]
        for k in range(1, N_DEV):
            tot = tot + buf[k]
        buf[0] = tot
        for idx in range(n_part):
            nr, nc = p_refs[idx].shape[0], p_refs[idx].shape[1] // d
            if idx < n_rep:
                w_ref, m_ref, v_ref = wmv[3 * idx:3 * idx + 3]
                g_ref, d_ref, nm_ref, nv_ref = outs[4 * idx:4 * idx + 4]
            else:
                g_ref = outs[4 * n_rep + idx - n_rep]
            pieces = [(slice(0, nr), slice(0, d), offs[idx], nr)] if nc == 1 else [
                (slice(r, r + 1), slice(q * d, (q + 1) * d), offs[idx] + r * nc + q, 1)
                for r in range(nr) for q in range(nc)]
            for rws, cols, row, cnt in pieces:
                g = buf[0, row:row + cnt, :]
                g_ref[rws, cols] = g
                if idx < n_rep:
                    delta, nm, nv = _adamw_math(w_ref[rws, cols], g, m_ref[rws, cols], v_ref[rws, cols])
                    d_ref[rws, cols] = delta
                    nm_ref[rws, cols] = nm
                    nv_ref[rws, cols] = nv

    vm = pl.BlockSpec(memory_space=pltpu.VMEM)
    operands = list(partials)
    for t in triples:
        operands += list(t)
    out_shape = []
    for idx in range(n_rep):
        out_shape += [jax.ShapeDtypeStruct(partials[idx].shape, F32)] * 4
    for idx in range(n_rep, n_part):
        out_shape.append(jax.ShapeDtypeStruct(partials[idx].shape, F32))
    return pl.pallas_call(
        body, name="small_allreduce_adamw", in_specs=[vm] * len(operands), out_specs=[vm] * len(out_shape),
        out_shape=out_shape,
        scratch_shapes=[pltpu.VMEM((N_DEV, total, d), F32), pltpu.SemaphoreType.DMA((7,)), pltpu.SemaphoreType.DMA((7,))],
        compiler_params=pltpu.CompilerParams(vmem_limit_bytes=VMEM_LIMIT_BYTES),
    )(*operands)


def kernel(x, g_mix, w_in, b_in, conv_a, w_out_a, conv_b, conv_b_bias, ln_b_g, ln_b_b, w_out_b, b_out_b, w_pool, pool_scale, w_o, g_mlp, w_mlp1, w_mlp2, g_final, loss_target, m_g_mix, m_w_in, m_b_in, m_conv_a, m_w_out_a, m_conv_b, m_conv_b_bias, m_ln_b_g, m_ln_b_b, m_w_out_b, m_b_out_b, m_w_pool, m_pool_scale, m_w_o, m_g_mlp, m_w_mlp1, m_w_mlp2, m_g_final, v_g_mix, v_w_in, v_b_in, v_conv_a, v_w_out_a, v_conv_b, v_conv_b_bias, v_ln_b_g, v_ln_b_b, v_w_out_b, v_b_out_b, v_w_pool, v_pool_scale, v_w_o, v_g_mlp, v_w_mlp1, v_w_mlp2, v_g_final):
    _, s, d = x.shape
    n_layers = g_mix.shape[0]
    p_in = b_in.shape[1]
    ci = w_in.shape[2]
    c1 = w_mlp1.shape[2]
    rf = w_mlp2.shape[1]
    rd = w_out_a.shape[1]
    f = rf * N_DEV
    rp = rf + 3 * rd
    o_a, o_b, o_o = rf // rd, rf // rd + 1, rf // rd + 2
    gc = d // N_GROUPS
    ca_rows = 8
    tm = min(1024, s)
    tr = min(512, s)
    tx = min(256, s)
    tk = min(2048, s)
    tk_mlp = min(4096, s)
    tk_in = min(2048, s)

    me_arr = jnp.reshape(_block_of(*_my_place()), (1,)).astype(jnp.int32)

    def layer_shards(l):
        row_pack = jnp.concatenate([w_mlp2[l], w_out_a[l], w_out_b[l], w_o[l]], axis=0).astype(BF16)
        return [w_in[l].astype(BF16), w_mlp1[l].astype(BF16), row_pack, w_pool[l].astype(BF16)]

    conv_pack = jnp.concatenate(
        [conv_a, jnp.zeros((n_layers, ca_rows - K_A, rd), F32), conv_b], axis=1)
    first_shards = layer_shards(0)
    layer_pairing = [True, False, False, False]
    g_in_first, g_conv = _gather_shards([first_shards[0], conv_pack], [True, False])
    conv_full = jnp.transpose(g_conv, (1, 2, 0, 3)).reshape(n_layers, ca_rows + K_B, d)
    conv_a_f = conv_full[:, :K_A]
    conv_b_f = conv_full[:, ca_rows:]
    first_row_going = _start_copies("gather_start_row_0", first_shards[2:], g_conv, blockwise=False)
    in_flight = [_start_copies("gather_start_mlp1_0", first_shards[1:2], first_row_going["token"], blockwise=False)]
    for l in range(1, n_layers):
        in_flight.append(_start_copies(f"gather_start_{l}", layer_shards(l), in_flight[-1]["token"], blockwise=False,
                                       paired=layer_pairing))
    token = in_flight[-1]["token"][0:1, 0:1]

    xs = [x[0]]
    saved = []
    weights = []
    row2 = lambda j, i: (i, 0)
    for l in range(n_layers):
        x0 = xs[-1]
        vec = lambda a: a[l:l + 1]
        if l > 0:
            srcs, lands = _wait_copies(f"gather_wait_{l}", in_flight[l], x0, blockwise=False)
            g_in, g_1, g_row, g_pool = _place_own(f"gather_own_{l}", lands, srcs, me_arr, blockwise=False,
                                                  paired=layer_pairing)
        else:
            g_in = g_in_first
        h = _rms_fwd(f"rms_mix_{l}", x0, vec(g_mix) + token if l == 0 else vec(g_mix), tr)
        proj = _mm(
            f"proj_{l}", h, g_in, grid=(N_DEV // 2, s // tm), a_spec=pl.BlockSpec((tm, d), row2),
            b_spec=pl.BlockSpec((None, d, 2 * ci), lambda j, i: (j, 0, 0)),
            extras=(vec(b_in),), extra_specs=(pl.BlockSpec((1, 2 * ci), lambda j, i: (0, j)),),
            epilogue=lambda v, b: v + b, out_shape=jax.ShapeDtypeStruct((s, p_in), BF16),
            o_spec=pl.BlockSpec((tm, 2 * ci), lambda j, i: (i, j)), dims=NN)
        if l == 0:
            srcs, lands = _wait_copies("gather_wait_row_0", first_row_going, proj, blockwise=False)
            g_row, g_pool = _place_own("gather_own_row_0", lands, srcs, me_arr, blockwise=False)
        p_a, sw, p_c, cv, y_a, y_b, pw, merged, x1, h2 = _mixer_fwd(
            f"mix_fwd_{l}", proj, x0, conv_a_f[l], conv_b_f[l], vec(conv_b_bias), vec(ln_b_g), vec(ln_b_b),
            vec(b_out_b), vec(pool_scale), vec(g_mlp), g_row, g_pool, (o_a, o_b, o_o), d, tx)
        if l == 0:
            srcs, lands = _wait_copies("gather_wait_mlp1_0", in_flight[0], x1, blockwise=False)
            g_1, = _place_own("gather_own_mlp1_0", lands, srcs, me_arr, blockwise=False)
        weights.append((g_in, g_1, g_row, g_pool))
        a_pre = _mm(f"mlp1_{l}", h2, g_1, grid=(N_DEV // 2, s // tm), a_spec=pl.BlockSpec((tm, d), row2),
                    b_spec=pl.BlockSpec((2, d, c1), lambda j, i: (j, 0, 0)), slabs="n",
                    out_shape=jax.ShapeDtypeStruct((s, f), BF16),
                    o_spec=pl.BlockSpec((tm, 2 * c1), lambda j, i: (i, j)), dims=NN)
        x2 = _mm(f"mlp2_{l}", a_pre, g_row, grid=(1, s // tr), a_spec=pl.BlockSpec((tr, f), row2),
                 b_spec=pl.BlockSpec((N_DEV, rf, d), lambda j, i: (0, 0, 0)), prologue=_relu_sq,
                 extras=(x1,), extra_specs=(pl.BlockSpec((tr, d), row2),), epilogue=lambda v, r: v + r,
                 out_shape=jax.ShapeDtypeStruct((s, d), F32), o_spec=pl.BlockSpec((tr, d), row2), dims=NN)
        saved.append((x0, h, proj, p_a, sw, p_c, cv, y_a, y_b, pw, merged, x1, h2, a_pre))
        xs.append(x2)

    loss_part, dx, dx16, dg_final = _loss_head(xs[-1], g_final.reshape(1, d), loss_target[0], tr)
    loss = lax.psum(loss_part[0, 0], ("x", "y", "c"))

    small = [None] * n_layers
    exchanges = [None] * n_layers
    for l in reversed(range(n_layers)):
        x0, h, proj, p_a, sw, p_c, cv, y_a, y_b, pw, merged, x1, h2, a_pre = saved[l]
        g_in, g_1, g_row, g_pool = weights[l]
        vec = lambda a: a[l:l + 1]
        row_shape = jax.ShapeDtypeStruct((N_DEV, rp, d), BF16)

        def dd_grad(name, a, g, off, alias):
            return _mm(name, a, g, grid=(1, s // tk), a_spec=pl.BlockSpec((tk, d), lambda j, k: (k, 0)),
                       b_spec=pl.BlockSpec((tk, d), lambda j, k: (k, 0)), out_shape=row_shape,
                       o_spec=pl.BlockSpec((N_DEV, rd, d), lambda j, k: (0, off, 0)), dims=TN, nk=s // tk,
                       acc_shape=(d, d), alias_in=alias)

        d_a = _mm(f"d_act_{l}", dx16, g_row, grid=(N_DEV // 2, s // tm), a_spec=pl.BlockSpec((tm, d), row2),
                  b_spec=pl.BlockSpec((2, rf, d), lambda j, i: (j, 0, 0)), slabs="n",
                  extras=(a_pre,), extra_specs=(pl.BlockSpec((tm, 2 * rf), lambda j, i: (i, j)),),
                  epilogue=lambda v, a: v * (2.0 * jnp.maximum(a.astype(F32), 0.0)),
                  out_shape=jax.ShapeDtypeStruct((s, f), BF16),
                  o_spec=pl.BlockSpec((tm, 2 * rf), lambda j, i: (i, j)), dims=NT)
        dg_row = _mm(f"dw_mlp2_{l}", a_pre, dx16, grid=(N_DEV, s // tk_mlp),
                     a_spec=pl.BlockSpec((tk_mlp, rf), lambda j, k: (k, j)),
                     b_spec=pl.BlockSpec((tk_mlp, d), lambda j, k: (k, 0)), prologue=_relu_sq, out_shape=row_shape,
                     o_spec=pl.BlockSpec((None, rf, d), lambda j, k: (j, 0, 0)), dims=TN, nk=s // tk_mlp,
                     acc_shape=(rf, d))
        dg_1 = _mm(f"dw_mlp1_{l}", h2, d_a, grid=(N_DEV, s // tk_mlp),
                   a_spec=pl.BlockSpec((tk_mlp, d), lambda j, k: (k, 0)),
                   b_spec=pl.BlockSpec((tk_mlp, c1), lambda j, k: (k, j)),
                   out_shape=jax.ShapeDtypeStruct((N_DEV, d, c1), BF16),
                   o_spec=pl.BlockSpec((None, d, c1), lambda j, k: (j, 0, 0)), dims=TN, nk=s // tk_mlp,
                   acc_shape=(d, c1))
        mlp1_going = _start_copies(f"grads_start_mlp1_{l}", [dg_1], vec(g_mlp), blockwise=True)
        stream = [jax.ShapeDtypeStruct((s, d), F32), jax.ShapeDtypeStruct((s, d), BF16), jax.ShapeDtypeStruct((1, d), F32)]
        dx, dx16, dg_mlp = _mm(
            f"d_h2_{l}", d_a, g_1, grid=(1, s // tr), a_spec=pl.BlockSpec((tr, f), row2),
            b_spec=pl.BlockSpec((N_DEV, d, c1), lambda j, i: (0, 0, 0), pipeline_mode=pl.Buffered(1)), slabs="k",
            extras=(x1, vec(g_mlp), dx),
            extra_specs=(pl.BlockSpec((tr, d), row2), pl.BlockSpec((1, d), lambda j, i: (0, 0)),
                         pl.BlockSpec((tr, d), row2)),
            out_shape=stream, o_spec=[pl.BlockSpec((tr, d), row2), pl.BlockSpec((tr, d), row2),
                                      pl.BlockSpec((1, d), lambda j, i: (0, 0))],
            dims=NT, rms_bwd=(1, s // tr), after=mlp1_going["token"])
        d_ya, d_yb, d_pw, d_gates, d_pa, d_sw, d_pc, d_bout, d_pscale = _merge_bwd(
            f"merge_bwd_{l}", dx16, proj, y_a, y_b, pw, vec(pool_scale), g_row, g_pool, (o_a, o_b, o_o), d, tr)
        dg_row = dd_grad(f"dw_o_{l}", merged, dx16, o_o, dg_row)
        dg_row = dd_grad(f"dw_out_a_{l}", p_a, d_ya, o_a, dg_row)
        dg_row = dd_grad(f"dw_out_b_{l}", sw, d_yb, o_b, dg_row)
        dg_pool = _pool_wgrad(f"dw_pool_{l}", p_c, d_pw, d, tk)
        rest_going = _start_copies(f"grads_start_rest_{l}", [dg_row, dg_pool], vec(g_mlp), blockwise=True)
        d_proj, d_bin, d_ca, d_cb, d_cbb, d_lng, d_lnb = _mix_pre_bwd(
            f"mix_bwd_{l}", proj, cv, d_pa, d_sw, d_pc, d_gates, conv_a_f[l], conv_b_f[l],
            vec(ln_b_g) + rest_going["token"][0:1, 0:1], vec(ln_b_b), d, tx)
        dg_in = _mm(f"dw_in_{l}", h, d_proj, grid=(N_DEV // 2, s // tk_in),
                    a_spec=pl.BlockSpec((tk_in, d), lambda j, k: (k, 0)),
                    b_spec=pl.BlockSpec((tk_in, 2 * ci), lambda j, k: (k, j)),
                    out_shape=jax.ShapeDtypeStruct((N_DEV // 2, d, 2 * ci), BF16),
                    o_spec=pl.BlockSpec((None, d, 2 * ci), lambda j, k: (j, 0, 0)), dims=TN, nk=s // tk_in,
                    acc_shape=(d, 2 * ci), after=rest_going["token"])
        in_going = _start_copies(f"grads_start_in_{l}", [dg_in], vec(g_mix), blockwise=True, paired=[True])
        rows_ik = lambda i, k: (i, 0)
        once = dict(pipeline_mode=pl.Buffered(1))
        dx, dx16, dg_mix = _mm(
            f"d_h_{l}", d_proj, g_in, grid=(s // tm, N_DEV // 2), a_spec=pl.BlockSpec((tm, 2 * ci), lambda i, k: (i, k)),
            b_spec=pl.BlockSpec((None, d, 2 * ci), lambda i, k: (k, 0, 0)),
            extras=(x0, vec(g_mix), dx),
            extra_specs=(pl.BlockSpec((tm, d), rows_ik, **once), pl.BlockSpec((1, d), lambda i, k: (0, 0)),
                         pl.BlockSpec((tm, d), rows_ik, **once)),
            out_shape=stream, o_spec=[pl.BlockSpec((tm, d), rows_ik), pl.BlockSpec((tm, d), rows_ik),
                                      pl.BlockSpec((1, d), lambda i, k: (0, 0))],
            dims=NT, nk=N_DEV // 2, acc_shape=(tm, d), rms_bwd=(0, s // tm), after=in_going["token"])
        small[l] = (dg_mix, d_bin, d_cbb, d_lng, d_lnb, d_bout, d_pscale, dg_mlp, d_ca, d_cb)
        exchanges[l] = (in_going, mlp1_going, rest_going)

    grad_x = dx[None]

    names = ("g_mix", "b_in", "conv_b_bias", "ln_b_g", "ln_b_b", "b_out_b", "pool_scale", "g_mlp")
    given = dict(g_mix=(g_mix, m_g_mix, v_g_mix), b_in=(b_in, m_b_in, v_b_in),
                 conv_b_bias=(conv_b_bias, m_conv_b_bias, v_conv_b_bias), ln_b_g=(ln_b_g, m_ln_b_g, v_ln_b_g),
                 ln_b_b=(ln_b_b, m_ln_b_b, v_ln_b_b), b_out_b=(b_out_b, m_b_out_b, v_b_out_b),
                 pool_scale=(pool_scale, m_pool_scale, v_pool_scale), g_mlp=(g_mlp, m_g_mlp, v_g_mlp))
    partials, triples = [], []
    for i, nm in enumerate(names):
        partials.append(jnp.concatenate([small[l][i] for l in range(n_layers)], axis=0))
        triples.append(given[nm])
    partials.append(dg_final)
    triples.append(tuple(a.reshape(1, d) for a in (g_final, m_g_final, v_g_final)))
    partials.append(jnp.concatenate([small[l][8] for l in range(n_layers)], axis=0))
    partials.append(jnp.concatenate([small[l][9] for l in range(n_layers)], axis=0))
    outs = _small_update(partials, triples)
    rep = {nm: outs[4 * i:4 * i + 4] for i, nm in enumerate(names)}
    rep["g_final"] = [a.reshape(d) for a in outs[4 * len(names):4 * len(names) + 4]]
    me = _block_of(*_my_place())
    gca = lax.dynamic_slice_in_dim(outs[-2].reshape(n_layers, K_A, d), me * rd, rd, axis=2)
    gcb = lax.dynamic_slice_in_dim(outs[-1].reshape(n_layers, K_B, d), me * rd, rd, axis=2)

    r_in, r_1, r_row, r_pool = [], [], [], []
    for l in reversed(range(n_layers)):
        in_going, mlp1_going, rest_going = exchanges[l]
        srcs_m, lands_m = _wait_copies(f"grads_wait_mlp1_{l}", mlp1_going, outs[0], blockwise=True)
        srcs_r, lands_r = _wait_copies(f"grads_wait_rest_{l}", rest_going, outs[0], blockwise=True)
        srcs_i, lands_i = _wait_copies(f"grads_wait_in_{l}", in_going, outs[0], blockwise=True)
        got = _place_own(f"grads_own_{l}", lands_i + lands_m + lands_r, srcs_i + srcs_m + srcs_r, me_arr, blockwise=True,
                         paired=layer_pairing)
        for lst, arr in zip((r_in, r_1, r_row, r_pool), got):
            lst.insert(0, arr)
    tb = min(256, d)
    layers = range(n_layers)
    res = {}
    res["w_in"] = _adamw("adamw_w_in", r_in, w_in, m_w_in, v_w_in, grid=(n_layers, d // tb),
                         part_specs=[_layer_part_spec(li, (N_DEV, tb, ci), d // tb) for li in layers],
                         w_spec=pl.BlockSpec((None, tb, ci), lambda l, i: (l, i, 0)))
    res["w_mlp1"] = _adamw("adamw_w_mlp1", r_1, w_mlp1, m_w_mlp1, v_w_mlp1, grid=(n_layers, d // tb),
                           part_specs=[_layer_part_spec(li, (N_DEV, tb, c1), d // tb) for li in layers],
                           w_spec=pl.BlockSpec((None, tb, c1), lambda l, i: (l, i, 0)))
    tf = min(256, rf)
    res["w_mlp2"] = _adamw("adamw_w_mlp2", r_row, w_mlp2, m_w_mlp2, v_w_mlp2, grid=(n_layers, rf // tf),
                           part_specs=[_layer_part_spec(li, (N_DEV, tf, d), rf // tf) for li in layers],
                           w_spec=pl.BlockSpec((None, tf, d), lambda l, i: (l, i, 0)))
    for nm, off, trip in (("w_out_a", o_a, (w_out_a, m_w_out_a, v_w_out_a)),
                          ("w_out_b", o_b, (w_out_b, m_w_out_b, v_w_out_b)), ("w_o", o_o, (w_o, m_w_o, v_w_o))):
        res[nm] = _adamw(f"adamw_{nm}", r_row, *trip, grid=(n_layers, 1),
                         part_specs=[_layer_part_spec(li, (N_DEV, rd, d), 1, row_off=off) for li in layers],
                         w_spec=pl.BlockSpec((None, rd, d), lambda l, i: (l, 0, 0)))
    res["w_pool"] = _adamw("adamw_w_pool", r_pool, w_pool, m_w_pool, v_w_pool, grid=(n_layers, 1),
                           part_specs=[_layer_part_spec(li, (N_DEV, N_GROUPS, gc // N_DEV, gc), 1) for li in layers],
                           w_spec=pl.BlockSpec((None, N_GROUPS, gc // N_DEV, gc), lambda l, i: (l, 0, 0, 0)))
    whole3 = lambda: (0, 0, 0)
    res["conv_a"] = _adamw("adamw_conv_a", [gca[None]], conv_a, m_conv_a, v_conv_a, grid=(),
                           part_specs=[pl.BlockSpec((1, n_layers, K_A, rd), lambda: (0, 0, 0, 0))],
                           w_spec=pl.BlockSpec((n_layers, K_A, rd), whole3))
    res["conv_b"] = _adamw("adamw_conv_b", [gcb[None]], conv_b, m_conv_b, v_conv_b, grid=(),
                           part_specs=[pl.BlockSpec((1, n_layers, K_B, rd), lambda: (0, 0, 0, 0))],
                           w_spec=pl.BlockSpec((n_layers, K_B, rd), whole3))
    res.update(rep)

    order = ("g_mix", "w_in", "b_in", "conv_a", "w_out_a", "conv_b", "conv_b_bias", "ln_b_g", "ln_b_b", "w_out_b",
             "b_out_b", "w_pool", "pool_scale", "w_o", "g_mlp", "w_mlp1", "w_mlp2", "g_final")
    out = [loss, grad_x]
    for kind in range(4):
        out += [res[nm][kind] for nm in order]
    return tuple(out)
```

```python
import jax
import jax.numpy as jnp
from jax import lax
from jax.experimental import pallas as pl
from jax.experimental.pallas import tpu as pltpu

F32 = jnp.float32
BF16 = jnp.bfloat16
MESH = pl.DeviceIdType.MESH

N_DEV = 8
EPS = 1e-6
K_A = 3
K_B = 31
POOL_WINDOWS = (2, 4, 8, 16)
N_GROUPS = len(POOL_WINDOWS)
HALO = 32
CHUNK = 16
FWD_CHUNK = 32
SUBLANES = 8
TAP_GROUP = 16
ADAM_LR, ADAM_B1, ADAM_B2, ADAM_EPS, ADAM_WD, ADAM_STEP = 0.001, 0.9, 0.999, 1e-08, 0.01, 10
VMEM_LIMIT_BYTES = 60 * 1024 * 1024

NN = (((1,), (0,)), ((), ()))
NT = (((1,), (1,)), ((), ()))
TN = (((0,), (0,)), ((), ()))


def _params(*sem):
    return pltpu.CompilerParams(dimension_semantics=sem, vmem_limit_bytes=VMEM_LIMIT_BYTES)


def _sigmoid(v):
    return 1.0 / (1.0 + jnp.exp(-v))


def _mm(name, a, b, *, grid, a_spec, b_spec, out_shape, o_spec, dims, nk=1, acc_shape=None,
        extras=(), extra_specs=(), prologue=None, epilogue=None, alias_in=None, slabs=None, after=None,
        rms_bwd=None):
    n_extra = len(extras)
    has_alias = alias_in is not None
    n_unread = (1 if has_alias else 0) + (1 if after is not None else 0)

    def body(*refs):
        a_ref, b_ref = refs[0], refs[1]
        ex = refs[2:2 + n_extra]
        o_ref = refs[2 + n_extra + n_unread]
        av = a_ref[...]
        if prologue is not None:
            av = prologue(av)
        av = av.astype(BF16)

        def finish_rms(val):
            x_ref, g_ref, dr_ref = ex
            dx_ref, dx16_ref, dg_ref = refs[2 + n_extra + n_unread:5 + n_extra + n_unread]
            acc_g = refs[-1]
            row_axis, n_rows = rms_bwd
            ri = pl.program_id(row_axis)
            xv = x_ref[...]
            r = lax.rsqrt(jnp.mean(xv * xv, axis=-1, keepdims=True) + EPS)
            xh = xv * r
            part = _colsum8(val * xh)

            @pl.when(ri == 0)
            def _():
                acc_g[...] = part

            @pl.when(ri > 0)
            def _():
                acc_g[...] += part

            dxh = val * g_ref[...]
            dx = r * (dxh - xh * jnp.mean(dxh * xh, axis=-1, keepdims=True)) + dr_ref[...]
            dx_ref[...] = dx
            dx16_ref[...] = dx.astype(BF16)

            @pl.when(ri == n_rows - 1)
            def _():
                dg_ref[...] = jnp.sum(acc_g[...], axis=0, keepdims=True)

        def finish(val, cols=None):
            if rms_bwd is not None:
                return finish_rms(val)
            if epilogue is not None:
                val = epilogue(val, *[e[...] if cols is None else e[:, cols] for e in ex])
            if cols is None:
                o_ref[...] = val.astype(o_ref.dtype).reshape(o_ref.shape)
            else:
                o_ref[:, cols] = val.astype(o_ref.dtype)

        if slabs == "n":
            for q in range(b_ref.shape[0]):
                pq = lax.dot_general(av, b_ref[q].astype(BF16), dims, preferred_element_type=F32)
                finish(pq, slice(q * pq.shape[1], (q + 1) * pq.shape[1]))
            return
        if slabs == "k":
            kc = av.shape[1] // b_ref.shape[0]
            p = None
            for q in range(b_ref.shape[0]):
                pq = lax.dot_general(av[:, q * kc:(q + 1) * kc], b_ref[q].astype(BF16), dims,
                                     preferred_element_type=F32)
                p = pq if p is None else p + pq
        else:
            bv = b_ref[...]
            bv = bv.reshape((-1, bv.shape[-1])).astype(BF16)
            p = lax.dot_general(av, bv, dims, preferred_element_type=F32)

        if nk == 1:
            finish(p)
        else:
            acc = refs[-2] if rms_bwd is not None else refs[-1]
            k = pl.program_id(len(grid) - 1)

            @pl.when(k == 0)
            def _():
                acc[...] = p

            @pl.when(k > 0)
            def _():
                acc[...] += p

            @pl.when(k == nk - 1)
            def _():
                finish(acc[...])

    in_specs = [a_spec, b_spec, *extra_specs]
    operands = [a, b, *extras]
    aliases = {}
    if has_alias:
        in_specs.append(pl.BlockSpec(memory_space=pl.ANY))
        operands.append(alias_in)
        aliases = {len(operands) - 1: 0}
    if after is not None:
        in_specs.append(pl.BlockSpec(memory_space=pl.ANY))
        operands.append(after)
    sem = ("parallel",) * (len(grid) - 1) + (("arbitrary",) if nk > 1 else ("parallel",))
    scratch = [pltpu.VMEM(acc_shape, F32)] if nk > 1 else []
    if rms_bwd is not None:
        sem = ("arbitrary",) * len(grid)
        scratch.append(pltpu.VMEM((8, extras[0].shape[-1]), F32))
    return pl.pallas_call(
        body, name=name, grid=grid, in_specs=in_specs, out_specs=o_spec, out_shape=out_shape,
        scratch_shapes=scratch, input_output_aliases=aliases, compiler_params=_params(*sem),
    )(*operands)


def _relu_sq(v):
    r = jnp.maximum(v, 0)
    return r * r


def _rms_fwd(name, x, g, tm):
    s, d = x.shape

    def body(x_ref, g_ref, h_ref):
        xv = x_ref[...]
        r = lax.rsqrt(jnp.mean(xv * xv, axis=-1, keepdims=True) + EPS)
        h_ref[...] = (xv * r * g_ref[...]).astype(h_ref.dtype)

    return pl.pallas_call(
        body, name=name, grid=(s // tm,),
        in_specs=[pl.BlockSpec((tm, d), lambda i: (i, 0)), pl.BlockSpec((1, d), lambda i: (0, 0))],
        out_specs=pl.BlockSpec((tm, d), lambda i: (i, 0)),
        out_shape=jax.ShapeDtypeStruct((s, d), BF16), compiler_params=_params("parallel"),
    )(x, g)


def _colsum8(v):
    return jnp.sum(v.reshape(v.shape[0] // 8, 8, v.shape[1]), axis=0)


def _rms_bwd(name, dh, x, g, dres, tm):
    s, d = x.shape
    n = s // tm

    def body(dh_ref, x_ref, g_ref, dr_ref, dx_ref, dx16_ref, dg_ref, acc):
        i = pl.program_id(0)
        xv = x_ref[...]
        r = lax.rsqrt(jnp.mean(xv * xv, axis=-1, keepdims=True) + EPS)
        xh = xv * r
        dhv = dh_ref[...].astype(F32)
        part = _colsum8(dhv * xh)

        @pl.when(i == 0)
        def _():
            acc[...] = part

        @pl.when(i > 0)
        def _():
            acc[...] += part

        dxh = dhv * g_ref[...]
        dx = r * (dxh - xh * jnp.mean(dxh * xh, axis=-1, keepdims=True)) + dr_ref[...]
        dx_ref[...] = dx
        dx16_ref[...] = dx.astype(BF16)

        @pl.when(i == n - 1)
        def _():
            dg_ref[...] = jnp.sum(acc[...], axis=0, keepdims=True)

    row = lambda i: (i, 0)
    return pl.pallas_call(
        body, name=name, grid=(n,),
        in_specs=[pl.BlockSpec((tm, d), row), pl.BlockSpec((tm, d), row), pl.BlockSpec((1, d), lambda i: (0, 0)),
                  pl.BlockSpec((tm, d), row)],
        out_specs=[pl.BlockSpec((tm, d), row), pl.BlockSpec((tm, d), row), pl.BlockSpec((1, d), lambda i: (0, 0))],
        out_shape=[jax.ShapeDtypeStruct((s, d), F32), jax.ShapeDtypeStruct((s, d), BF16),
                   jax.ShapeDtypeStruct((1, d), F32)],
        scratch_shapes=[pltpu.VMEM((8, d), F32)], compiler_params=_params("arbitrary"),
    )(dh, x, g, dres)


def _loss_head(x, g, target, tm):
    s, d = x.shape
    n = s // tm

    def body(x_ref, g_ref, t_ref, loss_ref, dx_ref, dx16_ref, dg_ref, acc_l, acc_g):
        i = pl.program_id(0)
        xv = x_ref[...]
        r = lax.rsqrt(jnp.mean(xv * xv, axis=-1, keepdims=True) + EPS)
        xh = xv * r
        err = xh * g_ref[...] - t_ref[...]
        dy = err * (1.0 / d)
        lpart = _colsum8(err * err)
        gpart = _colsum8(dy * xh)

        @pl.when(i == 0)
        def _():
            acc_l[...] = lpart
            acc_g[...] = gpart

        @pl.when(i > 0)
        def _():
            acc_l[...] += lpart
            acc_g[...] += gpart

        dxh = dy * g_ref[...]
        dx = r * (dxh - xh * jnp.mean(dxh * xh, axis=-1, keepdims=True))
        dx_ref[...] = dx
        dx16_ref[...] = dx.astype(BF16)

        @pl.when(i == n - 1)
        def _():
            loss_ref[...] = (0.5 / d) * jnp.sum(jnp.sum(acc_l[...], axis=0, keepdims=True), axis=1, keepdims=True)
            dg_ref[...] = jnp.sum(acc_g[...], axis=0, keepdims=True)

    return pl.pallas_call(
        body, name="loss_head", grid=(n,),
        in_specs=[pl.BlockSpec((tm, d), lambda i: (i, 0)), pl.BlockSpec((1, d), lambda i: (0, 0)),
                  pl.BlockSpec((tm, d), lambda i: (i, 0))],
        out_specs=[pl.BlockSpec((1, 1), lambda i: (0, 0)), pl.BlockSpec((tm, d), lambda i: (i, 0)),
                   pl.BlockSpec((tm, d), lambda i: (i, 0)), pl.BlockSpec((1, d), lambda i: (0, 0))],
        out_shape=[jax.ShapeDtypeStruct((1, 1), F32), jax.ShapeDtypeStruct((s, d), F32),
                   jax.ShapeDtypeStruct((s, d), BF16), jax.ShapeDtypeStruct((1, d), F32)],
        scratch_shapes=[pltpu.VMEM((8, d), F32), pltpu.VMEM((8, d), F32)], compiler_params=_params("arbitrary"),
    )(x, g, target)


def _sec(ref, n, d):
    return ref[:, n * d:(n + 1) * d].astype(F32)


def _fill_shifts(sh, ext):
    rows = ext.shape[0] - SUBLANES
    for b in range(1, SUBLANES):
        sh[b - 1, 0:rows, :] = ext[b:b + rows, :]


def _shifted(sh, ext, off, n):
    b = off % SUBLANES
    if b == 0:
        return ext[off:off + n, :]
    return sh[b - 1, off - b:off - b + n, :]


def _spread_taps(dst, w_ref):
    for k in range(w_ref.shape[0]):
        dst[k] = jnp.broadcast_to(w_ref[k:k + 1, :], dst.shape[1:])


def _times_tap(x, tap):
    return (x.reshape(x.shape[0] // SUBLANES, SUBLANES, x.shape[1]) * tap[None]).reshape(x.shape)


def _window_sums(src, s_a, s_b, gc, first, rows, back):
    n = src.shape[0]
    sign = -1 if back else 1
    lo = [SUBLANES * j if back else 0 for j in range(4)]
    hi = [n if back else n - SUBLANES * j for j in range(4)]
    sl = lambda j, shift: slice(lo[j] + shift, hi[j] + shift)
    s_a[sl(1, 0), :] = src[sl(1, 0), :] + src[sl(1, sign * 1), :]
    out = [s_a[first:first + rows, 0:gc]]
    s_b[sl(2, 0), gc:] = s_a[sl(2, 0), gc:] + s_a[sl(2, sign * 2), gc:]
    out.append(s_b[first:first + rows, gc:2 * gc])
    s_a[sl(3, 0), 2 * gc:] = s_b[sl(3, 0), 2 * gc:] + s_b[sl(3, sign * 4), 2 * gc:]
    out.append(s_a[first:first + rows, 2 * gc:3 * gc])
    out.append(s_a[first:first + rows, 3 * gc:] + s_a[first + sign * SUBLANES:first + sign * SUBLANES + rows, 3 * gc:])
    return out


def _pool_count(row0, rows, window):
    t = row0 + lax.broadcasted_iota(jnp.int32, (rows, 1), 0)
    return jnp.minimum(t + 1, window).astype(F32)


def _group_weight(w_ref, gi, gc):
    return w_ref[:, gi].reshape(gc, gc)


def _mixer_fwd(name, proj, x0, conv_a, conv_b, conv_b_bias, ln_g, ln_b, b_out_b, pool_scale, g_next, g_row, g_pool, offs,
               d, tm):
    s = proj.shape[0]
    n = s // tm
    gc = d // N_GROUPS
    hb = tm // HALO
    rd = d // N_DEV
    o_a, o_b, o_o = offs

    def body(pj_ref, hp_ref, x0_ref, ca_ref, cb_ref, cbb_ref, lng_ref, lnb_ref, bo_ref, sc_ref, gn_ref, wa_ref, wb_ref,
             wo_ref, wp_ref, pa_ref, sw_ref, pc_ref, cv_ref, ya_ref, yb_ref, pw_ref, mg_ref, x1_ref, h2_ref,
             eua, eub, euc, sh, taps_a, taps_b, sum_a, sum_b):
        i = pl.program_id(0)
        keep = (i > 0).astype(F32)
        _spread_taps(taps_a, ca_ref)
        _spread_taps(taps_b, cb_ref)
        eua[0:HALO, :] = _sec(hp_ref, 1, d) * _sec(hp_ref, 2, d) * keep
        eub[0:HALO, :] = _sec(hp_ref, 3, d) * _sigmoid(_sec(hp_ref, 4, d)) * keep
        euc[0:HALO, :] = _sec(hp_ref, 5, d) * keep
        eua[HALO:HALO + tm, :] = _sec(pj_ref, 1, d) * _sec(pj_ref, 2, d)
        eub[HALO:HALO + tm, :] = _sec(pj_ref, 3, d) * _sigmoid(_sec(pj_ref, 4, d))
        euc[HALO:HALO + tm, :] = _sec(pj_ref, 5, d)
        _fill_shifts(sh, eub)
        for c in range(tm // FWD_CHUNK):
            r0 = c * FWD_CHUNK
            z = jnp.zeros((FWD_CHUNK, d), F32)
            for k in range(K_A):
                z = z + _times_tap(eua[HALO + r0 - (K_A - 1) + k:HALO + r0 - (K_A - 1) + k + FWD_CHUNK, :], taps_a[k])
            pa_ref[r0:r0 + FWD_CHUNK, :] = (pj_ref[r0:r0 + FWD_CHUNK, 0:d].astype(F32) * z).astype(pa_ref.dtype)
            cv = jnp.zeros((FWD_CHUNK, d), F32) + cbb_ref[...]
            for k in range(K_B):
                cv = cv + _times_tap(_shifted(sh, eub, HALO + r0 - (K_B - 1) + k, FWD_CHUNK), taps_b[k])
            cv_ref[r0:r0 + FWD_CHUNK, :] = cv.astype(cv_ref.dtype)
        cvv = cv_ref[...].astype(F32)
        mu = jnp.mean(cvv, axis=-1, keepdims=True)
        xc = cvv - mu
        xh = xc * lax.rsqrt(jnp.mean(xc * xc, axis=-1, keepdims=True) + EPS)
        ln = xh * lng_ref[...] + lnb_ref[...]
        sw_ref[...] = (ln * _sigmoid(ln)).astype(sw_ref.dtype)
        sums = _window_sums(euc, sum_a, sum_b, gc, HALO, tm, back=True)
        for gi, w in enumerate(POOL_WINDOWS):
            cols = slice(gi * gc, (gi + 1) * gc)
            cnt = _pool_count(i * tm, tm, w)
            pc_ref[:, cols] = (sums[gi] / cnt - euc[HALO:HALO + tm, cols]).astype(pc_ref.dtype)
        ya_ref[...] = jnp.dot(pa_ref[...], wa_ref[...].reshape(d, d), preferred_element_type=F32).astype(ya_ref.dtype)
        yb_ref[...] = (jnp.dot(sw_ref[...], wb_ref[...].reshape(d, d), preferred_element_type=F32)
                       + bo_ref[...]).astype(yb_ref.dtype)
        for gi in range(N_GROUPS):
            cols = slice(gi * gc, (gi + 1) * gc)
            pw_ref[:, cols] = jnp.dot(pc_ref[:, cols], _group_weight(wp_ref, gi, gc),
                                      preferred_element_type=F32).astype(pw_ref.dtype)
        m = _sigmoid(_sec(pj_ref, 6, d)) * ya_ref[...].astype(F32)
        m = m + _sigmoid(_sec(pj_ref, 7, d)) * yb_ref[...].astype(F32)
        m = m + _sigmoid(_sec(pj_ref, 8, d)) * (pw_ref[...].astype(F32) * sc_ref[...])
        mg_ref[...] = m.astype(mg_ref.dtype)
        x1 = x0_ref[...] + jnp.dot(mg_ref[...], wo_ref[...].reshape(d, d), preferred_element_type=F32)
        x1_ref[...] = x1
        h2_ref[...] = (x1 * lax.rsqrt(jnp.mean(x1 * x1, axis=-1, keepdims=True) + EPS) * gn_ref[...]).astype(h2_ref.dtype)

    row = lambda i: (i, 0)
    fixed = lambda i: (0, 0)
    act = jax.ShapeDtypeStruct((s, d), BF16)

    def dd_weight(off):
        return pl.BlockSpec((N_DEV, rd, d), lambda i: (0, off, 0), pipeline_mode=pl.Buffered(1))

    return pl.pallas_call(
        body, name=name, grid=(n,),
        in_specs=[pl.BlockSpec((tm, 9 * d), row),
                  pl.BlockSpec((HALO, 6 * d), lambda i: (jnp.maximum(i * hb - 1, 0), 0)),
                  pl.BlockSpec((tm, d), row),
                  pl.BlockSpec((K_A, d), fixed), pl.BlockSpec((K_B, d), fixed), pl.BlockSpec((1, d), fixed),
                  pl.BlockSpec((1, d), fixed), pl.BlockSpec((1, d), fixed), pl.BlockSpec((1, d), fixed),
                  pl.BlockSpec((1, d), fixed), pl.BlockSpec((1, d), fixed), dd_weight(o_a), dd_weight(o_b), dd_weight(o_o),
                  pl.BlockSpec((N_DEV, N_GROUPS, gc // N_DEV, gc), lambda i: (0, 0, 0, 0),
                               pipeline_mode=pl.Buffered(1))],
        out_specs=[pl.BlockSpec((tm, d), row)] * 10,
        out_shape=[act] * 8 + [jax.ShapeDtypeStruct((s, d), F32), act],
        scratch_shapes=[pltpu.VMEM((tm + HALO, d), F32)] * 3 + [pltpu.VMEM((SUBLANES - 1, tm + HALO, d), F32),
                                                                pltpu.VMEM((K_A, SUBLANES, d), F32),
                                                                pltpu.VMEM((K_B, SUBLANES, d), F32)]
                       + [pltpu.VMEM((tm + HALO, d), F32)] * 2,
        compiler_params=_params("parallel"),
    )(proj, proj, x0, conv_a, conv_b, conv_b_bias, ln_g, ln_b, b_out_b, pool_scale, g_next, g_row, g_row, g_row,
      g_pool)


def _merge_bwd(name, dx16, proj, ya, yb, pw, pool_scale, g_row, g_pool, offs, d, tm):
    s = proj.shape[0]
    n = s // tm
    gc = d // N_GROUPS
    rd = d // N_DEV
    o_a, o_b, o_o = offs

    def body(dx_ref, g_ref, ya_ref, yb_ref, pw_ref, sc_ref, wa_ref, wb_ref, wo_ref, wp_ref,
             dya_ref, dyb_ref, dpw_ref, dg_ref, dpa_ref, dsw_ref, dpc_ref, dbo_ref, dsc_ref, acc_b, acc_s):
        i = pl.program_id(0)
        dmv = lax.dot_general(dx_ref[...], wo_ref[...].reshape(d, d), NT,
                              preferred_element_type=F32).astype(BF16).astype(F32)
        scale = sc_ref[...]
        g0 = _sigmoid(_sec(g_ref, 0, d))
        dya_ref[...] = (dmv * g0).astype(dya_ref.dtype)
        dg_ref[:, 0:d] = (dmv * ya_ref[...].astype(F32) * g0 * (1.0 - g0)).astype(dg_ref.dtype)
        g1 = _sigmoid(_sec(g_ref, 1, d))
        dyb = dmv * g1
        dyb_ref[...] = dyb.astype(dyb_ref.dtype)
        dg_ref[:, d:2 * d] = (dmv * yb_ref[...].astype(F32) * g1 * (1.0 - g1)).astype(dg_ref.dtype)
        g2 = _sigmoid(_sec(g_ref, 2, d))
        pwv = pw_ref[...].astype(F32)
        dyc = dmv * g2
        dpw_ref[...] = (dyc * scale).astype(dpw_ref.dtype)
        dg_ref[:, 2 * d:3 * d] = (dmv * (pwv * scale) * g2 * (1.0 - g2)).astype(dg_ref.dtype)
        pb = _colsum8(dyb)
        ps = _colsum8(dyc * pwv)

        @pl.when(i == 0)
        def _():
            acc_b[...] = pb
            acc_s[...] = ps

        @pl.when(i > 0)
        def _():
            acc_b[...] += pb
            acc_s[...] += ps

        dpa_ref[...] = lax.dot_general(dya_ref[...], wa_ref[...].reshape(d, d), NT,
                                       preferred_element_type=F32).astype(dpa_ref.dtype)
        dsw_ref[...] = lax.dot_general(dyb_ref[...], wb_ref[...].reshape(d, d), NT,
                                       preferred_element_type=F32).astype(dsw_ref.dtype)
        for gi in range(N_GROUPS):
            cols = slice(gi * gc, (gi + 1) * gc)
            dpc_ref[:, cols] = lax.dot_general(dpw_ref[:, cols], _group_weight(wp_ref, gi, gc), NT,
                                               preferred_element_type=F32).astype(dpc_ref.dtype)

        @pl.when(i == n - 1)
        def _():
            dbo_ref[...] = jnp.sum(acc_b[...], axis=0, keepdims=True)
            dsc_ref[...] = jnp.sum(acc_s[...], axis=0, keepdims=True)

    row = lambda i: (i, 0)
    fixed = lambda i: (0, 0)
    act = jax.ShapeDtypeStruct((s, d), BF16)
    vec = jax.ShapeDtypeStruct((1, d), F32)

    def dd_weight(off):
        return pl.BlockSpec((N_DEV, rd, d), lambda i: (0, off, 0), pipeline_mode=pl.Buffered(1))

    return pl.pallas_call(
        body, name=name, grid=(n,),
        in_specs=[pl.BlockSpec((tm, d), row), pl.BlockSpec((tm, 3 * d), lambda i: (i, 2)), pl.BlockSpec((tm, d), row),
                  pl.BlockSpec((tm, d), row), pl.BlockSpec((tm, d), row), pl.BlockSpec((1, d), fixed),
                  dd_weight(o_a), dd_weight(o_b), dd_weight(o_o),
                  pl.BlockSpec((N_DEV, N_GROUPS, gc // N_DEV, gc), lambda i: (0, 0, 0, 0),
                               pipeline_mode=pl.Buffered(1))],
        out_specs=[pl.BlockSpec((tm, d), row)] * 3 + [pl.BlockSpec((tm, 3 * d), row)] + [pl.BlockSpec((tm, d), row)] * 3
                  + [pl.BlockSpec((1, d), fixed)] * 2,
        out_shape=[act, act, act, jax.ShapeDtypeStruct((s, 3 * d), BF16), act, act, act, vec, vec],
        scratch_shapes=[pltpu.VMEM((8, d), F32)] * 2, compiler_params=_params("arbitrary"),
    )(dx16, proj, ya, yb, pw, pool_scale, g_row, g_row, g_row, g_pool)


def _mix_pre_bwd(name, proj, cv, dpa, dsw, dpc, dgates, conv_a, conv_b, ln_g, ln_b, d, tm):
    s = proj.shape[0]
    n = s // tm
    gc = d // N_GROUPS
    hb = tm // HALO
    last_halo = s // HALO - 1
    te = tm + HALO

    def ln_bwd(cvv, dswv, lng, lnb):
        mu = jnp.mean(cvv, axis=-1, keepdims=True)
        xc = cvv - mu
        rstd = lax.rsqrt(jnp.mean(xc * xc, axis=-1, keepdims=True) + EPS)
        xh = xc * rstd
        ln = xh * lng + lnb
        sg = _sigmoid(ln)
        dln = dswv * (sg * (1.0 + ln * (1.0 - sg)))
        dxh = dln * lng
        dcv = rstd * (dxh - jnp.mean(dxh, axis=-1, keepdims=True) - xh * jnp.mean(dxh * xh, axis=-1, keepdims=True))
        return dcv, dln, xh

    def body(pj_ref, hp_ref, hf_ref, cv_ref, cvf_ref, dpa_ref, dpaf_ref, dsw_ref, dswf_ref, dpc_ref, dpcf_ref, dgt_ref,
             ca_ref, cb_ref, lng_ref, lnb_ref,
             dpj_ref, dbin_ref, dca_ref, dcb_ref, dcbb_ref, dlng_ref, dlnb_ref,
             eua, eub, edz, edcv, eq, sh, dub_s, taps_a, taps_b, sum_a, sum_b, acc_bin, acc_ca, acc_cb, acc_v):
        i = pl.program_id(0)
        keep_p = (i > 0).astype(F32)
        keep_f = (i < n - 1).astype(F32)
        _spread_taps(taps_a, ca_ref)
        _spread_taps(taps_b, cb_ref)

        @pl.when(i == 0)
        def _():
            acc_bin[...] = jnp.zeros_like(acc_bin)
            acc_ca[...] = jnp.zeros_like(acc_ca)
            acc_cb[...] = jnp.zeros_like(acc_cb)
            acc_v[...] = jnp.zeros_like(acc_v)

        eua[0:HALO, :] = _sec(hp_ref, 1, d) * _sec(hp_ref, 2, d) * keep_p
        eua[HALO:te, :] = _sec(pj_ref, 1, d) * _sec(pj_ref, 2, d)
        eub[HALO:te, :] = _sec(pj_ref, 3, d) * _sigmoid(_sec(pj_ref, 4, d))
        edz[0:tm, :] = dpa_ref[...].astype(F32) * _sec(pj_ref, 0, d)
        edz[tm:te, :] = dpaf_ref[...].astype(F32) * _sec(hf_ref, 0, d) * keep_f
        dcv, dln, xh = ln_bwd(cv_ref[...].astype(F32), dsw_ref[...].astype(F32), lng_ref[...], lnb_ref[...])
        edcv[0:tm, :] = dcv
        acc_v[0:8, :] += _colsum8(dcv)
        acc_v[8:16, :] += _colsum8(dln * xh)
        acc_v[16:24, :] += _colsum8(dln)
        dcvf, _, _ = ln_bwd(cvf_ref[...].astype(F32), dswf_ref[...].astype(F32), lng_ref[...], lnb_ref[...])
        edcv[tm:te, :] = dcvf * keep_f
        for gi, w in enumerate(POOL_WINDOWS):
            cols = slice(gi * gc, (gi + 1) * gc)
            eq[0:tm, cols] = dpc_ref[:, cols].astype(F32) / _pool_count(i * tm, tm, w)
            eq[tm:te, cols] = dpcf_ref[:, cols].astype(F32) / _pool_count((i + 1) * tm, HALO, w) * keep_f

        def put(sec_idx, r0, val):
            dpj_ref[r0:r0 + CHUNK, sec_idx * d:(sec_idx + 1) * d] = val.astype(dpj_ref.dtype)
            acc_bin[:, sec_idx * d:(sec_idx + 1) * d] += _colsum8(val)

        _fill_shifts(sh, edcv)
        for k0 in range(0, K_B, TAP_GROUP):
            taps = range(k0, min(k0 + TAP_GROUP, K_B))
            a = {k: jnp.zeros((8, d), F32) for k in taps}
            for c in range(tm // CHUNK):
                r0 = c * CHUNK
                ub = eub[HALO + r0:HALO + r0 + CHUNK, :]
                part = None
                for k in taps:
                    t = _shifted(sh, edcv, r0 + (K_B - 1) - k, CHUNK)
                    part = _times_tap(t, taps_b[k]) if part is None else part + _times_tap(t, taps_b[k])
                    a[k] = a[k] + _colsum8(ub * t)
                if k0 == 0:
                    dub_s[r0:r0 + CHUNK, :] = part
                else:
                    dub_s[r0:r0 + CHUNK, :] += part
            for k in taps:
                acc_cb[k] += a[k]
        wa = [jnp.zeros((8, d), F32) for _ in range(K_A)]
        for c in range(tm // CHUNK):
            r0 = c * CHUNK
            rows = slice(r0, r0 + CHUNK)
            z = jnp.zeros((CHUNK, d), F32)
            dua = jnp.zeros((CHUNK, d), F32)
            ua = eua[HALO + r0:HALO + r0 + CHUNK, :]
            for k in range(K_A):
                z = z + _times_tap(eua[HALO + r0 - (K_A - 1) + k:HALO + r0 - (K_A - 1) + k + CHUNK, :], taps_a[k])
                t = edz[r0 + (K_A - 1) - k:r0 + (K_A - 1) - k + CHUNK, :]
                dua = dua + _times_tap(t, taps_a[k])
                wa[k] = wa[k] + _colsum8(ua * t)
            put(0, r0, dpa_ref[rows, :].astype(F32) * z)
            put(1, r0, dua * pj_ref[rows, 2 * d:3 * d].astype(F32))
            put(2, r0, dua * pj_ref[rows, d:2 * d].astype(F32))
            dub = dub_s[rows, :]
            bval = pj_ref[rows, 3 * d:4 * d].astype(F32)
            sg = _sigmoid(pj_ref[rows, 4 * d:5 * d].astype(F32))
            put(3, r0, dub * sg)
            put(4, r0, dub * bval * sg * (1.0 - sg))
        sums = _window_sums(eq, sum_a, sum_b, gc, 0, tm, back=False)
        for gi in range(N_GROUPS):
            cols = slice(gi * gc, (gi + 1) * gc)
            dci = sums[gi] - dpc_ref[:, cols].astype(F32)
            dpj_ref[:, 5 * d + gi * gc:5 * d + (gi + 1) * gc] = dci.astype(dpj_ref.dtype)
            acc_bin[:, 5 * d + gi * gc:5 * d + (gi + 1) * gc] += _colsum8(dci)
        for q in range(3):
            gv = dgt_ref[:, q * d:(q + 1) * d]
            dpj_ref[:, (6 + q) * d:(7 + q) * d] = gv
            acc_bin[:, (6 + q) * d:(7 + q) * d] += _colsum8(gv.astype(F32))
        for k in range(K_A):
            acc_ca[k] += wa[k]

        @pl.when(i == n - 1)
        def _():
            dbin_ref[...] = jnp.sum(acc_bin[...], axis=0, keepdims=True)
            for k in range(K_A):
                dca_ref[k:k + 1, :] = jnp.sum(acc_ca[k], axis=0, keepdims=True)
            for k in range(K_B):
                dcb_ref[k:k + 1, :] = jnp.sum(acc_cb[k], axis=0, keepdims=True)
            dcbb_ref[...] = jnp.sum(acc_v[0:8, :], axis=0, keepdims=True)
            dlng_ref[...] = jnp.sum(acc_v[8:16, :], axis=0, keepdims=True)
            dlnb_ref[...] = jnp.sum(acc_v[16:24, :], axis=0, keepdims=True)

    row = lambda i: (i, 0)
    fixed = lambda i: (0, 0)
    past = lambda i: (jnp.maximum(i * hb - 1, 0), 0)
    fut = lambda i: (jnp.minimum((i + 1) * hb, last_halo), 0)
    vec = jax.ShapeDtypeStruct((1, d), F32)
    tile_and_halo = [pl.BlockSpec((tm, d), row), pl.BlockSpec((HALO, d), fut)]
    return pl.pallas_call(
        body, name=name, grid=(n,),
        in_specs=[pl.BlockSpec((tm, 6 * d), row), pl.BlockSpec((HALO, 6 * d), past), pl.BlockSpec((HALO, 6 * d), fut),
                  *tile_and_halo, *tile_and_halo, *tile_and_halo, *tile_and_halo,
                  pl.BlockSpec((tm, 3 * d), row),
                  pl.BlockSpec((K_A, d), fixed), pl.BlockSpec((K_B, d), fixed), pl.BlockSpec((1, d), fixed),
                  pl.BlockSpec((1, d), fixed)],
        out_specs=[pl.BlockSpec((tm, 9 * d), row), pl.BlockSpec((1, 9 * d), fixed), pl.BlockSpec((K_A, d), fixed),
                   pl.BlockSpec((K_B, d), fixed), pl.BlockSpec((1, d), fixed), pl.BlockSpec((1, d), fixed),
                   pl.BlockSpec((1, d), fixed)],
        out_shape=[jax.ShapeDtypeStruct((s, 9 * d), BF16), jax.ShapeDtypeStruct((1, 9 * d), F32),
                   jax.ShapeDtypeStruct((K_A, d), F32), jax.ShapeDtypeStruct((K_B, d), F32), vec, vec, vec],
        scratch_shapes=[pltpu.VMEM((te, d), F32)] * 5 + [pltpu.VMEM((SUBLANES - 1, te, d), F32),
                                                         pltpu.VMEM((tm, d), F32),
                                                         pltpu.VMEM((K_A, SUBLANES, d), F32),
                                                         pltpu.VMEM((K_B, SUBLANES, d), F32),
                                                         pltpu.VMEM((te, d), F32), pltpu.VMEM((te, d), F32),
                                                         pltpu.VMEM((8, 9 * d), F32), pltpu.VMEM((K_A, 8, d), F32),
                                                         pltpu.VMEM((K_B, 8, d), F32), pltpu.VMEM((24, d), F32)],
        compiler_params=_params("arbitrary"),
    )(proj, proj, proj, cv, cv, dpa, dpa, dsw, dsw, dpc, dpc, dgates, conv_a, conv_b, ln_g, ln_b)


def _pool_wgrad(name, p, dpw, d, tk):
    s = p.shape[0]
    gc = d // N_GROUPS
    n = s // tk

    def body(p_ref, g_ref, o_ref, acc):
        k = pl.program_id(0)
        for gi in range(N_GROUPS):
            cols = slice(gi * gc, (gi + 1) * gc)
            part = lax.dot_general(p_ref[:, cols], g_ref[:, cols], TN, preferred_element_type=F32)

            @pl.when(k == 0)
            def _():
                acc[gi] = part

            @pl.when(k > 0)
            def _():
                acc[gi] += part

        @pl.when(k == n - 1)
        def _():
            for gi in range(N_GROUPS):
                o_ref[:, gi] = acc[gi].astype(o_ref.dtype).reshape(N_DEV, gc // N_DEV, gc)

    return pl.pallas_call(
        body, name=name, grid=(n,),
        in_specs=[pl.BlockSpec((tk, d), lambda k: (k, 0)), pl.BlockSpec((tk, d), lambda k: (k, 0))],
        out_specs=pl.BlockSpec((N_DEV, N_GROUPS, gc // N_DEV, gc), lambda k: (0, 0, 0, 0)),
        out_shape=jax.ShapeDtypeStruct((N_DEV, N_GROUPS, gc // N_DEV, gc), BF16),
        scratch_shapes=[pltpu.VMEM((N_GROUPS, gc, gc), F32)], compiler_params=_params("arbitrary"),
    )(p, dpw)


def _my_place():
    x, y, c = lax.axis_index("x"), lax.axis_index("y"), lax.axis_index("c")
    return x, y, c


def _block_of(x, y, c):
    return 4 * x + 2 * y + c


def _slot(ref, k, paired):
    if not paired:
        return ref.at[k]
    cols = ref.shape[-1] // 2
    return ref.at[k // 2, :, pl.ds(pl.multiple_of((k % 2) * cols, 128), cols)]


def _slot_shape(shape, paired):
    return (N_DEV // 2, shape[0], 2 * shape[1]) if paired else (N_DEV, *shape)


def _gather_shards(shards, paired):
    n_arr = len(shards)

    def body(*refs):
        srcs = refs[:n_arr]
        outs = refs[n_arr:2 * n_arr]
        send_sems, recv_sems, local_sems = refs[2 * n_arr:]
        x, y, c = _my_place()
        me, sibling = (x, y, c), (x, y, 1 - c)
        chips = [(1 - x, y), (x, 1 - y), (1 - x, 1 - y)]

        def copy(n, k, block, to, src=None):
            rows = _slot(outs[n], _block_of(*block), paired[n])
            return pltpu.make_async_remote_copy(
                src_ref=rows if src is None else src, dst_ref=rows, send_sem=send_sems.at[n, k],
                recv_sem=recv_sems.at[n, k], device_id=to, device_id_type=MESH)

        mine = [pltpu.make_async_copy(srcs[n], _slot(outs[n], _block_of(*me), paired[n]), local_sems.at[n])
                for n in range(n_arr)]
        for cp in mine:
            cp.start()
        first = []
        for n in range(n_arr):
            first.append(copy(n, 0, me, sibling, src=srcs[n]))
            first += [copy(n, 1 + j, me, (*chip, c), src=srcs[n]) for j, chip in enumerate(chips)]
        for cp in first:
            cp.start()
        passed = []
        for n in range(n_arr):
            for j, chip in enumerate(chips):
                copy(n, 1 + j, (*chip, c), me).wait_recv()
                fwd = copy(n, 4 + j, (*chip, c), sibling)
                fwd.start()
                passed.append(fwd)
        for n in range(n_arr):
            copy(n, 0, sibling, me).wait_recv()
            for j, chip in enumerate(chips):
                copy(n, 4 + j, (*chip, 1 - c), me).wait_recv()
        for cp in first + passed:
            cp.wait_send()
        for cp in mine:
            cp.wait()

    any_spec = pl.BlockSpec(memory_space=pl.ANY)
    return pl.pallas_call(
        body, name="gather_weights",
        in_specs=[any_spec] * n_arr, out_specs=[any_spec] * n_arr,
        out_shape=[jax.ShapeDtypeStruct(_slot_shape(sh.shape, p), sh.dtype) for sh, p in zip(shards, paired)],
        scratch_shapes=[pltpu.SemaphoreType.DMA((n_arr, 7)), pltpu.SemaphoreType.DMA((n_arr, 7)),
                        pltpu.SemaphoreType.DMA((n_arr,))],
    )(*shards)


def _peers(x, y, c):
    out = []
    for r in range(1, N_DEV):
        fx, fy, fc = (r >> 2) & 1, (r >> 1) & 1, r & 1
        out.append(((1 - x) if fx else x, (1 - y) if fy else y, (1 - c) if fc else c))
    return out


HBM_SPEC = pl.BlockSpec(memory_space=pltpu.HBM)
SEM_SPEC = pl.BlockSpec(memory_space=pltpu.SEMAPHORE)
ANY_SPEC = pl.BlockSpec(memory_space=pl.ANY)
N_PEERS = N_DEV - 1


def _peer_copy(src_ref, land_ref, send_sems, recv_sems, i, r, peer, me, blockwise, paired):
    src = _slot(src_ref, _block_of(*peer), paired) if blockwise else src_ref
    dst = land_ref.at[me] if blockwise else _slot(land_ref, me, paired)
    return pltpu.make_async_remote_copy(
        src_ref=src, dst_ref=dst, send_sem=send_sems.at[i * N_PEERS + r],
        recv_sem=recv_sems.at[i * N_PEERS + r], device_id=peer, device_id_type=MESH)


def _block_shape(shape, paired):
    return (shape[1], shape[2] // 2) if paired else tuple(shape[1:])


def _start_copies(name, srcs, after, blockwise, paired=None):
    n = len(srcs)
    paired = paired or [False] * n

    def body(*refs):
        s_in, l_in = refs[:n], refs[n:2 * n]
        send_sems, recv_sems = refs[2 * n + 1], refs[2 * n + 2]
        token = refs[-1]
        x, y, c = _my_place()
        me = _block_of(x, y, c)
        for i in range(n):
            for r, peer in enumerate(_peers(x, y, c)):
                _peer_copy(s_in[i], l_in[i], send_sems, recv_sems, i, r, peer, me, blockwise, paired[i]).start()
        token[...] = jnp.zeros_like(token)

    land_shapes = [(N_DEV, *_block_shape(s.shape, p)) if blockwise else _slot_shape(s.shape, p)
                   for s, p in zip(srcs, paired)]
    lands = [pltpu.with_memory_space_constraint(lax.empty(sh, s.dtype), pltpu.HBM) for sh, s in zip(land_shapes, srcs)]
    ins = [pltpu.with_memory_space_constraint(s, pltpu.HBM) for s in srcs]
    out = pl.pallas_call(
        body, name=name,
        out_shape=(pltpu.SemaphoreType.DMA((n * N_PEERS,)), pltpu.SemaphoreType.DMA((n * N_PEERS,)),
                   *[pltpu.HBM(s.shape, s.dtype) for s in srcs],
                   *[pltpu.HBM(sh, s.dtype) for sh, s in zip(land_shapes, srcs)],
                   jax.ShapeDtypeStruct((8, 128), F32)),
        in_specs=[HBM_SPEC] * (2 * n) + [ANY_SPEC],
        out_specs=(SEM_SPEC, SEM_SPEC, *[HBM_SPEC] * (2 * n), pl.BlockSpec(memory_space=pltpu.VMEM)),
        input_output_aliases={i: 2 + i for i in range(2 * n)},
        compiler_params=pltpu.CompilerParams(has_side_effects=pltpu.SideEffectType.DATAFLOW_SIDE_EFFECTING),
    )(*ins, *lands, after)
    return dict(send=out[0], recv=out[1], srcs=list(out[2:2 + n]), lands=list(out[2 + n:2 + 2 * n]), token=out[-1],
                paired=paired)


def _wait_copies(name, state, after, blockwise):
    n = len(state["srcs"])
    paired = state["paired"]

    def body(*refs):
        s_in, l_in = refs[:n], refs[n:2 * n]
        send_sems, recv_sems = refs[2 * n], refs[2 * n + 1]
        x, y, c = _my_place()
        me = _block_of(x, y, c)
        for i in range(n):
            for r, peer in enumerate(_peers(x, y, c)):
                cp = _peer_copy(s_in[i], l_in[i], send_sems, recv_sems, i, r, peer, me, blockwise, paired[i])
                cp.wait_send()
                cp.wait_recv()

    both = state["srcs"] + state["lands"]
    out = pl.pallas_call(
        body, name=name, out_shape=tuple(pltpu.HBM(a.shape, a.dtype) for a in both),
        in_specs=[HBM_SPEC] * (2 * n) + [SEM_SPEC, SEM_SPEC, ANY_SPEC], out_specs=tuple([HBM_SPEC] * (2 * n)),
        input_output_aliases={i: i for i in range(2 * n)},
        compiler_params=pltpu.CompilerParams(has_side_effects=pltpu.SideEffectType.DATAFLOW_SIDE_EFFECTING),
    )(*both, state["send"], state["recv"], after)
    return list(out[:n]), list(out[n:])


COPY_BLOCK_BYTES = 2 * 1024 * 1024


def _place_own(name, lands, srcs, me, blockwise, paired=None):
    out = []
    paired = paired or [False] * len(lands)
    for i, (land, src) in enumerate(zip(lands, srcs)):
        in_slots = src if blockwise else land
        part = _block_shape(in_slots.shape, paired[i])
        row_bytes = land.dtype.itemsize
        for extent in part[1:]:
            row_bytes *= extent
        tr = part[0]
        while tr * row_bytes > COPY_BLOCK_BYTES and tr % 16 == 0:
            tr //= 2
        tail = (0,) * (len(part) - 1)

        def body(me_ref, s_ref, l_ref, o_ref):
            o_ref[...] = s_ref[...]

        if paired[i]:
            slot_spec = pl.BlockSpec((None, tr, part[1]), lambda j, me_ref: (me_ref[0] // 2, j, me_ref[0] % 2))
        else:
            slot_spec = pl.BlockSpec((None, tr, *part[1:]), lambda j, me_ref: (me_ref[0], j, *tail))
        if blockwise:
            s_spec = slot_spec
            o_spec = pl.BlockSpec((None, tr, *part[1:]), lambda j, me_ref: (me_ref[0], j, *tail))
        else:
            s_spec = pl.BlockSpec((tr, *part[1:]), lambda j, me_ref: (j, *tail))
            o_spec = slot_spec
        out.append(pl.pallas_call(
            body, name=f"{name}_{i}",
            grid_spec=pltpu.PrefetchScalarGridSpec(
                num_scalar_prefetch=1, grid=(part[0] // tr,), in_specs=[s_spec, ANY_SPEC], out_specs=o_spec),
            out_shape=jax.ShapeDtypeStruct(land.shape, land.dtype), input_output_aliases={2: 0},
            compiler_params=_params("parallel"),
        )(me, src, land))
    return out


def _adamw_math(w, g, m, v):
    m = ADAM_B1 * m + (1.0 - ADAM_B1) * g
    v = ADAM_B2 * v + (1.0 - ADAM_B2) * (g * g)
    m_hat = m / (1.0 - ADAM_B1 ** ADAM_STEP)
    v_hat = v / (1.0 - ADAM_B2 ** ADAM_STEP)
    delta = -ADAM_LR * (m_hat / (jnp.sqrt(v_hat) + ADAM_EPS) + ADAM_WD * w)
    return delta, m, v


def _adamw(name, parts, w, m, v, *, grid, part_specs, w_spec):
    n_layers = len(parts)
    n_parts = parts[0].shape[0]

    def body(*refs):
        p_refs = refs[:n_layers]
        w_ref, m_ref, v_ref, g_ref, d_ref, nm_ref, nv_ref = refs[n_layers:]

        def total(p_ref):
            t = p_ref[0].astype(F32)
            for k in range(1, n_parts):
                t = t + p_ref[k].astype(F32)
            return t

        g = total(p_refs[0])
        for li in range(1, n_layers):
            g = jnp.where(pl.program_id(0) == li, total(p_refs[li]), g)
        delta, nm, nv = _adamw_math(w_ref[...], g, m_ref[...], v_ref[...])
        g_ref[...] = g
        d_ref[...] = delta
        nm_ref[...] = nm
        nv_ref[...] = nv

    out = jax.ShapeDtypeStruct(w.shape, F32)
    return pl.pallas_call(
        body, name=name, grid=grid, in_specs=[*part_specs, w_spec, w_spec, w_spec], out_specs=[w_spec] * 4,
        out_shape=[out] * 4, compiler_params=_params(*(("parallel",) * len(grid))),
    )(*parts, w, m, v)


def _layer_part_spec(layer, block, n_blocks, row_off=0):
    def index_map(l, i):
        ii = jnp.where(l == layer, i, jnp.where(l < layer, 0, n_blocks - 1))
        return (0, row_off + ii) + (0,) * (len(block) - 2)
    return pl.BlockSpec(block, index_map)


def _small_update(partials, triples):
    d = partials[-1].shape[-1]
    n_rep = len(triples)
    n_part = len(partials)
    rows = []
    for p in partials:
        rows.append(p.shape[0] * (p.shape[1] // d))
    offs = [sum(rows[:i]) for i in range(n_part)]
    total = -(-sum(rows) // 8) * 8

    def body(*refs):
        p_refs = refs[:n_part]
        wmv = refs[n_part:n_part + 3 * n_rep]
        outs = refs[n_part + 3 * n_rep:n_part + 3 * n_rep + 4 * n_rep + (n_part - n_rep)]
        buf, send_sems, recv_sems = refs[-3:]
        x, y, c = _my_place()
        me = _block_of(x, y, c)
        peers = _peers(x, y, c)
        mine = buf.at[me]
        if total > sum(rows):
            mine[sum(rows):total, :] = jnp.zeros((total - sum(rows), d), F32)
        for p_ref, off in zip(p_refs, offs):
            nr, nc = p_ref.shape[0], p_ref.shape[1] // d
            if nc == 1:
                mine[off:off + nr, :] = p_ref[...]
            else:
                for r in range(nr):
                    for q in range(nc):
                        mine[off + r * nc + q:off + r * nc + q + 1, :] = p_ref[r:r + 1, q * d:(q + 1) * d]
        sends =[pltpu.make_async_remote_copy(
            src_ref=buf.at[me], dst_ref=buf.at[me], send_sem=send_sems.at[r], recv_sem=recv_sems.at[r],
            device_id=peer, device_id_type=MESH) for r, peer in enumerate(peers)]
        for cp in sends:
            cp.start()
        for r, peer in enumerate(peers):
            pltpu.make_async_remote_copy(
                src_ref=buf.at[me], dst_ref=buf.at[_block_of(*peer)], send_sem=send_sems.at[r],
                recv_sem=recv_sems.at[r], device_id=peer, device_id_type=MESH).wait_recv()
        for cp in sends:
            cp.wait_send()
        tot = buf[0]
        for k in range(1, N_DEV):
            tot = tot + buf[k]
        buf[0] = tot
        for idx in range(n_part):
            nr, nc = p_refs[idx].shape[0], p_refs[idx].shape[1] // d
            if idx < n_rep:
                w_ref, m_ref, v_ref = wmv[3 * idx:3 * idx + 3]
                g_ref, d_ref, nm_ref, nv_ref = outs[4 * idx:4 * idx + 4]
            else:
                g_ref = outs[4 * n_rep + idx - n_rep]
            pieces = [(slice(0, nr), slice(0, d), offs[idx], nr)] if nc == 1 else [
                (slice(r, r + 1), slice(q * d, (q + 1) * d), offs[idx] + r * nc + q, 1)
                for r in range(nr) for q in range(nc)]
            for rws, cols, row, cnt in pieces:
                g = buf[0, row:row + cnt, :]
                g_ref[rws, cols] = g
                if idx < n_rep:
                    delta, nm, nv = _adamw_math(w_ref[rws, cols], g, m_ref[rws, cols], v_ref[rws, cols])
                    d_ref[rws, cols] = delta
                    nm_ref[rws, cols] = nm
                    nv_ref[rws, cols] = nv

    vm = pl.BlockSpec(memory_space=pltpu.VMEM)
    operands = list(partials)
    for t in triples:
        operands += list(t)
    out_shape = []
    for idx in range(n_rep):
        out_shape += [jax.ShapeDtypeStruct(partials[idx].shape, F32)] * 4
    for idx in range(n_rep, n_part):
        out_shape.append(jax.ShapeDtypeStruct(partials[idx].shape, F32))
    return pl.pallas_call(
        body, name="small_allreduce_adamw", in_specs=[vm] * len(operands), out_specs=[vm] * len(out_shape),
        out_shape=out_shape,
        scratch_shapes=[pltpu.VMEM((N_DEV, total, d), F32), pltpu.SemaphoreType.DMA((7,)), pltpu.SemaphoreType.DMA((7,))],
        compiler_params=pltpu.CompilerParams(vmem_limit_bytes=VMEM_LIMIT_BYTES),
    )(*operands)


def kernel(x, g_mix, w_in, b_in, conv_a, w_out_a, conv_b, conv_b_bias, ln_b_g, ln_b_b, w_out_b, b_out_b, w_pool, pool_scale, w_o, g_mlp, w_mlp1, w_mlp2, g_final, loss_target, m_g_mix, m_w_in, m_b_in, m_conv_a, m_w_out_a, m_conv_b, m_conv_b_bias, m_ln_b_g, m_ln_b_b, m_w_out_b, m_b_out_b, m_w_pool, m_pool_scale, m_w_o, m_g_mlp, m_w_mlp1, m_w_mlp2, m_g_final, v_g_mix, v_w_in, v_b_in, v_conv_a, v_w_out_a, v_conv_b, v_conv_b_bias, v_ln_b_g, v_ln_b_b, v_w_out_b, v_b_out_b, v_w_pool, v_pool_scale, v_w_o, v_g_mlp, v_w_mlp1, v_w_mlp2, v_g_final):
    _, s, d = x.shape
    n_layers = g_mix.shape[0]
    p_in = b_in.shape[1]
    ci = w_in.shape[2]
    c1 = w_mlp1.shape[2]
    rf = w_mlp2.shape[1]
    rd = w_out_a.shape[1]
    f = rf * N_DEV
    rp = rf + 3 * rd
    o_a, o_b, o_o = rf // rd, rf // rd + 1, rf // rd + 2
    gc = d // N_GROUPS
    ca_rows = 8
    tm = min(1024, s)
    tr = min(512, s)
    tx = min(256, s)
    tk = min(2048, s)
    tk_mlp = min(4096, s)
    tk_in = min(2048, s)

    me_arr = jnp.reshape(_block_of(*_my_place()), (1,)).astype(jnp.int32)

    def layer_shards(l):
        row_pack = jnp.concatenate([w_mlp2[l], w_out_a[l], w_out_b[l], w_o[l]], axis=0).astype(BF16)
        return [w_in[l].astype(BF16), w_mlp1[l].astype(BF16), row_pack, w_pool[l].astype(BF16)]

    conv_pack = jnp.concatenate(
        [conv_a, jnp.zeros((n_layers, ca_rows - K_A, rd), F32), conv_b], axis=1)
    first_shards = layer_shards(0)
    layer_pairing = [True, False, False, False]
    g_in_first, g_conv = _gather_shards([first_shards[0], conv_pack], [True, False])
    conv_full = jnp.transpose(g_conv, (1, 2, 0, 3)).reshape(n_layers, ca_rows + K_B, d)
    conv_a_f = conv_full[:, :K_A]
    conv_b_f = conv_full[:, ca_rows:]
    first_row_going = _start_copies("gather_start_row_0", first_shards[2:], g_conv, blockwise=False)
    in_flight = [_start_copies("gather_start_mlp1_0", first_shards[1:2], first_row_going["token"], blockwise=False)]
    for l in range(1, n_layers):
        in_flight.append(_start_copies(f"gather_start_{l}", layer_shards(l), in_flight[-1]["token"], blockwise=False,
                                       paired=layer_pairing))
    token = in_flight[-1]["token"][0:1, 0:1]

    xs = [x[0]]
    saved = []
    weights = []
    row2 = lambda j, i: (i, 0)
    for l in range(n_layers):
        x0 = xs[-1]
        vec = lambda a: a[l:l + 1]
        if l > 0:
            srcs, lands = _wait_copies(f"gather_wait_{l}", in_flight[l], x0, blockwise=False)
            g_in, g_1, g_row, g_pool = _place_own(f"gather_own_{l}", lands, srcs, me_arr, blockwise=False,
                                                  paired=layer_pairing)
        else:
            g_in = g_in_first
        h = _rms_fwd(f"rms_mix_{l}", x0, vec(g_mix) + token if l == 0 else vec(g_mix), tr)
        proj = _mm(
            f"proj_{l}", h, g_in, grid=(N_DEV // 2, s // tm), a_spec=pl.BlockSpec((tm, d), row2),
            b_spec=pl.BlockSpec((None, d, 2 * ci), lambda j, i: (j, 0, 0)),
            extras=(vec(b_in),), extra_specs=(pl.BlockSpec((1, 2 * ci), lambda j, i: (0, j)),),
            epilogue=lambda v, b: v + b, out_shape=jax.ShapeDtypeStruct((s, p_in), BF16),
            o_spec=pl.BlockSpec((tm, 2 * ci), lambda j, i: (i, j)), dims=NN)
        if l == 0:
            srcs, lands = _wait_copies("gather_wait_row_0", first_row_going, proj, blockwise=False)
            g_row, g_pool = _place_own("gather_own_row_0", lands, srcs, me_arr, blockwise=False)
        p_a, sw, p_c, cv, y_a, y_b, pw, merged, x1, h2 = _mixer_fwd(
            f"mix_fwd_{l}", proj, x0, conv_a_f[l], conv_b_f[l], vec(conv_b_bias), vec(ln_b_g), vec(ln_b_b),
            vec(b_out_b), vec(pool_scale), vec(g_mlp), g_row, g_pool, (o_a, o_b, o_o), d, tx)
        if l == 0:
            srcs, lands = _wait_copies("gather_wait_mlp1_0", in_flight[0], x1, blockwise=False)
            g_1, = _place_own("gather_own_mlp1_0", lands, srcs, me_arr, blockwise=False)
        weights.append((g_in, g_1, g_row, g_pool))
        a_pre = _mm(f"mlp1_{l}", h2, g_1, grid=(N_DEV // 2, s // tm), a_spec=pl.BlockSpec((tm, d), row2),
                    b_spec=pl.BlockSpec((2, d, c1), lambda j, i: (j, 0, 0)), slabs="n",
                    out_shape=jax.ShapeDtypeStruct((s, f), BF16),
                    o_spec=pl.BlockSpec((tm, 2 * c1), lambda j, i: (i, j)), dims=NN)
        x2 = _mm(f"mlp2_{l}", a_pre, g_row, grid=(1, s // tr), a_spec=pl.BlockSpec((tr, f), row2),
                 b_spec=pl.BlockSpec((N_DEV, rf, d), lambda j, i: (0, 0, 0)), prologue=_relu_sq,
                 extras=(x1,), extra_specs=(pl.BlockSpec((tr, d), row2),), epilogue=lambda v, r: v + r,
                 out_shape=jax.ShapeDtypeStruct((s, d), F32), o_spec=pl.BlockSpec((tr, d), row2), dims=NN)
        saved.append((x0, h, proj, p_a, sw, p_c, cv, y_a, y_b, pw, merged, x1, h2, a_pre))
        xs.append(x2)

    loss_part, dx, dx16, dg_final = _loss_head(xs[-1], g_final.reshape(1, d), loss_target[0], tr)
    loss = lax.psum(loss_part[0, 0], ("x", "y", "c"))

    small = [None] * n_layers
    exchanges = [None] * n_layers
    for l in reversed(range(n_layers)):
        x0, h, proj, p_a, sw, p_c, cv, y_a, y_b, pw, merged, x1, h2, a_pre = saved[l]
        g_in, g_1, g_row, g_pool = weights[l]
        vec = lambda a: a[l:l + 1]
        row_shape = jax.ShapeDtypeStruct((N_DEV, rp, d), BF16)

        def dd_grad(name, a, g, off, alias):
            return _mm(name, a, g, grid=(1, s // tk), a_spec=pl.BlockSpec((tk, d), lambda j, k: (k, 0)),
                       b_spec=pl.BlockSpec((tk, d), lambda j, k: (k, 0)), out_shape=row_shape,
                       o_spec=pl.BlockSpec((N_DEV, rd, d), lambda j, k: (0, off, 0)), dims=TN, nk=s // tk,
                       acc_shape=(d, d), alias_in=alias)

        d_a = _mm(f"d_act_{l}", dx16, g_row, grid=(N_DEV // 2, s // tm), a_spec=pl.BlockSpec((tm, d), row2),
                  b_spec=pl.BlockSpec((2, rf, d), lambda j, i: (j, 0, 0)), slabs="n",
                  extras=(a_pre,), extra_specs=(pl.BlockSpec((tm, 2 * rf), lambda j, i: (i, j)),),
                  epilogue=lambda v, a: v * (2.0 * jnp.maximum(a.astype(F32), 0.0)),
                  out_shape=jax.ShapeDtypeStruct((s, f), BF16),
                  o_spec=pl.BlockSpec((tm, 2 * rf), lambda j, i: (i, j)), dims=NT)
        dg_row = _mm(f"dw_mlp2_{l}", a_pre, dx16, grid=(N_DEV, s // tk_mlp),
                     a_spec=pl.BlockSpec((tk_mlp, rf), lambda j, k: (k, j)),
                     b_spec=pl.BlockSpec((tk_mlp, d), lambda j, k: (k, 0)), prologue=_relu_sq, out_shape=row_shape,
                     o_spec=pl.BlockSpec((None, rf, d), lambda j, k: (j, 0, 0)), dims=TN, nk=s // tk_mlp,
                     acc_shape=(rf, d))
        dg_1 = _mm(f"dw_mlp1_{l}", h2, d_a, grid=(N_DEV, s // tk_mlp),
                   a_spec=pl.BlockSpec((tk_mlp, d), lambda j, k: (k, 0)),
                   b_spec=pl.BlockSpec((tk_mlp, c1), lambda j, k: (k, j)),
                   out_shape=jax.ShapeDtypeStruct((N_DEV, d, c1), BF16),
                   o_spec=pl.BlockSpec((None, d, c1), lambda j, k: (j, 0, 0)), dims=TN, nk=s // tk_mlp,
                   acc_shape=(d, c1))
        mlp1_going = _start_copies(f"grads_start_mlp1_{l}", [dg_1], vec(g_mlp), blockwise=True)
        stream = [jax.ShapeDtypeStruct((s, d), F32), jax.ShapeDtypeStruct((s, d), BF16), jax.ShapeDtypeStruct((1, d), F32)]
        dx, dx16, dg_mlp = _mm(
            f"d_h2_{l}", d_a, g_1, grid=(1, s // tr), a_spec=pl.BlockSpec((tr, f), row2),
            b_spec=pl.BlockSpec((N_DEV, d, c1), lambda j, i: (0, 0, 0), pipeline_mode=pl.Buffered(1)), slabs="k",
            extras=(x1, vec(g_mlp), dx),
            extra_specs=(pl.BlockSpec((tr, d), row2), pl.BlockSpec((1, d), lambda j, i: (0, 0)),
                         pl.BlockSpec((tr, d), row2)),
            out_shape=stream, o_spec=[pl.BlockSpec((tr, d), row2), pl.BlockSpec((tr, d), row2),
                                      pl.BlockSpec((1, d), lambda j, i: (0, 0))],
            dims=NT, rms_bwd=(1, s // tr), after=mlp1_going["token"])
        d_ya, d_yb, d_pw, d_gates, d_pa, d_sw, d_pc, d_bout, d_pscale = _merge_bwd(
            f"merge_bwd_{l}", dx16, proj, y_a, y_b, pw, vec(pool_scale), g_row, g_pool, (o_a, o_b, o_o), d, tr)
        dg_row = dd_grad(f"dw_o_{l}", merged, dx16, o_o, dg_row)
        dg_row = dd_grad(f"dw_out_a_{l}", p_a, d_ya, o_a, dg_row)
        dg_row = dd_grad(f"dw_out_b_{l}", sw, d_yb, o_b, dg_row)
        dg_pool = _pool_wgrad(f"dw_pool_{l}", p_c, d_pw, d, tk)
        rest_going = _start_copies(f"grads_start_rest_{l}", [dg_row, dg_pool], vec(g_mlp), blockwise=True)
        d_proj, d_bin, d_ca, d_cb, d_cbb, d_lng, d_lnb = _mix_pre_bwd(
            f"mix_bwd_{l}", proj, cv, d_pa, d_sw, d_pc, d_gates, conv_a_f[l], conv_b_f[l],
            vec(ln_b_g) + rest_going["token"][0:1, 0:1], vec(ln_b_b), d, tx)
        dg_in = _mm(f"dw_in_{l}", h, d_proj, grid=(N_DEV // 2, s // tk_in),
                    a_spec=pl.BlockSpec((tk_in, d), lambda j, k: (k, 0)),
                    b_spec=pl.BlockSpec((tk_in, 2 * ci), lambda j, k: (k, j)),
                    out_shape=jax.ShapeDtypeStruct((N_DEV // 2, d, 2 * ci), BF16),
                    o_spec=pl.BlockSpec((None, d, 2 * ci), lambda j, k: (j, 0, 0)), dims=TN, nk=s // tk_in,
                    acc_shape=(d, 2 * ci), after=rest_going["token"])
        in_going = _start_copies(f"grads_start_in_{l}", [dg_in], vec(g_mix), blockwise=True, paired=[True])
        d_h = _mm(f"d_h_{l}", d_proj, g_in, grid=(s // tm, N_DEV // 2),
                  a_spec=pl.BlockSpec((tm, 2 * ci), lambda i, k: (i, k)),
                  b_spec=pl.BlockSpec((None, d, 2 * ci), lambda i, k: (k, 0, 0)),
                  out_shape=jax.ShapeDtypeStruct((s, d), BF16), o_spec=pl.BlockSpec((tm, d), lambda i, k: (i, 0)),
                  dims=NT, nk=N_DEV // 2, acc_shape=(tm, d), after=in_going["token"])
        dx, dx16, dg_mix = _rms_bwd(f"rms_mix_bwd_{l}", d_h, x0, vec(g_mix), dx, tr)
        small[l] = (dg_mix, d_bin, d_cbb, d_lng, d_lnb, d_bout, d_pscale, dg_mlp, d_ca, d_cb)
        exchanges[l] = (in_going, mlp1_going, rest_going)

    grad_x = dx[None]

    names = ("g_mix", "b_in", "conv_b_bias", "ln_b_g", "ln_b_b", "b_out_b", "pool_scale", "g_mlp")
    given = dict(g_mix=(g_mix, m_g_mix, v_g_mix), b_in=(b_in, m_b_in, v_b_in),
                 conv_b_bias=(conv_b_bias, m_conv_b_bias, v_conv_b_bias), ln_b_g=(ln_b_g, m_ln_b_g, v_ln_b_g),
                 ln_b_b=(ln_b_b, m_ln_b_b, v_ln_b_b), b_out_b=(b_out_b, m_b_out_b, v_b_out_b),
                 pool_scale=(pool_scale, m_pool_scale, v_pool_scale), g_mlp=(g_mlp, m_g_mlp, v_g_mlp))
    partials, triples = [], []
    for i, nm in enumerate(names):
        partials.append(jnp.concatenate([small[l][i] for l in range(n_layers)], axis=0))
        triples.append(given[nm])
    partials.append(dg_final)
    triples.append(tuple(a.reshape(1, d) for a in (g_final, m_g_final, v_g_final)))
    partials.append(jnp.concatenate([small[l][8] for l in range(n_layers)], axis=0))
    partials.append(jnp.concatenate([small[l][9] for l in range(n_layers)], axis=0))
    outs = _small_update(partials, triples)
    rep = {nm: outs[4 * i:4 * i + 4] for i, nm in enumerate(names)}
    rep["g_final"] = [a.reshape(d) for a in outs[4 * len(names):4 * len(names) + 4]]
    me = _block_of(*_my_place())
    gca = lax.dynamic_slice_in_dim(outs[-2].reshape(n_layers, K_A, d), me * rd, rd, axis=2)
    gcb = lax.dynamic_slice_in_dim(outs[-1].reshape(n_layers, K_B, d), me * rd, rd, axis=2)

    r_in, r_1, r_row, r_pool = [], [], [], []
    for l in reversed(range(n_layers)):
        in_going, mlp1_going, rest_going = exchanges[l]
        srcs_m, lands_m = _wait_copies(f"grads_wait_mlp1_{l}", mlp1_going, outs[0], blockwise=True)
        srcs_r, lands_r = _wait_copies(f"grads_wait_rest_{l}", rest_going, outs[0], blockwise=True)
        srcs_i, lands_i = _wait_copies(f"grads_wait_in_{l}", in_going, outs[0], blockwise=True)
        got = _place_own(f"grads_own_{l}", lands_i + lands_m + lands_r, srcs_i + srcs_m + srcs_r, me_arr, blockwise=True,
                         paired=layer_pairing)
        for lst, arr in zip((r_in, r_1, r_row, r_pool), got):
            lst.insert(0, arr)
    tb = min(256, d)
    layers = range(n_layers)
    res = {}
    res["w_in"] = _adamw("adamw_w_in", r_in, w_in, m_w_in, v_w_in, grid=(n_layers, d // tb),
                         part_specs=[_layer_part_spec(li, (N_DEV, tb, ci), d // tb) for li in layers],
                         w_spec=pl.BlockSpec((None, tb, ci), lambda l, i: (l, i, 0)))
    res["w_mlp1"] = _adamw("adamw_w_mlp1", r_1, w_mlp1, m_w_mlp1, v_w_mlp1, grid=(n_layers, d // tb),
                           part_specs=[_layer_part_spec(li, (N_DEV, tb, c1), d // tb) for li in layers],
                           w_spec=pl.BlockSpec((None, tb, c1), lambda l, i: (l, i, 0)))
    tf = min(256, rf)
    res["w_mlp2"] = _adamw("adamw_w_mlp2", r_row, w_mlp2, m_w_mlp2, v_w_mlp2, grid=(n_layers, rf // tf),
                           part_specs=[_layer_part_spec(li, (N_DEV, tf, d), rf // tf) for li in layers],
                           w_spec=pl.BlockSpec((None, tf, d), lambda l, i: (l, i, 0)))
    for nm, off, trip in (("w_out_a", o_a, (w_out_a, m_w_out_a, v_w_out_a)),
                          ("w_out_b", o_b, (w_out_b, m_w_out_b, v_w_out_b)), ("w_o", o_o, (w_o, m_w_o, v_w_o))):
        res[nm] = _adamw(f"adamw_{nm}", r_row, *trip, grid=(n_layers, 1),
                         part_specs=[_layer_part_spec(li, (N_DEV, rd, d), 1, row_off=off) for li in layers],
                         w_spec=pl.BlockSpec((None, rd, d), lambda l, i: (l, 0, 0)))
    res["w_pool"] = _adamw("adamw_w_pool", r_pool, w_pool, m_w_pool, v_w_pool, grid=(n_layers, 1),
                           part_specs=[_layer_part_spec(li, (N_DEV, N_GROUPS, gc // N_DEV, gc), 1) for li in layers],
                           w_spec=pl.BlockSpec((None, N_GROUPS, gc // N_DEV, gc), lambda l, i: (l, 0, 0, 0)))
    whole3 = lambda: (0, 0, 0)
    res["conv_a"] = _adamw("adamw_conv_a", [gca[None]], conv_a, m_conv_a, v_conv_a, grid=(),
                           part_specs=[pl.BlockSpec((1, n_layers, K_A, rd), lambda: (0, 0, 0, 0))],
                           w_spec=pl.BlockSpec((n_layers, K_A, rd), whole3))
    res["conv_b"] = _adamw("adamw_conv_b", [gcb[None]], conv_b, m_conv_b, v_conv_b, grid=(),
                           part_specs=[pl.BlockSpec((1, n_layers, K_B, rd), lambda: (0, 0, 0, 0))],
                           w_spec=pl.BlockSpec((n_layers, K_B, rd), whole3))
    res.update(rep)

    order = ("g_mix", "w_in", "b_in", "conv_a", "w_out_a", "conv_b", "conv_b_bias", "ln_b_g", "ln_b_b", "w_out_b",
             "b_out_b", "w_pool", "pool_scale", "w_o", "g_mlp", "w_mlp1", "w_mlp2", "g_final")
    out = [loss, grad_x]
    for kind in range(4):
        out += [res[nm][kind] for nm in order]
    return tuple(out)
```
